```python
import jax, jax.numpy as jnp
from jax import lax
import numpy as np

D_MODEL = 1024
BATCH = 2
SEQ = 8192
DEPTH = 2

EXPAND = 2
D_INNER = EXPAND * D_MODEL
N_MIXERS = 4
D_BRANCH = D_INNER // N_MIXERS
EPS = 1e-6
NEG_BIG = -1e30

GLA_HEADS = 4
GLA_DV = D_BRANCH // GLA_HEADS
GLA_DK = GLA_DV // 2
GLA_GATE_RANK = 16
GLA_GATE_NORM = 16.0
GLA_CHUNK = 64

MLSTM_HEADS = 4
MLSTM_DH = D_BRANCH // MLSTM_HEADS
MLSTM_CONV = 4
MLSTM_CHUNK = 64

HGRN_HEADS = 4
HGRN_DK = 128
HGRN_DV = D_BRANCH // HGRN_HEADS
HGRN_CHUNK = 64

SSD_HEAD_DIM = 64
SSD_HEADS = D_BRANCH // SSD_HEAD_DIM
SSD_GROUPS = 2
SSD_STATE = 128
SSD_CONV = 4
SSD_CHUNK = 128

GLA_QK = GLA_HEADS * GLA_DK
HGRN_QF = HGRN_HEADS * HGRN_DK
SSD_BC = SSD_GROUPS * SSD_STATE
PROJ_SIZES = (
    GLA_QK, GLA_QK, D_BRANCH, GLA_GATE_RANK, D_BRANCH,
    D_BRANCH, D_BRANCH, D_BRANCH, MLSTM_HEADS, MLSTM_HEADS, D_BRANCH, D_BRANCH,
    HGRN_QF, HGRN_QF, D_BRANCH, D_BRANCH,
    D_BRANCH, SSD_BC, SSD_BC, SSD_HEADS, D_BRANCH,
)
D_PROJ = sum(PROJ_SIZES)

kernel_name = "hymba_gla_mlstm_hgrn2_ssd_hybrid"


def rmsnorm(x, w):
    xf = x.astype(jnp.float32)
    y = xf * lax.rsqrt(jnp.mean(xf * xf, axis=-1, keepdims=True) + EPS)
    return (y * w.astype(jnp.float32)).astype(x.dtype)


def grouped_rmsnorm(x, w, n_groups):
    shp = x.shape
    xg = x.reshape(shp[:-1] + (n_groups, shp[-1] // n_groups))
    return rmsnorm(xg, w.reshape(n_groups, -1)).reshape(shp)


def causal_dwconv(x, w, b):
    K = w.shape[0]
    y = lax.conv_general_dilated(x, w[:, None, :].astype(x.dtype), window_strides=(1,),
                                 padding=[(K - 1, 0)], dimension_numbers=('NWC', 'WIO', 'NWC'),
                                 feature_group_count=x.shape[-1])
    return y + b.astype(x.dtype)


def masked_exp(logw, mask):
    return jnp.where(mask, jnp.exp(jnp.where(mask, logw, 0.0)), 0.0)


def to_chunks(t, c):
    Bsz, S = t.shape[:2]
    t = t.reshape((Bsz, S // c, c) + t.shape[2:])
    if t.ndim == 5:
        return t.transpose(1, 0, 3, 2, 4)
    return t.transpose(1, 0, 3, 2)


def from_chunks(o):
    NC, Bsz, H, c, d = o.shape
    return o.transpose(1, 0, 3, 2, 4).reshape(Bsz, NC * c, H * d)


def chunk_gated_linear_attention(q, k, v, log_g, chunk):
    Bsz, S, H, dk = q.shape
    dv = v.shape[-1]
    causal = jnp.tril(jnp.ones((chunk, chunk), bool))[:, :, None]

    def step(state, inp):
        qb, kb, vb, gb = inp
        G = jnp.cumsum(gb, axis=2)
        diff = G[:, :, :, None, :] - G[:, :, None, :, :]
        decay = masked_exp(diff, causal)
        scores = jnp.einsum('bhid,bhjd,bhijd->bhij', qb, kb, decay)
        out = (jnp.einsum('bhij,bhje->bhie', scores, vb)
               + jnp.einsum('bhid,bhde->bhie', qb * jnp.exp(G), state))
        G_last = G[:, :, -1]
        k_dec = kb * jnp.exp(G_last[:, :, None, :] - G)
        state = jnp.exp(G_last)[..., None] * state + jnp.einsum('bhcd,bhce->bhde', k_dec, vb)
        return state, out

    state0 = jnp.zeros((Bsz, H, dk, dv), q.dtype)
    _, out = lax.scan(step, state0, (to_chunks(q, chunk), to_chunks(k, chunk),
                                     to_chunks(v, chunk), to_chunks(log_g, chunk)))
    return from_chunks(out)


def chunk_mlstm(q, k, v, i_pre, log_f, chunk):
    Bsz, S, H, dk = q.shape
    dv = v.shape[-1]
    causal = jnp.tril(jnp.ones((chunk, chunk), bool))

    def step(carry, inp):
        C, n, m = carry
        qb, kb, vb, ib, fb = inp
        b = jnp.cumsum(fb, axis=-1)
        logw = jnp.where(causal, b[..., :, None] - b[..., None, :] + ib[..., None, :], NEG_BIG)
        m_inter = b + m[..., None]
        m_row = jnp.maximum(jnp.max(logw, axis=-1), m_inter)
        s = jnp.einsum('bhid,bhjd->bhij', qb, kb) * masked_exp(logw - m_row[..., None], causal)
        inter = jnp.exp(m_inter - m_row)
        num = (jnp.einsum('bhij,bhje->bhie', s, vb)
               + inter[..., None] * jnp.einsum('bhid,bhde->bhie', qb, C))
        den = jnp.sum(s, axis=-1) + inter * jnp.einsum('bhid,bhd->bhi', qb, n)
        h = num / jnp.maximum(jnp.abs(den), jnp.exp(-m_row))[..., None]
        b_last = b[..., -1]
        logw_end = b_last[..., None] - b + ib
        m_new = jnp.maximum(b_last + m, jnp.max(logw_end, axis=-1))
        carry_decay = jnp.exp(b_last + m - m_new)
        k_w = kb * jnp.exp(logw_end - m_new[..., None])[..., None]
        C = carry_decay[..., None, None] * C + jnp.einsum('bhcd,bhce->bhde', k_w, vb)
        n = carry_decay[..., None] * n + jnp.sum(k_w, axis=2)
        return (C, n, m_new), h

    carry0 = (jnp.zeros((Bsz, H, dk, dv), q.dtype), jnp.zeros((Bsz, H, dk), q.dtype),
              jnp.zeros((Bsz, H), q.dtype))
    _, out = lax.scan(step, carry0, (to_chunks(q, chunk), to_chunks(k, chunk), to_chunks(v, chunk),
                                     to_chunks(i_pre, chunk), to_chunks(log_f, chunk)))
    return from_chunks(out)


def segsum_exp(a):
    T = a.shape[-1]
    rep = jnp.broadcast_to(a[..., :, None], a.shape + (T,))
    rep = jnp.where(jnp.tril(jnp.ones((T, T), bool), -1), rep, 0.0)
    ss = jnp.cumsum(rep, axis=-2)
    return masked_exp(ss, jnp.tril(jnp.ones((T, T), bool)))


def ssd_chunked(x, a, Bm, Cm, chunk):
    Bsz, S, H, P = x.shape
    G, N = Bm.shape[2], Bm.shape[3]
    R = H // G
    NC = S // chunk
    x = x.reshape(Bsz, NC, chunk, G, R, P)
    Bm = Bm.reshape(Bsz, NC, chunk, G, N)
    Cm = Cm.reshape(Bsz, NC, chunk, G, N)
    a = a.reshape(Bsz, NC, chunk, G, R).transpose(0, 3, 4, 1, 2)
    a_cs = jnp.cumsum(a, axis=-1)
    L = segsum_exp(a)
    CB = jnp.einsum('bclgn,bcsgn->bgcls', Cm, Bm)
    y_diag = jnp.einsum('bgrcls,bcsgrp->bclgrp', CB[:, :, None] * L, x)
    decay_states = jnp.exp(a_cs[..., -1:] - a_cs)
    states = jnp.einsum('bclgn,bgrcl,bclgrp->bcgrpn', Bm, decay_states, x)
    states = jnp.concatenate([jnp.zeros_like(states[:, :1]), states], axis=1)
    chunk_a = jnp.pad(a_cs[..., -1], ((0, 0), (0, 0), (0, 0), (1, 0)))
    decay_chunk = segsum_exp(chunk_a)
    states = jnp.einsum('bgrzc,bcgrpn->bzgrpn', decay_chunk, states)[:, :-1]
    y_off = jnp.einsum('bclgn,bcgrpn,bgrcl->bclgrp', Cm, states, jnp.exp(a_cs))
    return (y_diag + y_off).reshape(Bsz, S, H, P)


def gla_branch(q_raw, k_raw, v_raw, gr, z, gate_w, gate_b, norm_w):
    Bsz, S = q_raw.shape[:2]
    q = q_raw.reshape(Bsz, S, GLA_HEADS, GLA_DK) * (GLA_DK ** -0.5)
    k = k_raw.reshape(Bsz, S, GLA_HEADS, GLA_DK)
    v = v_raw.reshape(Bsz, S, GLA_HEADS, GLA_DV)
    log_g = jax.nn.log_sigmoid(gr @ gate_w + gate_b) / GLA_GATE_NORM
    log_g = log_g.reshape(Bsz, S, GLA_HEADS, GLA_DK)
    o = chunk_gated_linear_attention(q, k, v, log_g, GLA_CHUNK)
    return grouped_rmsnorm(o, norm_w, GLA_HEADS) * jax.nn.silu(z)


def mlstm_branch(q_raw, k_raw, v_raw, i_raw, f_raw, o_raw, z, conv_w, conv_b, i_b, f_b, norm_w):
    Bsz, S = q_raw.shape[:2]
    qk = jax.nn.silu(causal_dwconv(jnp.concatenate([q_raw, k_raw], axis=-1), conv_w, conv_b))
    q = qk[..., :D_BRANCH].reshape(Bsz, S, MLSTM_HEADS, MLSTM_DH)
    k = qk[..., D_BRANCH:].reshape(Bsz, S, MLSTM_HEADS, MLSTM_DH) * (MLSTM_DH ** -0.5)
    v = v_raw.reshape(Bsz, S, MLSTM_HEADS, MLSTM_DH)
    i_pre = i_raw + i_b
    log_f = jax.nn.log_sigmoid(f_raw + f_b)
    h = chunk_mlstm(q, k, v, i_pre, log_f, MLSTM_CHUNK)
    h = jax.nn.sigmoid(o_raw) * h
    return grouped_rmsnorm(h, norm_w, MLSTM_HEADS) * jax.nn.silu(z)


def hgrn2_branch(q_raw, f_raw, i_raw, z, lb, norm_w):
    Bsz, S = q_raw.shape[:2]
    lb = lb.reshape(HGRN_HEADS, HGRN_DK)
    fr = f_raw.reshape(Bsz, S, HGRN_HEADS, HGRN_DK)
    f = lb + (1.0 - lb) * jax.nn.sigmoid(fr)
    log_f = jnp.log(jnp.maximum(f, 1e-30))
    k = (1.0 - lb) * jax.nn.sigmoid(-fr)
    q = q_raw.reshape(Bsz, S, HGRN_HEADS, HGRN_DK) * (HGRN_DK ** -0.5)
    v = i_raw.reshape(Bsz, S, HGRN_HEADS, HGRN_DV)
    o = chunk_gated_linear_attention(q, k, v, log_f, HGRN_CHUNK)
    return grouped_rmsnorm(o, norm_w, HGRN_HEADS) * jax.nn.silu(z)


def ssd_branch(x_raw, B_raw, C_raw, dt_raw, z, conv_w, conv_b, dt_bias, A_log, D, norm_w):
    Bsz, S = x_raw.shape[:2]
    xbc = jax.nn.silu(causal_dwconv(jnp.concatenate([x_raw, B_raw, C_raw], axis=-1), conv_w, conv_b))
    xs = xbc[..., :D_BRANCH].reshape(Bsz, S, SSD_HEADS, SSD_HEAD_DIM)
    Bm = xbc[..., D_BRANCH:D_BRANCH + SSD_BC].reshape(Bsz, S, SSD_GROUPS, SSD_STATE)
    Cm = xbc[..., D_BRANCH + SSD_BC:].reshape(Bsz, S, SSD_GROUPS, SSD_STATE)
    dt = jax.nn.softplus(dt_raw + dt_bias)
    A = -jnp.exp(A_log)
    y = ssd_chunked(xs * dt[..., None], dt * A, Bm, Cm, SSD_CHUNK)
    y = (y + D[:, None] * xs).reshape(Bsz, S, D_BRANCH)
    return grouped_rmsnorm(y * jax.nn.silu(z), norm_w, SSD_GROUPS)


def setup_inputs(seed: int = 0) -> dict:
    key = jax.random.key(seed)
    ks = jax.random.split(key, 24)
    f32 = jnp.float32
    nrm = lambda k, shape, s: s * jax.random.normal(k, shape, f32)
    dt0 = jnp.exp(jax.random.uniform(ks[15], (DEPTH, SSD_HEADS), f32, np.log(1e-3), np.log(1e-1)))
    return {
        "x": jax.random.normal(ks[0], (BATCH, SEQ, D_MODEL), f32),
        "norm_w": 1.0 + nrm(ks[1], (DEPTH, D_MODEL), 0.02),
        "w_in": nrm(ks[2], (DEPTH, D_MODEL, D_PROJ), D_MODEL ** -0.5),
        "gla_gate_w": nrm(ks[3], (DEPTH, GLA_GATE_RANK, GLA_QK), GLA_GATE_RANK ** -0.5),
        "gla_gate_b": nrm(ks[4], (DEPTH, GLA_QK), 0.1),
        "gla_norm_w": 1.0 + nrm(ks[5], (DEPTH, D_BRANCH), 0.02),
        "ml_conv_w": nrm(ks[6], (DEPTH, MLSTM_CONV, 2 * D_BRANCH), MLSTM_CONV ** -0.5),
        "ml_conv_b": nrm(ks[7], (DEPTH, 2 * D_BRANCH), 0.02),
        "ml_i_b": nrm(ks[8], (DEPTH, MLSTM_HEADS), 0.1),
        "ml_f_b": jnp.linspace(3.0, 6.0, MLSTM_HEADS, dtype=f32)[None] + nrm(ks[9], (DEPTH, MLSTM_HEADS), 0.1),
        "ml_norm_w": 1.0 + nrm(ks[10], (DEPTH, D_BRANCH), 0.02),
        "hg_lb_logits": nrm(ks[11], (DEPTH, HGRN_QF), 0.1),
        "hg_norm_w": 1.0 + nrm(ks[12], (DEPTH, D_BRANCH), 0.02),
        "ssd_conv_w": nrm(ks[13], (DEPTH, SSD_CONV, D_BRANCH + 2 * SSD_BC), SSD_CONV ** -0.5),
        "ssd_conv_b": nrm(ks[14], (DEPTH, D_BRANCH + 2 * SSD_BC), 0.02),
        "ssd_dt_bias": dt0 + jnp.log(-jnp.expm1(-dt0)),
        "ssd_A_log": jnp.log(jax.random.uniform(ks[16], (DEPTH, SSD_HEADS), f32, 1.0, 16.0)),
        "ssd_D": 1.0 + nrm(ks[17], (DEPTH, SSD_HEADS), 0.02),
        "ssd_norm_w": 1.0 + nrm(ks[18], (DEPTH, D_BRANCH), 0.02),
        "w_out": nrm(ks[19], (DEPTH, D_INNER, D_MODEL), D_INNER ** -0.5),
        "final_norm_w": 1.0 + nrm(ks[20], (D_MODEL,), 0.02),
    }


def reference(x, norm_w, w_in, gla_gate_w, gla_gate_b, gla_norm_w, ml_conv_w, ml_conv_b, ml_i_b,
              ml_f_b, ml_norm_w, hg_lb_logits, hg_norm_w, ssd_conv_w, ssd_conv_b, ssd_dt_bias,
              ssd_A_log, ssd_D, ssd_norm_w, w_out, final_norm_w):
    f32 = jnp.float32
    split_at = [int(s) for s in np.cumsum(PROJ_SIZES)[:-1]]
    p = jax.nn.softmax(hg_lb_logits.astype(f32), axis=0)
    lower_bounds = jnp.cumsum(p, axis=0) - p[0:1]
    h = x
    for l in range(DEPTH):
        u = rmsnorm(h, norm_w[l])
        proj = (u @ w_in[l]).astype(f32)
        (a_q, a_k, a_v, a_gr, a_z,
         b_q, b_k, b_v, b_i, b_f, b_o, b_z,
         c_q, c_f, c_i, c_z,
         d_x, d_B, d_C, d_dt, d_z) = jnp.split(proj, split_at, axis=-1)
        y_a = gla_branch(a_q, a_k, a_v, a_gr, a_z, gla_gate_w[l].astype(f32), gla_gate_b[l].astype(f32),
                         gla_norm_w[l])
        y_b = mlstm_branch(b_q, b_k, b_v, b_i, b_f, b_o, b_z, ml_conv_w[l].astype(f32), ml_conv_b[l],
                           ml_i_b[l].astype(f32), ml_f_b[l].astype(f32), ml_norm_w[l])
        y_c = hgrn2_branch(c_q, c_f, c_i, c_z, lower_bounds[l], hg_norm_w[l])
        y_d = ssd_branch(d_x, d_B, d_C, d_dt, d_z, ssd_conv_w[l].astype(f32), ssd_conv_b[l],
                         ssd_dt_bias[l].astype(f32), ssd_A_log[l].astype(f32), ssd_D[l].astype(f32),
                         ssd_norm_w[l])
        mixed = jnp.concatenate([y_a, y_b, y_c, y_d], axis=-1).astype(h.dtype)
        h = h + mixed @ w_out[l]
    return rmsnorm(h, final_norm_w)
```

```python
import functools

import numpy as np
import jax
import jax.numpy as jnp
from jax import lax
from jax.experimental import pallas as pl
from jax.experimental.pallas import tpu as pltpu

F32 = jnp.float32
BF16 = jnp.bfloat16

D_MODEL = 1024
D_BRANCH = 512
EPS = 1e-6
NEG_BIG = -1e30

GLA_HEADS, GLA_DK, GLA_DV = 4, 64, 128
GLA_GATE_RANK, GLA_GATE_NORM = 16, 16.0
MLSTM_HEADS, MLSTM_DH, MLSTM_CONV = 4, 128, 4
HGRN_HEADS, HGRN_DK, HGRN_DV = 4, 128, 128
SSD_HEAD_DIM, SSD_HEADS, SSD_GROUPS, SSD_STATE, SSD_CONV = 64, 8, 2, 128, 4
GLA_QK = GLA_HEADS * GLA_DK
HGRN_QF = HGRN_HEADS * HGRN_DK
SSD_BC = SSD_GROUPS * SSD_STATE
PROJ_SIZES = (
    GLA_QK, GLA_QK, D_BRANCH, GLA_GATE_RANK, D_BRANCH,
    D_BRANCH, D_BRANCH, D_BRANCH, MLSTM_HEADS, MLSTM_HEADS, D_BRANCH, D_BRANCH,
    HGRN_QF, HGRN_QF, D_BRANCH, D_BRANCH,
    D_BRANCH, SSD_BC, SSD_BC, SSD_HEADS, D_BRANCH,
)

LANES = 128
HIST = 8
VMEM_LIMIT = 56 * 1024 * 1024

CHUNK = 64
SSD_CHUNK = 128
N_LEVELS = 6

SM_GR, SM_I, SM_F, SM_DT = 0, 16, 20, 24


def _gla_constants():
    c = CHUNK
    t = np.arange(c)[:, None]
    d = np.arange(c)[None, :]
    blocks = [(d <= t), (d > t)]
    masks = [np.eye(c, dtype=bool)]
    for l in range(N_LEVELS):
        s = c >> (l + 1)
        mid_t = (t // (2 * s)) * (2 * s) + s
        upper = t >= mid_t
        sel = np.where(upper, (d >= mid_t) & (d <= t), (d > t) & (d <= mid_t - 1))
        blocks.append(sel)
        i, j = t, d
        same = (i // (2 * s)) == (j // (2 * s))
        mid_i = (i // (2 * s)) * (2 * s) + s
        masks.append(same & (i >= mid_i) & (j < mid_i))
    mstack = np.concatenate(blocks, axis=0).astype(np.float32)
    return jnp.asarray(mstack, BF16), jnp.asarray(np.stack(masks).astype(np.float32))


def _tri(n):
    return jnp.asarray(np.tril(np.ones((n, n), np.float32)), BF16)


def _lane_selector(first_lane, count):
    m = np.zeros((16, LANES), np.float32)
    for r in range(count):
        m[r, first_lane + r] = 1.0
    return jnp.asarray(m, BF16)


def _head_expander(first_lane, heads, width):
    m = np.zeros((LANES, heads * width), np.float32)
    for h in range(heads):
        m[first_lane + h, h * width:(h + 1) * width] = 1.0
    return jnp.asarray(m, BF16)


def _dot(a, b):
    return jnp.dot(a, b, preferred_element_type=F32)


def _dot_nt(a, b):
    return lax.dot_general(a, b, (((1,), (1,)), ((), ())), preferred_element_type=F32)


def _dot_tn(a, b):
    return lax.dot_general(a, b, (((0,), (0,)), ((), ())), preferred_element_type=F32)


def _split3(x):
    hi = x.astype(BF16)
    r1 = x - hi.astype(F32)
    mid = r1.astype(BF16)
    lo = (r1 - mid.astype(F32)).astype(BF16)
    return hi, mid, lo


def _sel_dot(sel, x):
    hi, mid, lo = _split3(x)
    return _dot(sel, hi) + _dot(sel, mid) + _dot(sel, lo)


def _sel_dot_nt(sel, x):
    hi, mid, lo = _split3(x)
    return _dot_nt(sel, hi) + _dot_nt(sel, mid) + _dot_nt(sel, lo)


def _dot_sel(x, sel):
    hi, mid, lo = _split3(x)
    return _dot(hi, sel) + _dot(mid, sel) + _dot(lo, sel)


def _dot_f32(a, b):
    ah = a.astype(BF16)
    al = (a - ah.astype(F32)).astype(BF16)
    bh = b.astype(BF16)
    bl = (b - bh.astype(F32)).astype(BF16)
    return _dot(ah, bh) + _dot(al, bh) + _dot(ah, bl)


def _softplus(x):
    return jnp.maximum(x, 0.0) + jnp.log1p(jnp.exp(-jnp.abs(x)))


def _log_sigmoid(x):
    return -_softplus(-x)


def _silu(x):
    return x * jax.nn.sigmoid(x)


def _lane_iota():
    return lax.broadcasted_iota(jnp.int32, (1, LANES), 1)


def _inproj_kernel(x_ref, nw_ref, wa_ref, wb_ref, wc_ref, wd_ref, ws_ref,
                   oa_ref, ob_ref, oc_ref, od_ref, os_ref):
    x = x_ref[...]
    ms = jnp.mean(x * x, axis=-1, keepdims=True)
    u = (x * lax.rsqrt(ms + EPS) * nw_ref[...]).astype(BF16)
    for w_ref, o_ref in ((wa_ref, oa_ref), (wb_ref, ob_ref), (wc_ref, oc_ref),
                         (wd_ref, od_ref), (ws_ref, os_ref)):
        o_ref[...] = _dot(u, w_ref[...])


def _inproj(h, norm_w, weights, tile):
    n_tok = h.shape[0]
    grid = (n_tok // tile,)
    const = lambda i: (0, 0)
    row = lambda i: (i, 0)
    in_specs = [pl.BlockSpec((tile, D_MODEL), row), pl.BlockSpec((1, D_MODEL), const)]
    in_specs += [pl.BlockSpec(w.shape, const, pipeline_mode=pl.Buffered(1)) for w in weights]
    out_specs = [pl.BlockSpec((tile, w.shape[1]), row) for w in weights]
    out_shape = [jax.ShapeDtypeStruct((n_tok, w.shape[1]), F32) for w in weights]
    return pl.pallas_call(
        _inproj_kernel, grid=grid, in_specs=in_specs, out_specs=out_specs, out_shape=out_shape,
        compiler_params=pltpu.CompilerParams(dimension_semantics=("arbitrary",),
                                             vmem_limit_bytes=VMEM_LIMIT),
        name="inproj",
    )(h, norm_w, *weights)


def _outproj_kernel(h_ref, ya_ref, yb_ref, yc_ref, yd_ref, w_ref, fw_ref, o_ref, *, final):
    acc = h_ref[...]
    for i, y_ref in enumerate((ya_ref, yb_ref, yc_ref, yd_ref)):
        acc = acc + _dot(y_ref[...], w_ref[i * D_BRANCH:(i + 1) * D_BRANCH, :])
    if final:
        ms = jnp.mean(acc * acc, axis=-1, keepdims=True)
        acc = acc * lax.rsqrt(ms + EPS) * fw_ref[...]
    o_ref[...] = acc


def _outproj(h, ys, w_out, final_w, final, tile):
    n_tok = h.shape[0]
    const = lambda i: (0, 0)
    row = lambda i: (i, 0)
    in_specs = [pl.BlockSpec((tile, D_MODEL), row)]
    in_specs += [pl.BlockSpec((tile, D_BRANCH), row) for _ in ys]
    in_specs += [pl.BlockSpec(w_out.shape, const), pl.BlockSpec((1, D_MODEL), const)]
    return pl.pallas_call(
        functools.partial(_outproj_kernel, final=final),
        grid=(n_tok // tile,), in_specs=in_specs,
        out_specs=pl.BlockSpec((tile, D_MODEL), row),
        out_shape=jax.ShapeDtypeStruct((n_tok, D_MODEL), F32),
        compiler_params=pltpu.CompilerParams(dimension_semantics=("arbitrary",),
                                             vmem_limit_bytes=VMEM_LIMIT),
        name="outproj",
    )(h, *ys, w_out, final_w)


def _norm_gate_store(y_ref, rows, col0, parts, nw_ref, z_parts):
    width = sum(p.shape[-1] for p in parts)
    ss = sum(jnp.sum(p * p, axis=-1, keepdims=True) for p in parts)
    scale = lax.rsqrt(ss * (1.0 / width) + EPS)
    c = col0
    for p, z in zip(parts, z_parts):
        w = p.shape[-1]
        out = p * scale * nw_ref[:, c:c + w]
        if z is not None:
            out = out * _silu(z)
        y_ref[rows, c:c + w] = out.astype(y_ref.dtype)
        c += w


def _causal_conv_silu(src_ref, width, xe_ref, hist_ref, w_ref, b_ref, dst_ref, tile, taps):
    xe_ref[0:HIST, :] = hist_ref[...]
    xe_ref[HIST:HIST + tile, :] = src_ref[:, 0:width]
    hist_ref[...] = src_ref[tile - HIST:tile, 0:width]
    blk = 64
    for r in range(0, tile, blk):
        acc = b_ref[...] + w_ref[taps - 1:taps, :] * xe_ref[HIST + r:HIST + r + blk, :]
        for k in range(taps - 1):
            off = HIST - (taps - 1) + k
            acc = acc + w_ref[k:k + 1, :] * xe_ref[off + r:off + r + blk, :]
        dst_ref[r:r + blk, :] = _silu(acc)


def _gla_chunks(q_ref, k_ref, lg_ref, v_ref, z_ref, y_ref, st_ref, nw_ref, ms_ref, mk_ref,
                *, dk, tile):
    n_groups = q_ref.shape[1] // LANES
    hp = LANES // dk
    lane = _lane_iota()

    def body(c, carry):
        rows = pl.ds(pl.multiple_of(c * CHUNK, CHUNK), CHUNK)
        for g in range(n_groups):
            ls = slice(g * LANES, (g + 1) * LANES)
            e = _sel_dot(ms_ref[...], lg_ref[rows, ls])
            q = q_ref[rows, ls]
            k = k_ref[rows, ls]
            gcs = e[0:CHUNK]
            qg = (q * jnp.exp(gcs)).astype(BF16)
            kd = (k * jnp.exp(e[CHUNK:2 * CHUNK])).astype(BF16)
            dec = jnp.exp(gcs[CHUNK - 1:CHUNK, :])
            ql = [q.astype(BF16)]
            kl = [k.astype(BF16)]
            for l in range(N_LEVELS):
                w = jnp.exp(e[(2 + l) * CHUNK:(3 + l) * CHUNK])
                ql.append((q * w).astype(BF16))
                kl.append((k * w).astype(BF16))
            st = st_ref[g]
            stb = st.astype(BF16)
            upd = None
            for j in range(hp):
                h = g * hp + j
                hs = slice(h * LANES, (h + 1) * LANES)
                lm = None if hp == 1 else (lane >= j * dk) & (lane < (j + 1) * dk)
                pick = (lambda a: a) if lm is None else (lambda a: jnp.where(lm, a, jnp.zeros_like(a)))
                a = jnp.zeros((CHUNK, CHUNK), F32)
                for l in range(N_LEVELS + 1):
                    a = a + _dot_nt(pick(ql[l]), kl[l]) * mk_ref[l]
                vh = v_ref[rows, hs].astype(BF16)
                o = _dot(a.astype(BF16), vh) + _dot_nt(pick(qg), stb)
                _norm_gate_store(y_ref, rows, h * LANES, [o], nw_ref, [z_ref[rows, hs]])
                u = _dot_tn(vh, kd)
                upd = u if upd is None else jnp.where(lm, u, upd)
            st_ref[g] = st * dec + upd
        return carry

    lax.fori_loop(0, tile // CHUNK, body, 0)


def _mixer_call(kernel, n_batch, seq, tile, tok_inputs, const_inputs, scratch, name):
    nt = seq // tile
    row = lambda b, t: (b * nt + t, 0)
    in_specs = [pl.BlockSpec((tile, a.shape[1]), row) for a in tok_inputs]
    for a in const_inputs:
        in_specs.append(pl.BlockSpec(a.shape, lambda b, t, nd=a.ndim: (0,) * nd))
    return pl.pallas_call(
        kernel, grid=(n_batch, nt), in_specs=in_specs,
        out_specs=pl.BlockSpec((tile, D_BRANCH), row),
        out_shape=jax.ShapeDtypeStruct((n_batch * seq, D_BRANCH), BF16),
        scratch_shapes=scratch,
        compiler_params=pltpu.CompilerParams(dimension_semantics=("arbitrary", "arbitrary"),
                                             vmem_limit_bytes=VMEM_LIMIT),
        name=name,
    )(*tok_inputs, *const_inputs)


def _gla_kernel(pa_ref, sm_ref, gw_ref, gb_ref, nw_ref, ms_ref, mk_ref, y_ref,
                q_s, lg_s, st_s, *, tile):
    @pl.when(pl.program_id(1) == 0)
    def _():
        st_s[...] = jnp.zeros_like(st_s)

    q_s[...] = pa_ref[:, 0:GLA_QK] * (GLA_DK ** -0.5)
    x = _dot_f32(sm_ref[...], gw_ref[...]) + gb_ref[...]
    lg_s[...] = _log_sigmoid(x) * (1.0 / GLA_GATE_NORM)
    _gla_chunks(q_s, pa_ref.at[:, GLA_QK:2 * GLA_QK], lg_s,
                pa_ref.at[:, 2 * GLA_QK:2 * GLA_QK + D_BRANCH],
                pa_ref.at[:, 2 * GLA_QK + D_BRANCH:2 * GLA_QK + 2 * D_BRANCH],
                y_ref, st_s, nw_ref, ms_ref, mk_ref, dk=GLA_DK, tile=tile)


def _hgrn_kernel(pc_ref, lb_ref, nw_ref, ms_ref, mk_ref, y_ref, q_s, k_s, lg_s, st_s, *, tile):
    @pl.when(pl.program_id(1) == 0)
    def _():
        st_s[...] = jnp.zeros_like(st_s)

    lb = lb_ref[...]
    fr = pc_ref[:, HGRN_QF:2 * HGRN_QF]
    f = lb + (1.0 - lb) * jax.nn.sigmoid(fr)
    lg_s[...] = jnp.log(jnp.maximum(f, 1e-30))
    k_s[...] = (1.0 - lb) * jax.nn.sigmoid(-fr)
    q_s[...] = pc_ref[:, 0:HGRN_QF] * (HGRN_DK ** -0.5)
    _gla_chunks(q_s, k_s, lg_s,
                pc_ref.at[:, 2 * HGRN_QF:2 * HGRN_QF + D_BRANCH],
                pc_ref.at[:, 2 * HGRN_QF + D_BRANCH:2 * HGRN_QF + 2 * D_BRANCH],
                y_ref, st_s, nw_ref, ms_ref, mk_ref, dk=HGRN_DK, tile=tile)


def _mlstm_kernel(pb_ref, sm_ref, cw_ref, cb_ref, gbias_ref, nw_ref, tri_ref, sel_i_ref, sel_f_ref,
                  y_ref, xe_s, hist_s, qk_s, g_s, c_s, n_s, m_s, *, tile):
    @pl.when(pl.program_id(1) == 0)
    def _():
        hist_s[...] = jnp.zeros_like(hist_s)
        c_s[...] = jnp.zeros_like(c_s)
        n_s[...] = jnp.zeros_like(n_s)
        m_s[...] = jnp.zeros_like(m_s)

    _causal_conv_silu(pb_ref, 2 * D_BRANCH, xe_s, hist_s, cw_ref, cb_ref, qk_s, tile, MLSTM_CONV)
    lane = _lane_iota()
    gates = sm_ref[...] + gbias_ref[...]
    is_f = (lane >= SM_F) & (lane < SM_F + MLSTM_HEADS)
    g_s[...] = jnp.where(is_f, _log_sigmoid(gates), gates)

    ri = lax.broadcasted_iota(jnp.int32, (CHUNK, CHUNK), 0)
    ci = lax.broadcasted_iota(jnp.int32, (CHUNK, CHUNK), 1)
    causal = ci <= ri
    v0, o0, z0 = 2 * D_BRANCH, 3 * D_BRANCH, 4 * D_BRANCH

    def body(c, carry):
        rows = pl.ds(pl.multiple_of(c * CHUNK, CHUNK), CHUNK)
        gc = g_s[rows, :]
        bcol = _sel_dot(tri_ref[...], gc)
        i_rows = _sel_dot_nt(sel_i_ref[...], gc)
        b_rows = _sel_dot_nt(sel_f_ref[...], bcol)
        for h in range(MLSTM_HEADS):
            hs = slice(h * LANES, (h + 1) * LANES)
            b_c = bcol[:, SM_F + h:SM_F + h + 1]
            i_c = gc[:, SM_I + h:SM_I + h + 1]
            b_r = b_rows[h:h + 1, :]
            i_r = i_rows[h:h + 1, :]
            m = m_s[h:h + 1, 0:1]
            logw = jnp.where(causal, b_c - b_r + i_r, NEG_BIG)
            m_inter = b_c + m
            m_row = jnp.maximum(jnp.max(logw, axis=-1, keepdims=True), m_inter)
            w = jnp.exp(logw - m_row)
            qh = qk_s[rows, hs]
            kh = qk_s[rows, D_BRANCH + h * LANES:D_BRANCH + (h + 1) * LANES] * (MLSTM_DH ** -0.5)
            vh = pb_ref[rows, v0 + h * LANES:v0 + (h + 1) * LANES].astype(BF16)
            qb = qh.astype(BF16)
            s = _dot_nt(qb, kh.astype(BF16)) * w
            inter = jnp.exp(m_inter - m_row)
            cst = c_s[h]
            nst = n_s[h:h + 1, :]
            num = _dot(s.astype(BF16), vh) + inter * _dot(qb, cst.astype(BF16))
            den = jnp.sum(s, axis=-1, keepdims=True) + inter * jnp.sum(qh * nst, axis=-1, keepdims=True)
            hh = num / jnp.maximum(jnp.abs(den), jnp.exp(-m_row))
            b_last = b_c[CHUNK - 1:CHUNK, :]
            lwe = b_last - b_c + i_c
            m_new = jnp.maximum(b_last + m, jnp.max(lwe, axis=0, keepdims=True))
            cd = jnp.exp(b_last + m - m_new)
            kw = kh * jnp.exp(lwe - m_new)
            c_s[h] = cd * cst + _dot_tn(kw.astype(BF16), vh)
            n_s[h:h + 1, :] = cd * nst + jnp.sum(kw, axis=0, keepdims=True)
            m_s[h:h + 1, :] = jnp.broadcast_to(m_new, (1, LANES))
            og = jax.nn.sigmoid(pb_ref[rows, o0 + h * LANES:o0 + (h + 1) * LANES])
            _norm_gate_store(y_ref, rows, h * LANES, [og * hh], nw_ref,
                             [pb_ref[rows, z0 + h * LANES:z0 + (h + 1) * LANES]])
        return carry

    lax.fori_loop(0, tile // CHUNK, body, 0)


def _ssd_kernel(pd_ref, sm_ref, cw_ref, cb_ref, dtb_ref, alog_ref, dx_ref, nw_ref,
                tri_ref, sel_ref, exp_ref, y_ref, xe_s, hist_s, xbc_s, a_s, dtx_s, st_s, *, tile):
    @pl.when(pl.program_id(1) == 0)
    def _():
        hist_s[...] = jnp.zeros_like(hist_s)
        st_s[...] = jnp.zeros_like(st_s)

    width = D_BRANCH + 2 * SSD_BC
    _causal_conv_silu(pd_ref, width, xe_s, hist_s, cw_ref, cb_ref, xbc_s, tile, SSD_CONV)
    lane = _lane_iota()
    is_dt = (lane >= SM_DT) & (lane < SM_DT + SSD_HEADS)
    dt = jnp.where(is_dt, _softplus(sm_ref[...] + dtb_ref[...]), 0.0)
    a_s[...] = dt * jnp.where(is_dt, -jnp.exp(alog_ref[...]), 0.0)
    dtx_s[...] = _dot_sel(dt, exp_ref[...])

    ri = lax.broadcasted_iota(jnp.int32, (SSD_CHUNK, SSD_CHUNK), 0)
    ci = lax.broadcasted_iota(jnp.int32, (SSD_CHUNK, SSD_CHUNK), 1)
    tril = ci <= ri
    b0, c0, z0 = D_BRANCH, D_BRANCH + SSD_BC, D_BRANCH + 2 * SSD_BC
    heads_per_pair = LANES // SSD_HEAD_DIM
    pairs_per_group = SSD_HEADS // SSD_GROUPS // heads_per_pair

    def body(c, carry):
        rows = pl.ds(pl.multiple_of(c * SSD_CHUNK, SSD_CHUNK), SSD_CHUNK)
        acs = _sel_dot(tri_ref[...], a_s[rows, :])
        acs_x = _dot_sel(acs, exp_ref[...])
        a_rows = _sel_dot_nt(sel_ref[...], acs)
        for g in range(SSD_GROUPS):
            bg = xbc_s[rows, b0 + g * SSD_STATE:b0 + (g + 1) * SSD_STATE]
            cg = xbc_s[rows, c0 + g * SSD_STATE:c0 + (g + 1) * SSD_STATE]
            cgb = cg.astype(BF16)
            cb = _dot_nt(cgb, bg.astype(BF16))
            ys, zs = [], []
            for p in range(pairs_per_group):
                pair = g * pairs_per_group + p
                ls = slice(pair * LANES, (pair + 1) * LANES)
                xs = xbc_s[rows, ls]
                xdt = xs * dtx_s[rows, ls]
                st = st_s[pair]
                ax = acs_x[:, ls]
                y = _dot(cgb, st.astype(BF16)) * jnp.exp(ax) + dx_ref[:, ls] * xs
                upd = jnp.zeros_like(st)
                for j in range(heads_per_pair):
                    h = pair * heads_per_pair + j
                    lm = (lane >= j * SSD_HEAD_DIM) & (lane < (j + 1) * SSD_HEAD_DIM)
                    xh = jnp.where(lm, xdt, 0.0).astype(BF16)
                    a_c = acs[:, SM_DT + h:SM_DT + h + 1]
                    a_r = a_rows[h:h + 1, :]
                    lmat = jnp.exp(jnp.where(tril, a_c - a_r, NEG_BIG))
                    y = y + _dot((cb * lmat).astype(BF16), xh)
                    dcol = jnp.exp(a_c[SSD_CHUNK - 1:SSD_CHUNK, :] - a_c)
                    upd = upd + _dot_tn((bg * dcol).astype(BF16), xh)
                st_s[pair] = st * jnp.exp(ax[SSD_CHUNK - 1:SSD_CHUNK, :]) + upd
                ys.append(y * _silu(pd_ref[rows, z0 + pair * LANES:z0 + (pair + 1) * LANES]))
                zs.append(None)
            _norm_gate_store(y_ref, rows, g * pairs_per_group * LANES, ys, nw_ref, zs)
        return carry

    lax.fori_loop(0, tile // SSD_CHUNK, body, 0)


def _pad_lanes(parts, total=LANES):
    width = sum(p.shape[-1] for p in parts)
    lead = parts[0].shape[:-1]
    return jnp.concatenate(list(parts) + [jnp.zeros(lead + (total - width,), parts[0].dtype)], axis=-1)


def _small_vector(i_part, f_part, dt_part):
    z = jnp.zeros((GLA_GATE_RANK,), F32)
    return _pad_lanes([z, i_part.astype(F32), f_part.astype(F32), dt_part.astype(F32)])[None, :]


def kernel(x, norm_w, w_in, gla_gate_w, gla_gate_b, gla_norm_w, ml_conv_w, ml_conv_b, ml_i_b, ml_f_b,
           ml_norm_w, hg_lb_logits, hg_norm_w, ssd_conv_w, ssd_conv_b, ssd_dt_bias, ssd_A_log, ssd_D,
           ssd_norm_w, w_out, final_norm_w):
    n_batch, seq, _ = x.shape
    depth = w_in.shape[0]
    tile = min(256, seq)
    n_tok = n_batch * seq

    mstack, masks = _gla_constants()
    tri_c, tri_s = _tri(CHUNK), _tri(SSD_CHUNK)
    sel_i, sel_f = _lane_selector(SM_I, MLSTM_HEADS), _lane_selector(SM_F, MLSTM_HEADS)
    sel_dt = _lane_selector(SM_DT, SSD_HEADS)
    expander = _head_expander(SM_DT, SSD_HEADS, SSD_HEAD_DIM)
    zero4 = jnp.zeros((MLSTM_HEADS,), F32)
    zero8 = jnp.zeros((SSD_HEADS,), F32)

    p = jax.nn.softmax(hg_lb_logits.astype(F32), axis=0)
    lower_bounds = jnp.cumsum(p, axis=0) - p[0:1]

    split_at = [int(s) for s in np.cumsum(PROJ_SIZES)[:-1]]
    h = x.reshape(n_tok, D_MODEL)
    row2 = lambda v: v.astype(F32).reshape(1, -1)
    for l in range(depth):
        (a_q, a_k, a_v, a_gr, a_z, b_q, b_k, b_v, b_i, b_f, b_o, b_z,
         c_q, c_f, c_i, c_z, d_x, d_b, d_c, d_dt, d_z) = jnp.split(w_in[l], split_at, axis=-1)
        weights = [
            jnp.concatenate([a_q, a_k, a_v, a_z], axis=-1).astype(BF16),
            jnp.concatenate([b_q, b_k, b_v, b_o, b_z], axis=-1).astype(BF16),
            jnp.concatenate([c_q, c_f, c_i, c_z], axis=-1).astype(BF16),
            jnp.concatenate([d_x, d_b, d_c, d_z], axis=-1).astype(BF16),
            _pad_lanes([a_gr, b_i, b_f, d_dt]).astype(BF16),
        ]
        pa, pb, pc, pd, sm = _inproj(h, row2(norm_w[l]), weights, tile)

        gate_w = jnp.concatenate(
            [gla_gate_w[l].astype(F32), jnp.zeros((LANES - GLA_GATE_RANK, GLA_QK), F32)], axis=0)
        y_a = _mixer_call(
            functools.partial(_gla_kernel, tile=tile), n_batch, seq, tile, [pa, sm],
            [gate_w, row2(gla_gate_b[l]), row2(gla_norm_w[l]), mstack, masks],
            [pltpu.VMEM((tile, GLA_QK), F32), pltpu.VMEM((tile, GLA_QK), F32),
             pltpu.VMEM((GLA_QK // LANES, GLA_DV, LANES), F32)], "gla")

        y_b = _mixer_call(
            functools.partial(_mlstm_kernel, tile=tile), n_batch, seq, tile, [pb, sm],
            [ml_conv_w[l].astype(F32), row2(ml_conv_b[l]),
             _small_vector(ml_i_b[l], ml_f_b[l], zero8), row2(ml_norm_w[l]), tri_c, sel_i, sel_f],
            [pltpu.VMEM((tile + HIST, 2 * D_BRANCH), F32), pltpu.VMEM((HIST, 2 * D_BRANCH), F32),
             pltpu.VMEM((tile, 2 * D_BRANCH), F32), pltpu.VMEM((tile, LANES), F32),
             pltpu.VMEM((MLSTM_HEADS, MLSTM_DH, MLSTM_DH), F32), pltpu.VMEM((8, LANES), F32),
             pltpu.VMEM((8, LANES), F32)], "mlstm")

        y_c = _mixer_call(
            functools.partial(_hgrn_kernel, tile=tile), n_batch, seq, tile, [pc],
            [row2(lower_bounds[l]), row2(hg_norm_w[l]), mstack, masks],
            [pltpu.VMEM((tile, HGRN_QF), F32), pltpu.VMEM((tile, HGRN_QF), F32),
             pltpu.VMEM((tile, HGRN_QF), F32), pltpu.VMEM((HGRN_QF // LANES, HGRN_DV, LANES), F32)],
            "hgrn")

        width = D_BRANCH + 2 * SSD_BC
        y_d = _mixer_call(
            functools.partial(_ssd_kernel, tile=tile), n_batch, seq, tile, [pd, sm],
            [ssd_conv_w[l].astype(F32), row2(ssd_conv_b[l]),
             _small_vector(zero4, zero4, ssd_dt_bias[l]), _small_vector(zero4, zero4, ssd_A_log[l]),
             row2(jnp.repeat(ssd_D[l].astype(F32), SSD_HEAD_DIM)), row2(ssd_norm_w[l]),
             tri_s, sel_dt, expander],
            [pltpu.VMEM((tile + HIST, width), F32), pltpu.VMEM((HIST, width), F32),
             pltpu.VMEM((tile, width), F32), pltpu.VMEM((tile, LANES), F32),
             pltpu.VMEM((tile, D_BRANCH), F32),
             pltpu.VMEM((D_BRANCH // LANES, SSD_STATE, LANES), F32)], "ssd")

        h = _outproj(h, [y_a, y_b, y_c, y_d], w_out[l].astype(BF16), row2(final_norm_w),
                     l == depth - 1, tile)
    return h.reshape(n_batch, seq, D_MODEL)
```

```python
import functools

import numpy as np
import jax
import jax.numpy as jnp
from jax import lax
from jax.experimental import pallas as pl
from jax.experimental.pallas import tpu as pltpu

F32 = jnp.float32
BF16 = jnp.bfloat16

D_MODEL = 1024
D_BRANCH = 512
EPS = 1e-6
NEG_BIG = -1e30

GLA_HEADS, GLA_DK, GLA_DV = 4, 64, 128
GLA_GATE_RANK, GLA_GATE_NORM = 16, 16.0
MLSTM_HEADS, MLSTM_DH, MLSTM_CONV = 4, 128, 4
HGRN_HEADS, HGRN_DK, HGRN_DV = 4, 128, 128
SSD_HEAD_DIM, SSD_HEADS, SSD_GROUPS, SSD_STATE, SSD_CONV = 64, 8, 2, 128, 4
GLA_QK = GLA_HEADS * GLA_DK
HGRN_QF = HGRN_HEADS * HGRN_DK
SSD_BC = SSD_GROUPS * SSD_STATE
PROJ_SIZES = (
    GLA_QK, GLA_QK, D_BRANCH, GLA_GATE_RANK, D_BRANCH,
    D_BRANCH, D_BRANCH, D_BRANCH, MLSTM_HEADS, MLSTM_HEADS, D_BRANCH, D_BRANCH,
    HGRN_QF, HGRN_QF, D_BRANCH, D_BRANCH,
    D_BRANCH, SSD_BC, SSD_BC, SSD_HEADS, D_BRANCH,
)

LANES = 128
HIST = 8
VMEM_LIMIT = 56 * 1024 * 1024

CHUNK = 64
SSD_CHUNK = 128
N_LEVELS = 6
MIN_VPU_LEVEL = 4

SM_GR, SM_I, SM_F, SM_DT = 0, 16, 20, 24


def _gla_constants():
    c = CHUNK
    t = np.arange(c)[:, None]
    d = np.arange(c)[None, :]
    blocks = [(d <= t), (d > t)]
    masks = [np.eye(c, dtype=bool)]
    for l in range(N_LEVELS):
        s = c >> (l + 1)
        mid_t = (t // (2 * s)) * (2 * s) + s
        upper = t >= mid_t
        if s < MIN_VPU_LEVEL:
            blocks.append(np.where(upper, (d >= mid_t) & (d <= t), (d > t) & (d <= mid_t - 1)))
        same = (t // (2 * s)) == (d // (2 * s))
        masks.append(same & upper & (d < mid_t))
    mstack = np.concatenate(blocks, axis=0).astype(np.float32)
    mstack = np.concatenate([mstack, mstack], axis=1)
    return jnp.asarray(mstack, BF16), jnp.asarray(np.stack(masks).astype(np.float32))


def _tri(n):
    return jnp.asarray(np.tril(np.ones((n, n), np.float32)), BF16)


def _lane_selector(first_lane, count):
    m = np.zeros((16, LANES), np.float32)
    for r in range(count):
        m[r, first_lane + r] = 1.0
    return jnp.asarray(m, BF16)


def _head_expander(first_lane, heads, width):
    m = np.zeros((LANES, heads * width), np.float32)
    for h in range(heads):
        m[first_lane + h, h * width:(h + 1) * width] = 1.0
    return jnp.asarray(m, BF16)


def _dot(a, b):
    return jnp.dot(a, b, preferred_element_type=F32)


def _dot_nt(a, b):
    return lax.dot_general(a, b, (((1,), (1,)), ((), ())), preferred_element_type=F32)


def _dot_tn(a, b):
    return lax.dot_general(a, b, (((0,), (0,)), ((), ())), preferred_element_type=F32)


def _split3(x):
    hi = x.astype(BF16)
    r1 = x - hi.astype(F32)
    mid = r1.astype(BF16)
    lo = (r1 - mid.astype(F32)).astype(BF16)
    return hi, mid, lo


def _sel_dot(sel, x):
    hi, mid, lo = _split3(x)
    return _dot(sel, hi) + _dot(sel, mid) + _dot(sel, lo)


def _sel_dot_nt(sel, x):
    hi, mid, lo = _split3(x)
    return _dot_nt(sel, hi) + _dot_nt(sel, mid) + _dot_nt(sel, lo)


def _sel_dot2(sel2, x):
    hi = x.astype(BF16)
    mid = (x - hi.astype(F32)).astype(BF16)
    return _dot(sel2, jnp.concatenate([hi, mid], axis=0))


def _level_exponent(gcs, s):
    pieces = []
    for b in range(0, CHUNK, 2 * s):
        ref_row = gcs[b + s - 1:b + s, :]
        if s >= 8:
            pieces += [ref_row - gcs[b:b + s], gcs[b + s:b + 2 * s] - ref_row]
        else:
            diff = gcs[b:b + 2 * s] - ref_row
            pieces.append(jnp.minimum(diff, -diff))
    return jnp.concatenate(pieces, axis=0)


def _dot_sel(x, sel):
    hi, mid, lo = _split3(x)
    return _dot(hi, sel) + _dot(mid, sel) + _dot(lo, sel)


def _dot_f32(a, b):
    ah = a.astype(BF16)
    al = (a - ah.astype(F32)).astype(BF16)
    bh = b.astype(BF16)
    bl = (b - bh.astype(F32)).astype(BF16)
    return _dot(ah, bh) + _dot(al, bh) + _dot(ah, bl)


def _softplus(x):
    return jnp.maximum(x, 0.0) + jnp.log1p(jnp.exp(-jnp.abs(x)))


def _log_sigmoid(x):
    return -_softplus(-x)


def _silu(x):
    return x * jax.nn.sigmoid(x)


def _lane_iota():
    return lax.broadcasted_iota(jnp.int32, (1, LANES), 1)


def _inproj_kernel(x_ref, nw_ref, wa_ref, wb_ref, wc_ref, wd_ref, ws_ref,
                   oa_ref, ob_ref, oc_ref, od_ref, os_ref):
    x = x_ref[...]
    ms = jnp.mean(x * x, axis=-1, keepdims=True)
    u = (x * lax.rsqrt(ms + EPS) * nw_ref[...]).astype(BF16)
    for w_ref, o_ref in ((wa_ref, oa_ref), (wb_ref, ob_ref), (wc_ref, oc_ref),
                         (wd_ref, od_ref), (ws_ref, os_ref)):
        o_ref[...] = _dot(u, w_ref[...])


def _inproj(h, norm_w, weights, tile):
    n_tok = h.shape[0]
    grid = (n_tok // tile,)
    const = lambda i: (0, 0)
    row = lambda i: (i, 0)
    in_specs = [pl.BlockSpec((tile, D_MODEL), row), pl.BlockSpec((1, D_MODEL), const)]
    in_specs += [pl.BlockSpec(w.shape, const, pipeline_mode=pl.Buffered(1)) for w in weights]
    out_specs = [pl.BlockSpec((tile, w.shape[1]), row) for w in weights]
    out_shape = [jax.ShapeDtypeStruct((n_tok, w.shape[1]), F32) for w in weights]
    return pl.pallas_call(
        _inproj_kernel, grid=grid, in_specs=in_specs, out_specs=out_specs, out_shape=out_shape,
        compiler_params=pltpu.CompilerParams(dimension_semantics=("arbitrary",),
                                             vmem_limit_bytes=VMEM_LIMIT),
        name="inproj",
    )(h, norm_w, *weights)


def _outproj_kernel(h_ref, ya_ref, yb_ref, yc_ref, yd_ref, w_ref, fw_ref, o_ref, *, final):
    acc = h_ref[...]
    for i, y_ref in enumerate((ya_ref, yb_ref, yc_ref, yd_ref)):
        acc = acc + _dot(y_ref[...], w_ref[i * D_BRANCH:(i + 1) * D_BRANCH, :])
    if final:
        ms = jnp.mean(acc * acc, axis=-1, keepdims=True)
        acc = acc * lax.rsqrt(ms + EPS) * fw_ref[...]
    o_ref[...] = acc


def _outproj(h, ys, w_out, final_w, final, tile):
    n_tok = h.shape[0]
    const = lambda i: (0, 0)
    row = lambda i: (i, 0)
    in_specs = [pl.BlockSpec((tile, D_MODEL), row)]
    in_specs += [pl.BlockSpec((tile, D_BRANCH), row) for _ in ys]
    in_specs += [pl.BlockSpec(w_out.shape, const), pl.BlockSpec((1, D_MODEL), const)]
    return pl.pallas_call(
        functools.partial(_outproj_kernel, final=final),
        grid=(n_tok // tile,), in_specs=in_specs,
        out_specs=pl.BlockSpec((tile, D_MODEL), row),
        out_shape=jax.ShapeDtypeStruct((n_tok, D_MODEL), F32),
        compiler_params=pltpu.CompilerParams(dimension_semantics=("arbitrary",),
                                             vmem_limit_bytes=VMEM_LIMIT),
        name="outproj",
    )(h, *ys, w_out, final_w)


def _norm_gate_store(y_ref, rows, col0, parts, nw_ref, z_parts):
    width = sum(p.shape[-1] for p in parts)
    ss = sum(jnp.sum(p * p, axis=-1, keepdims=True) for p in parts)
    scale = lax.rsqrt(ss * (1.0 / width) + EPS)
    c = col0
    for p, z in zip(parts, z_parts):
        w = p.shape[-1]
        out = p * scale * nw_ref[:, c:c + w]
        if z is not None:
            out = out * _silu(z)
        y_ref[rows, c:c + w] = out.astype(y_ref.dtype)
        c += w


def _causal_conv_silu(src_ref, width, xe_ref, hist_ref, w_ref, b_ref, dst_ref, tile, taps):
    xe_ref[0:HIST, :] = hist_ref[...]
    xe_ref[HIST:HIST + tile, :] = src_ref[:, 0:width]
    hist_ref[...] = src_ref[tile - HIST:tile, 0:width]
    blk = 64
    for r in range(0, tile, blk):
        acc = b_ref[...] + w_ref[taps - 1:taps, :] * xe_ref[HIST + r:HIST + r + blk, :]
        for k in range(taps - 1):
            off = HIST - (taps - 1) + k
            acc = acc + w_ref[k:k + 1, :] * xe_ref[off + r:off + r + blk, :]
        dst_ref[r:r + blk, :] = _silu(acc)


def _gla_chunks(q_ref, k_ref, lg_ref, v_ref, z_ref, y_ref, st_ref, nw_ref, ms_ref, mk_ref,
                *, dk, tile):
    n_groups = q_ref.shape[1] // LANES
    hp = LANES // dk
    lane = _lane_iota()

    def body(c, carry):
        rows = pl.ds(pl.multiple_of(c * CHUNK, CHUNK), CHUNK)
        es = [_sel_dot2(ms_ref[...], lg_ref[rows, p * 2 * LANES:(p + 1) * 2 * LANES])
              for p in range(n_groups // 2)]
        ops = []
        for g in range(n_groups):
            ls = slice(g * LANES, (g + 1) * LANES)
            e = es[g // 2][:, (g % 2) * LANES:(g % 2 + 1) * LANES]
            q = q_ref[rows, ls]
            k = k_ref[rows, ls]
            gcs = e[0:CHUNK]
            qg = (q * jnp.exp(gcs)).astype(BF16)
            kd = (k * jnp.exp(e[CHUNK:2 * CHUNK])).astype(BF16)
            dec = jnp.exp(gcs[CHUNK - 1:CHUNK, :])
            ql = [q.astype(BF16)]
            kl = [k.astype(BF16)]
            mxu_block = 2
            for l in range(N_LEVELS):
                s = CHUNK >> (l + 1)
                if s >= MIN_VPU_LEVEL:
                    el = _level_exponent(gcs, s)
                else:
                    el = e[mxu_block * CHUNK:(mxu_block + 1) * CHUNK]
                    mxu_block += 1
                w = jnp.exp(el)
                ql.append((q * w).astype(BF16))
                kl.append((k * w).astype(BF16))
            ops.append((qg, kd, dec, ql, kl))
        heads = []
        for g in range(n_groups):
            qg, kd, dec, ql, kl = ops[g]
            for j in range(hp):
                lm = None if hp == 1 else (lane >= j * dk) & (lane < (j + 1) * dk)
                pick = (lambda a: a) if lm is None else (lambda a, lm=lm: jnp.where(lm, a, jnp.zeros_like(a)))
                a = _dot_nt(pick(ql[0]), kl[0]) * mk_ref[0]
                for l in range(1, N_LEVELS + 1):
                    a = a + _dot_nt(pick(ql[l]), kl[l]) * mk_ref[l]
                heads.append((g, j, lm, pick, a.astype(BF16)))
        outs, upds = [], [None] * n_groups
        sts = [st_ref[g] for g in range(n_groups)]
        stbs = [st.astype(BF16) for st in sts]
        for g, j, lm, pick, ab in heads:
            h = g * hp + j
            hs = slice(h * LANES, (h + 1) * LANES)
            qg, kd = ops[g][0], ops[g][1]
            vh = v_ref[rows, hs].astype(BF16)
            outs.append(_dot(ab, vh) + _dot_nt(pick(qg), stbs[g]))
            u = _dot_tn(vh, kd)
            upds[g] = u if upds[g] is None else jnp.where(lm, u, upds[g])
        for g in range(n_groups):
            st_ref[g] = sts[g] * ops[g][2] + upds[g]
        for h, o in enumerate(outs):
            hs = slice(h * LANES, (h + 1) * LANES)
            _norm_gate_store(y_ref, rows, h * LANES, [o], nw_ref, [z_ref[rows, hs]])
        return carry

    lax.fori_loop(0, tile // CHUNK, body, 0)


def _mixer_call(kernel, n_batch, seq, tile, tok_inputs, const_inputs, scratch, name):
    nt = seq // tile
    row = lambda b, t: (b * nt + t, 0)
    in_specs = [pl.BlockSpec((tile, a.shape[1]), row) for a in tok_inputs]
    for a in const_inputs:
        in_specs.append(pl.BlockSpec(a.shape, lambda b, t, nd=a.ndim: (0,) * nd))
    return pl.pallas_call(
        kernel, grid=(n_batch, nt), in_specs=in_specs,
        out_specs=pl.BlockSpec((tile, D_BRANCH), row),
        out_shape=jax.ShapeDtypeStruct((n_batch * seq, D_BRANCH), BF16),
        scratch_shapes=scratch,
        compiler_params=pltpu.CompilerParams(dimension_semantics=("arbitrary", "arbitrary"),
                                             vmem_limit_bytes=VMEM_LIMIT),
        name=name,
    )(*tok_inputs, *const_inputs)


def _gla_kernel(pa_ref, sm_ref, gw_ref, gb_ref, nw_ref, ms_ref, mk_ref, y_ref,
                q_s, lg_s, st_s, *, tile):
    @pl.when(pl.program_id(1) == 0)
    def _():
        st_s[...] = jnp.zeros_like(st_s)

    q_s[...] = pa_ref[:, 0:GLA_QK] * (GLA_DK ** -0.5)
    x = _dot_f32(sm_ref[...], gw_ref[...]) + gb_ref[...]
    lg_s[...] = _log_sigmoid(x) * (1.0 / GLA_GATE_NORM)
    _gla_chunks(q_s, pa_ref.at[:, GLA_QK:2 * GLA_QK], lg_s,
                pa_ref.at[:, 2 * GLA_QK:2 * GLA_QK + D_BRANCH],
                pa_ref.at[:, 2 * GLA_QK + D_BRANCH:2 * GLA_QK + 2 * D_BRANCH],
                y_ref, st_s, nw_ref, ms_ref, mk_ref, dk=GLA_DK, tile=tile)


def _hgrn_kernel(pc_ref, lb_ref, nw_ref, ms_ref, mk_ref, y_ref, q_s, k_s, lg_s, st_s, *, tile):
    @pl.when(pl.program_id(1) == 0)
    def _():
        st_s[...] = jnp.zeros_like(st_s)

    lb = lb_ref[...]
    fr = pc_ref[:, HGRN_QF:2 * HGRN_QF]
    f = lb + (1.0 - lb) * jax.nn.sigmoid(fr)
    lg_s[...] = jnp.log(jnp.maximum(f, 1e-30))
    k_s[...] = (1.0 - lb) * jax.nn.sigmoid(-fr)
    q_s[...] = pc_ref[:, 0:HGRN_QF] * (HGRN_DK ** -0.5)
    _gla_chunks(q_s, k_s, lg_s,
                pc_ref.at[:, 2 * HGRN_QF:2 * HGRN_QF + D_BRANCH],
                pc_ref.at[:, 2 * HGRN_QF + D_BRANCH:2 * HGRN_QF + 2 * D_BRANCH],
                y_ref, st_s, nw_ref, ms_ref, mk_ref, dk=HGRN_DK, tile=tile)


def _mlstm_kernel(pb_ref, sm_ref, cw_ref, cb_ref, gbias_ref, nw_ref, tri_ref, sel_i_ref, sel_f_ref,
                  y_ref, xe_s, hist_s, qk_s, g_s, c_s, n_s, m_s, *, tile):
    @pl.when(pl.program_id(1) == 0)
    def _():
        hist_s[...] = jnp.zeros_like(hist_s)
        c_s[...] = jnp.zeros_like(c_s)
        n_s[...] = jnp.zeros_like(n_s)
        m_s[...] = jnp.zeros_like(m_s)

    _causal_conv_silu(pb_ref, 2 * D_BRANCH, xe_s, hist_s, cw_ref, cb_ref, qk_s, tile, MLSTM_CONV)
    lane = _lane_iota()
    gates = sm_ref[...] + gbias_ref[...]
    is_f = (lane >= SM_F) & (lane < SM_F + MLSTM_HEADS)
    g_s[...] = jnp.where(is_f, _log_sigmoid(gates), gates)

    ri = lax.broadcasted_iota(jnp.int32, (CHUNK, CHUNK), 0)
    ci = lax.broadcasted_iota(jnp.int32, (CHUNK, CHUNK), 1)
    causal = ci <= ri
    v0, o0, z0 = 2 * D_BRANCH, 3 * D_BRANCH, 4 * D_BRANCH

    def body(c, carry):
        rows = pl.ds(pl.multiple_of(c * CHUNK, CHUNK), CHUNK)
        heads = range(MLSTM_HEADS)
        qs = [qk_s[rows, h * LANES:(h + 1) * LANES] for h in heads]
        ks = [qk_s[rows, D_BRANCH + h * LANES:D_BRANCH + (h + 1) * LANES] * (MLSTM_DH ** -0.5)
              for h in heads]
        qbs = [q.astype(BF16) for q in qs]
        vs = [pb_ref[rows, v0 + h * LANES:v0 + (h + 1) * LANES].astype(BF16) for h in heads]
        qk = [_dot_nt(qbs[h], ks[h].astype(BF16)) for h in heads]
        qc = [_dot(qbs[h], c_s[h].astype(BF16)) for h in heads]
        gc = g_s[rows, :]
        bcol = _sel_dot(tri_ref[...], gc)
        i_rows = _sel_dot_nt(sel_i_ref[...], gc)
        b_rows = _sel_dot_nt(sel_f_ref[...], bcol)
        ws, inters, m_rows, kws, cds, m_news = [], [], [], [], [], []
        for h in heads:
            b_c = bcol[:, SM_F + h:SM_F + h + 1]
            i_c = gc[:, SM_I + h:SM_I + h + 1]
            m = m_s[h:h + 1, 0:1]
            logw = jnp.where(causal, b_c - b_rows[h:h + 1, :] + i_rows[h:h + 1, :], NEG_BIG)
            m_inter = b_c + m
            m_row = jnp.maximum(jnp.max(logw, axis=-1, keepdims=True), m_inter)
            ws.append(jnp.exp(logw - m_row))
            inters.append(jnp.exp(m_inter - m_row))
            m_rows.append(m_row)
            b_last = b_c[CHUNK - 1:CHUNK, :]
            lwe = b_last - b_c + i_c
            m_new = jnp.maximum(b_last + m, jnp.max(lwe, axis=0, keepdims=True))
            cds.append(jnp.exp(b_last + m - m_new))
            kws.append(ks[h] * jnp.exp(lwe - m_new))
            m_news.append(m_new)
        ss = [qk[h] * ws[h] for h in heads]
        nums = [_dot(ss[h].astype(BF16), vs[h]) for h in heads]
        cups = [_dot_tn(kws[h].astype(BF16), vs[h]) for h in heads]
        for h in heads:
            nst = n_s[h:h + 1, :]
            num = nums[h] + inters[h] * qc[h]
            den = (jnp.sum(ss[h], axis=-1, keepdims=True)
                   + inters[h] * jnp.sum(qs[h] * nst, axis=-1, keepdims=True))
            hh = num / jnp.maximum(jnp.abs(den), jnp.exp(-m_rows[h]))
            c_s[h] = cds[h] * c_s[h] + cups[h]
            n_s[h:h + 1, :] = cds[h] * nst + jnp.sum(kws[h], axis=0, keepdims=True)
            m_s[h:h + 1, :] = jnp.broadcast_to(m_news[h], (1, LANES))
            og = jax.nn.sigmoid(pb_ref[rows, o0 + h * LANES:o0 + (h + 1) * LANES])
            _norm_gate_store(y_ref, rows, h * LANES, [og * hh], nw_ref,
                             [pb_ref[rows, z0 + h * LANES:z0 + (h + 1) * LANES]])
        return carry

    lax.fori_loop(0, tile // CHUNK, body, 0)


def _ssd_kernel(pd_ref, sm_ref, cw_ref, cb_ref, dtb_ref, alog_ref, dx_ref, nw_ref,
                tri_ref, sel_ref, exp_ref, y_ref, xe_s, hist_s, xbc_s, a_s, dtx_s, st_s, *, tile):
    @pl.when(pl.program_id(1) == 0)
    def _():
        hist_s[...] = jnp.zeros_like(hist_s)
        st_s[...] = jnp.zeros_like(st_s)

    width = D_BRANCH + 2 * SSD_BC
    _causal_conv_silu(pd_ref, width, xe_s, hist_s, cw_ref, cb_ref, xbc_s, tile, SSD_CONV)
    lane = _lane_iota()
    is_dt = (lane >= SM_DT) & (lane < SM_DT + SSD_HEADS)
    dt = jnp.where(is_dt, _softplus(sm_ref[...] + dtb_ref[...]), 0.0)
    a_s[...] = dt * jnp.where(is_dt, -jnp.exp(alog_ref[...]), 0.0)
    dtx_s[...] = _dot_sel(dt, exp_ref[...])

    ri = lax.broadcasted_iota(jnp.int32, (SSD_CHUNK, SSD_CHUNK), 0)
    ci = lax.broadcasted_iota(jnp.int32, (SSD_CHUNK, SSD_CHUNK), 1)
    tril = ci <= ri
    b0, c0, z0 = D_BRANCH, D_BRANCH + SSD_BC, D_BRANCH + 2 * SSD_BC
    heads_per_pair = LANES // SSD_HEAD_DIM
    pairs_per_group = SSD_HEADS // SSD_GROUPS // heads_per_pair

    def body(c, carry):
        rows = pl.ds(pl.multiple_of(c * SSD_CHUNK, SSD_CHUNK), SSD_CHUNK)
        n_pairs = SSD_GROUPS * pairs_per_group
        bgs = [xbc_s[rows, b0 + g * SSD_STATE:b0 + (g + 1) * SSD_STATE] for g in range(SSD_GROUPS)]
        cgbs = [xbc_s[rows, c0 + g * SSD_STATE:c0 + (g + 1) * SSD_STATE].astype(BF16)
                for g in range(SSD_GROUPS)]
        cbs = [_dot_nt(cgbs[g], bgs[g].astype(BF16)) for g in range(SSD_GROUPS)]
        cst = [_dot(cgbs[p // pairs_per_group], st_s[p].astype(BF16)) for p in range(n_pairs)]
        acs = _sel_dot(tri_ref[...], a_s[rows, :])
        acs_x = _dot_sel(acs, exp_ref[...])
        a_rows = _sel_dot_nt(sel_ref[...], acs)
        xss = [xbc_s[rows, p * LANES:(p + 1) * LANES] for p in range(n_pairs)]
        xdts = [xss[p] * dtx_s[rows, p * LANES:(p + 1) * LANES] for p in range(n_pairs)]
        lhs, bds, xhs = [], [], []
        for h in range(SSD_HEADS):
            p, j = divmod(h, heads_per_pair)
            g = p // pairs_per_group
            lm = (lane >= j * SSD_HEAD_DIM) & (lane < (j + 1) * SSD_HEAD_DIM)
            xhs.append(jnp.where(lm, xdts[p], 0.0).astype(BF16))
            a_c = acs[:, SM_DT + h:SM_DT + h + 1]
            lmat = jnp.exp(jnp.where(tril, a_c - a_rows[h:h + 1, :], NEG_BIG))
            lhs.append((cbs[g] * lmat).astype(BF16))
            dcol = jnp.exp(a_c[SSD_CHUNK - 1:SSD_CHUNK, :] - a_c)
            bds.append((bgs[g] * dcol).astype(BF16))
        yds = [_dot(lhs[h], xhs[h]) for h in range(SSD_HEADS)]
        ups = [_dot_tn(bds[h], xhs[h]) for h in range(SSD_HEADS)]
        for g in range(SSD_GROUPS):
            ys = []
            for pp in range(pairs_per_group):
                p = g * pairs_per_group + pp
                ls = slice(p * LANES, (p + 1) * LANES)
                ax = acs_x[:, ls]
                y = cst[p] * jnp.exp(ax) + dx_ref[:, ls] * xss[p]
                upd = None
                for j in range(heads_per_pair):
                    h = p * heads_per_pair + j
                    y = y + yds[h]
                    upd = ups[h] if upd is None else upd + ups[h]
                st_s[p] = st_s[p] * jnp.exp(ax[SSD_CHUNK - 1:SSD_CHUNK, :]) + upd
                ys.append(y * _silu(pd_ref[rows, z0 + p * LANES:z0 + (p + 1) * LANES]))
            _norm_gate_store(y_ref, rows, g * pairs_per_group * LANES, ys, nw_ref, [None] * len(ys))
        return carry

    lax.fori_loop(0, tile // SSD_CHUNK, body, 0)


def _pad_lanes(parts, total=LANES):
    width = sum(p.shape[-1] for p in parts)
    lead = parts[0].shape[:-1]
    return jnp.concatenate(list(parts) + [jnp.zeros(lead + (total - width,), parts[0].dtype)], axis=-1)


def _small_vector(i_part, f_part, dt_part):
    z = jnp.zeros((GLA_GATE_RANK,), F32)
    return _pad_lanes([z, i_part.astype(F32), f_part.astype(F32), dt_part.astype(F32)])[None, :]


def kernel(x, norm_w, w_in, gla_gate_w, gla_gate_b, gla_norm_w, ml_conv_w, ml_conv_b, ml_i_b, ml_f_b,
           ml_norm_w, hg_lb_logits, hg_norm_w, ssd_conv_w, ssd_conv_b, ssd_dt_bias, ssd_A_log, ssd_D,
           ssd_norm_w, w_out, final_norm_w):
    n_batch, seq, _ = x.shape
    depth = w_in.shape[0]
    tile = min(256, seq)
    n_tok = n_batch * seq

    mstack, masks = _gla_constants()
    tri_c, tri_s = _tri(CHUNK), _tri(SSD_CHUNK)
    sel_i, sel_f = _lane_selector(SM_I, MLSTM_HEADS), _lane_selector(SM_F, MLSTM_HEADS)
    sel_dt = _lane_selector(SM_DT, SSD_HEADS)
    expander = _head_expander(SM_DT, SSD_HEADS, SSD_HEAD_DIM)
    zero4 = jnp.zeros((MLSTM_HEADS,), F32)
    zero8 = jnp.zeros((SSD_HEADS,), F32)

    p = jax.nn.softmax(hg_lb_logits.astype(F32), axis=0)
    lower_bounds = jnp.cumsum(p, axis=0) - p[0:1]

    split_at = [int(s) for s in np.cumsum(PROJ_SIZES)[:-1]]
    h = x.reshape(n_tok, D_MODEL)
    row2 = lambda v: v.astype(F32).reshape(1, -1)
    for l in range(depth):
        (a_q, a_k, a_v, a_gr, a_z, b_q, b_k, b_v, b_i, b_f, b_o, b_z,
         c_q, c_f, c_i, c_z, d_x, d_b, d_c, d_dt, d_z) = jnp.split(w_in[l], split_at, axis=-1)
        weights = [
            jnp.concatenate([a_q, a_k, a_v, a_z], axis=-1).astype(BF16),
            jnp.concatenate([b_q, b_k, b_v, b_o, b_z], axis=-1).astype(BF16),
            jnp.concatenate([c_q, c_f, c_i, c_z], axis=-1).astype(BF16),
            jnp.concatenate([d_x, d_b, d_c, d_z], axis=-1).astype(BF16),
            _pad_lanes([a_gr, b_i, b_f, d_dt]).astype(BF16),
        ]
        pa, pb, pc, pd, sm = _inproj(h, row2(norm_w[l]), weights, tile)

        gate_w = jnp.concatenate(
            [gla_gate_w[l].astype(F32), jnp.zeros((LANES - GLA_GATE_RANK, GLA_QK), F32)], axis=0)
        y_a = _mixer_call(
            functools.partial(_gla_kernel, tile=tile), n_batch, seq, tile, [pa, sm],
            [gate_w, row2(gla_gate_b[l]), row2(gla_norm_w[l]), mstack, masks],
            [pltpu.VMEM((tile, GLA_QK), F32), pltpu.VMEM((tile, GLA_QK), F32),
             pltpu.VMEM((GLA_QK // LANES, GLA_DV, LANES), F32)], "gla")

        y_b = _mixer_call(
            functools.partial(_mlstm_kernel, tile=tile), n_batch, seq, tile, [pb, sm],
            [ml_conv_w[l].astype(F32), row2(ml_conv_b[l]),
             _small_vector(ml_i_b[l], ml_f_b[l], zero8), row2(ml_norm_w[l]), tri_c, sel_i, sel_f],
            [pltpu.VMEM((tile + HIST, 2 * D_BRANCH), F32), pltpu.VMEM((HIST, 2 * D_BRANCH), F32),
             pltpu.VMEM((tile, 2 * D_BRANCH), F32), pltpu.VMEM((tile, LANES), F32),
             pltpu.VMEM((MLSTM_HEADS, MLSTM_DH, MLSTM_DH), F32), pltpu.VMEM((8, LANES), F32),
             pltpu.VMEM((8, LANES), F32)], "mlstm")

        y_c = _mixer_call(
            functools.partial(_hgrn_kernel, tile=tile), n_batch, seq, tile, [pc],
            [row2(lower_bounds[l]), row2(hg_norm_w[l]), mstack, masks],
            [pltpu.VMEM((tile, HGRN_QF), F32), pltpu.VMEM((tile, HGRN_QF), F32),
             pltpu.VMEM((tile, HGRN_QF), F32), pltpu.VMEM((HGRN_QF // LANES, HGRN_DV, LANES), F32)],
            "hgrn")

        width = D_BRANCH + 2 * SSD_BC
        y_d = _mixer_call(
            functools.partial(_ssd_kernel, tile=tile), n_batch, seq, tile, [pd, sm],
            [ssd_conv_w[l].astype(F32), row2(ssd_conv_b[l]),
             _small_vector(zero4, zero4, ssd_dt_bias[l]), _small_vector(zero4, zero4, ssd_A_log[l]),
             row2(jnp.repeat(ssd_D[l].astype(F32), SSD_HEAD_DIM)), row2(ssd_norm_w[l]),
             tri_s, sel_dt, expander],
            [pltpu.VMEM((tile + HIST, width), F32), pltpu.VMEM((HIST, width), F32),
             pltpu.VMEM((tile, width), F32), pltpu.VMEM((tile, LANES), F32),
             pltpu.VMEM((tile, D_BRANCH), F32),
             pltpu.VMEM((D_BRANCH // LANES, SSD_STATE, LANES), F32)], "ssd")

        h = _outproj(h, [y_a, y_b, y_c, y_d], w_out[l].astype(BF16), row2(final_norm_w),
                     l == depth - 1, tile)
    return h.reshape(n_batch, seq, D_MODEL)
```

```python
import functools

import numpy as np
import jax
import jax.numpy as jnp
from jax import lax
from jax.experimental import pallas as pl
from jax.experimental.pallas import tpu as pltpu

F32 = jnp.float32
BF16 = jnp.bfloat16

D_MODEL = 1024
D_BRANCH = 512
EPS = 1e-6
NEG_BIG = -1e30

GLA_HEADS, GLA_DK, GLA_DV = 4, 64, 128
GLA_GATE_RANK, GLA_GATE_NORM = 16, 16.0
MLSTM_HEADS, MLSTM_DH, MLSTM_CONV = 4, 128, 4
HGRN_HEADS, HGRN_DK, HGRN_DV = 4, 128, 128
SSD_HEAD_DIM, SSD_HEADS, SSD_GROUPS, SSD_STATE, SSD_CONV = 64, 8, 2, 128, 4
GLA_QK = GLA_HEADS * GLA_DK
HGRN_QF = HGRN_HEADS * HGRN_DK
SSD_BC = SSD_GROUPS * SSD_STATE
PROJ_SIZES = (
    GLA_QK, GLA_QK, D_BRANCH, GLA_GATE_RANK, D_BRANCH,
    D_BRANCH, D_BRANCH, D_BRANCH, MLSTM_HEADS, MLSTM_HEADS, D_BRANCH, D_BRANCH,
    HGRN_QF, HGRN_QF, D_BRANCH, D_BRANCH,
    D_BRANCH, SSD_BC, SSD_BC, SSD_HEADS, D_BRANCH,
)

LANES = 128
HIST = 8
VMEM_LIMIT = 56 * 1024 * 1024

OUT_TILE = 512

CHUNK = 64
SSD_CHUNK = 128
N_LEVELS = 6
MIN_VPU_LEVEL = 4

SM_GR, SM_I, SM_F, SM_DT = 0, 16, 20, 24


def _gla_constants():
    c = CHUNK
    t = np.arange(c)[:, None]
    d = np.arange(c)[None, :]
    blocks = [(d <= t), (d > t)]
    masks = [np.eye(c, dtype=bool)]
    for l in range(N_LEVELS):
        s = c >> (l + 1)
        mid_t = (t // (2 * s)) * (2 * s) + s
        upper = t >= mid_t
        if s < MIN_VPU_LEVEL:
            blocks.append(np.where(upper, (d >= mid_t) & (d <= t), (d > t) & (d <= mid_t - 1)))
        same = (t // (2 * s)) == (d // (2 * s))
        masks.append(same & upper & (d < mid_t))
    mstack = np.concatenate(blocks, axis=0).astype(np.float32)
    mstack = np.concatenate([mstack, mstack], axis=1)
    return jnp.asarray(mstack, BF16), jnp.asarray(np.stack(masks).astype(np.float32))


def _tri(n):
    return jnp.asarray(np.tril(np.ones((n, n), np.float32)), BF16)


def _lane_selector(first_lane, count):
    m = np.zeros((16, LANES), np.float32)
    for r in range(count):
        m[r, first_lane + r] = 1.0
    return jnp.asarray(m, BF16)


def _head_expander(first_lane, heads, width):
    m = np.zeros((LANES, heads * width), np.float32)
    for h in range(heads):
        m[first_lane + h, h * width:(h + 1) * width] = 1.0
    return jnp.asarray(m, BF16)


def _dot(a, b):
    return jnp.dot(a, b, preferred_element_type=F32)


def _dot_nt(a, b):
    return lax.dot_general(a, b, (((1,), (1,)), ((), ())), preferred_element_type=F32)


def _dot_tn(a, b):
    return lax.dot_general(a, b, (((0,), (0,)), ((), ())), preferred_element_type=F32)


def _split3(x):
    hi = x.astype(BF16)
    r1 = x - hi.astype(F32)
    mid = r1.astype(BF16)
    lo = (r1 - mid.astype(F32)).astype(BF16)
    return hi, mid, lo


def _sel_dot(sel, x):
    hi, mid, lo = _split3(x)
    return _dot(sel, hi) + _dot(sel, mid) + _dot(sel, lo)


def _sel_dot_nt(sel, x):
    hi, mid, lo = _split3(x)
    return _dot_nt(sel, hi) + _dot_nt(sel, mid) + _dot_nt(sel, lo)


def _sel_dot2(sel2, x):
    hi = x.astype(BF16)
    mid = (x - hi.astype(F32)).astype(BF16)
    return _dot(sel2, jnp.concatenate([hi, mid], axis=0))


def _level_exponent(gcs, s):
    pieces = []
    for b in range(0, CHUNK, 2 * s):
        ref_row = gcs[b + s - 1:b + s, :]
        if s >= 8:
            pieces += [ref_row - gcs[b:b + s], gcs[b + s:b + 2 * s] - ref_row]
        else:
            diff = gcs[b:b + 2 * s] - ref_row
            pieces.append(jnp.minimum(diff, -diff))
    return jnp.concatenate(pieces, axis=0)


def _dot_sel(x, sel):
    hi, mid, lo = _split3(x)
    return _dot(hi, sel) + _dot(mid, sel) + _dot(lo, sel)


def _dot_f32(a, b):
    ah = a.astype(BF16)
    al = (a - ah.astype(F32)).astype(BF16)
    bh = b.astype(BF16)
    bl = (b - bh.astype(F32)).astype(BF16)
    return _dot(ah, bh) + _dot(al, bh) + _dot(ah, bl)


def _softplus(x):
    return jnp.maximum(x, 0.0) + jnp.log1p(jnp.exp(-jnp.abs(x)))


def _log_sigmoid(x):
    return -_softplus(-x)


def _silu(x):
    return x * jax.nn.sigmoid(x)


def _lane_iota():
    return lax.broadcasted_iota(jnp.int32, (1, LANES), 1)


def _inproj_kernel(x_ref, nw_ref, wa_ref, wb_ref, wc_ref, wd_ref, ws_ref,
                   oa_ref, ob_ref, oc_ref, od_ref, os_ref):
    x = x_ref[...]
    ms = jnp.mean(x * x, axis=-1, keepdims=True)
    u = (x * lax.rsqrt(ms + EPS) * nw_ref[...]).astype(BF16)
    for w_ref, o_ref in ((wa_ref, oa_ref), (wb_ref, ob_ref), (wc_ref, oc_ref),
                         (wd_ref, od_ref), (ws_ref, os_ref)):
        o_ref[...] = _dot(u, w_ref[...])


def _inproj(h, norm_w, weights, tile):
    n_tok = h.shape[0]
    grid = (n_tok // tile,)
    const = lambda i: (0, 0)
    row = lambda i: (i, 0)
    in_specs = [pl.BlockSpec((tile, D_MODEL), row), pl.BlockSpec((1, D_MODEL), const)]
    in_specs += [pl.BlockSpec(w.shape, const, pipeline_mode=pl.Buffered(1)) for w in weights]
    out_specs = [pl.BlockSpec((tile, w.shape[1]), row) for w in weights]
    out_shape = [jax.ShapeDtypeStruct((n_tok, w.shape[1]), F32) for w in weights]
    return pl.pallas_call(
        _inproj_kernel, grid=grid, in_specs=in_specs, out_specs=out_specs, out_shape=out_shape,
        compiler_params=pltpu.CompilerParams(dimension_semantics=("arbitrary",),
                                             vmem_limit_bytes=VMEM_LIMIT),
        name="inproj",
    )(h, norm_w, *weights)


def _outproj_kernel(h_ref, ya_ref, yb_ref, yc_ref, yd_ref, w_ref, fw_ref, o_ref, *, final):
    acc = h_ref[...]
    for i, y_ref in enumerate((ya_ref, yb_ref, yc_ref, yd_ref)):
        acc = acc + _dot(y_ref[...], w_ref[i * D_BRANCH:(i + 1) * D_BRANCH, :])
    if final:
        ms = jnp.mean(acc * acc, axis=-1, keepdims=True)
        acc = acc * lax.rsqrt(ms + EPS) * fw_ref[...]
    o_ref[...] = acc


def _outproj(h, ys, w_out, final_w, final, tile):
    n_tok = h.shape[0]
    const = lambda i: (0, 0)
    row = lambda i: (i, 0)
    in_specs = [pl.BlockSpec((tile, D_MODEL), row)]
    in_specs += [pl.BlockSpec((tile, D_BRANCH), row) for _ in ys]
    in_specs += [pl.BlockSpec(w_out.shape, const), pl.BlockSpec((1, D_MODEL), const)]
    return pl.pallas_call(
        functools.partial(_outproj_kernel, final=final),
        grid=(n_tok // tile,), in_specs=in_specs,
        out_specs=pl.BlockSpec((tile, D_MODEL), row),
        out_shape=jax.ShapeDtypeStruct((n_tok, D_MODEL), F32),
        compiler_params=pltpu.CompilerParams(dimension_semantics=("arbitrary",),
                                             vmem_limit_bytes=VMEM_LIMIT),
        name="outproj",
    )(h, *ys, w_out, final_w)


def _norm_gate_store(y_ref, rows, col0, parts, nw_ref, z_parts):
    width = sum(p.shape[-1] for p in parts)
    ss = sum(jnp.sum(p * p, axis=-1, keepdims=True) for p in parts)
    scale = lax.rsqrt(ss * (1.0 / width) + EPS)
    c = col0
    for p, z in zip(parts, z_parts):
        w = p.shape[-1]
        out = p * scale * nw_ref[:, c:c + w]
        if z is not None:
            out = out * _silu(z)
        y_ref[rows, c:c + w] = out.astype(y_ref.dtype)
        c += w


def _causal_conv_silu(src_ref, width, xe_ref, hist_ref, w_ref, b_ref, dst_ref, tile, taps):
    xe_ref[0:HIST, :] = hist_ref[...]
    xe_ref[HIST:HIST + tile, :] = src_ref[:, 0:width]
    hist_ref[...] = src_ref[tile - HIST:tile, 0:width]
    blk = 64
    for r in range(0, tile, blk):
        acc = b_ref[...] + w_ref[taps - 1:taps, :] * xe_ref[HIST + r:HIST + r + blk, :]
        for k in range(taps - 1):
            off = HIST - (taps - 1) + k
            acc = acc + w_ref[k:k + 1, :] * xe_ref[off + r:off + r + blk, :]
        dst_ref[r:r + blk, :] = _silu(acc)


def _run_stages(streams, stages, rows):
    for stage in stages:
        for d in streams:
            stage(d, rows)


def _gla_chunks(streams, nw_ref, ms_ref, mk_ref, *, dk, tile):
    n_groups = streams[0]["q"].shape[1] // LANES
    hp = LANES // dk
    lane = _lane_iota()
    lane_masks = [None if hp == 1 else (lane >= j * dk) & (lane < (j + 1) * dk) for j in range(hp)]

    def pick(a, lm):
        return a if lm is None else jnp.where(lm, a, jnp.zeros_like(a))

    def cumsums(d, rows):
        d["es"] = [_sel_dot2(ms_ref[...], d["lg"][rows, p * 2 * LANES:(p + 1) * 2 * LANES])
                   for p in range(n_groups // 2)]

    def operands(d, rows):
        ops = []
        for g in range(n_groups):
            ls = slice(g * LANES, (g + 1) * LANES)
            e = d["es"][g // 2][:, (g % 2) * LANES:(g % 2 + 1) * LANES]
            q = d["q"][rows, ls]
            k = d["k"][rows, ls]
            gcs = e[0:CHUNK]
            qg = (q * jnp.exp(gcs)).astype(BF16)
            kd = (k * jnp.exp(e[CHUNK:2 * CHUNK])).astype(BF16)
            dec = jnp.exp(gcs[CHUNK - 1:CHUNK, :])
            ql = [q.astype(BF16)]
            kl = [k.astype(BF16)]
            mxu_block = 2
            for l in range(N_LEVELS):
                s = CHUNK >> (l + 1)
                if s >= MIN_VPU_LEVEL:
                    el = _level_exponent(gcs, s)
                else:
                    el = e[mxu_block * CHUNK:(mxu_block + 1) * CHUNK]
                    mxu_block += 1
                w = jnp.exp(el)
                ql.append((q * w).astype(BF16))
                kl.append((k * w).astype(BF16))
            ops.append((qg, kd, dec, ql, kl))
        d["ops"] = ops

    def scores(d, rows):
        heads = []
        for g in range(n_groups):
            ql, kl = d["ops"][g][3], d["ops"][g][4]
            for j in range(hp):
                lm = lane_masks[j]
                a = _dot_nt(pick(ql[0], lm), kl[0]) * mk_ref[0]
                for l in range(1, N_LEVELS + 1):
                    a = a + _dot_nt(pick(ql[l], lm), kl[l]) * mk_ref[l]
                heads.append((g, j, a.astype(BF16)))
        d["heads"] = heads

    def outputs(d, rows):
        outs, upds = [], [None] * n_groups
        sts = [d["st"][g] for g in range(n_groups)]
        stbs = [st.astype(BF16) for st in sts]
        for g, j, ab in d["heads"]:
            h = g * hp + j
            lm = lane_masks[j]
            qg, kd = d["ops"][g][0], d["ops"][g][1]
            vh = d["v"][rows, h * LANES:(h + 1) * LANES].astype(BF16)
            outs.append(_dot(ab, vh) + _dot_nt(pick(qg, lm), stbs[g]))
            u = _dot_tn(vh, kd)
            upds[g] = u if upds[g] is None else jnp.where(lm, u, upds[g])
        d["outs"], d["upds"], d["sts"] = outs, upds, sts

    def finish(d, rows):
        for g in range(n_groups):
            d["st"][g] = d["sts"][g] * d["ops"][g][2] + d["upds"][g]
        for h, o in enumerate(d["outs"]):
            _norm_gate_store(d["y"], rows, h * LANES, [o], nw_ref,
                             [d["z"][rows, h * LANES:(h + 1) * LANES]])

    def body(c, carry):
        rows = pl.ds(pl.multiple_of(c * CHUNK, CHUNK), CHUNK)
        _run_stages([dict(d) for d in streams], (cumsums, operands, scores, outputs, finish), rows)
        return carry

    lax.fori_loop(0, tile // CHUNK, body, 0)


def _mixer_call(kernel, n_batch, seq, tile, tok_inputs, const_inputs, scratch, name):
    tok = lambda t: (0, t, 0)
    in_specs = [pl.BlockSpec((n_batch, tile, a.shape[2]), tok) for a in tok_inputs]
    for a in const_inputs:
        in_specs.append(pl.BlockSpec(a.shape, lambda t, nd=a.ndim: (0,) * nd))
    return pl.pallas_call(
        kernel, grid=(seq // tile,), in_specs=in_specs,
        out_specs=pl.BlockSpec((n_batch, tile, D_BRANCH), tok),
        out_shape=jax.ShapeDtypeStruct((n_batch, seq, D_BRANCH), BF16),
        scratch_shapes=scratch,
        compiler_params=pltpu.CompilerParams(dimension_semantics=("arbitrary",),
                                             vmem_limit_bytes=VMEM_LIMIT),
        name=name,
    )(*tok_inputs, *const_inputs)


def _zero_at_start(*refs):
    @pl.when(pl.program_id(0) == 0)
    def _():
        for r in refs:
            r[...] = jnp.zeros_like(r)


def _gla_kernel(pa_ref, sm_ref, gw_ref, gb_ref, nw_ref, ms_ref, mk_ref, y_ref,
                q_s, lg_s, st_s, *, tile):
    _zero_at_start(st_s)
    k0, v0, z0 = GLA_QK, 2 * GLA_QK, 2 * GLA_QK + D_BRANCH
    streams = []
    for s in range(pa_ref.shape[0]):
        q_s[s] = pa_ref[s, :, 0:GLA_QK] * (GLA_DK ** -0.5)
        x = _dot_f32(sm_ref[s], gw_ref[...]) + gb_ref[...]
        lg_s[s] = _log_sigmoid(x) * (1.0 / GLA_GATE_NORM)
        streams.append(dict(q=q_s.at[s], k=pa_ref.at[s, :, k0:v0], lg=lg_s.at[s],
                            v=pa_ref.at[s, :, v0:z0], z=pa_ref.at[s, :, z0:z0 + D_BRANCH],
                            y=y_ref.at[s], st=st_s.at[s]))
    _gla_chunks(streams, nw_ref, ms_ref, mk_ref, dk=GLA_DK, tile=tile)


def _hgrn_kernel(pc_ref, lb_ref, nw_ref, ms_ref, mk_ref, y_ref, q_s, k_s, lg_s, st_s, *, tile):
    _zero_at_start(st_s)
    v0, z0 = 2 * HGRN_QF, 2 * HGRN_QF + D_BRANCH
    lb = lb_ref[...]
    streams = []
    for s in range(pc_ref.shape[0]):
        fr = pc_ref[s, :, HGRN_QF:2 * HGRN_QF]
        f = lb + (1.0 - lb) * jax.nn.sigmoid(fr)
        lg_s[s] = jnp.log(jnp.maximum(f, 1e-30))
        k_s[s] = (1.0 - lb) * jax.nn.sigmoid(-fr)
        q_s[s] = pc_ref[s, :, 0:HGRN_QF] * (HGRN_DK ** -0.5)
        streams.append(dict(q=q_s.at[s], k=k_s.at[s], lg=lg_s.at[s],
                            v=pc_ref.at[s, :, v0:z0], z=pc_ref.at[s, :, z0:z0 + D_BRANCH],
                            y=y_ref.at[s], st=st_s.at[s]))
    _gla_chunks(streams, nw_ref, ms_ref, mk_ref, dk=HGRN_DK, tile=tile)


def _mlstm_kernel(pb_ref, sm_ref, cw_ref, cb_ref, gbias_ref, nw_ref, tri_ref, sel_i_ref, sel_f_ref,
                  y_ref, xe_s, hist_s, qk_s, g_s, c_s, n_s, m_s, *, tile):
    _zero_at_start(hist_s, c_s, n_s, m_s)
    lane = _lane_iota()
    is_f = (lane >= SM_F) & (lane < SM_F + MLSTM_HEADS)
    streams = []
    for s in range(pb_ref.shape[0]):
        _causal_conv_silu(pb_ref.at[s], 2 * D_BRANCH, xe_s, hist_s.at[s], cw_ref, cb_ref,
                          qk_s.at[s], tile, MLSTM_CONV)
        gates = sm_ref[s] + gbias_ref[...]
        g_s[s] = jnp.where(is_f, _log_sigmoid(gates), gates)
        streams.append(dict(p=pb_ref.at[s], qk=qk_s.at[s], g=g_s.at[s], c=c_s.at[s], n=n_s.at[s],
                            m=m_s.at[s], y=y_ref.at[s]))

    ri = lax.broadcasted_iota(jnp.int32, (CHUNK, CHUNK), 0)
    ci = lax.broadcasted_iota(jnp.int32, (CHUNK, CHUNK), 1)
    causal = ci <= ri
    v0, o0, z0 = 2 * D_BRANCH, 3 * D_BRANCH, 4 * D_BRANCH
    heads = range(MLSTM_HEADS)

    def qk_products(d, rows):
        qk_ref = d["qk"]
        d["qs"] = [qk_ref[rows, h * LANES:(h + 1) * LANES] for h in heads]
        d["ks"] = [qk_ref[rows, D_BRANCH + h * LANES:D_BRANCH + (h + 1) * LANES] * (MLSTM_DH ** -0.5)
                   for h in heads]
        qbs = [q.astype(BF16) for q in d["qs"]]
        d["vs"] = [d["p"][rows, v0 + h * LANES:v0 + (h + 1) * LANES].astype(BF16) for h in heads]
        d["qk_raw"] = [_dot_nt(qbs[h], d["ks"][h].astype(BF16)) for h in heads]
        d["qc"] = [_dot(qbs[h], d["c"][h].astype(BF16)) for h in heads]

    def gate_sums(d, rows):
        gc = d["g"][rows, :]
        bcol = _sel_dot(tri_ref[...], gc)
        d["gc"], d["bcol"] = gc, bcol
        d["i_rows"] = _sel_dot_nt(sel_i_ref[...], gc)
        d["b_rows"] = _sel_dot_nt(sel_f_ref[...], bcol)

    def weights(d, rows):
        ws, inters, m_rows, kws, cds, m_news = [], [], [], [], [], []
        for h in heads:
            b_c = d["bcol"][:, SM_F + h:SM_F + h + 1]
            i_c = d["gc"][:, SM_I + h:SM_I + h + 1]
            m = d["m"][h:h + 1, 0:1]
            logw = jnp.where(causal, b_c - d["b_rows"][h:h + 1, :] + d["i_rows"][h:h + 1, :], NEG_BIG)
            m_inter = b_c + m
            m_row = jnp.maximum(jnp.max(logw, axis=-1, keepdims=True), m_inter)
            ws.append(jnp.exp(logw - m_row))
            inters.append(jnp.exp(m_inter - m_row))
            m_rows.append(m_row)
            b_last = b_c[CHUNK - 1:CHUNK, :]
            lwe = b_last - b_c + i_c
            m_new = jnp.maximum(b_last + m, jnp.max(lwe, axis=0, keepdims=True))
            cds.append(jnp.exp(b_last + m - m_new))
            kws.append(d["ks"][h] * jnp.exp(lwe - m_new))
            m_news.append(m_new)
        d.update(ws=ws, inters=inters, m_rows=m_rows, kws=kws, cds=cds, m_news=m_news)

    def numerators(d, rows):
        d["ss"] = [d["qk_raw"][h] * d["ws"][h] for h in heads]
        d["nums"] = [_dot(d["ss"][h].astype(BF16), d["vs"][h]) for h in heads]
        d["cups"] = [_dot_tn(d["kws"][h].astype(BF16), d["vs"][h]) for h in heads]

    def finish(d, rows):
        for h in heads:
            nst = d["n"][h:h + 1, :]
            inter = d["inters"][h]
            num = d["nums"][h] + inter * d["qc"][h]
            den = (jnp.sum(d["ss"][h], axis=-1, keepdims=True)
                   + inter * jnp.sum(d["qs"][h] * nst, axis=-1, keepdims=True))
            hh = num / jnp.maximum(jnp.abs(den), jnp.exp(-d["m_rows"][h]))
            cd = d["cds"][h]
            d["c"][h] = cd * d["c"][h] + d["cups"][h]
            d["n"][h:h + 1, :] = cd * nst + jnp.sum(d["kws"][h], axis=0, keepdims=True)
            d["m"][h:h + 1, :] = jnp.broadcast_to(d["m_news"][h], (1, LANES))
            og = jax.nn.sigmoid(d["p"][rows, o0 + h * LANES:o0 + (h + 1) * LANES])
            _norm_gate_store(d["y"], rows, h * LANES, [og * hh], nw_ref,
                             [d["p"][rows, z0 + h * LANES:z0 + (h + 1) * LANES]])

    def body(c, carry):
        rows = pl.ds(pl.multiple_of(c * CHUNK, CHUNK), CHUNK)
        for d in streams:
            _run_stages([dict(d)], (qk_products, gate_sums, weights, numerators, finish), rows)
        return carry

    lax.fori_loop(0, tile // CHUNK, body, 0)


def _ssd_kernel(pd_ref, sm_ref, cw_ref, cb_ref, dtb_ref, alog_ref, dx_ref, nw_ref,
                tri_ref, sel_ref, exp_ref, y_ref, xe_s, hist_s, xbc_s, a_s, dtx_s, st_s, *, tile):
    _zero_at_start(hist_s, st_s)
    width = D_BRANCH + 2 * SSD_BC
    lane = _lane_iota()
    is_dt = (lane >= SM_DT) & (lane < SM_DT + SSD_HEADS)
    streams = []
    for s in range(pd_ref.shape[0]):
        _causal_conv_silu(pd_ref.at[s], width, xe_s, hist_s.at[s], cw_ref, cb_ref, xbc_s.at[s],
                          tile, SSD_CONV)
        dt = jnp.where(is_dt, _softplus(sm_ref[s] + dtb_ref[...]), 0.0)
        a_s[s] = dt * jnp.where(is_dt, -jnp.exp(alog_ref[...]), 0.0)
        dtx_s[s] = _dot_sel(dt, exp_ref[...])
        streams.append(dict(p=pd_ref.at[s], xbc=xbc_s.at[s], a=a_s.at[s], dtx=dtx_s.at[s],
                            st=st_s.at[s], y=y_ref.at[s]))

    ri = lax.broadcasted_iota(jnp.int32, (SSD_CHUNK, SSD_CHUNK), 0)
    ci = lax.broadcasted_iota(jnp.int32, (SSD_CHUNK, SSD_CHUNK), 1)
    tril = ci <= ri
    b0, c0, z0 = D_BRANCH, D_BRANCH + SSD_BC, D_BRANCH + 2 * SSD_BC
    heads_per_pair = LANES // SSD_HEAD_DIM
    pairs_per_group = SSD_HEADS // SSD_GROUPS // heads_per_pair
    n_pairs = SSD_GROUPS * pairs_per_group
    groups = range(SSD_GROUPS)

    def products(d, rows):
        xbc = d["xbc"]
        d["bgs"] = [xbc[rows, b0 + g * SSD_STATE:b0 + (g + 1) * SSD_STATE] for g in groups]
        cgbs = [xbc[rows, c0 + g * SSD_STATE:c0 + (g + 1) * SSD_STATE].astype(BF16) for g in groups]
        d["cbs"] = [_dot_nt(cgbs[g], d["bgs"][g].astype(BF16)) for g in groups]
        d["cst"] = [_dot(cgbs[p // pairs_per_group], d["st"][p].astype(BF16)) for p in range(n_pairs)]

    def decay_sums(d, rows):
        acs = _sel_dot(tri_ref[...], d["a"][rows, :])
        d["acs"] = acs
        d["acs_x"] = _dot_sel(acs, exp_ref[...])
        d["a_rows"] = _sel_dot_nt(sel_ref[...], acs)

    def decays(d, rows):
        d["xss"] = [d["xbc"][rows, p * LANES:(p + 1) * LANES] for p in range(n_pairs)]
        xdts = [d["xss"][p] * d["dtx"][rows, p * LANES:(p + 1) * LANES] for p in range(n_pairs)]
        lhs, bds, xhs = [], [], []
        for h in range(SSD_HEADS):
            p, j = divmod(h, heads_per_pair)
            g = p // pairs_per_group
            lm = (lane >= j * SSD_HEAD_DIM) & (lane < (j + 1) * SSD_HEAD_DIM)
            xhs.append(jnp.where(lm, xdts[p], 0.0).astype(BF16))
            a_c = d["acs"][:, SM_DT + h:SM_DT + h + 1]
            lmat = jnp.exp(jnp.where(tril, a_c - d["a_rows"][h:h + 1, :], NEG_BIG))
            lhs.append((d["cbs"][g] * lmat).astype(BF16))
            dcol = jnp.exp(a_c[SSD_CHUNK - 1:SSD_CHUNK, :] - a_c)
            bds.append((d["bgs"][g] * dcol).astype(BF16))
        d.update(lhs=lhs, bds=bds, xhs=xhs)

    def chunk_products(d, rows):
        d["yds"] = [_dot(d["lhs"][h], d["xhs"][h]) for h in range(SSD_HEADS)]
        d["ups"] = [_dot_tn(d["bds"][h], d["xhs"][h]) for h in range(SSD_HEADS)]

    def finish(d, rows):
        for g in groups:
            ys = []
            for pp in range(pairs_per_group):
                p = g * pairs_per_group + pp
                ls = slice(p * LANES, (p + 1) * LANES)
                ax = d["acs_x"][:, ls]
                y = d["cst"][p] * jnp.exp(ax) + dx_ref[:, ls] * d["xss"][p]
                upd = None
                for j in range(heads_per_pair):
                    h = p * heads_per_pair + j
                    y = y + d["yds"][h]
                    upd = d["ups"][h] if upd is None else upd + d["ups"][h]
                d["st"][p] = d["st"][p] * jnp.exp(ax[SSD_CHUNK - 1:SSD_CHUNK, :]) + upd
                ys.append(y * _silu(d["p"][rows, z0 + p * LANES:z0 + (p + 1) * LANES]))
            _norm_gate_store(d["y"], rows, g * pairs_per_group * LANES, ys, nw_ref, [None] * len(ys))

    def body(c, carry):
        rows = pl.ds(pl.multiple_of(c * SSD_CHUNK, SSD_CHUNK), SSD_CHUNK)
        _run_stages([dict(d) for d in streams],
                    (products, decay_sums, decays, chunk_products, finish), rows)
        return carry

    lax.fori_loop(0, tile // SSD_CHUNK, body, 0)


def _pad_lanes(parts, total=LANES):
    width = sum(p.shape[-1] for p in parts)
    lead = parts[0].shape[:-1]
    return jnp.concatenate(list(parts) + [jnp.zeros(lead + (total - width,), parts[0].dtype)], axis=-1)


def _small_vector(i_part, f_part, dt_part):
    z = jnp.zeros((GLA_GATE_RANK,), F32)
    return _pad_lanes([z, i_part.astype(F32), f_part.astype(F32), dt_part.astype(F32)])[None, :]


def kernel(x, norm_w, w_in, gla_gate_w, gla_gate_b, gla_norm_w, ml_conv_w, ml_conv_b, ml_i_b, ml_f_b,
           ml_norm_w, hg_lb_logits, hg_norm_w, ssd_conv_w, ssd_conv_b, ssd_dt_bias, ssd_A_log, ssd_D,
           ssd_norm_w, w_out, final_norm_w):
    n_batch, seq, _ = x.shape
    depth = w_in.shape[0]
    tile = min(256, seq)
    n_tok = n_batch * seq

    mstack, masks = _gla_constants()
    tri_c, tri_s = _tri(CHUNK), _tri(SSD_CHUNK)
    sel_i, sel_f = _lane_selector(SM_I, MLSTM_HEADS), _lane_selector(SM_F, MLSTM_HEADS)
    sel_dt = _lane_selector(SM_DT, SSD_HEADS)
    expander = _head_expander(SM_DT, SSD_HEADS, SSD_HEAD_DIM)
    zero4 = jnp.zeros((MLSTM_HEADS,), F32)
    zero8 = jnp.zeros((SSD_HEADS,), F32)

    p = jax.nn.softmax(hg_lb_logits.astype(F32), axis=0)
    lower_bounds = jnp.cumsum(p, axis=0) - p[0:1]

    split_at = [int(s) for s in np.cumsum(PROJ_SIZES)[:-1]]
    h = x.reshape(n_tok, D_MODEL)
    row2 = lambda v: v.astype(F32).reshape(1, -1)
    for l in range(depth):
        (a_q, a_k, a_v, a_gr, a_z, b_q, b_k, b_v, b_i, b_f, b_o, b_z,
         c_q, c_f, c_i, c_z, d_x, d_b, d_c, d_dt, d_z) = jnp.split(w_in[l], split_at, axis=-1)
        weights = [
            jnp.concatenate([a_q, a_k, a_v, a_z], axis=-1).astype(BF16),
            jnp.concatenate([b_q, b_k, b_v, b_o, b_z], axis=-1).astype(BF16),
            jnp.concatenate([c_q, c_f, c_i, c_z], axis=-1).astype(BF16),
            jnp.concatenate([d_x, d_b, d_c, d_z], axis=-1).astype(BF16),
            _pad_lanes([a_gr, b_i, b_f, d_dt]).astype(BF16),
        ]
        pa, pb, pc, pd, sm = [p.reshape(n_batch, seq, -1)
                              for p in _inproj(h, row2(norm_w[l]), weights, tile)]

        gate_w = jnp.concatenate(
            [gla_gate_w[l].astype(F32), jnp.zeros((LANES - GLA_GATE_RANK, GLA_QK), F32)], axis=0)
        y_a = _mixer_call(
            functools.partial(_gla_kernel, tile=tile), n_batch, seq, tile, [pa, sm],
            [gate_w, row2(gla_gate_b[l]), row2(gla_norm_w[l]), mstack, masks],
            [pltpu.VMEM((n_batch, tile, GLA_QK), F32), pltpu.VMEM((n_batch, tile, GLA_QK), F32),
             pltpu.VMEM((n_batch, GLA_QK // LANES, GLA_DV, LANES), F32)], "gla")

        y_b = _mixer_call(
            functools.partial(_mlstm_kernel, tile=tile), n_batch, seq, tile, [pb, sm],
            [ml_conv_w[l].astype(F32), row2(ml_conv_b[l]),
             _small_vector(ml_i_b[l], ml_f_b[l], zero8), row2(ml_norm_w[l]), tri_c, sel_i, sel_f],
            [pltpu.VMEM((tile + HIST, 2 * D_BRANCH), F32), pltpu.VMEM((n_batch, HIST, 2 * D_BRANCH), F32),
             pltpu.VMEM((n_batch, tile, 2 * D_BRANCH), F32), pltpu.VMEM((n_batch, tile, LANES), F32),
             pltpu.VMEM((n_batch, MLSTM_HEADS, MLSTM_DH, MLSTM_DH), F32),
             pltpu.VMEM((n_batch, 8, LANES), F32), pltpu.VMEM((n_batch, 8, LANES), F32)], "mlstm")

        y_c = _mixer_call(
            functools.partial(_hgrn_kernel, tile=tile), n_batch, seq, tile, [pc],
            [row2(lower_bounds[l]), row2(hg_norm_w[l]), mstack, masks],
            [pltpu.VMEM((n_batch, tile, HGRN_QF), F32), pltpu.VMEM((n_batch, tile, HGRN_QF), F32),
             pltpu.VMEM((n_batch, tile, HGRN_QF), F32),
             pltpu.VMEM((n_batch, HGRN_QF // LANES, HGRN_DV, LANES), F32)], "hgrn")

        width = D_BRANCH + 2 * SSD_BC
        y_d = _mixer_call(
            functools.partial(_ssd_kernel, tile=tile), n_batch, seq, tile, [pd, sm],
            [ssd_conv_w[l].astype(F32), row2(ssd_conv_b[l]),
             _small_vector(zero4, zero4, ssd_dt_bias[l]), _small_vector(zero4, zero4, ssd_A_log[l]),
             row2(jnp.repeat(ssd_D[l].astype(F32), SSD_HEAD_DIM)), row2(ssd_norm_w[l]),
             tri_s, sel_dt, expander],
            [pltpu.VMEM((tile + HIST, width), F32), pltpu.VMEM((n_batch, HIST, width), F32),
             pltpu.VMEM((n_batch, tile, width), F32), pltpu.VMEM((n_batch, tile, LANES), F32),
             pltpu.VMEM((n_batch, tile, D_BRANCH), F32),
             pltpu.VMEM((n_batch, D_BRANCH // LANES, SSD_STATE, LANES), F32)], "ssd")

        ys = [y.reshape(n_tok, D_BRANCH) for y in (y_a, y_b, y_c, y_d)]
        h = _outproj(h, ys, w_out[l].astype(BF16), row2(final_norm_w), l == depth - 1,
                     min(OUT_TILE, n_tok))
    return h.reshape(n_batch, seq, D_MODEL)
```

```python
import functools

import numpy as np
import jax
import jax.numpy as jnp
from jax import lax
from jax.experimental import pallas as pl
from jax.experimental.pallas import tpu as pltpu

F32 = jnp.float32
BF16 = jnp.bfloat16

D_MODEL = 1024
D_BRANCH = 512
EPS = 1e-6
NEG_BIG = -1e30

GLA_HEADS, GLA_DK, GLA_DV = 4, 64, 128
GLA_GATE_RANK, GLA_GATE_NORM = 16, 16.0
MLSTM_HEADS, MLSTM_DH, MLSTM_CONV = 4, 128, 4
HGRN_HEADS, HGRN_DK, HGRN_DV = 4, 128, 128
SSD_HEAD_DIM, SSD_HEADS, SSD_GROUPS, SSD_STATE, SSD_CONV = 64, 8, 2, 128, 4
GLA_QK = GLA_HEADS * GLA_DK
HGRN_QF = HGRN_HEADS * HGRN_DK
SSD_BC = SSD_GROUPS * SSD_STATE
PROJ_SIZES = (
    GLA_QK, GLA_QK, D_BRANCH, GLA_GATE_RANK, D_BRANCH,
    D_BRANCH, D_BRANCH, D_BRANCH, MLSTM_HEADS, MLSTM_HEADS, D_BRANCH, D_BRANCH,
    HGRN_QF, HGRN_QF, D_BRANCH, D_BRANCH,
    D_BRANCH, SSD_BC, SSD_BC, SSD_HEADS, D_BRANCH,
)

LANES = 128
HIST = 8
VMEM_LIMIT = 56 * 1024 * 1024

IN_TILE = 256
MIX_TILE = 512
OUT_TILE = 512

CHUNK = 64
N_LEVELS = 6
MIN_VPU_LEVEL = 4

SM_GR, SM_I, SM_F, SM_DT = 0, 16, 20, 24


def _gla_constants():
    c = CHUNK
    t = np.arange(c)[:, None]
    d = np.arange(c)[None, :]
    blocks = [(d <= t), (d > t)]
    masks = [np.eye(c, dtype=bool)]
    for l in range(N_LEVELS):
        s = c >> (l + 1)
        mid_t = (t // (2 * s)) * (2 * s) + s
        upper = t >= mid_t
        if s < MIN_VPU_LEVEL:
            blocks.append(np.where(upper, (d >= mid_t) & (d <= t), (d > t) & (d <= mid_t - 1)))
        same = (t // (2 * s)) == (d // (2 * s))
        masks.append(same & upper & (d < mid_t))
    mstack = np.concatenate(blocks, axis=0).astype(np.float32)
    mstack = np.concatenate([mstack, mstack], axis=1)
    return jnp.asarray(mstack, BF16), jnp.asarray(np.stack(masks).astype(np.float32))


def _tri(n):
    return jnp.asarray(np.tril(np.ones((n, n), np.float32)), BF16)


def _dot(a, b):
    return jnp.dot(a, b, preferred_element_type=F32)


def _dot_nt(a, b):
    return lax.dot_general(a, b, (((1,), (1,)), ((), ())), preferred_element_type=F32)


def _dot_tn(a, b):
    return lax.dot_general(a, b, (((0,), (0,)), ((), ())), preferred_element_type=F32)


def _split3(x):
    hi = x.astype(BF16)
    r1 = x - hi.astype(F32)
    mid = r1.astype(BF16)
    lo = (r1 - mid.astype(F32)).astype(BF16)
    return hi, mid, lo


def _sel_dot(sel, x):
    hi, mid, lo = _split3(x)
    return _dot(sel, hi) + _dot(sel, mid) + _dot(sel, lo)


def _sel_dot_nt(sel, x):
    hi, mid, lo = _split3(x)
    return _dot_nt(sel, hi) + _dot_nt(sel, mid) + _dot_nt(sel, lo)


def _sel_dot2(sel2, x):
    hi = x.astype(BF16)
    mid = (x - hi.astype(F32)).astype(BF16)
    return _dot(sel2, jnp.concatenate([hi, mid], axis=0))


def _level_exponent(gcs, s):
    pieces = []
    for b in range(0, CHUNK, 2 * s):
        ref_row = gcs[b + s - 1:b + s, :]
        if s >= 8:
            pieces += [ref_row - gcs[b:b + s], gcs[b + s:b + 2 * s] - ref_row]
        else:
            diff = gcs[b:b + 2 * s] - ref_row
            pieces.append(jnp.minimum(diff, -diff))
    return jnp.concatenate(pieces, axis=0)


def _dot_sel2(x, sel2):
    hi = x.astype(BF16)
    mid = (x - hi.astype(F32)).astype(BF16)
    return _dot(jnp.concatenate([hi, mid], axis=1), sel2)


def _dot_f32(a, b):
    ah = a.astype(BF16)
    al = (a - ah.astype(F32)).astype(BF16)
    bh = b.astype(BF16)
    bl = (b - bh.astype(F32)).astype(BF16)
    return _dot(ah, bh) + _dot(al, bh) + _dot(ah, bl)


def _softplus(x):
    return jnp.maximum(x, 0.0) + jnp.log1p(jnp.exp(-jnp.abs(x)))


def _log_sigmoid(x):
    return -_softplus(-x)


def _silu(x):
    return x * jax.nn.sigmoid(x)


def _lane_iota():
    return lax.broadcasted_iota(jnp.int32, (1, LANES), 1)


def _inproj_kernel(x_ref, nw_ref, wa_ref, wb_ref, wc_ref, wd_ref, ws_ref,
                   oa_ref, ob_ref, oc_ref, od_ref, os_ref):
    x = x_ref[...]
    ms = jnp.mean(x * x, axis=-1, keepdims=True)
    u = (x * lax.rsqrt(ms + EPS) * nw_ref[...]).astype(BF16)
    for w_ref, o_ref in ((wa_ref, oa_ref), (wb_ref, ob_ref), (wc_ref, oc_ref),
                         (wd_ref, od_ref), (ws_ref, os_ref)):
        o_ref[...] = _dot(u, w_ref[...])


def _inproj(h, norm_w, weights, tile):
    n_tok = h.shape[0]
    grid = (n_tok // tile,)
    const = lambda i: (0, 0)
    row = lambda i: (i, 0)
    in_specs = [pl.BlockSpec((tile, D_MODEL), row), pl.BlockSpec((1, D_MODEL), const)]
    in_specs += [pl.BlockSpec(w.shape, const, pipeline_mode=pl.Buffered(1)) for w in weights]
    out_specs = [pl.BlockSpec((tile, w.shape[1]), row) for w in weights]
    out_shape = [jax.ShapeDtypeStruct((n_tok, w.shape[1]), F32) for w in weights]
    return pl.pallas_call(
        _inproj_kernel, grid=grid, in_specs=in_specs, out_specs=out_specs, out_shape=out_shape,
        compiler_params=pltpu.CompilerParams(dimension_semantics=("arbitrary",),
                                             vmem_limit_bytes=VMEM_LIMIT),
        name="inproj",
    )(h, norm_w, *weights)


def _outproj_kernel(h_ref, ya_ref, yb_ref, yc_ref, yd_ref, w_ref, fw_ref, o_ref, *, final):
    acc = h_ref[...]
    for i, y_ref in enumerate((ya_ref, yb_ref, yc_ref, yd_ref)):
        acc = acc + _dot(y_ref[...], w_ref[i * D_BRANCH:(i + 1) * D_BRANCH, :])
    if final:
        ms = jnp.mean(acc * acc, axis=-1, keepdims=True)
        acc = acc * lax.rsqrt(ms + EPS) * fw_ref[...]
    o_ref[...] = acc


def _outproj(h, ys, w_out, final_w, final, tile):
    n_tok = h.shape[0]
    const = lambda i: (0, 0)
    row = lambda i: (i, 0)
    in_specs = [pl.BlockSpec((tile, D_MODEL), row)]
    in_specs += [pl.BlockSpec((tile, D_BRANCH), row) for _ in ys]
    in_specs += [pl.BlockSpec(w_out.shape, const), pl.BlockSpec((1, D_MODEL), const)]
    return pl.pallas_call(
        functools.partial(_outproj_kernel, final=final),
        grid=(n_tok // tile,), in_specs=in_specs,
        out_specs=pl.BlockSpec((tile, D_MODEL), row),
        out_shape=jax.ShapeDtypeStruct((n_tok, D_MODEL), F32),
        compiler_params=pltpu.CompilerParams(dimension_semantics=("arbitrary",),
                                             vmem_limit_bytes=VMEM_LIMIT),
        name="outproj",
    )(h, *ys, w_out, final_w)


def _norm_gate_store(y_ref, rows, col0, parts, nw_ref, z_parts):
    width = sum(p.shape[-1] for p in parts)
    ss = sum(jnp.sum(p * p, axis=-1, keepdims=True) for p in parts)
    scale = lax.rsqrt(ss * (1.0 / width) + EPS)
    c = col0
    for p, z in zip(parts, z_parts):
        w = p.shape[-1]
        out = p * scale * nw_ref[:, c:c + w]
        if z is not None:
            out = out * _silu(z)
        y_ref[rows, c:c + w] = out.astype(y_ref.dtype)
        c += w


def _causal_conv_silu(src_ref, width, xe_ref, hist_ref, w_ref, b_ref, dst_ref, tile, taps):
    xe_ref[0:HIST, :] = hist_ref[...]
    xe_ref[HIST:HIST + tile, :] = src_ref[:, 0:width]
    hist_ref[...] = src_ref[tile - HIST:tile, 0:width]
    blk = 64
    for r in range(0, tile, blk):
        acc = b_ref[...] + w_ref[taps - 1:taps, :] * xe_ref[HIST + r:HIST + r + blk, :]
        for k in range(taps - 1):
            off = HIST - (taps - 1) + k
            acc = acc + w_ref[k:k + 1, :] * xe_ref[off + r:off + r + blk, :]
        dst_ref[r:r + blk, :] = _silu(acc)


def _run_stages(streams, stages, rows):
    for stage in stages:
        for d in streams:
            stage(d, rows)


def _gla_chunks(streams, ms_ref, mk_ref, *, tile):
    lane = _lane_iota()

    def lane_mask(d, j):
        dk = d["dk"]
        return None if dk == LANES else (lane >= j * dk) & (lane < (j + 1) * dk)

    def pick(a, lm):
        return a if lm is None else jnp.where(lm, a, jnp.zeros_like(a))

    def n_groups(d):
        return d["q"].shape[1] // LANES

    def cumsums(d, rows):
        d["es"] = [_sel_dot2(ms_ref[...], d["lg"][rows, p * 2 * LANES:(p + 1) * 2 * LANES])
                   for p in range(n_groups(d) // 2)]

    def operands(d, rows):
        ops = []
        for g in range(n_groups(d)):
            ls = slice(g * LANES, (g + 1) * LANES)
            e = d["es"][g // 2][:, (g % 2) * LANES:(g % 2 + 1) * LANES]
            q = d["q"][rows, ls]
            k = d["k"][rows, ls]
            gcs = e[0:CHUNK]
            qg = (q * jnp.exp(gcs)).astype(BF16)
            kd = (k * jnp.exp(e[CHUNK:2 * CHUNK])).astype(BF16)
            dec = jnp.exp(gcs[CHUNK - 1:CHUNK, :])
            qb = q.astype(BF16)
            kb = k.astype(BF16)
            ql, kl = [qb], [kb]
            mxu_block = 2
            for l in range(N_LEVELS):
                s = CHUNK >> (l + 1)
                if s >= MIN_VPU_LEVEL:
                    el = _level_exponent(gcs, s)
                else:
                    el = e[mxu_block * CHUNK:(mxu_block + 1) * CHUNK]
                    mxu_block += 1
                w = jnp.exp(el).astype(BF16)
                ql.append(qb * w)
                kl.append(kb * w)
            ops.append((qg, kd, dec, ql, kl))
        d["ops"] = ops

    def scores(d, rows):
        heads = []
        for g in range(n_groups(d)):
            ql, kl = d["ops"][g][3], d["ops"][g][4]
            for j in range(LANES // d["dk"]):
                lm = lane_mask(d, j)
                a = _dot_nt(pick(ql[0], lm), kl[0]) * mk_ref[0]
                for l in range(1, N_LEVELS + 1):
                    a = a + _dot_nt(pick(ql[l], lm), kl[l]) * mk_ref[l]
                heads.append((g, j, a.astype(BF16)))
        d["heads"] = heads

    def outputs(d, rows):
        hp = LANES // d["dk"]
        outs, upds = [], [None] * n_groups(d)
        sts = [d["st"][g] for g in range(n_groups(d))]
        stbs = [st.astype(BF16) for st in sts]
        for g, j, ab in d["heads"]:
            h = g * hp + j
            lm = lane_mask(d, j)
            qg, kd = d["ops"][g][0], d["ops"][g][1]
            vh = d["v"][rows, h * LANES:(h + 1) * LANES].astype(BF16)
            outs.append(_dot(ab, vh) + _dot_nt(pick(qg, lm), stbs[g]))
            u = _dot_tn(vh, kd)
            upds[g] = u if upds[g] is None else jnp.where(lm, u, upds[g])
        d["outs"], d["upds"], d["sts"] = outs, upds, sts

    def finish(d, rows):
        for g in range(n_groups(d)):
            d["st"][g] = d["sts"][g] * d["ops"][g][2] + d["upds"][g]
        for h, o in enumerate(d["outs"]):
            _norm_gate_store(d["y"], rows, h * LANES, [o], d["nw"],
                             [d["z"][rows, h * LANES:(h + 1) * LANES]])

    def body(c, carry):
        rows = pl.ds(pl.multiple_of(c * CHUNK, CHUNK), CHUNK)
        _run_stages([dict(d) for d in streams], (cumsums, operands, scores, outputs, finish), rows)
        return carry

    lax.fori_loop(0, tile // CHUNK, body, 0)


def _mixer_call(kernel, n_batch, seq, tile, tok_inputs, const_inputs, scratch, name, n_out=1):
    tok = lambda t: (0, t, 0)
    in_specs = [pl.BlockSpec((n_batch, tile, a.shape[2]), tok) for a in tok_inputs]
    for a in const_inputs:
        in_specs.append(pl.BlockSpec(a.shape, lambda t, nd=a.ndim: (0,) * nd))
    out_spec = pl.BlockSpec((n_batch, tile, D_BRANCH), tok)
    out_shape = jax.ShapeDtypeStruct((n_batch, seq, D_BRANCH), BF16)
    return pl.pallas_call(
        kernel, grid=(seq // tile,), in_specs=in_specs,
        out_specs=out_spec if n_out == 1 else [out_spec] * n_out,
        out_shape=out_shape if n_out == 1 else [out_shape] * n_out,
        scratch_shapes=scratch,
        compiler_params=pltpu.CompilerParams(dimension_semantics=("arbitrary",),
                                             vmem_limit_bytes=VMEM_LIMIT),
        name=name,
    )(*tok_inputs, *const_inputs)


def _zero_at_start(*refs):
    @pl.when(pl.program_id(0) == 0)
    def _():
        for r in refs:
            r[...] = jnp.zeros_like(r)


def _gla_hgrn_kernel(pa_ref, sm_ref, pc_ref, gw_ref, gb_ref, nwa_ref, lb_ref, nwc_ref, ms_ref, mk_ref,
                     ya_ref, yc_ref, qa_s, lga_s, sta_s, qc_s, kc_s, lgc_s, stc_s, *, tile):
    _zero_at_start(sta_s, stc_s)
    ka0, va0, za0 = GLA_QK, 2 * GLA_QK, 2 * GLA_QK + D_BRANCH
    vc0, zc0 = 2 * HGRN_QF, 2 * HGRN_QF + D_BRANCH
    lb = lb_ref[...]
    streams = []
    for s in range(pa_ref.shape[0]):
        qa_s[s] = pa_ref[s, :, 0:GLA_QK] * (GLA_DK ** -0.5)
        x = _dot_f32(sm_ref[s], gw_ref[...]) + gb_ref[...]
        lga_s[s] = _log_sigmoid(x) * (1.0 / GLA_GATE_NORM)
        streams.append(dict(q=qa_s.at[s], k=pa_ref.at[s, :, ka0:va0], lg=lga_s.at[s],
                            v=pa_ref.at[s, :, va0:za0], z=pa_ref.at[s, :, za0:za0 + D_BRANCH],
                            y=ya_ref.at[s], st=sta_s.at[s], nw=nwa_ref, dk=GLA_DK))
        fr = pc_ref[s, :, HGRN_QF:2 * HGRN_QF]
        f = lb + (1.0 - lb) * jax.nn.sigmoid(fr)
        lgc_s[s] = jnp.log(jnp.maximum(f, 1e-30))
        kc_s[s] = (1.0 - lb) * jax.nn.sigmoid(-fr)
        qc_s[s] = pc_ref[s, :, 0:HGRN_QF] * (HGRN_DK ** -0.5)
        streams.append(dict(q=qc_s.at[s], k=kc_s.at[s], lg=lgc_s.at[s],
                            v=pc_ref.at[s, :, vc0:zc0], z=pc_ref.at[s, :, zc0:zc0 + D_BRANCH],
                            y=yc_ref.at[s], st=stc_s.at[s], nw=nwc_ref, dk=HGRN_DK))
    _gla_chunks(streams, ms_ref, mk_ref, tile=tile)


def _mlstm_constants():
    half = LANES // 2
    full0 = half * MLSTM_HEADS
    e = np.zeros((LANES, full0 + 2 * LANES * MLSTM_HEADS), np.float32)
    diff0 = full0 + LANES * MLSTM_HEADS
    sel = np.zeros((16, LANES), np.float32)
    for h in range(MLSTM_HEADS):
        e[SM_F + h, h * half:(h + 1) * half] = 1.0
        e[SM_F + h, full0 + h * LANES:full0 + (h + 1) * LANES] = 1.0
        e[SM_I + h, diff0 + h * LANES:diff0 + (h + 1) * LANES] = 1.0
        e[SM_F + h, diff0 + h * LANES:diff0 + (h + 1) * LANES] = -1.0
        sel[h // 2, SM_I + h] = 1.0
        sel[h // 2, SM_F + h] = -1.0
    return jnp.asarray(np.concatenate([e, e], axis=0), BF16), jnp.asarray(sel, BF16)


def _mlstm_kernel(pb_ref, sm_ref, cw_ref, cb_ref, gbias_ref, nw_ref, tri_ref, exp_ref, sel_ref,
                  y_ref, xe_s, hist_s, qk_s, g_s, c_s, m_s, *, tile):
    _zero_at_start(hist_s, c_s, m_s)
    lane = _lane_iota()
    is_f = (lane >= SM_F) & (lane < SM_F + MLSTM_HEADS)
    streams = []
    for s in range(pb_ref.shape[0]):
        _causal_conv_silu(pb_ref.at[s], 2 * D_BRANCH, xe_s, hist_s.at[s], cw_ref, cb_ref,
                          qk_s.at[s], tile, MLSTM_CONV)
        gates = sm_ref[s] + gbias_ref[...]
        g_s[s] = jnp.where(is_f, _log_sigmoid(gates), gates)
        streams.append(dict(p=pb_ref.at[s], qk=qk_s.at[s], g=g_s.at[s], c=c_s.at[s],
                            m=m_s.at[s], y=y_ref.at[s]))

    half = LANES // 2
    ri = lax.broadcasted_iota(jnp.int32, (CHUNK, LANES), 0)
    ci = lax.broadcasted_iota(jnp.int32, (CHUNK, LANES), 1)
    causal2 = (ci & (half - 1)) <= ri
    lo_half = lane < half
    even = (lane & 1) == 0
    v0, o0, z0 = 2 * D_BRANCH, 3 * D_BRANCH, 4 * D_BRANCH
    heads = range(MLSTM_HEADS)
    pairs = range(MLSTM_HEADS // 2)
    full0 = half * MLSTM_HEADS
    diff0 = full0 + LANES * MLSTM_HEADS

    def side_by_side(a, b, zero):
        return jnp.concatenate([jnp.concatenate([a, zero], axis=1),
                                jnp.concatenate([zero, b], axis=1)], axis=0)

    def qk_products(d, rows):
        qk_ref = d["qk"]
        qbs = [qk_ref[rows, h * LANES:(h + 1) * LANES].astype(BF16) for h in heads]
        d["ks"] = [qk_ref[rows, D_BRANCH + h * LANES:D_BRANCH + (h + 1) * LANES] * (MLSTM_DH ** -0.5)
                   for h in heads]
        kbs = [k.astype(BF16) for k in d["ks"]]
        zero = jnp.zeros((CHUNK, LANES), BF16)
        ones = jnp.ones((CHUNK, LANES), BF16)
        d["qk_raw"] = [_dot_nt(jnp.concatenate([qbs[2 * p], qbs[2 * p + 1]], axis=1),
                               side_by_side(kbs[2 * p], kbs[2 * p + 1], zero)) for p in pairs]
        d["qc"] = [_dot(qbs[h], d["c"][h].astype(BF16)) for h in heads]
        d["vaug"] = [jnp.concatenate([d["p"][rows, v0 + h * LANES:v0 + (h + 1) * LANES].astype(BF16),
                                      ones], axis=1) for h in heads]

    def gate_sums(d, rows):
        gc = d["g"][rows, :]
        bcol = _sel_dot(tri_ref[...], gc)
        ib = jnp.where(is_f, bcol, gc)
        d["ex"] = _dot_sel2(ib, exp_ref[...])
        by_parity = jnp.concatenate([jnp.where(even, ib, 0.0), jnp.where(even, 0.0, ib)], axis=0)
        d["drows"] = _sel_dot_nt(sel_ref[...], by_parity)

    def weights(d, rows):
        ex = d["ex"]
        mx = d["m"][0:1, :]
        ss, mrs = [], []
        for p in pairs:
            bx = ex[:, p * LANES:(p + 1) * LANES]
            lw = jnp.where(causal2, bx + d["drows"][p:p + 1, :], NEG_BIG)
            mr0 = jnp.max(jnp.where(lo_half, lw, NEG_BIG), axis=-1, keepdims=True)
            mr1 = jnp.max(jnp.where(lo_half, NEG_BIG, lw), axis=-1, keepdims=True)
            m64 = jnp.where(lo_half, mx[:, 2 * p * LANES:(2 * p + 1) * LANES],
                            mx[:, (2 * p + 1) * LANES:(2 * p + 2) * LANES])
            m_row = jnp.maximum(jnp.where(lo_half, mr0, mr1), bx + m64)
            ss.append((d["qk_raw"][p] * jnp.exp(lw - m_row)).astype(BF16))
            mrs += [mr0, mr1]
        d["ss"], d["mrs"] = ss, mrs
        b_last = ex[CHUNK - 1:CHUNK, full0:diff0]
        lwe = ex[:, diff0:] + b_last
        m_new = jnp.maximum(b_last + mx, jnp.max(lwe, axis=0, keepdims=True))
        d["cd"] = jnp.exp(b_last + mx - m_new)
        d["m_new"] = m_new
        kw = jnp.exp(lwe - m_new)
        d["kws"] = [(d["ks"][h] * kw[:, h * LANES:(h + 1) * LANES]).astype(BF16) for h in heads]

    def numerators(d, rows):
        zero = jnp.zeros((CHUNK, 2 * LANES), BF16)
        d["nums"] = [_dot(d["ss"][p], side_by_side(d["vaug"][2 * p], d["vaug"][2 * p + 1], zero))
                     for p in pairs]
        d["cups"] = [_dot_tn(d["kws"][h], d["vaug"][h]) for h in heads]

    def finish(d, rows):
        ex = d["ex"]
        mx = d["m"][0:1, :]
        for h in heads:
            p, hd = divmod(h, 2)
            hs = slice(h * LANES, (h + 1) * LANES)
            m_inter = ex[:, full0 + h * LANES:full0 + (h + 1) * LANES] + mx[:, hs]
            m_row = jnp.maximum(d["mrs"][h], m_inter)
            inter = jnp.exp(m_inter - m_row)
            sv = d["nums"][p][:, hd * 2 * LANES:(hd + 1) * 2 * LANES]
            num = sv[:, 0:LANES] + inter * d["qc"][h][:, 0:LANES]
            den = sv[:, LANES:] + inter * d["qc"][h][:, LANES:]
            hh = num / jnp.maximum(jnp.abs(den), jnp.exp(-m_row))
            cd = d["cd"][:, hs]
            d["c"][h] = jnp.concatenate([cd, cd], axis=1) * d["c"][h] + d["cups"][h]
            og = jax.nn.sigmoid(d["p"][rows, o0 + h * LANES:o0 + (h + 1) * LANES])
            _norm_gate_store(d["y"], rows, h * LANES, [og * hh], nw_ref,
                             [d["p"][rows, z0 + h * LANES:z0 + (h + 1) * LANES]])
        d["m"][0:1, :] = d["m_new"]

    def body(c, carry):
        rows = pl.ds(pl.multiple_of(c * CHUNK, CHUNK), CHUNK)
        _run_stages([dict(d) for d in streams],
                    (qk_products, gate_sums, weights, numerators, finish), rows)
        return carry

    lax.fori_loop(0, tile // CHUNK, body, 0)


def _ssd_constants():
    e = np.zeros((LANES, SSD_HEADS * SSD_HEAD_DIM), np.float32)
    sel = np.zeros((16, LANES), np.float32)
    for h in range(SSD_HEADS):
        e[SM_DT + h, h * SSD_HEAD_DIM:(h + 1) * SSD_HEAD_DIM] = 1.0
        sel[h // 2, SM_DT + h] = 1.0
    return jnp.asarray(np.concatenate([e, e], axis=0), BF16), jnp.asarray(sel, BF16)


def _ssd_kernel(pd_ref, sm_ref, cw_ref, cb_ref, dtb_ref, alog_ref, dx_ref, nw_ref,
                tri_ref, sel_ref, exp_ref, y_ref, xe_s, hist_s, xbc_s, dt_s, st_s, *, tile):
    _zero_at_start(hist_s, st_s)
    width = D_BRANCH + 2 * SSD_BC
    lane = _lane_iota()
    is_dt = (lane >= SM_DT) & (lane < SM_DT + SSD_HEADS)
    streams = []
    for s in range(pd_ref.shape[0]):
        _causal_conv_silu(pd_ref.at[s], width, xe_s, hist_s.at[s], cw_ref, cb_ref, xbc_s.at[s],
                          tile, SSD_CONV)
        dt_s[s] = jnp.where(is_dt, _softplus(sm_ref[s] + dtb_ref[...]), 0.0)
        streams.append(dict(p=pd_ref.at[s], xbc=xbc_s.at[s], dt=dt_s.at[s], st=st_s.at[s],
                            y=y_ref.at[s]))

    half = LANES // 2
    ri = lax.broadcasted_iota(jnp.int32, (CHUNK, LANES), 0)
    ci = lax.broadcasted_iota(jnp.int32, (CHUNK, LANES), 1)
    causal2 = (ci & (half - 1)) <= ri
    lo_half = lane < half
    even = (lane & 1) == 0
    a_lane = jnp.where(is_dt, -jnp.exp(alog_ref[...]), 0.0)
    b0, c0, z0 = D_BRANCH, D_BRANCH + SSD_BC, D_BRANCH + 2 * SSD_BC
    groups = range(SSD_GROUPS)
    group_w = D_BRANCH // SSD_GROUPS
    pairs = range(D_BRANCH // LANES)
    pairs_per_group = group_w // LANES

    def products(d, rows):
        xbc = d["xbc"]
        d["bgs"] = [xbc[rows, b0 + g * SSD_STATE:b0 + (g + 1) * SSD_STATE].astype(BF16) for g in groups]
        cgbs = [xbc[rows, c0 + g * SSD_STATE:c0 + (g + 1) * SSD_STATE].astype(BF16) for g in groups]
        d["cb2"] = [_dot_nt(cgbs[g], jnp.concatenate([d["bgs"][g], d["bgs"][g]], axis=0))
                    for g in groups]
        d["cst"] = [_dot(cgbs[g], d["st"][g].astype(BF16)) for g in groups]

    def decay_sums(d, rows):
        dt = d["dt"][rows, :]
        acs = _sel_dot(tri_ref[...], dt * a_lane)
        ex = _dot_sel2(jnp.concatenate([dt, acs], axis=0), exp_ref[...])
        d["dtx"], d["acs_x"] = ex[0:CHUNK], ex[CHUNK:]
        by_parity = jnp.concatenate([jnp.where(even, acs, 0.0), jnp.where(even, 0.0, acs)], axis=0)
        d["a_rows"] = _sel_dot_nt(sel_ref[...], by_parity)

    def decays(d, rows):
        d["xss"], d["ms"], d["xblk"], xdecs = [], [], [], []
        for p in pairs:
            ls = slice(p * LANES, (p + 1) * LANES)
            ax = d["acs_x"][:, ls]
            lmat = jnp.exp(jnp.where(causal2, ax - d["a_rows"][p:p + 1, :], NEG_BIG))
            d["ms"].append((d["cb2"][p // pairs_per_group] * lmat).astype(BF16))
            xs = d["xbc"][rows, ls]
            xdt = xs * d["dtx"][:, ls]
            d["xss"].append(xs)
            d["xblk"].append(jnp.concatenate([jnp.where(lo_half, xdt, 0.0), jnp.where(lo_half, 0.0, xdt)],
                                             axis=0).astype(BF16))
            xdecs.append((xdt * jnp.exp(ax[CHUNK - 1:CHUNK, :] - ax)).astype(BF16))
        d["xdec"] = [jnp.concatenate(xdecs[g * pairs_per_group:(g + 1) * pairs_per_group], axis=1)
                     for g in groups]

    def chunk_products(d, rows):
        d["yds"] = [_dot(d["ms"][p], d["xblk"][p]) for p in pairs]
        d["ups"] = [_dot_tn(d["bgs"][g], d["xdec"][g]) for g in groups]

    def finish(d, rows):
        for g in groups:
            gs = slice(g * group_w, (g + 1) * group_w)
            eax = jnp.exp(d["acs_x"][:, gs])
            d["st"][g] = d["st"][g] * eax[CHUNK - 1:CHUNK, :] + d["ups"][g]
            ys = []
            for pp in range(pairs_per_group):
                p = g * pairs_per_group + pp
                ls = slice(p * LANES, (p + 1) * LANES)
                y = (d["cst"][g][:, pp * LANES:(pp + 1) * LANES] * eax[:, pp * LANES:(pp + 1) * LANES]
                     + dx_ref[:, ls] * d["xss"][p] + d["yds"][p])
                ys.append(y * _silu(d["p"][rows, z0 + p * LANES:z0 + (p + 1) * LANES]))
            _norm_gate_store(d["y"], rows, g * group_w, ys, nw_ref, [None] * len(ys))

    def body(c, carry):
        rows = pl.ds(pl.multiple_of(c * CHUNK, CHUNK), CHUNK)
        _run_stages([dict(d) for d in streams],
                    (products, decay_sums, decays, chunk_products, finish), rows)
        return carry

    lax.fori_loop(0, tile // CHUNK, body, 0)


def _pad_lanes(parts, total=LANES):
    width = sum(p.shape[-1] for p in parts)
    lead = parts[0].shape[:-1]
    return jnp.concatenate(list(parts) + [jnp.zeros(lead + (total - width,), parts[0].dtype)], axis=-1)


def _small_vector(i_part, f_part, dt_part):
    z = jnp.zeros((GLA_GATE_RANK,), F32)
    return _pad_lanes([z, i_part.astype(F32), f_part.astype(F32), dt_part.astype(F32)])[None, :]


def kernel(x, norm_w, w_in, gla_gate_w, gla_gate_b, gla_norm_w, ml_conv_w, ml_conv_b, ml_i_b, ml_f_b,
           ml_norm_w, hg_lb_logits, hg_norm_w, ssd_conv_w, ssd_conv_b, ssd_dt_bias, ssd_A_log, ssd_D,
           ssd_norm_w, w_out, final_norm_w):
    n_batch, seq, _ = x.shape
    depth = w_in.shape[0]
    tile = min(MIX_TILE, seq)
    n_tok = n_batch * seq

    mstack, masks = _gla_constants()
    tri_c = _tri(CHUNK)
    ml_exp, ml_sel = _mlstm_constants()
    ssd_exp, ssd_sel = _ssd_constants()
    zero4 = jnp.zeros((MLSTM_HEADS,), F32)
    zero8 = jnp.zeros((SSD_HEADS,), F32)

    p = jax.nn.softmax(hg_lb_logits.astype(F32), axis=0)
    lower_bounds = jnp.cumsum(p, axis=0) - p[0:1]

    split_at = [int(s) for s in np.cumsum(PROJ_SIZES)[:-1]]
    h = x.reshape(n_tok, D_MODEL)
    row2 = lambda v: v.astype(F32).reshape(1, -1)
    for l in range(depth):
        (a_q, a_k, a_v, a_gr, a_z, b_q, b_k, b_v, b_i, b_f, b_o, b_z,
         c_q, c_f, c_i, c_z, d_x, d_b, d_c, d_dt, d_z) = jnp.split(w_in[l], split_at, axis=-1)
        weights = [
            jnp.concatenate([a_q, a_k, a_v, a_z], axis=-1).astype(BF16),
            jnp.concatenate([b_q, b_k, b_v, b_o, b_z], axis=-1).astype(BF16),
            jnp.concatenate([c_q, c_f, c_i, c_z], axis=-1).astype(BF16),
            jnp.concatenate([d_x, d_b, d_c, d_z], axis=-1).astype(BF16),
            _pad_lanes([a_gr, b_i, b_f, d_dt]).astype(BF16),
        ]
        pa, pb, pc, pd, sm = [p.reshape(n_batch, seq, -1)
                              for p in _inproj(h, row2(norm_w[l]), weights, min(IN_TILE, n_tok))]

        gate_w = jnp.concatenate(
            [gla_gate_w[l].astype(F32), jnp.zeros((LANES - GLA_GATE_RANK, GLA_QK), F32)], axis=0)
        y_a, y_c = _mixer_call(
            functools.partial(_gla_hgrn_kernel, tile=tile), n_batch, seq, tile, [pa, sm, pc],
            [gate_w, row2(gla_gate_b[l]), row2(gla_norm_w[l]), row2(lower_bounds[l]),
             row2(hg_norm_w[l]), mstack, masks],
            [pltpu.VMEM((n_batch, tile, GLA_QK), F32), pltpu.VMEM((n_batch, tile, GLA_QK), F32),
             pltpu.VMEM((n_batch, GLA_QK // LANES, GLA_DV, LANES), F32),
             pltpu.VMEM((n_batch, tile, HGRN_QF), F32), pltpu.VMEM((n_batch, tile, HGRN_QF), F32),
             pltpu.VMEM((n_batch, tile, HGRN_QF), F32),
             pltpu.VMEM((n_batch, HGRN_QF // LANES, HGRN_DV, LANES), F32)], "gla_hgrn", n_out=2)

        y_b = _mixer_call(
            functools.partial(_mlstm_kernel, tile=tile), n_batch, seq, tile, [pb, sm],
            [ml_conv_w[l].astype(F32), row2(ml_conv_b[l]),
             _small_vector(ml_i_b[l], ml_f_b[l], zero8), row2(ml_norm_w[l]), tri_c, ml_exp, ml_sel],
            [pltpu.VMEM((tile + HIST, 2 * D_BRANCH), F32), pltpu.VMEM((n_batch, HIST, 2 * D_BRANCH), F32),
             pltpu.VMEM((n_batch, tile, 2 * D_BRANCH), F32), pltpu.VMEM((n_batch, tile, LANES), F32),
             pltpu.VMEM((n_batch, MLSTM_HEADS, MLSTM_DH, 2 * MLSTM_DH), F32),
             pltpu.VMEM((n_batch, 8, MLSTM_HEADS * LANES), F32)], "mlstm")

        width = D_BRANCH + 2 * SSD_BC
        y_d = _mixer_call(
            functools.partial(_ssd_kernel, tile=tile), n_batch, seq, tile, [pd, sm],
            [ssd_conv_w[l].astype(F32), row2(ssd_conv_b[l]),
             _small_vector(zero4, zero4, ssd_dt_bias[l]), _small_vector(zero4, zero4, ssd_A_log[l]),
             row2(jnp.repeat(ssd_D[l].astype(F32), SSD_HEAD_DIM)), row2(ssd_norm_w[l]),
             tri_c, ssd_sel, ssd_exp],
            [pltpu.VMEM((tile + HIST, width), F32), pltpu.VMEM((n_batch, HIST, width), F32),
             pltpu.VMEM((n_batch, tile, width), F32), pltpu.VMEM((n_batch, tile, LANES), F32),
             pltpu.VMEM((n_batch, SSD_GROUPS, SSD_STATE, D_BRANCH // SSD_GROUPS), F32)], "ssd")

        ys = [y.reshape(n_tok, D_BRANCH) for y in (y_a, y_b, y_c, y_d)]
        h = _outproj(h, ys, w_out[l].astype(BF16), row2(final_norm_w), l == depth - 1,
                     min(OUT_TILE, n_tok))
    return h.reshape(n_batch, seq, D_MODEL)
```

```python
import functools

import numpy as np
import jax
import jax.numpy as jnp
from jax import lax
from jax.experimental import pallas as pl
from jax.experimental.pallas import tpu as pltpu

F32 = jnp.float32
BF16 = jnp.bfloat16

D_MODEL = 1024
D_BRANCH = 512
EPS = 1e-6
NEG_BIG = -1e30

GLA_HEADS, GLA_DK, GLA_DV = 4, 64, 128
GLA_GATE_RANK, GLA_GATE_NORM = 16, 16.0
MLSTM_HEADS, MLSTM_DH, MLSTM_CONV = 4, 128, 4
HGRN_HEADS, HGRN_DK, HGRN_DV = 4, 128, 128
SSD_HEAD_DIM, SSD_HEADS, SSD_GROUPS, SSD_STATE, SSD_CONV = 64, 8, 2, 128, 4
GLA_QK = GLA_HEADS * GLA_DK
HGRN_QF = HGRN_HEADS * HGRN_DK
SSD_BC = SSD_GROUPS * SSD_STATE
PROJ_SIZES = (
    GLA_QK, GLA_QK, D_BRANCH, GLA_GATE_RANK, D_BRANCH,
    D_BRANCH, D_BRANCH, D_BRANCH, MLSTM_HEADS, MLSTM_HEADS, D_BRANCH, D_BRANCH,
    HGRN_QF, HGRN_QF, D_BRANCH, D_BRANCH,
    D_BRANCH, SSD_BC, SSD_BC, SSD_HEADS, D_BRANCH,
)

LANES = 128
HIST = 8
VMEM_LIMIT = 56 * 1024 * 1024

IN_TILE = 256
MIX_TILE = 512
OUT_TILE = 512

CHUNK = 64
N_LEVELS = 6
MIN_VPU_LEVEL = 4

SM_GR, SM_I, SM_F, SM_DT = 0, 16, 20, 24

W_A = 2 * GLA_QK + 2 * D_BRANCH
W_B = 5 * D_BRANCH
W_C = 2 * HGRN_QF + 2 * D_BRANCH
W_D = 2 * D_BRANCH + 2 * SSD_BC
COL_A = 0
COL_B = COL_A + W_A
COL_C = COL_B + W_B
COL_D = COL_C + W_C
COL_S = COL_D + W_D


def _gla_constants():
    c = CHUNK
    t = np.arange(c)[:, None]
    d = np.arange(c)[None, :]
    blocks = [(d <= t), (d > t)]
    masks = [np.eye(c, dtype=bool)]
    for l in range(N_LEVELS):
        s = c >> (l + 1)
        mid_t = (t // (2 * s)) * (2 * s) + s
        upper = t >= mid_t
        if s < MIN_VPU_LEVEL:
            blocks.append(np.where(upper, (d >= mid_t) & (d <= t), (d > t) & (d <= mid_t - 1)))
        same = (t // (2 * s)) == (d // (2 * s))
        masks.append(same & upper & (d < mid_t))
    mstack = np.concatenate(blocks, axis=0).astype(np.float32)
    mstack = np.concatenate([mstack, mstack], axis=1)
    return jnp.asarray(mstack, BF16), jnp.asarray(np.stack(masks).astype(np.float32))


def _tri(n):
    return jnp.asarray(np.tril(np.ones((n, n), np.float32)), BF16)


def _dot(a, b):
    return jnp.dot(a, b, preferred_element_type=F32)


def _dot_nt(a, b):
    return lax.dot_general(a, b, (((1,), (1,)), ((), ())), preferred_element_type=F32)


def _dot_tn(a, b):
    return lax.dot_general(a, b, (((0,), (0,)), ((), ())), preferred_element_type=F32)


def _split3(x):
    hi = x.astype(BF16)
    r1 = x - hi.astype(F32)
    mid = r1.astype(BF16)
    lo = (r1 - mid.astype(F32)).astype(BF16)
    return hi, mid, lo


def _sel_dot(sel, x):
    hi, mid, lo = _split3(x)
    return _dot(sel, hi) + _dot(sel, mid) + _dot(sel, lo)


def _sel_dot_nt(sel, x):
    hi, mid, lo = _split3(x)
    return _dot_nt(sel, hi) + _dot_nt(sel, mid) + _dot_nt(sel, lo)


def _sel_dot2(sel2, x):
    hi = x.astype(BF16)
    mid = (x - hi.astype(F32)).astype(BF16)
    return _dot(sel2, jnp.concatenate([hi, mid], axis=0))


def _level_exponent(gcs, s):
    pieces = []
    for b in range(0, CHUNK, 2 * s):
        ref_row = gcs[b + s - 1:b + s, :]
        if s >= 8:
            pieces += [ref_row - gcs[b:b + s], gcs[b + s:b + 2 * s] - ref_row]
        else:
            diff = gcs[b:b + 2 * s] - ref_row
            pieces.append(jnp.minimum(diff, -diff))
    return jnp.concatenate(pieces, axis=0)


def _dot_sel2(x, sel2):
    hi = x.astype(BF16)
    mid = (x - hi.astype(F32)).astype(BF16)
    return _dot(jnp.concatenate([hi, mid], axis=1), sel2)


def _dot_f32(a, b):
    ah = a.astype(BF16)
    al = (a - ah.astype(F32)).astype(BF16)
    bh = b.astype(BF16)
    bl = (b - bh.astype(F32)).astype(BF16)
    return _dot(ah, bh) + _dot(al, bh) + _dot(ah, bl)


def _softplus(x):
    return jnp.maximum(x, 0.0) + jnp.log1p(jnp.exp(-jnp.abs(x)))


def _log_sigmoid(x):
    return -_softplus(-x)


def _silu(x):
    return x * jax.nn.sigmoid(x)


def _lane_iota():
    return lax.broadcasted_iota(jnp.int32, (1, LANES), 1)


def _causal_conv_silu(raw, xe_ref, hist_ref, w_ref, b_ref, taps):
    rows = raw.shape[0]
    xe_ref[0:HIST, :] = hist_ref[...]
    xe_ref[HIST:HIST + rows, :] = raw
    hist_ref[...] = raw[rows - HIST:rows, :]
    blk = 64
    out = []
    for r in range(0, rows, blk):
        acc = b_ref[...] + w_ref[taps - 1:taps, :] * xe_ref[HIST + r:HIST + r + blk, :]
        for k in range(taps - 1):
            off = HIST - (taps - 1) + k
            acc = acc + w_ref[k:k + 1, :] * xe_ref[off + r:off + r + blk, :]
        out.append(_silu(acc))
    return out


def _inproj_kernel(x_ref, nw_ref, w_ref, gw_ref, gb_ref, lb_ref, cwb_ref, cbb_ref, cwd_ref, cbd_ref,
                   gbias_ref, oa_ref, ob_ref, oc_ref, od_ref, os_ref, xe_s, hist_b, hist_d,
                   *, tiles_per_seq):
    @pl.when(pl.program_id(0) % tiles_per_seq == 0)
    def _():
        hist_b[...] = jnp.zeros_like(hist_b)
        hist_d[...] = jnp.zeros_like(hist_d)

    x = x_ref[...]
    ms = jnp.mean(x * x, axis=-1, keepdims=True)
    u = (x * lax.rsqrt(ms + EPS) * nw_ref[...]).astype(BF16)
    lane = _lane_iota()
    blk = 64

    small = _dot(u, w_ref[:, COL_S:COL_S + LANES])
    biased = small + gbias_ref[...]
    is_f = (lane >= SM_F) & (lane < SM_F + MLSTM_HEADS)
    is_dt = (lane >= SM_DT) & (lane < SM_DT + SSD_HEADS)
    os_ref[...] = jnp.where(is_f, _log_sigmoid(biased), jnp.where(is_dt, _softplus(biased), biased))

    raw = _dot(u, w_ref[:, COL_A:COL_A + W_A])
    oa_ref[:, 0:GLA_QK] = raw[:, 0:GLA_QK] * (GLA_DK ** -0.5)
    oa_ref[:, GLA_QK:W_A] = raw[:, GLA_QK:W_A]
    gate = _dot_f32(small, gw_ref[...]) + gb_ref[...]
    oa_ref[:, W_A:W_A + GLA_QK] = _log_sigmoid(gate) * (1.0 / GLA_GATE_NORM)

    raw = _dot(u, w_ref[:, COL_C:COL_C + W_C])
    lb = lb_ref[...]
    fr = raw[:, HGRN_QF:2 * HGRN_QF]
    oc_ref[:, 0:HGRN_QF] = raw[:, 0:HGRN_QF] * (HGRN_DK ** -0.5)
    oc_ref[:, HGRN_QF:2 * HGRN_QF] = (1.0 - lb) * jax.nn.sigmoid(-fr)
    oc_ref[:, 2 * HGRN_QF:W_C] = raw[:, 2 * HGRN_QF:W_C]
    oc_ref[:, W_C:W_C + HGRN_QF] = jnp.log(jnp.maximum(lb + (1.0 - lb) * jax.nn.sigmoid(fr), 1e-30))

    raw = _dot(u, w_ref[:, COL_B:COL_B + W_B])
    conv_w = 2 * D_BRANCH
    k_scale = jnp.where(lax.broadcasted_iota(jnp.int32, (1, conv_w), 1) < D_BRANCH, 1.0, MLSTM_DH ** -0.5)
    for i, blk_out in enumerate(_causal_conv_silu(raw[:, 0:conv_w], xe_s, hist_b, cwb_ref, cbb_ref,
                                                  MLSTM_CONV)):
        ob_ref[i * blk:(i + 1) * blk, 0:conv_w] = blk_out * k_scale
    ob_ref[:, conv_w:W_B] = raw[:, conv_w:W_B]

    raw = _dot(u, w_ref[:, COL_D:COL_D + W_D])
    conv_w = D_BRANCH + 2 * SSD_BC
    for i, blk_out in enumerate(_causal_conv_silu(raw[:, 0:conv_w], xe_s, hist_d, cwd_ref, cbd_ref,
                                                  SSD_CONV)):
        od_ref[i * blk:(i + 1) * blk, 0:conv_w] = blk_out
    od_ref[:, conv_w:W_D] = raw[:, conv_w:W_D]


def _inproj(h, norm_w, w_all, layer, consts, tile, tiles_per_seq):
    n_tok = h.shape[0]
    const = lambda i: (0, 0)
    row = lambda i: (i, 0)
    in_specs = [pl.BlockSpec((tile, D_MODEL), row), pl.BlockSpec((1, D_MODEL), const),
                pl.BlockSpec((None,) + w_all.shape[1:], lambda i: (layer, 0, 0),
                             pipeline_mode=pl.Buffered(1))]
    in_specs += [pl.BlockSpec(c.shape, const) for c in consts]
    widths = (W_A + GLA_QK, W_B, W_C + HGRN_QF, W_D, LANES)
    conv_w = 2 * D_BRANCH
    return pl.pallas_call(
        functools.partial(_inproj_kernel, tiles_per_seq=tiles_per_seq),
        grid=(n_tok // tile,), in_specs=in_specs,
        out_specs=[pl.BlockSpec((tile, w), row) for w in widths],
        out_shape=[jax.ShapeDtypeStruct((n_tok, w), F32) for w in widths],
        scratch_shapes=[pltpu.VMEM((tile + HIST, conv_w), F32), pltpu.VMEM((HIST, conv_w), F32),
                        pltpu.VMEM((HIST, conv_w), F32)],
        compiler_params=pltpu.CompilerParams(dimension_semantics=("arbitrary",),
                                             vmem_limit_bytes=VMEM_LIMIT),
        name="inproj",
    )(h, norm_w, w_all, *consts)


def _outproj_kernel(h_ref, ya_ref, yb_ref, yc_ref, yd_ref, w_ref, fw_ref, o_ref, *, final):
    acc = h_ref[...]
    for i, y_ref in enumerate((ya_ref, yb_ref, yc_ref, yd_ref)):
        acc = acc + _dot(y_ref[...], w_ref[i * D_BRANCH:(i + 1) * D_BRANCH, :])
    if final:
        ms = jnp.mean(acc * acc, axis=-1, keepdims=True)
        acc = acc * lax.rsqrt(ms + EPS) * fw_ref[...]
    o_ref[...] = acc


def _outproj(h, ys, w_out, final_w, final, tile):
    n_tok = h.shape[0]
    const = lambda i: (0, 0)
    row = lambda i: (i, 0)
    in_specs = [pl.BlockSpec((tile, D_MODEL), row)]
    in_specs += [pl.BlockSpec((tile, D_BRANCH), row) for _ in ys]
    in_specs += [pl.BlockSpec(w_out.shape, const), pl.BlockSpec((1, D_MODEL), const)]
    return pl.pallas_call(
        functools.partial(_outproj_kernel, final=final),
        grid=(n_tok // tile,), in_specs=in_specs,
        out_specs=pl.BlockSpec((tile, D_MODEL), row),
        out_shape=jax.ShapeDtypeStruct((n_tok, D_MODEL), F32),
        compiler_params=pltpu.CompilerParams(dimension_semantics=("arbitrary",),
                                             vmem_limit_bytes=VMEM_LIMIT),
        name="outproj",
    )(h, *ys, w_out, final_w)


def _norm_gate_store(y_ref, rows, col0, parts, nw_ref, z_parts):
    width = sum(p.shape[-1] for p in parts)
    ss = sum(jnp.sum(p * p, axis=-1, keepdims=True) for p in parts)
    scale = lax.rsqrt(ss * (1.0 / width) + EPS)
    c = col0
    for p, z in zip(parts, z_parts):
        w = p.shape[-1]
        out = p * scale * nw_ref[:, c:c + w]
        if z is not None:
            out = out * _silu(z)
        y_ref[rows, c:c + w] = out.astype(y_ref.dtype)
        c += w


def _run_stages(streams, stages, rows):
    for stage in stages:
        for d in streams:
            stage(d, rows)


def _gla_chunks(streams, ms_ref, mk_ref, *, tile):
    lane = _lane_iota()

    def lane_mask(d, j):
        dk = d["dk"]
        return None if dk == LANES else (lane >= j * dk) & (lane < (j + 1) * dk)

    def pick(a, lm):
        return a if lm is None else jnp.where(lm, a, jnp.zeros_like(a))

    def n_groups(d):
        return d["q"].shape[1] // LANES

    def cumsums(d, rows):
        d["es"] = [_sel_dot2(ms_ref[...], d["lg"][rows, p * 2 * LANES:(p + 1) * 2 * LANES])
                   for p in range(n_groups(d) // 2)]

    def operands(d, rows):
        ops = []
        for g in range(n_groups(d)):
            ls = slice(g * LANES, (g + 1) * LANES)
            e = d["es"][g // 2][:, (g % 2) * LANES:(g % 2 + 1) * LANES]
            q = d["q"][rows, ls]
            k = d["k"][rows, ls]
            gcs = e[0:CHUNK]
            qg = (q * jnp.exp(gcs)).astype(BF16)
            kd = (k * jnp.exp(e[CHUNK:2 * CHUNK])).astype(BF16)
            dec = jnp.exp(gcs[CHUNK - 1:CHUNK, :])
            qb = q.astype(BF16)
            kb = k.astype(BF16)
            ql, kl = [qb], [kb]
            mxu_block = 2
            for l in range(N_LEVELS):
                s = CHUNK >> (l + 1)
                if s >= MIN_VPU_LEVEL:
                    el = _level_exponent(gcs, s)
                else:
                    el = e[mxu_block * CHUNK:(mxu_block + 1) * CHUNK]
                    mxu_block += 1
                w = jnp.exp(el).astype(BF16)
                ql.append(qb * w)
                kl.append(kb * w)
            ops.append((qg, kd, dec, ql, kl))
        d["ops"] = ops

    def scores(d, rows):
        heads = []
        for g in range(n_groups(d)):
            ql, kl = d["ops"][g][3], d["ops"][g][4]
            for j in range(LANES // d["dk"]):
                lm = lane_mask(d, j)
                a = _dot_nt(pick(ql[0], lm), kl[0]) * mk_ref[0]
                for l in range(1, N_LEVELS + 1):
                    a = a + _dot_nt(pick(ql[l], lm), kl[l]) * mk_ref[l]
                heads.append((g, j, a.astype(BF16)))
        d["heads"] = heads

    def outputs(d, rows):
        hp = LANES // d["dk"]
        outs, upds = [], [None] * n_groups(d)
        sts = [d["st"][g] for g in range(n_groups(d))]
        stbs = [st.astype(BF16) for st in sts]
        for g, j, ab in d["heads"]:
            h = g * hp + j
            lm = lane_mask(d, j)
            qg, kd = d["ops"][g][0], d["ops"][g][1]
            vh = d["v"][rows, h * LANES:(h + 1) * LANES].astype(BF16)
            outs.append(_dot(ab, vh) + _dot_nt(pick(qg, lm), stbs[g]))
            u = _dot_tn(vh, kd)
            upds[g] = u if upds[g] is None else jnp.where(lm, u, upds[g])
        d["outs"], d["upds"], d["sts"] = outs, upds, sts

    def finish(d, rows):
        for g in range(n_groups(d)):
            d["st"][g] = d["sts"][g] * d["ops"][g][2] + d["upds"][g]
        for h, o in enumerate(d["outs"]):
            _norm_gate_store(d["y"], rows, h * LANES, [o], d["nw"],
                             [d["z"][rows, h * LANES:(h + 1) * LANES]])

    def body(c, carry):
        rows = pl.ds(pl.multiple_of(c * CHUNK, CHUNK), CHUNK)
        _run_stages([dict(d) for d in streams], (cumsums, operands, scores, outputs, finish), rows)
        return carry

    lax.fori_loop(0, tile // CHUNK, body, 0)


def _mixer_call(kernel, n_batch, seq, tile, tok_inputs, const_inputs, scratch, name, n_out=1):
    tok = lambda t: (0, t, 0)
    in_specs = [pl.BlockSpec((n_batch, tile, a.shape[2]), tok) for a in tok_inputs]
    for a in const_inputs:
        in_specs.append(pl.BlockSpec(a.shape, lambda t, nd=a.ndim: (0,) * nd))
    out_spec = pl.BlockSpec((n_batch, tile, D_BRANCH), tok)
    out_shape = jax.ShapeDtypeStruct((n_batch, seq, D_BRANCH), BF16)
    return pl.pallas_call(
        kernel, grid=(seq // tile,), in_specs=in_specs,
        out_specs=out_spec if n_out == 1 else [out_spec] * n_out,
        out_shape=out_shape if n_out == 1 else [out_shape] * n_out,
        scratch_shapes=scratch,
        compiler_params=pltpu.CompilerParams(dimension_semantics=("arbitrary",),
                                             vmem_limit_bytes=VMEM_LIMIT),
        name=name,
    )(*tok_inputs, *const_inputs)


def _zero_at_start(*refs):
    @pl.when(pl.program_id(0) == 0)
    def _():
        for r in refs:
            r[...] = jnp.zeros_like(r)


def _gla_hgrn_kernel(pa_ref, pc_ref, nwa_ref, nwc_ref, ms_ref, mk_ref, ya_ref, yc_ref, sta_s, stc_s,
                     *, tile):
    _zero_at_start(sta_s, stc_s)
    streams = []
    for s in range(pa_ref.shape[0]):
        for p_ref, qk, y_ref, st_s, nw_ref, dk in ((pa_ref, GLA_QK, ya_ref, sta_s, nwa_ref, GLA_DK),
                                                   (pc_ref, HGRN_QF, yc_ref, stc_s, nwc_ref, HGRN_DK)):
            v0, z0, lg0 = 2 * qk, 2 * qk + D_BRANCH, 2 * qk + 2 * D_BRANCH
            streams.append(dict(q=p_ref.at[s, :, 0:qk], k=p_ref.at[s, :, qk:v0],
                                v=p_ref.at[s, :, v0:z0], z=p_ref.at[s, :, z0:lg0],
                                lg=p_ref.at[s, :, lg0:lg0 + qk],
                                y=y_ref.at[s], st=st_s.at[s], nw=nw_ref, dk=dk))
    _gla_chunks(streams, ms_ref, mk_ref, tile=tile)


def _mlstm_constants():
    half = LANES // 2
    full0 = half * MLSTM_HEADS
    e = np.zeros((LANES, full0 + 2 * LANES * MLSTM_HEADS), np.float32)
    diff0 = full0 + LANES * MLSTM_HEADS
    sel = np.zeros((16, LANES), np.float32)
    for h in range(MLSTM_HEADS):
        e[SM_F + h, h * half:(h + 1) * half] = 1.0
        e[SM_F + h, full0 + h * LANES:full0 + (h + 1) * LANES] = 1.0
        e[SM_I + h, diff0 + h * LANES:diff0 + (h + 1) * LANES] = 1.0
        e[SM_F + h, diff0 + h * LANES:diff0 + (h + 1) * LANES] = -1.0
        sel[h // 2, SM_I + h] = 1.0
        sel[h // 2, SM_F + h] = -1.0
    return jnp.asarray(np.concatenate([e, e], axis=0), BF16), jnp.asarray(sel, BF16)


def _mlstm_kernel(pb_ref, sm_ref, nw_ref, tri_ref, exp_ref, sel_ref, y_ref, c_s, m_s, *, tile):
    _zero_at_start(c_s, m_s)
    lane = _lane_iota()
    is_f = (lane >= SM_F) & (lane < SM_F + MLSTM_HEADS)
    streams = [dict(p=pb_ref.at[s], g=sm_ref.at[s], c=c_s.at[s], m=m_s.at[s], y=y_ref.at[s])
               for s in range(pb_ref.shape[0])]

    half = LANES // 2
    ri = lax.broadcasted_iota(jnp.int32, (CHUNK, LANES), 0)
    ci = lax.broadcasted_iota(jnp.int32, (CHUNK, LANES), 1)
    causal2 = (ci & (half - 1)) <= ri
    lo_half = lane < half
    even = (lane & 1) == 0
    v0, o0, z0 = 2 * D_BRANCH, 3 * D_BRANCH, 4 * D_BRANCH
    heads = range(MLSTM_HEADS)
    pairs = range(MLSTM_HEADS // 2)
    full0 = half * MLSTM_HEADS
    diff0 = full0 + LANES * MLSTM_HEADS

    def side_by_side(a, b, zero):
        return jnp.concatenate([jnp.concatenate([a, zero], axis=1),
                                jnp.concatenate([zero, b], axis=1)], axis=0)

    def qk_products(d, rows):
        qk_ref = d["p"]
        qbs = [qk_ref[rows, h * LANES:(h + 1) * LANES].astype(BF16) for h in heads]
        d["ks"] = [qk_ref[rows, D_BRANCH + h * LANES:D_BRANCH + (h + 1) * LANES] for h in heads]
        kbs = [k.astype(BF16) for k in d["ks"]]
        zero = jnp.zeros((CHUNK, LANES), BF16)
        ones = jnp.ones((CHUNK, LANES), BF16)
        d["qk_raw"] = [_dot_nt(jnp.concatenate([qbs[2 * p], qbs[2 * p + 1]], axis=1),
                               side_by_side(kbs[2 * p], kbs[2 * p + 1], zero)) for p in pairs]
        d["qc"] = [_dot(qbs[h], d["c"][h].astype(BF16)) for h in heads]
        d["vaug"] = [jnp.concatenate([d["p"][rows, v0 + h * LANES:v0 + (h + 1) * LANES].astype(BF16),
                                      ones], axis=1) for h in heads]

    def gate_sums(d, rows):
        gc = d["g"][rows, :]
        bcol = _sel_dot(tri_ref[...], gc)
        ib = jnp.where(is_f, bcol, gc)
        d["ex"] = _dot_sel2(ib, exp_ref[...])
        by_parity = jnp.concatenate([jnp.where(even, ib, 0.0), jnp.where(even, 0.0, ib)], axis=0)
        d["drows"] = _sel_dot_nt(sel_ref[...], by_parity)

    def weights(d, rows):
        ex = d["ex"]
        mx = d["m"][0:1, :]
        ss, mrs = [], []
        for p in pairs:
            bx = ex[:, p * LANES:(p + 1) * LANES]
            lw = jnp.where(causal2, bx + d["drows"][p:p + 1, :], NEG_BIG)
            mr0 = jnp.max(jnp.where(lo_half, lw, NEG_BIG), axis=-1, keepdims=True)
            mr1 = jnp.max(jnp.where(lo_half, NEG_BIG, lw), axis=-1, keepdims=True)
            m64 = jnp.where(lo_half, mx[:, 2 * p * LANES:(2 * p + 1) * LANES],
                            mx[:, (2 * p + 1) * LANES:(2 * p + 2) * LANES])
            m_row = jnp.maximum(jnp.where(lo_half, mr0, mr1), bx + m64)
            ss.append((d["qk_raw"][p] * jnp.exp(lw - m_row)).astype(BF16))
            mrs += [mr0, mr1]
        d["ss"], d["mrs"] = ss, mrs
        b_last = ex[CHUNK - 1:CHUNK, full0:diff0]
        lwe = ex[:, diff0:] + b_last
        m_new = jnp.maximum(b_last + mx, jnp.max(lwe, axis=0, keepdims=True))
        d["cd"] = jnp.exp(b_last + mx - m_new)
        d["m_new"] = m_new
        kw = jnp.exp(lwe - m_new)
        d["kws"] = [(d["ks"][h] * kw[:, h * LANES:(h + 1) * LANES]).astype(BF16) for h in heads]

    def numerators(d, rows):
        zero = jnp.zeros((CHUNK, 2 * LANES), BF16)
        d["nums"] = [_dot(d["ss"][p], side_by_side(d["vaug"][2 * p], d["vaug"][2 * p + 1], zero))
                     for p in pairs]
        d["cups"] = [_dot_tn(d["kws"][h], d["vaug"][h]) for h in heads]

    def finish(d, rows):
        ex = d["ex"]
        mx = d["m"][0:1, :]
        for h in heads:
            p, hd = divmod(h, 2)
            hs = slice(h * LANES, (h + 1) * LANES)
            m_inter = ex[:, full0 + h * LANES:full0 + (h + 1) * LANES] + mx[:, hs]
            m_row = jnp.maximum(d["mrs"][h], m_inter)
            inter = jnp.exp(m_inter - m_row)
            sv = d["nums"][p][:, hd * 2 * LANES:(hd + 1) * 2 * LANES]
            num = sv[:, 0:LANES] + inter * d["qc"][h][:, 0:LANES]
            den = sv[:, LANES:] + inter * d["qc"][h][:, LANES:]
            hh = num / jnp.maximum(jnp.abs(den), jnp.exp(-m_row))
            cd = d["cd"][:, hs]
            d["c"][h] = jnp.concatenate([cd, cd], axis=1) * d["c"][h] + d["cups"][h]
            og = jax.nn.sigmoid(d["p"][rows, o0 + h * LANES:o0 + (h + 1) * LANES])
            _norm_gate_store(d["y"], rows, h * LANES, [og * hh], nw_ref,
                             [d["p"][rows, z0 + h * LANES:z0 + (h + 1) * LANES]])
        d["m"][0:1, :] = d["m_new"]

    def body(c, carry):
        rows = pl.ds(pl.multiple_of(c * CHUNK, CHUNK), CHUNK)
        _run_stages([dict(d) for d in streams],
                    (qk_products, gate_sums, weights, numerators, finish), rows)
        return carry

    lax.fori_loop(0, tile // CHUNK, body, 0)


def _ssd_constants():
    e = np.zeros((LANES, SSD_HEADS * SSD_HEAD_DIM), np.float32)
    sel = np.zeros((16, LANES), np.float32)
    for h in range(SSD_HEADS):
        e[SM_DT + h, h * SSD_HEAD_DIM:(h + 1) * SSD_HEAD_DIM] = 1.0
        sel[h // 2, SM_DT + h] = 1.0
    return jnp.asarray(np.concatenate([e, e], axis=0), BF16), jnp.asarray(sel, BF16)


def _ssd_kernel(pd_ref, sm_ref, alog_ref, dx_ref, nw_ref, tri_ref, sel_ref, exp_ref, y_ref, st_s,
                *, tile):
    _zero_at_start(st_s)
    lane = _lane_iota()
    is_dt = (lane >= SM_DT) & (lane < SM_DT + SSD_HEADS)
    streams = [dict(p=pd_ref.at[s], xbc=pd_ref.at[s], dt=sm_ref.at[s], st=st_s.at[s], y=y_ref.at[s])
               for s in range(pd_ref.shape[0])]

    half = LANES // 2
    ri = lax.broadcasted_iota(jnp.int32, (CHUNK, LANES), 0)
    ci = lax.broadcasted_iota(jnp.int32, (CHUNK, LANES), 1)
    causal2 = (ci & (half - 1)) <= ri
    lo_half = lane < half
    even = (lane & 1) == 0
    a_lane = jnp.where(is_dt, -jnp.exp(alog_ref[...]), 0.0)
    b0, c0, z0 = D_BRANCH, D_BRANCH + SSD_BC, D_BRANCH + 2 * SSD_BC
    groups = range(SSD_GROUPS)
    group_w = D_BRANCH // SSD_GROUPS
    pairs = range(D_BRANCH // LANES)
    pairs_per_group = group_w // LANES

    def products(d, rows):
        xbc = d["xbc"]
        d["bgs"] = [xbc[rows, b0 + g * SSD_STATE:b0 + (g + 1) * SSD_STATE].astype(BF16) for g in groups]
        cgbs = [xbc[rows, c0 + g * SSD_STATE:c0 + (g + 1) * SSD_STATE].astype(BF16) for g in groups]
        d["cb2"] = [_dot_nt(cgbs[g], jnp.concatenate([d["bgs"][g], d["bgs"][g]], axis=0))
                    for g in groups]
        d["cst"] = [_dot(cgbs[g], d["st"][g].astype(BF16)) for g in groups]

    def decay_sums(d, rows):
        dt = d["dt"][rows, :]
        acs = _sel_dot(tri_ref[...], dt * a_lane)
        ex = _dot_sel2(jnp.concatenate([dt, acs], axis=0), exp_ref[...])
        d["dtx"], d["acs_x"] = ex[0:CHUNK], ex[CHUNK:]
        by_parity = jnp.concatenate([jnp.where(even, acs, 0.0), jnp.where(even, 0.0, acs)], axis=0)
        d["a_rows"] = _sel_dot_nt(sel_ref[...], by_parity)

    def decays(d, rows):
        d["xss"], d["ms"], d["xblk"], xdecs = [], [], [], []
        for p in pairs:
            ls = slice(p * LANES, (p + 1) * LANES)
            ax = d["acs_x"][:, ls]
            lmat = jnp.exp(jnp.where(causal2, ax - d["a_rows"][p:p + 1, :], NEG_BIG))
            d["ms"].append((d["cb2"][p // pairs_per_group] * lmat).astype(BF16))
            xs = d["xbc"][rows, ls]
            xdt = xs * d["dtx"][:, ls]
            d["xss"].append(xs)
            d["xblk"].append(jnp.concatenate([jnp.where(lo_half, xdt, 0.0), jnp.where(lo_half, 0.0, xdt)],
                                             axis=0).astype(BF16))
            xdecs.append((xdt * jnp.exp(ax[CHUNK - 1:CHUNK, :] - ax)).astype(BF16))
        d["xdec"] = [jnp.concatenate(xdecs[g * pairs_per_group:(g + 1) * pairs_per_group], axis=1)
                     for g in groups]

    def chunk_products(d, rows):
        d["yds"] = [_dot(d["ms"][p], d["xblk"][p]) for p in pairs]
        d["ups"] = [_dot_tn(d["bgs"][g], d["xdec"][g]) for g in groups]

    def finish(d, rows):
        for g in groups:
            gs = slice(g * group_w, (g + 1) * group_w)
            eax = jnp.exp(d["acs_x"][:, gs])
            d["st"][g] = d["st"][g] * eax[CHUNK - 1:CHUNK, :] + d["ups"][g]
            ys = []
            for pp in range(pairs_per_group):
                p = g * pairs_per_group + pp
                ls = slice(p * LANES, (p + 1) * LANES)
                y = (d["cst"][g][:, pp * LANES:(pp + 1) * LANES] * eax[:, pp * LANES:(pp + 1) * LANES]
                     + dx_ref[:, ls] * d["xss"][p] + d["yds"][p])
                ys.append(y * _silu(d["p"][rows, z0 + p * LANES:z0 + (p + 1) * LANES]))
            _norm_gate_store(d["y"], rows, g * group_w, ys, nw_ref, [None] * len(ys))

    def body(c, carry):
        rows = pl.ds(pl.multiple_of(c * CHUNK, CHUNK), CHUNK)
        _run_stages([dict(d) for d in streams],
                    (products, decay_sums, decays, chunk_products, finish), rows)
        return carry

    lax.fori_loop(0, tile // CHUNK, body, 0)


def _pad_lanes(parts, total=LANES):
    width = sum(p.shape[-1] for p in parts)
    lead = parts[0].shape[:-1]
    return jnp.concatenate(list(parts) + [jnp.zeros(lead + (total - width,), parts[0].dtype)], axis=-1)


def _small_vector(i_part, f_part, dt_part):
    z = jnp.zeros((GLA_GATE_RANK,), F32)
    return _pad_lanes([z, i_part.astype(F32), f_part.astype(F32), dt_part.astype(F32)])[None, :]


def kernel(x, norm_w, w_in, gla_gate_w, gla_gate_b, gla_norm_w, ml_conv_w, ml_conv_b, ml_i_b, ml_f_b,
           ml_norm_w, hg_lb_logits, hg_norm_w, ssd_conv_w, ssd_conv_b, ssd_dt_bias, ssd_A_log, ssd_D,
           ssd_norm_w, w_out, final_norm_w):
    n_batch, seq, _ = x.shape
    depth = w_in.shape[0]
    tile = min(MIX_TILE, seq)
    n_tok = n_batch * seq

    mstack, masks = _gla_constants()
    tri_c = _tri(CHUNK)
    ml_exp, ml_sel = _mlstm_constants()
    ssd_exp, ssd_sel = _ssd_constants()
    zero4 = jnp.zeros((MLSTM_HEADS,), F32)
    in_tile = min(IN_TILE, seq)

    p = jax.nn.softmax(hg_lb_logits.astype(F32), axis=0)
    lower_bounds = jnp.cumsum(p, axis=0) - p[0:1]

    split_at = [int(s) for s in np.cumsum(PROJ_SIZES)[:-1]]
    (a_q, a_k, a_v, a_gr, a_z, b_q, b_k, b_v, b_i, b_f, b_o, b_z,
     c_q, c_f, c_i, c_z, d_x, d_b, d_c, d_dt, d_z) = jnp.split(w_in, split_at, axis=-1)
    w_all = jnp.concatenate(
        [a_q, a_k, a_v, a_z, b_q, b_k, b_v, b_o, b_z, c_q, c_f, c_i, c_z, d_x, d_b, d_c, d_z,
         _pad_lanes([a_gr, b_i, b_f, d_dt])], axis=-1).astype(BF16)

    h = x.reshape(n_tok, D_MODEL)
    row2 = lambda v: v.astype(F32).reshape(1, -1)
    for l in range(depth):
        gate_w = jnp.concatenate(
            [gla_gate_w[l].astype(F32), jnp.zeros((LANES - GLA_GATE_RANK, GLA_QK), F32)], axis=0)
        consts = [gate_w, row2(gla_gate_b[l]), row2(lower_bounds[l]),
                  ml_conv_w[l].astype(F32), row2(ml_conv_b[l]),
                  ssd_conv_w[l].astype(F32), row2(ssd_conv_b[l]),
                  _small_vector(ml_i_b[l], ml_f_b[l], ssd_dt_bias[l])]
        pa, pb, pc, pd, sm = [p.reshape(n_batch, seq, -1)
                              for p in _inproj(h, row2(norm_w[l]), w_all, l, consts, in_tile,
                                               seq // in_tile)]

        y_a, y_c = _mixer_call(
            functools.partial(_gla_hgrn_kernel, tile=tile), n_batch, seq, tile, [pa, pc],
            [row2(gla_norm_w[l]), row2(hg_norm_w[l]), mstack, masks],
            [pltpu.VMEM((n_batch, GLA_QK // LANES, GLA_DV, LANES), F32),
             pltpu.VMEM((n_batch, HGRN_QF // LANES, HGRN_DV, LANES), F32)], "gla_hgrn", n_out=2)

        y_b = _mixer_call(
            functools.partial(_mlstm_kernel, tile=tile), n_batch, seq, tile, [pb, sm],
            [row2(ml_norm_w[l]), tri_c, ml_exp, ml_sel],
            [pltpu.VMEM((n_batch, MLSTM_HEADS, MLSTM_DH, 2 * MLSTM_DH), F32),
             pltpu.VMEM((n_batch, 8, MLSTM_HEADS * LANES), F32)], "mlstm")

        y_d = _mixer_call(
            functools.partial(_ssd_kernel, tile=tile), n_batch, seq, tile, [pd, sm],
            [_small_vector(zero4, zero4, ssd_A_log[l]),
             row2(jnp.repeat(ssd_D[l].astype(F32), SSD_HEAD_DIM)), row2(ssd_norm_w[l]),
             tri_c, ssd_sel, ssd_exp],
            [pltpu.VMEM((n_batch, SSD_GROUPS, SSD_STATE, D_BRANCH // SSD_GROUPS), F32)], "ssd")

        ys = [y.reshape(n_tok, D_BRANCH) for y in (y_a, y_b, y_c, y_d)]
        h = _outproj(h, ys, w_out[l].astype(BF16), row2(final_norm_w), l == depth - 1,
                     min(OUT_TILE, n_tok))
    return h.reshape(n_batch, seq, D_MODEL)
```

```python
import functools

import numpy as np
import jax
import jax.numpy as jnp
from jax import lax
from jax.experimental import pallas as pl
from jax.experimental.pallas import tpu as pltpu

F32 = jnp.float32
BF16 = jnp.bfloat16

D_MODEL = 1024
D_BRANCH = 512
EPS = 1e-6
NEG_BIG = -1e30

GLA_HEADS, GLA_DK, GLA_DV = 4, 64, 128
GLA_GATE_RANK, GLA_GATE_NORM = 16, 16.0
MLSTM_HEADS, MLSTM_DH, MLSTM_CONV = 4, 128, 4
HGRN_HEADS, HGRN_DK, HGRN_DV = 4, 128, 128
SSD_HEAD_DIM, SSD_HEADS, SSD_GROUPS, SSD_STATE, SSD_CONV = 64, 8, 2, 128, 4
GLA_QK = GLA_HEADS * GLA_DK
HGRN_QF = HGRN_HEADS * HGRN_DK
SSD_BC = SSD_GROUPS * SSD_STATE
PROJ_SIZES = (
    GLA_QK, GLA_QK, D_BRANCH, GLA_GATE_RANK, D_BRANCH,
    D_BRANCH, D_BRANCH, D_BRANCH, MLSTM_HEADS, MLSTM_HEADS, D_BRANCH, D_BRANCH,
    HGRN_QF, HGRN_QF, D_BRANCH, D_BRANCH,
    D_BRANCH, SSD_BC, SSD_BC, SSD_HEADS, D_BRANCH,
)

LANES = 128
HIST = 8
VMEM_LIMIT = 56 * 1024 * 1024

IN_TILE = 256
MIX_TILE = 256
OUT_TILE = 512

CHUNK = 64
N_LEVELS = 6
MIN_VPU_LEVEL = 4

SM_GR, SM_I, SM_F, SM_DT = 0, 16, 20, 24

W_A = 2 * GLA_QK + 2 * D_BRANCH
W_B = 5 * D_BRANCH
W_C = 2 * HGRN_QF + 2 * D_BRANCH
W_D = 2 * D_BRANCH + 2 * SSD_BC
COL_A = 0
COL_B = COL_A + W_A
COL_C = COL_B + W_B
COL_D = COL_C + W_C
COL_S = COL_D + W_D


def _gla_constants():
    c = CHUNK
    t = np.arange(c)[:, None]
    d = np.arange(c)[None, :]
    blocks = [(d <= t), (d > t)]
    masks = [np.eye(c, dtype=bool)]
    for l in range(N_LEVELS):
        s = c >> (l + 1)
        mid_t = (t // (2 * s)) * (2 * s) + s
        upper = t >= mid_t
        if s < MIN_VPU_LEVEL:
            blocks.append(np.where(upper, (d >= mid_t) & (d <= t), (d > t) & (d <= mid_t - 1)))
        same = (t // (2 * s)) == (d // (2 * s))
        masks.append(same & upper & (d < mid_t))
    mstack = np.concatenate(blocks, axis=0).astype(np.float32)
    mstack = np.concatenate([mstack, mstack], axis=1)
    return jnp.asarray(mstack, BF16), jnp.asarray(np.stack(masks).astype(np.float32))


def _tri(n):
    return jnp.asarray(np.tril(np.ones((n, n), np.float32)), BF16)


def _dot(a, b):
    return jnp.dot(a, b, preferred_element_type=F32)


def _dot_nt(a, b):
    return lax.dot_general(a, b, (((1,), (1,)), ((), ())), preferred_element_type=F32)


def _dot_tn(a, b):
    return lax.dot_general(a, b, (((0,), (0,)), ((), ())), preferred_element_type=F32)


def _split3(x):
    hi = x.astype(BF16)
    r1 = x - hi.astype(F32)
    mid = r1.astype(BF16)
    lo = (r1 - mid.astype(F32)).astype(BF16)
    return hi, mid, lo


def _sel_dot(sel, x):
    hi, mid, lo = _split3(x)
    return _dot(sel, hi) + _dot(sel, mid) + _dot(sel, lo)


def _sel_dot_nt(sel, x):
    hi, mid, lo = _split3(x)
    return _dot_nt(sel, hi) + _dot_nt(sel, mid) + _dot_nt(sel, lo)


def _sel_dot2(sel2, x):
    hi = x.astype(BF16)
    mid = (x - hi.astype(F32)).astype(BF16)
    return _dot(sel2, jnp.concatenate([hi, mid], axis=0))


def _level_exponent(gcs, s):
    pieces = []
    for b in range(0, CHUNK, 2 * s):
        ref_row = gcs[b + s - 1:b + s, :]
        if s >= 8:
            pieces += [ref_row - gcs[b:b + s], gcs[b + s:b + 2 * s] - ref_row]
        else:
            diff = gcs[b:b + 2 * s] - ref_row
            pieces.append(jnp.minimum(diff, -diff))
    return jnp.concatenate(pieces, axis=0)


def _dot_sel2(x, sel2):
    hi = x.astype(BF16)
    mid = (x - hi.astype(F32)).astype(BF16)
    return _dot(jnp.concatenate([hi, mid], axis=1), sel2)


def _dot_f32(a, b):
    ah = a.astype(BF16)
    al = (a - ah.astype(F32)).astype(BF16)
    bh = b.astype(BF16)
    bl = (b - bh.astype(F32)).astype(BF16)
    return _dot(ah, bh) + _dot(al, bh) + _dot(ah, bl)


def _softplus(x):
    return jnp.maximum(x, 0.0) + jnp.log1p(jnp.exp(-jnp.abs(x)))


def _log_sigmoid(x):
    return -_softplus(-x)


def _silu(x):
    return x * jax.nn.sigmoid(x)


def _lane_iota():
    return lax.broadcasted_iota(jnp.int32, (1, LANES), 1)


def _causal_conv_silu(raw, xe_ref, hist_ref, w_ref, b_ref, taps):
    rows = raw.shape[0]
    xe_ref[0:HIST, :] = hist_ref[...]
    xe_ref[HIST:HIST + rows, :] = raw
    hist_ref[...] = raw[rows - HIST:rows, :]
    blk = 64
    out = []
    for r in range(0, rows, blk):
        acc = b_ref[...] + w_ref[taps - 1:taps, :] * xe_ref[HIST + r:HIST + r + blk, :]
        for k in range(taps - 1):
            off = HIST - (taps - 1) + k
            acc = acc + w_ref[k:k + 1, :] * xe_ref[off + r:off + r + blk, :]
        out.append(_silu(acc))
    return out


def _inproj_kernel(x_ref, nw_ref, w_ref, gw_ref, gb_ref, lb_ref, cwb_ref, cbb_ref, cwd_ref, cbd_ref,
                   gbias_ref, oa_ref, ob_ref, oc_ref, od_ref, os_ref, xe_s, hist_b, hist_d,
                   *, tiles_per_seq):
    @pl.when(pl.program_id(0) % tiles_per_seq == 0)
    def _():
        hist_b[...] = jnp.zeros_like(hist_b)
        hist_d[...] = jnp.zeros_like(hist_d)

    x = x_ref[...]
    ms = jnp.mean(x * x, axis=-1, keepdims=True)
    u = (x * lax.rsqrt(ms + EPS) * nw_ref[...]).astype(BF16)
    lane = _lane_iota()
    blk = 64

    small = _dot(u, w_ref[:, COL_S:COL_S + LANES])
    biased = small + gbias_ref[...]
    is_f = (lane >= SM_F) & (lane < SM_F + MLSTM_HEADS)
    is_dt = (lane >= SM_DT) & (lane < SM_DT + SSD_HEADS)
    os_ref[...] = jnp.where(is_f, _log_sigmoid(biased), jnp.where(is_dt, _softplus(biased), biased))

    raw = _dot(u, w_ref[:, COL_B:COL_B + W_B])
    conv_w = 2 * D_BRANCH
    k_scale = jnp.where(lax.broadcasted_iota(jnp.int32, (1, conv_w), 1) < D_BRANCH, 1.0, MLSTM_DH ** -0.5)
    for i, blk_out in enumerate(_causal_conv_silu(raw[:, 0:conv_w], xe_s, hist_b, cwb_ref, cbb_ref,
                                                  MLSTM_CONV)):
        ob_ref[i * blk:(i + 1) * blk, 0:conv_w] = blk_out * k_scale
    ob_ref[:, conv_w:W_B] = raw[:, conv_w:W_B]

    raw = _dot(u, w_ref[:, COL_D:COL_D + W_D])
    conv_w = D_BRANCH + 2 * SSD_BC
    for i, blk_out in enumerate(_causal_conv_silu(raw[:, 0:conv_w], xe_s, hist_d, cwd_ref, cbd_ref,
                                                  SSD_CONV)):
        od_ref[i * blk:(i + 1) * blk, 0:conv_w] = blk_out
    od_ref[:, conv_w:W_D] = raw[:, conv_w:W_D]

    raw = _dot(u, w_ref[:, COL_C:COL_C + W_C])
    lb = lb_ref[...]
    fr = raw[:, HGRN_QF:2 * HGRN_QF]
    oc_ref[:, 0:HGRN_QF] = raw[:, 0:HGRN_QF] * (HGRN_DK ** -0.5)
    oc_ref[:, HGRN_QF:2 * HGRN_QF] = (1.0 - lb) * jax.nn.sigmoid(-fr)
    oc_ref[:, 2 * HGRN_QF:W_C] = raw[:, 2 * HGRN_QF:W_C]
    oc_ref[:, W_C:W_C + HGRN_QF] = jnp.log(jnp.maximum(lb + (1.0 - lb) * jax.nn.sigmoid(fr), 1e-30))

    gate = _dot_f32(small, gw_ref[...]) + gb_ref[...]
    oa_ref[:, W_A:W_A + GLA_QK] = _log_sigmoid(gate) * (1.0 / GLA_GATE_NORM)
    raw = _dot(u, w_ref[:, COL_A:COL_A + W_A])
    oa_ref[:, 0:GLA_QK] = raw[:, 0:GLA_QK] * (GLA_DK ** -0.5)
    oa_ref[:, GLA_QK:W_A] = raw[:, GLA_QK:W_A]


def _inproj(h, norm_w, w_all, layer, consts, tile, tiles_per_seq):
    n_tok = h.shape[0]
    const = lambda i: (0, 0)
    row = lambda i: (i, 0)
    in_specs = [pl.BlockSpec((tile, D_MODEL), row), pl.BlockSpec((1, D_MODEL), const),
                pl.BlockSpec((D_MODEL, w_all.shape[1]), lambda i: (layer, 0),
                             pipeline_mode=pl.Buffered(1))]
    in_specs += [pl.BlockSpec(c.shape, const) for c in consts]
    widths = (W_A + GLA_QK, W_B, W_C + HGRN_QF, W_D, LANES)
    conv_w = 2 * D_BRANCH
    return pl.pallas_call(
        functools.partial(_inproj_kernel, tiles_per_seq=tiles_per_seq),
        grid=(n_tok // tile,), in_specs=in_specs,
        out_specs=[pl.BlockSpec((tile, w), row) for w in widths],
        out_shape=[jax.ShapeDtypeStruct((n_tok, w), F32) for w in widths],
        scratch_shapes=[pltpu.VMEM((tile + HIST, conv_w), F32), pltpu.VMEM((HIST, conv_w), F32),
                        pltpu.VMEM((HIST, conv_w), F32)],
        compiler_params=pltpu.CompilerParams(dimension_semantics=("arbitrary",),
                                             vmem_limit_bytes=VMEM_LIMIT),
        name="inproj",
    )(h, norm_w, w_all, *consts)


def _outproj_kernel(h_ref, ya_ref, yb_ref, yc_ref, yd_ref, w_ref, fw_ref, o_ref, *, final):
    acc = h_ref[...]
    for i, y_ref in enumerate((ya_ref, yb_ref, yc_ref, yd_ref)):
        acc = acc + _dot(y_ref[...], w_ref[i * D_BRANCH:(i + 1) * D_BRANCH, :])
    if final:
        ms = jnp.mean(acc * acc, axis=-1, keepdims=True)
        acc = acc * lax.rsqrt(ms + EPS) * fw_ref[...]
    o_ref[...] = acc


def _outproj(h, ys, w_out, final_w, final, tile):
    n_tok = h.shape[0]
    const = lambda i: (0, 0)
    row = lambda i: (i, 0)
    in_specs = [pl.BlockSpec((tile, D_MODEL), row)]
    in_specs += [pl.BlockSpec((tile, D_BRANCH), row) for _ in ys]
    in_specs += [pl.BlockSpec(w_out.shape, const), pl.BlockSpec((1, D_MODEL), const)]
    return pl.pallas_call(
        functools.partial(_outproj_kernel, final=final),
        grid=(n_tok // tile,), in_specs=in_specs,
        out_specs=pl.BlockSpec((tile, D_MODEL), row),
        out_shape=jax.ShapeDtypeStruct((n_tok, D_MODEL), F32),
        compiler_params=pltpu.CompilerParams(dimension_semantics=("arbitrary",),
                                             vmem_limit_bytes=VMEM_LIMIT),
        name="outproj",
    )(h, *ys, w_out, final_w)


def _norm_gate_store(y_ref, rows, col0, parts, nw_ref, z_parts):
    width = sum(p.shape[-1] for p in parts)
    ss = sum(jnp.sum(p * p, axis=-1, keepdims=True) for p in parts)
    scale = lax.rsqrt(ss * (1.0 / width) + EPS)
    c = col0
    for p, z in zip(parts, z_parts):
        w = p.shape[-1]
        out = p * scale * nw_ref[:, c:c + w]
        if z is not None:
            out = out * _silu(z)
        y_ref[rows, c:c + w] = out.astype(y_ref.dtype)
        c += w


N_STAGES = 5


def _run_chunks(work, tile):
    def body(c, carry):
        rows = pl.ds(pl.multiple_of(c * CHUNK, CHUNK), CHUNK)
        live = [(dict(d), stages) for d, stages in work]
        for k in range(N_STAGES):
            for d, stages in live:
                stages[k](d, rows)
        return carry

    lax.fori_loop(0, tile // CHUNK, body, 0)


def _gla_stages(ms_ref, mk_ref):
    lane = _lane_iota()

    def lane_mask(d, j):
        dk = d["dk"]
        return None if dk == LANES else (lane >= j * dk) & (lane < (j + 1) * dk)

    def pick(a, lm):
        return a if lm is None else jnp.where(lm, a, jnp.zeros_like(a))

    def n_groups(d):
        return d["q"].shape[1] // LANES

    def cumsums(d, rows):
        d["es"] = [_sel_dot2(ms_ref[...], d["lg"][rows, p * 2 * LANES:(p + 1) * 2 * LANES])
                   for p in range(n_groups(d) // 2)]

    def operands(d, rows):
        ops = []
        for g in range(n_groups(d)):
            ls = slice(g * LANES, (g + 1) * LANES)
            e = d["es"][g // 2][:, (g % 2) * LANES:(g % 2 + 1) * LANES]
            q = d["q"][rows, ls]
            k = d["k"][rows, ls]
            gcs = e[0:CHUNK]
            qg = (q * jnp.exp(gcs)).astype(BF16)
            kd = (k * jnp.exp(e[CHUNK:2 * CHUNK])).astype(BF16)
            dec = jnp.exp(gcs[CHUNK - 1:CHUNK, :])
            qb = q.astype(BF16)
            kb = k.astype(BF16)
            ql, kl = [qb], [kb]
            mxu_block = 2
            for l in range(N_LEVELS):
                s = CHUNK >> (l + 1)
                if s >= MIN_VPU_LEVEL:
                    el = _level_exponent(gcs, s)
                else:
                    el = e[mxu_block * CHUNK:(mxu_block + 1) * CHUNK]
                    mxu_block += 1
                w = jnp.exp(el).astype(BF16)
                ql.append(qb * w)
                kl.append(kb * w)
            ops.append((qg, kd, dec, ql, kl))
        d["ops"] = ops

    def scores(d, rows):
        heads = []
        for g in range(n_groups(d)):
            ql, kl = d["ops"][g][3], d["ops"][g][4]
            for j in range(LANES // d["dk"]):
                lm = lane_mask(d, j)
                a = _dot_nt(pick(ql[0], lm), kl[0]) * mk_ref[0]
                for l in range(1, N_LEVELS + 1):
                    a = a + _dot_nt(pick(ql[l], lm), kl[l]) * mk_ref[l]
                heads.append((g, j, a.astype(BF16)))
        d["heads"] = heads

    def outputs(d, rows):
        hp = LANES // d["dk"]
        outs, upds = [], [None] * n_groups(d)
        sts = [d["st"][g] for g in range(n_groups(d))]
        stbs = [st.astype(BF16) for st in sts]
        for g, j, ab in d["heads"]:
            h = g * hp + j
            lm = lane_mask(d, j)
            qg, kd = d["ops"][g][0], d["ops"][g][1]
            vh = d["v"][rows, h * LANES:(h + 1) * LANES].astype(BF16)
            outs.append(_dot(ab, vh) + _dot_nt(pick(qg, lm), stbs[g]))
            u = _dot_tn(vh, kd)
            upds[g] = u if upds[g] is None else jnp.where(lm, u, upds[g])
        d["outs"], d["upds"], d["sts"] = outs, upds, sts

    def finish(d, rows):
        for g in range(n_groups(d)):
            d["st"][g] = d["sts"][g] * d["ops"][g][2] + d["upds"][g]
        for h, o in enumerate(d["outs"]):
            _norm_gate_store(d["y"], rows, h * LANES, [o], d["nw"],
                             [d["z"][rows, h * LANES:(h + 1) * LANES]])

    return cumsums, operands, scores, outputs, finish


def _gla_stream(p_ref, s, qk, dk, y_ref, st_s, nw_ref):
    v0, z0, lg0 = 2 * qk, 2 * qk + D_BRANCH, 2 * qk + 2 * D_BRANCH
    return dict(q=p_ref.at[s, :, 0:qk], k=p_ref.at[s, :, qk:v0], v=p_ref.at[s, :, v0:z0],
                z=p_ref.at[s, :, z0:lg0], lg=p_ref.at[s, :, lg0:lg0 + qk],
                y=y_ref.at[s], st=st_s.at[s], nw=nw_ref, dk=dk)


def _mlstm_constants():
    half = LANES // 2
    full0 = half * MLSTM_HEADS
    e = np.zeros((LANES, full0 + 2 * LANES * MLSTM_HEADS), np.float32)
    diff0 = full0 + LANES * MLSTM_HEADS
    sel = np.zeros((16, LANES), np.float32)
    for h in range(MLSTM_HEADS):
        e[SM_F + h, h * half:(h + 1) * half] = 1.0
        e[SM_F + h, full0 + h * LANES:full0 + (h + 1) * LANES] = 1.0
        e[SM_I + h, diff0 + h * LANES:diff0 + (h + 1) * LANES] = 1.0
        e[SM_F + h, diff0 + h * LANES:diff0 + (h + 1) * LANES] = -1.0
        sel[h // 2, SM_I + h] = 1.0
        sel[h // 2, SM_F + h] = -1.0
    return jnp.asarray(np.concatenate([e, e], axis=0), BF16), jnp.asarray(sel, BF16)


def _mlstm_stages(nw_ref, tri_ref, exp_ref, sel_ref):
    lane = _lane_iota()
    is_f = (lane >= SM_F) & (lane < SM_F + MLSTM_HEADS)

    half = LANES // 2
    ri = lax.broadcasted_iota(jnp.int32, (CHUNK, LANES), 0)
    ci = lax.broadcasted_iota(jnp.int32, (CHUNK, LANES), 1)
    causal2 = (ci & (half - 1)) <= ri
    lo_half = lane < half
    even = (lane & 1) == 0
    v0, o0, z0 = 2 * D_BRANCH, 3 * D_BRANCH, 4 * D_BRANCH
    heads = range(MLSTM_HEADS)
    pairs = range(MLSTM_HEADS // 2)
    full0 = half * MLSTM_HEADS
    diff0 = full0 + LANES * MLSTM_HEADS

    def side_by_side(a, b, zero):
        return jnp.concatenate([jnp.concatenate([a, zero], axis=1),
                                jnp.concatenate([zero, b], axis=1)], axis=0)

    def qk_products(d, rows):
        qk_ref = d["p"]
        qbs = [qk_ref[rows, h * LANES:(h + 1) * LANES].astype(BF16) for h in heads]
        d["ks"] = [qk_ref[rows, D_BRANCH + h * LANES:D_BRANCH + (h + 1) * LANES] for h in heads]
        kbs = [k.astype(BF16) for k in d["ks"]]
        zero = jnp.zeros((CHUNK, LANES), BF16)
        ones = jnp.ones((CHUNK, LANES), BF16)
        d["qk_raw"] = [_dot_nt(jnp.concatenate([qbs[2 * p], qbs[2 * p + 1]], axis=1),
                               side_by_side(kbs[2 * p], kbs[2 * p + 1], zero)) for p in pairs]
        d["qc"] = [_dot(qbs[h], d["c"][h].astype(BF16)) for h in heads]
        d["vaug"] = [jnp.concatenate([d["p"][rows, v0 + h * LANES:v0 + (h + 1) * LANES].astype(BF16),
                                      ones], axis=1) for h in heads]

    def gate_sums(d, rows):
        gc = d["g"][rows, :]
        bcol = _sel_dot(tri_ref[...], gc)
        ib = jnp.where(is_f, bcol, gc)
        d["ex"] = _dot_sel2(ib, exp_ref[...])
        by_parity = jnp.concatenate([jnp.where(even, ib, 0.0), jnp.where(even, 0.0, ib)], axis=0)
        d["drows"] = _sel_dot_nt(sel_ref[...], by_parity)

    def weights(d, rows):
        ex = d["ex"]
        mx = d["m"][0:1, :]
        ss, mrs = [], []
        for p in pairs:
            bx = ex[:, p * LANES:(p + 1) * LANES]
            lw = jnp.where(causal2, bx + d["drows"][p:p + 1, :], NEG_BIG)
            mr0 = jnp.max(jnp.where(lo_half, lw, NEG_BIG), axis=-1, keepdims=True)
            mr1 = jnp.max(jnp.where(lo_half, NEG_BIG, lw), axis=-1, keepdims=True)
            m64 = jnp.where(lo_half, mx[:, 2 * p * LANES:(2 * p + 1) * LANES],
                            mx[:, (2 * p + 1) * LANES:(2 * p + 2) * LANES])
            m_row = jnp.maximum(jnp.where(lo_half, mr0, mr1), bx + m64)
            ss.append((d["qk_raw"][p] * jnp.exp(lw - m_row)).astype(BF16))
            mrs += [mr0, mr1]
        d["ss"], d["mrs"] = ss, mrs
        b_last = ex[CHUNK - 1:CHUNK, full0:diff0]
        lwe = ex[:, diff0:] + b_last
        m_new = jnp.maximum(b_last + mx, jnp.max(lwe, axis=0, keepdims=True))
        d["cd"] = jnp.exp(b_last + mx - m_new)
        d["m_new"] = m_new
        kw = jnp.exp(lwe - m_new)
        d["kws"] = [(d["ks"][h] * kw[:, h * LANES:(h + 1) * LANES]).astype(BF16) for h in heads]

    def numerators(d, rows):
        zero = jnp.zeros((CHUNK, 2 * LANES), BF16)
        d["nums"] = [_dot(d["ss"][p], side_by_side(d["vaug"][2 * p], d["vaug"][2 * p + 1], zero))
                     for p in pairs]
        d["cups"] = [_dot_tn(d["kws"][h], d["vaug"][h]) for h in heads]

    def finish(d, rows):
        ex = d["ex"]
        mx = d["m"][0:1, :]
        for h in heads:
            p, hd = divmod(h, 2)
            hs = slice(h * LANES, (h + 1) * LANES)
            m_inter = ex[:, full0 + h * LANES:full0 + (h + 1) * LANES] + mx[:, hs]
            m_row = jnp.maximum(d["mrs"][h], m_inter)
            inter = jnp.exp(m_inter - m_row)
            sv = d["nums"][p][:, hd * 2 * LANES:(hd + 1) * 2 * LANES]
            num = sv[:, 0:LANES] + inter * d["qc"][h][:, 0:LANES]
            den = sv[:, LANES:] + inter * d["qc"][h][:, LANES:]
            hh = num / jnp.maximum(jnp.abs(den), jnp.exp(-m_row))
            cd = d["cd"][:, hs]
            d["c"][h] = jnp.concatenate([cd, cd], axis=1) * d["c"][h] + d["cups"][h]
            og = jax.nn.sigmoid(d["p"][rows, o0 + h * LANES:o0 + (h + 1) * LANES])
            _norm_gate_store(d["y"], rows, h * LANES, [og * hh], nw_ref,
                             [d["p"][rows, z0 + h * LANES:z0 + (h + 1) * LANES]])
        d["m"][0:1, :] = d["m_new"]

    return qk_products, gate_sums, weights, numerators, finish


def _ssd_constants():
    e = np.zeros((LANES, SSD_HEADS * SSD_HEAD_DIM), np.float32)
    sel = np.zeros((16, LANES), np.float32)
    for h in range(SSD_HEADS):
        e[SM_DT + h, h * SSD_HEAD_DIM:(h + 1) * SSD_HEAD_DIM] = 1.0
        sel[h // 2, SM_DT + h] = 1.0
    return jnp.asarray(np.concatenate([e, e], axis=0), BF16), jnp.asarray(sel, BF16)


def _ssd_stages(alog_ref, dx_ref, nw_ref, tri_ref, sel_ref, exp_ref):
    lane = _lane_iota()
    is_dt = (lane >= SM_DT) & (lane < SM_DT + SSD_HEADS)

    half = LANES // 2
    ri = lax.broadcasted_iota(jnp.int32, (CHUNK, LANES), 0)
    ci = lax.broadcasted_iota(jnp.int32, (CHUNK, LANES), 1)
    causal2 = (ci & (half - 1)) <= ri
    lo_half = lane < half
    even = (lane & 1) == 0
    a_lane = jnp.where(is_dt, -jnp.exp(alog_ref[...]), 0.0)
    b0, c0, z0 = D_BRANCH, D_BRANCH + SSD_BC, D_BRANCH + 2 * SSD_BC
    groups = range(SSD_GROUPS)
    group_w = D_BRANCH // SSD_GROUPS
    pairs = range(D_BRANCH // LANES)
    pairs_per_group = group_w // LANES

    def products(d, rows):
        xbc = d["p"]
        d["bgs"] = [xbc[rows, b0 + g * SSD_STATE:b0 + (g + 1) * SSD_STATE].astype(BF16) for g in groups]
        cgbs = [xbc[rows, c0 + g * SSD_STATE:c0 + (g + 1) * SSD_STATE].astype(BF16) for g in groups]
        d["cb2"] = [_dot_nt(cgbs[g], jnp.concatenate([d["bgs"][g], d["bgs"][g]], axis=0))
                    for g in groups]
        d["cst"] = [_dot(cgbs[g], d["st"][g].astype(BF16)) for g in groups]

    def decay_sums(d, rows):
        dt = d["dt"][rows, :]
        acs = _sel_dot(tri_ref[...], dt * a_lane)
        ex = _dot_sel2(jnp.concatenate([dt, acs], axis=0), exp_ref[...])
        d["dtx"], d["acs_x"] = ex[0:CHUNK], ex[CHUNK:]
        by_parity = jnp.concatenate([jnp.where(even, acs, 0.0), jnp.where(even, 0.0, acs)], axis=0)
        d["a_rows"] = _sel_dot_nt(sel_ref[...], by_parity)

    def decays(d, rows):
        d["xss"], d["ms"], d["xblk"], xdecs = [], [], [], []
        for p in pairs:
            ls = slice(p * LANES, (p + 1) * LANES)
            ax = d["acs_x"][:, ls]
            lmat = jnp.exp(jnp.where(causal2, ax - d["a_rows"][p:p + 1, :], NEG_BIG))
            d["ms"].append((d["cb2"][p // pairs_per_group] * lmat).astype(BF16))
            xs = d["p"][rows, ls]
            xdt = xs * d["dtx"][:, ls]
            d["xss"].append(xs)
            d["xblk"].append(jnp.concatenate([jnp.where(lo_half, xdt, 0.0), jnp.where(lo_half, 0.0, xdt)],
                                             axis=0).astype(BF16))
            xdecs.append((xdt * jnp.exp(ax[CHUNK - 1:CHUNK, :] - ax)).astype(BF16))
        d["xdec"] = [jnp.concatenate(xdecs[g * pairs_per_group:(g + 1) * pairs_per_group], axis=1)
                     for g in groups]

    def chunk_products(d, rows):
        d["yds"] = [_dot(d["ms"][p], d["xblk"][p]) for p in pairs]
        d["ups"] = [_dot_tn(d["bgs"][g], d["xdec"][g]) for g in groups]

    def finish(d, rows):
        for g in groups:
            gs = slice(g * group_w, (g + 1) * group_w)
            eax = jnp.exp(d["acs_x"][:, gs])
            d["st"][g] = d["st"][g] * eax[CHUNK - 1:CHUNK, :] + d["ups"][g]
            ys = []
            for pp in range(pairs_per_group):
                p = g * pairs_per_group + pp
                ls = slice(p * LANES, (p + 1) * LANES)
                y = (d["cst"][g][:, pp * LANES:(pp + 1) * LANES] * eax[:, pp * LANES:(pp + 1) * LANES]
                     + dx_ref[:, ls] * d["xss"][p] + d["yds"][p])
                ys.append(y * _silu(d["p"][rows, z0 + p * LANES:z0 + (p + 1) * LANES]))
            _norm_gate_store(d["y"], rows, g * group_w, ys, nw_ref, [None] * len(ys))

    return products, decay_sums, decays, chunk_products, finish


def _mixers_kernel(pa_ref, pb_ref, pc_ref, pd_ref, sm_ref, nwa_ref, nwb_ref, nwc_ref, nwd_ref,
                   ms_ref, mk_ref, tri_ref, mexp_ref, msel_ref, alog_ref, dx_ref, dsel_ref, dexp_ref,
                   ya_ref, yb_ref, yc_ref, yd_ref, sta_s, stc_s, cb_s, mb_s, std_s, *, tile):
    @pl.when(pl.program_id(0) == 0)
    def _():
        for r in (sta_s, stc_s, cb_s, mb_s, std_s):
            r[...] = jnp.zeros_like(r)

    gla = _gla_stages(ms_ref, mk_ref)
    mlstm = _mlstm_stages(nwb_ref, tri_ref, mexp_ref, msel_ref)
    ssd = _ssd_stages(alog_ref, dx_ref, nwd_ref, tri_ref, dsel_ref, dexp_ref)
    work = []
    for s in range(pa_ref.shape[0]):
        work.append((_gla_stream(pa_ref, s, GLA_QK, GLA_DK, ya_ref, sta_s, nwa_ref), gla))
        work.append((dict(p=pb_ref.at[s], g=sm_ref.at[s], c=cb_s.at[s], m=mb_s.at[s], y=yb_ref.at[s]),
                     mlstm))
        work.append((_gla_stream(pc_ref, s, HGRN_QF, HGRN_DK, yc_ref, stc_s, nwc_ref), gla))
        work.append((dict(p=pd_ref.at[s], dt=sm_ref.at[s], st=std_s.at[s], y=yd_ref.at[s]), ssd))
    _run_chunks(work, tile)


def _mixers(tok_inputs, const_inputs, n_batch, seq, tile):
    tok = lambda t: (0, t, 0)
    in_specs = [pl.BlockSpec((n_batch, tile, a.shape[2]), tok) for a in tok_inputs]
    for a in const_inputs:
        in_specs.append(pl.BlockSpec(a.shape, lambda t, nd=a.ndim: (0,) * nd))
    scratch = [pltpu.VMEM((n_batch, GLA_QK // LANES, GLA_DV, LANES), F32),
               pltpu.VMEM((n_batch, HGRN_QF // LANES, HGRN_DV, LANES), F32),
               pltpu.VMEM((n_batch, MLSTM_HEADS, MLSTM_DH, 2 * MLSTM_DH), F32),
               pltpu.VMEM((n_batch, 8, MLSTM_HEADS * LANES), F32),
               pltpu.VMEM((n_batch, SSD_GROUPS, SSD_STATE, D_BRANCH // SSD_GROUPS), F32)]
    return pl.pallas_call(
        functools.partial(_mixers_kernel, tile=tile), grid=(seq // tile,), in_specs=in_specs,
        out_specs=[pl.BlockSpec((n_batch, tile, D_BRANCH), tok)] * 4,
        out_shape=[jax.ShapeDtypeStruct((n_batch, seq, D_BRANCH), BF16)] * 4,
        scratch_shapes=scratch,
        compiler_params=pltpu.CompilerParams(dimension_semantics=("arbitrary",),
                                             vmem_limit_bytes=VMEM_LIMIT),
        name="mixers",
    )(*tok_inputs, *const_inputs)


def _pad_lanes(parts, total=LANES):
    width = sum(p.shape[-1] for p in parts)
    lead = parts[0].shape[:-1]
    return jnp.concatenate(list(parts) + [jnp.zeros(lead + (total - width,), parts[0].dtype)], axis=-1)


def _small_vector(i_part, f_part, dt_part):
    z = jnp.zeros((GLA_GATE_RANK,), F32)
    return _pad_lanes([z, i_part.astype(F32), f_part.astype(F32), dt_part.astype(F32)])[None, :]


def kernel(x, norm_w, w_in, gla_gate_w, gla_gate_b, gla_norm_w, ml_conv_w, ml_conv_b, ml_i_b, ml_f_b,
           ml_norm_w, hg_lb_logits, hg_norm_w, ssd_conv_w, ssd_conv_b, ssd_dt_bias, ssd_A_log, ssd_D,
           ssd_norm_w, w_out, final_norm_w):
    n_batch, seq, _ = x.shape
    depth = w_in.shape[0]
    tile = min(MIX_TILE, seq)
    n_tok = n_batch * seq

    mstack, masks = _gla_constants()
    tri_c = _tri(CHUNK)
    ml_exp, ml_sel = _mlstm_constants()
    ssd_exp, ssd_sel = _ssd_constants()
    zero4 = jnp.zeros((MLSTM_HEADS,), F32)
    in_tile = min(IN_TILE, seq)

    p = jax.nn.softmax(hg_lb_logits.astype(F32), axis=0)
    lower_bounds = jnp.cumsum(p, axis=0) - p[0:1]

    split_at = [int(s) for s in np.cumsum(PROJ_SIZES)[:-1]]
    (a_q, a_k, a_v, a_gr, a_z, b_q, b_k, b_v, b_i, b_f, b_o, b_z,
     c_q, c_f, c_i, c_z, d_x, d_b, d_c, d_dt, d_z) = jnp.split(
         w_in.astype(BF16).reshape(depth * D_MODEL, -1), split_at, axis=-1)
    w_all = jnp.concatenate(
        [a_q, a_k, a_v, a_z, b_q, b_k, b_v, b_o, b_z, c_q, c_f, c_i, c_z, d_x, d_b, d_c, d_z,
         _pad_lanes([a_gr, b_i, b_f, d_dt])], axis=-1)

    h = x.reshape(n_tok, D_MODEL)
    row2 = lambda v: v.astype(F32).reshape(1, -1)
    for l in range(depth):
        gate_w = jnp.concatenate(
            [gla_gate_w[l].astype(F32), jnp.zeros((LANES - GLA_GATE_RANK, GLA_QK), F32)], axis=0)
        consts = [gate_w, row2(gla_gate_b[l]), row2(lower_bounds[l]),
                  ml_conv_w[l].astype(F32), row2(ml_conv_b[l]),
                  ssd_conv_w[l].astype(F32), row2(ssd_conv_b[l]),
                  _small_vector(ml_i_b[l], ml_f_b[l], ssd_dt_bias[l])]
        pa, pb, pc, pd, sm = [p.reshape(n_batch, seq, -1)
                              for p in _inproj(h, row2(norm_w[l]), w_all, l, consts, in_tile,
                                               seq // in_tile)]

        ys = _mixers(
            [pa, pb, pc, pd, sm],
            [row2(gla_norm_w[l]), row2(ml_norm_w[l]), row2(hg_norm_w[l]), row2(ssd_norm_w[l]),
             mstack, masks, tri_c, ml_exp, ml_sel, _small_vector(zero4, zero4, ssd_A_log[l]),
             row2(jnp.repeat(ssd_D[l].astype(F32), SSD_HEAD_DIM)), ssd_sel, ssd_exp],
            n_batch, seq, tile)
        ys = [y.reshape(n_tok, D_BRANCH) for y in ys]
        h = _outproj(h, ys, w_out[l].astype(BF16), row2(final_norm_w), l == depth - 1,
                     min(OUT_TILE, n_tok))
    return h.reshape(n_batch, seq, D_MODEL)
```

```python
import functools

import numpy as np
import jax
import jax.numpy as jnp
from jax import lax
from jax.experimental import pallas as pl
from jax.experimental.pallas import tpu as pltpu

F32 = jnp.float32
BF16 = jnp.bfloat16

D_MODEL = 1024
D_BRANCH = 512
EPS = 1e-6
NEG_BIG = -1e30

GLA_HEADS, GLA_DK, GLA_DV = 4, 64, 128
GLA_GATE_RANK, GLA_GATE_NORM = 16, 16.0
MLSTM_HEADS, MLSTM_DH, MLSTM_CONV = 4, 128, 4
HGRN_HEADS, HGRN_DK, HGRN_DV = 4, 128, 128
SSD_HEAD_DIM, SSD_HEADS, SSD_GROUPS, SSD_STATE, SSD_CONV = 64, 8, 2, 128, 4
GLA_QK = GLA_HEADS * GLA_DK
HGRN_QF = HGRN_HEADS * HGRN_DK
SSD_BC = SSD_GROUPS * SSD_STATE
PROJ_SIZES = (
    GLA_QK, GLA_QK, D_BRANCH, GLA_GATE_RANK, D_BRANCH,
    D_BRANCH, D_BRANCH, D_BRANCH, MLSTM_HEADS, MLSTM_HEADS, D_BRANCH, D_BRANCH,
    HGRN_QF, HGRN_QF, D_BRANCH, D_BRANCH,
    D_BRANCH, SSD_BC, SSD_BC, SSD_HEADS, D_BRANCH,
)

LANES = 128
HIST = 8
VMEM_LIMIT = 56 * 1024 * 1024

REALIGN_ROWS = 256
IN_TILE = 256
MIX_TILE = 256
OUT_TILE = 512

CHUNK = 64
N_LEVELS = 6
MIN_VPU_LEVEL = 4

SM_GR, SM_I, SM_F, SM_DT = 0, 16, 20, 24

W_A = 2 * GLA_QK + 2 * D_BRANCH
W_B = 5 * D_BRANCH
W_C = 2 * HGRN_QF + 2 * D_BRANCH
W_D = 2 * D_BRANCH + 2 * SSD_BC
COL_A = 0
COL_B = COL_A + W_A
COL_C = COL_B + W_B
COL_D = COL_C + W_C
COL_S = COL_D + W_D


def _gla_constants():
    c = CHUNK
    t = np.arange(c)[:, None]
    d = np.arange(c)[None, :]
    blocks = [(d <= t), (d > t)]
    masks = [np.eye(c, dtype=bool)]
    for l in range(N_LEVELS):
        s = c >> (l + 1)
        mid_t = (t // (2 * s)) * (2 * s) + s
        upper = t >= mid_t
        if s < MIN_VPU_LEVEL:
            blocks.append(np.where(upper, (d >= mid_t) & (d <= t), (d > t) & (d <= mid_t - 1)))
        same = (t // (2 * s)) == (d // (2 * s))
        masks.append(same & upper & (d < mid_t))
    mstack = np.concatenate(blocks, axis=0).astype(np.float32)
    mstack = np.concatenate([mstack, mstack], axis=1)
    masks = np.stack(masks).astype(np.float32)
    return jnp.asarray(mstack, BF16), jnp.asarray(np.concatenate([masks, masks], axis=-1))


def _tri(n):
    return jnp.asarray(np.tril(np.ones((n, n), np.float32)), BF16)


def _dot(a, b):
    return jnp.dot(a, b, preferred_element_type=F32)


def _dot_nt(a, b):
    return lax.dot_general(a, b, (((1,), (1,)), ((), ())), preferred_element_type=F32)


def _dot_tn(a, b):
    return lax.dot_general(a, b, (((0,), (0,)), ((), ())), preferred_element_type=F32)


def _split3(x):
    hi = x.astype(BF16)
    r1 = x - hi.astype(F32)
    mid = r1.astype(BF16)
    lo = (r1 - mid.astype(F32)).astype(BF16)
    return hi, mid, lo


def _sel_dot(sel, x):
    hi, mid, lo = _split3(x)
    return _dot(sel, hi) + _dot(sel, mid) + _dot(sel, lo)


def _sel_dot_nt(sel, x):
    hi, mid, lo = _split3(x)
    return _dot_nt(sel, hi) + _dot_nt(sel, mid) + _dot_nt(sel, lo)


def _sel_dot2(sel2, x):
    hi = x.astype(BF16)
    mid = (x - hi.astype(F32)).astype(BF16)
    return _dot(sel2, jnp.concatenate([hi, mid], axis=0))


def _level_exponent(gcs, s):
    pieces = []
    for b in range(0, CHUNK, 2 * s):
        ref_row = gcs[b + s - 1:b + s, :]
        if s >= 8:
            pieces += [ref_row - gcs[b:b + s], gcs[b + s:b + 2 * s] - ref_row]
        else:
            diff = gcs[b:b + 2 * s] - ref_row
            pieces.append(jnp.minimum(diff, -diff))
    return jnp.concatenate(pieces, axis=0)


def _dot_sel2(x, sel2):
    hi = x.astype(BF16)
    mid = (x - hi.astype(F32)).astype(BF16)
    return _dot(jnp.concatenate([hi, mid], axis=1), sel2)


def _dot_f32(a, b):
    ah = a.astype(BF16)
    al = (a - ah.astype(F32)).astype(BF16)
    bh = b.astype(BF16)
    bl = (b - bh.astype(F32)).astype(BF16)
    return _dot(ah, bh) + _dot(al, bh) + _dot(ah, bl)


def _softplus(x):
    return jnp.maximum(x, 0.0) + jnp.log1p(jnp.exp(-jnp.abs(x)))


def _log_sigmoid(x):
    return -_softplus(-x)


def _silu(x):
    return x * jax.nn.sigmoid(x)


def _lane_iota():
    return lax.broadcasted_iota(jnp.int32, (1, LANES), 1)


def _side_by_side(a, b, zero):
    return jnp.concatenate([jnp.concatenate([a, zero], axis=1),
                            jnp.concatenate([zero, b], axis=1)], axis=0)


def _wide_segments():
    segs, src, dst = [], 0, 0
    run_start = None
    for size in PROJ_SIZES + (0,):
        wide = size >= LANES
        if wide and run_start is None:
            run_start = src
        if not wide and run_start is not None:
            segs.append((run_start, dst, src - run_start))
            dst += src - run_start
            run_start = None
        src += size
    return segs


def _realign_kernel(w_ref, o_ref):
    piece = 4 * LANES
    for src, dst, width in _wide_segments():
        for c in range(0, width, piece):
            n = min(piece, width - c)
            o_ref[:, dst + c:dst + c + n] = w_ref[:, src + c:src + c + n].astype(BF16)


def _realign(w2d, rows):
    n = w2d.shape[0]
    return pl.pallas_call(
        _realign_kernel, grid=(n // rows,),
        in_specs=[pl.BlockSpec((rows, w2d.shape[1]), lambda i: (i, 0))],
        out_specs=pl.BlockSpec((rows, COL_S), lambda i: (i, 0)),
        out_shape=jax.ShapeDtypeStruct((n, COL_S), BF16),
        compiler_params=pltpu.CompilerParams(dimension_semantics=("arbitrary",),
                                             vmem_limit_bytes=VMEM_LIMIT),
        name="realign",
    )(w2d)


def _causal_conv_silu(raw, xe_ref, hist_ref, w_ref, b_ref, taps):
    rows = raw.shape[0]
    xe_ref[0:HIST, :] = hist_ref[...]
    xe_ref[HIST:HIST + rows, :] = raw
    hist_ref[...] = raw[rows - HIST:rows, :]
    blk = 64
    out = []
    for r in range(0, rows, blk):
        acc = b_ref[...] + w_ref[taps - 1:taps, :] * xe_ref[HIST + r:HIST + r + blk, :]
        for k in range(taps - 1):
            off = HIST - (taps - 1) + k
            acc = acc + w_ref[k:k + 1, :] * xe_ref[off + r:off + r + blk, :]
        out.append(_silu(acc))
    return out


def _inproj_kernel(x_ref, nw_ref, w_ref, ws_ref, gw_ref, gb_ref, lb_ref, cwb_ref, cbb_ref, cwd_ref, cbd_ref,
                   gbias_ref, oa_ref, ob_ref, oc_ref, od_ref, os_ref, xe_s, hist_b, hist_d,
                   *, tiles_per_seq):
    @pl.when(pl.program_id(0) % tiles_per_seq == 0)
    def _():
        hist_b[...] = jnp.zeros_like(hist_b)
        hist_d[...] = jnp.zeros_like(hist_d)

    x = x_ref[...]
    ms = jnp.mean(x * x, axis=-1, keepdims=True)
    u = (x * lax.rsqrt(ms + EPS) * nw_ref[...]).astype(BF16)
    lane = _lane_iota()
    blk = 64

    small = _dot(u, ws_ref[...])
    biased = small + gbias_ref[...]
    is_f = (lane >= SM_F) & (lane < SM_F + MLSTM_HEADS)
    is_dt = (lane >= SM_DT) & (lane < SM_DT + SSD_HEADS)
    os_ref[...] = jnp.where(is_f, _log_sigmoid(biased), jnp.where(is_dt, _softplus(biased), biased))

    raw = _dot(u, w_ref[:, COL_B:COL_B + W_B])
    conv_w = 2 * D_BRANCH
    k_scale = jnp.where(lax.broadcasted_iota(jnp.int32, (1, conv_w), 1) < D_BRANCH, 1.0, MLSTM_DH ** -0.5)
    for i, blk_out in enumerate(_causal_conv_silu(raw[:, 0:conv_w], xe_s, hist_b, cwb_ref, cbb_ref,
                                                  MLSTM_CONV)):
        ob_ref[i * blk:(i + 1) * blk, 0:conv_w] = blk_out * k_scale
    o0, z0 = 3 * D_BRANCH, 4 * D_BRANCH
    ob_ref[:, conv_w:o0] = raw[:, conv_w:o0]
    ob_ref[:, o0:z0] = jax.nn.sigmoid(raw[:, o0:z0])
    ob_ref[:, z0:W_B] = _silu(raw[:, z0:W_B])

    raw = _dot(u, w_ref[:, COL_D:COL_D + W_D])
    conv_w = D_BRANCH + 2 * SSD_BC
    for i, blk_out in enumerate(_causal_conv_silu(raw[:, 0:conv_w], xe_s, hist_d, cwd_ref, cbd_ref,
                                                  SSD_CONV)):
        od_ref[i * blk:(i + 1) * blk, 0:conv_w] = blk_out
    od_ref[:, conv_w:W_D] = _silu(raw[:, conv_w:W_D])

    raw = _dot(u, w_ref[:, COL_C:COL_C + W_C])
    lb = lb_ref[...]
    fr = raw[:, HGRN_QF:2 * HGRN_QF]
    oc_ref[:, 0:HGRN_QF] = raw[:, 0:HGRN_QF] * (HGRN_DK ** -0.5)
    oc_ref[:, HGRN_QF:2 * HGRN_QF] = (1.0 - lb) * jax.nn.sigmoid(-fr)
    z0 = 2 * HGRN_QF + D_BRANCH
    oc_ref[:, 2 * HGRN_QF:z0] = raw[:, 2 * HGRN_QF:z0]
    oc_ref[:, z0:W_C] = _silu(raw[:, z0:W_C])
    oc_ref[:, W_C:W_C + HGRN_QF] = jnp.log(jnp.maximum(lb + (1.0 - lb) * jax.nn.sigmoid(fr), 1e-30))

    gate = _dot_f32(small, gw_ref[...]) + gb_ref[...]
    oa_ref[:, W_A:W_A + GLA_QK] = _log_sigmoid(gate) * (1.0 / GLA_GATE_NORM)
    raw = _dot(u, w_ref[:, COL_A:COL_A + W_A])
    oa_ref[:, 0:GLA_QK] = raw[:, 0:GLA_QK] * (GLA_DK ** -0.5)
    z0 = 2 * GLA_QK + D_BRANCH
    oa_ref[:, GLA_QK:z0] = raw[:, GLA_QK:z0]
    oa_ref[:, z0:W_A] = _silu(raw[:, z0:W_A])


def _inproj(h, norm_w, w_wide, w_small, layer, consts, tile, tiles_per_seq):
    n_tok = h.shape[0]
    const = lambda i: (0, 0)
    row = lambda i: (i, 0)
    in_specs = [pl.BlockSpec((tile, D_MODEL), row), pl.BlockSpec((1, D_MODEL), const),
                pl.BlockSpec((D_MODEL, COL_S), lambda i: (layer, 0), pipeline_mode=pl.Buffered(1)),
                pl.BlockSpec((D_MODEL, LANES), lambda i: (layer, 0))]
    in_specs += [pl.BlockSpec(c.shape, const) for c in consts]
    widths = (W_A + GLA_QK, W_B, W_C + HGRN_QF, W_D, LANES)
    conv_w = 2 * D_BRANCH
    return pl.pallas_call(
        functools.partial(_inproj_kernel, tiles_per_seq=tiles_per_seq),
        grid=(n_tok // tile,), in_specs=in_specs,
        out_specs=[pl.BlockSpec((tile, w), row) for w in widths],
        out_shape=[jax.ShapeDtypeStruct((n_tok, w), F32) for w in widths],
        scratch_shapes=[pltpu.VMEM((tile + HIST, conv_w), F32), pltpu.VMEM((HIST, conv_w), F32),
                        pltpu.VMEM((HIST, conv_w), F32)],
        compiler_params=pltpu.CompilerParams(dimension_semantics=("arbitrary",),
                                             vmem_limit_bytes=VMEM_LIMIT),
        name="inproj",
    )(h, norm_w, w_wide, w_small, *consts)


def _outproj_kernel(h_ref, ya_ref, yb_ref, yc_ref, yd_ref, w_ref, fw_ref, o_ref, *, final):
    acc = h_ref[...]
    for i, y_ref in enumerate((ya_ref, yb_ref, yc_ref, yd_ref)):
        acc = acc + _dot(y_ref[...], w_ref[i * D_BRANCH:(i + 1) * D_BRANCH, :])
    if final:
        ms = jnp.mean(acc * acc, axis=-1, keepdims=True)
        acc = acc * lax.rsqrt(ms + EPS) * fw_ref[...]
    o_ref[...] = acc


def _outproj(h, ys, w_out, final_w, final, tile):
    n_tok = h.shape[0]
    const = lambda i: (0, 0)
    row = lambda i: (i, 0)
    in_specs = [pl.BlockSpec((tile, D_MODEL), row)]
    in_specs += [pl.BlockSpec((tile, D_BRANCH), row) for _ in ys]
    in_specs += [pl.BlockSpec(w_out.shape, const), pl.BlockSpec((1, D_MODEL), const)]
    return pl.pallas_call(
        functools.partial(_outproj_kernel, final=final),
        grid=(n_tok // tile,), in_specs=in_specs,
        out_specs=pl.BlockSpec((tile, D_MODEL), row),
        out_shape=jax.ShapeDtypeStruct((n_tok, D_MODEL), F32),
        compiler_params=pltpu.CompilerParams(dimension_semantics=("arbitrary",),
                                             vmem_limit_bytes=VMEM_LIMIT),
        name="outproj",
    )(h, *ys, w_out, final_w)


def _norm_gate_store(y_ref, rows, col0, parts, nw_ref, z_parts):
    width = sum(p.shape[-1] for p in parts)
    ss = sum(jnp.sum(p * p, axis=-1, keepdims=True) for p in parts)
    scale = lax.rsqrt(ss * (1.0 / width) + EPS)
    c = col0
    for p, z in zip(parts, z_parts):
        w = p.shape[-1]
        out = p * scale * nw_ref[:, c:c + w]
        if z is not None:
            out = out * z
        y_ref[rows, c:c + w] = out.astype(y_ref.dtype)
        c += w


N_STAGES = 5
STAGE_SKEW = (0,)


def _run_chunks(work, tile):
    def body(c, carry):
        rows = pl.ds(pl.multiple_of(c * CHUNK, CHUNK), CHUNK)
        live = [(dict(d), stages, STAGE_SKEW[i % len(STAGE_SKEW)]) for i, (d, stages) in enumerate(work)]
        for slot in range(N_STAGES + max(STAGE_SKEW)):
            for d, stages, skew in live:
                if 0 <= slot - skew < N_STAGES:
                    stages[slot - skew](d, rows)
        return carry

    lax.fori_loop(0, tile // CHUNK, body, 0)


def _gla_stages(ms_ref, mk_ref):
    lane = _lane_iota()

    def lane_mask(d, j):
        dk = d["dk"]
        return None if dk == LANES else (lane >= j * dk) & (lane < (j + 1) * dk)

    def pick(a, lm):
        return a if lm is None else jnp.where(lm, a, jnp.zeros_like(a))

    def n_groups(d):
        return d["q"].shape[1] // LANES

    def cumsums(d, rows):
        d["es"] = [_sel_dot2(ms_ref[...], d["lg"][rows, p * 2 * LANES:(p + 1) * 2 * LANES])
                   for p in range(n_groups(d) // 2)]

    def operands(d, rows):
        ops = []
        for g in range(n_groups(d)):
            ls = slice(g * LANES, (g + 1) * LANES)
            e = d["es"][g // 2][:, (g % 2) * LANES:(g % 2 + 1) * LANES]
            q = d["q"][rows, ls]
            k = d["k"][rows, ls]
            gcs = e[0:CHUNK]
            qg = (q * jnp.exp(gcs)).astype(BF16)
            kd = (k * jnp.exp(e[CHUNK:2 * CHUNK])).astype(BF16)
            dec = jnp.exp(gcs[CHUNK - 1:CHUNK, :])
            qb = q.astype(BF16)
            kb = k.astype(BF16)
            ql, kl = [qb], [kb]
            mxu_block = 2
            for l in range(N_LEVELS):
                s = CHUNK >> (l + 1)
                if s >= MIN_VPU_LEVEL:
                    el = _level_exponent(gcs, s)
                else:
                    el = e[mxu_block * CHUNK:(mxu_block + 1) * CHUNK]
                    mxu_block += 1
                w = jnp.exp(el).astype(BF16)
                ql.append(qb * w)
                kl.append(kb * w)
            ops.append((qg, kd, dec, ql, kl))
        d["ops"] = ops

    def head_operands(d, h, l):
        g, j = divmod(h, LANES // d["dk"])
        return pick(d["ops"][g][3][l], lane_mask(d, j)), d["ops"][g][4][l]

    def scores(d, rows):
        zero = jnp.zeros((CHUNK, LANES), BF16)
        pairs = []
        for p in range(n_groups(d) * (LANES // d["dk"]) // 2):
            a = None
            for l in range(N_LEVELS + 1):
                q0, k0 = head_operands(d, 2 * p, l)
                q1, k1 = head_operands(d, 2 * p + 1, l)
                s = _dot_nt(jnp.concatenate([q0, q1], axis=1), _side_by_side(k0, k1, zero)) * mk_ref[l]
                a = s if a is None else a + s
            pairs.append(a.astype(BF16))
        d["pairs"] = pairs

    def outputs(d, rows):
        hp = LANES // d["dk"]
        zero = jnp.zeros((CHUNK, LANES), BF16)
        outs, upds = [], [None] * n_groups(d)
        sts = [d["st"][g] for g in range(n_groups(d))]
        stbs = [st.astype(BF16) for st in sts]
        vbs = [d["v"][rows, h * LANES:(h + 1) * LANES].astype(BF16) for h in range(n_groups(d) * hp)]
        intra = [_dot(a, _side_by_side(vbs[2 * p], vbs[2 * p + 1], zero))
                 for p, a in enumerate(d["pairs"])]
        for h, vh in enumerate(vbs):
            g, j = divmod(h, hp)
            lm = lane_mask(d, j)
            qg, kd = d["ops"][g][0], d["ops"][g][1]
            outs.append(intra[h // 2][:, (h % 2) * LANES:(h % 2 + 1) * LANES]
                        + _dot_nt(pick(qg, lm), stbs[g]))
            u = _dot_tn(vh, kd)
            upds[g] = u if upds[g] is None else jnp.where(lm, u, upds[g])
        d["outs"], d["upds"], d["sts"] = outs, upds, sts

    def finish(d, rows):
        for g in range(n_groups(d)):
            d["st"][g] = d["sts"][g] * d["ops"][g][2] + d["upds"][g]
        for h, o in enumerate(d["outs"]):
            _norm_gate_store(d["y"], rows, h * LANES, [o], d["nw"],
                             [d["z"][rows, h * LANES:(h + 1) * LANES]])

    return cumsums, operands, scores, outputs, finish


def _gla_stream(p_ref, s, qk, dk, y_ref, st_s, nw_ref):
    v0, z0, lg0 = 2 * qk, 2 * qk + D_BRANCH, 2 * qk + 2 * D_BRANCH
    return dict(q=p_ref.at[s, :, 0:qk], k=p_ref.at[s, :, qk:v0], v=p_ref.at[s, :, v0:z0],
                z=p_ref.at[s, :, z0:lg0], lg=p_ref.at[s, :, lg0:lg0 + qk],
                y=y_ref.at[s], st=st_s.at[s], nw=nw_ref, dk=dk)


def _mlstm_constants():
    half = LANES // 2
    full0 = half * MLSTM_HEADS
    e = np.zeros((LANES, full0 + 2 * LANES * MLSTM_HEADS), np.float32)
    diff0 = full0 + LANES * MLSTM_HEADS
    sel = np.zeros((16, LANES), np.float32)
    for h in range(MLSTM_HEADS):
        e[SM_F + h, h * half:(h + 1) * half] = 1.0
        e[SM_F + h, full0 + h * LANES:full0 + (h + 1) * LANES] = 1.0
        e[SM_I + h, diff0 + h * LANES:diff0 + (h + 1) * LANES] = 1.0
        e[SM_F + h, diff0 + h * LANES:diff0 + (h + 1) * LANES] = -1.0
        sel[h // 2, SM_I + h] = 1.0
        sel[h // 2, SM_F + h] = -1.0
    return jnp.asarray(np.concatenate([e, e], axis=0), BF16), jnp.asarray(sel, BF16)


def _mlstm_stages(nw_ref, tri_ref, exp_ref, sel_ref):
    lane = _lane_iota()
    is_f = (lane >= SM_F) & (lane < SM_F + MLSTM_HEADS)

    half = LANES // 2
    ri = lax.broadcasted_iota(jnp.int32, (CHUNK, LANES), 0)
    ci = lax.broadcasted_iota(jnp.int32, (CHUNK, LANES), 1)
    causal2 = (ci & (half - 1)) <= ri
    lo_half = lane < half
    even = (lane & 1) == 0
    v0, o0, z0 = 2 * D_BRANCH, 3 * D_BRANCH, 4 * D_BRANCH
    heads = range(MLSTM_HEADS)
    pairs = range(MLSTM_HEADS // 2)
    full0 = half * MLSTM_HEADS
    diff0 = full0 + LANES * MLSTM_HEADS

    def qk_products(d, rows):
        qk_ref = d["p"]
        qbs = [qk_ref[rows, h * LANES:(h + 1) * LANES].astype(BF16) for h in heads]
        d["ks"] = [qk_ref[rows, D_BRANCH + h * LANES:D_BRANCH + (h + 1) * LANES] for h in heads]
        kbs = [k.astype(BF16) for k in d["ks"]]
        zero = jnp.zeros((CHUNK, LANES), BF16)
        ones = jnp.ones((CHUNK, LANES), BF16)
        d["qk_raw"] = [_dot_nt(jnp.concatenate([qbs[2 * p], qbs[2 * p + 1]], axis=1),
                               _side_by_side(kbs[2 * p], kbs[2 * p + 1], zero)) for p in pairs]
        d["qc"] = [_dot(qbs[h], d["c"][h].astype(BF16)) for h in heads]
        d["vaug"] = [jnp.concatenate([d["p"][rows, v0 + h * LANES:v0 + (h + 1) * LANES].astype(BF16),
                                      ones], axis=1) for h in heads]

    def gate_sums(d, rows):
        gc = d["g"][rows, :]
        bcol = _sel_dot(tri_ref[...], gc)
        ib = jnp.where(is_f, bcol, gc)
        d["ex"] = _dot_sel2(ib, exp_ref[...])
        by_parity = jnp.concatenate([jnp.where(even, ib, 0.0), jnp.where(even, 0.0, ib)], axis=0)
        d["drows"] = _sel_dot_nt(sel_ref[...], by_parity)

    def weights(d, rows):
        ex = d["ex"]
        mx = d["m"][0:1, :]
        ss, mrs = [], []
        for p in pairs:
            bx = ex[:, p * LANES:(p + 1) * LANES]
            lw = jnp.where(causal2, bx + d["drows"][p:p + 1, :], NEG_BIG)
            mr0 = jnp.max(jnp.where(lo_half, lw, NEG_BIG), axis=-1, keepdims=True)
            mr1 = jnp.max(jnp.where(lo_half, NEG_BIG, lw), axis=-1, keepdims=True)
            m64 = jnp.where(lo_half, mx[:, 2 * p * LANES:(2 * p + 1) * LANES],
                            mx[:, (2 * p + 1) * LANES:(2 * p + 2) * LANES])
            m_row = jnp.maximum(jnp.where(lo_half, mr0, mr1), bx + m64)
            ss.append((d["qk_raw"][p] * jnp.exp(lw - m_row)).astype(BF16))
            mrs += [mr0, mr1]
        d["ss"], d["mrs"] = ss, mrs
        b_last = ex[CHUNK - 1:CHUNK, full0:diff0]
        lwe = ex[:, diff0:] + b_last
        m_new = jnp.maximum(b_last + mx, jnp.max(lwe, axis=0, keepdims=True))
        d["cd"] = jnp.exp(b_last + mx - m_new)
        d["m_new"] = m_new
        kw = jnp.exp(lwe - m_new)
        d["kws"] = [(d["ks"][h] * kw[:, h * LANES:(h + 1) * LANES]).astype(BF16) for h in heads]

    def numerators(d, rows):
        zero = jnp.zeros((CHUNK, 2 * LANES), BF16)
        d["nums"] = [_dot(d["ss"][p], _side_by_side(d["vaug"][2 * p], d["vaug"][2 * p + 1], zero))
                     for p in pairs]
        d["cups"] = [_dot_tn(d["kws"][h], d["vaug"][h]) for h in heads]

    def finish(d, rows):
        ex = d["ex"]
        mx = d["m"][0:1, :]
        for h in heads:
            p, hd = divmod(h, 2)
            hs = slice(h * LANES, (h + 1) * LANES)
            m_inter = ex[:, full0 + h * LANES:full0 + (h + 1) * LANES] + mx[:, hs]
            m_row = jnp.maximum(d["mrs"][h], m_inter)
            inter = jnp.exp(m_inter - m_row)
            sv = d["nums"][p][:, hd * 2 * LANES:(hd + 1) * 2 * LANES]
            num = sv[:, 0:LANES] + inter * d["qc"][h][:, 0:LANES]
            den = sv[:, LANES:] + inter * d["qc"][h][:, LANES:]
            hh = num / jnp.maximum(jnp.abs(den), jnp.exp(-m_row))
            cd = d["cd"][:, hs]
            d["c"][h] = jnp.concatenate([cd, cd], axis=1) * d["c"][h] + d["cups"][h]
            og = d["p"][rows, o0 + h * LANES:o0 + (h + 1) * LANES]
            _norm_gate_store(d["y"], rows, h * LANES, [og * hh], nw_ref,
                             [d["p"][rows, z0 + h * LANES:z0 + (h + 1) * LANES]])
        d["m"][0:1, :] = d["m_new"]

    return qk_products, gate_sums, weights, numerators, finish


def _ssd_constants():
    e = np.zeros((LANES, SSD_HEADS * SSD_HEAD_DIM), np.float32)
    sel = np.zeros((16, LANES), np.float32)
    for h in range(SSD_HEADS):
        e[SM_DT + h, h * SSD_HEAD_DIM:(h + 1) * SSD_HEAD_DIM] = 1.0
        sel[h // 2, SM_DT + h] = 1.0
    return jnp.asarray(np.concatenate([e, e], axis=0), BF16), jnp.asarray(sel, BF16)


def _ssd_stages(alog_ref, dx_ref, nw_ref, tri_ref, sel_ref, exp_ref):
    lane = _lane_iota()
    is_dt = (lane >= SM_DT) & (lane < SM_DT + SSD_HEADS)

    half = LANES // 2
    ri = lax.broadcasted_iota(jnp.int32, (CHUNK, LANES), 0)
    ci = lax.broadcasted_iota(jnp.int32, (CHUNK, LANES), 1)
    causal2 = (ci & (half - 1)) <= ri
    lo_half = lane < half
    even = (lane & 1) == 0
    a_lane = jnp.where(is_dt, -jnp.exp(alog_ref[...]), 0.0)
    b0, c0, z0 = D_BRANCH, D_BRANCH + SSD_BC, D_BRANCH + 2 * SSD_BC
    groups = range(SSD_GROUPS)
    group_w = D_BRANCH // SSD_GROUPS
    pairs = range(D_BRANCH // LANES)
    pairs_per_group = group_w // LANES

    def products(d, rows):
        xbc = d["p"]
        d["bgs"] = [xbc[rows, b0 + g * SSD_STATE:b0 + (g + 1) * SSD_STATE].astype(BF16) for g in groups]
        cgbs = [xbc[rows, c0 + g * SSD_STATE:c0 + (g + 1) * SSD_STATE].astype(BF16) for g in groups]
        d["cb2"] = [_dot_nt(cgbs[g], jnp.concatenate([d["bgs"][g], d["bgs"][g]], axis=0))
                    for g in groups]
        d["cst"] = [_dot(cgbs[g], d["st"][g].astype(BF16)) for g in groups]

    def decay_sums(d, rows):
        dt = d["dt"][rows, :]
        acs = _sel_dot(tri_ref[...], dt * a_lane)
        ex = _dot_sel2(jnp.concatenate([dt, acs], axis=0), exp_ref[...])
        d["dtx"], d["acs_x"] = ex[0:CHUNK], ex[CHUNK:]
        by_parity = jnp.concatenate([jnp.where(even, acs, 0.0), jnp.where(even, 0.0, acs)], axis=0)
        d["a_rows"] = _sel_dot_nt(sel_ref[...], by_parity)

    def decays(d, rows):
        d["xss"], d["ms"], d["xblk"], xdecs = [], [], [], []
        for p in pairs:
            ls = slice(p * LANES, (p + 1) * LANES)
            ax = d["acs_x"][:, ls]
            lmat = jnp.exp(jnp.where(causal2, ax - d["a_rows"][p:p + 1, :], NEG_BIG))
            d["ms"].append((d["cb2"][p // pairs_per_group] * lmat).astype(BF16))
            xs = d["p"][rows, ls]
            xdt = xs * d["dtx"][:, ls]
            d["xss"].append(xs)
            d["xblk"].append(jnp.concatenate([jnp.where(lo_half, xdt, 0.0), jnp.where(lo_half, 0.0, xdt)],
                                             axis=0).astype(BF16))
            xdecs.append((xdt * jnp.exp(ax[CHUNK - 1:CHUNK, :] - ax)).astype(BF16))
        d["xdec"] = [jnp.concatenate(xdecs[g * pairs_per_group:(g + 1) * pairs_per_group], axis=1)
                     for g in groups]

    def chunk_products(d, rows):
        d["yds"] = [_dot(d["ms"][p], d["xblk"][p]) for p in pairs]
        d["ups"] = [_dot_tn(d["bgs"][g], d["xdec"][g]) for g in groups]

    def finish(d, rows):
        for g in groups:
            gs = slice(g * group_w, (g + 1) * group_w)
            eax = jnp.exp(d["acs_x"][:, gs])
            d["st"][g] = d["st"][g] * eax[CHUNK - 1:CHUNK, :] + d["ups"][g]
            ys = []
            for pp in range(pairs_per_group):
                p = g * pairs_per_group + pp
                ls = slice(p * LANES, (p + 1) * LANES)
                y = (d["cst"][g][:, pp * LANES:(pp + 1) * LANES] * eax[:, pp * LANES:(pp + 1) * LANES]
                     + dx_ref[:, ls] * d["xss"][p] + d["yds"][p])
                ys.append(y * d["p"][rows, z0 + p * LANES:z0 + (p + 1) * LANES])
            _norm_gate_store(d["y"], rows, g * group_w, ys, nw_ref, [None] * len(ys))

    return products, decay_sums, decays, chunk_products, finish


def _mixers_kernel(pa_ref, pb_ref, pc_ref, pd_ref, sm_ref, nwa_ref, nwb_ref, nwc_ref, nwd_ref,
                   ms_ref, mk_ref, tri_ref, mexp_ref, msel_ref, alog_ref, dx_ref, dsel_ref, dexp_ref,
                   ya_ref, yb_ref, yc_ref, yd_ref, sta_s, stc_s, cb_s, mb_s, std_s, *, tile):
    @pl.when(pl.program_id(0) == 0)
    def _():
        for r in (sta_s, stc_s, cb_s, mb_s, std_s):
            r[...] = jnp.zeros_like(r)

    gla = _gla_stages(ms_ref, mk_ref)
    mlstm = _mlstm_stages(nwb_ref, tri_ref, mexp_ref, msel_ref)
    ssd = _ssd_stages(alog_ref, dx_ref, nwd_ref, tri_ref, dsel_ref, dexp_ref)
    work = []
    for s in range(pa_ref.shape[0]):
        work.append((_gla_stream(pa_ref, s, GLA_QK, GLA_DK, ya_ref, sta_s, nwa_ref), gla))
        work.append((dict(p=pb_ref.at[s], g=sm_ref.at[s], c=cb_s.at[s], m=mb_s.at[s], y=yb_ref.at[s]),
                     mlstm))
        work.append((_gla_stream(pc_ref, s, HGRN_QF, HGRN_DK, yc_ref, stc_s, nwc_ref), gla))
        work.append((dict(p=pd_ref.at[s], dt=sm_ref.at[s], st=std_s.at[s], y=yd_ref.at[s]), ssd))
    _run_chunks(work, tile)


def _mixers(tok_inputs, const_inputs, n_batch, seq, tile):
    tok = lambda t: (0, t, 0)
    in_specs = [pl.BlockSpec((n_batch, tile, a.shape[2]), tok) for a in tok_inputs]
    for a in const_inputs:
        in_specs.append(pl.BlockSpec(a.shape, lambda t, nd=a.ndim: (0,) * nd))
    scratch = [pltpu.VMEM((n_batch, GLA_QK // LANES, GLA_DV, LANES), F32),
               pltpu.VMEM((n_batch, HGRN_QF // LANES, HGRN_DV, LANES), F32),
               pltpu.VMEM((n_batch, MLSTM_HEADS, MLSTM_DH, 2 * MLSTM_DH), F32),
               pltpu.VMEM((n_batch, 8, MLSTM_HEADS * LANES), F32),
               pltpu.VMEM((n_batch, SSD_GROUPS, SSD_STATE, D_BRANCH // SSD_GROUPS), F32)]
    return pl.pallas_call(
        functools.partial(_mixers_kernel, tile=tile), grid=(seq // tile,), in_specs=in_specs,
        out_specs=[pl.BlockSpec((n_batch, tile, D_BRANCH), tok)] * 4,
        out_shape=[jax.ShapeDtypeStruct((n_batch, seq, D_BRANCH), BF16)] * 4,
        scratch_shapes=scratch,
        compiler_params=pltpu.CompilerParams(dimension_semantics=("arbitrary",),
                                             vmem_limit_bytes=VMEM_LIMIT),
        name="mixers",
    )(*tok_inputs, *const_inputs)


def _pad_lanes(parts, total=LANES):
    width = sum(p.shape[-1] for p in parts)
    lead = parts[0].shape[:-1]
    return jnp.concatenate(list(parts) + [jnp.zeros(lead + (total - width,), parts[0].dtype)], axis=-1)


def _small_vector(i_part, f_part, dt_part):
    z = jnp.zeros((GLA_GATE_RANK,), F32)
    return _pad_lanes([z, i_part.astype(F32), f_part.astype(F32), dt_part.astype(F32)])[None, :]


def kernel(x, norm_w, w_in, gla_gate_w, gla_gate_b, gla_norm_w, ml_conv_w, ml_conv_b, ml_i_b, ml_f_b,
           ml_norm_w, hg_lb_logits, hg_norm_w, ssd_conv_w, ssd_conv_b, ssd_dt_bias, ssd_A_log, ssd_D,
           ssd_norm_w, w_out, final_norm_w):
    n_batch, seq, _ = x.shape
    depth = w_in.shape[0]
    tile = min(MIX_TILE, seq)
    n_tok = n_batch * seq

    mstack, masks = _gla_constants()
    tri_c = _tri(CHUNK)
    ml_exp, ml_sel = _mlstm_constants()
    ssd_exp, ssd_sel = _ssd_constants()
    zero4 = jnp.zeros((MLSTM_HEADS,), F32)
    in_tile = min(IN_TILE, seq)

    p = jax.nn.softmax(hg_lb_logits.astype(F32), axis=0)
    lower_bounds = jnp.cumsum(p, axis=0) - p[0:1]

    w2d = w_in.reshape(depth * D_MODEL, -1)
    w_wide = _realign(w2d, REALIGN_ROWS)
    starts = np.cumsum((0,) + PROJ_SIZES)
    narrow = [w2d[:, starts[i]:starts[i + 1]] for i, size in enumerate(PROJ_SIZES) if size < LANES]
    w_small = _pad_lanes(narrow).astype(BF16)

    h = x.reshape(n_tok, D_MODEL)
    row2 = lambda v: v.astype(F32).reshape(1, -1)
    for l in range(depth):
        gate_w = jnp.concatenate(
            [gla_gate_w[l].astype(F32), jnp.zeros((LANES - GLA_GATE_RANK, GLA_QK), F32)], axis=0)
        consts = [gate_w, row2(gla_gate_b[l]), row2(lower_bounds[l]),
                  ml_conv_w[l].astype(F32), row2(ml_conv_b[l]),
                  ssd_conv_w[l].astype(F32), row2(ssd_conv_b[l]),
                  _small_vector(ml_i_b[l], ml_f_b[l], ssd_dt_bias[l])]
        pa, pb, pc, pd, sm = [p.reshape(n_batch, seq, -1)
                              for p in _inproj(h, row2(norm_w[l]), w_wide, w_small, l, consts, in_tile,
                                               seq // in_tile)]

        ys = _mixers(
            [pa, pb, pc, pd, sm],
            [row2(gla_norm_w[l]), row2(ml_norm_w[l]), row2(hg_norm_w[l]), row2(ssd_norm_w[l]),
             mstack, masks, tri_c, ml_exp, ml_sel, _small_vector(zero4, zero4, ssd_A_log[l]),
             row2(jnp.repeat(ssd_D[l].astype(F32), SSD_HEAD_DIM)), ssd_sel, ssd_exp],
            n_batch, seq, tile)
        ys = [y.reshape(n_tok, D_BRANCH) for y in ys]
        h = _outproj(h, ys, w_out[l].astype(BF16), row2(final_norm_w), l == depth - 1,
                     min(OUT_TILE, n_tok))
    return h.reshape(n_batch, seq, D_MODEL)
```

```python
import functools

import numpy as np
import jax
import jax.numpy as jnp
from jax import lax
from jax.experimental import pallas as pl
from jax.experimental.pallas import tpu as pltpu

F32 = jnp.float32
BF16 = jnp.bfloat16

D_MODEL = 1024
D_BRANCH = 512
EPS = 1e-6
NEG_BIG = -1e30

GLA_HEADS, GLA_DK, GLA_DV = 4, 64, 128
GLA_GATE_RANK, GLA_GATE_NORM = 16, 16.0
MLSTM_HEADS, MLSTM_DH, MLSTM_CONV = 4, 128, 4
HGRN_HEADS, HGRN_DK, HGRN_DV = 4, 128, 128
SSD_HEAD_DIM, SSD_HEADS, SSD_GROUPS, SSD_STATE, SSD_CONV = 64, 8, 2, 128, 4
GLA_QK = GLA_HEADS * GLA_DK
HGRN_QF = HGRN_HEADS * HGRN_DK
SSD_BC = SSD_GROUPS * SSD_STATE
PROJ_SIZES = (
    GLA_QK, GLA_QK, D_BRANCH, GLA_GATE_RANK, D_BRANCH,
    D_BRANCH, D_BRANCH, D_BRANCH, MLSTM_HEADS, MLSTM_HEADS, D_BRANCH, D_BRANCH,
    HGRN_QF, HGRN_QF, D_BRANCH, D_BRANCH,
    D_BRANCH, SSD_BC, SSD_BC, SSD_HEADS, D_BRANCH,
)

LANES = 128
HIST = 8
VMEM_LIMIT = 56 * 1024 * 1024

REALIGN_ROWS = 256
IN_TILE = 256
MIX_TILE = 256
OUT_TILE = 512

CHUNK = 64
N_LEVELS = 6
MIN_VPU_LEVEL = 4

SM_GR, SM_I, SM_F, SM_DT = 0, 16, 20, 24

W_A = 2 * GLA_QK + 2 * D_BRANCH
W_B = 5 * D_BRANCH
W_C = 2 * HGRN_QF + 2 * D_BRANCH
W_D = 2 * D_BRANCH + 2 * SSD_BC
COL_A = 0
COL_B = COL_A + W_A
COL_C = COL_B + W_B
COL_D = COL_C + W_C
COL_S = COL_D + W_D


def _gla_constants():
    c = CHUNK
    t = np.arange(c)[:, None]
    d = np.arange(c)[None, :]
    blocks = [(d <= t), (d > t)]
    masks = [np.eye(c, dtype=bool)]
    for l in range(N_LEVELS):
        s = c >> (l + 1)
        mid_t = (t // (2 * s)) * (2 * s) + s
        upper = t >= mid_t
        if s < MIN_VPU_LEVEL:
            blocks.append(np.where(upper, (d >= mid_t) & (d <= t), (d > t) & (d <= mid_t - 1)))
        same = (t // (2 * s)) == (d // (2 * s))
        masks.append(same & upper & (d < mid_t))
    mstack = np.concatenate(blocks, axis=0).astype(np.float32)
    mstack = np.concatenate([mstack, mstack], axis=1)
    masks = np.stack(masks).astype(np.float32)
    return jnp.asarray(mstack, BF16), jnp.asarray(np.concatenate([masks, masks], axis=-1))


def _tri(n):
    return jnp.asarray(np.tril(np.ones((n, n), np.float32)), BF16)


def _dot(a, b):
    return jnp.dot(a, b, preferred_element_type=F32)


def _dot_nt(a, b):
    return lax.dot_general(a, b, (((1,), (1,)), ((), ())), preferred_element_type=F32)


def _dot_tn(a, b):
    return lax.dot_general(a, b, (((0,), (0,)), ((), ())), preferred_element_type=F32)


def _split3(x):
    hi = x.astype(BF16)
    r1 = x - hi.astype(F32)
    mid = r1.astype(BF16)
    lo = (r1 - mid.astype(F32)).astype(BF16)
    return hi, mid, lo


def _sel_dot(sel, x):
    hi, mid, lo = _split3(x)
    return _dot(sel, hi) + _dot(sel, mid) + _dot(sel, lo)


def _sel_dot_nt(sel, x):
    hi, mid, lo = _split3(x)
    return _dot_nt(sel, hi) + _dot_nt(sel, mid) + _dot_nt(sel, lo)


def _sel_dot2(sel2, x):
    hi = x.astype(BF16)
    mid = (x - hi.astype(F32)).astype(BF16)
    return _dot(sel2, jnp.concatenate([hi, mid], axis=0))


def _level_exponent(gcs, s):
    pieces = []
    for b in range(0, CHUNK, 2 * s):
        ref_row = gcs[b + s - 1:b + s, :]
        if s >= 8:
            pieces += [ref_row - gcs[b:b + s], gcs[b + s:b + 2 * s] - ref_row]
        else:
            diff = gcs[b:b + 2 * s] - ref_row
            pieces.append(jnp.minimum(diff, -diff))
    return jnp.concatenate(pieces, axis=0)


def _dot_sel2(x, sel2):
    hi = x.astype(BF16)
    mid = (x - hi.astype(F32)).astype(BF16)
    return _dot(jnp.concatenate([hi, mid], axis=1), sel2)


def _dot_f32(a, b):
    ah = a.astype(BF16)
    al = (a - ah.astype(F32)).astype(BF16)
    bh = b.astype(BF16)
    bl = (b - bh.astype(F32)).astype(BF16)
    return _dot(ah, bh) + _dot(al, bh) + _dot(ah, bl)


def _softplus(x):
    return jnp.maximum(x, 0.0) + jnp.log1p(jnp.exp(-jnp.abs(x)))


def _log_sigmoid(x):
    return -_softplus(-x)


def _silu(x):
    return x * jax.nn.sigmoid(x)


def _lane_iota():
    return lax.broadcasted_iota(jnp.int32, (1, LANES), 1)


def _side_by_side(a, b, zero):
    return jnp.concatenate([jnp.concatenate([a, zero], axis=1),
                            jnp.concatenate([zero, b], axis=1)], axis=0)


def _wide_segments():
    segs, src, dst = [], 0, 0
    run_start = None
    for size in PROJ_SIZES + (0,):
        wide = size >= LANES
        if wide and run_start is None:
            run_start = src
        if not wide and run_start is not None:
            segs.append((run_start, dst, src - run_start))
            dst += src - run_start
            run_start = None
        src += size
    return segs


def _realign_kernel(wt_ref, o_ref, os_ref):
    for src, dst, n in _wide_segments():
        for r in range(0, n, LANES):
            o_ref[:, dst + r:dst + r + LANES] = wt_ref[src + r:src + r + LANES, :].T.astype(BF16)
    narrow, src = [], 0
    for size in PROJ_SIZES:
        if size < LANES:
            narrow.append(wt_ref[src:src + size, :])
        src += size
    used = sum(p.shape[0] for p in narrow)
    narrow.append(jnp.zeros((LANES - used, wt_ref.shape[1]), F32))
    os_ref[...] = jnp.concatenate(narrow, axis=0).T


def _realign(wt, rows):
    depth, n_proj, d_model = wt.shape
    return pl.pallas_call(
        _realign_kernel, grid=(depth, d_model // rows),
        in_specs=[pl.BlockSpec((None, n_proj, rows), lambda l, j: (l, 0, j))],
        out_specs=[pl.BlockSpec((None, rows, COL_S), lambda l, j: (l, j, 0)),
                   pl.BlockSpec((None, rows, LANES), lambda l, j: (l, j, 0))],
        out_shape=[jax.ShapeDtypeStruct((depth, d_model, COL_S), BF16),
                   jax.ShapeDtypeStruct((depth, d_model, LANES), F32)],
        compiler_params=pltpu.CompilerParams(dimension_semantics=("arbitrary", "arbitrary"),
                                             vmem_limit_bytes=VMEM_LIMIT),
        name="realign",
    )(wt)


def _causal_conv_silu(raw, xe_ref, hist_ref, w_ref, b_ref, taps):
    rows = raw.shape[0]
    xe_ref[0:HIST, :] = hist_ref[...]
    xe_ref[HIST:HIST + rows, :] = raw
    hist_ref[...] = raw[rows - HIST:rows, :]
    blk = 64
    out = []
    for r in range(0, rows, blk):
        acc = b_ref[...] + w_ref[taps - 1:taps, :] * xe_ref[HIST + r:HIST + r + blk, :]
        for k in range(taps - 1):
            off = HIST - (taps - 1) + k
            acc = acc + w_ref[k:k + 1, :] * xe_ref[off + r:off + r + blk, :]
        out.append(_silu(acc))
    return out


def _inproj_kernel(x_ref, nw_ref, w_ref, ws_ref, gw_ref, gb_ref, lb_ref, cwb_ref, cbb_ref, cwd_ref, cbd_ref,
                   gbias_ref, oa_ref, ob_ref, oc_ref, od_ref, os_ref, xe_s, hist_b, hist_d,
                   *, tiles_per_seq):
    @pl.when(pl.program_id(0) % tiles_per_seq == 0)
    def _():
        hist_b[...] = jnp.zeros_like(hist_b)
        hist_d[...] = jnp.zeros_like(hist_d)

    x = x_ref[...]
    ms = jnp.mean(x * x, axis=-1, keepdims=True)
    u = (x * lax.rsqrt(ms + EPS) * nw_ref[...]).astype(BF16)
    lane = _lane_iota()
    blk = 64

    small = _dot(u, ws_ref[...].astype(BF16))
    biased = small + gbias_ref[...]
    is_f = (lane >= SM_F) & (lane < SM_F + MLSTM_HEADS)
    is_dt = (lane >= SM_DT) & (lane < SM_DT + SSD_HEADS)
    os_ref[...] = jnp.where(is_f, _log_sigmoid(biased), jnp.where(is_dt, _softplus(biased), biased))

    raw = _dot(u, w_ref[:, COL_B:COL_B + W_B])
    conv_w = 2 * D_BRANCH
    k_scale = jnp.where(lax.broadcasted_iota(jnp.int32, (1, conv_w), 1) < D_BRANCH, 1.0, MLSTM_DH ** -0.5)
    for i, blk_out in enumerate(_causal_conv_silu(raw[:, 0:conv_w], xe_s, hist_b, cwb_ref, cbb_ref,
                                                  MLSTM_CONV)):
        ob_ref[i * blk:(i + 1) * blk, 0:conv_w] = blk_out * k_scale
    o0, z0 = 3 * D_BRANCH, 4 * D_BRANCH
    ob_ref[:, conv_w:o0] = raw[:, conv_w:o0]
    ob_ref[:, o0:z0] = jax.nn.sigmoid(raw[:, o0:z0])
    ob_ref[:, z0:W_B] = _silu(raw[:, z0:W_B])

    raw = _dot(u, w_ref[:, COL_D:COL_D + W_D])
    conv_w = D_BRANCH + 2 * SSD_BC
    for i, blk_out in enumerate(_causal_conv_silu(raw[:, 0:conv_w], xe_s, hist_d, cwd_ref, cbd_ref,
                                                  SSD_CONV)):
        od_ref[i * blk:(i + 1) * blk, 0:conv_w] = blk_out
    od_ref[:, conv_w:W_D] = _silu(raw[:, conv_w:W_D])

    raw = _dot(u, w_ref[:, COL_C:COL_C + W_C])
    lb = lb_ref[...]
    fr = raw[:, HGRN_QF:2 * HGRN_QF]
    oc_ref[:, 0:HGRN_QF] = raw[:, 0:HGRN_QF] * (HGRN_DK ** -0.5)
    oc_ref[:, HGRN_QF:2 * HGRN_QF] = (1.0 - lb) * jax.nn.sigmoid(-fr)
    z0 = 2 * HGRN_QF + D_BRANCH
    oc_ref[:, 2 * HGRN_QF:z0] = raw[:, 2 * HGRN_QF:z0]
    oc_ref[:, z0:W_C] = _silu(raw[:, z0:W_C])
    oc_ref[:, W_C:W_C + HGRN_QF] = jnp.log(jnp.maximum(lb + (1.0 - lb) * jax.nn.sigmoid(fr), 1e-30))

    gate = _dot_f32(small, gw_ref[...]) + gb_ref[...]
    oa_ref[:, W_A:W_A + GLA_QK] = _log_sigmoid(gate) * (1.0 / GLA_GATE_NORM)
    raw = _dot(u, w_ref[:, COL_A:COL_A + W_A])
    oa_ref[:, 0:GLA_QK] = raw[:, 0:GLA_QK] * (GLA_DK ** -0.5)
    z0 = 2 * GLA_QK + D_BRANCH
    oa_ref[:, GLA_QK:z0] = raw[:, GLA_QK:z0]
    oa_ref[:, z0:W_A] = _silu(raw[:, z0:W_A])


def _inproj(h, norm_w, w_wide, w_small, layer, consts, tile, tiles_per_seq):
    n_tok = h.shape[0]
    const = lambda i: (0, 0)
    row = lambda i: (i, 0)
    in_specs = [pl.BlockSpec((tile, D_MODEL), row), pl.BlockSpec((1, D_MODEL), const),
                pl.BlockSpec((None, D_MODEL, COL_S), lambda i: (layer, 0, 0), pipeline_mode=pl.Buffered(1)),
                pl.BlockSpec((None, D_MODEL, LANES), lambda i: (layer, 0, 0))]
    in_specs += [pl.BlockSpec(c.shape, const) for c in consts]
    widths = (W_A + GLA_QK, W_B, W_C + HGRN_QF, W_D, LANES)
    conv_w = 2 * D_BRANCH
    return pl.pallas_call(
        functools.partial(_inproj_kernel, tiles_per_seq=tiles_per_seq),
        grid=(n_tok // tile,), in_specs=in_specs,
        out_specs=[pl.BlockSpec((tile, w), row) for w in widths],
        out_shape=[jax.ShapeDtypeStruct((n_tok, w), F32) for w in widths],
        scratch_shapes=[pltpu.VMEM((tile + HIST, conv_w), F32), pltpu.VMEM((HIST, conv_w), F32),
                        pltpu.VMEM((HIST, conv_w), F32)],
        compiler_params=pltpu.CompilerParams(dimension_semantics=("arbitrary",),
                                             vmem_limit_bytes=VMEM_LIMIT),
        name="inproj",
    )(h, norm_w, w_wide, w_small, *consts)


def _outproj_kernel(h_ref, ya_ref, yb_ref, yc_ref, yd_ref, w_ref, fw_ref, o_ref, *, final):
    acc = h_ref[...]
    for i, y_ref in enumerate((ya_ref, yb_ref, yc_ref, yd_ref)):
        acc = acc + _dot(y_ref[...], w_ref[i * D_BRANCH:(i + 1) * D_BRANCH, :])
    if final:
        ms = jnp.mean(acc * acc, axis=-1, keepdims=True)
        acc = acc * lax.rsqrt(ms + EPS) * fw_ref[...]
    o_ref[...] = acc


def _outproj(h, ys, w_out, final_w, final, tile):
    n_tok = h.shape[0]
    const = lambda i: (0, 0)
    row = lambda i: (i, 0)
    in_specs = [pl.BlockSpec((tile, D_MODEL), row)]
    in_specs += [pl.BlockSpec((tile, D_BRANCH), row) for _ in ys]
    in_specs += [pl.BlockSpec(w_out.shape, const), pl.BlockSpec((1, D_MODEL), const)]
    return pl.pallas_call(
        functools.partial(_outproj_kernel, final=final),
        grid=(n_tok // tile,), in_specs=in_specs,
        out_specs=pl.BlockSpec((tile, D_MODEL), row),
        out_shape=jax.ShapeDtypeStruct((n_tok, D_MODEL), F32),
        compiler_params=pltpu.CompilerParams(dimension_semantics=("arbitrary",),
                                             vmem_limit_bytes=VMEM_LIMIT),
        name="outproj",
    )(h, *ys, w_out, final_w)


def _norm_gate_store(y_ref, rows, col0, parts, nw_ref, z_parts):
    width = sum(p.shape[-1] for p in parts)
    ss = sum(jnp.sum(p * p, axis=-1, keepdims=True) for p in parts)
    scale = lax.rsqrt(ss * (1.0 / width) + EPS)
    c = col0
    for p, z in zip(parts, z_parts):
        w = p.shape[-1]
        out = p * scale * nw_ref[:, c:c + w]
        if z is not None:
            out = out * z
        y_ref[rows, c:c + w] = out.astype(y_ref.dtype)
        c += w


N_STAGES = 5
STAGE_SKEW = (0,)


def _run_chunks(work, tile):
    def body(c, carry):
        rows = pl.ds(pl.multiple_of(c * CHUNK, CHUNK), CHUNK)
        live = [(dict(d), stages, STAGE_SKEW[i % len(STAGE_SKEW)]) for i, (d, stages) in enumerate(work)]
        for slot in range(N_STAGES + max(STAGE_SKEW)):
            for d, stages, skew in live:
                if 0 <= slot - skew < N_STAGES:
                    stages[slot - skew](d, rows)
        return carry

    lax.fori_loop(0, tile // CHUNK, body, 0)


def _gla_stages(ms_ref, mk_ref):
    lane = _lane_iota()

    def lane_mask(d, j):
        dk = d["dk"]
        return None if dk == LANES else (lane >= j * dk) & (lane < (j + 1) * dk)

    def pick(a, lm):
        return a if lm is None else jnp.where(lm, a, jnp.zeros_like(a))

    def n_groups(d):
        return d["q"].shape[1] // LANES

    def cumsums(d, rows):
        d["es"] = [_sel_dot2(ms_ref[...], d["lg"][rows, p * 2 * LANES:(p + 1) * 2 * LANES])
                   for p in range(n_groups(d) // 2)]

    def operands(d, rows):
        ops = []
        for g in range(n_groups(d)):
            ls = slice(g * LANES, (g + 1) * LANES)
            e = d["es"][g // 2][:, (g % 2) * LANES:(g % 2 + 1) * LANES]
            q = d["q"][rows, ls]
            k = d["k"][rows, ls]
            gcs = e[0:CHUNK]
            qg = (q * jnp.exp(gcs)).astype(BF16)
            kd = (k * jnp.exp(e[CHUNK:2 * CHUNK])).astype(BF16)
            dec = jnp.exp(gcs[CHUNK - 1:CHUNK, :])
            qb = q.astype(BF16)
            kb = k.astype(BF16)
            ql, kl = [qb], [kb]
            mxu_block = 2
            for l in range(N_LEVELS):
                s = CHUNK >> (l + 1)
                if s >= MIN_VPU_LEVEL:
                    el = _level_exponent(gcs, s)
                else:
                    el = e[mxu_block * CHUNK:(mxu_block + 1) * CHUNK]
                    mxu_block += 1
                w = jnp.exp(el).astype(BF16)
                ql.append(qb * w)
                kl.append(kb * w)
            ops.append((qg, kd, dec, ql, kl))
        d["ops"] = ops

    def head_operands(d, h, l):
        g, j = divmod(h, LANES // d["dk"])
        return pick(d["ops"][g][3][l], lane_mask(d, j)), d["ops"][g][4][l]

    def scores(d, rows):
        zero = jnp.zeros((CHUNK, LANES), BF16)
        pairs = []
        for p in range(n_groups(d) * (LANES // d["dk"]) // 2):
            a = None
            for l in range(N_LEVELS + 1):
                q0, k0 = head_operands(d, 2 * p, l)
                q1, k1 = head_operands(d, 2 * p + 1, l)
                s = _dot_nt(jnp.concatenate([q0, q1], axis=1), _side_by_side(k0, k1, zero)) * mk_ref[l]
                a = s if a is None else a + s
            pairs.append(a.astype(BF16))
        d["pairs"] = pairs

    def outputs(d, rows):
        hp = LANES // d["dk"]
        zero = jnp.zeros((CHUNK, LANES), BF16)
        outs, upds = [], [None] * n_groups(d)
        sts = [d["st"][g] for g in range(n_groups(d))]
        stbs = [st.astype(BF16) for st in sts]
        vbs = [d["v"][rows, h * LANES:(h + 1) * LANES].astype(BF16) for h in range(n_groups(d) * hp)]
        intra = [_dot(a, _side_by_side(vbs[2 * p], vbs[2 * p + 1], zero))
                 for p, a in enumerate(d["pairs"])]
        for h, vh in enumerate(vbs):
            g, j = divmod(h, hp)
            lm = lane_mask(d, j)
            qg, kd = d["ops"][g][0], d["ops"][g][1]
            outs.append(intra[h // 2][:, (h % 2) * LANES:(h % 2 + 1) * LANES]
                        + _dot_nt(pick(qg, lm), stbs[g]))
            u = _dot_tn(vh, kd)
            upds[g] = u if upds[g] is None else jnp.where(lm, u, upds[g])
        d["outs"], d["upds"], d["sts"] = outs, upds, sts

    def finish(d, rows):
        for g in range(n_groups(d)):
            d["st"][g] = d["sts"][g] * d["ops"][g][2] + d["upds"][g]
        for h, o in enumerate(d["outs"]):
            _norm_gate_store(d["y"], rows, h * LANES, [o], d["nw"],
                             [d["z"][rows, h * LANES:(h + 1) * LANES]])

    return cumsums, operands, scores, outputs, finish


def _gla_stream(p_ref, s, qk, dk, y_ref, st_s, nw_ref):
    v0, z0, lg0 = 2 * qk, 2 * qk + D_BRANCH, 2 * qk + 2 * D_BRANCH
    return dict(q=p_ref.at[s, :, 0:qk], k=p_ref.at[s, :, qk:v0], v=p_ref.at[s, :, v0:z0],
                z=p_ref.at[s, :, z0:lg0], lg=p_ref.at[s, :, lg0:lg0 + qk],
                y=y_ref.at[s], st=st_s.at[s], nw=nw_ref, dk=dk)


def _mlstm_constants():
    half = LANES // 2
    full0 = half * MLSTM_HEADS
    e = np.zeros((LANES, full0 + 2 * LANES * MLSTM_HEADS), np.float32)
    diff0 = full0 + LANES * MLSTM_HEADS
    sel = np.zeros((16, LANES), np.float32)
    for h in range(MLSTM_HEADS):
        e[SM_F + h, h * half:(h + 1) * half] = 1.0
        e[SM_F + h, full0 + h * LANES:full0 + (h + 1) * LANES] = 1.0
        e[SM_I + h, diff0 + h * LANES:diff0 + (h + 1) * LANES] = 1.0
        e[SM_F + h, diff0 + h * LANES:diff0 + (h + 1) * LANES] = -1.0
        sel[h // 2, SM_I + h] = 1.0
        sel[h // 2, SM_F + h] = -1.0
    return jnp.asarray(np.concatenate([e, e], axis=0), BF16), jnp.asarray(sel, BF16)


def _mlstm_stages(nw_ref, tri_ref, exp_ref, sel_ref):
    lane = _lane_iota()
    is_f = (lane >= SM_F) & (lane < SM_F + MLSTM_HEADS)

    half = LANES // 2
    ri = lax.broadcasted_iota(jnp.int32, (CHUNK, LANES), 0)
    ci = lax.broadcasted_iota(jnp.int32, (CHUNK, LANES), 1)
    causal2 = (ci & (half - 1)) <= ri
    lo_half = lane < half
    even = (lane & 1) == 0
    v0, o0, z0 = 2 * D_BRANCH, 3 * D_BRANCH, 4 * D_BRANCH
    heads = range(MLSTM_HEADS)
    pairs = range(MLSTM_HEADS // 2)
    full0 = half * MLSTM_HEADS
    diff0 = full0 + LANES * MLSTM_HEADS

    def qk_products(d, rows):
        qk_ref = d["p"]
        qbs = [qk_ref[rows, h * LANES:(h + 1) * LANES].astype(BF16) for h in heads]
        d["ks"] = [qk_ref[rows, D_BRANCH + h * LANES:D_BRANCH + (h + 1) * LANES] for h in heads]
        kbs = [k.astype(BF16) for k in d["ks"]]
        zero = jnp.zeros((CHUNK, LANES), BF16)
        ones = jnp.ones((CHUNK, LANES), BF16)
        d["qk_raw"] = [_dot_nt(jnp.concatenate([qbs[2 * p], qbs[2 * p + 1]], axis=1),
                               _side_by_side(kbs[2 * p], kbs[2 * p + 1], zero)) for p in pairs]
        d["qc"] = [_dot(qbs[h], d["c"][h].astype(BF16)) for h in heads]
        d["vaug"] = [jnp.concatenate([d["p"][rows, v0 + h * LANES:v0 + (h + 1) * LANES].astype(BF16),
                                      ones], axis=1) for h in heads]

    def gate_sums(d, rows):
        gc = d["g"][rows, :]
        bcol = _sel_dot(tri_ref[...], gc)
        ib = jnp.where(is_f, bcol, gc)
        d["ex"] = _dot_sel2(ib, exp_ref[...])
        by_parity = jnp.concatenate([jnp.where(even, ib, 0.0), jnp.where(even, 0.0, ib)], axis=0)
        d["drows"] = _sel_dot_nt(sel_ref[...], by_parity)

    def weights(d, rows):
        ex = d["ex"]
        mx = d["m"][0:1, :]
        ss, mrs = [], []
        for p in pairs:
            bx = ex[:, p * LANES:(p + 1) * LANES]
            lw = jnp.where(causal2, bx + d["drows"][p:p + 1, :], NEG_BIG)
            mr0 = jnp.max(jnp.where(lo_half, lw, NEG_BIG), axis=-1, keepdims=True)
            mr1 = jnp.max(jnp.where(lo_half, NEG_BIG, lw), axis=-1, keepdims=True)
            m64 = jnp.where(lo_half, mx[:, 2 * p * LANES:(2 * p + 1) * LANES],
                            mx[:, (2 * p + 1) * LANES:(2 * p + 2) * LANES])
            m_row = jnp.maximum(jnp.where(lo_half, mr0, mr1), bx + m64)
            ss.append((d["qk_raw"][p] * jnp.exp(lw - m_row)).astype(BF16))
            mrs += [mr0, mr1]
        d["ss"], d["mrs"] = ss, mrs
        b_last = ex[CHUNK - 1:CHUNK, full0:diff0]
        lwe = ex[:, diff0:] + b_last
        m_new = jnp.maximum(b_last + mx, jnp.max(lwe, axis=0, keepdims=True))
        d["cd"] = jnp.exp(b_last + mx - m_new)
        d["m_new"] = m_new
        kw = jnp.exp(lwe - m_new)
        d["kws"] = [(d["ks"][h] * kw[:, h * LANES:(h + 1) * LANES]).astype(BF16) for h in heads]

    def numerators(d, rows):
        zero = jnp.zeros((CHUNK, 2 * LANES), BF16)
        d["nums"] = [_dot(d["ss"][p], _side_by_side(d["vaug"][2 * p], d["vaug"][2 * p + 1], zero))
                     for p in pairs]
        d["cups"] = [_dot_tn(d["kws"][h], d["vaug"][h]) for h in heads]

    def finish(d, rows):
        ex = d["ex"]
        mx = d["m"][0:1, :]
        for h in heads:
            p, hd = divmod(h, 2)
            hs = slice(h * LANES, (h + 1) * LANES)
            m_inter = ex[:, full0 + h * LANES:full0 + (h + 1) * LANES] + mx[:, hs]
            m_row = jnp.maximum(d["mrs"][h], m_inter)
            inter = jnp.exp(m_inter - m_row)
            sv = d["nums"][p][:, hd * 2 * LANES:(hd + 1) * 2 * LANES]
            num = sv[:, 0:LANES] + inter * d["qc"][h][:, 0:LANES]
            den = sv[:, LANES:] + inter * d["qc"][h][:, LANES:]
            hh = num / jnp.maximum(jnp.abs(den), jnp.exp(-m_row))
            cd = d["cd"][:, hs]
            d["c"][h] = jnp.concatenate([cd, cd], axis=1) * d["c"][h] + d["cups"][h]
            og = d["p"][rows, o0 + h * LANES:o0 + (h + 1) * LANES]
            _norm_gate_store(d["y"], rows, h * LANES, [og * hh], nw_ref,
                             [d["p"][rows, z0 + h * LANES:z0 + (h + 1) * LANES]])
        d["m"][0:1, :] = d["m_new"]

    return qk_products, gate_sums, weights, numerators, finish


def _ssd_constants():
    e = np.zeros((LANES, SSD_HEADS * SSD_HEAD_DIM), np.float32)
    sel = np.zeros((16, LANES), np.float32)
    for h in range(SSD_HEADS):
        e[SM_DT + h, h * SSD_HEAD_DIM:(h + 1) * SSD_HEAD_DIM] = 1.0
        sel[h // 2, SM_DT + h] = 1.0
    return jnp.asarray(np.concatenate([e, e], axis=0), BF16), jnp.asarray(sel, BF16)


def _ssd_stages(alog_ref, dx_ref, nw_ref, tri_ref, sel_ref, exp_ref):
    lane = _lane_iota()
    is_dt = (lane >= SM_DT) & (lane < SM_DT + SSD_HEADS)

    half = LANES // 2
    ri = lax.broadcasted_iota(jnp.int32, (CHUNK, LANES), 0)
    ci = lax.broadcasted_iota(jnp.int32, (CHUNK, LANES), 1)
    causal2 = (ci & (half - 1)) <= ri
    lo_half = lane < half
    even = (lane & 1) == 0
    a_lane = jnp.where(is_dt, -jnp.exp(alog_ref[...]), 0.0)
    b0, c0, z0 = D_BRANCH, D_BRANCH + SSD_BC, D_BRANCH + 2 * SSD_BC
    groups = range(SSD_GROUPS)
    group_w = D_BRANCH // SSD_GROUPS
    pairs = range(D_BRANCH // LANES)
    pairs_per_group = group_w // LANES

    def products(d, rows):
        xbc = d["p"]
        d["bgs"] = [xbc[rows, b0 + g * SSD_STATE:b0 + (g + 1) * SSD_STATE].astype(BF16) for g in groups]
        cgbs = [xbc[rows, c0 + g * SSD_STATE:c0 + (g + 1) * SSD_STATE].astype(BF16) for g in groups]
        d["cb2"] = [_dot_nt(cgbs[g], jnp.concatenate([d["bgs"][g], d["bgs"][g]], axis=0))
                    for g in groups]
        d["cst"] = [_dot(cgbs[g], d["st"][g].astype(BF16)) for g in groups]

    def decay_sums(d, rows):
        dt = d["dt"][rows, :]
        acs = _sel_dot(tri_ref[...], dt * a_lane)
        ex = _dot_sel2(jnp.concatenate([dt, acs], axis=0), exp_ref[...])
        d["dtx"], d["acs_x"] = ex[0:CHUNK], ex[CHUNK:]
        by_parity = jnp.concatenate([jnp.where(even, acs, 0.0), jnp.where(even, 0.0, acs)], axis=0)
        d["a_rows"] = _sel_dot_nt(sel_ref[...], by_parity)

    def decays(d, rows):
        d["xss"], d["ms"], d["xblk"], xdecs = [], [], [], []
        for p in pairs:
            ls = slice(p * LANES, (p + 1) * LANES)
            ax = d["acs_x"][:, ls]
            lmat = jnp.exp(jnp.where(causal2, ax - d["a_rows"][p:p + 1, :], NEG_BIG))
            d["ms"].append((d["cb2"][p // pairs_per_group] * lmat).astype(BF16))
            xs = d["p"][rows, ls]
            xdt = xs * d["dtx"][:, ls]
            d["xss"].append(xs)
            d["xblk"].append(jnp.concatenate([jnp.where(lo_half, xdt, 0.0), jnp.where(lo_half, 0.0, xdt)],
                                             axis=0).astype(BF16))
            xdecs.append((xdt * jnp.exp(ax[CHUNK - 1:CHUNK, :] - ax)).astype(BF16))
        d["xdec"] = [jnp.concatenate(xdecs[g * pairs_per_group:(g + 1) * pairs_per_group], axis=1)
                     for g in groups]

    def chunk_products(d, rows):
        d["yds"] = [_dot(d["ms"][p], d["xblk"][p]) for p in pairs]
        d["ups"] = [_dot_tn(d["bgs"][g], d["xdec"][g]) for g in groups]

    def finish(d, rows):
        for g in groups:
            gs = slice(g * group_w, (g + 1) * group_w)
            eax = jnp.exp(d["acs_x"][:, gs])
            d["st"][g] = d["st"][g] * eax[CHUNK - 1:CHUNK, :] + d["ups"][g]
            ys = []
            for pp in range(pairs_per_group):
                p = g * pairs_per_group + pp
                ls = slice(p * LANES, (p + 1) * LANES)
                y = (d["cst"][g][:, pp * LANES:(pp + 1) * LANES] * eax[:, pp * LANES:(pp + 1) * LANES]
                     + dx_ref[:, ls] * d["xss"][p] + d["yds"][p])
                ys.append(y * d["p"][rows, z0 + p * LANES:z0 + (p + 1) * LANES])
            _norm_gate_store(d["y"], rows, g * group_w, ys, nw_ref, [None] * len(ys))

    return products, decay_sums, decays, chunk_products, finish


def _mixers_kernel(pa_ref, pb_ref, pc_ref, pd_ref, sm_ref, nwa_ref, nwb_ref, nwc_ref, nwd_ref,
                   ms_ref, mk_ref, tri_ref, mexp_ref, msel_ref, alog_ref, dx_ref, dsel_ref, dexp_ref,
                   ya_ref, yb_ref, yc_ref, yd_ref, sta_s, stc_s, cb_s, mb_s, std_s, *, tile):
    @pl.when(pl.program_id(0) == 0)
    def _():
        for r in (sta_s, stc_s, cb_s, mb_s, std_s):
            r[...] = jnp.zeros_like(r)

    gla = _gla_stages(ms_ref, mk_ref)
    mlstm = _mlstm_stages(nwb_ref, tri_ref, mexp_ref, msel_ref)
    ssd = _ssd_stages(alog_ref, dx_ref, nwd_ref, tri_ref, dsel_ref, dexp_ref)
    work = []
    for s in range(pa_ref.shape[0]):
        work.append((_gla_stream(pa_ref, s, GLA_QK, GLA_DK, ya_ref, sta_s, nwa_ref), gla))
        work.append((dict(p=pb_ref.at[s], g=sm_ref.at[s], c=cb_s.at[s], m=mb_s.at[s], y=yb_ref.at[s]),
                     mlstm))
        work.append((_gla_stream(pc_ref, s, HGRN_QF, HGRN_DK, yc_ref, stc_s, nwc_ref), gla))
        work.append((dict(p=pd_ref.at[s], dt=sm_ref.at[s], st=std_s.at[s], y=yd_ref.at[s]), ssd))
    _run_chunks(work, tile)


def _mixers(tok_inputs, const_inputs, n_batch, seq, tile):
    tok = lambda t: (0, t, 0)
    in_specs = [pl.BlockSpec((n_batch, tile, a.shape[2]), tok) for a in tok_inputs]
    for a in const_inputs:
        in_specs.append(pl.BlockSpec(a.shape, lambda t, nd=a.ndim: (0,) * nd))
    scratch = [pltpu.VMEM((n_batch, GLA_QK // LANES, GLA_DV, LANES), F32),
               pltpu.VMEM((n_batch, HGRN_QF // LANES, HGRN_DV, LANES), F32),
               pltpu.VMEM((n_batch, MLSTM_HEADS, MLSTM_DH, 2 * MLSTM_DH), F32),
               pltpu.VMEM((n_batch, 8, MLSTM_HEADS * LANES), F32),
               pltpu.VMEM((n_batch, SSD_GROUPS, SSD_STATE, D_BRANCH // SSD_GROUPS), F32)]
    return pl.pallas_call(
        functools.partial(_mixers_kernel, tile=tile), grid=(seq // tile,), in_specs=in_specs,
        out_specs=[pl.BlockSpec((n_batch, tile, D_BRANCH), tok)] * 4,
        out_shape=[jax.ShapeDtypeStruct((n_batch, seq, D_BRANCH), BF16)] * 4,
        scratch_shapes=scratch,
        compiler_params=pltpu.CompilerParams(dimension_semantics=("arbitrary",),
                                             vmem_limit_bytes=VMEM_LIMIT),
        name="mixers",
    )(*tok_inputs, *const_inputs)


def _pad_lanes(parts, total=LANES):
    width = sum(p.shape[-1] for p in parts)
    lead = parts[0].shape[:-1]
    return jnp.concatenate(list(parts) + [jnp.zeros(lead + (total - width,), parts[0].dtype)], axis=-1)


def _small_vector(i_part, f_part, dt_part):
    z = jnp.zeros((GLA_GATE_RANK,), F32)
    return _pad_lanes([z, i_part.astype(F32), f_part.astype(F32), dt_part.astype(F32)])[None, :]


def kernel(x, norm_w, w_in, gla_gate_w, gla_gate_b, gla_norm_w, ml_conv_w, ml_conv_b, ml_i_b, ml_f_b,
           ml_norm_w, hg_lb_logits, hg_norm_w, ssd_conv_w, ssd_conv_b, ssd_dt_bias, ssd_A_log, ssd_D,
           ssd_norm_w, w_out, final_norm_w):
    n_batch, seq, _ = x.shape
    depth = w_in.shape[0]
    tile = min(MIX_TILE, seq)
    n_tok = n_batch * seq

    mstack, masks = _gla_constants()
    tri_c = _tri(CHUNK)
    ml_exp, ml_sel = _mlstm_constants()
    ssd_exp, ssd_sel = _ssd_constants()
    zero4 = jnp.zeros((MLSTM_HEADS,), F32)
    in_tile = min(IN_TILE, seq)

    p = jax.nn.softmax(hg_lb_logits.astype(F32), axis=0)
    lower_bounds = jnp.cumsum(p, axis=0) - p[0:1]

    w_wide, w_small = _realign(jnp.swapaxes(w_in, 1, 2), REALIGN_ROWS)

    h = x.reshape(n_tok, D_MODEL)
    row2 = lambda v: v.astype(F32).reshape(1, -1)
    for l in range(depth):
        gate_w = jnp.concatenate(
            [gla_gate_w[l].astype(F32), jnp.zeros((LANES - GLA_GATE_RANK, GLA_QK), F32)], axis=0)
        consts = [gate_w, row2(gla_gate_b[l]), row2(lower_bounds[l]),
                  ml_conv_w[l].astype(F32), row2(ml_conv_b[l]),
                  ssd_conv_w[l].astype(F32), row2(ssd_conv_b[l]),
                  _small_vector(ml_i_b[l], ml_f_b[l], ssd_dt_bias[l])]
        pa, pb, pc, pd, sm = [p.reshape(n_batch, seq, -1)
                              for p in _inproj(h, row2(norm_w[l]), w_wide, w_small, l, consts, in_tile,
                                               seq // in_tile)]

        ys = _mixers(
            [pa, pb, pc, pd, sm],
            [row2(gla_norm_w[l]), row2(ml_norm_w[l]), row2(hg_norm_w[l]), row2(ssd_norm_w[l]),
             mstack, masks, tri_c, ml_exp, ml_sel, _small_vector(zero4, zero4, ssd_A_log[l]),
             row2(jnp.repeat(ssd_D[l].astype(F32), SSD_HEAD_DIM)), ssd_sel, ssd_exp],
            n_batch, seq, tile)
        ys = [y.reshape(n_tok, D_BRANCH) for y in ys]
        h = _outproj(h, ys, w_out[l].astype(BF16), row2(final_norm_w), l == depth - 1,
                     min(OUT_TILE, n_tok))
    return h.reshape(n_batch, seq, D_MODEL)
```

```python
import functools

import numpy as np
import jax
import jax.numpy as jnp
from jax import lax
from jax.experimental import pallas as pl
from jax.experimental.pallas import tpu as pltpu

F32 = jnp.float32
BF16 = jnp.bfloat16

D_MODEL = 1024
D_BRANCH = 512
EPS = 1e-6
NEG_BIG = -1e30

GLA_HEADS, GLA_DK, GLA_DV = 4, 64, 128
GLA_GATE_RANK, GLA_GATE_NORM = 16, 16.0
MLSTM_HEADS, MLSTM_DH, MLSTM_CONV = 4, 128, 4
HGRN_HEADS, HGRN_DK, HGRN_DV = 4, 128, 128
SSD_HEAD_DIM, SSD_HEADS, SSD_GROUPS, SSD_STATE, SSD_CONV = 64, 8, 2, 128, 4
GLA_QK = GLA_HEADS * GLA_DK
HGRN_QF = HGRN_HEADS * HGRN_DK
SSD_BC = SSD_GROUPS * SSD_STATE
PROJ_SIZES = (
    GLA_QK, GLA_QK, D_BRANCH, GLA_GATE_RANK, D_BRANCH,
    D_BRANCH, D_BRANCH, D_BRANCH, MLSTM_HEADS, MLSTM_HEADS, D_BRANCH, D_BRANCH,
    HGRN_QF, HGRN_QF, D_BRANCH, D_BRANCH,
    D_BRANCH, SSD_BC, SSD_BC, SSD_HEADS, D_BRANCH,
)

LANES = 128
HIST = 8
VMEM_LIMIT = 56 * 1024 * 1024

REALIGN_ROWS = 256
IN_TILE = 256
MIX_TILE = 512
OUT_TILE = 512

CHUNK = 64
N_LEVELS = 6
MIN_VPU_LEVEL = 4

SM_GR, SM_I, SM_F, SM_DT = 0, 16, 20, 24

W_A = 2 * GLA_QK + 2 * D_BRANCH
W_B = 5 * D_BRANCH
W_C = 2 * HGRN_QF + 2 * D_BRANCH
W_D = 2 * D_BRANCH + 2 * SSD_BC
COL_A = 0
COL_B = COL_A + W_A
COL_C = COL_B + W_B
COL_D = COL_C + W_C
COL_S = COL_D + W_D


def _gla_constants():
    c = CHUNK
    t = np.arange(c)[:, None]
    d = np.arange(c)[None, :]
    blocks = [(d <= t), (d > t)]
    masks = [np.eye(c, dtype=bool)]
    for l in range(N_LEVELS):
        s = c >> (l + 1)
        mid_t = (t // (2 * s)) * (2 * s) + s
        upper = t >= mid_t
        if s < MIN_VPU_LEVEL:
            blocks.append(np.where(upper, (d >= mid_t) & (d <= t), (d > t) & (d <= mid_t - 1)))
        same = (t // (2 * s)) == (d // (2 * s))
        masks.append(same & upper & (d < mid_t))
    mstack = np.concatenate(blocks, axis=0).astype(np.float32)
    mstack = np.concatenate([mstack, mstack], axis=1)
    masks = np.stack(masks).astype(np.float32)
    return jnp.asarray(mstack, BF16), jnp.asarray(np.concatenate([masks, masks], axis=-1))


def _tri(n):
    return jnp.asarray(np.tril(np.ones((n, n), np.float32)), BF16)


def _dot(a, b):
    return jnp.dot(a, b, preferred_element_type=F32)


def _dot_nt(a, b):
    return lax.dot_general(a, b, (((1,), (1,)), ((), ())), preferred_element_type=F32)


def _dot_tn(a, b):
    return lax.dot_general(a, b, (((0,), (0,)), ((), ())), preferred_element_type=F32)


def _split3(x):
    hi = x.astype(BF16)
    r1 = x - hi.astype(F32)
    mid = r1.astype(BF16)
    lo = (r1 - mid.astype(F32)).astype(BF16)
    return hi, mid, lo


def _sel_dot(sel, x):
    hi, mid, lo = _split3(x)
    return _dot(sel, hi) + _dot(sel, mid) + _dot(sel, lo)


def _sel_dot_nt(sel, x):
    hi, mid, lo = _split3(x)
    return _dot_nt(sel, hi) + _dot_nt(sel, mid) + _dot_nt(sel, lo)


def _sel_dot2(sel2, x):
    hi = x.astype(BF16)
    mid = (x - hi.astype(F32)).astype(BF16)
    return _dot(sel2, jnp.concatenate([hi, mid], axis=0))


def _level_exponent(gcs, s):
    pieces = []
    for b in range(0, CHUNK, 2 * s):
        ref_row = gcs[b + s - 1:b + s, :]
        if s >= 8:
            pieces += [ref_row - gcs[b:b + s], gcs[b + s:b + 2 * s] - ref_row]
        else:
            diff = gcs[b:b + 2 * s] - ref_row
            pieces.append(jnp.minimum(diff, -diff))
    return jnp.concatenate(pieces, axis=0)


def _dot_sel2(x, sel2):
    hi = x.astype(BF16)
    mid = (x - hi.astype(F32)).astype(BF16)
    return _dot(jnp.concatenate([hi, mid], axis=1), sel2)


def _dot_f32(a, b):
    ah = a.astype(BF16)
    al = (a - ah.astype(F32)).astype(BF16)
    bh = b.astype(BF16)
    bl = (b - bh.astype(F32)).astype(BF16)
    return _dot(ah, bh) + _dot(al, bh) + _dot(ah, bl)


def _softplus(x):
    return jnp.maximum(x, 0.0) + jnp.log1p(jnp.exp(-jnp.abs(x)))


def _log_sigmoid(x):
    return -_softplus(-x)


def _silu(x):
    return x * jax.nn.sigmoid(x)


def _lane_iota():
    return lax.broadcasted_iota(jnp.int32, (1, LANES), 1)


def _side_by_side(a, b, zero):
    return jnp.concatenate([jnp.concatenate([a, zero], axis=1),
                            jnp.concatenate([zero, b], axis=1)], axis=0)


def _wide_segments():
    segs, src, dst = [], 0, 0
    run_start = None
    for size in PROJ_SIZES + (0,):
        wide = size >= LANES
        if wide and run_start is None:
            run_start = src
        if not wide and run_start is not None:
            segs.append((run_start, dst, src - run_start))
            dst += src - run_start
            run_start = None
        src += size
    return segs


def _realign_kernel(wt_ref, o_ref, os_ref):
    for src, dst, n in _wide_segments():
        for r in range(0, n, LANES):
            o_ref[:, dst + r:dst + r + LANES] = wt_ref[src + r:src + r + LANES, :].T.astype(BF16)
    narrow, src = [], 0
    for size in PROJ_SIZES:
        if size < LANES:
            narrow.append(wt_ref[src:src + size, :])
        src += size
    used = sum(p.shape[0] for p in narrow)
    narrow.append(jnp.zeros((LANES - used, wt_ref.shape[1]), F32))
    os_ref[...] = jnp.concatenate(narrow, axis=0).T


def _realign(wt, rows):
    depth, n_proj, d_model = wt.shape
    return pl.pallas_call(
        _realign_kernel, grid=(depth, d_model // rows),
        in_specs=[pl.BlockSpec((None, n_proj, rows), lambda l, j: (l, 0, j))],
        out_specs=[pl.BlockSpec((None, rows, COL_S), lambda l, j: (l, j, 0)),
                   pl.BlockSpec((None, rows, LANES), lambda l, j: (l, j, 0))],
        out_shape=[jax.ShapeDtypeStruct((depth, d_model, COL_S), BF16),
                   jax.ShapeDtypeStruct((depth, d_model, LANES), F32)],
        compiler_params=pltpu.CompilerParams(dimension_semantics=("arbitrary", "arbitrary"),
                                             vmem_limit_bytes=VMEM_LIMIT),
        name="realign",
    )(wt)


def _causal_conv_silu(raw, xe_ref, hist_ref, w_ref, b_ref, taps):
    rows = raw.shape[0]
    xe_ref[0:HIST, :] = hist_ref[...]
    xe_ref[HIST:HIST + rows, :] = raw
    hist_ref[...] = raw[rows - HIST:rows, :]
    blk = 64
    out = []
    for r in range(0, rows, blk):
        acc = b_ref[...] + w_ref[taps - 1:taps, :] * xe_ref[HIST + r:HIST + r + blk, :]
        for k in range(taps - 1):
            off = HIST - (taps - 1) + k
            acc = acc + w_ref[k:k + 1, :] * xe_ref[off + r:off + r + blk, :]
        out.append(_silu(acc))
    return out


def _inproj_kernel(x_ref, nw_ref, w_ref, ws_ref, gw_ref, gb_ref, lb_ref, cwb_ref, cbb_ref, cwd_ref, cbd_ref,
                   gbias_ref, oa_ref, la_ref, ob_ref, oc_ref, lc_ref, od_ref, os_ref, xe_s, hist_b, hist_d,
                   *, tiles_per_seq):
    @pl.when(pl.program_id(0) % tiles_per_seq == 0)
    def _():
        hist_b[...] = jnp.zeros_like(hist_b)
        hist_d[...] = jnp.zeros_like(hist_d)

    x = x_ref[...]
    ms = jnp.mean(x * x, axis=-1, keepdims=True)
    u = (x * lax.rsqrt(ms + EPS) * nw_ref[...]).astype(BF16)
    lane = _lane_iota()
    blk = 64

    small = _dot(u, ws_ref[...].astype(BF16))
    biased = small + gbias_ref[...]
    is_f = (lane >= SM_F) & (lane < SM_F + MLSTM_HEADS)
    is_dt = (lane >= SM_DT) & (lane < SM_DT + SSD_HEADS)
    os_ref[...] = jnp.where(is_f, _log_sigmoid(biased), jnp.where(is_dt, _softplus(biased), biased))

    raw = _dot(u, w_ref[:, COL_B:COL_B + W_B])
    conv_w = 2 * D_BRANCH
    k_scale = jnp.where(lax.broadcasted_iota(jnp.int32, (1, conv_w), 1) < D_BRANCH, 1.0, MLSTM_DH ** -0.5)
    for i, blk_out in enumerate(_causal_conv_silu(raw[:, 0:conv_w], xe_s, hist_b, cwb_ref, cbb_ref,
                                                  MLSTM_CONV)):
        ob_ref[i * blk:(i + 1) * blk, 0:conv_w] = (blk_out * k_scale).astype(ob_ref.dtype)
    o0, z0 = 3 * D_BRANCH, 4 * D_BRANCH
    ob_ref[:, conv_w:o0] = (raw[:, conv_w:o0]).astype(ob_ref.dtype)
    ob_ref[:, o0:z0] = (jax.nn.sigmoid(raw[:, o0:z0])).astype(ob_ref.dtype)
    ob_ref[:, z0:W_B] = (_silu(raw[:, z0:W_B])).astype(ob_ref.dtype)

    raw = _dot(u, w_ref[:, COL_D:COL_D + W_D])
    conv_w = D_BRANCH + 2 * SSD_BC
    for i, blk_out in enumerate(_causal_conv_silu(raw[:, 0:conv_w], xe_s, hist_d, cwd_ref, cbd_ref,
                                                  SSD_CONV)):
        od_ref[i * blk:(i + 1) * blk, 0:conv_w] = (blk_out).astype(od_ref.dtype)
    od_ref[:, conv_w:W_D] = (_silu(raw[:, conv_w:W_D])).astype(od_ref.dtype)

    raw = _dot(u, w_ref[:, COL_C:COL_C + W_C])
    lb = lb_ref[...]
    fr = raw[:, HGRN_QF:2 * HGRN_QF]
    oc_ref[:, 0:HGRN_QF] = (raw[:, 0:HGRN_QF] * (HGRN_DK ** -0.5)).astype(oc_ref.dtype)
    oc_ref[:, HGRN_QF:2 * HGRN_QF] = ((1.0 - lb) * jax.nn.sigmoid(-fr)).astype(oc_ref.dtype)
    z0 = 2 * HGRN_QF + D_BRANCH
    oc_ref[:, 2 * HGRN_QF:z0] = (raw[:, 2 * HGRN_QF:z0]).astype(oc_ref.dtype)
    oc_ref[:, z0:W_C] = (_silu(raw[:, z0:W_C])).astype(oc_ref.dtype)
    lc_ref[...] = jnp.log(jnp.maximum(lb + (1.0 - lb) * jax.nn.sigmoid(fr), 1e-30))

    gate = _dot_f32(small, gw_ref[...]) + gb_ref[...]
    la_ref[...] = _log_sigmoid(gate) * (1.0 / GLA_GATE_NORM)
    raw = _dot(u, w_ref[:, COL_A:COL_A + W_A])
    oa_ref[:, 0:GLA_QK] = (raw[:, 0:GLA_QK] * (GLA_DK ** -0.5)).astype(oa_ref.dtype)
    z0 = 2 * GLA_QK + D_BRANCH
    oa_ref[:, GLA_QK:z0] = (raw[:, GLA_QK:z0]).astype(oa_ref.dtype)
    oa_ref[:, z0:W_A] = (_silu(raw[:, z0:W_A])).astype(oa_ref.dtype)


def _inproj(h, norm_w, w_wide, w_small, layer, consts, tile, tiles_per_seq):
    n_tok = h.shape[0]
    const = lambda i: (0, 0)
    row = lambda i: (i, 0)
    in_specs = [pl.BlockSpec((tile, D_MODEL), row), pl.BlockSpec((1, D_MODEL), const),
                pl.BlockSpec((None, D_MODEL, COL_S), lambda i: (layer, 0, 0), pipeline_mode=pl.Buffered(1)),
                pl.BlockSpec((None, D_MODEL, LANES), lambda i: (layer, 0, 0))]
    in_specs += [pl.BlockSpec(c.shape, const) for c in consts]
    outs = ((W_A, BF16), (GLA_QK, F32), (W_B, BF16), (W_C, BF16), (HGRN_QF, F32), (W_D, BF16), (LANES, F32))
    conv_w = 2 * D_BRANCH
    return pl.pallas_call(
        functools.partial(_inproj_kernel, tiles_per_seq=tiles_per_seq),
        grid=(n_tok // tile,), in_specs=in_specs,
        out_specs=[pl.BlockSpec((tile, w), row) for w, _ in outs],
        out_shape=[jax.ShapeDtypeStruct((n_tok, w), dt) for w, dt in outs],
        scratch_shapes=[pltpu.VMEM((tile + HIST, conv_w), F32), pltpu.VMEM((HIST, conv_w), F32),
                        pltpu.VMEM((HIST, conv_w), F32)],
        compiler_params=pltpu.CompilerParams(dimension_semantics=("arbitrary",),
                                             vmem_limit_bytes=VMEM_LIMIT),
        name="inproj",
    )(h, norm_w, w_wide, w_small, *consts)


def _outproj_kernel(h_ref, ya_ref, yb_ref, yc_ref, yd_ref, w_ref, fw_ref, o_ref, *, final):
    acc = h_ref[...]
    for i, y_ref in enumerate((ya_ref, yb_ref, yc_ref, yd_ref)):
        acc = acc + _dot(y_ref[...], w_ref[i * D_BRANCH:(i + 1) * D_BRANCH, :])
    if final:
        ms = jnp.mean(acc * acc, axis=-1, keepdims=True)
        acc = acc * lax.rsqrt(ms + EPS) * fw_ref[...]
    o_ref[...] = acc


def _outproj(h, ys, w_out, final_w, final, tile):
    n_tok = h.shape[0]
    const = lambda i: (0, 0)
    row = lambda i: (i, 0)
    in_specs = [pl.BlockSpec((tile, D_MODEL), row)]
    in_specs += [pl.BlockSpec((tile, D_BRANCH), row) for _ in ys]
    in_specs += [pl.BlockSpec(w_out.shape, const), pl.BlockSpec((1, D_MODEL), const)]
    return pl.pallas_call(
        functools.partial(_outproj_kernel, final=final),
        grid=(n_tok // tile,), in_specs=in_specs,
        out_specs=pl.BlockSpec((tile, D_MODEL), row),
        out_shape=jax.ShapeDtypeStruct((n_tok, D_MODEL), F32),
        compiler_params=pltpu.CompilerParams(dimension_semantics=("arbitrary",),
                                             vmem_limit_bytes=VMEM_LIMIT),
        name="outproj",
    )(h, *ys, w_out, final_w)


def _norm_gate_store(y_ref, rows, col0, parts, nw_ref, z_parts):
    width = sum(p.shape[-1] for p in parts)
    ss = sum(jnp.sum(p * p, axis=-1, keepdims=True) for p in parts)
    scale = lax.rsqrt(ss * (1.0 / width) + EPS)
    c = col0
    for p, z in zip(parts, z_parts):
        w = p.shape[-1]
        out = p * scale * nw_ref[:, c:c + w]
        if z is not None:
            out = out * z
        y_ref[rows, c:c + w] = out.astype(y_ref.dtype)
        c += w


N_STAGES = 5
STAGE_SKEW = (0,)


def _run_chunks(work, tile):
    def body(c, carry):
        rows = pl.ds(pl.multiple_of(c * CHUNK, CHUNK), CHUNK)
        live = [(dict(d), stages, STAGE_SKEW[i % len(STAGE_SKEW)]) for i, (d, stages) in enumerate(work)]
        for slot in range(N_STAGES + max(STAGE_SKEW)):
            for d, stages, skew in live:
                if 0 <= slot - skew < N_STAGES:
                    stages[slot - skew](d, rows)
        return carry

    lax.fori_loop(0, tile // CHUNK, body, 0)


def _gla_stages(ms_ref, mk_ref):
    lane = _lane_iota()

    def lane_mask(d, j):
        dk = d["dk"]
        return None if dk == LANES else (lane >= j * dk) & (lane < (j + 1) * dk)

    def pick(a, lm):
        return a if lm is None else jnp.where(lm, a, jnp.zeros_like(a))

    def n_groups(d):
        return d["q"].shape[1] // LANES

    def cumsums(d, rows):
        d["es"] = [_sel_dot2(ms_ref[...], d["lg"][rows, p * 2 * LANES:(p + 1) * 2 * LANES])
                   for p in range(n_groups(d) // 2)]

    def operands(d, rows):
        ops = []
        for g in range(n_groups(d)):
            ls = slice(g * LANES, (g + 1) * LANES)
            e = d["es"][g // 2][:, (g % 2) * LANES:(g % 2 + 1) * LANES]
            q = d["q"][rows, ls]
            k = d["k"][rows, ls]
            gcs = e[0:CHUNK]
            qg = (q * jnp.exp(gcs)).astype(BF16)
            kd = (k * jnp.exp(e[CHUNK:2 * CHUNK])).astype(BF16)
            dec = jnp.exp(gcs[CHUNK - 1:CHUNK, :])
            qb = q.astype(BF16)
            kb = k.astype(BF16)
            ql, kl = [qb], [kb]
            mxu_block = 2
            for l in range(N_LEVELS):
                s = CHUNK >> (l + 1)
                if s >= MIN_VPU_LEVEL:
                    el = _level_exponent(gcs, s)
                else:
                    el = e[mxu_block * CHUNK:(mxu_block + 1) * CHUNK]
                    mxu_block += 1
                w = jnp.exp(el).astype(BF16)
                ql.append(qb * w)
                kl.append(kb * w)
            ops.append((qg, kd, dec, ql, kl))
        d["ops"] = ops

    def head_operands(d, h, l):
        g, j = divmod(h, LANES // d["dk"])
        return pick(d["ops"][g][3][l], lane_mask(d, j)), d["ops"][g][4][l]

    def scores(d, rows):
        zero = jnp.zeros((CHUNK, LANES), BF16)
        pairs = []
        for p in range(n_groups(d) * (LANES // d["dk"]) // 2):
            a = None
            for l in range(N_LEVELS + 1):
                q0, k0 = head_operands(d, 2 * p, l)
                q1, k1 = head_operands(d, 2 * p + 1, l)
                s = _dot_nt(jnp.concatenate([q0, q1], axis=1), _side_by_side(k0, k1, zero)) * mk_ref[l]
                a = s if a is None else a + s
            pairs.append(a.astype(BF16))
        d["pairs"] = pairs

    def outputs(d, rows):
        hp = LANES // d["dk"]
        zero = jnp.zeros((CHUNK, LANES), BF16)
        outs, upds = [], [None] * n_groups(d)
        sts = [d["st"][g] for g in range(n_groups(d))]
        stbs = [st.astype(BF16) for st in sts]
        vbs = [d["v"][rows, h * LANES:(h + 1) * LANES].astype(BF16) for h in range(n_groups(d) * hp)]
        intra = [_dot(a, _side_by_side(vbs[2 * p], vbs[2 * p + 1], zero))
                 for p, a in enumerate(d["pairs"])]
        for h, vh in enumerate(vbs):
            g, j = divmod(h, hp)
            lm = lane_mask(d, j)
            qg, kd = d["ops"][g][0], d["ops"][g][1]
            outs.append(intra[h // 2][:, (h % 2) * LANES:(h % 2 + 1) * LANES]
                        + _dot_nt(pick(qg, lm), stbs[g]))
            u = _dot_tn(vh, kd)
            upds[g] = u if upds[g] is None else jnp.where(lm, u, upds[g])
        d["outs"], d["upds"], d["sts"] = outs, upds, sts

    def finish(d, rows):
        for g in range(n_groups(d)):
            d["st"][g] = d["sts"][g] * d["ops"][g][2] + d["upds"][g]
        for h, o in enumerate(d["outs"]):
            _norm_gate_store(d["y"], rows, h * LANES, [o], d["nw"],
                             [d["z"][rows, h * LANES:(h + 1) * LANES]])

    return cumsums, operands, scores, outputs, finish


def _gla_stream(p_ref, lg_ref, s, qk, dk, y_ref, st_s, nw_ref):
    v0, z0 = 2 * qk, 2 * qk + D_BRANCH
    return dict(q=p_ref.at[s, :, 0:qk], k=p_ref.at[s, :, qk:v0], v=p_ref.at[s, :, v0:z0],
                z=p_ref.at[s, :, z0:z0 + D_BRANCH], lg=lg_ref.at[s],
                y=y_ref.at[s], st=st_s.at[s], nw=nw_ref, dk=dk)


def _mlstm_constants():
    half = LANES // 2
    full0 = half * MLSTM_HEADS
    e = np.zeros((LANES, full0 + 2 * LANES * MLSTM_HEADS), np.float32)
    diff0 = full0 + LANES * MLSTM_HEADS
    sel = np.zeros((16, LANES), np.float32)
    for h in range(MLSTM_HEADS):
        e[SM_F + h, h * half:(h + 1) * half] = 1.0
        e[SM_F + h, full0 + h * LANES:full0 + (h + 1) * LANES] = 1.0
        e[SM_I + h, diff0 + h * LANES:diff0 + (h + 1) * LANES] = 1.0
        e[SM_F + h, diff0 + h * LANES:diff0 + (h + 1) * LANES] = -1.0
        sel[h // 2, SM_I + h] = 1.0
        sel[h // 2, SM_F + h] = -1.0
    return jnp.asarray(np.concatenate([e, e], axis=0), BF16), jnp.asarray(sel, BF16)


def _mlstm_stages(nw_ref, tri_ref, exp_ref, sel_ref):
    lane = _lane_iota()
    is_f = (lane >= SM_F) & (lane < SM_F + MLSTM_HEADS)

    half = LANES // 2
    ri = lax.broadcasted_iota(jnp.int32, (CHUNK, LANES), 0)
    ci = lax.broadcasted_iota(jnp.int32, (CHUNK, LANES), 1)
    causal2 = (ci & (half - 1)) <= ri
    lo_half = lane < half
    even = (lane & 1) == 0
    v0, o0, z0 = 2 * D_BRANCH, 3 * D_BRANCH, 4 * D_BRANCH
    heads = range(MLSTM_HEADS)
    pairs = range(MLSTM_HEADS // 2)
    full0 = half * MLSTM_HEADS
    diff0 = full0 + LANES * MLSTM_HEADS

    def qk_products(d, rows):
        qk_ref = d["p"]
        qbs = [qk_ref[rows, h * LANES:(h + 1) * LANES].astype(BF16) for h in heads]
        d["ks"] = [qk_ref[rows, D_BRANCH + h * LANES:D_BRANCH + (h + 1) * LANES] for h in heads]
        kbs = [k.astype(BF16) for k in d["ks"]]
        zero = jnp.zeros((CHUNK, LANES), BF16)
        ones = jnp.ones((CHUNK, LANES), BF16)
        d["qk_raw"] = [_dot_nt(jnp.concatenate([qbs[2 * p], qbs[2 * p + 1]], axis=1),
                               _side_by_side(kbs[2 * p], kbs[2 * p + 1], zero)) for p in pairs]
        d["qc"] = [_dot(qbs[h], d["c"][h].astype(BF16)) for h in heads]
        d["vaug"] = [jnp.concatenate([d["p"][rows, v0 + h * LANES:v0 + (h + 1) * LANES].astype(BF16),
                                      ones], axis=1) for h in heads]

    def gate_sums(d, rows):
        gc = d["g"][rows, :]
        bcol = _sel_dot(tri_ref[...], gc)
        ib = jnp.where(is_f, bcol, gc)
        d["ex"] = _dot_sel2(ib, exp_ref[...])
        by_parity = jnp.concatenate([jnp.where(even, ib, 0.0), jnp.where(even, 0.0, ib)], axis=0)
        d["drows"] = _sel_dot_nt(sel_ref[...], by_parity)

    def weights(d, rows):
        ex = d["ex"]
        mx = d["m"][0:1, :]
        ss, mrs = [], []
        for p in pairs:
            bx = ex[:, p * LANES:(p + 1) * LANES]
            lw = jnp.where(causal2, bx + d["drows"][p:p + 1, :], NEG_BIG)
            mr0 = jnp.max(jnp.where(lo_half, lw, NEG_BIG), axis=-1, keepdims=True)
            mr1 = jnp.max(jnp.where(lo_half, NEG_BIG, lw), axis=-1, keepdims=True)
            m64 = jnp.where(lo_half, mx[:, 2 * p * LANES:(2 * p + 1) * LANES],
                            mx[:, (2 * p + 1) * LANES:(2 * p + 2) * LANES])
            m_row = jnp.maximum(jnp.where(lo_half, mr0, mr1), bx + m64)
            ss.append((d["qk_raw"][p] * jnp.exp(lw - m_row)).astype(BF16))
            mrs += [mr0, mr1]
        d["ss"], d["mrs"] = ss, mrs
        b_last = ex[CHUNK - 1:CHUNK, full0:diff0]
        lwe = ex[:, diff0:] + b_last
        m_new = jnp.maximum(b_last + mx, jnp.max(lwe, axis=0, keepdims=True))
        d["cd"] = jnp.exp(b_last + mx - m_new)
        d["m_new"] = m_new
        kw = jnp.exp(lwe - m_new)
        d["kws"] = [(d["ks"][h] * kw[:, h * LANES:(h + 1) * LANES]).astype(BF16) for h in heads]

    def numerators(d, rows):
        zero = jnp.zeros((CHUNK, 2 * LANES), BF16)
        d["nums"] = [_dot(d["ss"][p], _side_by_side(d["vaug"][2 * p], d["vaug"][2 * p + 1], zero))
                     for p in pairs]
        d["cups"] = [_dot_tn(d["kws"][h], d["vaug"][h]) for h in heads]

    def finish(d, rows):
        ex = d["ex"]
        mx = d["m"][0:1, :]
        for h in heads:
            p, hd = divmod(h, 2)
            hs = slice(h * LANES, (h + 1) * LANES)
            m_inter = ex[:, full0 + h * LANES:full0 + (h + 1) * LANES] + mx[:, hs]
            m_row = jnp.maximum(d["mrs"][h], m_inter)
            inter = jnp.exp(m_inter - m_row)
            sv = d["nums"][p][:, hd * 2 * LANES:(hd + 1) * 2 * LANES]
            num = sv[:, 0:LANES] + inter * d["qc"][h][:, 0:LANES]
            den = sv[:, LANES:] + inter * d["qc"][h][:, LANES:]
            hh = num / jnp.maximum(jnp.abs(den), jnp.exp(-m_row))
            cd = d["cd"][:, hs]
            d["c"][h] = jnp.concatenate([cd, cd], axis=1) * d["c"][h] + d["cups"][h]
            og = d["p"][rows, o0 + h * LANES:o0 + (h + 1) * LANES]
            _norm_gate_store(d["y"], rows, h * LANES, [og * hh], nw_ref,
                             [d["p"][rows, z0 + h * LANES:z0 + (h + 1) * LANES]])
        d["m"][0:1, :] = d["m_new"]

    return qk_products, gate_sums, weights, numerators, finish


def _ssd_constants():
    e = np.zeros((LANES, SSD_HEADS * SSD_HEAD_DIM), np.float32)
    sel = np.zeros((16, LANES), np.float32)
    for h in range(SSD_HEADS):
        e[SM_DT + h, h * SSD_HEAD_DIM:(h + 1) * SSD_HEAD_DIM] = 1.0
        sel[h // 2, SM_DT + h] = 1.0
    return jnp.asarray(np.concatenate([e, e], axis=0), BF16), jnp.asarray(sel, BF16)


def _ssd_stages(alog_ref, dx_ref, nw_ref, tri_ref, sel_ref, exp_ref):
    lane = _lane_iota()
    is_dt = (lane >= SM_DT) & (lane < SM_DT + SSD_HEADS)

    half = LANES // 2
    ri = lax.broadcasted_iota(jnp.int32, (CHUNK, LANES), 0)
    ci = lax.broadcasted_iota(jnp.int32, (CHUNK, LANES), 1)
    causal2 = (ci & (half - 1)) <= ri
    lo_half = lane < half
    even = (lane & 1) == 0
    a_lane = jnp.where(is_dt, -jnp.exp(alog_ref[...]), 0.0)
    b0, c0, z0 = D_BRANCH, D_BRANCH + SSD_BC, D_BRANCH + 2 * SSD_BC
    groups = range(SSD_GROUPS)
    group_w = D_BRANCH // SSD_GROUPS
    pairs = range(D_BRANCH // LANES)
    pairs_per_group = group_w // LANES

    def products(d, rows):
        xbc = d["p"]
        d["bgs"] = [xbc[rows, b0 + g * SSD_STATE:b0 + (g + 1) * SSD_STATE].astype(BF16) for g in groups]
        cgbs = [xbc[rows, c0 + g * SSD_STATE:c0 + (g + 1) * SSD_STATE].astype(BF16) for g in groups]
        d["cb2"] = [_dot_nt(cgbs[g], jnp.concatenate([d["bgs"][g], d["bgs"][g]], axis=0))
                    for g in groups]
        d["cst"] = [_dot(cgbs[g], d["st"][g].astype(BF16)) for g in groups]

    def decay_sums(d, rows):
        dt = d["dt"][rows, :]
        acs = _sel_dot(tri_ref[...], dt * a_lane)
        ex = _dot_sel2(jnp.concatenate([dt, acs], axis=0), exp_ref[...])
        d["dtx"], d["acs_x"] = ex[0:CHUNK], ex[CHUNK:]
        by_parity = jnp.concatenate([jnp.where(even, acs, 0.0), jnp.where(even, 0.0, acs)], axis=0)
        d["a_rows"] = _sel_dot_nt(sel_ref[...], by_parity)

    def decays(d, rows):
        d["xss"], d["ms"], d["xblk"], xdecs = [], [], [], []
        for p in pairs:
            ls = slice(p * LANES, (p + 1) * LANES)
            ax = d["acs_x"][:, ls]
            lmat = jnp.exp(jnp.where(causal2, ax - d["a_rows"][p:p + 1, :], NEG_BIG))
            d["ms"].append((d["cb2"][p // pairs_per_group] * lmat).astype(BF16))
            xs = d["p"][rows, ls]
            xdt = xs * d["dtx"][:, ls]
            d["xss"].append(xs)
            d["xblk"].append(jnp.concatenate([jnp.where(lo_half, xdt, 0.0), jnp.where(lo_half, 0.0, xdt)],
                                             axis=0).astype(BF16))
            xdecs.append((xdt * jnp.exp(ax[CHUNK - 1:CHUNK, :] - ax)).astype(BF16))
        d["xdec"] = [jnp.concatenate(xdecs[g * pairs_per_group:(g + 1) * pairs_per_group], axis=1)
                     for g in groups]

    def chunk_products(d, rows):
        d["yds"] = [_dot(d["ms"][p], d["xblk"][p]) for p in pairs]
        d["ups"] = [_dot_tn(d["bgs"][g], d["xdec"][g]) for g in groups]

    def finish(d, rows):
        for g in groups:
            gs = slice(g * group_w, (g + 1) * group_w)
            eax = jnp.exp(d["acs_x"][:, gs])
            d["st"][g] = d["st"][g] * eax[CHUNK - 1:CHUNK, :] + d["ups"][g]
            ys = []
            for pp in range(pairs_per_group):
                p = g * pairs_per_group + pp
                ls = slice(p * LANES, (p + 1) * LANES)
                y = (d["cst"][g][:, pp * LANES:(pp + 1) * LANES] * eax[:, pp * LANES:(pp + 1) * LANES]
                     + dx_ref[:, ls] * d["xss"][p] + d["yds"][p])
                ys.append(y * d["p"][rows, z0 + p * LANES:z0 + (p + 1) * LANES])
            _norm_gate_store(d["y"], rows, g * group_w, ys, nw_ref, [None] * len(ys))

    return products, decay_sums, decays, chunk_products, finish


def _mixers_kernel(pa_ref, la_ref, pb_ref, pc_ref, lc_ref, pd_ref, sm_ref, nwa_ref, nwb_ref, nwc_ref, nwd_ref,
                   ms_ref, mk_ref, tri_ref, mexp_ref, msel_ref, alog_ref, dx_ref, dsel_ref, dexp_ref,
                   ya_ref, yb_ref, yc_ref, yd_ref, sta_s, stc_s, cb_s, mb_s, std_s, *, tile):
    @pl.when(pl.program_id(0) == 0)
    def _():
        for r in (sta_s, stc_s, cb_s, mb_s, std_s):
            r[...] = jnp.zeros_like(r)

    gla = _gla_stages(ms_ref, mk_ref)
    mlstm = _mlstm_stages(nwb_ref, tri_ref, mexp_ref, msel_ref)
    ssd = _ssd_stages(alog_ref, dx_ref, nwd_ref, tri_ref, dsel_ref, dexp_ref)
    work = []
    for s in range(pa_ref.shape[0]):
        work.append((_gla_stream(pa_ref, la_ref, s, GLA_QK, GLA_DK, ya_ref, sta_s, nwa_ref), gla))
        work.append((dict(p=pb_ref.at[s], g=sm_ref.at[s], c=cb_s.at[s], m=mb_s.at[s], y=yb_ref.at[s]),
                     mlstm))
        work.append((_gla_stream(pc_ref, lc_ref, s, HGRN_QF, HGRN_DK, yc_ref, stc_s, nwc_ref), gla))
        work.append((dict(p=pd_ref.at[s], dt=sm_ref.at[s], st=std_s.at[s], y=yd_ref.at[s]), ssd))
    _run_chunks(work, tile)


def _mixers(tok_inputs, const_inputs, n_batch, seq, tile):
    tok = lambda t: (0, t, 0)
    in_specs = [pl.BlockSpec((n_batch, tile, a.shape[2]), tok) for a in tok_inputs]
    for a in const_inputs:
        in_specs.append(pl.BlockSpec(a.shape, lambda t, nd=a.ndim: (0,) * nd))
    scratch = [pltpu.VMEM((n_batch, GLA_QK // LANES, GLA_DV, LANES), F32),
               pltpu.VMEM((n_batch, HGRN_QF // LANES, HGRN_DV, LANES), F32),
               pltpu.VMEM((n_batch, MLSTM_HEADS, MLSTM_DH, 2 * MLSTM_DH), F32),
               pltpu.VMEM((n_batch, 8, MLSTM_HEADS * LANES), F32),
               pltpu.VMEM((n_batch, SSD_GROUPS, SSD_STATE, D_BRANCH // SSD_GROUPS), F32)]
    return pl.pallas_call(
        functools.partial(_mixers_kernel, tile=tile), grid=(seq // tile,), in_specs=in_specs,
        out_specs=[pl.BlockSpec((n_batch, tile, D_BRANCH), tok)] * 4,
        out_shape=[jax.ShapeDtypeStruct((n_batch, seq, D_BRANCH), BF16)] * 4,
        scratch_shapes=scratch,
        compiler_params=pltpu.CompilerParams(dimension_semantics=("arbitrary",),
                                             vmem_limit_bytes=VMEM_LIMIT),
        name="mixers",
    )(*tok_inputs, *const_inputs)


def _pad_lanes(parts, total=LANES):
    width = sum(p.shape[-1] for p in parts)
    lead = parts[0].shape[:-1]
    return jnp.concatenate(list(parts) + [jnp.zeros(lead + (total - width,), parts[0].dtype)], axis=-1)


def _small_vector(i_part, f_part, dt_part):
    z = jnp.zeros((GLA_GATE_RANK,), F32)
    return _pad_lanes([z, i_part.astype(F32), f_part.astype(F32), dt_part.astype(F32)])[None, :]


def kernel(x, norm_w, w_in, gla_gate_w, gla_gate_b, gla_norm_w, ml_conv_w, ml_conv_b, ml_i_b, ml_f_b,
           ml_norm_w, hg_lb_logits, hg_norm_w, ssd_conv_w, ssd_conv_b, ssd_dt_bias, ssd_A_log, ssd_D,
           ssd_norm_w, w_out, final_norm_w):
    n_batch, seq, _ = x.shape
    depth = w_in.shape[0]
    tile = min(MIX_TILE, seq)
    n_tok = n_batch * seq

    mstack, masks = _gla_constants()
    tri_c = _tri(CHUNK)
    ml_exp, ml_sel = _mlstm_constants()
    ssd_exp, ssd_sel = _ssd_constants()
    zero4 = jnp.zeros((MLSTM_HEADS,), F32)
    in_tile = min(IN_TILE, seq)

    p = jax.nn.softmax(hg_lb_logits.astype(F32), axis=0)
    lower_bounds = jnp.cumsum(p, axis=0) - p[0:1]

    w_wide, w_small = _realign(jnp.swapaxes(w_in, 1, 2), REALIGN_ROWS)

    h = x.reshape(n_tok, D_MODEL)
    row2 = lambda v: v.astype(F32).reshape(1, -1)
    for l in range(depth):
        gate_w = jnp.concatenate(
            [gla_gate_w[l].astype(F32), jnp.zeros((LANES - GLA_GATE_RANK, GLA_QK), F32)], axis=0)
        consts = [gate_w, row2(gla_gate_b[l]), row2(lower_bounds[l]),
                  ml_conv_w[l].astype(F32), row2(ml_conv_b[l]),
                  ssd_conv_w[l].astype(F32), row2(ssd_conv_b[l]),
                  _small_vector(ml_i_b[l], ml_f_b[l], ssd_dt_bias[l])]
        pa, la, pb, pc, lc, pd, sm = [p.reshape(n_batch, seq, -1)
                              for p in _inproj(h, row2(norm_w[l]), w_wide, w_small, l, consts, in_tile,
                                               seq // in_tile)]

        ys = _mixers(
            [pa, la, pb, pc, lc, pd, sm],
            [row2(gla_norm_w[l]), row2(ml_norm_w[l]), row2(hg_norm_w[l]), row2(ssd_norm_w[l]),
             mstack, masks, tri_c, ml_exp, ml_sel, _small_vector(zero4, zero4, ssd_A_log[l]),
             row2(jnp.repeat(ssd_D[l].astype(F32), SSD_HEAD_DIM)), ssd_sel, ssd_exp],
            n_batch, seq, tile)
        ys = [y.reshape(n_tok, D_BRANCH) for y in ys]
        h = _outproj(h, ys, w_out[l].astype(BF16), row2(final_norm_w), l == depth - 1,
                     min(OUT_TILE, n_tok))
    return h.reshape(n_batch, seq, D_MODEL)
```

```python
import functools

import numpy as np
import jax
import jax.numpy as jnp
from jax import lax
from jax.experimental import pallas as pl
from jax.experimental.pallas import tpu as pltpu

F32 = jnp.float32
BF16 = jnp.bfloat16

D_MODEL = 1024
D_BRANCH = 512
EPS = 1e-6
NEG_BIG = -1e30

GLA_HEADS, GLA_DK, GLA_DV = 4, 64, 128
GLA_GATE_RANK, GLA_GATE_NORM = 16, 16.0
MLSTM_HEADS, MLSTM_DH, MLSTM_CONV = 4, 128, 4
HGRN_HEADS, HGRN_DK, HGRN_DV = 4, 128, 128
SSD_HEAD_DIM, SSD_HEADS, SSD_GROUPS, SSD_STATE, SSD_CONV = 64, 8, 2, 128, 4
GLA_QK = GLA_HEADS * GLA_DK
HGRN_QF = HGRN_HEADS * HGRN_DK
SSD_BC = SSD_GROUPS * SSD_STATE
PROJ_SIZES = (
    GLA_QK, GLA_QK, D_BRANCH, GLA_GATE_RANK, D_BRANCH,
    D_BRANCH, D_BRANCH, D_BRANCH, MLSTM_HEADS, MLSTM_HEADS, D_BRANCH, D_BRANCH,
    HGRN_QF, HGRN_QF, D_BRANCH, D_BRANCH,
    D_BRANCH, SSD_BC, SSD_BC, SSD_HEADS, D_BRANCH,
)

LANES = 128
HIST = 8
VMEM_LIMIT = 56 * 1024 * 1024

REALIGN_ROWS = 256
IN_TILE = 256
MIX_TILE = 256
OUT_TILE = 512

CHUNK = 64
N_LEVELS = 6
MIN_VPU_LEVEL = 4

SM_GR, SM_I, SM_F, SM_DT = 0, 16, 20, 24

W_A = 2 * GLA_QK + 2 * D_BRANCH
W_B = 5 * D_BRANCH
W_C = 2 * HGRN_QF + 2 * D_BRANCH
W_D = 2 * D_BRANCH + 2 * SSD_BC
COL_A = 0
COL_B = COL_A + W_A
COL_C = COL_B + W_B
COL_D = COL_C + W_C
COL_S = COL_D + W_D


def _gla_constants():
    c = CHUNK
    t = np.arange(c)[:, None]
    d = np.arange(c)[None, :]
    blocks = [(d <= t), (d > t)]
    masks = [np.eye(c, dtype=bool)]
    for l in range(N_LEVELS):
        s = c >> (l + 1)
        mid_t = (t // (2 * s)) * (2 * s) + s
        upper = t >= mid_t
        if s < MIN_VPU_LEVEL:
            blocks.append(np.where(upper, (d >= mid_t) & (d <= t), (d > t) & (d <= mid_t - 1)))
        same = (t // (2 * s)) == (d // (2 * s))
        masks.append(same & upper & (d < mid_t))
    mstack = np.concatenate(blocks, axis=0).astype(np.float32)
    mstack = np.concatenate([mstack, mstack], axis=1)
    masks = np.stack(masks).astype(np.float32)
    return jnp.asarray(mstack, BF16), jnp.asarray(np.concatenate([masks, masks], axis=-1))


def _tri(n):
    return jnp.asarray(np.tril(np.ones((n, n), np.float32)), BF16)


def _dot(a, b):
    return jnp.dot(a, b, preferred_element_type=F32)


def _dot_nt(a, b):
    return lax.dot_general(a, b, (((1,), (1,)), ((), ())), preferred_element_type=F32)


def _dot_tn(a, b):
    return lax.dot_general(a, b, (((0,), (0,)), ((), ())), preferred_element_type=F32)


def _split3(x):
    hi = x.astype(BF16)
    r1 = x - hi.astype(F32)
    mid = r1.astype(BF16)
    lo = (r1 - mid.astype(F32)).astype(BF16)
    return hi, mid, lo


def _sel_dot(sel, x):
    hi, mid, lo = _split3(x)
    return _dot(sel, hi) + _dot(sel, mid) + _dot(sel, lo)


def _sel_dot_nt(sel, x):
    hi, mid, lo = _split3(x)
    return _dot_nt(sel, hi) + _dot_nt(sel, mid) + _dot_nt(sel, lo)


def _sel_dot2(sel2, x):
    hi = x.astype(BF16)
    mid = (x - hi.astype(F32)).astype(BF16)
    return _dot(sel2, jnp.concatenate([hi, mid], axis=0))


def _level_exponent(gcs, s):
    pieces = []
    for b in range(0, CHUNK, 2 * s):
        ref_row = gcs[b + s - 1:b + s, :]
        if s >= 8:
            pieces += [ref_row - gcs[b:b + s], gcs[b + s:b + 2 * s] - ref_row]
        else:
            diff = gcs[b:b + 2 * s] - ref_row
            pieces.append(jnp.minimum(diff, -diff))
    return jnp.concatenate(pieces, axis=0)


def _dot_sel2(x, sel2):
    hi = x.astype(BF16)
    mid = (x - hi.astype(F32)).astype(BF16)
    return _dot(jnp.concatenate([hi, mid], axis=1), sel2)


def _dot_f32(a, b):
    ah = a.astype(BF16)
    al = (a - ah.astype(F32)).astype(BF16)
    bh = b.astype(BF16)
    bl = (b - bh.astype(F32)).astype(BF16)
    return _dot(ah, bh) + _dot(al, bh) + _dot(ah, bl)


def _softplus(x):
    return jnp.maximum(x, 0.0) + jnp.log1p(jnp.exp(-jnp.abs(x)))


def _log_sigmoid(x):
    return -_softplus(-x)


def _silu(x):
    return x * jax.nn.sigmoid(x)


def _lane_iota():
    return lax.broadcasted_iota(jnp.int32, (1, LANES), 1)


def _side_by_side(a, b, zero):
    return jnp.concatenate([jnp.concatenate([a, zero], axis=1),
                            jnp.concatenate([zero, b], axis=1)], axis=0)


def _wide_segments():
    segs, src, dst = [], 0, 0
    run_start = None
    for size in PROJ_SIZES + (0,):
        wide = size >= LANES
        if wide and run_start is None:
            run_start = src
        if not wide and run_start is not None:
            segs.append((run_start, dst, src - run_start))
            dst += src - run_start
            run_start = None
        src += size
    return segs


def _realign_kernel(wt_ref, o_ref, os_ref):
    for src, dst, n in _wide_segments():
        for r in range(0, n, LANES):
            o_ref[:, dst + r:dst + r + LANES] = wt_ref[src + r:src + r + LANES, :].T.astype(BF16)
    narrow, src = [], 0
    for size in PROJ_SIZES:
        if size < LANES:
            narrow.append(wt_ref[src:src + size, :])
        src += size
    used = sum(p.shape[0] for p in narrow)
    narrow.append(jnp.zeros((LANES - used, wt_ref.shape[1]), F32))
    os_ref[...] = jnp.concatenate(narrow, axis=0).T


def _realign(wt, rows):
    depth, n_proj, d_model = wt.shape
    return pl.pallas_call(
        _realign_kernel, grid=(depth, d_model // rows),
        in_specs=[pl.BlockSpec((None, n_proj, rows), lambda l, j: (l, 0, j))],
        out_specs=[pl.BlockSpec((None, rows, COL_S), lambda l, j: (l, j, 0)),
                   pl.BlockSpec((None, rows, LANES), lambda l, j: (l, j, 0))],
        out_shape=[jax.ShapeDtypeStruct((depth, d_model, COL_S), BF16),
                   jax.ShapeDtypeStruct((depth, d_model, LANES), F32)],
        compiler_params=pltpu.CompilerParams(dimension_semantics=("arbitrary", "arbitrary"),
                                             vmem_limit_bytes=VMEM_LIMIT),
        name="realign",
    )(wt)


CONV_ROWS, CONV_LANES = 256, 128


def _causal_conv_silu(raw, xe_ref, hist_ref, w_ref, b_ref, taps, out_ref, scale=None):
    rows, n = raw.shape
    xe_ref[0:HIST, :] = hist_ref[...]
    xe_ref[HIST:HIST + rows, :] = raw
    hist_ref[...] = raw[rows - HIST:rows, :]
    for r in range(0, rows, CONV_ROWS):
        for l in range(0, n, CONV_LANES):
            cs = slice(l, l + CONV_LANES)
            acc = b_ref[:, cs] + w_ref[taps - 1:taps, cs] * xe_ref[HIST + r:HIST + r + CONV_ROWS, cs]
            for k in range(taps - 1):
                off = HIST - (taps - 1) + k + r
                acc = acc + w_ref[k:k + 1, cs] * xe_ref[off:off + CONV_ROWS, cs]
            out = _silu(acc)
            out_ref[r:r + CONV_ROWS, cs] = out if scale is None else out * scale[:, cs]


def _inproj_kernel(x_ref, xn_ref, nw_ref, w_ref, ws_ref, gw_ref, gb_ref, lb_ref, cwb_ref, cbb_ref, cwd_ref,
                   cbd_ref, gbias_ref, oa_ref, ob_ref, oc_ref, od_ref, os_ref, xe_s, hist_b, hist_d, u_s,
                   *, tiles_per_seq):
    step = pl.program_id(0)

    @pl.when(step % tiles_per_seq == 0)
    def _():
        hist_b[...] = jnp.zeros_like(hist_b)
        hist_d[...] = jnp.zeros_like(hist_d)

    def normed(x):
        ms = jnp.mean(x * x, axis=-1, keepdims=True)
        return (x * lax.rsqrt(ms + EPS) * nw_ref[...]).astype(BF16)

    @pl.when(step == 0)
    def _():
        u_s[0] = normed(x_ref[...])

    u = u_s[step % 2]
    lane = _lane_iota()

    small = _dot(u, ws_ref[...].astype(BF16))
    biased = small + gbias_ref[...]
    is_f = (lane >= SM_F) & (lane < SM_F + MLSTM_HEADS)
    is_dt = (lane >= SM_DT) & (lane < SM_DT + SSD_HEADS)
    os_ref[...] = jnp.where(is_f, _log_sigmoid(biased), jnp.where(is_dt, _softplus(biased), biased))

    raw = _dot(u, w_ref[:, COL_B:COL_B + W_B])
    conv_w = 2 * D_BRANCH
    k_scale = jnp.where(lax.broadcasted_iota(jnp.int32, (1, conv_w), 1) < D_BRANCH, 1.0, MLSTM_DH ** -0.5)
    _causal_conv_silu(raw[:, 0:conv_w], xe_s, hist_b, cwb_ref, cbb_ref, MLSTM_CONV, ob_ref, k_scale)
    o0, z0 = 3 * D_BRANCH, 4 * D_BRANCH
    ob_ref[:, conv_w:o0] = raw[:, conv_w:o0]
    ob_ref[:, o0:z0] = jax.nn.sigmoid(raw[:, o0:z0])
    ob_ref[:, z0:W_B] = _silu(raw[:, z0:W_B])

    raw = _dot(u, w_ref[:, COL_D:COL_D + W_D])
    conv_w = D_BRANCH + 2 * SSD_BC
    _causal_conv_silu(raw[:, 0:conv_w], xe_s, hist_d, cwd_ref, cbd_ref, SSD_CONV, od_ref)
    od_ref[:, conv_w:W_D] = _silu(raw[:, conv_w:W_D])
    u_s[(step + 1) % 2] = normed(xn_ref[...])

    raw = _dot(u, w_ref[:, COL_C:COL_C + W_C])
    lb = lb_ref[...]
    fr = raw[:, HGRN_QF:2 * HGRN_QF]
    oc_ref[:, 0:HGRN_QF] = raw[:, 0:HGRN_QF] * (HGRN_DK ** -0.5)
    oc_ref[:, HGRN_QF:2 * HGRN_QF] = (1.0 - lb) * jax.nn.sigmoid(-fr)
    z0 = 2 * HGRN_QF + D_BRANCH
    oc_ref[:, 2 * HGRN_QF:z0] = raw[:, 2 * HGRN_QF:z0]
    oc_ref[:, z0:W_C] = _silu(raw[:, z0:W_C])
    oc_ref[:, W_C:W_C + HGRN_QF] = jnp.log(jnp.maximum(lb + (1.0 - lb) * jax.nn.sigmoid(fr), 1e-30))

    gate = _dot_f32(small, gw_ref[...]) + gb_ref[...]
    oa_ref[:, W_A:W_A + GLA_QK] = _log_sigmoid(gate) * (1.0 / GLA_GATE_NORM)
    raw = _dot(u, w_ref[:, COL_A:COL_A + W_A])
    oa_ref[:, 0:GLA_QK] = raw[:, 0:GLA_QK] * (GLA_DK ** -0.5)
    z0 = 2 * GLA_QK + D_BRANCH
    oa_ref[:, GLA_QK:z0] = raw[:, GLA_QK:z0]
    oa_ref[:, z0:W_A] = _silu(raw[:, z0:W_A])


def _inproj(h, norm_w, w_wide, w_small, layer, consts, tile, tiles_per_seq):
    n_tok = h.shape[0]
    const = lambda i: (0, 0)
    row = lambda i: (i, 0)
    n_tiles = n_tok // tile
    in_specs = [pl.BlockSpec((tile, D_MODEL), row),
                pl.BlockSpec((tile, D_MODEL), lambda i: (jnp.minimum(i + 1, n_tiles - 1), 0)),
                pl.BlockSpec((1, D_MODEL), const),
                pl.BlockSpec((None, D_MODEL, COL_S), lambda i: (layer, 0, 0), pipeline_mode=pl.Buffered(1)),
                pl.BlockSpec((None, D_MODEL, LANES), lambda i: (layer, 0, 0))]
    in_specs += [pl.BlockSpec(c.shape, const) for c in consts]
    widths = (W_A + GLA_QK, W_B, W_C + HGRN_QF, W_D, LANES)
    conv_w = 2 * D_BRANCH
    return pl.pallas_call(
        functools.partial(_inproj_kernel, tiles_per_seq=tiles_per_seq),
        grid=(n_tiles,), in_specs=in_specs,
        out_specs=[pl.BlockSpec((tile, w), row) for w in widths],
        out_shape=[jax.ShapeDtypeStruct((n_tok, w), F32) for w in widths],
        scratch_shapes=[pltpu.VMEM((tile + HIST, conv_w), F32), pltpu.VMEM((HIST, conv_w), F32),
                        pltpu.VMEM((HIST, conv_w), F32), pltpu.VMEM((2, tile, D_MODEL), BF16)],
        compiler_params=pltpu.CompilerParams(dimension_semantics=("arbitrary",),
                                             vmem_limit_bytes=VMEM_LIMIT),
        name="inproj",
    )(h, h, norm_w, w_wide, w_small, *consts)


def _outproj_kernel(h_ref, ya_ref, yb_ref, yc_ref, yd_ref, w_ref, fw_ref, o_ref, *, final):
    acc = h_ref[...]
    for i, y_ref in enumerate((ya_ref, yb_ref, yc_ref, yd_ref)):
        acc = acc + _dot(y_ref[...], w_ref[i * D_BRANCH:(i + 1) * D_BRANCH, :])
    if final:
        ms = jnp.mean(acc * acc, axis=-1, keepdims=True)
        acc = acc * lax.rsqrt(ms + EPS) * fw_ref[...]
    o_ref[...] = acc


def _outproj(h, ys, w_out, final_w, final, tile):
    n_tok = h.shape[0]
    const = lambda i: (0, 0)
    row = lambda i: (i, 0)
    in_specs = [pl.BlockSpec((tile, D_MODEL), row)]
    in_specs += [pl.BlockSpec((tile, D_BRANCH), row) for _ in ys]
    in_specs += [pl.BlockSpec(w_out.shape, const), pl.BlockSpec((1, D_MODEL), const)]
    return pl.pallas_call(
        functools.partial(_outproj_kernel, final=final),
        grid=(n_tok // tile,), in_specs=in_specs,
        out_specs=pl.BlockSpec((tile, D_MODEL), row),
        out_shape=jax.ShapeDtypeStruct((n_tok, D_MODEL), F32),
        compiler_params=pltpu.CompilerParams(dimension_semantics=("arbitrary",),
                                             vmem_limit_bytes=VMEM_LIMIT),
        name="outproj",
    )(h, *ys, w_out, final_w)


def _norm_gate_store(y_ref, rows, col0, parts, nw_ref, z_parts):
    width = sum(p.shape[-1] for p in parts)
    ss = sum(jnp.sum(p * p, axis=-1, keepdims=True) for p in parts)
    scale = lax.rsqrt(ss * (1.0 / width) + EPS)
    c = col0
    for p, z in zip(parts, z_parts):
        w = p.shape[-1]
        out = p * scale * nw_ref[:, c:c + w]
        if z is not None:
            out = out * z
        y_ref[rows, c:c + w] = out.astype(y_ref.dtype)
        c += w


N_STAGES = 5
CHUNK_UNROLL = 1
STAGE_SKEW = (0,)


def _run_chunks(work, tile):
    def body(c, carry):
        rows = pl.ds(pl.multiple_of(c * CHUNK, CHUNK), CHUNK)
        live = [(dict(d), stages, STAGE_SKEW[i % len(STAGE_SKEW)]) for i, (d, stages) in enumerate(work)]
        for slot in range(N_STAGES + max(STAGE_SKEW)):
            for d, stages, skew in live:
                if 0 <= slot - skew < N_STAGES:
                    stages[slot - skew](d, rows)
        return carry

    lax.fori_loop(0, tile // CHUNK, body, 0, unroll=CHUNK_UNROLL)


def _gla_stages(ms_ref, mk_ref):
    lane = _lane_iota()

    def lane_mask(d, j):
        dk = d["dk"]
        return None if dk == LANES else (lane >= j * dk) & (lane < (j + 1) * dk)

    def pick(a, lm):
        return a if lm is None else jnp.where(lm, a, jnp.zeros_like(a))

    def n_groups(d):
        return d["q"].shape[1] // LANES

    def cumsums(d, rows):
        d["es"] = [_sel_dot2(ms_ref[...], d["lg"][rows, p * 2 * LANES:(p + 1) * 2 * LANES])
                   for p in range(n_groups(d) // 2)]

    def operands(d, rows):
        ops = []
        for g in range(n_groups(d)):
            ls = slice(g * LANES, (g + 1) * LANES)
            e = d["es"][g // 2][:, (g % 2) * LANES:(g % 2 + 1) * LANES]
            q = d["q"][rows, ls]
            k = d["k"][rows, ls]
            gcs = e[0:CHUNK]
            qg = (q * jnp.exp(gcs)).astype(BF16)
            kd = (k * jnp.exp(e[CHUNK:2 * CHUNK])).astype(BF16)
            dec = jnp.exp(gcs[CHUNK - 1:CHUNK, :])
            qb = q.astype(BF16)
            kb = k.astype(BF16)
            ql, kl = [qb], [kb]
            mxu_block = 2
            for l in range(N_LEVELS):
                s = CHUNK >> (l + 1)
                if s >= MIN_VPU_LEVEL:
                    el = _level_exponent(gcs, s)
                else:
                    el = e[mxu_block * CHUNK:(mxu_block + 1) * CHUNK]
                    mxu_block += 1
                w = jnp.exp(el).astype(BF16)
                ql.append(qb * w)
                kl.append(kb * w)
            ops.append((qg, kd, dec, ql, kl))
        d["ops"] = ops

    def head_operands(d, h, l):
        g, j = divmod(h, LANES // d["dk"])
        return pick(d["ops"][g][3][l], lane_mask(d, j)), d["ops"][g][4][l]

    def scores(d, rows):
        zero = jnp.zeros((CHUNK, LANES), BF16)
        pairs = []
        for p in range(n_groups(d) * (LANES // d["dk"]) // 2):
            a = None
            for l in range(N_LEVELS + 1):
                q0, k0 = head_operands(d, 2 * p, l)
                q1, k1 = head_operands(d, 2 * p + 1, l)
                s = _dot_nt(jnp.concatenate([q0, q1], axis=1), _side_by_side(k0, k1, zero)) * mk_ref[l]
                a = s if a is None else a + s
            pairs.append(a.astype(BF16))
        d["pairs"] = pairs

    def outputs(d, rows):
        hp = LANES // d["dk"]
        zero = jnp.zeros((CHUNK, LANES), BF16)
        outs, upds = [], [None] * n_groups(d)
        sts = [d["st"][g] for g in range(n_groups(d))]
        stbs = [st.astype(BF16) for st in sts]
        vbs = [d["v"][rows, h * LANES:(h + 1) * LANES].astype(BF16) for h in range(n_groups(d) * hp)]
        intra = [_dot(a, _side_by_side(vbs[2 * p], vbs[2 * p + 1], zero))
                 for p, a in enumerate(d["pairs"])]
        for h, vh in enumerate(vbs):
            g, j = divmod(h, hp)
            lm = lane_mask(d, j)
            qg, kd = d["ops"][g][0], d["ops"][g][1]
            outs.append(intra[h // 2][:, (h % 2) * LANES:(h % 2 + 1) * LANES]
                        + _dot_nt(pick(qg, lm), stbs[g]))
            u = _dot_tn(vh, kd)
            upds[g] = u if upds[g] is None else jnp.where(lm, u, upds[g])
        d["outs"], d["upds"], d["sts"] = outs, upds, sts

    def finish(d, rows):
        for g in range(n_groups(d)):
            d["st"][g] = d["sts"][g] * d["ops"][g][2] + d["upds"][g]
        for h, o in enumerate(d["outs"]):
            _norm_gate_store(d["y"], rows, h * LANES, [o], d["nw"],
                             [d["z"][rows, h * LANES:(h + 1) * LANES]])

    return cumsums, operands, scores, outputs, finish


def _gla_stream(p_ref, s, qk, dk, y_ref, st_s, nw_ref):
    v0, z0, lg0 = 2 * qk, 2 * qk + D_BRANCH, 2 * qk + 2 * D_BRANCH
    return dict(q=p_ref.at[s, :, 0:qk], k=p_ref.at[s, :, qk:v0], v=p_ref.at[s, :, v0:z0],
                z=p_ref.at[s, :, z0:lg0], lg=p_ref.at[s, :, lg0:lg0 + qk],
                y=y_ref.at[s], st=st_s.at[s], nw=nw_ref, dk=dk)


def _mlstm_constants():
    half = LANES // 2
    full0 = half * MLSTM_HEADS
    e = np.zeros((LANES, full0 + 2 * LANES * MLSTM_HEADS), np.float32)
    diff0 = full0 + LANES * MLSTM_HEADS
    sel = np.zeros((16, LANES), np.float32)
    for h in range(MLSTM_HEADS):
        e[SM_F + h, h * half:(h + 1) * half] = 1.0
        e[SM_F + h, full0 + h * LANES:full0 + (h + 1) * LANES] = 1.0
        e[SM_I + h, diff0 + h * LANES:diff0 + (h + 1) * LANES] = 1.0
        e[SM_F + h, diff0 + h * LANES:diff0 + (h + 1) * LANES] = -1.0
        sel[h // 2, SM_I + h] = 1.0
        sel[h // 2, SM_F + h] = -1.0
    return jnp.asarray(np.concatenate([e, e], axis=0), BF16), jnp.asarray(sel, BF16)


def _mlstm_stages(nw_ref, tri_ref, exp_ref, sel_ref):
    lane = _lane_iota()
    is_f = (lane >= SM_F) & (lane < SM_F + MLSTM_HEADS)

    half = LANES // 2
    ri = lax.broadcasted_iota(jnp.int32, (CHUNK, LANES), 0)
    ci = lax.broadcasted_iota(jnp.int32, (CHUNK, LANES), 1)
    causal2 = (ci & (half - 1)) <= ri
    lo_half = lane < half
    even = (lane & 1) == 0
    v0, o0, z0 = 2 * D_BRANCH, 3 * D_BRANCH, 4 * D_BRANCH
    heads = range(MLSTM_HEADS)
    pairs = range(MLSTM_HEADS // 2)
    full0 = half * MLSTM_HEADS
    diff0 = full0 + LANES * MLSTM_HEADS

    def qk_products(d, rows):
        qk_ref = d["p"]
        qbs = [qk_ref[rows, h * LANES:(h + 1) * LANES].astype(BF16) for h in heads]
        d["ks"] = [qk_ref[rows, D_BRANCH + h * LANES:D_BRANCH + (h + 1) * LANES] for h in heads]
        kbs = [k.astype(BF16) for k in d["ks"]]
        zero = jnp.zeros((CHUNK, LANES), BF16)
        ones = jnp.ones((CHUNK, LANES), BF16)
        d["qk_raw"] = [_dot_nt(jnp.concatenate([qbs[2 * p], qbs[2 * p + 1]], axis=1),
                               _side_by_side(kbs[2 * p], kbs[2 * p + 1], zero)) for p in pairs]
        d["qc"] = [_dot(qbs[h], d["c"][h].astype(BF16)) for h in heads]
        d["vaug"] = [jnp.concatenate([d["p"][rows, v0 + h * LANES:v0 + (h + 1) * LANES].astype(BF16),
                                      ones], axis=1) for h in heads]

    def gate_sums(d, rows):
        gc = d["g"][rows, :]
        bcol = _sel_dot(tri_ref[...], gc)
        ib = jnp.where(is_f, bcol, gc)
        d["ex"] = _dot_sel2(ib, exp_ref[...])
        by_parity = jnp.concatenate([jnp.where(even, ib, 0.0), jnp.where(even, 0.0, ib)], axis=0)
        d["drows"] = _sel_dot_nt(sel_ref[...], by_parity)

    def weights(d, rows):
        ex = d["ex"]
        mx = d["m"][0:1, :]
        ss, mrs = [], []
        for p in pairs:
            bx = ex[:, p * LANES:(p + 1) * LANES]
            lw = jnp.where(causal2, bx + d["drows"][p:p + 1, :], NEG_BIG)
            mr0 = jnp.max(jnp.where(lo_half, lw, NEG_BIG), axis=-1, keepdims=True)
            mr1 = jnp.max(jnp.where(lo_half, NEG_BIG, lw), axis=-1, keepdims=True)
            m64 = jnp.where(lo_half, mx[:, 2 * p * LANES:(2 * p + 1) * LANES],
                            mx[:, (2 * p + 1) * LANES:(2 * p + 2) * LANES])
            m_row = jnp.maximum(jnp.where(lo_half, mr0, mr1), bx + m64)
            ss.append((d["qk_raw"][p] * jnp.exp(lw - m_row)).astype(BF16))
            mrs += [mr0, mr1]
        d["ss"], d["mrs"] = ss, mrs
        b_last = ex[CHUNK - 1:CHUNK, full0:diff0]
        lwe = ex[:, diff0:] + b_last
        m_new = jnp.maximum(b_last + mx, jnp.max(lwe, axis=0, keepdims=True))
        d["cd"] = jnp.exp(b_last + mx - m_new)
        d["m_new"] = m_new
        kw = jnp.exp(lwe - m_new)
        d["kws"] = [(d["ks"][h] * kw[:, h * LANES:(h + 1) * LANES]).astype(BF16) for h in heads]

    def numerators(d, rows):
        zero = jnp.zeros((CHUNK, 2 * LANES), BF16)
        d["nums"] = [_dot(d["ss"][p], _side_by_side(d["vaug"][2 * p], d["vaug"][2 * p + 1], zero))
                     for p in pairs]
        d["cups"] = [_dot_tn(d["kws"][h], d["vaug"][h]) for h in heads]

    def finish(d, rows):
        ex = d["ex"]
        mx = d["m"][0:1, :]
        for h in heads:
            p, hd = divmod(h, 2)
            hs = slice(h * LANES, (h + 1) * LANES)
            m_inter = ex[:, full0 + h * LANES:full0 + (h + 1) * LANES] + mx[:, hs]
            m_row = jnp.maximum(d["mrs"][h], m_inter)
            inter = jnp.exp(m_inter - m_row)
            sv = d["nums"][p][:, hd * 2 * LANES:(hd + 1) * 2 * LANES]
            num = sv[:, 0:LANES] + inter * d["qc"][h][:, 0:LANES]
            den = sv[:, LANES:] + inter * d["qc"][h][:, LANES:]
            hh = num / jnp.maximum(jnp.abs(den), jnp.exp(-m_row))
            cd = d["cd"][:, hs]
            d["c"][h] = jnp.concatenate([cd, cd], axis=1) * d["c"][h] + d["cups"][h]
            og = d["p"][rows, o0 + h * LANES:o0 + (h + 1) * LANES]
            _norm_gate_store(d["y"], rows, h * LANES, [og * hh], nw_ref,
                             [d["p"][rows, z0 + h * LANES:z0 + (h + 1) * LANES]])
        d["m"][0:1, :] = d["m_new"]

    return qk_products, gate_sums, weights, numerators, finish


def _ssd_constants():
    e = np.zeros((LANES, SSD_HEADS * SSD_HEAD_DIM), np.float32)
    sel = np.zeros((16, LANES), np.float32)
    for h in range(SSD_HEADS):
        e[SM_DT + h, h * SSD_HEAD_DIM:(h + 1) * SSD_HEAD_DIM] = 1.0
        sel[h // 2, SM_DT + h] = 1.0
    return jnp.asarray(np.concatenate([e, e], axis=0), BF16), jnp.asarray(sel, BF16)


def _ssd_stages(alog_ref, dx_ref, nw_ref, tri_ref, sel_ref, exp_ref):
    lane = _lane_iota()
    is_dt = (lane >= SM_DT) & (lane < SM_DT + SSD_HEADS)

    half = LANES // 2
    ri = lax.broadcasted_iota(jnp.int32, (CHUNK, LANES), 0)
    ci = lax.broadcasted_iota(jnp.int32, (CHUNK, LANES), 1)
    causal2 = (ci & (half - 1)) <= ri
    lo_half = lane < half
    even = (lane & 1) == 0
    a_lane = jnp.where(is_dt, -jnp.exp(alog_ref[...]), 0.0)
    b0, c0, z0 = D_BRANCH, D_BRANCH + SSD_BC, D_BRANCH + 2 * SSD_BC
    groups = range(SSD_GROUPS)
    group_w = D_BRANCH // SSD_GROUPS
    pairs = range(D_BRANCH // LANES)
    pairs_per_group = group_w // LANES

    def products(d, rows):
        xbc = d["p"]
        d["bgs"] = [xbc[rows, b0 + g * SSD_STATE:b0 + (g + 1) * SSD_STATE].astype(BF16) for g in groups]
        cgbs = [xbc[rows, c0 + g * SSD_STATE:c0 + (g + 1) * SSD_STATE].astype(BF16) for g in groups]
        d["cb2"] = [_dot_nt(cgbs[g], jnp.concatenate([d["bgs"][g], d["bgs"][g]], axis=0))
                    for g in groups]
        d["cst"] = [_dot(cgbs[g], d["st"][g].astype(BF16)) for g in groups]

    def decay_sums(d, rows):
        dt = d["dt"][rows, :]
        acs = _sel_dot(tri_ref[...], dt * a_lane)
        ex = _dot_sel2(jnp.concatenate([dt, acs], axis=0), exp_ref[...])
        d["dtx"], d["acs_x"] = ex[0:CHUNK], ex[CHUNK:]
        by_parity = jnp.concatenate([jnp.where(even, acs, 0.0), jnp.where(even, 0.0, acs)], axis=0)
        d["a_rows"] = _sel_dot_nt(sel_ref[...], by_parity)

    def decays(d, rows):
        d["xss"], d["ms"], d["xblk"], xdecs = [], [], [], []
        for p in pairs:
            ls = slice(p * LANES, (p + 1) * LANES)
            ax = d["acs_x"][:, ls]
            lmat = jnp.exp(jnp.where(causal2, ax - d["a_rows"][p:p + 1, :], NEG_BIG))
            d["ms"].append((d["cb2"][p // pairs_per_group] * lmat).astype(BF16))
            xs = d["p"][rows, ls]
            xdt = xs * d["dtx"][:, ls]
            d["xss"].append(xs)
            d["xblk"].append(jnp.concatenate([jnp.where(lo_half, xdt, 0.0), jnp.where(lo_half, 0.0, xdt)],
                                             axis=0).astype(BF16))
            xdecs.append((xdt * jnp.exp(ax[CHUNK - 1:CHUNK, :] - ax)).astype(BF16))
        d["xdec"] = [jnp.concatenate(xdecs[g * pairs_per_group:(g + 1) * pairs_per_group], axis=1)
                     for g in groups]

    def chunk_products(d, rows):
        d["yds"] = [_dot(d["ms"][p], d["xblk"][p]) for p in pairs]
        d["ups"] = [_dot_tn(d["bgs"][g], d["xdec"][g]) for g in groups]

    def finish(d, rows):
        for g in groups:
            gs = slice(g * group_w, (g + 1) * group_w)
            eax = jnp.exp(d["acs_x"][:, gs])
            d["st"][g] = d["st"][g] * eax[CHUNK - 1:CHUNK, :] + d["ups"][g]
            ys = []
            for pp in range(pairs_per_group):
                p = g * pairs_per_group + pp
                ls = slice(p * LANES, (p + 1) * LANES)
                y = (d["cst"][g][:, pp * LANES:(pp + 1) * LANES] * eax[:, pp * LANES:(pp + 1) * LANES]
                     + dx_ref[:, ls] * d["xss"][p] + d["yds"][p])
                ys.append(y * d["p"][rows, z0 + p * LANES:z0 + (p + 1) * LANES])
            _norm_gate_store(d["y"], rows, g * group_w, ys, nw_ref, [None] * len(ys))

    return products, decay_sums, decays, chunk_products, finish


def _mixers_kernel(pa_ref, pb_ref, pc_ref, pd_ref, sm_ref, nwa_ref, nwb_ref, nwc_ref, nwd_ref,
                   ms_ref, mk_ref, tri_ref, mexp_ref, msel_ref, alog_ref, dx_ref, dsel_ref, dexp_ref,
                   ya_ref, yb_ref, yc_ref, yd_ref, sta_s, stc_s, cb_s, mb_s, std_s, *, tile):
    @pl.when(pl.program_id(0) == 0)
    def _():
        for r in (sta_s, stc_s, cb_s, mb_s, std_s):
            r[...] = jnp.zeros_like(r)

    gla = _gla_stages(ms_ref, mk_ref)
    mlstm = _mlstm_stages(nwb_ref, tri_ref, mexp_ref, msel_ref)
    ssd = _ssd_stages(alog_ref, dx_ref, nwd_ref, tri_ref, dsel_ref, dexp_ref)
    batch = range(pa_ref.shape[0])
    work = [(_gla_stream(pc_ref, s, HGRN_QF, HGRN_DK, yc_ref, stc_s, nwc_ref), gla) for s in batch]
    work += [(dict(p=pb_ref.at[s], g=sm_ref.at[s], c=cb_s.at[s], m=mb_s.at[s], y=yb_ref.at[s]), mlstm)
             for s in batch]
    work += [(_gla_stream(pa_ref, s, GLA_QK, GLA_DK, ya_ref, sta_s, nwa_ref), gla) for s in batch]
    work += [(dict(p=pd_ref.at[s], dt=sm_ref.at[s], st=std_s.at[s], y=yd_ref.at[s]), ssd) for s in batch]
    _run_chunks(work, tile)


def _mixers(tok_inputs, const_inputs, n_batch, seq, tile):
    tok = lambda t: (0, t, 0)
    in_specs = [pl.BlockSpec((n_batch, tile, a.shape[2]), tok) for a in tok_inputs]
    for a in const_inputs:
        in_specs.append(pl.BlockSpec(a.shape, lambda t, nd=a.ndim: (0,) * nd))
    scratch = [pltpu.VMEM((n_batch, GLA_QK // LANES, GLA_DV, LANES), F32),
               pltpu.VMEM((n_batch, HGRN_QF // LANES, HGRN_DV, LANES), F32),
               pltpu.VMEM((n_batch, MLSTM_HEADS, MLSTM_DH, 2 * MLSTM_DH), F32),
               pltpu.VMEM((n_batch, 8, MLSTM_HEADS * LANES), F32),
               pltpu.VMEM((n_batch, SSD_GROUPS, SSD_STATE, D_BRANCH // SSD_GROUPS), F32)]
    return pl.pallas_call(
        functools.partial(_mixers_kernel, tile=tile), grid=(seq // tile,), in_specs=in_specs,
        out_specs=[pl.BlockSpec((n_batch, tile, D_BRANCH), tok)] * 4,
        out_shape=[jax.ShapeDtypeStruct((n_batch, seq, D_BRANCH), BF16)] * 4,
        scratch_shapes=scratch,
        compiler_params=pltpu.CompilerParams(dimension_semantics=("arbitrary",),
                                             vmem_limit_bytes=VMEM_LIMIT),
        name="mixers",
    )(*tok_inputs, *const_inputs)


def _pad_lanes(parts, total=LANES):
    width = sum(p.shape[-1] for p in parts)
    lead = parts[0].shape[:-1]
    return jnp.concatenate(list(parts) + [jnp.zeros(lead + (total - width,), parts[0].dtype)], axis=-1)


def _small_vector(i_part, f_part, dt_part):
    z = jnp.zeros((GLA_GATE_RANK,), F32)
    return _pad_lanes([z, i_part.astype(F32), f_part.astype(F32), dt_part.astype(F32)])[None, :]


def kernel(x, norm_w, w_in, gla_gate_w, gla_gate_b, gla_norm_w, ml_conv_w, ml_conv_b, ml_i_b, ml_f_b,
           ml_norm_w, hg_lb_logits, hg_norm_w, ssd_conv_w, ssd_conv_b, ssd_dt_bias, ssd_A_log, ssd_D,
           ssd_norm_w, w_out, final_norm_w):
    n_batch, seq, _ = x.shape
    depth = w_in.shape[0]
    tile = min(MIX_TILE, seq)
    n_tok = n_batch * seq

    mstack, masks = _gla_constants()
    tri_c = _tri(CHUNK)
    ml_exp, ml_sel = _mlstm_constants()
    ssd_exp, ssd_sel = _ssd_constants()
    zero4 = jnp.zeros((MLSTM_HEADS,), F32)
    in_tile = min(IN_TILE, seq)

    p = jax.nn.softmax(hg_lb_logits.astype(F32), axis=0)
    lower_bounds = jnp.cumsum(p, axis=0) - p[0:1]

    w_wide, w_small = _realign(jnp.swapaxes(w_in, 1, 2), REALIGN_ROWS)

    h = x.reshape(n_tok, D_MODEL)
    row2 = lambda v: v.astype(F32).reshape(1, -1)
    for l in range(depth):
        gate_w = jnp.concatenate(
            [gla_gate_w[l].astype(F32), jnp.zeros((LANES - GLA_GATE_RANK, GLA_QK), F32)], axis=0)
        consts = [gate_w, row2(gla_gate_b[l]), row2(lower_bounds[l]),
                  ml_conv_w[l].astype(F32), row2(ml_conv_b[l]),
                  ssd_conv_w[l].astype(F32), row2(ssd_conv_b[l]),
                  _small_vector(ml_i_b[l], ml_f_b[l], ssd_dt_bias[l])]
        pa, pb, pc, pd, sm = [p.reshape(n_batch, seq, -1)
                              for p in _inproj(h, row2(norm_w[l]), w_wide, w_small, l, consts, in_tile,
                                               seq // in_tile)]

        ys = _mixers(
            [pa, pb, pc, pd, sm],
            [row2(gla_norm_w[l]), row2(ml_norm_w[l]), row2(hg_norm_w[l]), row2(ssd_norm_w[l]),
             mstack, masks, tri_c, ml_exp, ml_sel, _small_vector(zero4, zero4, ssd_A_log[l]),
             row2(jnp.repeat(ssd_D[l].astype(F32), SSD_HEAD_DIM)), ssd_sel, ssd_exp],
            n_batch, seq, tile)
        ys = [y.reshape(n_tok, D_BRANCH) for y in ys]
        h = _outproj(h, ys, w_out[l].astype(BF16), row2(final_norm_w), l == depth - 1,
                     min(OUT_TILE, n_tok))
    return h.reshape(n_batch, seq, D_MODEL)
```

```python
import functools

import numpy as np
import jax
import jax.numpy as jnp
from jax import lax
from jax.experimental import pallas as pl
from jax.experimental.pallas import tpu as pltpu

F32 = jnp.float32
BF16 = jnp.bfloat16

D_MODEL = 1024
D_BRANCH = 512
EPS = 1e-6
NEG_BIG = -1e30

GLA_HEADS, GLA_DK, GLA_DV = 4, 64, 128
GLA_GATE_RANK, GLA_GATE_NORM = 16, 16.0
MLSTM_HEADS, MLSTM_DH, MLSTM_CONV = 4, 128, 4
HGRN_HEADS, HGRN_DK, HGRN_DV = 4, 128, 128
SSD_HEAD_DIM, SSD_HEADS, SSD_GROUPS, SSD_STATE, SSD_CONV = 64, 8, 2, 128, 4
GLA_QK = GLA_HEADS * GLA_DK
HGRN_QF = HGRN_HEADS * HGRN_DK
SSD_BC = SSD_GROUPS * SSD_STATE
PROJ_SIZES = (
    GLA_QK, GLA_QK, D_BRANCH, GLA_GATE_RANK, D_BRANCH,
    D_BRANCH, D_BRANCH, D_BRANCH, MLSTM_HEADS, MLSTM_HEADS, D_BRANCH, D_BRANCH,
    HGRN_QF, HGRN_QF, D_BRANCH, D_BRANCH,
    D_BRANCH, SSD_BC, SSD_BC, SSD_HEADS, D_BRANCH,
)

LANES = 128
HIST = 8
VMEM_LIMIT = 56 * 1024 * 1024

REALIGN_ROWS = 256
IN_TILE = 256
MIX_TILE = 256
OUT_TILE = 512

CHUNK = 64
N_LEVELS = 6
MIN_VPU_LEVEL = 4

SM_GR, SM_I, SM_F, SM_DT = 0, 16, 20, 24

W_A = 2 * GLA_QK + 2 * D_BRANCH
W_B = 5 * D_BRANCH
W_C = 2 * HGRN_QF + 2 * D_BRANCH
W_D = 2 * D_BRANCH + 2 * SSD_BC
COL_A = 0
COL_B = COL_A + W_A
COL_C = COL_B + W_B
COL_D = COL_C + W_C
COL_S = COL_D + W_D


def _gla_constants():
    c = CHUNK
    t = np.arange(c)[:, None]
    d = np.arange(c)[None, :]
    blocks = [(d <= t), (d > t)]
    masks = [np.eye(c, dtype=bool)]
    for l in range(N_LEVELS):
        s = c >> (l + 1)
        mid_t = (t // (2 * s)) * (2 * s) + s
        upper = t >= mid_t
        if s < MIN_VPU_LEVEL:
            blocks.append(np.where(upper, (d >= mid_t) & (d <= t), (d > t) & (d <= mid_t - 1)))
        same = (t // (2 * s)) == (d // (2 * s))
        masks.append(same & upper & (d < mid_t))
    mstack = np.concatenate(blocks, axis=0).astype(np.float32)
    mstack = np.concatenate([mstack, mstack], axis=1)
    masks = np.stack(masks).astype(np.float32)
    return jnp.asarray(mstack, BF16), jnp.asarray(np.concatenate([masks, masks], axis=-1))


def _tri(n):
    return jnp.asarray(np.tril(np.ones((n, n), np.float32)), BF16)


def _dot(a, b):
    return jnp.dot(a, b, preferred_element_type=F32)


def _dot_nt(a, b):
    return lax.dot_general(a, b, (((1,), (1,)), ((), ())), preferred_element_type=F32)


def _dot_tn(a, b):
    return lax.dot_general(a, b, (((0,), (0,)), ((), ())), preferred_element_type=F32)


def _split3(x):
    hi = x.astype(BF16)
    r1 = x - hi.astype(F32)
    mid = r1.astype(BF16)
    lo = (r1 - mid.astype(F32)).astype(BF16)
    return hi, mid, lo


def _sel_dot(sel, x):
    hi, mid, lo = _split3(x)
    return _dot(sel, hi) + _dot(sel, mid) + _dot(sel, lo)


def _sel_dot_nt(sel, x):
    hi, mid, lo = _split3(x)
    return _dot_nt(sel, hi) + _dot_nt(sel, mid) + _dot_nt(sel, lo)


def _sel_dot2(sel2, x):
    hi = x.astype(BF16)
    mid = (x - hi.astype(F32)).astype(BF16)
    return _dot(sel2, jnp.concatenate([hi, mid], axis=0))


def _level_exponent(gcs, s):
    pieces = []
    for b in range(0, CHUNK, 2 * s):
        ref_row = gcs[b + s - 1:b + s, :]
        if s >= 8:
            pieces += [ref_row - gcs[b:b + s], gcs[b + s:b + 2 * s] - ref_row]
        else:
            diff = gcs[b:b + 2 * s] - ref_row
            pieces.append(jnp.minimum(diff, -diff))
    return jnp.concatenate(pieces, axis=0)


def _dot_sel2(x, sel2):
    hi = x.astype(BF16)
    mid = (x - hi.astype(F32)).astype(BF16)
    return _dot(jnp.concatenate([hi, mid], axis=1), sel2)


def _dot_f32(a, b):
    ah = a.astype(BF16)
    al = (a - ah.astype(F32)).astype(BF16)
    bh = b.astype(BF16)
    bl = (b - bh.astype(F32)).astype(BF16)
    return _dot(ah, bh) + _dot(al, bh) + _dot(ah, bl)


def _softplus(x):
    return jnp.maximum(x, 0.0) + jnp.log1p(jnp.exp(-jnp.abs(x)))


def _log_sigmoid(x):
    return -_softplus(-x)


def _silu(x):
    return x * jax.nn.sigmoid(x)


def _lane_iota():
    return lax.broadcasted_iota(jnp.int32, (1, LANES), 1)


def _side_by_side(a, b, zero):
    return jnp.concatenate([jnp.concatenate([a, zero], axis=1),
                            jnp.concatenate([zero, b], axis=1)], axis=0)


def _wide_segments():
    segs, src, dst = [], 0, 0
    run_start = None
    for size in PROJ_SIZES + (0,):
        wide = size >= LANES
        if wide and run_start is None:
            run_start = src
        if not wide and run_start is not None:
            segs.append((run_start, dst, src - run_start))
            dst += src - run_start
            run_start = None
        src += size
    return segs


def _realign_kernel(wt_ref, o_ref, os_ref):
    for src, dst, n in _wide_segments():
        for r in range(0, n, LANES):
            o_ref[:, dst + r:dst + r + LANES] = wt_ref[src + r:src + r + LANES, :].T.astype(BF16)
    narrow, src = [], 0
    for size in PROJ_SIZES:
        if size < LANES:
            narrow.append(wt_ref[src:src + size, :])
        src += size
    used = sum(p.shape[0] for p in narrow)
    narrow.append(jnp.zeros((LANES - used, wt_ref.shape[1]), F32))
    os_ref[...] = jnp.concatenate(narrow, axis=0).T


def _realign(wt, rows):
    depth, n_proj, d_model = wt.shape
    return pl.pallas_call(
        _realign_kernel, grid=(depth, d_model // rows),
        in_specs=[pl.BlockSpec((None, n_proj, rows), lambda l, j: (l, 0, j))],
        out_specs=[pl.BlockSpec((None, rows, COL_S), lambda l, j: (l, j, 0)),
                   pl.BlockSpec((None, rows, LANES), lambda l, j: (l, j, 0))],
        out_shape=[jax.ShapeDtypeStruct((depth, d_model, COL_S), BF16),
                   jax.ShapeDtypeStruct((depth, d_model, LANES), F32)],
        compiler_params=pltpu.CompilerParams(dimension_semantics=("arbitrary", "arbitrary"),
                                             vmem_limit_bytes=VMEM_LIMIT),
        name="realign",
    )(wt)


CONV_ROWS, CONV_LANES = 256, 128


def _causal_conv_silu(raw, xe_ref, hist_ref, w_ref, b_ref, taps, out_ref, scale=None):
    rows, n = raw.shape
    xe_ref[0:HIST, :] = hist_ref[...]
    xe_ref[HIST:HIST + rows, :] = raw
    hist_ref[...] = raw[rows - HIST:rows, :]
    for r in range(0, rows, CONV_ROWS):
        for l in range(0, n, CONV_LANES):
            cs = slice(l, l + CONV_LANES)
            acc = b_ref[:, cs] + w_ref[taps - 1:taps, cs] * xe_ref[HIST + r:HIST + r + CONV_ROWS, cs]
            for k in range(taps - 1):
                off = HIST - (taps - 1) + k + r
                acc = acc + w_ref[k:k + 1, cs] * xe_ref[off:off + CONV_ROWS, cs]
            out = _silu(acc)
            out_ref[r:r + CONV_ROWS, cs] = out if scale is None else out * scale[:, cs]


def _inproj_kernel(x_ref, xn_ref, nw_ref, w_ref, ws_ref, gw_ref, gb_ref, lb_ref, cwb_ref, cbb_ref, cwd_ref,
                   cbd_ref, gbias_ref, oa_ref, ob_ref, oc_ref, od_ref, os_ref, xe_s, hist_b, hist_d, u_s,
                   *, tiles_per_seq):
    step = pl.program_id(0)

    @pl.when(step % tiles_per_seq == 0)
    def _():
        hist_b[...] = jnp.zeros_like(hist_b)
        hist_d[...] = jnp.zeros_like(hist_d)

    def normed(x):
        ms = jnp.mean(x * x, axis=-1, keepdims=True)
        return (x * lax.rsqrt(ms + EPS) * nw_ref[...]).astype(BF16)

    @pl.when(step == 0)
    def _():
        u_s[0] = normed(x_ref[...])

    u = u_s[step % 2]
    lane = _lane_iota()

    small = _dot(u, ws_ref[...].astype(BF16))
    biased = small + gbias_ref[...]
    is_f = (lane >= SM_F) & (lane < SM_F + MLSTM_HEADS)
    is_dt = (lane >= SM_DT) & (lane < SM_DT + SSD_HEADS)
    os_ref[...] = jnp.where(is_f, _log_sigmoid(biased), jnp.where(is_dt, _softplus(biased), biased))

    raw = _dot(u, w_ref[:, COL_B:COL_B + W_B])
    conv_w = 2 * D_BRANCH
    k_scale = jnp.where(lax.broadcasted_iota(jnp.int32, (1, conv_w), 1) < D_BRANCH, 1.0, MLSTM_DH ** -0.5)
    _causal_conv_silu(raw[:, 0:conv_w], xe_s, hist_b, cwb_ref, cbb_ref, MLSTM_CONV, ob_ref, k_scale)
    o0, z0 = 3 * D_BRANCH, 4 * D_BRANCH
    ob_ref[:, conv_w:o0] = raw[:, conv_w:o0]
    ob_ref[:, o0:z0] = jax.nn.sigmoid(raw[:, o0:z0])
    ob_ref[:, z0:W_B] = _silu(raw[:, z0:W_B])

    raw = _dot(u, w_ref[:, COL_D:COL_D + W_D])
    conv_w = D_BRANCH + 2 * SSD_BC
    _causal_conv_silu(raw[:, 0:conv_w], xe_s, hist_d, cwd_ref, cbd_ref, SSD_CONV, od_ref)
    od_ref[:, conv_w:W_D] = _silu(raw[:, conv_w:W_D])
    u_s[(step + 1) % 2] = normed(xn_ref[...])

    raw = _dot(u, w_ref[:, COL_C:COL_C + W_C])
    lb = lb_ref[...]
    fr = raw[:, HGRN_QF:2 * HGRN_QF]
    oc_ref[:, 0:HGRN_QF] = raw[:, 0:HGRN_QF] * (HGRN_DK ** -0.5)
    oc_ref[:, HGRN_QF:2 * HGRN_QF] = (1.0 - lb) * jax.nn.sigmoid(-fr)
    z0 = 2 * HGRN_QF + D_BRANCH
    oc_ref[:, 2 * HGRN_QF:z0] = raw[:, 2 * HGRN_QF:z0]
    oc_ref[:, z0:W_C] = _silu(raw[:, z0:W_C])
    oc_ref[:, W_C:W_C + HGRN_QF] = jnp.log(jnp.maximum(lb + (1.0 - lb) * jax.nn.sigmoid(fr), 1e-30))

    gate = _dot_f32(small, gw_ref[...]) + gb_ref[...]
    oa_ref[:, W_A:W_A + GLA_QK] = _log_sigmoid(gate) * (1.0 / GLA_GATE_NORM)
    raw = _dot(u, w_ref[:, COL_A:COL_A + W_A])
    oa_ref[:, 0:GLA_QK] = raw[:, 0:GLA_QK] * (GLA_DK ** -0.5)
    z0 = 2 * GLA_QK + D_BRANCH
    oa_ref[:, GLA_QK:z0] = raw[:, GLA_QK:z0]
    oa_ref[:, z0:W_A] = _silu(raw[:, z0:W_A])


def _inproj(h, norm_w, w_wide, w_small, layer, consts, tile, tiles_per_seq):
    n_tok = h.shape[0]
    const = lambda i: (0, 0)
    row = lambda i: (i, 0)
    n_tiles = n_tok // tile
    in_specs = [pl.BlockSpec((tile, D_MODEL), row),
                pl.BlockSpec((tile, D_MODEL), lambda i: (jnp.minimum(i + 1, n_tiles - 1), 0)),
                pl.BlockSpec((1, D_MODEL), const),
                pl.BlockSpec((None, D_MODEL, COL_S), lambda i: (layer, 0, 0), pipeline_mode=pl.Buffered(1)),
                pl.BlockSpec((None, D_MODEL, LANES), lambda i: (layer, 0, 0))]
    in_specs += [pl.BlockSpec(c.shape, const) for c in consts]
    widths = (W_A + GLA_QK, W_B, W_C + HGRN_QF, W_D, LANES)
    conv_w = 2 * D_BRANCH
    return pl.pallas_call(
        functools.partial(_inproj_kernel, tiles_per_seq=tiles_per_seq),
        grid=(n_tiles,), in_specs=in_specs,
        out_specs=[pl.BlockSpec((tile, w), row) for w in widths],
        out_shape=[jax.ShapeDtypeStruct((n_tok, w), F32) for w in widths],
        scratch_shapes=[pltpu.VMEM((tile + HIST, conv_w), F32), pltpu.VMEM((HIST, conv_w), F32),
                        pltpu.VMEM((HIST, conv_w), F32), pltpu.VMEM((2, tile, D_MODEL), BF16)],
        compiler_params=pltpu.CompilerParams(dimension_semantics=("arbitrary",),
                                             vmem_limit_bytes=VMEM_LIMIT),
        name="inproj",
    )(h, h, norm_w, w_wide, w_small, *consts)


def _outproj_kernel(h_ref, ya_ref, yb_ref, yc_ref, yd_ref, w_ref, fw_ref, o_ref, *, final):
    acc = h_ref[...]
    for i, y_ref in enumerate((ya_ref, yb_ref, yc_ref, yd_ref)):
        acc = acc + _dot(y_ref[...], w_ref[i * D_BRANCH:(i + 1) * D_BRANCH, :])
    if final:
        ms = jnp.mean(acc * acc, axis=-1, keepdims=True)
        acc = acc * lax.rsqrt(ms + EPS) * fw_ref[...]
    o_ref[...] = acc


def _outproj(h, ys, w_out, final_w, final, tile):
    n_tok = h.shape[0]
    const = lambda i: (0, 0)
    row = lambda i: (i, 0)
    in_specs = [pl.BlockSpec((tile, D_MODEL), row)]
    in_specs += [pl.BlockSpec((tile, D_BRANCH), row) for _ in ys]
    in_specs += [pl.BlockSpec(w_out.shape, const), pl.BlockSpec((1, D_MODEL), const)]
    return pl.pallas_call(
        functools.partial(_outproj_kernel, final=final),
        grid=(n_tok // tile,), in_specs=in_specs,
        out_specs=pl.BlockSpec((tile, D_MODEL), row),
        out_shape=jax.ShapeDtypeStruct((n_tok, D_MODEL), F32),
        compiler_params=pltpu.CompilerParams(dimension_semantics=("arbitrary",),
                                             vmem_limit_bytes=VMEM_LIMIT),
        name="outproj",
    )(h, *ys, w_out, final_w)


def _norm_gate_store(y_ref, rows, col0, parts, nw_ref, z_parts):
    width = sum(p.shape[-1] for p in parts)
    ss = sum(jnp.sum(p * p, axis=-1, keepdims=True) for p in parts)
    scale = lax.rsqrt(ss * (1.0 / width) + EPS)
    c = col0
    for p, z in zip(parts, z_parts):
        w = p.shape[-1]
        out = p * scale * nw_ref[:, c:c + w]
        if z is not None:
            out = out * z
        y_ref[rows, c:c + w] = out.astype(y_ref.dtype)
        c += w


N_STAGES = 5
CHUNK_UNROLL = 1
STAGE_SKEW = (0,)


def _run_chunks(work, tile):
    def body(c, carry):
        rows = pl.ds(pl.multiple_of(c * CHUNK, CHUNK), CHUNK)
        live = [(dict(d), stages, STAGE_SKEW[i % len(STAGE_SKEW)]) for i, (d, stages) in enumerate(work)]
        for slot in range(N_STAGES + max(STAGE_SKEW)):
            for d, stages, skew in live:
                if 0 <= slot - skew < N_STAGES:
                    stages[slot - skew](d, rows)
        return carry

    lax.fori_loop(0, tile // CHUNK, body, 0, unroll=CHUNK_UNROLL)


def _gla_stages(ms_ref, mk_ref):
    lane = _lane_iota()

    def lane_mask(d, j):
        dk = d["dk"]
        return None if dk == LANES else (lane >= j * dk) & (lane < (j + 1) * dk)

    def pick(a, lm):
        return a if lm is None else jnp.where(lm, a, jnp.zeros_like(a))

    def n_groups(d):
        return d["q"].shape[1] // LANES

    def cumsums(d, rows):
        d["es"] = [_sel_dot2(ms_ref[...], d["lg"][rows, p * 2 * LANES:(p + 1) * 2 * LANES])
                   for p in range(n_groups(d) // 2)]

    def operands(d, rows):
        ops = []
        for g in range(n_groups(d)):
            ls = slice(g * LANES, (g + 1) * LANES)
            e = d["es"][g // 2][:, (g % 2) * LANES:(g % 2 + 1) * LANES]
            q = d["q"][rows, ls]
            k = d["k"][rows, ls]
            gcs = e[0:CHUNK]
            qg = (q * jnp.exp(gcs)).astype(BF16)
            kd = (k * jnp.exp(e[CHUNK:2 * CHUNK])).astype(BF16)
            dec = jnp.exp(gcs[CHUNK - 1:CHUNK, :])
            qb = q.astype(BF16)
            kb = k.astype(BF16)
            ql, kl = [qb], [kb]
            mxu_block = 2
            for l in range(N_LEVELS):
                s = CHUNK >> (l + 1)
                if s >= MIN_VPU_LEVEL:
                    el = _level_exponent(gcs, s)
                else:
                    el = e[mxu_block * CHUNK:(mxu_block + 1) * CHUNK]
                    mxu_block += 1
                w = jnp.exp(el).astype(BF16)
                ql.append(qb * w)
                kl.append(kb * w)
            ops.append((qg, kd, dec, ql, kl))
        d["ops"] = ops

    def head_operands(d, h, l):
        g, j = divmod(h, LANES // d["dk"])
        return pick(d["ops"][g][3][l], lane_mask(d, j)), d["ops"][g][4][l]

    def scores(d, rows):
        zero = jnp.zeros((CHUNK, LANES), BF16)
        pairs = []
        for p in range(n_groups(d) * (LANES // d["dk"]) // 2):
            a = None
            for l in range(N_LEVELS + 1):
                q0, k0 = head_operands(d, 2 * p, l)
                q1, k1 = head_operands(d, 2 * p + 1, l)
                s = _dot_nt(jnp.concatenate([q0, q1], axis=1), _side_by_side(k0, k1, zero)) * mk_ref[l]
                a = s if a is None else a + s
            pairs.append(a.astype(BF16))
        d["pairs"] = pairs

    def outputs(d, rows):
        hp = LANES // d["dk"]
        zero = jnp.zeros((CHUNK, LANES), BF16)
        outs, upds = [], [None] * n_groups(d)
        sts = [d["st"][g] for g in range(n_groups(d))]
        stbs = [st.astype(BF16) for st in sts]
        vbs = [d["v"][rows, h * LANES:(h + 1) * LANES].astype(BF16) for h in range(n_groups(d) * hp)]
        intra = [_dot(a, _side_by_side(vbs[2 * p], vbs[2 * p + 1], zero))
                 for p, a in enumerate(d["pairs"])]
        for h, vh in enumerate(vbs):
            g, j = divmod(h, hp)
            lm = lane_mask(d, j)
            qg, kd = d["ops"][g][0], d["ops"][g][1]
            outs.append(intra[h // 2][:, (h % 2) * LANES:(h % 2 + 1) * LANES]
                        + _dot_nt(pick(qg, lm), stbs[g]))
            u = _dot_tn(vh, kd)
            upds[g] = u if upds[g] is None else jnp.where(lm, u, upds[g])
        d["outs"], d["upds"], d["sts"] = outs, upds, sts

    def finish(d, rows):
        for g in range(n_groups(d)):
            d["st"][g] = d["sts"][g] * d["ops"][g][2] + d["upds"][g]
        for h, o in enumerate(d["outs"]):
            _norm_gate_store(d["y"], rows, h * LANES, [o], d["nw"],
                             [d["z"][rows, h * LANES:(h + 1) * LANES]])

    return cumsums, operands, scores, outputs, finish


def _gla_stream(p_ref, s, qk, dk, y_ref, st_s, nw_ref):
    v0, z0, lg0 = 2 * qk, 2 * qk + D_BRANCH, 2 * qk + 2 * D_BRANCH
    return dict(q=p_ref.at[s, :, 0:qk], k=p_ref.at[s, :, qk:v0], v=p_ref.at[s, :, v0:z0],
                z=p_ref.at[s, :, z0:lg0], lg=p_ref.at[s, :, lg0:lg0 + qk],
                y=y_ref.at[s], st=st_s.at[s], nw=nw_ref, dk=dk)


def _mlstm_constants():
    half = LANES // 2
    full0 = half * MLSTM_HEADS
    e = np.zeros((LANES, full0 + 2 * LANES * MLSTM_HEADS), np.float32)
    diff0 = full0 + LANES * MLSTM_HEADS
    sel = np.zeros((16, LANES), np.float32)
    for h in range(MLSTM_HEADS):
        e[SM_F + h, h * half:(h + 1) * half] = 1.0
        e[SM_F + h, full0 + h * LANES:full0 + (h + 1) * LANES] = 1.0
        e[SM_I + h, diff0 + h * LANES:diff0 + (h + 1) * LANES] = 1.0
        e[SM_F + h, diff0 + h * LANES:diff0 + (h + 1) * LANES] = -1.0
        sel[h // 2, SM_I + h] = 1.0
        sel[h // 2, SM_F + h] = -1.0
    return jnp.asarray(np.concatenate([e, e], axis=0), BF16), jnp.asarray(sel, BF16)


def _mlstm_stages(nw_ref, tri_ref, exp_ref, sel_ref):
    lane = _lane_iota()
    is_f = (lane >= SM_F) & (lane < SM_F + MLSTM_HEADS)

    half = LANES // 2
    ri = lax.broadcasted_iota(jnp.int32, (CHUNK, LANES), 0)
    ci = lax.broadcasted_iota(jnp.int32, (CHUNK, LANES), 1)
    causal2 = (ci & (half - 1)) <= ri
    lo_half = lane < half
    even = (lane & 1) == 0
    v0, o0, z0 = 2 * D_BRANCH, 3 * D_BRANCH, 4 * D_BRANCH
    heads = range(MLSTM_HEADS)
    pairs = range(MLSTM_HEADS // 2)
    full0 = half * MLSTM_HEADS
    diff0 = full0 + LANES * MLSTM_HEADS

    def qk_products(d, rows):
        qk_ref = d["p"]
        qbs = [qk_ref[rows, h * LANES:(h + 1) * LANES].astype(BF16) for h in heads]
        d["ks"] = [qk_ref[rows, D_BRANCH + h * LANES:D_BRANCH + (h + 1) * LANES] for h in heads]
        kbs = [k.astype(BF16) for k in d["ks"]]
        zero = jnp.zeros((CHUNK, LANES), BF16)
        ones = jnp.ones((CHUNK, LANES), BF16)
        d["qk_raw"] = [_dot_nt(jnp.concatenate([qbs[2 * p], qbs[2 * p + 1]], axis=1),
                               _side_by_side(kbs[2 * p], kbs[2 * p + 1], zero)) for p in pairs]
        d["qc"] = [_dot(qbs[h], d["c"][h].astype(BF16)) for h in heads]
        d["vaug"] = [jnp.concatenate([d["p"][rows, v0 + h * LANES:v0 + (h + 1) * LANES].astype(BF16),
                                      ones], axis=1) for h in heads]

    def gate_sums(d, rows):
        gc = d["g"][rows, :]
        bcol = _sel_dot(tri_ref[...], gc)
        ib = jnp.where(is_f, bcol, gc)
        d["ex"] = _dot_sel2(ib, exp_ref[...])
        by_parity = jnp.concatenate([jnp.where(even, ib, 0.0), jnp.where(even, 0.0, ib)], axis=0)
        d["drows"] = _sel_dot_nt(sel_ref[...], by_parity)

    def weights(d, rows):
        ex = d["ex"]
        mx = d["m"][0:1, :]
        ss, mrs = [], []
        for p in pairs:
            bx = ex[:, p * LANES:(p + 1) * LANES]
            lw = jnp.where(causal2, bx + d["drows"][p:p + 1, :], NEG_BIG)
            mr0 = jnp.max(jnp.where(lo_half, lw, NEG_BIG), axis=-1, keepdims=True)
            mr1 = jnp.max(jnp.where(lo_half, NEG_BIG, lw), axis=-1, keepdims=True)
            m64 = jnp.where(lo_half, mx[:, 2 * p * LANES:(2 * p + 1) * LANES],
                            mx[:, (2 * p + 1) * LANES:(2 * p + 2) * LANES])
            m_row = jnp.maximum(jnp.where(lo_half, mr0, mr1), bx + m64)
            ss.append((d["qk_raw"][p] * jnp.exp(lw - m_row)).astype(BF16))
            mrs += [mr0, mr1]
        d["ss"], d["mrs"] = ss, mrs
        b_last = ex[CHUNK - 1:CHUNK, full0:diff0]
        lwe = ex[:, diff0:] + b_last
        m_new = jnp.maximum(b_last + mx, jnp.max(lwe, axis=0, keepdims=True))
        d["cd"] = jnp.exp(b_last + mx - m_new)
        d["m_new"] = m_new
        kw = jnp.exp(lwe - m_new)
        d["kws"] = [(d["ks"][h] * kw[:, h * LANES:(h + 1) * LANES]).astype(BF16) for h in heads]

    def numerators(d, rows):
        zero = jnp.zeros((CHUNK, 2 * LANES), BF16)
        d["nums"] = [_dot(d["ss"][p], _side_by_side(d["vaug"][2 * p], d["vaug"][2 * p + 1], zero))
                     for p in pairs]
        d["cups"] = [_dot_tn(d["kws"][h], d["vaug"][h]) for h in heads]

    def finish(d, rows):
        ex = d["ex"]
        mx = d["m"][0:1, :]
        for h in heads:
            p, hd = divmod(h, 2)
            hs = slice(h * LANES, (h + 1) * LANES)
            m_inter = ex[:, full0 + h * LANES:full0 + (h + 1) * LANES] + mx[:, hs]
            m_row = jnp.maximum(d["mrs"][h], m_inter)
            inter = jnp.exp(m_inter - m_row)
            sv = d["nums"][p][:, hd * 2 * LANES:(hd + 1) * 2 * LANES]
            num = sv[:, 0:LANES] + inter * d["qc"][h][:, 0:LANES]
            den = sv[:, LANES:] + inter * d["qc"][h][:, LANES:]
            hh = num / jnp.maximum(jnp.abs(den), jnp.exp(-m_row))
            cd = d["cd"][:, hs]
            d["c"][h] = jnp.concatenate([cd, cd], axis=1) * d["c"][h] + d["cups"][h]
            og = d["p"][rows, o0 + h * LANES:o0 + (h + 1) * LANES]
            _norm_gate_store(d["y"], rows, h * LANES, [og * hh], nw_ref,
                             [d["p"][rows, z0 + h * LANES:z0 + (h + 1) * LANES]])
        d["m"][0:1, :] = d["m_new"]

    return qk_products, gate_sums, weights, numerators, finish


def _ssd_constants():
    e = np.zeros((LANES, SSD_HEADS * SSD_HEAD_DIM), np.float32)
    sel = np.zeros((16, LANES), np.float32)
    for h in range(SSD_HEADS):
        e[SM_DT + h, h * SSD_HEAD_DIM:(h + 1) * SSD_HEAD_DIM] = 1.0
        sel[h // 2, SM_DT + h] = 1.0
    return jnp.asarray(np.concatenate([e, e], axis=0), BF16), jnp.asarray(sel, BF16)


def _ssd_stages(alog_ref, dx_ref, nw_ref, tri_ref, sel_ref, exp_ref):
    lane = _lane_iota()
    is_dt = (lane >= SM_DT) & (lane < SM_DT + SSD_HEADS)

    half = LANES // 2
    ri = lax.broadcasted_iota(jnp.int32, (CHUNK, LANES), 0)
    ci = lax.broadcasted_iota(jnp.int32, (CHUNK, LANES), 1)
    causal2 = (ci & (half - 1)) <= ri
    lo_half = lane < half
    even = (lane & 1) == 0
    a_lane = jnp.where(is_dt, -jnp.exp(alog_ref[...]), 0.0)
    b0, c0, z0 = D_BRANCH, D_BRANCH + SSD_BC, D_BRANCH + 2 * SSD_BC
    groups = range(SSD_GROUPS)
    group_w = D_BRANCH // SSD_GROUPS
    pairs = range(D_BRANCH // LANES)
    pairs_per_group = group_w // LANES

    def products(d, rows):
        xbc = d["p"]
        d["bgs"] = [xbc[rows, b0 + g * SSD_STATE:b0 + (g + 1) * SSD_STATE].astype(BF16) for g in groups]
        cgbs = [xbc[rows, c0 + g * SSD_STATE:c0 + (g + 1) * SSD_STATE].astype(BF16) for g in groups]
        d["cb2"] = [_dot_nt(cgbs[g], jnp.concatenate([d["bgs"][g], d["bgs"][g]], axis=0))
                    for g in groups]
        d["cst"] = [_dot(cgbs[g], d["st"][g].astype(BF16)) for g in groups]

    def decay_sums(d, rows):
        dt = d["dt"][rows, :]
        acs = _sel_dot(tri_ref[...], dt * a_lane)
        ex = _dot_sel2(jnp.concatenate([dt, acs], axis=0), exp_ref[...])
        d["dtx"], d["acs_x"] = ex[0:CHUNK], ex[CHUNK:]
        by_parity = jnp.concatenate([jnp.where(even, acs, 0.0), jnp.where(even, 0.0, acs)], axis=0)
        d["a_rows"] = _sel_dot_nt(sel_ref[...], by_parity)

    def decays(d, rows):
        d["xss"], d["ms"], d["xblk"], xdecs = [], [], [], []
        for p in pairs:
            ls = slice(p * LANES, (p + 1) * LANES)
            ax = d["acs_x"][:, ls]
            lmat = jnp.exp(jnp.where(causal2, ax - d["a_rows"][p:p + 1, :], NEG_BIG))
            d["ms"].append((d["cb2"][p // pairs_per_group] * lmat).astype(BF16))
            xs = d["p"][rows, ls]
            xdt = xs * d["dtx"][:, ls]
            d["xss"].append(xs)
            d["xblk"].append(jnp.concatenate([jnp.where(lo_half, xdt, 0.0), jnp.where(lo_half, 0.0, xdt)],
                                             axis=0).astype(BF16))
            xdecs.append((xdt * jnp.exp(ax[CHUNK - 1:CHUNK, :] - ax)).astype(BF16))
        d["xdec"] = [jnp.concatenate(xdecs[g * pairs_per_group:(g + 1) * pairs_per_group], axis=1)
                     for g in groups]

    def chunk_products(d, rows):
        d["yds"] = [_dot(d["ms"][p], d["xblk"][p]) for p in pairs]
        d["ups"] = [_dot_tn(d["bgs"][g], d["xdec"][g]) for g in groups]

    def finish(d, rows):
        for g in groups:
            gs = slice(g * group_w, (g + 1) * group_w)
            eax = jnp.exp(d["acs_x"][:, gs])
            d["st"][g] = d["st"][g] * eax[CHUNK - 1:CHUNK, :] + d["ups"][g]
            ys = []
            for pp in range(pairs_per_group):
                p = g * pairs_per_group + pp
                ls = slice(p * LANES, (p + 1) * LANES)
                y = (d["cst"][g][:, pp * LANES:(pp + 1) * LANES] * eax[:, pp * LANES:(pp + 1) * LANES]
                     + dx_ref[:, ls] * d["xss"][p] + d["yds"][p])
                ys.append(y * d["p"][rows, z0 + p * LANES:z0 + (p + 1) * LANES])
            _norm_gate_store(d["y"], rows, g * group_w, ys, nw_ref, [None] * len(ys))

    return products, decay_sums, decays, chunk_products, finish


def _mixers_kernel(pa_ref, pb_ref, pc_ref, pd_ref, sm_ref, nwa_ref, nwb_ref, nwc_ref, nwd_ref,
                   ms_ref, mk_ref, tri_ref, mexp_ref, msel_ref, alog_ref, dx_ref, dsel_ref, dexp_ref,
                   ya_ref, yb_ref, yc_ref, yd_ref, sta_s, stc_s, cb_s, mb_s, std_s, *, tile):
    @pl.when(pl.program_id(0) == 0)
    def _():
        for r in (sta_s, stc_s, cb_s, mb_s, std_s):
            r[...] = jnp.zeros_like(r)

    gla = _gla_stages(ms_ref, mk_ref)
    mlstm = _mlstm_stages(nwb_ref, tri_ref, mexp_ref, msel_ref)
    ssd = _ssd_stages(alog_ref, dx_ref, nwd_ref, tri_ref, dsel_ref, dexp_ref)
    work = []
    for s in range(pa_ref.shape[0]):
        work.append((_gla_stream(pa_ref, s, GLA_QK, GLA_DK, ya_ref, sta_s, nwa_ref), gla))
        work.append((dict(p=pb_ref.at[s], g=sm_ref.at[s], c=cb_s.at[s], m=mb_s.at[s], y=yb_ref.at[s]),
                     mlstm))
        work.append((_gla_stream(pc_ref, s, HGRN_QF, HGRN_DK, yc_ref, stc_s, nwc_ref), gla))
        work.append((dict(p=pd_ref.at[s], dt=sm_ref.at[s], st=std_s.at[s], y=yd_ref.at[s]), ssd))
    _run_chunks(work, tile)


def _mixers(tok_inputs, const_inputs, n_batch, seq, tile):
    tok = lambda t: (0, t, 0)
    in_specs = [pl.BlockSpec((n_batch, tile, a.shape[2]), tok) for a in tok_inputs]
    for a in const_inputs:
        in_specs.append(pl.BlockSpec(a.shape, lambda t, nd=a.ndim: (0,) * nd))
    scratch = [pltpu.VMEM((n_batch, GLA_QK // LANES, GLA_DV, LANES), F32),
               pltpu.VMEM((n_batch, HGRN_QF // LANES, HGRN_DV, LANES), F32),
               pltpu.VMEM((n_batch, MLSTM_HEADS, MLSTM_DH, 2 * MLSTM_DH), F32),
               pltpu.VMEM((n_batch, 8, MLSTM_HEADS * LANES), F32),
               pltpu.VMEM((n_batch, SSD_GROUPS, SSD_STATE, D_BRANCH // SSD_GROUPS), F32)]
    return pl.pallas_call(
        functools.partial(_mixers_kernel, tile=tile), grid=(seq // tile,), in_specs=in_specs,
        out_specs=[pl.BlockSpec((n_batch, tile, D_BRANCH), tok)] * 4,
        out_shape=[jax.ShapeDtypeStruct((n_batch, seq, D_BRANCH), BF16)] * 4,
        scratch_shapes=scratch,
        compiler_params=pltpu.CompilerParams(dimension_semantics=("arbitrary",),
                                             vmem_limit_bytes=VMEM_LIMIT),
        name="mixers",
    )(*tok_inputs, *const_inputs)


def _pad_lanes(parts, total=LANES):
    width = sum(p.shape[-1] for p in parts)
    lead = parts[0].shape[:-1]
    return jnp.concatenate(list(parts) + [jnp.zeros(lead + (total - width,), parts[0].dtype)], axis=-1)


def _small_vector(i_part, f_part, dt_part):
    z = jnp.zeros((GLA_GATE_RANK,), F32)
    return _pad_lanes([z, i_part.astype(F32), f_part.astype(F32), dt_part.astype(F32)])[None, :]


def kernel(x, norm_w, w_in, gla_gate_w, gla_gate_b, gla_norm_w, ml_conv_w, ml_conv_b, ml_i_b, ml_f_b,
           ml_norm_w, hg_lb_logits, hg_norm_w, ssd_conv_w, ssd_conv_b, ssd_dt_bias, ssd_A_log, ssd_D,
           ssd_norm_w, w_out, final_norm_w):
    n_batch, seq, _ = x.shape
    depth = w_in.shape[0]
    tile = min(MIX_TILE, seq)
    n_tok = n_batch * seq

    mstack, masks = _gla_constants()
    tri_c = _tri(CHUNK)
    ml_exp, ml_sel = _mlstm_constants()
    ssd_exp, ssd_sel = _ssd_constants()
    zero4 = jnp.zeros((MLSTM_HEADS,), F32)
    in_tile = min(IN_TILE, seq)

    p = jax.nn.softmax(hg_lb_logits.astype(F32), axis=0)
    lower_bounds = jnp.cumsum(p, axis=0) - p[0:1]

    w_wide, w_small = _realign(jnp.swapaxes(w_in, 1, 2), REALIGN_ROWS)

    h = x.reshape(n_tok, D_MODEL)
    row2 = lambda v: v.astype(F32).reshape(1, -1)
    for l in range(depth):
        gate_w = jnp.concatenate(
            [gla_gate_w[l].astype(F32), jnp.zeros((LANES - GLA_GATE_RANK, GLA_QK), F32)], axis=0)
        consts = [gate_w, row2(gla_gate_b[l]), row2(lower_bounds[l]),
                  ml_conv_w[l].astype(F32), row2(ml_conv_b[l]),
                  ssd_conv_w[l].astype(F32), row2(ssd_conv_b[l]),
                  _small_vector(ml_i_b[l], ml_f_b[l], ssd_dt_bias[l])]
        pa, pb, pc, pd, sm = [p.reshape(n_batch, seq, -1)
                              for p in _inproj(h, row2(norm_w[l]), w_wide, w_small, l, consts, in_tile,
                                               seq // in_tile)]

        ys = _mixers(
            [pa, pb, pc, pd, sm],
            [row2(gla_norm_w[l]), row2(ml_norm_w[l]), row2(hg_norm_w[l]), row2(ssd_norm_w[l]),
             mstack, masks, tri_c, ml_exp, ml_sel, _small_vector(zero4, zero4, ssd_A_log[l]),
             row2(jnp.repeat(ssd_D[l].astype(F32), SSD_HEAD_DIM)), ssd_sel, ssd_exp],
            n_batch, seq, tile)
        ys = [y.reshape(n_tok, D_BRANCH) for y in ys]
        h = _outproj(h, ys, w_out[l].astype(BF16), row2(final_norm_w), l == depth - 1,
                     min(OUT_TILE, n_tok))
    return h.reshape(n_batch, seq, D_MODEL)
```

```python
import functools

import numpy as np
import jax
import jax.numpy as jnp
from jax import lax
from jax.experimental import pallas as pl
from jax.experimental.pallas import tpu as pltpu

F32 = jnp.float32
BF16 = jnp.bfloat16

D_MODEL = 1024
D_BRANCH = 512
EPS = 1e-6
NEG_BIG = -1e30

GLA_HEADS, GLA_DK, GLA_DV = 4, 64, 128
GLA_GATE_RANK, GLA_GATE_NORM = 16, 16.0
MLSTM_HEADS, MLSTM_DH, MLSTM_CONV = 4, 128, 4
HGRN_HEADS, HGRN_DK, HGRN_DV = 4, 128, 128
SSD_HEAD_DIM, SSD_HEADS, SSD_GROUPS, SSD_STATE, SSD_CONV = 64, 8, 2, 128, 4
GLA_QK = GLA_HEADS * GLA_DK
HGRN_QF = HGRN_HEADS * HGRN_DK
SSD_BC = SSD_GROUPS * SSD_STATE
PROJ_SIZES = (
    GLA_QK, GLA_QK, D_BRANCH, GLA_GATE_RANK, D_BRANCH,
    D_BRANCH, D_BRANCH, D_BRANCH, MLSTM_HEADS, MLSTM_HEADS, D_BRANCH, D_BRANCH,
    HGRN_QF, HGRN_QF, D_BRANCH, D_BRANCH,
    D_BRANCH, SSD_BC, SSD_BC, SSD_HEADS, D_BRANCH,
)

LANES = 128
HIST = 8
VMEM_LIMIT = 56 * 1024 * 1024

REALIGN_ROWS = 256
IN_TILE = 256
MIX_TILE = 256
OUT_TILE = 512

CHUNK = 64
N_LEVELS = 6
MIN_VPU_LEVEL = 4

SM_GR, SM_I, SM_F, SM_DT = 0, 16, 20, 24

W_A = 2 * GLA_QK + 2 * D_BRANCH
W_B = 5 * D_BRANCH
W_C = 2 * HGRN_QF + 2 * D_BRANCH
W_D = 2 * D_BRANCH + 2 * SSD_BC
COL_A = 0
COL_B = COL_A + W_A
COL_C = COL_B + W_B
COL_D = COL_C + W_C
COL_S = COL_D + W_D


def _gla_constants():
    c = CHUNK
    t = np.arange(c)[:, None]
    d = np.arange(c)[None, :]
    blocks = [(d <= t), (d > t)]
    masks = [np.eye(c, dtype=bool)]
    for l in range(N_LEVELS):
        s = c >> (l + 1)
        mid_t = (t // (2 * s)) * (2 * s) + s
        upper = t >= mid_t
        if s < MIN_VPU_LEVEL:
            blocks.append(np.where(upper, (d >= mid_t) & (d <= t), (d > t) & (d <= mid_t - 1)))
        same = (t // (2 * s)) == (d // (2 * s))
        masks.append(same & upper & (d < mid_t))
    mstack = np.concatenate(blocks, axis=0).astype(np.float32)
    mstack = np.concatenate([mstack, mstack], axis=1)
    masks = np.stack(masks).astype(np.float32)
    return jnp.asarray(mstack, BF16), jnp.asarray(np.concatenate([masks, masks], axis=-1))


def _tri(n):
    return jnp.asarray(np.tril(np.ones((n, n), np.float32)), BF16)


def _dot(a, b):
    return jnp.dot(a, b, preferred_element_type=F32)


def _dot_nt(a, b):
    return lax.dot_general(a, b, (((1,), (1,)), ((), ())), preferred_element_type=F32)


def _dot_tn(a, b):
    return lax.dot_general(a, b, (((0,), (0,)), ((), ())), preferred_element_type=F32)


def _split3(x):
    hi = x.astype(BF16)
    r1 = x - hi.astype(F32)
    mid = r1.astype(BF16)
    lo = (r1 - mid.astype(F32)).astype(BF16)
    return hi, mid, lo


def _sel_dot(sel, x):
    hi, mid, lo = _split3(x)
    return _dot(sel, hi) + _dot(sel, mid) + _dot(sel, lo)


def _sel_dot_nt(sel, x):
    hi, mid, lo = _split3(x)
    return _dot_nt(sel, hi) + _dot_nt(sel, mid) + _dot_nt(sel, lo)


def _sel_dot2(sel2, x):
    hi = x.astype(BF16)
    mid = (x - hi.astype(F32)).astype(BF16)
    return _dot(sel2, jnp.concatenate([hi, mid], axis=0))


def _level_exponent(gcs, s):
    pieces = []
    for b in range(0, CHUNK, 2 * s):
        ref_row = gcs[b + s - 1:b + s, :]
        if s >= 8:
            pieces += [ref_row - gcs[b:b + s], gcs[b + s:b + 2 * s] - ref_row]
        else:
            diff = gcs[b:b + 2 * s] - ref_row
            pieces.append(jnp.minimum(diff, -diff))
    return jnp.concatenate(pieces, axis=0)


def _dot_sel2(x, sel2):
    hi = x.astype(BF16)
    mid = (x - hi.astype(F32)).astype(BF16)
    return _dot(jnp.concatenate([hi, mid], axis=1), sel2)


def _dot_f32(a, b):
    ah = a.astype(BF16)
    al = (a - ah.astype(F32)).astype(BF16)
    bh = b.astype(BF16)
    bl = (b - bh.astype(F32)).astype(BF16)
    return _dot(ah, bh) + _dot(al, bh) + _dot(ah, bl)


def _softplus(x):
    return jnp.maximum(x, 0.0) + jnp.log1p(jnp.exp(-jnp.abs(x)))


def _log_sigmoid(x):
    return -_softplus(-x)


def _silu(x):
    return x * jax.nn.sigmoid(x)


def _lane_iota():
    return lax.broadcasted_iota(jnp.int32, (1, LANES), 1)


def _side_by_side(a, b, zero):
    return jnp.concatenate([jnp.concatenate([a, zero], axis=1),
                            jnp.concatenate([zero, b], axis=1)], axis=0)


def _wide_segments():
    segs, src, dst = [], 0, 0
    run_start = None
    for size in PROJ_SIZES + (0,):
        wide = size >= LANES
        if wide and run_start is None:
            run_start = src
        if not wide and run_start is not None:
            segs.append((run_start, dst, src - run_start))
            dst += src - run_start
            run_start = None
        src += size
    return segs


def _realign_kernel(wt_ref, o_ref, os_ref):
    for src, dst, n in _wide_segments():
        for r in range(0, n, LANES):
            o_ref[:, dst + r:dst + r + LANES] = wt_ref[src + r:src + r + LANES, :].T.astype(BF16)
    narrow, src = [], 0
    for size in PROJ_SIZES:
        if size < LANES:
            narrow.append(wt_ref[src:src + size, :])
        src += size
    used = sum(p.shape[0] for p in narrow)
    narrow.append(jnp.zeros((LANES - used, wt_ref.shape[1]), F32))
    os_ref[...] = jnp.concatenate(narrow, axis=0).T


def _realign(wt, rows):
    depth, n_proj, d_model = wt.shape
    return pl.pallas_call(
        _realign_kernel, grid=(depth, d_model // rows),
        in_specs=[pl.BlockSpec((None, n_proj, rows), lambda l, j: (l, 0, j))],
        out_specs=[pl.BlockSpec((None, rows, COL_S), lambda l, j: (l, j, 0)),
                   pl.BlockSpec((None, rows, LANES), lambda l, j: (l, j, 0))],
        out_shape=[jax.ShapeDtypeStruct((depth, d_model, COL_S), BF16),
                   jax.ShapeDtypeStruct((depth, d_model, LANES), F32)],
        compiler_params=pltpu.CompilerParams(dimension_semantics=("arbitrary", "arbitrary"),
                                             vmem_limit_bytes=VMEM_LIMIT),
        name="realign",
    )(wt)


CONV_ROWS, CONV_LANES = 256, 128


def _causal_conv_silu(raw, xe_ref, hist_ref, w_ref, b_ref, taps, out_ref, scale=None):
    rows, n = raw.shape
    xe_ref[0:HIST, :] = hist_ref[...]
    xe_ref[HIST:HIST + rows, :] = raw
    hist_ref[...] = raw[rows - HIST:rows, :]
    for r in range(0, rows, CONV_ROWS):
        for l in range(0, n, CONV_LANES):
            cs = slice(l, l + CONV_LANES)
            acc = b_ref[:, cs] + w_ref[taps - 1:taps, cs] * xe_ref[HIST + r:HIST + r + CONV_ROWS, cs]
            for k in range(taps - 1):
                off = HIST - (taps - 1) + k + r
                acc = acc + w_ref[k:k + 1, cs] * xe_ref[off:off + CONV_ROWS, cs]
            out = _silu(acc)
            out_ref[r:r + CONV_ROWS, cs] = out if scale is None else out * scale[:, cs]


def _inproj_kernel(x_ref, xn_ref, nw_ref, w_ref, ws_ref, gw_ref, gb_ref, lb_ref, cwb_ref, cbb_ref, cwd_ref,
                   cbd_ref, gbias_ref, oa_ref, ob_ref, oc_ref, od_ref, os_ref, xe_s, hist_b, hist_d, u_s,
                   *, tiles_per_seq):
    step = pl.program_id(0)

    @pl.when(step % tiles_per_seq == 0)
    def _():
        hist_b[...] = jnp.zeros_like(hist_b)
        hist_d[...] = jnp.zeros_like(hist_d)

    def normed(x):
        ms = jnp.mean(x * x, axis=-1, keepdims=True)
        return (x * lax.rsqrt(ms + EPS) * nw_ref[...]).astype(BF16)

    @pl.when(step == 0)
    def _():
        u_s[0] = normed(x_ref[...])

    u = u_s[step % 2]
    lane = _lane_iota()

    small = _dot(u, ws_ref[...].astype(BF16))
    biased = small + gbias_ref[...]
    is_f = (lane >= SM_F) & (lane < SM_F + MLSTM_HEADS)
    is_dt = (lane >= SM_DT) & (lane < SM_DT + SSD_HEADS)
    os_ref[...] = jnp.where(is_f, _log_sigmoid(biased), jnp.where(is_dt, _softplus(biased), biased))

    raw = _dot(u, w_ref[:, COL_B:COL_B + W_B])
    conv_w = 2 * D_BRANCH
    k_scale = jnp.where(lax.broadcasted_iota(jnp.int32, (1, conv_w), 1) < D_BRANCH, 1.0, MLSTM_DH ** -0.5)
    _causal_conv_silu(raw[:, 0:conv_w], xe_s, hist_b, cwb_ref, cbb_ref, MLSTM_CONV, ob_ref, k_scale)
    o0, z0 = 3 * D_BRANCH, 4 * D_BRANCH
    ob_ref[:, conv_w:o0] = raw[:, conv_w:o0]
    ob_ref[:, o0:z0] = jax.nn.sigmoid(raw[:, o0:z0])
    ob_ref[:, z0:W_B] = _silu(raw[:, z0:W_B])

    raw = _dot(u, w_ref[:, COL_D:COL_D + W_D])
    conv_w = D_BRANCH + 2 * SSD_BC
    _causal_conv_silu(raw[:, 0:conv_w], xe_s, hist_d, cwd_ref, cbd_ref, SSD_CONV, od_ref)
    od_ref[:, conv_w:W_D] = _silu(raw[:, conv_w:W_D])
    u_s[(step + 1) % 2] = normed(xn_ref[...])

    raw = _dot(u, w_ref[:, COL_C:COL_C + W_C])
    lb = lb_ref[...]
    fr = raw[:, HGRN_QF:2 * HGRN_QF]
    oc_ref[:, 0:HGRN_QF] = raw[:, 0:HGRN_QF] * (HGRN_DK ** -0.5)
    oc_ref[:, HGRN_QF:2 * HGRN_QF] = (1.0 - lb) * jax.nn.sigmoid(-fr)
    z0 = 2 * HGRN_QF + D_BRANCH
    oc_ref[:, 2 * HGRN_QF:z0] = raw[:, 2 * HGRN_QF:z0]
    oc_ref[:, z0:W_C] = _silu(raw[:, z0:W_C])
    oc_ref[:, W_C:W_C + HGRN_QF] = jnp.log(jnp.maximum(lb + (1.0 - lb) * jax.nn.sigmoid(fr), 1e-30))

    gate = _dot_f32(small, gw_ref[...]) + gb_ref[...]
    oa_ref[:, W_A:W_A + GLA_QK] = _log_sigmoid(gate) * (1.0 / GLA_GATE_NORM)
    raw = _dot(u, w_ref[:, COL_A:COL_A + W_A])
    oa_ref[:, 0:GLA_QK] = raw[:, 0:GLA_QK] * (GLA_DK ** -0.5)
    z0 = 2 * GLA_QK + D_BRANCH
    oa_ref[:, GLA_QK:z0] = raw[:, GLA_QK:z0]
    oa_ref[:, z0:W_A] = _silu(raw[:, z0:W_A])


def _inproj(h, norm_w, w_wide, w_small, layer, consts, tile, tiles_per_seq):
    n_tok = h.shape[0]
    const = lambda i: (0, 0)
    row = lambda i: (i, 0)
    n_tiles = n_tok // tile
    in_specs = [pl.BlockSpec((tile, D_MODEL), row),
                pl.BlockSpec((tile, D_MODEL), lambda i: (jnp.minimum(i + 1, n_tiles - 1), 0)),
                pl.BlockSpec((1, D_MODEL), const),
                pl.BlockSpec((None, D_MODEL, COL_S), lambda i: (layer, 0, 0), pipeline_mode=pl.Buffered(1)),
                pl.BlockSpec((None, D_MODEL, LANES), lambda i: (layer, 0, 0))]
    in_specs += [pl.BlockSpec(c.shape, const) for c in consts]
    widths = (W_A + GLA_QK, W_B, W_C + HGRN_QF, W_D, LANES)
    conv_w = 2 * D_BRANCH
    return pl.pallas_call(
        functools.partial(_inproj_kernel, tiles_per_seq=tiles_per_seq),
        grid=(n_tiles,), in_specs=in_specs,
        out_specs=[pl.BlockSpec((tile, w), row) for w in widths],
        out_shape=[jax.ShapeDtypeStruct((n_tok, w), F32) for w in widths],
        scratch_shapes=[pltpu.VMEM((tile + HIST, conv_w), F32), pltpu.VMEM((HIST, conv_w), F32),
                        pltpu.VMEM((HIST, conv_w), F32), pltpu.VMEM((2, tile, D_MODEL), BF16)],
        compiler_params=pltpu.CompilerParams(dimension_semantics=("arbitrary",),
                                             vmem_limit_bytes=VMEM_LIMIT),
        name="inproj",
    )(h, h, norm_w, w_wide, w_small, *consts)


def _outproj_kernel(h_ref, ya_ref, yb_ref, yc_ref, yd_ref, w_ref, fw_ref, o_ref, *, final):
    acc = h_ref[...]
    for i, y_ref in enumerate((ya_ref, yb_ref, yc_ref, yd_ref)):
        acc = acc + _dot(y_ref[...], w_ref[i * D_BRANCH:(i + 1) * D_BRANCH, :])
    if final:
        ms = jnp.mean(acc * acc, axis=-1, keepdims=True)
        acc = acc * lax.rsqrt(ms + EPS) * fw_ref[...]
    o_ref[...] = acc


def _outproj(h, ys, w_out, final_w, final, tile):
    n_tok = h.shape[0]
    const = lambda i: (0, 0)
    row = lambda i: (i, 0)
    in_specs = [pl.BlockSpec((tile, D_MODEL), row)]
    in_specs += [pl.BlockSpec((tile, D_BRANCH), row) for _ in ys]
    in_specs += [pl.BlockSpec(w_out.shape, const), pl.BlockSpec((1, D_MODEL), const)]
    return pl.pallas_call(
        functools.partial(_outproj_kernel, final=final),
        grid=(n_tok // tile,), in_specs=in_specs,
        out_specs=pl.BlockSpec((tile, D_MODEL), row),
        out_shape=jax.ShapeDtypeStruct((n_tok, D_MODEL), F32),
        compiler_params=pltpu.CompilerParams(dimension_semantics=("arbitrary",),
                                             vmem_limit_bytes=VMEM_LIMIT),
        name="outproj",
    )(h, *ys, w_out, final_w)


def _norm_gate_store(y_ref, rows, col0, parts, nw_ref, z_parts):
    width = sum(p.shape[-1] for p in parts)
    ss = sum(jnp.sum(p * p, axis=-1, keepdims=True) for p in parts)
    scale = lax.rsqrt(ss * (1.0 / width) + EPS)
    c = col0
    for p, z in zip(parts, z_parts):
        w = p.shape[-1]
        out = p * scale * nw_ref[:, c:c + w]
        if z is not None:
            out = out * z
        y_ref[rows, c:c + w] = out.astype(y_ref.dtype)
        c += w


N_STAGES = 5
CHUNK_UNROLL = 1
STAGE_SKEW = (0,)


def _run_chunks(work, tile):
    def body(c, carry):
        rows = pl.ds(pl.multiple_of(c * CHUNK, CHUNK), CHUNK)
        live = [(dict(d), stages, STAGE_SKEW[i % len(STAGE_SKEW)]) for i, (d, stages) in enumerate(work)]
        for slot in range(N_STAGES + max(STAGE_SKEW)):
            for d, stages, skew in live:
                if 0 <= slot - skew < N_STAGES:
                    stages[slot - skew](d, rows)
        return carry

    lax.fori_loop(0, tile // CHUNK, body, 0, unroll=CHUNK_UNROLL)


def _gla_stages(ms_ref, mk_ref):
    lane = _lane_iota()

    def lane_mask(d, j):
        dk = d["dk"]
        return None if dk == LANES else (lane >= j * dk) & (lane < (j + 1) * dk)

    def pick(a, lm):
        return a if lm is None else jnp.where(lm, a, jnp.zeros_like(a))

    def n_groups(d):
        return d["q"].shape[1] // LANES

    def cumsums(d, rows):
        d["es"] = [_sel_dot2(ms_ref[...], d["lg"][rows, p * 2 * LANES:(p + 1) * 2 * LANES])
                   for p in range(n_groups(d) // 2)]

    def operands(d, rows):
        ops = []
        for g in range(n_groups(d)):
            ls = slice(g * LANES, (g + 1) * LANES)
            e = d["es"][g // 2][:, (g % 2) * LANES:(g % 2 + 1) * LANES]
            q = d["q"][rows, ls]
            k = d["k"][rows, ls]
            gcs = e[0:CHUNK]
            qg = (q * jnp.exp(gcs)).astype(BF16)
            kd = (k * jnp.exp(e[CHUNK:2 * CHUNK])).astype(BF16)
            dec = jnp.exp(gcs[CHUNK - 1:CHUNK, :])
            qb = q.astype(BF16)
            kb = k.astype(BF16)
            ql, kl = [qb], [kb]
            mxu_block = 2
            for l in range(N_LEVELS):
                s = CHUNK >> (l + 1)
                if s >= MIN_VPU_LEVEL:
                    el = _level_exponent(gcs, s)
                else:
                    el = e[mxu_block * CHUNK:(mxu_block + 1) * CHUNK]
                    mxu_block += 1
                w = jnp.exp(el).astype(BF16)
                ql.append(qb * w)
                kl.append(kb * w)
            ops.append((qg, kd, dec, ql, kl))
        d["ops"] = ops

    def scores(d, rows):
        heads = []
        for g in range(n_groups(d)):
            ql, kl = d["ops"][g][3], d["ops"][g][4]
            for j in range(LANES // d["dk"]):
                lm = lane_mask(d, j)
                a = _dot_nt(pick(ql[0], lm), kl[0]) * mk_ref[0][:, 0:CHUNK]
                for l in range(1, N_LEVELS + 1):
                    a = a + _dot_nt(pick(ql[l], lm), kl[l]) * mk_ref[l][:, 0:CHUNK]
                heads.append((g, j, a.astype(BF16)))
        d["heads"] = heads

    def outputs(d, rows):
        hp = LANES // d["dk"]
        outs, upds = [], [None] * n_groups(d)
        sts = [d["st"][g] for g in range(n_groups(d))]
        stbs = [st.astype(BF16) for st in sts]
        for g, j, ab in d["heads"]:
            h = g * hp + j
            lm = lane_mask(d, j)
            qg, kd = d["ops"][g][0], d["ops"][g][1]
            vh = d["v"][rows, h * LANES:(h + 1) * LANES].astype(BF16)
            outs.append(_dot(ab, vh) + _dot_nt(pick(qg, lm), stbs[g]))
            u = _dot_tn(vh, kd)
            upds[g] = u if upds[g] is None else jnp.where(lm, u, upds[g])
        d["outs"], d["upds"], d["sts"] = outs, upds, sts

    def finish(d, rows):
        for g in range(n_groups(d)):
            d["st"][g] = d["sts"][g] * d["ops"][g][2] + d["upds"][g]
        for h, o in enumerate(d["outs"]):
            _norm_gate_store(d["y"], rows, h * LANES, [o], d["nw"],
                             [d["z"][rows, h * LANES:(h + 1) * LANES]])

    return cumsums, operands, scores, outputs, finish


def _gla_stream(p_ref, s, qk, dk, y_ref, st_s, nw_ref):
    v0, z0, lg0 = 2 * qk, 2 * qk + D_BRANCH, 2 * qk + 2 * D_BRANCH
    return dict(q=p_ref.at[s, :, 0:qk], k=p_ref.at[s, :, qk:v0], v=p_ref.at[s, :, v0:z0],
                z=p_ref.at[s, :, z0:lg0], lg=p_ref.at[s, :, lg0:lg0 + qk],
                y=y_ref.at[s], st=st_s.at[s], nw=nw_ref, dk=dk)


def _mlstm_constants():
    half = LANES // 2
    full0 = half * MLSTM_HEADS
    e = np.zeros((LANES, full0 + 2 * LANES * MLSTM_HEADS), np.float32)
    diff0 = full0 + LANES * MLSTM_HEADS
    sel = np.zeros((16, LANES), np.float32)
    for h in range(MLSTM_HEADS):
        e[SM_F + h, h * half:(h + 1) * half] = 1.0
        e[SM_F + h, full0 + h * LANES:full0 + (h + 1) * LANES] = 1.0
        e[SM_I + h, diff0 + h * LANES:diff0 + (h + 1) * LANES] = 1.0
        e[SM_F + h, diff0 + h * LANES:diff0 + (h + 1) * LANES] = -1.0
        sel[h // 2, SM_I + h] = 1.0
        sel[h // 2, SM_F + h] = -1.0
    return jnp.asarray(np.concatenate([e, e], axis=0), BF16), jnp.asarray(sel, BF16)


def _mlstm_stages(nw_ref, tri_ref, exp_ref, sel_ref):
    lane = _lane_iota()
    is_f = (lane >= SM_F) & (lane < SM_F + MLSTM_HEADS)

    half = LANES // 2
    ri = lax.broadcasted_iota(jnp.int32, (CHUNK, LANES), 0)
    ci = lax.broadcasted_iota(jnp.int32, (CHUNK, LANES), 1)
    causal2 = (ci & (half - 1)) <= ri
    lo_half = lane < half
    even = (lane & 1) == 0
    v0, o0, z0 = 2 * D_BRANCH, 3 * D_BRANCH, 4 * D_BRANCH
    heads = range(MLSTM_HEADS)
    pairs = range(MLSTM_HEADS // 2)
    full0 = half * MLSTM_HEADS
    diff0 = full0 + LANES * MLSTM_HEADS

    def qk_products(d, rows):
        qk_ref = d["p"]
        qbs = [qk_ref[rows, h * LANES:(h + 1) * LANES].astype(BF16) for h in heads]
        d["ks"] = [qk_ref[rows, D_BRANCH + h * LANES:D_BRANCH + (h + 1) * LANES] for h in heads]
        kbs = [k.astype(BF16) for k in d["ks"]]
        zero = jnp.zeros((CHUNK, LANES), BF16)
        ones = jnp.ones((CHUNK, LANES), BF16)
        d["qk_raw"] = [_dot_nt(jnp.concatenate([qbs[2 * p], qbs[2 * p + 1]], axis=1),
                               _side_by_side(kbs[2 * p], kbs[2 * p + 1], zero)) for p in pairs]
        d["qc"] = [_dot(qbs[h], d["c"][h].astype(BF16)) for h in heads]
        d["vaug"] = [jnp.concatenate([d["p"][rows, v0 + h * LANES:v0 + (h + 1) * LANES].astype(BF16),
                                      ones], axis=1) for h in heads]

    def gate_sums(d, rows):
        gc = d["g"][rows, :]
        bcol = _sel_dot(tri_ref[...], gc)
        ib = jnp.where(is_f, bcol, gc)
        d["ex"] = _dot_sel2(ib, exp_ref[...])
        by_parity = jnp.concatenate([jnp.where(even, ib, 0.0), jnp.where(even, 0.0, ib)], axis=0)
        d["drows"] = _sel_dot_nt(sel_ref[...], by_parity)

    def weights(d, rows):
        ex = d["ex"]
        mx = d["m"][0:1, :]
        ss, mrs = [], []
        for p in pairs:
            bx = ex[:, p * LANES:(p + 1) * LANES]
            lw = jnp.where(causal2, bx + d["drows"][p:p + 1, :], NEG_BIG)
            mr0 = jnp.max(jnp.where(lo_half, lw, NEG_BIG), axis=-1, keepdims=True)
            mr1 = jnp.max(jnp.where(lo_half, NEG_BIG, lw), axis=-1, keepdims=True)
            m64 = jnp.where(lo_half, mx[:, 2 * p * LANES:(2 * p + 1) * LANES],
                            mx[:, (2 * p + 1) * LANES:(2 * p + 2) * LANES])
            m_row = jnp.maximum(jnp.where(lo_half, mr0, mr1), bx + m64)
            ss.append((d["qk_raw"][p] * jnp.exp(lw - m_row)).astype(BF16))
            mrs += [mr0, mr1]
        d["ss"], d["mrs"] = ss, mrs
        b_last = ex[CHUNK - 1:CHUNK, full0:diff0]
        lwe = ex[:, diff0:] + b_last
        m_new = jnp.maximum(b_last + mx, jnp.max(lwe, axis=0, keepdims=True))
        d["cd"] = jnp.exp(b_last + mx - m_new)
        d["m_new"] = m_new
        kw = jnp.exp(lwe - m_new)
        d["kws"] = [(d["ks"][h] * kw[:, h * LANES:(h + 1) * LANES]).astype(BF16) for h in heads]

    def numerators(d, rows):
        zero = jnp.zeros((CHUNK, 2 * LANES), BF16)
        d["nums"] = [_dot(d["ss"][p], _side_by_side(d["vaug"][2 * p], d["vaug"][2 * p + 1], zero))
                     for p in pairs]
        d["cups"] = [_dot_tn(d["kws"][h], d["vaug"][h]) for h in heads]

    def finish(d, rows):
        ex = d["ex"]
        mx = d["m"][0:1, :]
        for h in heads:
            p, hd = divmod(h, 2)
            hs = slice(h * LANES, (h + 1) * LANES)
            m_inter = ex[:, full0 + h * LANES:full0 + (h + 1) * LANES] + mx[:, hs]
            m_row = jnp.maximum(d["mrs"][h], m_inter)
            inter = jnp.exp(m_inter - m_row)
            sv = d["nums"][p][:, hd * 2 * LANES:(hd + 1) * 2 * LANES]
            num = sv[:, 0:LANES] + inter * d["qc"][h][:, 0:LANES]
            den = sv[:, LANES:] + inter * d["qc"][h][:, LANES:]
            hh = num / jnp.maximum(jnp.abs(den), jnp.exp(-m_row))
            cd = d["cd"][:, hs]
            d["c"][h] = jnp.concatenate([cd, cd], axis=1) * d["c"][h] + d["cups"][h]
            og = d["p"][rows, o0 + h * LANES:o0 + (h + 1) * LANES]
            _norm_gate_store(d["y"], rows, h * LANES, [og * hh], nw_ref,
                             [d["p"][rows, z0 + h * LANES:z0 + (h + 1) * LANES]])
        d["m"][0:1, :] = d["m_new"]

    return qk_products, gate_sums, weights, numerators, finish


def _ssd_constants():
    e = np.zeros((LANES, SSD_HEADS * SSD_HEAD_DIM), np.float32)
    sel = np.zeros((16, LANES), np.float32)
    for h in range(SSD_HEADS):
        e[SM_DT + h, h * SSD_HEAD_DIM:(h + 1) * SSD_HEAD_DIM] = 1.0
        sel[h // 2, SM_DT + h] = 1.0
    return jnp.asarray(np.concatenate([e, e], axis=0), BF16), jnp.asarray(sel, BF16)


def _ssd_stages(alog_ref, dx_ref, nw_ref, tri_ref, sel_ref, exp_ref):
    lane = _lane_iota()
    is_dt = (lane >= SM_DT) & (lane < SM_DT + SSD_HEADS)

    half = LANES // 2
    ri = lax.broadcasted_iota(jnp.int32, (CHUNK, LANES), 0)
    ci = lax.broadcasted_iota(jnp.int32, (CHUNK, LANES), 1)
    causal2 = (ci & (half - 1)) <= ri
    lo_half = lane < half
    even = (lane & 1) == 0
    a_lane = jnp.where(is_dt, -jnp.exp(alog_ref[...]), 0.0)
    b0, c0, z0 = D_BRANCH, D_BRANCH + SSD_BC, D_BRANCH + 2 * SSD_BC
    groups = range(SSD_GROUPS)
    group_w = D_BRANCH // SSD_GROUPS
    pairs = range(D_BRANCH // LANES)
    pairs_per_group = group_w // LANES

    def products(d, rows):
        xbc = d["p"]
        d["bgs"] = [xbc[rows, b0 + g * SSD_STATE:b0 + (g + 1) * SSD_STATE].astype(BF16) for g in groups]
        cgbs = [xbc[rows, c0 + g * SSD_STATE:c0 + (g + 1) * SSD_STATE].astype(BF16) for g in groups]
        d["cb2"] = [_dot_nt(cgbs[g], jnp.concatenate([d["bgs"][g], d["bgs"][g]], axis=0))
                    for g in groups]
        d["cst"] = [_dot(cgbs[g], d["st"][g].astype(BF16)) for g in groups]

    def decay_sums(d, rows):
        dt = d["dt"][rows, :]
        acs = _sel_dot(tri_ref[...], dt * a_lane)
        ex = _dot_sel2(jnp.concatenate([dt, acs], axis=0), exp_ref[...])
        d["dtx"], d["acs_x"] = ex[0:CHUNK], ex[CHUNK:]
        by_parity = jnp.concatenate([jnp.where(even, acs, 0.0), jnp.where(even, 0.0, acs)], axis=0)
        d["a_rows"] = _sel_dot_nt(sel_ref[...], by_parity)

    def decays(d, rows):
        d["xss"], d["ms"], d["xblk"], xdecs = [], [], [], []
        for p in pairs:
            ls = slice(p * LANES, (p + 1) * LANES)
            ax = d["acs_x"][:, ls]
            lmat = jnp.exp(jnp.where(causal2, ax - d["a_rows"][p:p + 1, :], NEG_BIG))
            d["ms"].append((d["cb2"][p // pairs_per_group] * lmat).astype(BF16))
            xs = d["p"][rows, ls]
            xdt = xs * d["dtx"][:, ls]
            d["xss"].append(xs)
            d["xblk"].append(jnp.concatenate([jnp.where(lo_half, xdt, 0.0), jnp.where(lo_half, 0.0, xdt)],
                                             axis=0).astype(BF16))
            xdecs.append((xdt * jnp.exp(ax[CHUNK - 1:CHUNK, :] - ax)).astype(BF16))
        d["xdec"] = [jnp.concatenate(xdecs[g * pairs_per_group:(g + 1) * pairs_per_group], axis=1)
                     for g in groups]

    def chunk_products(d, rows):
        d["yds"] = [_dot(d["ms"][p], d["xblk"][p]) for p in pairs]
        d["ups"] = [_dot_tn(d["bgs"][g], d["xdec"][g]) for g in groups]

    def finish(d, rows):
        for g in groups:
            gs = slice(g * group_w, (g + 1) * group_w)
            eax = jnp.exp(d["acs_x"][:, gs])
            d["st"][g] = d["st"][g] * eax[CHUNK - 1:CHUNK, :] + d["ups"][g]
            ys = []
            for pp in range(pairs_per_group):
                p = g * pairs_per_group + pp
                ls = slice(p * LANES, (p + 1) * LANES)
                y = (d["cst"][g][:, pp * LANES:(pp + 1) * LANES] * eax[:, pp * LANES:(pp + 1) * LANES]
                     + dx_ref[:, ls] * d["xss"][p] + d["yds"][p])
                ys.append(y * d["p"][rows, z0 + p * LANES:z0 + (p + 1) * LANES])
            _norm_gate_store(d["y"], rows, g * group_w, ys, nw_ref, [None] * len(ys))

    return products, decay_sums, decays, chunk_products, finish


def _mixers_kernel(pa_ref, pb_ref, pc_ref, pd_ref, sm_ref, nwa_ref, nwb_ref, nwc_ref, nwd_ref,
                   ms_ref, mk_ref, tri_ref, mexp_ref, msel_ref, alog_ref, dx_ref, dsel_ref, dexp_ref,
                   ya_ref, yb_ref, yc_ref, yd_ref, sta_s, stc_s, cb_s, mb_s, std_s, *, tile):
    @pl.when(pl.program_id(0) == 0)
    def _():
        for r in (sta_s, stc_s, cb_s, mb_s, std_s):
            r[...] = jnp.zeros_like(r)

    gla = _gla_stages(ms_ref, mk_ref)
    mlstm = _mlstm_stages(nwb_ref, tri_ref, mexp_ref, msel_ref)
    ssd = _ssd_stages(alog_ref, dx_ref, nwd_ref, tri_ref, dsel_ref, dexp_ref)
    work = []
    for s in range(pa_ref.shape[0]):
        work.append((_gla_stream(pa_ref, s, GLA_QK, GLA_DK, ya_ref, sta_s, nwa_ref), gla))
        work.append((dict(p=pb_ref.at[s], g=sm_ref.at[s], c=cb_s.at[s], m=mb_s.at[s], y=yb_ref.at[s]),
                     mlstm))
        work.append((_gla_stream(pc_ref, s, HGRN_QF, HGRN_DK, yc_ref, stc_s, nwc_ref), gla))
        work.append((dict(p=pd_ref.at[s], dt=sm_ref.at[s], st=std_s.at[s], y=yd_ref.at[s]), ssd))
    _run_chunks(work, tile)


def _mixers(tok_inputs, const_inputs, n_batch, seq, tile):
    tok = lambda t: (0, t, 0)
    in_specs = [pl.BlockSpec((n_batch, tile, a.shape[2]), tok) for a in tok_inputs]
    for a in const_inputs:
        in_specs.append(pl.BlockSpec(a.shape, lambda t, nd=a.ndim: (0,) * nd))
    scratch = [pltpu.VMEM((n_batch, GLA_QK // LANES, GLA_DV, LANES), F32),
               pltpu.VMEM((n_batch, HGRN_QF // LANES, HGRN_DV, LANES), F32),
               pltpu.VMEM((n_batch, MLSTM_HEADS, MLSTM_DH, 2 * MLSTM_DH), F32),
               pltpu.VMEM((n_batch, 8, MLSTM_HEADS * LANES), F32),
               pltpu.VMEM((n_batch, SSD_GROUPS, SSD_STATE, D_BRANCH // SSD_GROUPS), F32)]
    return pl.pallas_call(
        functools.partial(_mixers_kernel, tile=tile), grid=(seq // tile,), in_specs=in_specs,
        out_specs=[pl.BlockSpec((n_batch, tile, D_BRANCH), tok)] * 4,
        out_shape=[jax.ShapeDtypeStruct((n_batch, seq, D_BRANCH), BF16)] * 4,
        scratch_shapes=scratch,
        compiler_params=pltpu.CompilerParams(dimension_semantics=("arbitrary",),
                                             vmem_limit_bytes=VMEM_LIMIT),
        name="mixers",
    )(*tok_inputs, *const_inputs)


def _pad_lanes(parts, total=LANES):
    width = sum(p.shape[-1] for p in parts)
    lead = parts[0].shape[:-1]
    return jnp.concatenate(list(parts) + [jnp.zeros(lead + (total - width,), parts[0].dtype)], axis=-1)


def _small_vector(i_part, f_part, dt_part):
    z = jnp.zeros((GLA_GATE_RANK,), F32)
    return _pad_lanes([z, i_part.astype(F32), f_part.astype(F32), dt_part.astype(F32)])[None, :]


def kernel(x, norm_w, w_in, gla_gate_w, gla_gate_b, gla_norm_w, ml_conv_w, ml_conv_b, ml_i_b, ml_f_b,
           ml_norm_w, hg_lb_logits, hg_norm_w, ssd_conv_w, ssd_conv_b, ssd_dt_bias, ssd_A_log, ssd_D,
           ssd_norm_w, w_out, final_norm_w):
    n_batch, seq, _ = x.shape
    depth = w_in.shape[0]
    tile = min(MIX_TILE, seq)
    n_tok = n_batch * seq

    mstack, masks = _gla_constants()
    tri_c = _tri(CHUNK)
    ml_exp, ml_sel = _mlstm_constants()
    ssd_exp, ssd_sel = _ssd_constants()
    zero4 = jnp.zeros((MLSTM_HEADS,), F32)
    in_tile = min(IN_TILE, seq)

    p = jax.nn.softmax(hg_lb_logits.astype(F32), axis=0)
    lower_bounds = jnp.cumsum(p, axis=0) - p[0:1]

    w_wide, w_small = _realign(jnp.swapaxes(w_in, 1, 2), REALIGN_ROWS)

    h = x.reshape(n_tok, D_MODEL)
    row2 = lambda v: v.astype(F32).reshape(1, -1)
    for l in range(depth):
        gate_w = jnp.concatenate(
            [gla_gate_w[l].astype(F32), jnp.zeros((LANES - GLA_GATE_RANK, GLA_QK), F32)], axis=0)
        consts = [gate_w, row2(gla_gate_b[l]), row2(lower_bounds[l]),
                  ml_conv_w[l].astype(F32), row2(ml_conv_b[l]),
                  ssd_conv_w[l].astype(F32), row2(ssd_conv_b[l]),
                  _small_vector(ml_i_b[l], ml_f_b[l], ssd_dt_bias[l])]
        pa, pb, pc, pd, sm = [p.reshape(n_batch, seq, -1)
                              for p in _inproj(h, row2(norm_w[l]), w_wide, w_small, l, consts, in_tile,
                                               seq // in_tile)]

        ys = _mixers(
            [pa, pb, pc, pd, sm],
            [row2(gla_norm_w[l]), row2(ml_norm_w[l]), row2(hg_norm_w[l]), row2(ssd_norm_w[l]),
             mstack, masks, tri_c, ml_exp, ml_sel, _small_vector(zero4, zero4, ssd_A_log[l]),
             row2(jnp.repeat(ssd_D[l].astype(F32), SSD_HEAD_DIM)), ssd_sel, ssd_exp],
            n_batch, seq, tile)
        ys = [y.reshape(n_tok, D_BRANCH) for y in ys]
        h = _outproj(h, ys, w_out[l].astype(BF16), row2(final_norm_w), l == depth - 1,
                     min(OUT_TILE, n_tok))
    return h.reshape(n_batch, seq, D_MODEL)
```

```python
import functools

import numpy as np
import jax
import jax.numpy as jnp
from jax import lax
from jax.experimental import pallas as pl
from jax.experimental.pallas import tpu as pltpu

F32 = jnp.float32
BF16 = jnp.bfloat16

D_MODEL = 1024
D_BRANCH = 512
EPS = 1e-6
NEG_BIG = -1e30

GLA_HEADS, GLA_DK, GLA_DV = 4, 64, 128
GLA_GATE_RANK, GLA_GATE_NORM = 16, 16.0
MLSTM_HEADS, MLSTM_DH, MLSTM_CONV = 4, 128, 4
HGRN_HEADS, HGRN_DK, HGRN_DV = 4, 128, 128
SSD_HEAD_DIM, SSD_HEADS, SSD_GROUPS, SSD_STATE, SSD_CONV = 64, 8, 2, 128, 4
GLA_QK = GLA_HEADS * GLA_DK
HGRN_QF = HGRN_HEADS * HGRN_DK
SSD_BC = SSD_GROUPS * SSD_STATE
PROJ_SIZES = (
    GLA_QK, GLA_QK, D_BRANCH, GLA_GATE_RANK, D_BRANCH,
    D_BRANCH, D_BRANCH, D_BRANCH, MLSTM_HEADS, MLSTM_HEADS, D_BRANCH, D_BRANCH,
    HGRN_QF, HGRN_QF, D_BRANCH, D_BRANCH,
    D_BRANCH, SSD_BC, SSD_BC, SSD_HEADS, D_BRANCH,
)

LANES = 128
HIST = 8
VMEM_LIMIT = 56 * 1024 * 1024

REALIGN_ROWS = 256
IN_TILE = 256
MIX_TILE = 256
OUT_TILE = 512

CHUNK = 64
N_LEVELS = 6
MIN_VPU_LEVEL = 4

SM_GR, SM_I, SM_F, SM_DT = 0, 16, 20, 24

W_A = 2 * GLA_QK + 2 * D_BRANCH
W_B = 5 * D_BRANCH
W_C = 2 * HGRN_QF + 2 * D_BRANCH
W_D = 2 * D_BRANCH + 2 * SSD_BC
COL_A = 0
COL_B = COL_A + W_A
COL_C = COL_B + W_B
COL_D = COL_C + W_C
COL_S = COL_D + W_D


def _gla_constants():
    c = CHUNK
    t = np.arange(c)[:, None]
    d = np.arange(c)[None, :]
    blocks = [(d <= t), (d > t)]
    masks = [np.eye(c, dtype=bool)]
    for l in range(N_LEVELS):
        s = c >> (l + 1)
        mid_t = (t // (2 * s)) * (2 * s) + s
        upper = t >= mid_t
        if s < MIN_VPU_LEVEL:
            blocks.append(np.where(upper, (d >= mid_t) & (d <= t), (d > t) & (d <= mid_t - 1)))
        same = (t // (2 * s)) == (d // (2 * s))
        masks.append(same & upper & (d < mid_t))
    mstack = np.concatenate(blocks, axis=0).astype(np.float32)
    mstack = np.concatenate([mstack, mstack], axis=1)
    masks = np.stack(masks).astype(np.float32)
    return jnp.asarray(mstack, BF16), jnp.asarray(np.concatenate([masks, masks], axis=-1))


def _tri(n):
    return jnp.asarray(np.tril(np.ones((n, n), np.float32)), BF16)


def _dot(a, b):
    return jnp.dot(a, b, preferred_element_type=F32)


def _dot_nt(a, b):
    return lax.dot_general(a, b, (((1,), (1,)), ((), ())), preferred_element_type=F32)


def _dot_tn(a, b):
    return lax.dot_general(a, b, (((0,), (0,)), ((), ())), preferred_element_type=F32)


def _split3(x):
    hi = x.astype(BF16)
    r1 = x - hi.astype(F32)
    mid = r1.astype(BF16)
    lo = (r1 - mid.astype(F32)).astype(BF16)
    return hi, mid, lo


def _sel_dot(sel, x):
    hi, mid, lo = _split3(x)
    return _dot(sel, hi) + _dot(sel, mid) + _dot(sel, lo)


def _sel_dot_nt(sel, x):
    hi, mid, lo = _split3(x)
    return _dot_nt(sel, hi) + _dot_nt(sel, mid) + _dot_nt(sel, lo)


def _sel_dot2(sel2, x):
    hi = x.astype(BF16)
    mid = (x - hi.astype(F32)).astype(BF16)
    return _dot(sel2, jnp.concatenate([hi, mid], axis=0))


def _level_exponent(gcs, s):
    pieces = []
    for b in range(0, CHUNK, 2 * s):
        ref_row = gcs[b + s - 1:b + s, :]
        if s >= 8:
            pieces += [ref_row - gcs[b:b + s], gcs[b + s:b + 2 * s] - ref_row]
        else:
            diff = gcs[b:b + 2 * s] - ref_row
            pieces.append(jnp.minimum(diff, -diff))
    return jnp.concatenate(pieces, axis=0)


def _dot_sel2(x, sel2):
    hi = x.astype(BF16)
    mid = (x - hi.astype(F32)).astype(BF16)
    return _dot(jnp.concatenate([hi, mid], axis=1), sel2)


def _dot_f32(a, b):
    ah = a.astype(BF16)
    al = (a - ah.astype(F32)).astype(BF16)
    bh = b.astype(BF16)
    bl = (b - bh.astype(F32)).astype(BF16)
    return _dot(ah, bh) + _dot(al, bh) + _dot(ah, bl)


def _softplus(x):
    return jnp.maximum(x, 0.0) + jnp.log1p(jnp.exp(-jnp.abs(x)))


def _log_sigmoid(x):
    return -_softplus(-x)


def _silu(x):
    return x * jax.nn.sigmoid(x)


def _lane_iota():
    return lax.broadcasted_iota(jnp.int32, (1, LANES), 1)


def _side_by_side(a, b, zero):
    return jnp.concatenate([jnp.concatenate([a, zero], axis=1),
                            jnp.concatenate([zero, b], axis=1)], axis=0)


def _wide_segments():
    segs, src, dst = [], 0, 0
    run_start = None
    for size in PROJ_SIZES + (0,):
        wide = size >= LANES
        if wide and run_start is None:
            run_start = src
        if not wide and run_start is not None:
            segs.append((run_start, dst, src - run_start))
            dst += src - run_start
            run_start = None
        src += size
    return segs


def _realign_kernel(wt_ref, o_ref, os_ref):
    for src, dst, n in _wide_segments():
        for r in range(0, n, LANES):
            o_ref[:, dst + r:dst + r + LANES] = wt_ref[src + r:src + r + LANES, :].T.astype(BF16)
    narrow, src = [], 0
    for size in PROJ_SIZES:
        if size < LANES:
            narrow.append(wt_ref[src:src + size, :])
        src += size
    used = sum(p.shape[0] for p in narrow)
    narrow.append(jnp.zeros((LANES - used, wt_ref.shape[1]), F32))
    os_ref[...] = jnp.concatenate(narrow, axis=0).T


def _realign(wt, rows):
    depth, n_proj, d_model = wt.shape
    return pl.pallas_call(
        _realign_kernel, grid=(depth, d_model // rows),
        in_specs=[pl.BlockSpec((None, n_proj, rows), lambda l, j: (l, 0, j))],
        out_specs=[pl.BlockSpec((None, rows, COL_S), lambda l, j: (l, j, 0)),
                   pl.BlockSpec((None, rows, LANES), lambda l, j: (l, j, 0))],
        out_shape=[jax.ShapeDtypeStruct((depth, d_model, COL_S), BF16),
                   jax.ShapeDtypeStruct((depth, d_model, LANES), F32)],
        compiler_params=pltpu.CompilerParams(dimension_semantics=("arbitrary", "arbitrary"),
                                             vmem_limit_bytes=VMEM_LIMIT),
        name="realign",
    )(wt)


CONV_ROWS, CONV_LANES = 256, 128


def _causal_conv_silu(raw, xe_ref, hist_ref, w_ref, b_ref, taps, out_ref, scale=None):
    rows, n = raw.shape
    xe_ref[0:HIST, :] = hist_ref[...]
    xe_ref[HIST:HIST + rows, :] = raw
    hist_ref[...] = raw[rows - HIST:rows, :]
    for r in range(0, rows, CONV_ROWS):
        for l in range(0, n, CONV_LANES):
            cs = slice(l, l + CONV_LANES)
            acc = b_ref[:, cs] + w_ref[taps - 1:taps, cs] * xe_ref[HIST + r:HIST + r + CONV_ROWS, cs]
            for k in range(taps - 1):
                off = HIST - (taps - 1) + k + r
                acc = acc + w_ref[k:k + 1, cs] * xe_ref[off:off + CONV_ROWS, cs]
            out = _silu(acc)
            out_ref[r:r + CONV_ROWS, cs] = out if scale is None else out * scale[:, cs]


def _inproj_kernel(x_ref, xn_ref, nw_ref, w_ref, ws_ref, gw_ref, gb_ref, lb_ref, cwb_ref, cbb_ref, cwd_ref,
                   cbd_ref, gbias_ref, oa_ref, ob_ref, oc_ref, od_ref, os_ref, xe_s, hist_b, hist_d, u_s,
                   *, tiles_per_seq):
    step = pl.program_id(0)

    @pl.when(step % tiles_per_seq == 0)
    def _():
        hist_b[...] = jnp.zeros_like(hist_b)
        hist_d[...] = jnp.zeros_like(hist_d)

    def normed(x):
        ms = jnp.mean(x * x, axis=-1, keepdims=True)
        return (x * lax.rsqrt(ms + EPS) * nw_ref[...]).astype(BF16)

    @pl.when(step == 0)
    def _():
        u_s[0] = normed(x_ref[...])

    u = u_s[step % 2]
    lane = _lane_iota()

    small = _dot(u, ws_ref[...].astype(BF16))
    biased = small + gbias_ref[...]
    is_f = (lane >= SM_F) & (lane < SM_F + MLSTM_HEADS)
    is_dt = (lane >= SM_DT) & (lane < SM_DT + SSD_HEADS)
    os_ref[...] = jnp.where(is_f, _log_sigmoid(biased), jnp.where(is_dt, _softplus(biased), biased))

    raw = _dot(u, w_ref[:, COL_B:COL_B + W_B])
    conv_w = 2 * D_BRANCH
    k_scale = jnp.where(lax.broadcasted_iota(jnp.int32, (1, conv_w), 1) < D_BRANCH, 1.0, MLSTM_DH ** -0.5)
    _causal_conv_silu(raw[:, 0:conv_w], xe_s, hist_b, cwb_ref, cbb_ref, MLSTM_CONV, ob_ref, k_scale)
    o0, z0 = 3 * D_BRANCH, 4 * D_BRANCH
    ob_ref[:, conv_w:o0] = raw[:, conv_w:o0]
    ob_ref[:, o0:z0] = jax.nn.sigmoid(raw[:, o0:z0])
    ob_ref[:, z0:W_B] = _silu(raw[:, z0:W_B])

    raw = _dot(u, w_ref[:, COL_D:COL_D + W_D])
    conv_w = D_BRANCH + 2 * SSD_BC
    _causal_conv_silu(raw[:, 0:conv_w], xe_s, hist_d, cwd_ref, cbd_ref, SSD_CONV, od_ref)
    od_ref[:, conv_w:W_D] = _silu(raw[:, conv_w:W_D])
    u_s[(step + 1) % 2] = normed(xn_ref[...])

    raw = _dot(u, w_ref[:, COL_C:COL_C + W_C])
    lb = lb_ref[...]
    fr = raw[:, HGRN_QF:2 * HGRN_QF]
    oc_ref[:, 0:HGRN_QF] = raw[:, 0:HGRN_QF] * (HGRN_DK ** -0.5)
    oc_ref[:, HGRN_QF:2 * HGRN_QF] = (1.0 - lb) * jax.nn.sigmoid(-fr)
    z0 = 2 * HGRN_QF + D_BRANCH
    oc_ref[:, 2 * HGRN_QF:z0] = raw[:, 2 * HGRN_QF:z0]
    oc_ref[:, z0:W_C] = _silu(raw[:, z0:W_C])
    oc_ref[:, W_C:W_C + HGRN_QF] = jnp.log(jnp.maximum(lb + (1.0 - lb) * jax.nn.sigmoid(fr), 1e-30))

    gate = _dot_f32(small, gw_ref[...]) + gb_ref[...]
    oa_ref[:, W_A:W_A + GLA_QK] = _log_sigmoid(gate) * (1.0 / GLA_GATE_NORM)
    raw = _dot(u, w_ref[:, COL_A:COL_A + W_A])
    oa_ref[:, 0:GLA_QK] = raw[:, 0:GLA_QK] * (GLA_DK ** -0.5)
    z0 = 2 * GLA_QK + D_BRANCH
    oa_ref[:, GLA_QK:z0] = raw[:, GLA_QK:z0]
    oa_ref[:, z0:W_A] = _silu(raw[:, z0:W_A])


def _inproj(h, norm_w, w_wide, w_small, layer, consts, tile, tiles_per_seq):
    n_tok = h.shape[0]
    const = lambda i: (0, 0)
    row = lambda i: (i, 0)
    n_tiles = n_tok // tile
    in_specs = [pl.BlockSpec((tile, D_MODEL), row),
                pl.BlockSpec((tile, D_MODEL), lambda i: (jnp.minimum(i + 1, n_tiles - 1), 0)),
                pl.BlockSpec((1, D_MODEL), const),
                pl.BlockSpec((None, D_MODEL, COL_S), lambda i: (layer, 0, 0), pipeline_mode=pl.Buffered(1)),
                pl.BlockSpec((None, D_MODEL, LANES), lambda i: (layer, 0, 0))]
    in_specs += [pl.BlockSpec(c.shape, const) for c in consts]
    widths = (W_A + GLA_QK, W_B, W_C + HGRN_QF, W_D, LANES)
    conv_w = 2 * D_BRANCH
    return pl.pallas_call(
        functools.partial(_inproj_kernel, tiles_per_seq=tiles_per_seq),
        grid=(n_tiles,), in_specs=in_specs,
        out_specs=[pl.BlockSpec((tile, w), row) for w in widths],
        out_shape=[jax.ShapeDtypeStruct((n_tok, w), F32) for w in widths],
        scratch_shapes=[pltpu.VMEM((tile + HIST, conv_w), F32), pltpu.VMEM((HIST, conv_w), F32),
                        pltpu.VMEM((HIST, conv_w), F32), pltpu.VMEM((2, tile, D_MODEL), BF16)],
        compiler_params=pltpu.CompilerParams(dimension_semantics=("arbitrary",),
                                             vmem_limit_bytes=VMEM_LIMIT),
        name="inproj",
    )(h, h, norm_w, w_wide, w_small, *consts)


def _outproj_kernel(h_ref, ya_ref, yb_ref, yc_ref, yd_ref, w_ref, fw_ref, o_ref, *, final):
    acc = h_ref[...]
    for i, y_ref in enumerate((ya_ref, yb_ref, yc_ref, yd_ref)):
        acc = acc + _dot(y_ref[...], w_ref[i * D_BRANCH:(i + 1) * D_BRANCH, :])
    if final:
        ms = jnp.mean(acc * acc, axis=-1, keepdims=True)
        acc = acc * lax.rsqrt(ms + EPS) * fw_ref[...]
    o_ref[...] = acc


def _outproj(h, ys, w_out, final_w, final, tile):
    n_tok = h.shape[0]
    const = lambda i: (0, 0)
    row = lambda i: (i, 0)
    in_specs = [pl.BlockSpec((tile, D_MODEL), row)]
    in_specs += [pl.BlockSpec((tile, D_BRANCH), row) for _ in ys]
    in_specs += [pl.BlockSpec(w_out.shape, const), pl.BlockSpec((1, D_MODEL), const)]
    return pl.pallas_call(
        functools.partial(_outproj_kernel, final=final),
        grid=(n_tok // tile,), in_specs=in_specs,
        out_specs=pl.BlockSpec((tile, D_MODEL), row),
        out_shape=jax.ShapeDtypeStruct((n_tok, D_MODEL), F32),
        compiler_params=pltpu.CompilerParams(dimension_semantics=("arbitrary",),
                                             vmem_limit_bytes=VMEM_LIMIT),
        name="outproj",
    )(h, *ys, w_out, final_w)


def _norm_gate_store(y_ref, rows, col0, parts, nw_ref, z_parts):
    width = sum(p.shape[-1] for p in parts)
    ss = sum(jnp.sum(p * p, axis=-1, keepdims=True) for p in parts)
    scale = lax.rsqrt(ss * (1.0 / width) + EPS)
    c = col0
    for p, z in zip(parts, z_parts):
        w = p.shape[-1]
        out = p * scale * nw_ref[:, c:c + w]
        if z is not None:
            out = out * z
        y_ref[rows, c:c + w] = out.astype(y_ref.dtype)
        c += w


N_STAGES = 5
CHUNK_UNROLL = 1


def _run_chunks(work, tile):
    def body(c, carry):
        rows = pl.ds(pl.multiple_of(c * CHUNK, CHUNK), CHUNK)
        live = [(dict(d), stages) for d, stages in work]
        for k in range(N_STAGES):
            joined = []
            for d, stages in live:
                joint, fn = stages[k] if isinstance(stages[k], tuple) else (None, stages[k])
                if joint is not None and not any(stages is s for s in joined):
                    joint([dd for dd, st in live if st is stages])
                    joined.append(stages)
                fn(d, rows)
        return carry

    lax.fori_loop(0, tile // CHUNK, body, 0, unroll=CHUNK_UNROLL)


def _gla_stages(ms_ref, mk_ref):
    lane = _lane_iota()

    def lane_mask(d, j):
        dk = d["dk"]
        return None if dk == LANES else (lane >= j * dk) & (lane < (j + 1) * dk)

    def pick(a, lm):
        return a if lm is None else jnp.where(lm, a, jnp.zeros_like(a))

    def n_groups(d):
        return d["q"].shape[1] // LANES

    def cumsums(d, rows):
        d["es"] = [_sel_dot2(ms_ref[...], d["lg"][rows, p * 2 * LANES:(p + 1) * 2 * LANES])
                   for p in range(n_groups(d) // 2)]

    def operands(d, rows):
        ops = []
        for g in range(n_groups(d)):
            ls = slice(g * LANES, (g + 1) * LANES)
            e = d["es"][g // 2][:, (g % 2) * LANES:(g % 2 + 1) * LANES]
            q = d["q"][rows, ls]
            k = d["k"][rows, ls]
            gcs = e[0:CHUNK]
            qg = (q * jnp.exp(gcs)).astype(BF16)
            kd = (k * jnp.exp(e[CHUNK:2 * CHUNK])).astype(BF16)
            dec = jnp.exp(gcs[CHUNK - 1:CHUNK, :])
            qb = q.astype(BF16)
            kb = k.astype(BF16)
            ql, kl = [qb], [kb]
            mxu_block = 2
            for l in range(N_LEVELS):
                s = CHUNK >> (l + 1)
                if s >= MIN_VPU_LEVEL:
                    el = _level_exponent(gcs, s)
                else:
                    el = e[mxu_block * CHUNK:(mxu_block + 1) * CHUNK]
                    mxu_block += 1
                w = jnp.exp(el).astype(BF16)
                ql.append(qb * w)
                kl.append(kb * w)
            ops.append((qg, kd, dec, ql, kl))
        d["ops"] = ops

    def scores(d, rows):
        heads = []
        for g in range(n_groups(d)):
            ql, kl = d["ops"][g][3], d["ops"][g][4]
            for j in range(LANES // d["dk"]):
                lm = lane_mask(d, j)
                a = _dot_nt(pick(ql[0], lm), kl[0]) * mk_ref[0][:, 0:CHUNK]
                for l in range(1, N_LEVELS + 1):
                    a = a + _dot_nt(pick(ql[l], lm), kl[l]) * mk_ref[l][:, 0:CHUNK]
                heads.append((g, j, a.astype(BF16)))
        d["heads"] = heads

    def outputs(d, rows):
        hp = LANES // d["dk"]
        outs, upds = [], [None] * n_groups(d)
        sts = [d["st"][g] for g in range(n_groups(d))]
        stbs = [st.astype(BF16) for st in sts]
        for g, j, ab in d["heads"]:
            h = g * hp + j
            lm = lane_mask(d, j)
            qg, kd = d["ops"][g][0], d["ops"][g][1]
            vh = d["v"][rows, h * LANES:(h + 1) * LANES].astype(BF16)
            outs.append(_dot(ab, vh) + _dot_nt(pick(qg, lm), stbs[g]))
            u = _dot_tn(vh, kd)
            upds[g] = u if upds[g] is None else jnp.where(lm, u, upds[g])
        d["outs"], d["upds"], d["sts"] = outs, upds, sts

    def finish(d, rows):
        for g in range(n_groups(d)):
            d["st"][g] = d["sts"][g] * d["ops"][g][2] + d["upds"][g]
        for h, o in enumerate(d["outs"]):
            _norm_gate_store(d["y"], rows, h * LANES, [o], d["nw"],
                             [d["z"][rows, h * LANES:(h + 1) * LANES]])

    return cumsums, operands, scores, outputs, finish


def _gla_stream(p_ref, s, qk, dk, y_ref, st_s, nw_ref):
    v0, z0, lg0 = 2 * qk, 2 * qk + D_BRANCH, 2 * qk + 2 * D_BRANCH
    return dict(q=p_ref.at[s, :, 0:qk], k=p_ref.at[s, :, qk:v0], v=p_ref.at[s, :, v0:z0],
                z=p_ref.at[s, :, z0:lg0], lg=p_ref.at[s, :, lg0:lg0 + qk],
                y=y_ref.at[s], st=st_s.at[s], nw=nw_ref, dk=dk)


def _mlstm_constants():
    half = LANES // 2
    full0 = half * MLSTM_HEADS
    e = np.zeros((LANES, full0 + 2 * LANES * MLSTM_HEADS), np.float32)
    diff0 = full0 + LANES * MLSTM_HEADS
    sel = np.zeros((16, LANES), np.float32)
    for h in range(MLSTM_HEADS):
        e[SM_F + h, h * half:(h + 1) * half] = 1.0
        e[SM_F + h, full0 + h * LANES:full0 + (h + 1) * LANES] = 1.0
        e[SM_I + h, diff0 + h * LANES:diff0 + (h + 1) * LANES] = 1.0
        e[SM_F + h, diff0 + h * LANES:diff0 + (h + 1) * LANES] = -1.0
        sel[h // 2, SM_I + h] = 1.0
        sel[h // 2, SM_F + h] = -1.0
    return jnp.asarray(np.concatenate([e, e], axis=0), BF16), jnp.asarray(sel, BF16)


def _mlstm_stages(nw_ref, tri_ref, exp_ref, sel_ref):
    lane = _lane_iota()
    is_f = (lane >= SM_F) & (lane < SM_F + MLSTM_HEADS)

    half = LANES // 2
    ri = lax.broadcasted_iota(jnp.int32, (CHUNK, LANES), 0)
    ci = lax.broadcasted_iota(jnp.int32, (CHUNK, LANES), 1)
    causal2 = (ci & (half - 1)) <= ri
    lo_half = lane < half
    even = (lane & 1) == 0
    v0, o0, z0 = 2 * D_BRANCH, 3 * D_BRANCH, 4 * D_BRANCH
    heads = range(MLSTM_HEADS)
    pairs = range(MLSTM_HEADS // 2)
    full0 = half * MLSTM_HEADS
    diff0 = full0 + LANES * MLSTM_HEADS

    def qk_products(d, rows):
        qk_ref = d["p"]
        qbs = [qk_ref[rows, h * LANES:(h + 1) * LANES].astype(BF16) for h in heads]
        d["ks"] = [qk_ref[rows, D_BRANCH + h * LANES:D_BRANCH + (h + 1) * LANES] for h in heads]
        kbs = [k.astype(BF16) for k in d["ks"]]
        zero = jnp.zeros((CHUNK, LANES), BF16)
        ones = jnp.ones((CHUNK, LANES), BF16)
        d["qk_raw"] = [_dot_nt(jnp.concatenate([qbs[2 * p], qbs[2 * p + 1]], axis=1),
                               _side_by_side(kbs[2 * p], kbs[2 * p + 1], zero)) for p in pairs]
        d["qc"] = [_dot(qbs[h], d["c"][h].astype(BF16)) for h in heads]
        d["vaug"] = [jnp.concatenate([d["p"][rows, v0 + h * LANES:v0 + (h + 1) * LANES].astype(BF16),
                                      ones], axis=1) for h in heads]

    def gate_sums(d, rows):
        gc = d["g"][rows, :]
        bcol = _sel_dot(tri_ref[...], gc)
        ib = jnp.where(is_f, bcol, gc)
        d["ib"] = ib
        by_parity = jnp.concatenate([jnp.where(even, ib, 0.0), jnp.where(even, 0.0, ib)], axis=0)
        d["drows"] = _sel_dot_nt(sel_ref[...], by_parity)

    def spread_gates(ds):
        ex = _dot_sel2(jnp.concatenate([d["ib"] for d in ds], axis=0), exp_ref[...])
        for i, d in enumerate(ds):
            d["ex"] = ex[i * CHUNK:(i + 1) * CHUNK]

    def weights(d, rows):
        ex = d["ex"]
        mx = d["m"][0:1, :]
        ss, mrs = [], []
        for p in pairs:
            bx = ex[:, p * LANES:(p + 1) * LANES]
            lw = jnp.where(causal2, bx + d["drows"][p:p + 1, :], NEG_BIG)
            mr0 = jnp.max(jnp.where(lo_half, lw, NEG_BIG), axis=-1, keepdims=True)
            mr1 = jnp.max(jnp.where(lo_half, NEG_BIG, lw), axis=-1, keepdims=True)
            m64 = jnp.where(lo_half, mx[:, 2 * p * LANES:(2 * p + 1) * LANES],
                            mx[:, (2 * p + 1) * LANES:(2 * p + 2) * LANES])
            m_row = jnp.maximum(jnp.where(lo_half, mr0, mr1), bx + m64)
            ss.append((d["qk_raw"][p] * jnp.exp(lw - m_row)).astype(BF16))
            mrs += [mr0, mr1]
        d["ss"], d["mrs"] = ss, mrs
        b_last = ex[CHUNK - 1:CHUNK, full0:diff0]
        lwe = ex[:, diff0:] + b_last
        m_new = jnp.maximum(b_last + mx, jnp.max(lwe, axis=0, keepdims=True))
        d["cd"] = jnp.exp(b_last + mx - m_new)
        d["m_new"] = m_new
        kw = jnp.exp(lwe - m_new)
        d["kws"] = [(d["ks"][h] * kw[:, h * LANES:(h + 1) * LANES]).astype(BF16) for h in heads]

    def numerators(d, rows):
        zero = jnp.zeros((CHUNK, 2 * LANES), BF16)
        d["nums"] = [_dot(d["ss"][p], _side_by_side(d["vaug"][2 * p], d["vaug"][2 * p + 1], zero))
                     for p in pairs]
        d["cups"] = [_dot_tn(d["kws"][h], d["vaug"][h]) for h in heads]

    def finish(d, rows):
        ex = d["ex"]
        mx = d["m"][0:1, :]
        for h in heads:
            p, hd = divmod(h, 2)
            hs = slice(h * LANES, (h + 1) * LANES)
            m_inter = ex[:, full0 + h * LANES:full0 + (h + 1) * LANES] + mx[:, hs]
            m_row = jnp.maximum(d["mrs"][h], m_inter)
            inter = jnp.exp(m_inter - m_row)
            sv = d["nums"][p][:, hd * 2 * LANES:(hd + 1) * 2 * LANES]
            num = sv[:, 0:LANES] + inter * d["qc"][h][:, 0:LANES]
            den = sv[:, LANES:] + inter * d["qc"][h][:, LANES:]
            hh = num / jnp.maximum(jnp.abs(den), jnp.exp(-m_row))
            cd = d["cd"][:, hs]
            d["c"][h] = jnp.concatenate([cd, cd], axis=1) * d["c"][h] + d["cups"][h]
            og = d["p"][rows, o0 + h * LANES:o0 + (h + 1) * LANES]
            _norm_gate_store(d["y"], rows, h * LANES, [og * hh], nw_ref,
                             [d["p"][rows, z0 + h * LANES:z0 + (h + 1) * LANES]])
        d["m"][0:1, :] = d["m_new"]

    return qk_products, gate_sums, (spread_gates, weights), numerators, finish


def _ssd_constants():
    e = np.zeros((LANES, SSD_HEADS * SSD_HEAD_DIM), np.float32)
    sel = np.zeros((16, LANES), np.float32)
    for h in range(SSD_HEADS):
        e[SM_DT + h, h * SSD_HEAD_DIM:(h + 1) * SSD_HEAD_DIM] = 1.0
        sel[h // 2, SM_DT + h] = 1.0
    return jnp.asarray(np.concatenate([e, e], axis=0), BF16), jnp.asarray(sel, BF16)


def _ssd_stages(alog_ref, dx_ref, nw_ref, tri_ref, sel_ref, exp_ref):
    lane = _lane_iota()
    is_dt = (lane >= SM_DT) & (lane < SM_DT + SSD_HEADS)

    half = LANES // 2
    ri = lax.broadcasted_iota(jnp.int32, (CHUNK, LANES), 0)
    ci = lax.broadcasted_iota(jnp.int32, (CHUNK, LANES), 1)
    causal2 = (ci & (half - 1)) <= ri
    lo_half = lane < half
    even = (lane & 1) == 0
    a_lane = jnp.where(is_dt, -jnp.exp(alog_ref[...]), 0.0)
    b0, c0, z0 = D_BRANCH, D_BRANCH + SSD_BC, D_BRANCH + 2 * SSD_BC
    groups = range(SSD_GROUPS)
    group_w = D_BRANCH // SSD_GROUPS
    pairs = range(D_BRANCH // LANES)
    pairs_per_group = group_w // LANES

    def products(d, rows):
        xbc = d["p"]
        d["bgs"] = [xbc[rows, b0 + g * SSD_STATE:b0 + (g + 1) * SSD_STATE].astype(BF16) for g in groups]
        cgbs = [xbc[rows, c0 + g * SSD_STATE:c0 + (g + 1) * SSD_STATE].astype(BF16) for g in groups]
        d["cb2"] = [_dot_nt(cgbs[g], jnp.concatenate([d["bgs"][g], d["bgs"][g]], axis=0))
                    for g in groups]
        d["cst"] = [_dot(cgbs[g], d["st"][g].astype(BF16)) for g in groups]

    def decay_sums(d, rows):
        dt = d["dt"][rows, :]
        acs = _sel_dot(tri_ref[...], dt * a_lane)
        d["dt_acs"] = jnp.concatenate([dt, acs], axis=0)
        by_parity = jnp.concatenate([jnp.where(even, acs, 0.0), jnp.where(even, 0.0, acs)], axis=0)
        d["a_rows"] = _sel_dot_nt(sel_ref[...], by_parity)

    def spread_decays(ds):
        ex = _dot_sel2(jnp.concatenate([d["dt_acs"] for d in ds], axis=0), exp_ref[...])
        for i, d in enumerate(ds):
            d["dtx"] = ex[2 * i * CHUNK:(2 * i + 1) * CHUNK]
            d["acs_x"] = ex[(2 * i + 1) * CHUNK:(2 * i + 2) * CHUNK]

    def decays(d, rows):
        d["xss"], d["ms"], d["xblk"], xdecs = [], [], [], []
        for p in pairs:
            ls = slice(p * LANES, (p + 1) * LANES)
            ax = d["acs_x"][:, ls]
            lmat = jnp.exp(jnp.where(causal2, ax - d["a_rows"][p:p + 1, :], NEG_BIG))
            d["ms"].append((d["cb2"][p // pairs_per_group] * lmat).astype(BF16))
            xs = d["p"][rows, ls]
            xdt = xs * d["dtx"][:, ls]
            d["xss"].append(xs)
            d["xblk"].append(jnp.concatenate([jnp.where(lo_half, xdt, 0.0), jnp.where(lo_half, 0.0, xdt)],
                                             axis=0).astype(BF16))
            xdecs.append((xdt * jnp.exp(ax[CHUNK - 1:CHUNK, :] - ax)).astype(BF16))
        d["xdec"] = [jnp.concatenate(xdecs[g * pairs_per_group:(g + 1) * pairs_per_group], axis=1)
                     for g in groups]

    def chunk_products(d, rows):
        d["yds"] = [_dot(d["ms"][p], d["xblk"][p]) for p in pairs]
        d["ups"] = [_dot_tn(d["bgs"][g], d["xdec"][g]) for g in groups]

    def finish(d, rows):
        for g in groups:
            gs = slice(g * group_w, (g + 1) * group_w)
            eax = jnp.exp(d["acs_x"][:, gs])
            d["st"][g] = d["st"][g] * eax[CHUNK - 1:CHUNK, :] + d["ups"][g]
            ys = []
            for pp in range(pairs_per_group):
                p = g * pairs_per_group + pp
                ls = slice(p * LANES, (p + 1) * LANES)
                y = (d["cst"][g][:, pp * LANES:(pp + 1) * LANES] * eax[:, pp * LANES:(pp + 1) * LANES]
                     + dx_ref[:, ls] * d["xss"][p] + d["yds"][p])
                ys.append(y * d["p"][rows, z0 + p * LANES:z0 + (p + 1) * LANES])
            _norm_gate_store(d["y"], rows, g * group_w, ys, nw_ref, [None] * len(ys))

    return products, decay_sums, (spread_decays, decays), chunk_products, finish


def _mixers_kernel(pa_ref, pb_ref, pc_ref, pd_ref, sm_ref, nwa_ref, nwb_ref, nwc_ref, nwd_ref,
                   ms_ref, mk_ref, tri_ref, mexp_ref, msel_ref, alog_ref, dx_ref, dsel_ref, dexp_ref,
                   ya_ref, yb_ref, yc_ref, yd_ref, sta_s, stc_s, cb_s, mb_s, std_s, *, tile):
    @pl.when(pl.program_id(0) == 0)
    def _():
        for r in (sta_s, stc_s, cb_s, mb_s, std_s):
            r[...] = jnp.zeros_like(r)

    gla = _gla_stages(ms_ref, mk_ref)
    mlstm = _mlstm_stages(nwb_ref, tri_ref, mexp_ref, msel_ref)
    ssd = _ssd_stages(alog_ref, dx_ref, nwd_ref, tri_ref, dsel_ref, dexp_ref)
    work = []
    for s in range(pa_ref.shape[0]):
        work.append((_gla_stream(pa_ref, s, GLA_QK, GLA_DK, ya_ref, sta_s, nwa_ref), gla))
        work.append((dict(p=pb_ref.at[s], g=sm_ref.at[s], c=cb_s.at[s], m=mb_s.at[s], y=yb_ref.at[s]),
                     mlstm))
        work.append((_gla_stream(pc_ref, s, HGRN_QF, HGRN_DK, yc_ref, stc_s, nwc_ref), gla))
        work.append((dict(p=pd_ref.at[s], dt=sm_ref.at[s], st=std_s.at[s], y=yd_ref.at[s]), ssd))
    _run_chunks(work, tile)


def _mixers(tok_inputs, const_inputs, n_batch, seq, tile):
    tok = lambda t: (0, t, 0)
    in_specs = [pl.BlockSpec((n_batch, tile, a.shape[2]), tok) for a in tok_inputs]
    for a in const_inputs:
        in_specs.append(pl.BlockSpec(a.shape, lambda t, nd=a.ndim: (0,) * nd))
    scratch = [pltpu.VMEM((n_batch, GLA_QK // LANES, GLA_DV, LANES), F32),
               pltpu.VMEM((n_batch, HGRN_QF // LANES, HGRN_DV, LANES), F32),
               pltpu.VMEM((n_batch, MLSTM_HEADS, MLSTM_DH, 2 * MLSTM_DH), F32),
               pltpu.VMEM((n_batch, 8, MLSTM_HEADS * LANES), F32),
               pltpu.VMEM((n_batch, SSD_GROUPS, SSD_STATE, D_BRANCH // SSD_GROUPS), F32)]
    return pl.pallas_call(
        functools.partial(_mixers_kernel, tile=tile), grid=(seq // tile,), in_specs=in_specs,
        out_specs=[pl.BlockSpec((n_batch, tile, D_BRANCH), tok)] * 4,
        out_shape=[jax.ShapeDtypeStruct((n_batch, seq, D_BRANCH), BF16)] * 4,
        scratch_shapes=scratch,
        compiler_params=pltpu.CompilerParams(dimension_semantics=("arbitrary",),
                                             vmem_limit_bytes=VMEM_LIMIT),
        name="mixers",
    )(*tok_inputs, *const_inputs)


def _pad_lanes(parts, total=LANES):
    width = sum(p.shape[-1] for p in parts)
    lead = parts[0].shape[:-1]
    return jnp.concatenate(list(parts) + [jnp.zeros(lead + (total - width,), parts[0].dtype)], axis=-1)


def _small_vector(i_part, f_part, dt_part):
    z = jnp.zeros((GLA_GATE_RANK,), F32)
    return _pad_lanes([z, i_part.astype(F32), f_part.astype(F32), dt_part.astype(F32)])[None, :]


def kernel(x, norm_w, w_in, gla_gate_w, gla_gate_b, gla_norm_w, ml_conv_w, ml_conv_b, ml_i_b, ml_f_b,
           ml_norm_w, hg_lb_logits, hg_norm_w, ssd_conv_w, ssd_conv_b, ssd_dt_bias, ssd_A_log, ssd_D,
           ssd_norm_w, w_out, final_norm_w):
    n_batch, seq, _ = x.shape
    depth = w_in.shape[0]
    tile = min(MIX_TILE, seq)
    n_tok = n_batch * seq

    mstack, masks = _gla_constants()
    tri_c = _tri(CHUNK)
    ml_exp, ml_sel = _mlstm_constants()
    ssd_exp, ssd_sel = _ssd_constants()
    zero4 = jnp.zeros((MLSTM_HEADS,), F32)
    in_tile = min(IN_TILE, seq)

    p = jax.nn.softmax(hg_lb_logits.astype(F32), axis=0)
    lower_bounds = jnp.cumsum(p, axis=0) - p[0:1]

    w_wide, w_small = _realign(jnp.swapaxes(w_in, 1, 2), REALIGN_ROWS)

    h = x.reshape(n_tok, D_MODEL)
    row2 = lambda v: v.astype(F32).reshape(1, -1)
    for l in range(depth):
        gate_w = jnp.concatenate(
            [gla_gate_w[l].astype(F32), jnp.zeros((LANES - GLA_GATE_RANK, GLA_QK), F32)], axis=0)
        consts = [gate_w, row2(gla_gate_b[l]), row2(lower_bounds[l]),
                  ml_conv_w[l].astype(F32), row2(ml_conv_b[l]),
                  ssd_conv_w[l].astype(F32), row2(ssd_conv_b[l]),
                  _small_vector(ml_i_b[l], ml_f_b[l], ssd_dt_bias[l])]
        pa, pb, pc, pd, sm = [p.reshape(n_batch, seq, -1)
                              for p in _inproj(h, row2(norm_w[l]), w_wide, w_small, l, consts, in_tile,
                                               seq // in_tile)]

        ys = _mixers(
            [pa, pb, pc, pd, sm],
            [row2(gla_norm_w[l]), row2(ml_norm_w[l]), row2(hg_norm_w[l]), row2(ssd_norm_w[l]),
             mstack, masks, tri_c, ml_exp, ml_sel, _small_vector(zero4, zero4, ssd_A_log[l]),
             row2(jnp.repeat(ssd_D[l].astype(F32), SSD_HEAD_DIM)), ssd_sel, ssd_exp],
            n_batch, seq, tile)
        ys = [y.reshape(n_tok, D_BRANCH) for y in ys]
        h = _outproj(h, ys, w_out[l].astype(BF16), row2(final_norm_w), l == depth - 1,
                     min(OUT_TILE, n_tok))
    return h.reshape(n_batch, seq, D_MODEL)
```

```python
import functools

import numpy as np
import jax
import jax.numpy as jnp
from jax import lax
from jax.experimental import pallas as pl
from jax.experimental.pallas import tpu as pltpu

F32 = jnp.float32
BF16 = jnp.bfloat16

D_MODEL = 1024
D_BRANCH = 512
EPS = 1e-6
NEG_BIG = -1e30

GLA_HEADS, GLA_DK, GLA_DV = 4, 64, 128
GLA_GATE_RANK, GLA_GATE_NORM = 16, 16.0
MLSTM_HEADS, MLSTM_DH, MLSTM_CONV = 4, 128, 4
HGRN_HEADS, HGRN_DK, HGRN_DV = 4, 128, 128
SSD_HEAD_DIM, SSD_HEADS, SSD_GROUPS, SSD_STATE, SSD_CONV = 64, 8, 2, 128, 4
GLA_QK = GLA_HEADS * GLA_DK
HGRN_QF = HGRN_HEADS * HGRN_DK
SSD_BC = SSD_GROUPS * SSD_STATE
PROJ_SIZES = (
    GLA_QK, GLA_QK, D_BRANCH, GLA_GATE_RANK, D_BRANCH,
    D_BRANCH, D_BRANCH, D_BRANCH, MLSTM_HEADS, MLSTM_HEADS, D_BRANCH, D_BRANCH,
    HGRN_QF, HGRN_QF, D_BRANCH, D_BRANCH,
    D_BRANCH, SSD_BC, SSD_BC, SSD_HEADS, D_BRANCH,
)

LANES = 128
HIST = 8
VMEM_LIMIT = 56 * 1024 * 1024

REALIGN_ROWS = 256
IN_TILE = 256
MIX_TILE = 256
OUT_TILE = 512

CHUNK = 64
N_LEVELS = 6
MIN_VPU_LEVEL = 4

SM_GR, SM_I, SM_F, SM_DT = 0, 16, 20, 24

W_A = 2 * GLA_QK + 2 * D_BRANCH
W_B = 5 * D_BRANCH
W_C = 2 * HGRN_QF + 2 * D_BRANCH
W_D = 2 * D_BRANCH + 2 * SSD_BC
COL_A = 0
COL_B = COL_A + W_A
COL_C = COL_B + W_B
COL_D = COL_C + W_C
COL_S = COL_D + W_D


def _gla_constants():
    c = CHUNK
    t = np.arange(c)[:, None]
    d = np.arange(c)[None, :]
    blocks = [(d <= t), (d > t)]
    masks = [np.eye(c, dtype=bool)]
    for l in range(N_LEVELS):
        s = c >> (l + 1)
        mid_t = (t // (2 * s)) * (2 * s) + s
        upper = t >= mid_t
        if s < MIN_VPU_LEVEL:
            blocks.append(np.where(upper, (d >= mid_t) & (d <= t), (d > t) & (d <= mid_t - 1)))
        same = (t // (2 * s)) == (d // (2 * s))
        masks.append(same & upper & (d < mid_t))
    mstack = np.concatenate(blocks, axis=0).astype(np.float32)
    mstack = np.concatenate([mstack, mstack], axis=1)
    masks = np.stack(masks).astype(np.float32)
    return jnp.asarray(mstack, BF16), jnp.asarray(np.concatenate([masks, masks], axis=-1))


def _tri2(n):
    tri = np.tril(np.ones((n, n), np.float32))
    return jnp.asarray(np.concatenate([tri, tri], axis=1), BF16)


def _dot(a, b):
    return jnp.dot(a, b, preferred_element_type=F32)


def _dot_nt(a, b):
    return lax.dot_general(a, b, (((1,), (1,)), ((), ())), preferred_element_type=F32)


def _dot_tn(a, b):
    return lax.dot_general(a, b, (((0,), (0,)), ((), ())), preferred_element_type=F32)


def _sel_dot2(sel2, x):
    hi = x.astype(BF16)
    mid = (x - hi.astype(F32)).astype(BF16)
    return _dot(sel2, jnp.concatenate([hi, mid], axis=0))


def _sel_dot2_nt(sel2, x):
    hi = x.astype(BF16)
    mid = (x - hi.astype(F32)).astype(BF16)
    return _dot_nt(sel2, jnp.concatenate([hi, mid], axis=1))


def _level_exponent(gcs, s):
    pieces = []
    for b in range(0, CHUNK, 2 * s):
        ref_row = gcs[b + s - 1:b + s, :]
        if s >= 8:
            pieces += [ref_row - gcs[b:b + s], gcs[b + s:b + 2 * s] - ref_row]
        else:
            diff = gcs[b:b + 2 * s] - ref_row
            pieces.append(jnp.minimum(diff, -diff))
    return jnp.concatenate(pieces, axis=0)


def _dot_sel2(x, sel2):
    hi = x.astype(BF16)
    mid = (x - hi.astype(F32)).astype(BF16)
    return _dot(jnp.concatenate([hi, mid], axis=1), sel2)


def _dot_f32(a, b):
    ah = a.astype(BF16)
    al = (a - ah.astype(F32)).astype(BF16)
    bh = b.astype(BF16)
    bl = (b - bh.astype(F32)).astype(BF16)
    return _dot(ah, bh) + _dot(al, bh) + _dot(ah, bl)


def _softplus(x):
    return jnp.maximum(x, 0.0) + jnp.log1p(jnp.exp(-jnp.abs(x)))


def _log_sigmoid(x):
    return -_softplus(-x)


def _silu(x):
    return x * jax.nn.sigmoid(x)


def _lane_iota():
    return lax.broadcasted_iota(jnp.int32, (1, LANES), 1)


def _side_by_side(a, b, zero):
    return jnp.concatenate([jnp.concatenate([a, zero], axis=1),
                            jnp.concatenate([zero, b], axis=1)], axis=0)


def _wide_segments():
    segs, src, dst = [], 0, 0
    run_start = None
    for size in PROJ_SIZES + (0,):
        wide = size >= LANES
        if wide and run_start is None:
            run_start = src
        if not wide and run_start is not None:
            segs.append((run_start, dst, src - run_start))
            dst += src - run_start
            run_start = None
        src += size
    return segs


def _realign_kernel(wt_ref, o_ref, os_ref):
    for src, dst, n in _wide_segments():
        for r in range(0, n, LANES):
            o_ref[:, dst + r:dst + r + LANES] = wt_ref[src + r:src + r + LANES, :].T.astype(BF16)
    narrow, src = [], 0
    for size in PROJ_SIZES:
        if size < LANES:
            narrow.append(wt_ref[src:src + size, :])
        src += size
    used = sum(p.shape[0] for p in narrow)
    narrow.append(jnp.zeros((LANES - used, wt_ref.shape[1]), F32))
    os_ref[...] = jnp.concatenate(narrow, axis=0).T


def _realign(wt, rows):
    depth, n_proj, d_model = wt.shape
    return pl.pallas_call(
        _realign_kernel, grid=(depth, d_model // rows),
        in_specs=[pl.BlockSpec((None, n_proj, rows), lambda l, j: (l, 0, j))],
        out_specs=[pl.BlockSpec((None, rows, COL_S), lambda l, j: (l, j, 0)),
                   pl.BlockSpec((None, rows, LANES), lambda l, j: (l, j, 0))],
        out_shape=[jax.ShapeDtypeStruct((depth, d_model, COL_S), BF16),
                   jax.ShapeDtypeStruct((depth, d_model, LANES), F32)],
        compiler_params=pltpu.CompilerParams(dimension_semantics=("arbitrary", "arbitrary"),
                                             vmem_limit_bytes=VMEM_LIMIT),
        name="realign",
    )(wt)


CONV_ROWS, CONV_LANES = 256, 128


def _causal_conv_silu(raw, xe_ref, hist_ref, w_ref, b_ref, taps, out_ref, scale=None):
    rows, n = raw.shape
    xe_ref[0:HIST, :] = hist_ref[...]
    xe_ref[HIST:HIST + rows, :] = raw
    hist_ref[...] = raw[rows - HIST:rows, :]
    for r in range(0, rows, CONV_ROWS):
        for l in range(0, n, CONV_LANES):
            cs = slice(l, l + CONV_LANES)
            acc = b_ref[:, cs] + w_ref[taps - 1:taps, cs] * xe_ref[HIST + r:HIST + r + CONV_ROWS, cs]
            for k in range(taps - 1):
                off = HIST - (taps - 1) + k + r
                acc = acc + w_ref[k:k + 1, cs] * xe_ref[off:off + CONV_ROWS, cs]
            out = _silu(acc)
            out_ref[r:r + CONV_ROWS, cs] = out if scale is None else out * scale[:, cs]


def _inproj_kernel(x_ref, xn_ref, nw_ref, w_ref, ws_ref, gw_ref, gb_ref, lb_ref, cwb_ref, cbb_ref, cwd_ref,
                   cbd_ref, gbias_ref, oa_ref, ob_ref, oc_ref, od_ref, os_ref, xe_s, hist_b, hist_d, u_s,
                   *, tiles_per_seq):
    step = pl.program_id(0)

    @pl.when(step % tiles_per_seq == 0)
    def _():
        hist_b[...] = jnp.zeros_like(hist_b)
        hist_d[...] = jnp.zeros_like(hist_d)

    def normed(x):
        ms = jnp.mean(x * x, axis=-1, keepdims=True)
        return (x * lax.rsqrt(ms + EPS) * nw_ref[...]).astype(BF16)

    @pl.when(step == 0)
    def _():
        u_s[0] = normed(x_ref[...])

    u = u_s[step % 2]
    lane = _lane_iota()

    small = _dot(u, ws_ref[...].astype(BF16))
    biased = small + gbias_ref[...]
    is_f = (lane >= SM_F) & (lane < SM_F + MLSTM_HEADS)
    is_dt = (lane >= SM_DT) & (lane < SM_DT + SSD_HEADS)
    os_ref[...] = jnp.where(is_f, _log_sigmoid(biased), jnp.where(is_dt, _softplus(biased), biased))

    raw = _dot(u, w_ref[:, COL_B:COL_B + W_B])
    conv_w = 2 * D_BRANCH
    k_scale = jnp.where(lax.broadcasted_iota(jnp.int32, (1, conv_w), 1) < D_BRANCH, 1.0, MLSTM_DH ** -0.5)
    _causal_conv_silu(raw[:, 0:conv_w], xe_s, hist_b, cwb_ref, cbb_ref, MLSTM_CONV, ob_ref, k_scale)
    o0, z0 = 3 * D_BRANCH, 4 * D_BRANCH
    ob_ref[:, conv_w:o0] = raw[:, conv_w:o0]
    ob_ref[:, o0:z0] = jax.nn.sigmoid(raw[:, o0:z0])
    ob_ref[:, z0:W_B] = _silu(raw[:, z0:W_B])

    raw = _dot(u, w_ref[:, COL_D:COL_D + W_D])
    conv_w = D_BRANCH + 2 * SSD_BC
    _causal_conv_silu(raw[:, 0:conv_w], xe_s, hist_d, cwd_ref, cbd_ref, SSD_CONV, od_ref)
    od_ref[:, conv_w:W_D] = _silu(raw[:, conv_w:W_D])
    u_s[(step + 1) % 2] = normed(xn_ref[...])

    raw = _dot(u, w_ref[:, COL_C:COL_C + W_C])
    lb = lb_ref[...]
    fr = raw[:, HGRN_QF:2 * HGRN_QF]
    oc_ref[:, 0:HGRN_QF] = raw[:, 0:HGRN_QF] * (HGRN_DK ** -0.5)
    oc_ref[:, HGRN_QF:2 * HGRN_QF] = (1.0 - lb) * jax.nn.sigmoid(-fr)
    z0 = 2 * HGRN_QF + D_BRANCH
    oc_ref[:, 2 * HGRN_QF:z0] = raw[:, 2 * HGRN_QF:z0]
    oc_ref[:, z0:W_C] = _silu(raw[:, z0:W_C])
    oc_ref[:, W_C:W_C + HGRN_QF] = jnp.log(jnp.maximum(lb + (1.0 - lb) * jax.nn.sigmoid(fr), 1e-30))

    gate = _dot_f32(small, gw_ref[...]) + gb_ref[...]
    oa_ref[:, W_A:W_A + GLA_QK] = _log_sigmoid(gate) * (1.0 / GLA_GATE_NORM)
    raw = _dot(u, w_ref[:, COL_A:COL_A + W_A])
    oa_ref[:, 0:GLA_QK] = raw[:, 0:GLA_QK] * (GLA_DK ** -0.5)
    z0 = 2 * GLA_QK + D_BRANCH
    oa_ref[:, GLA_QK:z0] = raw[:, GLA_QK:z0]
    oa_ref[:, z0:W_A] = _silu(raw[:, z0:W_A])


def _inproj(h, norm_w, w_wide, w_small, layer, consts, tile, tiles_per_seq):
    n_tok = h.shape[0]
    const = lambda i: (0, 0)
    row = lambda i: (i, 0)
    n_tiles = n_tok // tile
    in_specs = [pl.BlockSpec((tile, D_MODEL), row),
                pl.BlockSpec((tile, D_MODEL), lambda i: (jnp.minimum(i + 1, n_tiles - 1), 0)),
                pl.BlockSpec((1, D_MODEL), const),
                pl.BlockSpec((None, D_MODEL, COL_S), lambda i: (layer, 0, 0), pipeline_mode=pl.Buffered(1)),
                pl.BlockSpec((None, D_MODEL, LANES), lambda i: (layer, 0, 0))]
    in_specs += [pl.BlockSpec(c.shape, const) for c in consts]
    widths = (W_A + GLA_QK, W_B, W_C + HGRN_QF, W_D, LANES)
    conv_w = 2 * D_BRANCH
    return pl.pallas_call(
        functools.partial(_inproj_kernel, tiles_per_seq=tiles_per_seq),
        grid=(n_tiles,), in_specs=in_specs,
        out_specs=[pl.BlockSpec((tile, w), row) for w in widths],
        out_shape=[jax.ShapeDtypeStruct((n_tok, w), F32) for w in widths],
        scratch_shapes=[pltpu.VMEM((tile + HIST, conv_w), F32), pltpu.VMEM((HIST, conv_w), F32),
                        pltpu.VMEM((HIST, conv_w), F32), pltpu.VMEM((2, tile, D_MODEL), BF16)],
        compiler_params=pltpu.CompilerParams(dimension_semantics=("arbitrary",),
                                             vmem_limit_bytes=VMEM_LIMIT),
        name="inproj",
    )(h, h, norm_w, w_wide, w_small, *consts)


def _outproj_kernel(h_ref, ya_ref, yb_ref, yc_ref, yd_ref, w_ref, fw_ref, o_ref, *, final):
    acc = h_ref[...]
    for i, y_ref in enumerate((ya_ref, yb_ref, yc_ref, yd_ref)):
        acc = acc + _dot(y_ref[...], w_ref[i * D_BRANCH:(i + 1) * D_BRANCH, :])
    if final:
        ms = jnp.mean(acc * acc, axis=-1, keepdims=True)
        acc = acc * lax.rsqrt(ms + EPS) * fw_ref[...]
    o_ref[...] = acc


def _outproj(h, ys, w_out, final_w, final, tile):
    n_tok = h.shape[0]
    const = lambda i: (0, 0)
    row = lambda i: (i, 0)
    in_specs = [pl.BlockSpec((tile, D_MODEL), row)]
    in_specs += [pl.BlockSpec((tile, D_BRANCH), row) for _ in ys]
    in_specs += [pl.BlockSpec(w_out.shape, const), pl.BlockSpec((1, D_MODEL), const)]
    return pl.pallas_call(
        functools.partial(_outproj_kernel, final=final),
        grid=(n_tok // tile,), in_specs=in_specs,
        out_specs=pl.BlockSpec((tile, D_MODEL), row),
        out_shape=jax.ShapeDtypeStruct((n_tok, D_MODEL), F32),
        compiler_params=pltpu.CompilerParams(dimension_semantics=("arbitrary",),
                                             vmem_limit_bytes=VMEM_LIMIT),
        name="outproj",
    )(h, *ys, w_out, final_w)


def _norm_gate_store(y_ref, rows, col0, parts, nw_ref, z_parts):
    width = sum(p.shape[-1] for p in parts)
    ss = sum(jnp.sum(p * p, axis=-1, keepdims=True) for p in parts)
    scale = lax.rsqrt(ss * (1.0 / width) + EPS)
    c = col0
    for p, z in zip(parts, z_parts):
        w = p.shape[-1]
        out = p * scale * nw_ref[:, c:c + w]
        if z is not None:
            out = out * z
        y_ref[rows, c:c + w] = out.astype(y_ref.dtype)
        c += w


N_STAGES = 5
CHUNK_UNROLL = 1


def _run_chunks(work, tile):
    def body(c, carry):
        rows = pl.ds(pl.multiple_of(c * CHUNK, CHUNK), CHUNK)
        live = [(dict(d), stages) for d, stages in work]
        for k in range(N_STAGES):
            joined = []
            for d, stages in live:
                joint, fn = stages[k] if isinstance(stages[k], tuple) else (None, stages[k])
                if joint is not None and not any(stages is s for s in joined):
                    joint([dd for dd, st in live if st is stages])
                    joined.append(stages)
                fn(d, rows)
        return carry

    lax.fori_loop(0, tile // CHUNK, body, 0, unroll=CHUNK_UNROLL)


def _gla_stages(ms_ref, mk_ref):
    lane = _lane_iota()

    def lane_mask(d, j):
        dk = d["dk"]
        return None if dk == LANES else (lane >= j * dk) & (lane < (j + 1) * dk)

    def pick(a, lm):
        return a if lm is None else jnp.where(lm, a, jnp.zeros_like(a))

    def n_groups(d):
        return d["q"].shape[1] // LANES

    def cumsums(d, rows):
        d["es"] = [_sel_dot2(ms_ref[...], d["lg"][rows, p * 2 * LANES:(p + 1) * 2 * LANES])
                   for p in range(n_groups(d) // 2)]

    def operands(d, rows):
        ops = []
        for g in range(n_groups(d)):
            ls = slice(g * LANES, (g + 1) * LANES)
            e = d["es"][g // 2][:, (g % 2) * LANES:(g % 2 + 1) * LANES]
            q = d["q"][rows, ls]
            k = d["k"][rows, ls]
            gcs = e[0:CHUNK]
            qg = (q * jnp.exp(gcs)).astype(BF16)
            kd = (k * jnp.exp(e[CHUNK:2 * CHUNK])).astype(BF16)
            dec = jnp.exp(gcs[CHUNK - 1:CHUNK, :])
            qb = q.astype(BF16)
            kb = k.astype(BF16)
            ql, kl = [qb], [kb]
            mxu_block = 2
            for l in range(N_LEVELS):
                s = CHUNK >> (l + 1)
                if s >= MIN_VPU_LEVEL:
                    el = _level_exponent(gcs, s)
                else:
                    el = e[mxu_block * CHUNK:(mxu_block + 1) * CHUNK]
                    mxu_block += 1
                w = jnp.exp(el).astype(BF16)
                ql.append(qb * w)
                kl.append(kb * w)
            ops.append((qg, kd, dec, ql, kl))
        d["ops"] = ops

    def scores(d, rows):
        heads = []
        for g in range(n_groups(d)):
            ql, kl = d["ops"][g][3], d["ops"][g][4]
            for j in range(LANES // d["dk"]):
                lm = lane_mask(d, j)
                a = _dot_nt(pick(ql[0], lm), kl[0]) * mk_ref[0][:, 0:CHUNK]
                for l in range(1, N_LEVELS + 1):
                    a = a + _dot_nt(pick(ql[l], lm), kl[l]) * mk_ref[l][:, 0:CHUNK]
                heads.append((g, j, a.astype(BF16)))
        d["heads"] = heads

    def outputs(d, rows):
        hp = LANES // d["dk"]
        outs, upds = [], [None] * n_groups(d)
        sts = [d["st"][g] for g in range(n_groups(d))]
        stbs = [st.astype(BF16) for st in sts]
        for g, j, ab in d["heads"]:
            h = g * hp + j
            lm = lane_mask(d, j)
            qg, kd = d["ops"][g][0], d["ops"][g][1]
            vh = d["v"][rows, h * LANES:(h + 1) * LANES].astype(BF16)
            outs.append(_dot(ab, vh) + _dot_nt(pick(qg, lm), stbs[g]))
            u = _dot_tn(vh, kd)
            upds[g] = u if upds[g] is None else jnp.where(lm, u, upds[g])
        d["outs"], d["upds"], d["sts"] = outs, upds, sts

    def finish(d, rows):
        for g in range(n_groups(d)):
            d["st"][g] = d["sts"][g] * d["ops"][g][2] + d["upds"][g]
        for h, o in enumerate(d["outs"]):
            _norm_gate_store(d["y"], rows, h * LANES, [o], d["nw"],
                             [d["z"][rows, h * LANES:(h + 1) * LANES]])

    return cumsums, operands, scores, outputs, finish


def _gla_stream(p_ref, s, qk, dk, y_ref, st_s, nw_ref):
    v0, z0, lg0 = 2 * qk, 2 * qk + D_BRANCH, 2 * qk + 2 * D_BRANCH
    return dict(q=p_ref.at[s, :, 0:qk], k=p_ref.at[s, :, qk:v0], v=p_ref.at[s, :, v0:z0],
                z=p_ref.at[s, :, z0:lg0], lg=p_ref.at[s, :, lg0:lg0 + qk],
                y=y_ref.at[s], st=st_s.at[s], nw=nw_ref, dk=dk)


def _mlstm_constants():
    half = LANES // 2
    full0 = half * MLSTM_HEADS
    e = np.zeros((LANES, full0 + 2 * LANES * MLSTM_HEADS), np.float32)
    diff0 = full0 + LANES * MLSTM_HEADS
    sel = np.zeros((16, LANES), np.float32)
    for h in range(MLSTM_HEADS):
        e[SM_F + h, h * half:(h + 1) * half] = 1.0
        e[SM_F + h, full0 + h * LANES:full0 + (h + 1) * LANES] = 1.0
        e[SM_I + h, diff0 + h * LANES:diff0 + (h + 1) * LANES] = 1.0
        e[SM_F + h, diff0 + h * LANES:diff0 + (h + 1) * LANES] = -1.0
        sel[h // 2, SM_I + h] = 1.0
        sel[h // 2, SM_F + h] = -1.0
    return (jnp.asarray(np.concatenate([e, e], axis=0), BF16),
            jnp.asarray(np.concatenate([sel, sel], axis=1), BF16))


def _mlstm_stages(nw_ref, tri_ref, exp_ref, sel_ref):
    lane = _lane_iota()
    is_f = (lane >= SM_F) & (lane < SM_F + MLSTM_HEADS)

    half = LANES // 2
    ri = lax.broadcasted_iota(jnp.int32, (CHUNK, LANES), 0)
    ci = lax.broadcasted_iota(jnp.int32, (CHUNK, LANES), 1)
    causal2 = (ci & (half - 1)) <= ri
    lo_half = lane < half
    even = (lane & 1) == 0
    v0, o0, z0 = 2 * D_BRANCH, 3 * D_BRANCH, 4 * D_BRANCH
    heads = range(MLSTM_HEADS)
    pairs = range(MLSTM_HEADS // 2)
    full0 = half * MLSTM_HEADS
    diff0 = full0 + LANES * MLSTM_HEADS

    def qk_products(d, rows):
        qk_ref = d["p"]
        qbs = [qk_ref[rows, h * LANES:(h + 1) * LANES].astype(BF16) for h in heads]
        d["ks"] = [qk_ref[rows, D_BRANCH + h * LANES:D_BRANCH + (h + 1) * LANES] for h in heads]
        kbs = [k.astype(BF16) for k in d["ks"]]
        zero = jnp.zeros((CHUNK, LANES), BF16)
        ones = jnp.ones((CHUNK, LANES), BF16)
        d["qk_raw"] = [_dot_nt(jnp.concatenate([qbs[2 * p], qbs[2 * p + 1]], axis=1),
                               _side_by_side(kbs[2 * p], kbs[2 * p + 1], zero)) for p in pairs]
        d["qc"] = [_dot(qbs[h], d["c"][h].astype(BF16)) for h in heads]
        d["vaug"] = [jnp.concatenate([d["p"][rows, v0 + h * LANES:v0 + (h + 1) * LANES].astype(BF16),
                                      ones], axis=1) for h in heads]

    def gate_sums(d, rows):
        gc = d["g"][rows, :]
        bcol = _sel_dot2(tri_ref[...], gc)
        ib = jnp.where(is_f, bcol, gc)
        d["ib"] = ib
        by_parity = jnp.concatenate([jnp.where(even, ib, 0.0), jnp.where(even, 0.0, ib)], axis=0)
        d["drows"] = _sel_dot2_nt(sel_ref[...], by_parity)

    def spread_gates(ds):
        ex = _dot_sel2(jnp.concatenate([d["ib"] for d in ds], axis=0), exp_ref[...])
        for i, d in enumerate(ds):
            d["ex"] = ex[i * CHUNK:(i + 1) * CHUNK]

    def weights(d, rows):
        ex = d["ex"]
        mx = d["m"][0:1, :]
        ss, mrs = [], []
        for p in pairs:
            bx = ex[:, p * LANES:(p + 1) * LANES]
            lw = jnp.where(causal2, bx + d["drows"][p:p + 1, :], NEG_BIG)
            mr0 = jnp.max(jnp.where(lo_half, lw, NEG_BIG), axis=-1, keepdims=True)
            mr1 = jnp.max(jnp.where(lo_half, NEG_BIG, lw), axis=-1, keepdims=True)
            m64 = jnp.where(lo_half, mx[:, 2 * p * LANES:(2 * p + 1) * LANES],
                            mx[:, (2 * p + 1) * LANES:(2 * p + 2) * LANES])
            m_row = jnp.maximum(jnp.where(lo_half, mr0, mr1), bx + m64)
            ss.append((d["qk_raw"][p] * jnp.exp(lw - m_row)).astype(BF16))
            mrs += [mr0, mr1]
        d["ss"], d["mrs"] = ss, mrs
        b_last = ex[CHUNK - 1:CHUNK, full0:diff0]
        lwe = ex[:, diff0:] + b_last
        m_new = jnp.maximum(b_last + mx, jnp.max(lwe, axis=0, keepdims=True))
        d["cd"] = jnp.exp(b_last + mx - m_new)
        d["m_new"] = m_new
        kw = jnp.exp(lwe - m_new)
        d["kws"] = [(d["ks"][h] * kw[:, h * LANES:(h + 1) * LANES]).astype(BF16) for h in heads]

    def numerators(d, rows):
        zero = jnp.zeros((CHUNK, 2 * LANES), BF16)
        d["nums"] = [_dot(d["ss"][p], _side_by_side(d["vaug"][2 * p], d["vaug"][2 * p + 1], zero))
                     for p in pairs]
        d["cups"] = [_dot_tn(d["kws"][h], d["vaug"][h]) for h in heads]

    def finish(d, rows):
        ex = d["ex"]
        mx = d["m"][0:1, :]
        for h in heads:
            p, hd = divmod(h, 2)
            hs = slice(h * LANES, (h + 1) * LANES)
            m_inter = ex[:, full0 + h * LANES:full0 + (h + 1) * LANES] + mx[:, hs]
            m_row = jnp.maximum(d["mrs"][h], m_inter)
            inter = jnp.exp(m_inter - m_row)
            sv = d["nums"][p][:, hd * 2 * LANES:(hd + 1) * 2 * LANES]
            num = sv[:, 0:LANES] + inter * d["qc"][h][:, 0:LANES]
            den = sv[:, LANES:] + inter * d["qc"][h][:, LANES:]
            hh = num / jnp.maximum(jnp.abs(den), jnp.exp(-m_row))
            cd = d["cd"][:, hs]
            d["c"][h] = jnp.concatenate([cd, cd], axis=1) * d["c"][h] + d["cups"][h]
            og = d["p"][rows, o0 + h * LANES:o0 + (h + 1) * LANES]
            _norm_gate_store(d["y"], rows, h * LANES, [og * hh], nw_ref,
                             [d["p"][rows, z0 + h * LANES:z0 + (h + 1) * LANES]])
        d["m"][0:1, :] = d["m_new"]

    return qk_products, gate_sums, (spread_gates, weights), numerators, finish


def _ssd_constants():
    e = np.zeros((LANES, SSD_HEADS * SSD_HEAD_DIM), np.float32)
    sel = np.zeros((16, LANES), np.float32)
    for h in range(SSD_HEADS):
        e[SM_DT + h, h * SSD_HEAD_DIM:(h + 1) * SSD_HEAD_DIM] = 1.0
        sel[h // 2, SM_DT + h] = 1.0
    return (jnp.asarray(np.concatenate([e, e], axis=0), BF16),
            jnp.asarray(np.concatenate([sel, sel], axis=1), BF16))


def _ssd_stages(alog_ref, dx_ref, nw_ref, tri_ref, sel_ref, exp_ref):
    lane = _lane_iota()
    is_dt = (lane >= SM_DT) & (lane < SM_DT + SSD_HEADS)

    half = LANES // 2
    ri = lax.broadcasted_iota(jnp.int32, (CHUNK, LANES), 0)
    ci = lax.broadcasted_iota(jnp.int32, (CHUNK, LANES), 1)
    causal2 = (ci & (half - 1)) <= ri
    lo_half = lane < half
    even = (lane & 1) == 0
    a_lane = jnp.where(is_dt, -jnp.exp(alog_ref[...]), 0.0)
    b0, c0, z0 = D_BRANCH, D_BRANCH + SSD_BC, D_BRANCH + 2 * SSD_BC
    groups = range(SSD_GROUPS)
    group_w = D_BRANCH // SSD_GROUPS
    pairs = range(D_BRANCH // LANES)
    pairs_per_group = group_w // LANES

    def products(d, rows):
        xbc = d["p"]
        d["bgs"] = [xbc[rows, b0 + g * SSD_STATE:b0 + (g + 1) * SSD_STATE].astype(BF16) for g in groups]
        cgbs = [xbc[rows, c0 + g * SSD_STATE:c0 + (g + 1) * SSD_STATE].astype(BF16) for g in groups]
        d["cb2"] = [_dot_nt(cgbs[g], jnp.concatenate([d["bgs"][g], d["bgs"][g]], axis=0))
                    for g in groups]
        d["cst"] = [_dot(cgbs[g], d["st"][g].astype(BF16)) for g in groups]

    def decay_sums(d, rows):
        dt = d["dt"][rows, :]
        acs = _sel_dot2(tri_ref[...], dt * a_lane)
        d["dt_acs"] = jnp.concatenate([dt, acs], axis=0)
        by_parity = jnp.concatenate([jnp.where(even, acs, 0.0), jnp.where(even, 0.0, acs)], axis=0)
        d["a_rows"] = _sel_dot2_nt(sel_ref[...], by_parity)

    def spread_decays(ds):
        ex = _dot_sel2(jnp.concatenate([d["dt_acs"] for d in ds], axis=0), exp_ref[...])
        for i, d in enumerate(ds):
            d["dtx"] = ex[2 * i * CHUNK:(2 * i + 1) * CHUNK]
            d["acs_x"] = ex[(2 * i + 1) * CHUNK:(2 * i + 2) * CHUNK]

    def decays(d, rows):
        d["xss"], d["ms"], d["xblk"], xdecs = [], [], [], []
        for p in pairs:
            ls = slice(p * LANES, (p + 1) * LANES)
            ax = d["acs_x"][:, ls]
            lmat = jnp.exp(jnp.where(causal2, ax - d["a_rows"][p:p + 1, :], NEG_BIG))
            d["ms"].append((d["cb2"][p // pairs_per_group] * lmat).astype(BF16))
            xs = d["p"][rows, ls]
            xdt = xs * d["dtx"][:, ls]
            d["xss"].append(xs)
            d["xblk"].append(jnp.concatenate([jnp.where(lo_half, xdt, 0.0), jnp.where(lo_half, 0.0, xdt)],
                                             axis=0).astype(BF16))
            xdecs.append((xdt * jnp.exp(ax[CHUNK - 1:CHUNK, :] - ax)).astype(BF16))
        d["xdec"] = [jnp.concatenate(xdecs[g * pairs_per_group:(g + 1) * pairs_per_group], axis=1)
                     for g in groups]

    def chunk_products(d, rows):
        d["yds"] = [_dot(d["ms"][p], d["xblk"][p]) for p in pairs]
        d["ups"] = [_dot_tn(d["bgs"][g], d["xdec"][g]) for g in groups]

    def finish(d, rows):
        for g in groups:
            gs = slice(g * group_w, (g + 1) * group_w)
            eax = jnp.exp(d["acs_x"][:, gs])
            d["st"][g] = d["st"][g] * eax[CHUNK - 1:CHUNK, :] + d["ups"][g]
            ys = []
            for pp in range(pairs_per_group):
                p = g * pairs_per_group + pp
                ls = slice(p * LANES, (p + 1) * LANES)
                y = (d["cst"][g][:, pp * LANES:(pp + 1) * LANES] * eax[:, pp * LANES:(pp + 1) * LANES]
                     + dx_ref[:, ls] * d["xss"][p] + d["yds"][p])
                ys.append(y * d["p"][rows, z0 + p * LANES:z0 + (p + 1) * LANES])
            _norm_gate_store(d["y"], rows, g * group_w, ys, nw_ref, [None] * len(ys))

    return products, decay_sums, (spread_decays, decays), chunk_products, finish


def _mixers_kernel(pa_ref, pb_ref, pc_ref, pd_ref, sm_ref, nwa_ref, nwb_ref, nwc_ref, nwd_ref,
                   ms_ref, mk_ref, tri_ref, mexp_ref, msel_ref, alog_ref, dx_ref, dsel_ref, dexp_ref,
                   ya_ref, yb_ref, yc_ref, yd_ref, sta_s, stc_s, cb_s, mb_s, std_s, *, tile):
    @pl.when(pl.program_id(0) == 0)
    def _():
        for r in (sta_s, stc_s, cb_s, mb_s, std_s):
            r[...] = jnp.zeros_like(r)

    gla = _gla_stages(ms_ref, mk_ref)
    mlstm = _mlstm_stages(nwb_ref, tri_ref, mexp_ref, msel_ref)
    ssd = _ssd_stages(alog_ref, dx_ref, nwd_ref, tri_ref, dsel_ref, dexp_ref)
    work = []
    for s in range(pa_ref.shape[0]):
        work.append((_gla_stream(pa_ref, s, GLA_QK, GLA_DK, ya_ref, sta_s, nwa_ref), gla))
        work.append((dict(p=pb_ref.at[s], g=sm_ref.at[s], c=cb_s.at[s], m=mb_s.at[s], y=yb_ref.at[s]),
                     mlstm))
        work.append((_gla_stream(pc_ref, s, HGRN_QF, HGRN_DK, yc_ref, stc_s, nwc_ref), gla))
        work.append((dict(p=pd_ref.at[s], dt=sm_ref.at[s], st=std_s.at[s], y=yd_ref.at[s]), ssd))
    _run_chunks(work, tile)


def _mixers(tok_inputs, const_inputs, n_batch, seq, tile):
    tok = lambda t: (0, t, 0)
    in_specs = [pl.BlockSpec((n_batch, tile, a.shape[2]), tok) for a in tok_inputs]
    for a in const_inputs:
        in_specs.append(pl.BlockSpec(a.shape, lambda t, nd=a.ndim: (0,) * nd))
    scratch = [pltpu.VMEM((n_batch, GLA_QK // LANES, GLA_DV, LANES), F32),
               pltpu.VMEM((n_batch, HGRN_QF // LANES, HGRN_DV, LANES), F32),
               pltpu.VMEM((n_batch, MLSTM_HEADS, MLSTM_DH, 2 * MLSTM_DH), F32),
               pltpu.VMEM((n_batch, 8, MLSTM_HEADS * LANES), F32),
               pltpu.VMEM((n_batch, SSD_GROUPS, SSD_STATE, D_BRANCH // SSD_GROUPS), F32)]
    return pl.pallas_call(
        functools.partial(_mixers_kernel, tile=tile), grid=(seq // tile,), in_specs=in_specs,
        out_specs=[pl.BlockSpec((n_batch, tile, D_BRANCH), tok)] * 4,
        out_shape=[jax.ShapeDtypeStruct((n_batch, seq, D_BRANCH), BF16)] * 4,
        scratch_shapes=scratch,
        compiler_params=pltpu.CompilerParams(dimension_semantics=("arbitrary",),
                                             vmem_limit_bytes=VMEM_LIMIT),
        name="mixers",
    )(*tok_inputs, *const_inputs)


def _pad_lanes(parts, total=LANES):
    width = sum(p.shape[-1] for p in parts)
    lead = parts[0].shape[:-1]
    return jnp.concatenate(list(parts) + [jnp.zeros(lead + (total - width,), parts[0].dtype)], axis=-1)


def _small_vector(i_part, f_part, dt_part):
    z = jnp.zeros((GLA_GATE_RANK,), F32)
    return _pad_lanes([z, i_part.astype(F32), f_part.astype(F32), dt_part.astype(F32)])[None, :]


def kernel(x, norm_w, w_in, gla_gate_w, gla_gate_b, gla_norm_w, ml_conv_w, ml_conv_b, ml_i_b, ml_f_b,
           ml_norm_w, hg_lb_logits, hg_norm_w, ssd_conv_w, ssd_conv_b, ssd_dt_bias, ssd_A_log, ssd_D,
           ssd_norm_w, w_out, final_norm_w):
    n_batch, seq, _ = x.shape
    depth = w_in.shape[0]
    tile = min(MIX_TILE, seq)
    n_tok = n_batch * seq

    mstack, masks = _gla_constants()
    tri_c = _tri2(CHUNK)
    ml_exp, ml_sel = _mlstm_constants()
    ssd_exp, ssd_sel = _ssd_constants()
    zero4 = jnp.zeros((MLSTM_HEADS,), F32)
    in_tile = min(IN_TILE, seq)

    p = jax.nn.softmax(hg_lb_logits.astype(F32), axis=0)
    lower_bounds = jnp.cumsum(p, axis=0) - p[0:1]

    w_wide, w_small = _realign(jnp.swapaxes(w_in, 1, 2), REALIGN_ROWS)

    h = x.reshape(n_tok, D_MODEL)
    row2 = lambda v: v.astype(F32).reshape(1, -1)
    for l in range(depth):
        gate_w = jnp.concatenate(
            [gla_gate_w[l].astype(F32), jnp.zeros((LANES - GLA_GATE_RANK, GLA_QK), F32)], axis=0)
        consts = [gate_w, row2(gla_gate_b[l]), row2(lower_bounds[l]),
                  ml_conv_w[l].astype(F32), row2(ml_conv_b[l]),
                  ssd_conv_w[l].astype(F32), row2(ssd_conv_b[l]),
                  _small_vector(ml_i_b[l], ml_f_b[l], ssd_dt_bias[l])]
        pa, pb, pc, pd, sm = [p.reshape(n_batch, seq, -1)
                              for p in _inproj(h, row2(norm_w[l]), w_wide, w_small, l, consts, in_tile,
                                               seq // in_tile)]

        ys = _mixers(
            [pa, pb, pc, pd, sm],
            [row2(gla_norm_w[l]), row2(ml_norm_w[l]), row2(hg_norm_w[l]), row2(ssd_norm_w[l]),
             mstack, masks, tri_c, ml_exp, ml_sel, _small_vector(zero4, zero4, ssd_A_log[l]),
             row2(jnp.repeat(ssd_D[l].astype(F32), SSD_HEAD_DIM)), ssd_sel, ssd_exp],
            n_batch, seq, tile)
        ys = [y.reshape(n_tok, D_BRANCH) for y in ys]
        h = _outproj(h, ys, w_out[l].astype(BF16), row2(final_norm_w), l == depth - 1,
                     min(OUT_TILE, n_tok))
    return h.reshape(n_batch, seq, D_MODEL)
```

```python
import functools

import numpy as np
import jax
import jax.numpy as jnp
from jax import lax
from jax.experimental import pallas as pl
from jax.experimental.pallas import tpu as pltpu

F32 = jnp.float32
BF16 = jnp.bfloat16

D_MODEL = 1024
D_BRANCH = 512
EPS = 1e-6
NEG_BIG = -1e30

GLA_HEADS, GLA_DK, GLA_DV = 4, 64, 128
GLA_GATE_RANK, GLA_GATE_NORM = 16, 16.0
MLSTM_HEADS, MLSTM_DH, MLSTM_CONV = 4, 128, 4
HGRN_HEADS, HGRN_DK, HGRN_DV = 4, 128, 128
SSD_HEAD_DIM, SSD_HEADS, SSD_GROUPS, SSD_STATE, SSD_CONV = 64, 8, 2, 128, 4
GLA_QK = GLA_HEADS * GLA_DK
HGRN_QF = HGRN_HEADS * HGRN_DK
SSD_BC = SSD_GROUPS * SSD_STATE
PROJ_SIZES = (
    GLA_QK, GLA_QK, D_BRANCH, GLA_GATE_RANK, D_BRANCH,
    D_BRANCH, D_BRANCH, D_BRANCH, MLSTM_HEADS, MLSTM_HEADS, D_BRANCH, D_BRANCH,
    HGRN_QF, HGRN_QF, D_BRANCH, D_BRANCH,
    D_BRANCH, SSD_BC, SSD_BC, SSD_HEADS, D_BRANCH,
)

LANES = 128
HIST = 8
VMEM_LIMIT = 56 * 1024 * 1024

REALIGN_ROWS = 256
IN_TILE = 256
MIX_TILE = 256
OUT_TILE = 512

CHUNK = 64
N_LEVELS = 6
MIN_VPU_LEVEL = 4

SM_GR, SM_I, SM_F, SM_DT = 0, 16, 20, 24

W_A = 2 * GLA_QK + 2 * D_BRANCH
W_B = 5 * D_BRANCH
W_C = 2 * HGRN_QF + 2 * D_BRANCH
W_D = 2 * D_BRANCH + 2 * SSD_BC
COL_A = 0
COL_B = COL_A + W_A
COL_C = COL_B + W_B
COL_D = COL_C + W_C
COL_S = COL_D + W_D


def _gla_constants():
    c = CHUNK
    t = np.arange(c)[:, None]
    d = np.arange(c)[None, :]
    blocks = [(d <= t), (d > t)]
    masks = [np.eye(c, dtype=bool)]
    for l in range(N_LEVELS):
        s = c >> (l + 1)
        mid_t = (t // (2 * s)) * (2 * s) + s
        upper = t >= mid_t
        if s < MIN_VPU_LEVEL:
            blocks.append(np.where(upper, (d >= mid_t) & (d <= t), (d > t) & (d <= mid_t - 1)))
        same = (t // (2 * s)) == (d // (2 * s))
        masks.append(same & upper & (d < mid_t))
    mstack = np.concatenate(blocks, axis=0).astype(np.float32)
    mstack = np.concatenate([mstack, mstack], axis=1)
    masks = np.stack(masks).astype(np.float32)
    return jnp.asarray(mstack, BF16), jnp.asarray(np.concatenate([masks, masks], axis=-1))


def _tri2(n):
    tri = np.tril(np.ones((n, n), np.float32))
    return jnp.asarray(np.concatenate([tri, tri], axis=1), BF16)


def _dot(a, b):
    return jnp.dot(a, b, preferred_element_type=F32)


def _dot_nt(a, b):
    return lax.dot_general(a, b, (((1,), (1,)), ((), ())), preferred_element_type=F32)


def _dot_tn(a, b):
    return lax.dot_general(a, b, (((0,), (0,)), ((), ())), preferred_element_type=F32)


def _sel_dot2(sel2, x):
    hi = x.astype(BF16)
    mid = (x - hi.astype(F32)).astype(BF16)
    return _dot(sel2, jnp.concatenate([hi, mid], axis=0))


def _sel_dot2_nt(sel2, x):
    hi = x.astype(BF16)
    mid = (x - hi.astype(F32)).astype(BF16)
    return _dot_nt(sel2, jnp.concatenate([hi, mid], axis=1))


def _level_exponent(gcs, s):
    pieces = []
    for b in range(0, CHUNK, 2 * s):
        ref_row = gcs[b + s - 1:b + s, :]
        if s >= 8:
            pieces += [ref_row - gcs[b:b + s], gcs[b + s:b + 2 * s] - ref_row]
        else:
            diff = gcs[b:b + 2 * s] - ref_row
            pieces.append(jnp.minimum(diff, -diff))
    return jnp.concatenate(pieces, axis=0)


def _dot_sel2(x, sel2):
    hi = x.astype(BF16)
    mid = (x - hi.astype(F32)).astype(BF16)
    return _dot(jnp.concatenate([hi, mid], axis=1), sel2)


def _dot_f32(a, b):
    ah = a.astype(BF16)
    al = (a - ah.astype(F32)).astype(BF16)
    bh = b.astype(BF16)
    bl = (b - bh.astype(F32)).astype(BF16)
    return _dot(ah, bh) + _dot(al, bh) + _dot(ah, bl)


def _softplus(x):
    return jnp.maximum(x, 0.0) + jnp.log1p(jnp.exp(-jnp.abs(x)))


def _log_sigmoid(x):
    return -_softplus(-x)


def _silu(x):
    return x * jax.nn.sigmoid(x)


def _lane_iota():
    return lax.broadcasted_iota(jnp.int32, (1, LANES), 1)


def _side_by_side(a, b, zero):
    return jnp.concatenate([jnp.concatenate([a, zero], axis=1),
                            jnp.concatenate([zero, b], axis=1)], axis=0)


def _wide_segments():
    segs, src, dst = [], 0, 0
    run_start = None
    for size in PROJ_SIZES + (0,):
        wide = size >= LANES
        if wide and run_start is None:
            run_start = src
        if not wide and run_start is not None:
            segs.append((run_start, dst, src - run_start))
            dst += src - run_start
            run_start = None
        src += size
    return segs


def _realign_kernel(wt_ref, o_ref, os_ref):
    for src, dst, n in _wide_segments():
        for r in range(0, n, LANES):
            o_ref[:, dst + r:dst + r + LANES] = wt_ref[src + r:src + r + LANES, :].T.astype(BF16)
    narrow, src = [], 0
    for size in PROJ_SIZES:
        if size < LANES:
            narrow.append(wt_ref[src:src + size, :])
        src += size
    used = sum(p.shape[0] for p in narrow)
    narrow.append(jnp.zeros((LANES - used, wt_ref.shape[1]), F32))
    os_ref[...] = jnp.concatenate(narrow, axis=0).T


def _realign(wt, rows):
    depth, n_proj, d_model = wt.shape
    return pl.pallas_call(
        _realign_kernel, grid=(depth, d_model // rows),
        in_specs=[pl.BlockSpec((None, n_proj, rows), lambda l, j: (l, 0, j))],
        out_specs=[pl.BlockSpec((None, rows, COL_S), lambda l, j: (l, j, 0)),
                   pl.BlockSpec((None, rows, LANES), lambda l, j: (l, j, 0))],
        out_shape=[jax.ShapeDtypeStruct((depth, d_model, COL_S), BF16),
                   jax.ShapeDtypeStruct((depth, d_model, LANES), F32)],
        compiler_params=pltpu.CompilerParams(dimension_semantics=("arbitrary", "arbitrary"),
                                             vmem_limit_bytes=VMEM_LIMIT),
        name="realign",
    )(wt)


CONV_ROWS, CONV_LANES = 256, 128


def _causal_conv_silu(raw, xe_ref, hist_ref, w_ref, b_ref, taps, out_ref, scale=None):
    rows, n = raw.shape
    xe_ref[0:HIST, :] = hist_ref[...]
    xe_ref[HIST:HIST + rows, :] = raw
    hist_ref[...] = raw[rows - HIST:rows, :]
    for r in range(0, rows, CONV_ROWS):
        for l in range(0, n, CONV_LANES):
            cs = slice(l, l + CONV_LANES)
            acc = b_ref[:, cs] + w_ref[taps - 1:taps, cs] * xe_ref[HIST + r:HIST + r + CONV_ROWS, cs]
            for k in range(taps - 1):
                off = HIST - (taps - 1) + k + r
                acc = acc + w_ref[k:k + 1, cs] * xe_ref[off:off + CONV_ROWS, cs]
            out = _silu(acc)
            out_ref[r:r + CONV_ROWS, cs] = out if scale is None else out * scale[:, cs]


def _inproj_kernel(x_ref, xn_ref, nw_ref, w_ref, ws_ref, gw_ref, gb_ref, lb_ref, cwb_ref, cbb_ref, cwd_ref,
                   cbd_ref, gbias_ref, oa_ref, ob_ref, oc_ref, od_ref, os_ref, xe_s, hist_b, hist_d, u_s,
                   *, tiles_per_seq):
    step = pl.program_id(0)

    @pl.when(step % tiles_per_seq == 0)
    def _():
        hist_b[...] = jnp.zeros_like(hist_b)
        hist_d[...] = jnp.zeros_like(hist_d)

    def normed(x):
        ms = jnp.mean(x * x, axis=-1, keepdims=True)
        return (x * lax.rsqrt(ms + EPS) * nw_ref[...]).astype(BF16)

    @pl.when(step == 0)
    def _():
        u_s[0] = normed(x_ref[...])

    u = u_s[step % 2]
    lane = _lane_iota()

    small = _dot(u, ws_ref[...].astype(BF16))
    biased = small + gbias_ref[...]
    is_f = (lane >= SM_F) & (lane < SM_F + MLSTM_HEADS)
    is_dt = (lane >= SM_DT) & (lane < SM_DT + SSD_HEADS)
    os_ref[...] = jnp.where(is_f, _log_sigmoid(biased), jnp.where(is_dt, _softplus(biased), biased))

    raw = _dot(u, w_ref[:, COL_B:COL_B + W_B])
    conv_w = 2 * D_BRANCH
    k_scale = jnp.where(lax.broadcasted_iota(jnp.int32, (1, conv_w), 1) < D_BRANCH, 1.0, MLSTM_DH ** -0.5)
    _causal_conv_silu(raw[:, 0:conv_w], xe_s, hist_b, cwb_ref, cbb_ref, MLSTM_CONV, ob_ref, k_scale)
    o0, z0 = 3 * D_BRANCH, 4 * D_BRANCH
    ob_ref[:, conv_w:o0] = raw[:, conv_w:o0]
    ob_ref[:, o0:z0] = jax.nn.sigmoid(raw[:, o0:z0])
    ob_ref[:, z0:W_B] = _silu(raw[:, z0:W_B])

    raw = _dot(u, w_ref[:, COL_D:COL_D + W_D])
    conv_w = D_BRANCH + 2 * SSD_BC
    _causal_conv_silu(raw[:, 0:conv_w], xe_s, hist_d, cwd_ref, cbd_ref, SSD_CONV, od_ref)
    od_ref[:, conv_w:W_D] = _silu(raw[:, conv_w:W_D])
    u_s[(step + 1) % 2] = normed(xn_ref[...])

    raw = _dot(u, w_ref[:, COL_C:COL_C + W_C])
    lb = lb_ref[...]
    fr = raw[:, HGRN_QF:2 * HGRN_QF]
    oc_ref[:, 0:HGRN_QF] = raw[:, 0:HGRN_QF] * (HGRN_DK ** -0.5)
    oc_ref[:, HGRN_QF:2 * HGRN_QF] = (1.0 - lb) * jax.nn.sigmoid(-fr)
    z0 = 2 * HGRN_QF + D_BRANCH
    oc_ref[:, 2 * HGRN_QF:z0] = raw[:, 2 * HGRN_QF:z0]
    oc_ref[:, z0:W_C] = _silu(raw[:, z0:W_C])
    oc_ref[:, W_C:W_C + HGRN_QF] = jnp.log(jnp.maximum(lb + (1.0 - lb) * jax.nn.sigmoid(fr), 1e-30))

    gate = _dot_f32(small, gw_ref[...]) + gb_ref[...]
    oa_ref[:, W_A:W_A + GLA_QK] = _log_sigmoid(gate) * (1.0 / GLA_GATE_NORM)
    raw = _dot(u, w_ref[:, COL_A:COL_A + W_A])
    oa_ref[:, 0:GLA_QK] = raw[:, 0:GLA_QK] * (GLA_DK ** -0.5)
    z0 = 2 * GLA_QK + D_BRANCH
    oa_ref[:, GLA_QK:z0] = raw[:, GLA_QK:z0]
    oa_ref[:, z0:W_A] = _silu(raw[:, z0:W_A])


def _inproj(h, norm_w, w_wide, w_small, layer, consts, tile, tiles_per_seq):
    n_tok = h.shape[0]
    const = lambda i: (0, 0)
    row = lambda i: (i, 0)
    n_tiles = n_tok // tile
    in_specs = [pl.BlockSpec((tile, D_MODEL), row),
                pl.BlockSpec((tile, D_MODEL), lambda i: (jnp.minimum(i + 1, n_tiles - 1), 0)),
                pl.BlockSpec((1, D_MODEL), const),
                pl.BlockSpec((None, D_MODEL, COL_S), lambda i: (layer, 0, 0), pipeline_mode=pl.Buffered(1)),
                pl.BlockSpec((None, D_MODEL, LANES), lambda i: (layer, 0, 0))]
    in_specs += [pl.BlockSpec(c.shape, const) for c in consts]
    widths = (W_A + GLA_QK, W_B, W_C + HGRN_QF, W_D, LANES)
    conv_w = 2 * D_BRANCH
    return pl.pallas_call(
        functools.partial(_inproj_kernel, tiles_per_seq=tiles_per_seq),
        grid=(n_tiles,), in_specs=in_specs,
        out_specs=[pl.BlockSpec((tile, w), row) for w in widths],
        out_shape=[jax.ShapeDtypeStruct((n_tok, w), F32) for w in widths],
        scratch_shapes=[pltpu.VMEM((tile + HIST, conv_w), F32), pltpu.VMEM((HIST, conv_w), F32),
                        pltpu.VMEM((HIST, conv_w), F32), pltpu.VMEM((2, tile, D_MODEL), BF16)],
        compiler_params=pltpu.CompilerParams(dimension_semantics=("arbitrary",),
                                             vmem_limit_bytes=VMEM_LIMIT),
        name="inproj",
    )(h, h, norm_w, w_wide, w_small, *consts)


def _outproj_kernel(h_ref, ya_ref, yb_ref, yc_ref, yd_ref, w_ref, fw_ref, o_ref, *, final):
    acc = h_ref[...]
    for i, y_ref in enumerate((ya_ref, yb_ref, yc_ref, yd_ref)):
        acc = acc + _dot(y_ref[...], w_ref[i * D_BRANCH:(i + 1) * D_BRANCH, :])
    if final:
        ms = jnp.mean(acc * acc, axis=-1, keepdims=True)
        acc = acc * lax.rsqrt(ms + EPS) * fw_ref[...]
    o_ref[...] = acc


def _outproj(h, ys, w_out, final_w, final, tile):
    n_tok = h.shape[0]
    const = lambda i: (0, 0)
    row = lambda i: (i, 0)
    in_specs = [pl.BlockSpec((tile, D_MODEL), row)]
    in_specs += [pl.BlockSpec((tile, D_BRANCH), row) for _ in ys]
    in_specs += [pl.BlockSpec(w_out.shape, const), pl.BlockSpec((1, D_MODEL), const)]
    return pl.pallas_call(
        functools.partial(_outproj_kernel, final=final),
        grid=(n_tok // tile,), in_specs=in_specs,
        out_specs=pl.BlockSpec((tile, D_MODEL), row),
        out_shape=jax.ShapeDtypeStruct((n_tok, D_MODEL), F32),
        compiler_params=pltpu.CompilerParams(dimension_semantics=("arbitrary",),
                                             vmem_limit_bytes=VMEM_LIMIT),
        name="outproj",
    )(h, *ys, w_out, final_w)


def _norm_gate_store(y_ref, rows, col0, parts, nw_ref, z_parts):
    width = sum(p.shape[-1] for p in parts)
    ss = sum(jnp.sum(p * p, axis=-1, keepdims=True) for p in parts)
    scale = lax.rsqrt(ss * (1.0 / width) + EPS)
    c = col0
    for p, z in zip(parts, z_parts):
        w = p.shape[-1]
        out = p * scale * nw_ref[:, c:c + w]
        if z is not None:
            out = out * z
        y_ref[rows, c:c + w] = out.astype(y_ref.dtype)
        c += w


N_STAGES = 5
CHUNK_UNROLL = 1


def _run_chunks(work, tile):
    def body(c, carry):
        rows = pl.ds(pl.multiple_of(c * CHUNK, CHUNK), CHUNK)
        live = [(dict(d), stages) for d, stages in work]
        for k in range(N_STAGES):
            joined = []
            for d, stages in live:
                joint, fn = stages[k] if isinstance(stages[k], tuple) else (None, stages[k])
                if joint is not None and not any(stages is s for s in joined):
                    joint([dd for dd, st in live if st is stages])
                    joined.append(stages)
                fn(d, rows)
        return carry

    lax.fori_loop(0, tile // CHUNK, body, 0, unroll=CHUNK_UNROLL)


def _gla_stages(ms_ref, mk_ref):
    lane = _lane_iota()

    def lane_mask(d, j):
        dk = d["dk"]
        return None if dk == LANES else (lane >= j * dk) & (lane < (j + 1) * dk)

    def pick(a, lm):
        return a if lm is None else jnp.where(lm, a, jnp.zeros_like(a))

    def n_groups(d):
        return d["q"].shape[1] // LANES

    def cumsums(d, rows):
        d["es"] = [_sel_dot2(ms_ref[...], d["lg"][rows, p * 2 * LANES:(p + 1) * 2 * LANES])
                   for p in range(n_groups(d) // 2)]

    def operands(d, rows):
        ops = []
        for g in range(n_groups(d)):
            ls = slice(g * LANES, (g + 1) * LANES)
            e = d["es"][g // 2][:, (g % 2) * LANES:(g % 2 + 1) * LANES]
            q = d["q"][rows, ls]
            k = d["k"][rows, ls]
            gcs = e[0:CHUNK]
            qg = (q * jnp.exp(gcs)).astype(BF16)
            kd = (k * jnp.exp(e[CHUNK:2 * CHUNK])).astype(BF16)
            dec = jnp.exp(gcs[CHUNK - 1:CHUNK, :])
            qb = q.astype(BF16)
            kb = k.astype(BF16)
            ql, kl = [q * k], [None]
            mxu_block = 2
            for l in range(N_LEVELS):
                s = CHUNK >> (l + 1)
                if s >= MIN_VPU_LEVEL:
                    el = _level_exponent(gcs, s)
                else:
                    el = e[mxu_block * CHUNK:(mxu_block + 1) * CHUNK]
                    mxu_block += 1
                w = jnp.exp(el).astype(BF16)
                ql.append(qb * w)
                kl.append(kb * w)
            ops.append((qg, kd, dec, ql, kl))
        d["ops"] = ops

    def scores(d, rows):
        heads = []
        for g in range(n_groups(d)):
            ql, kl = d["ops"][g][3], d["ops"][g][4]
            for j in range(LANES // d["dk"]):
                lm = lane_mask(d, j)
                a = jnp.sum(pick(ql[0], lm), axis=-1, keepdims=True) * mk_ref[0][:, 0:CHUNK]
                for l in range(1, N_LEVELS + 1):
                    a = a + _dot_nt(pick(ql[l], lm), kl[l]) * mk_ref[l][:, 0:CHUNK]
                heads.append((g, j, a.astype(BF16)))
        d["heads"] = heads

    def outputs(d, rows):
        hp = LANES // d["dk"]
        outs, upds = [], [None] * n_groups(d)
        sts = [d["st"][g] for g in range(n_groups(d))]
        stbs = [st.astype(BF16) for st in sts]
        for g, j, ab in d["heads"]:
            h = g * hp + j
            lm = lane_mask(d, j)
            qg, kd = d["ops"][g][0], d["ops"][g][1]
            vh = d["v"][rows, h * LANES:(h + 1) * LANES].astype(BF16)
            outs.append(_dot(ab, vh) + _dot_nt(pick(qg, lm), stbs[g]))
            u = _dot_tn(vh, kd)
            upds[g] = u if upds[g] is None else jnp.where(lm, u, upds[g])
        d["outs"], d["upds"], d["sts"] = outs, upds, sts

    def finish(d, rows):
        for g in range(n_groups(d)):
            d["st"][g] = d["sts"][g] * d["ops"][g][2] + d["upds"][g]
        for h, o in enumerate(d["outs"]):
            _norm_gate_store(d["y"], rows, h * LANES, [o], d["nw"],
                             [d["z"][rows, h * LANES:(h + 1) * LANES]])

    return cumsums, operands, scores, outputs, finish


def _gla_stream(p_ref, s, qk, dk, y_ref, st_s, nw_ref):
    v0, z0, lg0 = 2 * qk, 2 * qk + D_BRANCH, 2 * qk + 2 * D_BRANCH
    return dict(q=p_ref.at[s, :, 0:qk], k=p_ref.at[s, :, qk:v0], v=p_ref.at[s, :, v0:z0],
                z=p_ref.at[s, :, z0:lg0], lg=p_ref.at[s, :, lg0:lg0 + qk],
                y=y_ref.at[s], st=st_s.at[s], nw=nw_ref, dk=dk)


def _mlstm_constants():
    half = LANES // 2
    full0 = half * MLSTM_HEADS
    e = np.zeros((LANES, full0 + 2 * LANES * MLSTM_HEADS), np.float32)
    diff0 = full0 + LANES * MLSTM_HEADS
    sel = np.zeros((16, LANES), np.float32)
    for h in range(MLSTM_HEADS):
        e[SM_F + h, h * half:(h + 1) * half] = 1.0
        e[SM_F + h, full0 + h * LANES:full0 + (h + 1) * LANES] = 1.0
        e[SM_I + h, diff0 + h * LANES:diff0 + (h + 1) * LANES] = 1.0
        e[SM_F + h, diff0 + h * LANES:diff0 + (h + 1) * LANES] = -1.0
        sel[h // 2, SM_I + h] = 1.0
        sel[h // 2, SM_F + h] = -1.0
    return (jnp.asarray(np.concatenate([e, e], axis=0), BF16),
            jnp.asarray(np.concatenate([sel, sel], axis=1), BF16))


def _mlstm_stages(nw_ref, tri_ref, exp_ref, sel_ref):
    lane = _lane_iota()
    is_f = (lane >= SM_F) & (lane < SM_F + MLSTM_HEADS)

    half = LANES // 2
    ri = lax.broadcasted_iota(jnp.int32, (CHUNK, LANES), 0)
    ci = lax.broadcasted_iota(jnp.int32, (CHUNK, LANES), 1)
    causal2 = (ci & (half - 1)) <= ri
    lo_half = lane < half
    even = (lane & 1) == 0
    v0, o0, z0 = 2 * D_BRANCH, 3 * D_BRANCH, 4 * D_BRANCH
    heads = range(MLSTM_HEADS)
    pairs = range(MLSTM_HEADS // 2)
    full0 = half * MLSTM_HEADS
    diff0 = full0 + LANES * MLSTM_HEADS

    def qk_products(d, rows):
        qk_ref = d["p"]
        qbs = [qk_ref[rows, h * LANES:(h + 1) * LANES].astype(BF16) for h in heads]
        d["ks"] = [qk_ref[rows, D_BRANCH + h * LANES:D_BRANCH + (h + 1) * LANES] for h in heads]
        kbs = [k.astype(BF16) for k in d["ks"]]
        zero = jnp.zeros((CHUNK, LANES), BF16)
        ones = jnp.ones((CHUNK, LANES), BF16)
        d["qk_raw"] = [_dot_nt(jnp.concatenate([qbs[2 * p], qbs[2 * p + 1]], axis=1),
                               _side_by_side(kbs[2 * p], kbs[2 * p + 1], zero)) for p in pairs]
        d["qc"] = [_dot(qbs[h], d["c"][h].astype(BF16)) for h in heads]
        d["vaug"] = [jnp.concatenate([d["p"][rows, v0 + h * LANES:v0 + (h + 1) * LANES].astype(BF16),
                                      ones], axis=1) for h in heads]

    def gate_sums(d, rows):
        gc = d["g"][rows, :]
        bcol = _sel_dot2(tri_ref[...], gc)
        ib = jnp.where(is_f, bcol, gc)
        d["ib"] = ib
        by_parity = jnp.concatenate([jnp.where(even, ib, 0.0), jnp.where(even, 0.0, ib)], axis=0)
        d["drows"] = _sel_dot2_nt(sel_ref[...], by_parity)

    def spread_gates(ds):
        ex = _dot_sel2(jnp.concatenate([d["ib"] for d in ds], axis=0), exp_ref[...])
        for i, d in enumerate(ds):
            d["ex"] = ex[i * CHUNK:(i + 1) * CHUNK]

    def weights(d, rows):
        ex = d["ex"]
        mx = d["m"][0:1, :]
        ss, mrs = [], []
        for p in pairs:
            bx = ex[:, p * LANES:(p + 1) * LANES]
            lw = jnp.where(causal2, bx + d["drows"][p:p + 1, :], NEG_BIG)
            mr0 = jnp.max(jnp.where(lo_half, lw, NEG_BIG), axis=-1, keepdims=True)
            mr1 = jnp.max(jnp.where(lo_half, NEG_BIG, lw), axis=-1, keepdims=True)
            m64 = jnp.where(lo_half, mx[:, 2 * p * LANES:(2 * p + 1) * LANES],
                            mx[:, (2 * p + 1) * LANES:(2 * p + 2) * LANES])
            m_row = jnp.maximum(jnp.where(lo_half, mr0, mr1), bx + m64)
            ss.append((d["qk_raw"][p] * jnp.exp(lw - m_row)).astype(BF16))
            mrs += [mr0, mr1]
        d["ss"], d["mrs"] = ss, mrs
        b_last = ex[CHUNK - 1:CHUNK, full0:diff0]
        lwe = ex[:, diff0:] + b_last
        m_new = jnp.maximum(b_last + mx, jnp.max(lwe, axis=0, keepdims=True))
        d["cd"] = jnp.exp(b_last + mx - m_new)
        d["m_new"] = m_new
        kw = jnp.exp(lwe - m_new)
        d["kws"] = [(d["ks"][h] * kw[:, h * LANES:(h + 1) * LANES]).astype(BF16) for h in heads]

    def numerators(d, rows):
        zero = jnp.zeros((CHUNK, 2 * LANES), BF16)
        d["nums"] = [_dot(d["ss"][p], _side_by_side(d["vaug"][2 * p], d["vaug"][2 * p + 1], zero))
                     for p in pairs]
        d["cups"] = [_dot_tn(d["kws"][h], d["vaug"][h]) for h in heads]

    def finish(d, rows):
        ex = d["ex"]
        mx = d["m"][0:1, :]
        for h in heads:
            p, hd = divmod(h, 2)
            hs = slice(h * LANES, (h + 1) * LANES)
            m_inter = ex[:, full0 + h * LANES:full0 + (h + 1) * LANES] + mx[:, hs]
            m_row = jnp.maximum(d["mrs"][h], m_inter)
            inter = jnp.exp(m_inter - m_row)
            sv = d["nums"][p][:, hd * 2 * LANES:(hd + 1) * 2 * LANES]
            num = sv[:, 0:LANES] + inter * d["qc"][h][:, 0:LANES]
            den = sv[:, LANES:] + inter * d["qc"][h][:, LANES:]
            hh = num / jnp.maximum(jnp.abs(den), jnp.exp(-m_row))
            cd = d["cd"][:, hs]
            d["c"][h] = jnp.concatenate([cd, cd], axis=1) * d["c"][h] + d["cups"][h]
            og = d["p"][rows, o0 + h * LANES:o0 + (h + 1) * LANES]
            _norm_gate_store(d["y"], rows, h * LANES, [og * hh], nw_ref,
                             [d["p"][rows, z0 + h * LANES:z0 + (h + 1) * LANES]])
        d["m"][0:1, :] = d["m_new"]

    return qk_products, gate_sums, (spread_gates, weights), numerators, finish


def _ssd_constants():
    e = np.zeros((LANES, SSD_HEADS * SSD_HEAD_DIM), np.float32)
    sel = np.zeros((16, LANES), np.float32)
    for h in range(SSD_HEADS):
        e[SM_DT + h, h * SSD_HEAD_DIM:(h + 1) * SSD_HEAD_DIM] = 1.0
        sel[h // 2, SM_DT + h] = 1.0
    return (jnp.asarray(np.concatenate([e, e], axis=0), BF16),
            jnp.asarray(np.concatenate([sel, sel], axis=1), BF16))


def _ssd_stages(alog_ref, dx_ref, nw_ref, tri_ref, sel_ref, exp_ref):
    lane = _lane_iota()
    is_dt = (lane >= SM_DT) & (lane < SM_DT + SSD_HEADS)

    half = LANES // 2
    ri = lax.broadcasted_iota(jnp.int32, (CHUNK, LANES), 0)
    ci = lax.broadcasted_iota(jnp.int32, (CHUNK, LANES), 1)
    causal2 = (ci & (half - 1)) <= ri
    lo_half = lane < half
    even = (lane & 1) == 0
    a_lane = jnp.where(is_dt, -jnp.exp(alog_ref[...]), 0.0)
    b0, c0, z0 = D_BRANCH, D_BRANCH + SSD_BC, D_BRANCH + 2 * SSD_BC
    groups = range(SSD_GROUPS)
    group_w = D_BRANCH // SSD_GROUPS
    pairs = range(D_BRANCH // LANES)
    pairs_per_group = group_w // LANES

    def products(d, rows):
        xbc = d["p"]
        d["bgs"] = [xbc[rows, b0 + g * SSD_STATE:b0 + (g + 1) * SSD_STATE].astype(BF16) for g in groups]
        cgbs = [xbc[rows, c0 + g * SSD_STATE:c0 + (g + 1) * SSD_STATE].astype(BF16) for g in groups]
        d["cb2"] = [_dot_nt(cgbs[g], jnp.concatenate([d["bgs"][g], d["bgs"][g]], axis=0))
                    for g in groups]
        d["cst"] = [_dot(cgbs[g], d["st"][g].astype(BF16)) for g in groups]

    def decay_sums(d, rows):
        dt = d["dt"][rows, :]
        acs = _sel_dot2(tri_ref[...], dt * a_lane)
        d["dt_acs"] = jnp.concatenate([dt, acs], axis=0)
        by_parity = jnp.concatenate([jnp.where(even, acs, 0.0), jnp.where(even, 0.0, acs)], axis=0)
        d["a_rows"] = _sel_dot2_nt(sel_ref[...], by_parity)

    def spread_decays(ds):
        ex = _dot_sel2(jnp.concatenate([d["dt_acs"] for d in ds], axis=0), exp_ref[...])
        for i, d in enumerate(ds):
            d["dtx"] = ex[2 * i * CHUNK:(2 * i + 1) * CHUNK]
            d["acs_x"] = ex[(2 * i + 1) * CHUNK:(2 * i + 2) * CHUNK]

    def decays(d, rows):
        d["xss"], d["ms"], d["xblk"], xdecs = [], [], [], []
        for p in pairs:
            ls = slice(p * LANES, (p + 1) * LANES)
            ax = d["acs_x"][:, ls]
            lmat = jnp.exp(jnp.where(causal2, ax - d["a_rows"][p:p + 1, :], NEG_BIG))
            d["ms"].append((d["cb2"][p // pairs_per_group] * lmat).astype(BF16))
            xs = d["p"][rows, ls]
            xdt = xs * d["dtx"][:, ls]
            d["xss"].append(xs)
            d["xblk"].append(jnp.concatenate([jnp.where(lo_half, xdt, 0.0), jnp.where(lo_half, 0.0, xdt)],
                                             axis=0).astype(BF16))
            xdecs.append((xdt * jnp.exp(ax[CHUNK - 1:CHUNK, :] - ax)).astype(BF16))
        d["xdec"] = [jnp.concatenate(xdecs[g * pairs_per_group:(g + 1) * pairs_per_group], axis=1)
                     for g in groups]

    def chunk_products(d, rows):
        d["yds"] = [_dot(d["ms"][p], d["xblk"][p]) for p in pairs]
        d["ups"] = [_dot_tn(d["bgs"][g], d["xdec"][g]) for g in groups]

    def finish(d, rows):
        for g in groups:
            gs = slice(g * group_w, (g + 1) * group_w)
            eax = jnp.exp(d["acs_x"][:, gs])
            d["st"][g] = d["st"][g] * eax[CHUNK - 1:CHUNK, :] + d["ups"][g]
            ys = []
            for pp in range(pairs_per_group):
                p = g * pairs_per_group + pp
                ls = slice(p * LANES, (p + 1) * LANES)
                y = (d["cst"][g][:, pp * LANES:(pp + 1) * LANES] * eax[:, pp * LANES:(pp + 1) * LANES]
                     + dx_ref[:, ls] * d["xss"][p] + d["yds"][p])
                ys.append(y * d["p"][rows, z0 + p * LANES:z0 + (p + 1) * LANES])
            _norm_gate_store(d["y"], rows, g * group_w, ys, nw_ref, [None] * len(ys))

    return products, decay_sums, (spread_decays, decays), chunk_products, finish


def _mixers_kernel(pa_ref, pb_ref, pc_ref, pd_ref, sm_ref, nwa_ref, nwb_ref, nwc_ref, nwd_ref,
                   ms_ref, mk_ref, tri_ref, mexp_ref, msel_ref, alog_ref, dx_ref, dsel_ref, dexp_ref,
                   ya_ref, yb_ref, yc_ref, yd_ref, sta_s, stc_s, cb_s, mb_s, std_s, *, tile):
    @pl.when(pl.program_id(0) == 0)
    def _():
        for r in (sta_s, stc_s, cb_s, mb_s, std_s):
            r[...] = jnp.zeros_like(r)

    gla = _gla_stages(ms_ref, mk_ref)
    mlstm = _mlstm_stages(nwb_ref, tri_ref, mexp_ref, msel_ref)
    ssd = _ssd_stages(alog_ref, dx_ref, nwd_ref, tri_ref, dsel_ref, dexp_ref)
    work = []
    for s in range(pa_ref.shape[0]):
        work.append((_gla_stream(pa_ref, s, GLA_QK, GLA_DK, ya_ref, sta_s, nwa_ref), gla))
        work.append((dict(p=pb_ref.at[s], g=sm_ref.at[s], c=cb_s.at[s], m=mb_s.at[s], y=yb_ref.at[s]),
                     mlstm))
        work.append((_gla_stream(pc_ref, s, HGRN_QF, HGRN_DK, yc_ref, stc_s, nwc_ref), gla))
        work.append((dict(p=pd_ref.at[s], dt=sm_ref.at[s], st=std_s.at[s], y=yd_ref.at[s]), ssd))
    _run_chunks(work, tile)


def _mixers(tok_inputs, const_inputs, n_batch, seq, tile):
    tok = lambda t: (0, t, 0)
    in_specs = [pl.BlockSpec((n_batch, tile, a.shape[2]), tok) for a in tok_inputs]
    for a in const_inputs:
        in_specs.append(pl.BlockSpec(a.shape, lambda t, nd=a.ndim: (0,) * nd))
    scratch = [pltpu.VMEM((n_batch, GLA_QK // LANES, GLA_DV, LANES), F32),
               pltpu.VMEM((n_batch, HGRN_QF // LANES, HGRN_DV, LANES), F32),
               pltpu.VMEM((n_batch, MLSTM_HEADS, MLSTM_DH, 2 * MLSTM_DH), F32),
               pltpu.VMEM((n_batch, 8, MLSTM_HEADS * LANES), F32),
               pltpu.VMEM((n_batch, SSD_GROUPS, SSD_STATE, D_BRANCH // SSD_GROUPS), F32)]
    return pl.pallas_call(
        functools.partial(_mixers_kernel, tile=tile), grid=(seq // tile,), in_specs=in_specs,
        out_specs=[pl.BlockSpec((n_batch, tile, D_BRANCH), tok)] * 4,
        out_shape=[jax.ShapeDtypeStruct((n_batch, seq, D_BRANCH), BF16)] * 4,
        scratch_shapes=scratch,
        compiler_params=pltpu.CompilerParams(dimension_semantics=("arbitrary",),
                                             vmem_limit_bytes=VMEM_LIMIT),
        name="mixers",
    )(*tok_inputs, *const_inputs)


def _pad_lanes(parts, total=LANES):
    width = sum(p.shape[-1] for p in parts)
    lead = parts[0].shape[:-1]
    return jnp.concatenate(list(parts) + [jnp.zeros(lead + (total - width,), parts[0].dtype)], axis=-1)


def _small_vector(i_part, f_part, dt_part):
    z = jnp.zeros((GLA_GATE_RANK,), F32)
    return _pad_lanes([z, i_part.astype(F32), f_part.astype(F32), dt_part.astype(F32)])[None, :]


def kernel(x, norm_w, w_in, gla_gate_w, gla_gate_b, gla_norm_w, ml_conv_w, ml_conv_b, ml_i_b, ml_f_b,
           ml_norm_w, hg_lb_logits, hg_norm_w, ssd_conv_w, ssd_conv_b, ssd_dt_bias, ssd_A_log, ssd_D,
           ssd_norm_w, w_out, final_norm_w):
    n_batch, seq, _ = x.shape
    depth = w_in.shape[0]
    tile = min(MIX_TILE, seq)
    n_tok = n_batch * seq

    mstack, masks = _gla_constants()
    tri_c = _tri2(CHUNK)
    ml_exp, ml_sel = _mlstm_constants()
    ssd_exp, ssd_sel = _ssd_constants()
    zero4 = jnp.zeros((MLSTM_HEADS,), F32)
    in_tile = min(IN_TILE, seq)

    p = jax.nn.softmax(hg_lb_logits.astype(F32), axis=0)
    lower_bounds = jnp.cumsum(p, axis=0) - p[0:1]

    w_wide, w_small = _realign(jnp.swapaxes(w_in, 1, 2), REALIGN_ROWS)

    h = x.reshape(n_tok, D_MODEL)
    row2 = lambda v: v.astype(F32).reshape(1, -1)
    for l in range(depth):
        gate_w = jnp.concatenate(
            [gla_gate_w[l].astype(F32), jnp.zeros((LANES - GLA_GATE_RANK, GLA_QK), F32)], axis=0)
        consts = [gate_w, row2(gla_gate_b[l]), row2(lower_bounds[l]),
                  ml_conv_w[l].astype(F32), row2(ml_conv_b[l]),
                  ssd_conv_w[l].astype(F32), row2(ssd_conv_b[l]),
                  _small_vector(ml_i_b[l], ml_f_b[l], ssd_dt_bias[l])]
        pa, pb, pc, pd, sm = [p.reshape(n_batch, seq, -1)
                              for p in _inproj(h, row2(norm_w[l]), w_wide, w_small, l, consts, in_tile,
                                               seq // in_tile)]

        ys = _mixers(
            [pa, pb, pc, pd, sm],
            [row2(gla_norm_w[l]), row2(ml_norm_w[l]), row2(hg_norm_w[l]), row2(ssd_norm_w[l]),
             mstack, masks, tri_c, ml_exp, ml_sel, _small_vector(zero4, zero4, ssd_A_log[l]),
             row2(jnp.repeat(ssd_D[l].astype(F32), SSD_HEAD_DIM)), ssd_sel, ssd_exp],
            n_batch, seq, tile)
        ys = [y.reshape(n_tok, D_BRANCH) for y in ys]
        h = _outproj(h, ys, w_out[l].astype(BF16), row2(final_norm_w), l == depth - 1,
                     min(OUT_TILE, n_tok))
    return h.reshape(n_batch, seq, D_MODEL)
```

```python
import functools

import numpy as np
import jax
import jax.numpy as jnp
from jax import lax
from jax.experimental import pallas as pl
from jax.experimental.pallas import tpu as pltpu

F32 = jnp.float32
BF16 = jnp.bfloat16

D_MODEL = 1024
D_BRANCH = 512
EPS = 1e-6
NEG_BIG = -1e30

GLA_HEADS, GLA_DK, GLA_DV = 4, 64, 128
GLA_GATE_RANK, GLA_GATE_NORM = 16, 16.0
MLSTM_HEADS, MLSTM_DH, MLSTM_CONV = 4, 128, 4
HGRN_HEADS, HGRN_DK, HGRN_DV = 4, 128, 128
SSD_HEAD_DIM, SSD_HEADS, SSD_GROUPS, SSD_STATE, SSD_CONV = 64, 8, 2, 128, 4
GLA_QK = GLA_HEADS * GLA_DK
HGRN_QF = HGRN_HEADS * HGRN_DK
SSD_BC = SSD_GROUPS * SSD_STATE
PROJ_SIZES = (
    GLA_QK, GLA_QK, D_BRANCH, GLA_GATE_RANK, D_BRANCH,
    D_BRANCH, D_BRANCH, D_BRANCH, MLSTM_HEADS, MLSTM_HEADS, D_BRANCH, D_BRANCH,
    HGRN_QF, HGRN_QF, D_BRANCH, D_BRANCH,
    D_BRANCH, SSD_BC, SSD_BC, SSD_HEADS, D_BRANCH,
)

LANES = 128
HIST = 8
VMEM_LIMIT = 56 * 1024 * 1024

REALIGN_ROWS = 256
IN_TILE = 256
MIX_TILE = 256
OUT_TILE = 512

CHUNK = 64
N_LEVELS = 6
MIN_VPU_LEVEL = 4

SM_GR, SM_I, SM_F, SM_DT = 0, 16, 20, 24

W_A = 2 * GLA_QK + 2 * D_BRANCH
W_B = 5 * D_BRANCH
W_C = 2 * HGRN_QF + 2 * D_BRANCH
W_D = 2 * D_BRANCH + 2 * SSD_BC
COL_A = 0
COL_B = COL_A + W_A
COL_C = COL_B + W_B
COL_D = COL_C + W_C
COL_S = COL_D + W_D


def _gla_constants():
    c = CHUNK
    t = np.arange(c)[:, None]
    d = np.arange(c)[None, :]
    blocks = [(d <= t), (d > t)]
    masks = [np.eye(c, dtype=bool)]
    for l in range(N_LEVELS):
        s = c >> (l + 1)
        mid_t = (t // (2 * s)) * (2 * s) + s
        upper = t >= mid_t
        if 1 < s < MIN_VPU_LEVEL:
            blocks.append(np.where(upper, (d >= mid_t) & (d <= t), (d > t) & (d <= mid_t - 1)))
        same = (t // (2 * s)) == (d // (2 * s))
        masks.append(same & upper & (d < mid_t))
    mstack = np.concatenate(blocks, axis=0).astype(np.float32)
    mstack = np.concatenate([mstack, mstack], axis=1)
    masks = np.stack(masks).astype(np.float32)
    return jnp.asarray(mstack, BF16), jnp.asarray(np.concatenate([masks, masks], axis=-1))


def _tri2(n):
    tri = np.tril(np.ones((n, n), np.float32))
    return jnp.asarray(np.concatenate([tri, tri], axis=1), BF16)


def _dot(a, b):
    return jnp.dot(a, b, preferred_element_type=F32)


def _dot_nt(a, b):
    return lax.dot_general(a, b, (((1,), (1,)), ((), ())), preferred_element_type=F32)


def _dot_tn(a, b):
    return lax.dot_general(a, b, (((0,), (0,)), ((), ())), preferred_element_type=F32)


def _sel_dot2(sel2, x):
    hi = x.astype(BF16)
    mid = (x - hi.astype(F32)).astype(BF16)
    return _dot(sel2, jnp.concatenate([hi, mid], axis=0))


def _sel_dot2_nt(sel2, x):
    hi = x.astype(BF16)
    mid = (x - hi.astype(F32)).astype(BF16)
    return _dot_nt(sel2, jnp.concatenate([hi, mid], axis=1))


def _level_exponent(gcs, s):
    pieces = []
    for b in range(0, CHUNK, 2 * s):
        ref_row = gcs[b + s - 1:b + s, :]
        if s >= 8:
            pieces += [ref_row - gcs[b:b + s], gcs[b + s:b + 2 * s] - ref_row]
        else:
            diff = gcs[b:b + 2 * s] - ref_row
            pieces.append(jnp.minimum(diff, -diff))
    return jnp.concatenate(pieces, axis=0)


def _dot_sel2(x, sel2):
    hi = x.astype(BF16)
    mid = (x - hi.astype(F32)).astype(BF16)
    return _dot(jnp.concatenate([hi, mid], axis=1), sel2)


def _dot_f32(a, b):
    ah = a.astype(BF16)
    al = (a - ah.astype(F32)).astype(BF16)
    bh = b.astype(BF16)
    bl = (b - bh.astype(F32)).astype(BF16)
    return _dot(ah, bh) + _dot(al, bh) + _dot(ah, bl)


def _softplus(x):
    return jnp.maximum(x, 0.0) + jnp.log1p(jnp.exp(-jnp.abs(x)))


def _log_sigmoid(x):
    return -_softplus(-x)


def _silu(x):
    return x * jax.nn.sigmoid(x)


def _lane_iota():
    return lax.broadcasted_iota(jnp.int32, (1, LANES), 1)


def _side_by_side(a, b, zero):
    return jnp.concatenate([jnp.concatenate([a, zero], axis=1),
                            jnp.concatenate([zero, b], axis=1)], axis=0)


def _wide_segments():
    segs, src, dst = [], 0, 0
    run_start = None
    for size in PROJ_SIZES + (0,):
        wide = size >= LANES
        if wide and run_start is None:
            run_start = src
        if not wide and run_start is not None:
            segs.append((run_start, dst, src - run_start))
            dst += src - run_start
            run_start = None
        src += size
    return segs


def _realign_kernel(wt_ref, o_ref, os_ref):
    for src, dst, n in _wide_segments():
        for r in range(0, n, LANES):
            o_ref[:, dst + r:dst + r + LANES] = wt_ref[src + r:src + r + LANES, :].T.astype(BF16)
    narrow, src = [], 0
    for size in PROJ_SIZES:
        if size < LANES:
            narrow.append(wt_ref[src:src + size, :])
        src += size
    used = sum(p.shape[0] for p in narrow)
    narrow.append(jnp.zeros((LANES - used, wt_ref.shape[1]), F32))
    os_ref[...] = jnp.concatenate(narrow, axis=0).T


def _realign(wt, rows):
    depth, n_proj, d_model = wt.shape
    return pl.pallas_call(
        _realign_kernel, grid=(depth, d_model // rows),
        in_specs=[pl.BlockSpec((None, n_proj, rows), lambda l, j: (l, 0, j))],
        out_specs=[pl.BlockSpec((None, rows, COL_S), lambda l, j: (l, j, 0)),
                   pl.BlockSpec((None, rows, LANES), lambda l, j: (l, j, 0))],
        out_shape=[jax.ShapeDtypeStruct((depth, d_model, COL_S), BF16),
                   jax.ShapeDtypeStruct((depth, d_model, LANES), F32)],
        compiler_params=pltpu.CompilerParams(dimension_semantics=("arbitrary", "arbitrary"),
                                             vmem_limit_bytes=VMEM_LIMIT),
        name="realign",
    )(wt)


CONV_ROWS, CONV_LANES = 256, 128


def _causal_conv_silu(raw, xe_ref, hist_ref, w_ref, b_ref, taps, out_ref, scale=None):
    rows, n = raw.shape
    xe_ref[0:HIST, :] = hist_ref[...]
    xe_ref[HIST:HIST + rows, :] = raw
    hist_ref[...] = raw[rows - HIST:rows, :]
    for r in range(0, rows, CONV_ROWS):
        for l in range(0, n, CONV_LANES):
            cs = slice(l, l + CONV_LANES)
            acc = b_ref[:, cs] + w_ref[taps - 1:taps, cs] * xe_ref[HIST + r:HIST + r + CONV_ROWS, cs]
            for k in range(taps - 1):
                off = HIST - (taps - 1) + k + r
                acc = acc + w_ref[k:k + 1, cs] * xe_ref[off:off + CONV_ROWS, cs]
            out = _silu(acc)
            out_ref[r:r + CONV_ROWS, cs] = out if scale is None else out * scale[:, cs]


def _inproj_kernel(x_ref, xn_ref, nw_ref, w_ref, ws_ref, gw_ref, gb_ref, lb_ref, cwb_ref, cbb_ref, cwd_ref,
                   cbd_ref, gbias_ref, oa_ref, ob_ref, oc_ref, od_ref, os_ref, xe_s, hist_b, hist_d, u_s,
                   *, tiles_per_seq):
    step = pl.program_id(0)

    @pl.when(step % tiles_per_seq == 0)
    def _():
        hist_b[...] = jnp.zeros_like(hist_b)
        hist_d[...] = jnp.zeros_like(hist_d)

    def normed(x):
        ms = jnp.mean(x * x, axis=-1, keepdims=True)
        return (x * lax.rsqrt(ms + EPS) * nw_ref[...]).astype(BF16)

    @pl.when(step == 0)
    def _():
        u_s[0] = normed(x_ref[...])

    u = u_s[step % 2]
    lane = _lane_iota()

    small = _dot(u, ws_ref[...].astype(BF16))
    biased = small + gbias_ref[...]
    is_f = (lane >= SM_F) & (lane < SM_F + MLSTM_HEADS)
    is_dt = (lane >= SM_DT) & (lane < SM_DT + SSD_HEADS)
    os_ref[...] = jnp.where(is_f, _log_sigmoid(biased), jnp.where(is_dt, _softplus(biased), biased))

    raw = _dot(u, w_ref[:, COL_B:COL_B + W_B])
    conv_w = 2 * D_BRANCH
    k_scale = jnp.where(lax.broadcasted_iota(jnp.int32, (1, conv_w), 1) < D_BRANCH, 1.0, MLSTM_DH ** -0.5)
    _causal_conv_silu(raw[:, 0:conv_w], xe_s, hist_b, cwb_ref, cbb_ref, MLSTM_CONV, ob_ref, k_scale)
    o0, z0 = 3 * D_BRANCH, 4 * D_BRANCH
    ob_ref[:, conv_w:o0] = raw[:, conv_w:o0]
    ob_ref[:, o0:z0] = jax.nn.sigmoid(raw[:, o0:z0])
    ob_ref[:, z0:W_B] = _silu(raw[:, z0:W_B])

    raw = _dot(u, w_ref[:, COL_D:COL_D + W_D])
    conv_w = D_BRANCH + 2 * SSD_BC
    _causal_conv_silu(raw[:, 0:conv_w], xe_s, hist_d, cwd_ref, cbd_ref, SSD_CONV, od_ref)
    od_ref[:, conv_w:W_D] = _silu(raw[:, conv_w:W_D])
    u_s[(step + 1) % 2] = normed(xn_ref[...])

    raw = _dot(u, w_ref[:, COL_C:COL_C + W_C])
    lb = lb_ref[...]
    fr = raw[:, HGRN_QF:2 * HGRN_QF]
    oc_ref[:, 0:HGRN_QF] = raw[:, 0:HGRN_QF] * (HGRN_DK ** -0.5)
    oc_ref[:, HGRN_QF:2 * HGRN_QF] = (1.0 - lb) * jax.nn.sigmoid(-fr)
    z0 = 2 * HGRN_QF + D_BRANCH
    oc_ref[:, 2 * HGRN_QF:z0] = raw[:, 2 * HGRN_QF:z0]
    oc_ref[:, z0:W_C] = _silu(raw[:, z0:W_C])
    oc_ref[:, W_C:W_C + HGRN_QF] = jnp.log(jnp.maximum(lb + (1.0 - lb) * jax.nn.sigmoid(fr), 1e-30))

    gate = _dot_f32(small, gw_ref[...]) + gb_ref[...]
    oa_ref[:, W_A:W_A + GLA_QK] = _log_sigmoid(gate) * (1.0 / GLA_GATE_NORM)
    raw = _dot(u, w_ref[:, COL_A:COL_A + W_A])
    oa_ref[:, 0:GLA_QK] = raw[:, 0:GLA_QK] * (GLA_DK ** -0.5)
    z0 = 2 * GLA_QK + D_BRANCH
    oa_ref[:, GLA_QK:z0] = raw[:, GLA_QK:z0]
    oa_ref[:, z0:W_A] = _silu(raw[:, z0:W_A])


def _inproj(h, norm_w, w_wide, w_small, layer, consts, tile, tiles_per_seq):
    n_tok = h.shape[0]
    const = lambda i: (0, 0)
    row = lambda i: (i, 0)
    n_tiles = n_tok // tile
    in_specs = [pl.BlockSpec((tile, D_MODEL), row),
                pl.BlockSpec((tile, D_MODEL), lambda i: (jnp.minimum(i + 1, n_tiles - 1), 0)),
                pl.BlockSpec((1, D_MODEL), const),
                pl.BlockSpec((None, D_MODEL, COL_S), lambda i: (layer, 0, 0), pipeline_mode=pl.Buffered(1)),
                pl.BlockSpec((None, D_MODEL, LANES), lambda i: (layer, 0, 0))]
    in_specs += [pl.BlockSpec(c.shape, const) for c in consts]
    widths = (W_A + GLA_QK, W_B, W_C + HGRN_QF, W_D, LANES)
    conv_w = 2 * D_BRANCH
    return pl.pallas_call(
        functools.partial(_inproj_kernel, tiles_per_seq=tiles_per_seq),
        grid=(n_tiles,), in_specs=in_specs,
        out_specs=[pl.BlockSpec((tile, w), row) for w in widths],
        out_shape=[jax.ShapeDtypeStruct((n_tok, w), F32) for w in widths],
        scratch_shapes=[pltpu.VMEM((tile + HIST, conv_w), F32), pltpu.VMEM((HIST, conv_w), F32),
                        pltpu.VMEM((HIST, conv_w), F32), pltpu.VMEM((2, tile, D_MODEL), BF16)],
        compiler_params=pltpu.CompilerParams(dimension_semantics=("arbitrary",),
                                             vmem_limit_bytes=VMEM_LIMIT),
        name="inproj",
    )(h, h, norm_w, w_wide, w_small, *consts)


def _outproj_kernel(h_ref, ya_ref, yb_ref, yc_ref, yd_ref, w_ref, fw_ref, o_ref, *, final):
    acc = h_ref[...]
    for i, y_ref in enumerate((ya_ref, yb_ref, yc_ref, yd_ref)):
        acc = acc + _dot(y_ref[...], w_ref[i * D_BRANCH:(i + 1) * D_BRANCH, :])
    if final:
        ms = jnp.mean(acc * acc, axis=-1, keepdims=True)
        acc = acc * lax.rsqrt(ms + EPS) * fw_ref[...]
    o_ref[...] = acc


def _outproj(h, ys, w_out, final_w, final, tile):
    n_tok = h.shape[0]
    const = lambda i: (0, 0)
    row = lambda i: (i, 0)
    in_specs = [pl.BlockSpec((tile, D_MODEL), row)]
    in_specs += [pl.BlockSpec((tile, D_BRANCH), row) for _ in ys]
    in_specs += [pl.BlockSpec(w_out.shape, const), pl.BlockSpec((1, D_MODEL), const)]
    return pl.pallas_call(
        functools.partial(_outproj_kernel, final=final),
        grid=(n_tok // tile,), in_specs=in_specs,
        out_specs=pl.BlockSpec((tile, D_MODEL), row),
        out_shape=jax.ShapeDtypeStruct((n_tok, D_MODEL), F32),
        compiler_params=pltpu.CompilerParams(dimension_semantics=("arbitrary",),
                                             vmem_limit_bytes=VMEM_LIMIT),
        name="outproj",
    )(h, *ys, w_out, final_w)


def _norm_gate_store(y_ref, rows, col0, parts, nw_ref, z_parts):
    width = sum(p.shape[-1] for p in parts)
    ss = sum(jnp.sum(p * p, axis=-1, keepdims=True) for p in parts)
    scale = lax.rsqrt(ss * (1.0 / width) + EPS)
    c = col0
    for p, z in zip(parts, z_parts):
        w = p.shape[-1]
        out = p * scale * nw_ref[:, c:c + w]
        if z is not None:
            out = out * z
        y_ref[rows, c:c + w] = out.astype(y_ref.dtype)
        c += w


N_STAGES = 5
CHUNK_UNROLL = 1


def _run_chunks(work, tile):
    def body(c, carry):
        rows = pl.ds(pl.multiple_of(c * CHUNK, CHUNK), CHUNK)
        live = [(dict(d), stages) for d, stages in work]
        for k in range(N_STAGES):
            joined = []
            for d, stages in live:
                joint, fn = stages[k] if isinstance(stages[k], tuple) else (None, stages[k])
                if joint is not None and not any(stages is s for s in joined):
                    joint([dd for dd, st in live if st is stages])
                    joined.append(stages)
                fn(d, rows)
        return carry

    lax.fori_loop(0, tile // CHUNK, body, 0, unroll=CHUNK_UNROLL)


def _gla_stages(ms_ref, mk_ref):
    lane = _lane_iota()

    def lane_mask(d, j):
        dk = d["dk"]
        return None if dk == LANES else (lane >= j * dk) & (lane < (j + 1) * dk)

    def pick(a, lm):
        return a if lm is None else jnp.where(lm, a, jnp.zeros_like(a))

    def n_groups(d):
        return d["q"].shape[1] // LANES

    def cumsums(d, rows):
        d["es"] = [_sel_dot2(ms_ref[...], d["lg"][rows, p * 2 * LANES:(p + 1) * 2 * LANES])
                   for p in range(n_groups(d) // 2)]

    def operands(d, rows):
        ops = []
        for g in range(n_groups(d)):
            ls = slice(g * LANES, (g + 1) * LANES)
            e = d["es"][g // 2][:, (g % 2) * LANES:(g % 2 + 1) * LANES]
            q = d["q"][rows, ls]
            k = d["k"][rows, ls]
            gcs = e[0:CHUNK]
            qg = (q * jnp.exp(gcs)).astype(BF16)
            kd = (k * jnp.exp(e[CHUNK:2 * CHUNK])).astype(BF16)
            dec = jnp.exp(gcs[CHUNK - 1:CHUNK, :])
            qb = q.astype(BF16)
            kb = k.astype(BF16)
            ql, kl = [q * k], [None]
            mxu_block = 2
            for l in range(N_LEVELS):
                s = CHUNK >> (l + 1)
                if s == 1:
                    ql.append(q * jnp.exp(d["lg"][rows, ls]) * pltpu.roll(k, 1, axis=0))
                    kl.append(None)
                    continue
                if s >= MIN_VPU_LEVEL:
                    el = _level_exponent(gcs, s)
                else:
                    el = e[mxu_block * CHUNK:(mxu_block + 1) * CHUNK]
                    mxu_block += 1
                w = jnp.exp(el).astype(BF16)
                ql.append(qb * w)
                kl.append(kb * w)
            ops.append((qg, kd, dec, ql, kl))
        d["ops"] = ops

    def scores(d, rows):
        heads = []
        for g in range(n_groups(d)):
            ql, kl = d["ops"][g][3], d["ops"][g][4]
            for j in range(LANES // d["dk"]):
                lm = lane_mask(d, j)
                a = None
                for l in range(N_LEVELS + 1):
                    if kl[l] is None:
                        s = jnp.sum(pick(ql[l], lm), axis=-1, keepdims=True)
                    else:
                        s = _dot_nt(pick(ql[l], lm), kl[l])
                    s = s * mk_ref[l][:, 0:CHUNK]
                    a = s if a is None else a + s
                heads.append((g, j, a.astype(BF16)))
        d["heads"] = heads

    def outputs(d, rows):
        hp = LANES // d["dk"]
        outs, upds = [], [None] * n_groups(d)
        sts = [d["st"][g] for g in range(n_groups(d))]
        stbs = [st.astype(BF16) for st in sts]
        for g, j, ab in d["heads"]:
            h = g * hp + j
            lm = lane_mask(d, j)
            qg, kd = d["ops"][g][0], d["ops"][g][1]
            vh = d["v"][rows, h * LANES:(h + 1) * LANES].astype(BF16)
            outs.append(_dot(ab, vh) + _dot_nt(pick(qg, lm), stbs[g]))
            u = _dot_tn(vh, kd)
            upds[g] = u if upds[g] is None else jnp.where(lm, u, upds[g])
        d["outs"], d["upds"], d["sts"] = outs, upds, sts

    def finish(d, rows):
        for g in range(n_groups(d)):
            d["st"][g] = d["sts"][g] * d["ops"][g][2] + d["upds"][g]
        for h, o in enumerate(d["outs"]):
            _norm_gate_store(d["y"], rows, h * LANES, [o], d["nw"],
                             [d["z"][rows, h * LANES:(h + 1) * LANES]])

    return cumsums, operands, scores, outputs, finish


def _gla_stream(p_ref, s, qk, dk, y_ref, st_s, nw_ref):
    v0, z0, lg0 = 2 * qk, 2 * qk + D_BRANCH, 2 * qk + 2 * D_BRANCH
    return dict(q=p_ref.at[s, :, 0:qk], k=p_ref.at[s, :, qk:v0], v=p_ref.at[s, :, v0:z0],
                z=p_ref.at[s, :, z0:lg0], lg=p_ref.at[s, :, lg0:lg0 + qk],
                y=y_ref.at[s], st=st_s.at[s], nw=nw_ref, dk=dk)


def _mlstm_constants():
    half = LANES // 2
    full0 = half * MLSTM_HEADS
    e = np.zeros((LANES, full0 + 2 * LANES * MLSTM_HEADS), np.float32)
    diff0 = full0 + LANES * MLSTM_HEADS
    sel = np.zeros((16, LANES), np.float32)
    for h in range(MLSTM_HEADS):
        e[SM_F + h, h * half:(h + 1) * half] = 1.0
        e[SM_F + h, full0 + h * LANES:full0 + (h + 1) * LANES] = 1.0
        e[SM_I + h, diff0 + h * LANES:diff0 + (h + 1) * LANES] = 1.0
        e[SM_F + h, diff0 + h * LANES:diff0 + (h + 1) * LANES] = -1.0
        sel[h // 2, SM_I + h] = 1.0
        sel[h // 2, SM_F + h] = -1.0
    return (jnp.asarray(np.concatenate([e, e], axis=0), BF16),
            jnp.asarray(np.concatenate([sel, sel], axis=1), BF16))


def _mlstm_stages(nw_ref, tri_ref, exp_ref, sel_ref):
    lane = _lane_iota()
    is_f = (lane >= SM_F) & (lane < SM_F + MLSTM_HEADS)

    half = LANES // 2
    ri = lax.broadcasted_iota(jnp.int32, (CHUNK, LANES), 0)
    ci = lax.broadcasted_iota(jnp.int32, (CHUNK, LANES), 1)
    causal2 = (ci & (half - 1)) <= ri
    lo_half = lane < half
    even = (lane & 1) == 0
    v0, o0, z0 = 2 * D_BRANCH, 3 * D_BRANCH, 4 * D_BRANCH
    heads = range(MLSTM_HEADS)
    pairs = range(MLSTM_HEADS // 2)
    full0 = half * MLSTM_HEADS
    diff0 = full0 + LANES * MLSTM_HEADS

    def qk_products(d, rows):
        qk_ref = d["p"]
        qbs = [qk_ref[rows, h * LANES:(h + 1) * LANES].astype(BF16) for h in heads]
        d["ks"] = [qk_ref[rows, D_BRANCH + h * LANES:D_BRANCH + (h + 1) * LANES] for h in heads]
        kbs = [k.astype(BF16) for k in d["ks"]]
        zero = jnp.zeros((CHUNK, LANES), BF16)
        ones = jnp.ones((CHUNK, LANES), BF16)
        d["qk_raw"] = [_dot_nt(jnp.concatenate([qbs[2 * p], qbs[2 * p + 1]], axis=1),
                               _side_by_side(kbs[2 * p], kbs[2 * p + 1], zero)) for p in pairs]
        d["qc"] = [_dot(qbs[h], d["c"][h].astype(BF16)) for h in heads]
        d["vaug"] = [jnp.concatenate([d["p"][rows, v0 + h * LANES:v0 + (h + 1) * LANES].astype(BF16),
                                      ones], axis=1) for h in heads]

    def gate_sums(d, rows):
        gc = d["g"][rows, :]
        bcol = _sel_dot2(tri_ref[...], gc)
        ib = jnp.where(is_f, bcol, gc)
        d["ib"] = ib
        by_parity = jnp.concatenate([jnp.where(even, ib, 0.0), jnp.where(even, 0.0, ib)], axis=0)
        d["drows"] = _sel_dot2_nt(sel_ref[...], by_parity)

    def spread_gates(ds):
        ex = _dot_sel2(jnp.concatenate([d["ib"] for d in ds], axis=0), exp_ref[...])
        for i, d in enumerate(ds):
            d["ex"] = ex[i * CHUNK:(i + 1) * CHUNK]

    def weights(d, rows):
        ex = d["ex"]
        mx = d["m"][0:1, :]
        ss, mrs = [], []
        for p in pairs:
            bx = ex[:, p * LANES:(p + 1) * LANES]
            lw = jnp.where(causal2, bx + d["drows"][p:p + 1, :], NEG_BIG)
            mr0 = jnp.max(jnp.where(lo_half, lw, NEG_BIG), axis=-1, keepdims=True)
            mr1 = jnp.max(jnp.where(lo_half, NEG_BIG, lw), axis=-1, keepdims=True)
            m64 = jnp.where(lo_half, mx[:, 2 * p * LANES:(2 * p + 1) * LANES],
                            mx[:, (2 * p + 1) * LANES:(2 * p + 2) * LANES])
            m_row = jnp.maximum(jnp.where(lo_half, mr0, mr1), bx + m64)
            ss.append((d["qk_raw"][p] * jnp.exp(lw - m_row)).astype(BF16))
            mrs += [mr0, mr1]
        d["ss"], d["mrs"] = ss, mrs
        b_last = ex[CHUNK - 1:CHUNK, full0:diff0]
        lwe = ex[:, diff0:] + b_last
        m_new = jnp.maximum(b_last + mx, jnp.max(lwe, axis=0, keepdims=True))
        d["cd"] = jnp.exp(b_last + mx - m_new)
        d["m_new"] = m_new
        kw = jnp.exp(lwe - m_new)
        d["kws"] = [(d["ks"][h] * kw[:, h * LANES:(h + 1) * LANES]).astype(BF16) for h in heads]

    def numerators(d, rows):
        zero = jnp.zeros((CHUNK, 2 * LANES), BF16)
        d["nums"] = [_dot(d["ss"][p], _side_by_side(d["vaug"][2 * p], d["vaug"][2 * p + 1], zero))
                     for p in pairs]
        d["cups"] = [_dot_tn(d["kws"][h], d["vaug"][h]) for h in heads]

    def finish(d, rows):
        ex = d["ex"]
        mx = d["m"][0:1, :]
        for h in heads:
            p, hd = divmod(h, 2)
            hs = slice(h * LANES, (h + 1) * LANES)
            m_inter = ex[:, full0 + h * LANES:full0 + (h + 1) * LANES] + mx[:, hs]
            m_row = jnp.maximum(d["mrs"][h], m_inter)
            inter = jnp.exp(m_inter - m_row)
            sv = d["nums"][p][:, hd * 2 * LANES:(hd + 1) * 2 * LANES]
            num = sv[:, 0:LANES] + inter * d["qc"][h][:, 0:LANES]
            den = sv[:, LANES:] + inter * d["qc"][h][:, LANES:]
            hh = num / jnp.maximum(jnp.abs(den), jnp.exp(-m_row))
            cd = d["cd"][:, hs]
            d["c"][h] = jnp.concatenate([cd, cd], axis=1) * d["c"][h] + d["cups"][h]
            og = d["p"][rows, o0 + h * LANES:o0 + (h + 1) * LANES]
            _norm_gate_store(d["y"], rows, h * LANES, [og * hh], nw_ref,
                             [d["p"][rows, z0 + h * LANES:z0 + (h + 1) * LANES]])
        d["m"][0:1, :] = d["m_new"]

    return qk_products, gate_sums, (spread_gates, weights), numerators, finish


def _ssd_constants():
    e = np.zeros((LANES, SSD_HEADS * SSD_HEAD_DIM), np.float32)
    sel = np.zeros((16, LANES), np.float32)
    for h in range(SSD_HEADS):
        e[SM_DT + h, h * SSD_HEAD_DIM:(h + 1) * SSD_HEAD_DIM] = 1.0
        sel[h // 2, SM_DT + h] = 1.0
    return (jnp.asarray(np.concatenate([e, e], axis=0), BF16),
            jnp.asarray(np.concatenate([sel, sel], axis=1), BF16))


def _ssd_stages(alog_ref, dx_ref, nw_ref, tri_ref, sel_ref, exp_ref):
    lane = _lane_iota()
    is_dt = (lane >= SM_DT) & (lane < SM_DT + SSD_HEADS)

    half = LANES // 2
    ri = lax.broadcasted_iota(jnp.int32, (CHUNK, LANES), 0)
    ci = lax.broadcasted_iota(jnp.int32, (CHUNK, LANES), 1)
    causal2 = (ci & (half - 1)) <= ri
    lo_half = lane < half
    even = (lane & 1) == 0
    a_lane = jnp.where(is_dt, -jnp.exp(alog_ref[...]), 0.0)
    b0, c0, z0 = D_BRANCH, D_BRANCH + SSD_BC, D_BRANCH + 2 * SSD_BC
    groups = range(SSD_GROUPS)
    group_w = D_BRANCH // SSD_GROUPS
    pairs = range(D_BRANCH // LANES)
    pairs_per_group = group_w // LANES

    def products(d, rows):
        xbc = d["p"]
        d["bgs"] = [xbc[rows, b0 + g * SSD_STATE:b0 + (g + 1) * SSD_STATE].astype(BF16) for g in groups]
        cgbs = [xbc[rows, c0 + g * SSD_STATE:c0 + (g + 1) * SSD_STATE].astype(BF16) for g in groups]
        d["cb2"] = [_dot_nt(cgbs[g], jnp.concatenate([d["bgs"][g], d["bgs"][g]], axis=0))
                    for g in groups]
        d["cst"] = [_dot(cgbs[g], d["st"][g].astype(BF16)) for g in groups]

    def decay_sums(d, rows):
        dt = d["dt"][rows, :]
        acs = _sel_dot2(tri_ref[...], dt * a_lane)
        d["dt_acs"] = jnp.concatenate([dt, acs], axis=0)
        by_parity = jnp.concatenate([jnp.where(even, acs, 0.0), jnp.where(even, 0.0, acs)], axis=0)
        d["a_rows"] = _sel_dot2_nt(sel_ref[...], by_parity)

    def spread_decays(ds):
        ex = _dot_sel2(jnp.concatenate([d["dt_acs"] for d in ds], axis=0), exp_ref[...])
        for i, d in enumerate(ds):
            d["dtx"] = ex[2 * i * CHUNK:(2 * i + 1) * CHUNK]
            d["acs_x"] = ex[(2 * i + 1) * CHUNK:(2 * i + 2) * CHUNK]

    def decays(d, rows):
        d["xss"], d["ms"], d["xblk"], xdecs = [], [], [], []
        for p in pairs:
            ls = slice(p * LANES, (p + 1) * LANES)
            ax = d["acs_x"][:, ls]
            lmat = jnp.exp(jnp.where(causal2, ax - d["a_rows"][p:p + 1, :], NEG_BIG))
            d["ms"].append((d["cb2"][p // pairs_per_group] * lmat).astype(BF16))
            xs = d["p"][rows, ls]
            xdt = xs * d["dtx"][:, ls]
            d["xss"].append(xs)
            d["xblk"].append(jnp.concatenate([jnp.where(lo_half, xdt, 0.0), jnp.where(lo_half, 0.0, xdt)],
                                             axis=0).astype(BF16))
            xdecs.append((xdt * jnp.exp(ax[CHUNK - 1:CHUNK, :] - ax)).astype(BF16))
        d["xdec"] = [jnp.concatenate(xdecs[g * pairs_per_group:(g + 1) * pairs_per_group], axis=1)
                     for g in groups]

    def chunk_products(d, rows):
        d["yds"] = [_dot(d["ms"][p], d["xblk"][p]) for p in pairs]
        d["ups"] = [_dot_tn(d["bgs"][g], d["xdec"][g]) for g in groups]

    def finish(d, rows):
        for g in groups:
            gs = slice(g * group_w, (g + 1) * group_w)
            eax = jnp.exp(d["acs_x"][:, gs])
            d["st"][g] = d["st"][g] * eax[CHUNK - 1:CHUNK, :] + d["ups"][g]
            ys = []
            for pp in range(pairs_per_group):
                p = g * pairs_per_group + pp
                ls = slice(p * LANES, (p + 1) * LANES)
                y = (d["cst"][g][:, pp * LANES:(pp + 1) * LANES] * eax[:, pp * LANES:(pp + 1) * LANES]
                     + dx_ref[:, ls] * d["xss"][p] + d["yds"][p])
                ys.append(y * d["p"][rows, z0 + p * LANES:z0 + (p + 1) * LANES])
            _norm_gate_store(d["y"], rows, g * group_w, ys, nw_ref, [None] * len(ys))

    return products, decay_sums, (spread_decays, decays), chunk_products, finish


def _mixers_kernel(pa_ref, pb_ref, pc_ref, pd_ref, sm_ref, nwa_ref, nwb_ref, nwc_ref, nwd_ref,
                   ms_ref, mk_ref, tri_ref, mexp_ref, msel_ref, alog_ref, dx_ref, dsel_ref, dexp_ref,
                   ya_ref, yb_ref, yc_ref, yd_ref, sta_s, stc_s, cb_s, mb_s, std_s, *, tile):
    @pl.when(pl.program_id(0) == 0)
    def _():
        for r in (sta_s, stc_s, cb_s, mb_s, std_s):
            r[...] = jnp.zeros_like(r)

    gla = _gla_stages(ms_ref, mk_ref)
    mlstm = _mlstm_stages(nwb_ref, tri_ref, mexp_ref, msel_ref)
    ssd = _ssd_stages(alog_ref, dx_ref, nwd_ref, tri_ref, dsel_ref, dexp_ref)
    work = []
    for s in range(pa_ref.shape[0]):
        work.append((_gla_stream(pa_ref, s, GLA_QK, GLA_DK, ya_ref, sta_s, nwa_ref), gla))
        work.append((dict(p=pb_ref.at[s], g=sm_ref.at[s], c=cb_s.at[s], m=mb_s.at[s], y=yb_ref.at[s]),
                     mlstm))
        work.append((_gla_stream(pc_ref, s, HGRN_QF, HGRN_DK, yc_ref, stc_s, nwc_ref), gla))
        work.append((dict(p=pd_ref.at[s], dt=sm_ref.at[s], st=std_s.at[s], y=yd_ref.at[s]), ssd))
    _run_chunks(work, tile)


def _mixers(tok_inputs, const_inputs, n_batch, seq, tile):
    tok = lambda t: (0, t, 0)
    in_specs = [pl.BlockSpec((n_batch, tile, a.shape[2]), tok) for a in tok_inputs]
    for a in const_inputs:
        in_specs.append(pl.BlockSpec(a.shape, lambda t, nd=a.ndim: (0,) * nd))
    scratch = [pltpu.VMEM((n_batch, GLA_QK // LANES, GLA_DV, LANES), F32),
               pltpu.VMEM((n_batch, HGRN_QF // LANES, HGRN_DV, LANES), F32),
               pltpu.VMEM((n_batch, MLSTM_HEADS, MLSTM_DH, 2 * MLSTM_DH), F32),
               pltpu.VMEM((n_batch, 8, MLSTM_HEADS * LANES), F32),
               pltpu.VMEM((n_batch, SSD_GROUPS, SSD_STATE, D_BRANCH // SSD_GROUPS), F32)]
    return pl.pallas_call(
        functools.partial(_mixers_kernel, tile=tile), grid=(seq // tile,), in_specs=in_specs,
        out_specs=[pl.BlockSpec((n_batch, tile, D_BRANCH), tok)] * 4,
        out_shape=[jax.ShapeDtypeStruct((n_batch, seq, D_BRANCH), BF16)] * 4,
        scratch_shapes=scratch,
        compiler_params=pltpu.CompilerParams(dimension_semantics=("arbitrary",),
                                             vmem_limit_bytes=VMEM_LIMIT),
        name="mixers",
    )(*tok_inputs, *const_inputs)


def _pad_lanes(parts, total=LANES):
    width = sum(p.shape[-1] for p in parts)
    lead = parts[0].shape[:-1]
    return jnp.concatenate(list(parts) + [jnp.zeros(lead + (total - width,), parts[0].dtype)], axis=-1)


def _small_vector(i_part, f_part, dt_part):
    z = jnp.zeros((GLA_GATE_RANK,), F32)
    return _pad_lanes([z, i_part.astype(F32), f_part.astype(F32), dt_part.astype(F32)])[None, :]


def kernel(x, norm_w, w_in, gla_gate_w, gla_gate_b, gla_norm_w, ml_conv_w, ml_conv_b, ml_i_b, ml_f_b,
           ml_norm_w, hg_lb_logits, hg_norm_w, ssd_conv_w, ssd_conv_b, ssd_dt_bias, ssd_A_log, ssd_D,
           ssd_norm_w, w_out, final_norm_w):
    n_batch, seq, _ = x.shape
    depth = w_in.shape[0]
    tile = min(MIX_TILE, seq)
    n_tok = n_batch * seq

    mstack, masks = _gla_constants()
    tri_c = _tri2(CHUNK)
    ml_exp, ml_sel = _mlstm_constants()
    ssd_exp, ssd_sel = _ssd_constants()
    zero4 = jnp.zeros((MLSTM_HEADS,), F32)
    in_tile = min(IN_TILE, seq)

    p = jax.nn.softmax(hg_lb_logits.astype(F32), axis=0)
    lower_bounds = jnp.cumsum(p, axis=0) - p[0:1]

    w_wide, w_small = _realign(jnp.swapaxes(w_in, 1, 2), REALIGN_ROWS)

    h = x.reshape(n_tok, D_MODEL)
    row2 = lambda v: v.astype(F32).reshape(1, -1)
    for l in range(depth):
        gate_w = jnp.concatenate(
            [gla_gate_w[l].astype(F32), jnp.zeros((LANES - GLA_GATE_RANK, GLA_QK), F32)], axis=0)
        consts = [gate_w, row2(gla_gate_b[l]), row2(lower_bounds[l]),
                  ml_conv_w[l].astype(F32), row2(ml_conv_b[l]),
                  ssd_conv_w[l].astype(F32), row2(ssd_conv_b[l]),
                  _small_vector(ml_i_b[l], ml_f_b[l], ssd_dt_bias[l])]
        pa, pb, pc, pd, sm = [p.reshape(n_batch, seq, -1)
                              for p in _inproj(h, row2(norm_w[l]), w_wide, w_small, l, consts, in_tile,
                                               seq // in_tile)]

        ys = _mixers(
            [pa, pb, pc, pd, sm],
            [row2(gla_norm_w[l]), row2(ml_norm_w[l]), row2(hg_norm_w[l]), row2(ssd_norm_w[l]),
             mstack, masks, tri_c, ml_exp, ml_sel, _small_vector(zero4, zero4, ssd_A_log[l]),
             row2(jnp.repeat(ssd_D[l].astype(F32), SSD_HEAD_DIM)), ssd_sel, ssd_exp],
            n_batch, seq, tile)
        ys = [y.reshape(n_tok, D_BRANCH) for y in ys]
        h = _outproj(h, ys, w_out[l].astype(BF16), row2(final_norm_w), l == depth - 1,
                     min(OUT_TILE, n_tok))
    return h.reshape(n_batch, seq, D_MODEL)
```

```python
import functools

import numpy as np
import jax
import jax.numpy as jnp
from jax import lax
from jax.experimental import pallas as pl
from jax.experimental.pallas import tpu as pltpu

F32 = jnp.float32
BF16 = jnp.bfloat16

D_MODEL = 1024
D_BRANCH = 512
EPS = 1e-6
NEG_BIG = -1e30

GLA_HEADS, GLA_DK, GLA_DV = 4, 64, 128
GLA_GATE_RANK, GLA_GATE_NORM = 16, 16.0
MLSTM_HEADS, MLSTM_DH, MLSTM_CONV = 4, 128, 4
HGRN_HEADS, HGRN_DK, HGRN_DV = 4, 128, 128
SSD_HEAD_DIM, SSD_HEADS, SSD_GROUPS, SSD_STATE, SSD_CONV = 64, 8, 2, 128, 4
GLA_QK = GLA_HEADS * GLA_DK
HGRN_QF = HGRN_HEADS * HGRN_DK
SSD_BC = SSD_GROUPS * SSD_STATE
PROJ_SIZES = (
    GLA_QK, GLA_QK, D_BRANCH, GLA_GATE_RANK, D_BRANCH,
    D_BRANCH, D_BRANCH, D_BRANCH, MLSTM_HEADS, MLSTM_HEADS, D_BRANCH, D_BRANCH,
    HGRN_QF, HGRN_QF, D_BRANCH, D_BRANCH,
    D_BRANCH, SSD_BC, SSD_BC, SSD_HEADS, D_BRANCH,
)

LANES = 128
HIST = 8
VMEM_LIMIT = 56 * 1024 * 1024

REALIGN_ROWS = 256
IN_TILE = 256
MIX_TILE = 256
OUT_TILE = 512

CHUNK = 64
N_LEVELS = 6
MIN_VPU_LEVEL = 4

SM_GR, SM_I, SM_F, SM_DT = 0, 16, 20, 24

W_A = 2 * GLA_QK + 2 * D_BRANCH
W_B = 5 * D_BRANCH
W_C = 2 * HGRN_QF + 2 * D_BRANCH
W_D = 2 * D_BRANCH + 2 * SSD_BC
COL_A = 0
COL_B = COL_A + W_A
COL_C = COL_B + W_B
COL_D = COL_C + W_C
COL_S = COL_D + W_D


def _gla_constants():
    c = CHUNK
    t = np.arange(c)[:, None]
    d = np.arange(c)[None, :]
    blocks = [(d <= t), (d > t)]
    masks = [np.eye(c, dtype=bool)]
    for l in range(N_LEVELS):
        s = c >> (l + 1)
        mid_t = (t // (2 * s)) * (2 * s) + s
        upper = t >= mid_t
        if 1 < s < MIN_VPU_LEVEL:
            blocks.append(np.where(upper, (d >= mid_t) & (d <= t), (d > t) & (d <= mid_t - 1)))
        same = (t // (2 * s)) == (d // (2 * s))
        masks.append(same & upper & (d < mid_t))
    mstack = np.concatenate(blocks, axis=0).astype(np.float32)
    mstack = np.concatenate([mstack, mstack], axis=1)
    masks = np.stack(masks).astype(np.float32)
    return jnp.asarray(mstack, BF16), jnp.asarray(np.concatenate([masks, masks], axis=-1))


def _tri2(n):
    tri = np.tril(np.ones((n, n), np.float32))
    return jnp.asarray(np.concatenate([tri, tri], axis=1), BF16)


def _dot(a, b):
    return jnp.dot(a, b, preferred_element_type=F32)


def _dot_nt(a, b):
    return lax.dot_general(a, b, (((1,), (1,)), ((), ())), preferred_element_type=F32)


def _dot_tn(a, b):
    return lax.dot_general(a, b, (((0,), (0,)), ((), ())), preferred_element_type=F32)


def _sel_dot2(sel2, x):
    hi = x.astype(BF16)
    mid = (x - hi.astype(F32)).astype(BF16)
    return _dot(sel2, jnp.concatenate([hi, mid], axis=0))


def _sel_dot2_nt(sel2, x):
    hi = x.astype(BF16)
    mid = (x - hi.astype(F32)).astype(BF16)
    return _dot_nt(sel2, jnp.concatenate([hi, mid], axis=1))


def _level_exponent(gcs, s):
    pieces = []
    for b in range(0, CHUNK, 2 * s):
        ref_row = gcs[b + s - 1:b + s, :]
        if s >= 8:
            pieces += [ref_row - gcs[b:b + s], gcs[b + s:b + 2 * s] - ref_row]
        else:
            diff = gcs[b:b + 2 * s] - ref_row
            pieces.append(jnp.minimum(diff, -diff))
    return jnp.concatenate(pieces, axis=0)


def _dot_sel2(x, sel2):
    hi = x.astype(BF16)
    mid = (x - hi.astype(F32)).astype(BF16)
    return _dot(jnp.concatenate([hi, mid], axis=1), sel2)


def _dot_f32(a, b):
    ah = a.astype(BF16)
    al = (a - ah.astype(F32)).astype(BF16)
    bh = b.astype(BF16)
    bl = (b - bh.astype(F32)).astype(BF16)
    return _dot(ah, bh) + _dot(al, bh) + _dot(ah, bl)


def _softplus(x):
    return jnp.maximum(x, 0.0) + jnp.log1p(jnp.exp(-jnp.abs(x)))


def _log_sigmoid(x):
    return -_softplus(-x)


def _silu(x):
    return x * jax.nn.sigmoid(x)


def _lane_iota():
    return lax.broadcasted_iota(jnp.int32, (1, LANES), 1)


def _side_by_side(a, b, zero):
    return jnp.concatenate([jnp.concatenate([a, zero], axis=1),
                            jnp.concatenate([zero, b], axis=1)], axis=0)


def _wide_segments():
    segs, src, dst = [], 0, 0
    run_start = None
    for size in PROJ_SIZES + (0,):
        wide = size >= LANES
        if wide and run_start is None:
            run_start = src
        if not wide and run_start is not None:
            segs.append((run_start, dst, src - run_start))
            dst += src - run_start
            run_start = None
        src += size
    return segs


def _realign_kernel(wt_ref, o_ref, os_ref):
    for src, dst, n in _wide_segments():
        for r in range(0, n, LANES):
            o_ref[:, dst + r:dst + r + LANES] = wt_ref[src + r:src + r + LANES, :].T.astype(BF16)
    narrow, src = [], 0
    for size in PROJ_SIZES:
        if size < LANES:
            narrow.append(wt_ref[src:src + size, :])
        src += size
    used = sum(p.shape[0] for p in narrow)
    narrow.append(jnp.zeros((LANES - used, wt_ref.shape[1]), F32))
    os_ref[...] = jnp.concatenate(narrow, axis=0).T


def _realign(wt, rows):
    depth, n_proj, d_model = wt.shape
    return pl.pallas_call(
        _realign_kernel, grid=(depth, d_model // rows),
        in_specs=[pl.BlockSpec((None, n_proj, rows), lambda l, j: (l, 0, j))],
        out_specs=[pl.BlockSpec((None, rows, COL_S), lambda l, j: (l, j, 0)),
                   pl.BlockSpec((None, rows, LANES), lambda l, j: (l, j, 0))],
        out_shape=[jax.ShapeDtypeStruct((depth, d_model, COL_S), BF16),
                   jax.ShapeDtypeStruct((depth, d_model, LANES), F32)],
        compiler_params=pltpu.CompilerParams(dimension_semantics=("arbitrary", "arbitrary"),
                                             vmem_limit_bytes=VMEM_LIMIT),
        name="realign",
    )(wt)


CONV_ROWS, CONV_LANES = 256, 128


def _causal_conv_silu(raw, xe_ref, hist_ref, w_ref, b_ref, taps, out_ref, scale=None):
    rows, n = raw.shape
    xe_ref[0:HIST, :] = hist_ref[...]
    xe_ref[HIST:HIST + rows, :] = raw
    hist_ref[...] = raw[rows - HIST:rows, :]
    for r in range(0, rows, CONV_ROWS):
        for l in range(0, n, CONV_LANES):
            cs = slice(l, l + CONV_LANES)
            acc = b_ref[:, cs] + w_ref[taps - 1:taps, cs] * xe_ref[HIST + r:HIST + r + CONV_ROWS, cs]
            for k in range(taps - 1):
                off = HIST - (taps - 1) + k + r
                acc = acc + w_ref[k:k + 1, cs] * xe_ref[off:off + CONV_ROWS, cs]
            out = _silu(acc)
            out_ref[r:r + CONV_ROWS, cs] = out if scale is None else out * scale[:, cs]


def _inproj_kernel(x_ref, xn_ref, nw_ref, w_ref, ws_ref, gw_ref, gb_ref, lb_ref, cwb_ref, cbb_ref, cwd_ref,
                   cbd_ref, gbias_ref, oa_ref, ob_ref, oc_ref, od_ref, os_ref, xe_s, hist_b, hist_d, u_s,
                   *, tiles_per_seq):
    step = pl.program_id(0)

    @pl.when(step % tiles_per_seq == 0)
    def _():
        hist_b[...] = jnp.zeros_like(hist_b)
        hist_d[...] = jnp.zeros_like(hist_d)

    def normed(x):
        ms = jnp.mean(x * x, axis=-1, keepdims=True)
        return (x * lax.rsqrt(ms + EPS) * nw_ref[...]).astype(BF16)

    @pl.when(step == 0)
    def _():
        u_s[0] = normed(x_ref[...])

    u = u_s[step % 2]
    lane = _lane_iota()

    small = _dot(u, ws_ref[...].astype(BF16))
    biased = small + gbias_ref[...]
    is_f = (lane >= SM_F) & (lane < SM_F + MLSTM_HEADS)
    is_dt = (lane >= SM_DT) & (lane < SM_DT + SSD_HEADS)
    os_ref[...] = jnp.where(is_f, _log_sigmoid(biased), jnp.where(is_dt, _softplus(biased), biased))

    raw = _dot(u, w_ref[:, COL_B:COL_B + W_B])
    conv_w = 2 * D_BRANCH
    k_scale = jnp.where(lax.broadcasted_iota(jnp.int32, (1, conv_w), 1) < D_BRANCH, 1.0, MLSTM_DH ** -0.5)
    _causal_conv_silu(raw[:, 0:conv_w], xe_s, hist_b, cwb_ref, cbb_ref, MLSTM_CONV, ob_ref, k_scale)
    o0, z0 = 3 * D_BRANCH, 4 * D_BRANCH
    ob_ref[:, conv_w:o0] = raw[:, conv_w:o0]
    ob_ref[:, o0:z0] = jax.nn.sigmoid(raw[:, o0:z0])
    ob_ref[:, z0:W_B] = _silu(raw[:, z0:W_B])

    raw = _dot(u, w_ref[:, COL_D:COL_D + W_D])
    conv_w = D_BRANCH + 2 * SSD_BC
    _causal_conv_silu(raw[:, 0:conv_w], xe_s, hist_d, cwd_ref, cbd_ref, SSD_CONV, od_ref)
    od_ref[:, conv_w:W_D] = _silu(raw[:, conv_w:W_D])
    u_s[(step + 1) % 2] = normed(xn_ref[...])

    gate = _dot_f32(small, gw_ref[...]) + gb_ref[...]
    oa_ref[:, W_A:W_A + GLA_QK] = _log_sigmoid(gate) * (1.0 / GLA_GATE_NORM)
    z0 = 2 * GLA_QK + D_BRANCH
    oa_ref[:, z0:W_A] = _silu(_dot(u, w_ref[:, COL_A + z0:COL_A + W_A]))

    raw = _dot(u, w_ref[:, COL_C:COL_C + W_C])
    lb = lb_ref[...]
    fr = raw[:, HGRN_QF:2 * HGRN_QF]
    oc_ref[:, 0:HGRN_QF] = raw[:, 0:HGRN_QF] * (HGRN_DK ** -0.5)
    oc_ref[:, HGRN_QF:2 * HGRN_QF] = (1.0 - lb) * jax.nn.sigmoid(-fr)
    z0 = 2 * HGRN_QF + D_BRANCH
    oc_ref[:, 2 * HGRN_QF:z0] = raw[:, 2 * HGRN_QF:z0]
    oc_ref[:, z0:W_C] = _silu(raw[:, z0:W_C])
    oc_ref[:, W_C:W_C + HGRN_QF] = jnp.log(jnp.maximum(lb + (1.0 - lb) * jax.nn.sigmoid(fr), 1e-30))

    z0 = 2 * GLA_QK + D_BRANCH
    raw = _dot(u, w_ref[:, COL_A:COL_A + z0])
    oa_ref[:, 0:GLA_QK] = raw[:, 0:GLA_QK] * (GLA_DK ** -0.5)
    oa_ref[:, GLA_QK:z0] = raw[:, GLA_QK:z0]


def _inproj(h, norm_w, w_wide, w_small, layer, consts, tile, tiles_per_seq):
    n_tok = h.shape[0]
    const = lambda i: (0, 0)
    row = lambda i: (i, 0)
    n_tiles = n_tok // tile
    in_specs = [pl.BlockSpec((tile, D_MODEL), row),
                pl.BlockSpec((tile, D_MODEL), lambda i: (jnp.minimum(i + 1, n_tiles - 1), 0)),
                pl.BlockSpec((1, D_MODEL), const),
                pl.BlockSpec((None, D_MODEL, COL_S), lambda i: (layer, 0, 0), pipeline_mode=pl.Buffered(1)),
                pl.BlockSpec((None, D_MODEL, LANES), lambda i: (layer, 0, 0))]
    in_specs += [pl.BlockSpec(c.shape, const) for c in consts]
    widths = (W_A + GLA_QK, W_B, W_C + HGRN_QF, W_D, LANES)
    conv_w = 2 * D_BRANCH
    return pl.pallas_call(
        functools.partial(_inproj_kernel, tiles_per_seq=tiles_per_seq),
        grid=(n_tiles,), in_specs=in_specs,
        out_specs=[pl.BlockSpec((tile, w), row) for w in widths],
        out_shape=[jax.ShapeDtypeStruct((n_tok, w), F32) for w in widths],
        scratch_shapes=[pltpu.VMEM((tile + HIST, conv_w), F32), pltpu.VMEM((HIST, conv_w), F32),
                        pltpu.VMEM((HIST, conv_w), F32), pltpu.VMEM((2, tile, D_MODEL), BF16)],
        compiler_params=pltpu.CompilerParams(dimension_semantics=("arbitrary",),
                                             vmem_limit_bytes=VMEM_LIMIT),
        name="inproj",
    )(h, h, norm_w, w_wide, w_small, *consts)


def _outproj_kernel(h_ref, ya_ref, yb_ref, yc_ref, yd_ref, w_ref, fw_ref, o_ref, *, final):
    acc = h_ref[...]
    for i, y_ref in enumerate((ya_ref, yb_ref, yc_ref, yd_ref)):
        acc = acc + _dot(y_ref[...], w_ref[i * D_BRANCH:(i + 1) * D_BRANCH, :])
    if final:
        ms = jnp.mean(acc * acc, axis=-1, keepdims=True)
        acc = acc * lax.rsqrt(ms + EPS) * fw_ref[...]
    o_ref[...] = acc


def _outproj(h, ys, w_out, final_w, final, tile):
    n_tok = h.shape[0]
    const = lambda i: (0, 0)
    row = lambda i: (i, 0)
    in_specs = [pl.BlockSpec((tile, D_MODEL), row)]
    in_specs += [pl.BlockSpec((tile, D_BRANCH), row) for _ in ys]
    in_specs += [pl.BlockSpec(w_out.shape, const), pl.BlockSpec((1, D_MODEL), const)]
    return pl.pallas_call(
        functools.partial(_outproj_kernel, final=final),
        grid=(n_tok // tile,), in_specs=in_specs,
        out_specs=pl.BlockSpec((tile, D_MODEL), row),
        out_shape=jax.ShapeDtypeStruct((n_tok, D_MODEL), F32),
        compiler_params=pltpu.CompilerParams(dimension_semantics=("arbitrary",),
                                             vmem_limit_bytes=VMEM_LIMIT),
        name="outproj",
    )(h, *ys, w_out, final_w)


def _norm_gate_store(y_ref, rows, col0, parts, nw_ref, z_parts):
    width = sum(p.shape[-1] for p in parts)
    ss = sum(jnp.sum(p * p, axis=-1, keepdims=True) for p in parts)
    scale = lax.rsqrt(ss * (1.0 / width) + EPS)
    c = col0
    for p, z in zip(parts, z_parts):
        w = p.shape[-1]
        out = p * scale * nw_ref[:, c:c + w]
        if z is not None:
            out = out * z
        y_ref[rows, c:c + w] = out.astype(y_ref.dtype)
        c += w


N_STAGES = 5
CHUNK_UNROLL = 2


def _run_chunks(work, tile):
    def body(c, carry):
        rows = pl.ds(pl.multiple_of(c * CHUNK, CHUNK), CHUNK)
        live = [(dict(d), stages) for d, stages in work]
        for k in range(N_STAGES):
            joined = []
            for d, stages in live:
                joint, fn = stages[k] if isinstance(stages[k], tuple) else (None, stages[k])
                if joint is not None and not any(stages is s for s in joined):
                    joint([dd for dd, st in live if st is stages])
                    joined.append(stages)
                fn(d, rows)
        return carry

    lax.fori_loop(0, tile // CHUNK, body, 0, unroll=CHUNK_UNROLL)


def _gla_stages(ms_ref, mk_ref):
    lane = _lane_iota()

    def lane_mask(d, j):
        dk = d["dk"]
        return None if dk == LANES else (lane >= j * dk) & (lane < (j + 1) * dk)

    def pick(a, lm):
        return a if lm is None else jnp.where(lm, a, jnp.zeros_like(a))

    def n_groups(d):
        return d["q"].shape[1] // LANES

    def cumsums(d, rows):
        d["es"] = [_sel_dot2(ms_ref[...], d["lg"][rows, p * 2 * LANES:(p + 1) * 2 * LANES])
                   for p in range(n_groups(d) // 2)]

    def operands(d, rows):
        ops = []
        for g in range(n_groups(d)):
            ls = slice(g * LANES, (g + 1) * LANES)
            e = d["es"][g // 2][:, (g % 2) * LANES:(g % 2 + 1) * LANES]
            q = d["q"][rows, ls]
            k = d["k"][rows, ls]
            gcs = e[0:CHUNK]
            qg = (q * jnp.exp(gcs)).astype(BF16)
            kd = (k * jnp.exp(e[CHUNK:2 * CHUNK])).astype(BF16)
            dec = jnp.exp(gcs[CHUNK - 1:CHUNK, :])
            qb = q.astype(BF16)
            kb = k.astype(BF16)
            ql, kl = [q * k], [None]
            mxu_block = 2
            for l in range(N_LEVELS):
                s = CHUNK >> (l + 1)
                if s == 1:
                    ql.append(q * jnp.exp(d["lg"][rows, ls]) * pltpu.roll(k, 1, axis=0))
                    kl.append(None)
                    continue
                if s >= MIN_VPU_LEVEL:
                    el = _level_exponent(gcs, s)
                else:
                    el = e[mxu_block * CHUNK:(mxu_block + 1) * CHUNK]
                    mxu_block += 1
                w = jnp.exp(el).astype(BF16)
                ql.append(qb * w)
                kl.append(kb * w)
            ops.append((qg, kd, dec, ql, kl))
        d["ops"] = ops

    def scores(d, rows):
        heads = []
        for g in range(n_groups(d)):
            ql, kl = d["ops"][g][3], d["ops"][g][4]
            for j in range(LANES // d["dk"]):
                lm = lane_mask(d, j)
                a = None
                for l in range(N_LEVELS + 1):
                    if kl[l] is None:
                        s = jnp.sum(pick(ql[l], lm), axis=-1, keepdims=True)
                    else:
                        s = _dot_nt(pick(ql[l], lm), kl[l])
                    s = s * mk_ref[l][:, 0:CHUNK]
                    a = s if a is None else a + s
                heads.append((g, j, a.astype(BF16)))
        d["heads"] = heads

    def outputs(d, rows):
        hp = LANES // d["dk"]
        outs, upds = [], [None] * n_groups(d)
        sts = [d["st"][g] for g in range(n_groups(d))]
        stbs = [st.astype(BF16) for st in sts]
        for g, j, ab in d["heads"]:
            h = g * hp + j
            lm = lane_mask(d, j)
            qg, kd = d["ops"][g][0], d["ops"][g][1]
            vh = d["v"][rows, h * LANES:(h + 1) * LANES].astype(BF16)
            outs.append(_dot(ab, vh) + _dot_nt(pick(qg, lm), stbs[g]))
            u = _dot_tn(vh, kd)
            upds[g] = u if upds[g] is None else jnp.where(lm, u, upds[g])
        d["outs"], d["upds"], d["sts"] = outs, upds, sts

    def finish(d, rows):
        for g in range(n_groups(d)):
            d["st"][g] = d["sts"][g] * d["ops"][g][2] + d["upds"][g]
        for h, o in enumerate(d["outs"]):
            _norm_gate_store(d["y"], rows, h * LANES, [o], d["nw"],
                             [d["z"][rows, h * LANES:(h + 1) * LANES]])

    return cumsums, operands, scores, outputs, finish


def _gla_stream(p_ref, s, qk, dk, y_ref, st_s, nw_ref):
    v0, z0, lg0 = 2 * qk, 2 * qk + D_BRANCH, 2 * qk + 2 * D_BRANCH
    return dict(q=p_ref.at[s, :, 0:qk], k=p_ref.at[s, :, qk:v0], v=p_ref.at[s, :, v0:z0],
                z=p_ref.at[s, :, z0:lg0], lg=p_ref.at[s, :, lg0:lg0 + qk],
                y=y_ref.at[s], st=st_s.at[s], nw=nw_ref, dk=dk)


def _mlstm_constants():
    half = LANES // 2
    full0 = half * MLSTM_HEADS
    e = np.zeros((LANES, full0 + 2 * LANES * MLSTM_HEADS), np.float32)
    diff0 = full0 + LANES * MLSTM_HEADS
    sel = np.zeros((16, LANES), np.float32)
    for h in range(MLSTM_HEADS):
        e[SM_F + h, h * half:(h + 1) * half] = 1.0
        e[SM_F + h, full0 + h * LANES:full0 + (h + 1) * LANES] = 1.0
        e[SM_I + h, diff0 + h * LANES:diff0 + (h + 1) * LANES] = 1.0
        e[SM_F + h, diff0 + h * LANES:diff0 + (h + 1) * LANES] = -1.0
        sel[h // 2, SM_I + h] = 1.0
        sel[h // 2, SM_F + h] = -1.0
    return (jnp.asarray(np.concatenate([e, e], axis=0), BF16),
            jnp.asarray(np.concatenate([sel, sel], axis=1), BF16))


def _mlstm_stages(nw_ref, tri_ref, exp_ref, sel_ref):
    lane = _lane_iota()
    is_f = (lane >= SM_F) & (lane < SM_F + MLSTM_HEADS)

    half = LANES // 2
    ri = lax.broadcasted_iota(jnp.int32, (CHUNK, LANES), 0)
    ci = lax.broadcasted_iota(jnp.int32, (CHUNK, LANES), 1)
    causal2 = (ci & (half - 1)) <= ri
    lo_half = lane < half
    even = (lane & 1) == 0
    v0, o0, z0 = 2 * D_BRANCH, 3 * D_BRANCH, 4 * D_BRANCH
    heads = range(MLSTM_HEADS)
    pairs = range(MLSTM_HEADS // 2)
    full0 = half * MLSTM_HEADS
    diff0 = full0 + LANES * MLSTM_HEADS

    def qk_products(d, rows):
        qk_ref = d["p"]
        qbs = [qk_ref[rows, h * LANES:(h + 1) * LANES].astype(BF16) for h in heads]
        d["ks"] = [qk_ref[rows, D_BRANCH + h * LANES:D_BRANCH + (h + 1) * LANES] for h in heads]
        kbs = [k.astype(BF16) for k in d["ks"]]
        zero = jnp.zeros((CHUNK, LANES), BF16)
        ones = jnp.ones((CHUNK, LANES), BF16)
        d["qk_raw"] = [_dot_nt(jnp.concatenate([qbs[2 * p], qbs[2 * p + 1]], axis=1),
                               _side_by_side(kbs[2 * p], kbs[2 * p + 1], zero)) for p in pairs]
        d["qc"] = [_dot(qbs[h], d["c"][h].astype(BF16)) for h in heads]
        d["vaug"] = [jnp.concatenate([d["p"][rows, v0 + h * LANES:v0 + (h + 1) * LANES].astype(BF16),
                                      ones], axis=1) for h in heads]

    def gate_sums(d, rows):
        gc = d["g"][rows, :]
        bcol = _sel_dot2(tri_ref[...], gc)
        ib = jnp.where(is_f, bcol, gc)
        d["ib"] = ib
        by_parity = jnp.concatenate([jnp.where(even, ib, 0.0), jnp.where(even, 0.0, ib)], axis=0)
        d["drows"] = _sel_dot2_nt(sel_ref[...], by_parity)

    def spread_gates(ds):
        ex = _dot_sel2(jnp.concatenate([d["ib"] for d in ds], axis=0), exp_ref[...])
        for i, d in enumerate(ds):
            d["ex"] = ex[i * CHUNK:(i + 1) * CHUNK]

    def weights(d, rows):
        ex = d["ex"]
        mx = d["m"][0:1, :]
        ss, mrs = [], []
        for p in pairs:
            bx = ex[:, p * LANES:(p + 1) * LANES]
            lw = jnp.where(causal2, bx + d["drows"][p:p + 1, :], NEG_BIG)
            mr0 = jnp.max(jnp.where(lo_half, lw, NEG_BIG), axis=-1, keepdims=True)
            mr1 = jnp.max(jnp.where(lo_half, NEG_BIG, lw), axis=-1, keepdims=True)
            m64 = jnp.where(lo_half, mx[:, 2 * p * LANES:(2 * p + 1) * LANES],
                            mx[:, (2 * p + 1) * LANES:(2 * p + 2) * LANES])
            m_row = jnp.maximum(jnp.where(lo_half, mr0, mr1), bx + m64)
            ss.append((d["qk_raw"][p] * jnp.exp(lw - m_row)).astype(BF16))
            mrs += [mr0, mr1]
        d["ss"], d["mrs"] = ss, mrs
        b_last = ex[CHUNK - 1:CHUNK, full0:diff0]
        lwe = ex[:, diff0:] + b_last
        m_new = jnp.maximum(b_last + mx, jnp.max(lwe, axis=0, keepdims=True))
        d["cd"] = jnp.exp(b_last + mx - m_new)
        d["m_new"] = m_new
        kw = jnp.exp(lwe - m_new)
        d["kws"] = [(d["ks"][h] * kw[:, h * LANES:(h + 1) * LANES]).astype(BF16) for h in heads]

    def numerators(d, rows):
        zero = jnp.zeros((CHUNK, 2 * LANES), BF16)
        d["nums"] = [_dot(d["ss"][p], _side_by_side(d["vaug"][2 * p], d["vaug"][2 * p + 1], zero))
                     for p in pairs]
        d["cups"] = [_dot_tn(d["kws"][h], d["vaug"][h]) for h in heads]

    def finish(d, rows):
        ex = d["ex"]
        mx = d["m"][0:1, :]
        for h in heads:
            p, hd = divmod(h, 2)
            hs = slice(h * LANES, (h + 1) * LANES)
            m_inter = ex[:, full0 + h * LANES:full0 + (h + 1) * LANES] + mx[:, hs]
            m_row = jnp.maximum(d["mrs"][h], m_inter)
            inter = jnp.exp(m_inter - m_row)
            sv = d["nums"][p][:, hd * 2 * LANES:(hd + 1) * 2 * LANES]
            num = sv[:, 0:LANES] + inter * d["qc"][h][:, 0:LANES]
            den = sv[:, LANES:] + inter * d["qc"][h][:, LANES:]
            hh = num / jnp.maximum(jnp.abs(den), jnp.exp(-m_row))
            cd = d["cd"][:, hs]
            d["c"][h] = jnp.concatenate([cd, cd], axis=1) * d["c"][h] + d["cups"][h]
            og = d["p"][rows, o0 + h * LANES:o0 + (h + 1) * LANES]
            _norm_gate_store(d["y"], rows, h * LANES, [og * hh], nw_ref,
                             [d["p"][rows, z0 + h * LANES:z0 + (h + 1) * LANES]])
        d["m"][0:1, :] = d["m_new"]

    return qk_products, gate_sums, (spread_gates, weights), numerators, finish


def _ssd_constants():
    e = np.zeros((LANES, SSD_HEADS * SSD_HEAD_DIM), np.float32)
    sel = np.zeros((16, LANES), np.float32)
    for h in range(SSD_HEADS):
        e[SM_DT + h, h * SSD_HEAD_DIM:(h + 1) * SSD_HEAD_DIM] = 1.0
        sel[h // 2, SM_DT + h] = 1.0
    return (jnp.asarray(np.concatenate([e, e], axis=0), BF16),
            jnp.asarray(np.concatenate([sel, sel], axis=1), BF16))


def _ssd_stages(alog_ref, dx_ref, nw_ref, tri_ref, sel_ref, exp_ref):
    lane = _lane_iota()
    is_dt = (lane >= SM_DT) & (lane < SM_DT + SSD_HEADS)

    half = LANES // 2
    ri = lax.broadcasted_iota(jnp.int32, (CHUNK, LANES), 0)
    ci = lax.broadcasted_iota(jnp.int32, (CHUNK, LANES), 1)
    causal2 = (ci & (half - 1)) <= ri
    lo_half = lane < half
    even = (lane & 1) == 0
    a_lane = jnp.where(is_dt, -jnp.exp(alog_ref[...]), 0.0)
    b0, c0, z0 = D_BRANCH, D_BRANCH + SSD_BC, D_BRANCH + 2 * SSD_BC
    groups = range(SSD_GROUPS)
    group_w = D_BRANCH // SSD_GROUPS
    pairs = range(D_BRANCH // LANES)
    pairs_per_group = group_w // LANES

    def products(d, rows):
        xbc = d["p"]
        d["bgs"] = [xbc[rows, b0 + g * SSD_STATE:b0 + (g + 1) * SSD_STATE].astype(BF16) for g in groups]
        cgbs = [xbc[rows, c0 + g * SSD_STATE:c0 + (g + 1) * SSD_STATE].astype(BF16) for g in groups]
        d["cb2"] = [_dot_nt(cgbs[g], jnp.concatenate([d["bgs"][g], d["bgs"][g]], axis=0))
                    for g in groups]
        d["cst"] = [_dot(cgbs[g], d["st"][g].astype(BF16)) for g in groups]

    def decay_sums(d, rows):
        dt = d["dt"][rows, :]
        acs = _sel_dot2(tri_ref[...], dt * a_lane)
        d["dt_acs"] = jnp.concatenate([dt, acs], axis=0)
        by_parity = jnp.concatenate([jnp.where(even, acs, 0.0), jnp.where(even, 0.0, acs)], axis=0)
        d["a_rows"] = _sel_dot2_nt(sel_ref[...], by_parity)

    def spread_decays(ds):
        ex = _dot_sel2(jnp.concatenate([d["dt_acs"] for d in ds], axis=0), exp_ref[...])
        for i, d in enumerate(ds):
            d["dtx"] = ex[2 * i * CHUNK:(2 * i + 1) * CHUNK]
            d["acs_x"] = ex[(2 * i + 1) * CHUNK:(2 * i + 2) * CHUNK]

    def decays(d, rows):
        d["xss"], d["ms"], d["xblk"], xdecs = [], [], [], []
        for p in pairs:
            ls = slice(p * LANES, (p + 1) * LANES)
            ax = d["acs_x"][:, ls]
            lmat = jnp.exp(jnp.where(causal2, ax - d["a_rows"][p:p + 1, :], NEG_BIG))
            d["ms"].append((d["cb2"][p // pairs_per_group] * lmat).astype(BF16))
            xs = d["p"][rows, ls]
            xdt = xs * d["dtx"][:, ls]
            d["xss"].append(xs)
            d["xblk"].append(jnp.concatenate([jnp.where(lo_half, xdt, 0.0), jnp.where(lo_half, 0.0, xdt)],
                                             axis=0).astype(BF16))
            xdecs.append((xdt * jnp.exp(ax[CHUNK - 1:CHUNK, :] - ax)).astype(BF16))
        d["xdec"] = [jnp.concatenate(xdecs[g * pairs_per_group:(g + 1) * pairs_per_group], axis=1)
                     for g in groups]

    def chunk_products(d, rows):
        d["yds"] = [_dot(d["ms"][p], d["xblk"][p]) for p in pairs]
        d["ups"] = [_dot_tn(d["bgs"][g], d["xdec"][g]) for g in groups]

    def finish(d, rows):
        for g in groups:
            gs = slice(g * group_w, (g + 1) * group_w)
            eax = jnp.exp(d["acs_x"][:, gs])
            d["st"][g] = d["st"][g] * eax[CHUNK - 1:CHUNK, :] + d["ups"][g]
            ys = []
            for pp in range(pairs_per_group):
                p = g * pairs_per_group + pp
                ls = slice(p * LANES, (p + 1) * LANES)
                y = (d["cst"][g][:, pp * LANES:(pp + 1) * LANES] * eax[:, pp * LANES:(pp + 1) * LANES]
                     + dx_ref[:, ls] * d["xss"][p] + d["yds"][p])
                ys.append(y * d["p"][rows, z0 + p * LANES:z0 + (p + 1) * LANES])
            _norm_gate_store(d["y"], rows, g * group_w, ys, nw_ref, [None] * len(ys))

    return products, decay_sums, (spread_decays, decays), chunk_products, finish


def _mixers_kernel(pa_ref, pb_ref, pc_ref, pd_ref, sm_ref, nwa_ref, nwb_ref, nwc_ref, nwd_ref,
                   ms_ref, mk_ref, tri_ref, mexp_ref, msel_ref, alog_ref, dx_ref, dsel_ref, dexp_ref,
                   ya_ref, yb_ref, yc_ref, yd_ref, sta_s, stc_s, cb_s, mb_s, std_s, *, tile):
    @pl.when(pl.program_id(0) == 0)
    def _():
        for r in (sta_s, stc_s, cb_s, mb_s, std_s):
            r[...] = jnp.zeros_like(r)

    gla = _gla_stages(ms_ref, mk_ref)
    mlstm = _mlstm_stages(nwb_ref, tri_ref, mexp_ref, msel_ref)
    ssd = _ssd_stages(alog_ref, dx_ref, nwd_ref, tri_ref, dsel_ref, dexp_ref)
    work = []
    for s in range(pa_ref.shape[0]):
        work.append((_gla_stream(pa_ref, s, GLA_QK, GLA_DK, ya_ref, sta_s, nwa_ref), gla))
        work.append((dict(p=pb_ref.at[s], g=sm_ref.at[s], c=cb_s.at[s], m=mb_s.at[s], y=yb_ref.at[s]),
                     mlstm))
        work.append((_gla_stream(pc_ref, s, HGRN_QF, HGRN_DK, yc_ref, stc_s, nwc_ref), gla))
        work.append((dict(p=pd_ref.at[s], dt=sm_ref.at[s], st=std_s.at[s], y=yd_ref.at[s]), ssd))
    _run_chunks(work, tile)


def _mixers(tok_inputs, const_inputs, n_batch, seq, tile):
    tok = lambda t: (0, t, 0)
    in_specs = [pl.BlockSpec((n_batch, tile, a.shape[2]), tok) for a in tok_inputs]
    for a in const_inputs:
        in_specs.append(pl.BlockSpec(a.shape, lambda t, nd=a.ndim: (0,) * nd))
    scratch = [pltpu.VMEM((n_batch, GLA_QK // LANES, GLA_DV, LANES), F32),
               pltpu.VMEM((n_batch, HGRN_QF // LANES, HGRN_DV, LANES), F32),
               pltpu.VMEM((n_batch, MLSTM_HEADS, MLSTM_DH, 2 * MLSTM_DH), F32),
               pltpu.VMEM((n_batch, 8, MLSTM_HEADS * LANES), F32),
               pltpu.VMEM((n_batch, SSD_GROUPS, SSD_STATE, D_BRANCH // SSD_GROUPS), F32)]
    return pl.pallas_call(
        functools.partial(_mixers_kernel, tile=tile), grid=(seq // tile,), in_specs=in_specs,
        out_specs=[pl.BlockSpec((n_batch, tile, D_BRANCH), tok)] * 4,
        out_shape=[jax.ShapeDtypeStruct((n_batch, seq, D_BRANCH), BF16)] * 4,
        scratch_shapes=scratch,
        compiler_params=pltpu.CompilerParams(dimension_semantics=("arbitrary",),
                                             vmem_limit_bytes=VMEM_LIMIT),
        name="mixers",
    )(*tok_inputs, *const_inputs)


def _pad_lanes(parts, total=LANES):
    width = sum(p.shape[-1] for p in parts)
    lead = parts[0].shape[:-1]
    return jnp.concatenate(list(parts) + [jnp.zeros(lead + (total - width,), parts[0].dtype)], axis=-1)


def _small_vector(i_part, f_part, dt_part):
    z = jnp.zeros((GLA_GATE_RANK,), F32)
    return _pad_lanes([z, i_part.astype(F32), f_part.astype(F32), dt_part.astype(F32)])[None, :]


def kernel(x, norm_w, w_in, gla_gate_w, gla_gate_b, gla_norm_w, ml_conv_w, ml_conv_b, ml_i_b, ml_f_b,
           ml_norm_w, hg_lb_logits, hg_norm_w, ssd_conv_w, ssd_conv_b, ssd_dt_bias, ssd_A_log, ssd_D,
           ssd_norm_w, w_out, final_norm_w):
    n_batch, seq, _ = x.shape
    depth = w_in.shape[0]
    tile = min(MIX_TILE, seq)
    n_tok = n_batch * seq

    mstack, masks = _gla_constants()
    tri_c = _tri2(CHUNK)
    ml_exp, ml_sel = _mlstm_constants()
    ssd_exp, ssd_sel = _ssd_constants()
    zero4 = jnp.zeros((MLSTM_HEADS,), F32)
    in_tile = min(IN_TILE, seq)

    p = jax.nn.softmax(hg_lb_logits.astype(F32), axis=0)
    lower_bounds = jnp.cumsum(p, axis=0) - p[0:1]

    w_wide, w_small = _realign(jnp.swapaxes(w_in, 1, 2), REALIGN_ROWS)

    h = x.reshape(n_tok, D_MODEL)
    row2 = lambda v: v.astype(F32).reshape(1, -1)
    for l in range(depth):
        gate_w = jnp.concatenate(
            [gla_gate_w[l].astype(F32), jnp.zeros((LANES - GLA_GATE_RANK, GLA_QK), F32)], axis=0)
        consts = [gate_w, row2(gla_gate_b[l]), row2(lower_bounds[l]),
                  ml_conv_w[l].astype(F32), row2(ml_conv_b[l]),
                  ssd_conv_w[l].astype(F32), row2(ssd_conv_b[l]),
                  _small_vector(ml_i_b[l], ml_f_b[l], ssd_dt_bias[l])]
        pa, pb, pc, pd, sm = [p.reshape(n_batch, seq, -1)
                              for p in _inproj(h, row2(norm_w[l]), w_wide, w_small, l, consts, in_tile,
                                               seq // in_tile)]

        ys = _mixers(
            [pa, pb, pc, pd, sm],
            [row2(gla_norm_w[l]), row2(ml_norm_w[l]), row2(hg_norm_w[l]), row2(ssd_norm_w[l]),
             mstack, masks, tri_c, ml_exp, ml_sel, _small_vector(zero4, zero4, ssd_A_log[l]),
             row2(jnp.repeat(ssd_D[l].astype(F32), SSD_HEAD_DIM)), ssd_sel, ssd_exp],
            n_batch, seq, tile)
        ys = [y.reshape(n_tok, D_BRANCH) for y in ys]
        h = _outproj(h, ys, w_out[l].astype(BF16), row2(final_norm_w), l == depth - 1,
                     min(OUT_TILE, n_tok))
    return h.reshape(n_batch, seq, D_MODEL)
```

```python
import functools

import numpy as np
import jax
import jax.numpy as jnp
from jax import lax
from jax.experimental import pallas as pl
from jax.experimental.pallas import tpu as pltpu

F32 = jnp.float32
BF16 = jnp.bfloat16

D_MODEL = 1024
D_BRANCH = 512
EPS = 1e-6
NEG_BIG = -1e30

GLA_HEADS, GLA_DK, GLA_DV = 4, 64, 128
GLA_GATE_RANK, GLA_GATE_NORM = 16, 16.0
MLSTM_HEADS, MLSTM_DH, MLSTM_CONV = 4, 128, 4
HGRN_HEADS, HGRN_DK, HGRN_DV = 4, 128, 128
SSD_HEAD_DIM, SSD_HEADS, SSD_GROUPS, SSD_STATE, SSD_CONV = 64, 8, 2, 128, 4
GLA_QK = GLA_HEADS * GLA_DK
HGRN_QF = HGRN_HEADS * HGRN_DK
SSD_BC = SSD_GROUPS * SSD_STATE
PROJ_SIZES = (
    GLA_QK, GLA_QK, D_BRANCH, GLA_GATE_RANK, D_BRANCH,
    D_BRANCH, D_BRANCH, D_BRANCH, MLSTM_HEADS, MLSTM_HEADS, D_BRANCH, D_BRANCH,
    HGRN_QF, HGRN_QF, D_BRANCH, D_BRANCH,
    D_BRANCH, SSD_BC, SSD_BC, SSD_HEADS, D_BRANCH,
)

LANES = 128
HIST = 8
VMEM_LIMIT = 56 * 1024 * 1024

REALIGN_ROWS = 256
IN_TILE = 256
MIX_TILE = 256
OUT_TILE = 512

CHUNK = 64
N_LEVELS = 6
MIN_VPU_LEVEL = 4

SM_GR, SM_I, SM_F, SM_DT = 0, 16, 20, 24

W_A = 2 * GLA_QK + 2 * D_BRANCH
W_B = 5 * D_BRANCH
W_C = 2 * HGRN_QF + 2 * D_BRANCH
W_D = 2 * D_BRANCH + 2 * SSD_BC
COL_A = 0
COL_B = COL_A + W_A
COL_C = COL_B + W_B
COL_D = COL_C + W_C
COL_S = COL_D + W_D


def _gla_constants():
    c = CHUNK
    t = np.arange(c)[:, None]
    d = np.arange(c)[None, :]
    blocks = [(d <= t), (d > t)]
    masks = [np.eye(c, dtype=bool)]
    for l in range(N_LEVELS):
        s = c >> (l + 1)
        mid_t = (t // (2 * s)) * (2 * s) + s
        upper = t >= mid_t
        if 1 < s < MIN_VPU_LEVEL:
            blocks.append(np.where(upper, (d >= mid_t) & (d <= t), (d > t) & (d <= mid_t - 1)))
        same = (t // (2 * s)) == (d // (2 * s))
        masks.append(same & upper & (d < mid_t))
    mstack = np.concatenate(blocks, axis=0).astype(np.float32)
    mstack = np.concatenate([mstack, mstack], axis=1)
    masks = np.stack(masks).astype(np.float32)
    return jnp.asarray(mstack, BF16), jnp.asarray(np.concatenate([masks, masks], axis=-1))


def _tri2(n):
    tri = np.tril(np.ones((n, n), np.float32))
    return jnp.asarray(np.concatenate([tri, tri], axis=1), BF16)


def _dot(a, b):
    return jnp.dot(a, b, preferred_element_type=F32)


def _dot_nt(a, b):
    return lax.dot_general(a, b, (((1,), (1,)), ((), ())), preferred_element_type=F32)


def _dot_tn(a, b):
    return lax.dot_general(a, b, (((0,), (0,)), ((), ())), preferred_element_type=F32)


def _sel_dot2(sel2, x):
    hi = x.astype(BF16)
    mid = (x - hi.astype(F32)).astype(BF16)
    return _dot(sel2, jnp.concatenate([hi, mid], axis=0))


def _sel_dot2_nt(sel2, x):
    hi = x.astype(BF16)
    mid = (x - hi.astype(F32)).astype(BF16)
    return _dot_nt(sel2, jnp.concatenate([hi, mid], axis=1))


def _level_exponent(gcs, s):
    pieces = []
    for b in range(0, CHUNK, 2 * s):
        ref_row = gcs[b + s - 1:b + s, :]
        if s >= 8:
            pieces += [ref_row - gcs[b:b + s], gcs[b + s:b + 2 * s] - ref_row]
        else:
            diff = gcs[b:b + 2 * s] - ref_row
            pieces.append(jnp.minimum(diff, -diff))
    return jnp.concatenate(pieces, axis=0)


def _dot_sel2(x, sel2):
    hi = x.astype(BF16)
    mid = (x - hi.astype(F32)).astype(BF16)
    return _dot(jnp.concatenate([hi, mid], axis=1), sel2)


def _dot_f32(a, b):
    ah = a.astype(BF16)
    al = (a - ah.astype(F32)).astype(BF16)
    bh = b.astype(BF16)
    bl = (b - bh.astype(F32)).astype(BF16)
    return _dot(ah, bh) + _dot(al, bh) + _dot(ah, bl)


def _softplus(x):
    return jnp.maximum(x, 0.0) + jnp.log1p(jnp.exp(-jnp.abs(x)))


def _log_sigmoid(x):
    return -_softplus(-x)


def _silu(x):
    return x * jax.nn.sigmoid(x)


def _lane_iota():
    return lax.broadcasted_iota(jnp.int32, (1, LANES), 1)


def _side_by_side(a, b, zero):
    return jnp.concatenate([jnp.concatenate([a, zero], axis=1),
                            jnp.concatenate([zero, b], axis=1)], axis=0)


def _wide_segments():
    segs, src, dst = [], 0, 0
    run_start = None
    for size in PROJ_SIZES + (0,):
        wide = size >= LANES
        if wide and run_start is None:
            run_start = src
        if not wide and run_start is not None:
            segs.append((run_start, dst, src - run_start))
            dst += src - run_start
            run_start = None
        src += size
    return segs


def _realign_kernel(wt_ref, o_ref, os_ref):
    for src, dst, n in _wide_segments():
        for r in range(0, n, LANES):
            o_ref[:, dst + r:dst + r + LANES] = wt_ref[src + r:src + r + LANES, :].T.astype(BF16)
    narrow, src = [], 0
    for size in PROJ_SIZES:
        if size < LANES:
            narrow.append(wt_ref[src:src + size, :])
        src += size
    used = sum(p.shape[0] for p in narrow)
    narrow.append(jnp.zeros((LANES - used, wt_ref.shape[1]), F32))
    os_ref[...] = jnp.concatenate(narrow, axis=0).T


def _realign(wt, rows):
    depth, n_proj, d_model = wt.shape
    return pl.pallas_call(
        _realign_kernel, grid=(depth, d_model // rows),
        in_specs=[pl.BlockSpec((None, n_proj, rows), lambda l, j: (l, 0, j))],
        out_specs=[pl.BlockSpec((None, rows, COL_S), lambda l, j: (l, j, 0)),
                   pl.BlockSpec((None, rows, LANES), lambda l, j: (l, j, 0))],
        out_shape=[jax.ShapeDtypeStruct((depth, d_model, COL_S), BF16),
                   jax.ShapeDtypeStruct((depth, d_model, LANES), F32)],
        compiler_params=pltpu.CompilerParams(dimension_semantics=("arbitrary", "arbitrary"),
                                             vmem_limit_bytes=VMEM_LIMIT),
        name="realign",
    )(wt)


CONV_ROWS, CONV_LANES = 256, 128


def _causal_conv_silu(raw, xe_ref, hist_ref, w_ref, b_ref, taps, out_ref, scale=None):
    rows, n = raw.shape
    xe_ref[0:HIST, :] = hist_ref[...]
    xe_ref[HIST:HIST + rows, :] = raw
    hist_ref[...] = raw[rows - HIST:rows, :]
    for r in range(0, rows, CONV_ROWS):
        for l in range(0, n, CONV_LANES):
            cs = slice(l, l + CONV_LANES)
            acc = b_ref[:, cs] + w_ref[taps - 1:taps, cs] * xe_ref[HIST + r:HIST + r + CONV_ROWS, cs]
            for k in range(taps - 1):
                off = HIST - (taps - 1) + k + r
                acc = acc + w_ref[k:k + 1, cs] * xe_ref[off:off + CONV_ROWS, cs]
            out = _silu(acc)
            out_ref[r:r + CONV_ROWS, cs] = out if scale is None else out * scale[:, cs]


def _inproj_kernel(x_ref, xn_ref, nw_ref, w_ref, ws_ref, gw_ref, gb_ref, lb_ref, cwb_ref, cbb_ref, cwd_ref,
                   cbd_ref, gbias_ref, oa_ref, ob_ref, oc_ref, od_ref, os_ref, xe_s, hist_b, hist_d, u_s,
                   *, tiles_per_seq):
    step = pl.program_id(0)

    @pl.when(step % tiles_per_seq == 0)
    def _():
        hist_b[...] = jnp.zeros_like(hist_b)
        hist_d[...] = jnp.zeros_like(hist_d)

    def normed(x):
        ms = jnp.mean(x * x, axis=-1, keepdims=True)
        return (x * lax.rsqrt(ms + EPS) * nw_ref[...]).astype(BF16)

    @pl.when(step == 0)
    def _():
        u_s[0] = normed(x_ref[...])

    u = u_s[step % 2]
    lane = _lane_iota()

    small = _dot(u, ws_ref[...].astype(BF16))
    biased = small + gbias_ref[...]
    is_f = (lane >= SM_F) & (lane < SM_F + MLSTM_HEADS)
    is_dt = (lane >= SM_DT) & (lane < SM_DT + SSD_HEADS)
    os_ref[...] = jnp.where(is_f, _log_sigmoid(biased), jnp.where(is_dt, _softplus(biased), biased))

    raw = _dot(u, w_ref[:, COL_B:COL_B + W_B])
    conv_w = 2 * D_BRANCH
    k_scale = jnp.where(lax.broadcasted_iota(jnp.int32, (1, conv_w), 1) < D_BRANCH, 1.0, MLSTM_DH ** -0.5)
    _causal_conv_silu(raw[:, 0:conv_w], xe_s, hist_b, cwb_ref, cbb_ref, MLSTM_CONV, ob_ref, k_scale)
    o0, z0 = 3 * D_BRANCH, 4 * D_BRANCH
    ob_ref[:, conv_w:o0] = raw[:, conv_w:o0]
    ob_ref[:, o0:z0] = jax.nn.sigmoid(raw[:, o0:z0])
    ob_ref[:, z0:W_B] = _silu(raw[:, z0:W_B])

    raw = _dot(u, w_ref[:, COL_D:COL_D + W_D])
    conv_w = D_BRANCH + 2 * SSD_BC
    _causal_conv_silu(raw[:, 0:conv_w], xe_s, hist_d, cwd_ref, cbd_ref, SSD_CONV, od_ref)
    od_ref[:, conv_w:W_D] = _silu(raw[:, conv_w:W_D])
    u_s[(step + 1) % 2] = normed(xn_ref[...])

    raw = _dot(u, w_ref[:, COL_C:COL_C + W_C])
    lb = lb_ref[...]
    fr = raw[:, HGRN_QF:2 * HGRN_QF]
    oc_ref[:, 0:HGRN_QF] = raw[:, 0:HGRN_QF] * (HGRN_DK ** -0.5)
    oc_ref[:, HGRN_QF:2 * HGRN_QF] = (1.0 - lb) * jax.nn.sigmoid(-fr)
    z0 = 2 * HGRN_QF + D_BRANCH
    oc_ref[:, 2 * HGRN_QF:z0] = raw[:, 2 * HGRN_QF:z0]
    oc_ref[:, z0:W_C] = _silu(raw[:, z0:W_C])
    oc_ref[:, W_C:W_C + HGRN_QF] = jnp.log(jnp.maximum(lb + (1.0 - lb) * jax.nn.sigmoid(fr), 1e-30))

    gate = _dot_f32(small, gw_ref[...]) + gb_ref[...]
    oa_ref[:, W_A:W_A + GLA_QK] = _log_sigmoid(gate) * (1.0 / GLA_GATE_NORM)
    raw = _dot(u, w_ref[:, COL_A:COL_A + W_A])
    oa_ref[:, 0:GLA_QK] = raw[:, 0:GLA_QK] * (GLA_DK ** -0.5)
    z0 = 2 * GLA_QK + D_BRANCH
    oa_ref[:, GLA_QK:z0] = raw[:, GLA_QK:z0]
    oa_ref[:, z0:W_A] = _silu(raw[:, z0:W_A])


def _inproj(h, norm_w, w_wide, w_small, layer, consts, tile, tiles_per_seq):
    n_tok = h.shape[0]
    const = lambda i: (0, 0)
    row = lambda i: (i, 0)
    n_tiles = n_tok // tile
    in_specs = [pl.BlockSpec((tile, D_MODEL), row),
                pl.BlockSpec((tile, D_MODEL), lambda i: (jnp.minimum(i + 1, n_tiles - 1), 0)),
                pl.BlockSpec((1, D_MODEL), const),
                pl.BlockSpec((None, D_MODEL, COL_S), lambda i: (layer, 0, 0), pipeline_mode=pl.Buffered(1)),
                pl.BlockSpec((None, D_MODEL, LANES), lambda i: (layer, 0, 0))]
    in_specs += [pl.BlockSpec(c.shape, const) for c in consts]
    widths = (W_A + GLA_QK, W_B, W_C + HGRN_QF, W_D, LANES)
    conv_w = 2 * D_BRANCH
    return pl.pallas_call(
        functools.partial(_inproj_kernel, tiles_per_seq=tiles_per_seq),
        grid=(n_tiles,), in_specs=in_specs,
        out_specs=[pl.BlockSpec((tile, w), row) for w in widths],
        out_shape=[jax.ShapeDtypeStruct((n_tok, w), F32) for w in widths],
        scratch_shapes=[pltpu.VMEM((tile + HIST, conv_w), F32), pltpu.VMEM((HIST, conv_w), F32),
                        pltpu.VMEM((HIST, conv_w), F32), pltpu.VMEM((2, tile, D_MODEL), BF16)],
        compiler_params=pltpu.CompilerParams(dimension_semantics=("arbitrary",),
                                             vmem_limit_bytes=VMEM_LIMIT),
        name="inproj",
    )(h, h, norm_w, w_wide, w_small, *consts)


def _outproj_kernel(h_ref, ya_ref, yb_ref, yc_ref, yd_ref, w_ref, fw_ref, o_ref, *, final):
    acc = h_ref[...]
    for i, y_ref in enumerate((ya_ref, yb_ref, yc_ref, yd_ref)):
        acc = acc + _dot(y_ref[...], w_ref[i * D_BRANCH:(i + 1) * D_BRANCH, :])
    if final:
        ms = jnp.mean(acc * acc, axis=-1, keepdims=True)
        acc = acc * lax.rsqrt(ms + EPS) * fw_ref[...]
    o_ref[...] = acc


def _outproj(h, ys, w_out, final_w, final, tile):
    n_tok = h.shape[0]
    const = lambda i: (0, 0)
    row = lambda i: (i, 0)
    in_specs = [pl.BlockSpec((tile, D_MODEL), row)]
    in_specs += [pl.BlockSpec((tile, D_BRANCH), row) for _ in ys]
    in_specs += [pl.BlockSpec(w_out.shape, const), pl.BlockSpec((1, D_MODEL), const)]
    return pl.pallas_call(
        functools.partial(_outproj_kernel, final=final),
        grid=(n_tok // tile,), in_specs=in_specs,
        out_specs=pl.BlockSpec((tile, D_MODEL), row),
        out_shape=jax.ShapeDtypeStruct((n_tok, D_MODEL), F32),
        compiler_params=pltpu.CompilerParams(dimension_semantics=("arbitrary",),
                                             vmem_limit_bytes=VMEM_LIMIT),
        name="outproj",
    )(h, *ys, w_out, final_w)


def _norm_gate_store(y_ref, rows, col0, parts, nw_ref, z_parts):
    width = sum(p.shape[-1] for p in parts)
    ss = sum(jnp.sum(p * p, axis=-1, keepdims=True) for p in parts)
    scale = lax.rsqrt(ss * (1.0 / width) + EPS)
    c = col0
    for p, z in zip(parts, z_parts):
        w = p.shape[-1]
        out = p * scale * nw_ref[:, c:c + w]
        if z is not None:
            out = out * z
        y_ref[rows, c:c + w] = out.astype(y_ref.dtype)
        c += w


N_STAGES = 5
CHUNK_UNROLL = 1


def _run_chunks(work, tile):
    def body(c, carry):
        rows = pl.ds(pl.multiple_of(c * CHUNK, CHUNK), CHUNK)
        live = [(dict(d), stages) for d, stages in work]
        for k in range(N_STAGES):
            joined = []
            for d, stages in live:
                joint, fn = stages[k] if isinstance(stages[k], tuple) else (None, stages[k])
                if joint is not None and not any(stages is s for s in joined):
                    joint([dd for dd, st in live if st is stages])
                    joined.append(stages)
                fn(d, rows)
        return carry

    lax.fori_loop(0, tile // CHUNK, body, 0, unroll=CHUNK_UNROLL)


def _gla_stages(ms_ref, mk_ref):
    lane = _lane_iota()

    def lane_mask(d, j):
        dk = d["dk"]
        return None if dk == LANES else (lane >= j * dk) & (lane < (j + 1) * dk)

    def pick(a, lm):
        return a if lm is None else jnp.where(lm, a, jnp.zeros_like(a))

    def n_groups(d):
        return d["q"].shape[1] // LANES

    def cumsums(d, rows):
        d["es"] = [_sel_dot2(ms_ref[...], d["lg"][rows, p * 2 * LANES:(p + 1) * 2 * LANES])
                   for p in range(n_groups(d) // 2)]

    def operands(d, rows):
        ops = []
        for g in range(n_groups(d)):
            ls = slice(g * LANES, (g + 1) * LANES)
            e = d["es"][g // 2][:, (g % 2) * LANES:(g % 2 + 1) * LANES]
            q = d["q"][rows, ls]
            k = d["k"][rows, ls]
            gcs = e[0:CHUNK]
            qg = (q * jnp.exp(gcs)).astype(BF16)
            kd = (k * jnp.exp(e[CHUNK:2 * CHUNK])).astype(BF16)
            dec = jnp.exp(gcs[CHUNK - 1:CHUNK, :])
            qb = q.astype(BF16)
            kb = k.astype(BF16)
            ql, kl = [q * k], [None]
            mxu_block = 2
            for l in range(N_LEVELS):
                s = CHUNK >> (l + 1)
                if s == 1:
                    ql.append(q * jnp.exp(d["lg"][rows, ls]) * pltpu.roll(k, 1, axis=0))
                    kl.append(None)
                    continue
                if s >= MIN_VPU_LEVEL:
                    el = _level_exponent(gcs, s)
                else:
                    el = e[mxu_block * CHUNK:(mxu_block + 1) * CHUNK]
                    mxu_block += 1
                w = jnp.exp(el).astype(BF16)
                ql.append(qb * w)
                kl.append(kb * w)
            ops.append((qg, kd, dec, ql, kl))
        d["ops"] = ops

    def stacked(d, a):
        hp = LANES // d["dk"]
        return a if hp == 1 else jnp.concatenate([pick(a, lane_mask(d, j)) for j in range(hp)], axis=0)

    def scores(d, rows):
        hp = LANES // d["dk"]
        mask_rows = lambda l: jnp.concatenate([mk_ref[l][:, 0:CHUNK]] * hp, axis=0)
        groups = []
        for g in range(n_groups(d)):
            ql, kl = d["ops"][g][3], d["ops"][g][4]
            a = None
            for l in range(N_LEVELS + 1):
                if kl[l] is None:
                    s = jnp.sum(stacked(d, ql[l]), axis=-1, keepdims=True)
                else:
                    s = _dot_nt(stacked(d, ql[l]), kl[l])
                s = s * mask_rows(l)
                a = s if a is None else a + s
            groups.append(a.astype(BF16))
        d["scores"] = groups

    def outputs(d, rows):
        hp = LANES // d["dk"]
        outs, upds = [], [None] * n_groups(d)
        sts = [d["st"][g] for g in range(n_groups(d))]
        for g in range(n_groups(d)):
            qg, kd = d["ops"][g][0], d["ops"][g][1]
            inter = _dot_nt(stacked(d, qg), sts[g].astype(BF16))
            for j in range(hp):
                h = g * hp + j
                hr = slice(j * CHUNK, (j + 1) * CHUNK)
                vh = d["v"][rows, h * LANES:(h + 1) * LANES].astype(BF16)
                outs.append(_dot(d["scores"][g][hr], vh) + inter[hr])
                u = _dot_tn(vh, kd)
                upds[g] = u if upds[g] is None else jnp.where(lane_mask(d, j), u, upds[g])
        d["outs"], d["upds"], d["sts"] = outs, upds, sts

    def finish(d, rows):
        for g in range(n_groups(d)):
            d["st"][g] = d["sts"][g] * d["ops"][g][2] + d["upds"][g]
        for h, o in enumerate(d["outs"]):
            _norm_gate_store(d["y"], rows, h * LANES, [o], d["nw"],
                             [d["z"][rows, h * LANES:(h + 1) * LANES]])

    return cumsums, operands, scores, outputs, finish


def _gla_stream(p_ref, s, qk, dk, y_ref, st_s, nw_ref):
    v0, z0, lg0 = 2 * qk, 2 * qk + D_BRANCH, 2 * qk + 2 * D_BRANCH
    return dict(q=p_ref.at[s, :, 0:qk], k=p_ref.at[s, :, qk:v0], v=p_ref.at[s, :, v0:z0],
                z=p_ref.at[s, :, z0:lg0], lg=p_ref.at[s, :, lg0:lg0 + qk],
                y=y_ref.at[s], st=st_s.at[s], nw=nw_ref, dk=dk)


def _mlstm_constants():
    half = LANES // 2
    full0 = half * MLSTM_HEADS
    e = np.zeros((LANES, full0 + 2 * LANES * MLSTM_HEADS), np.float32)
    diff0 = full0 + LANES * MLSTM_HEADS
    sel = np.zeros((16, LANES), np.float32)
    for h in range(MLSTM_HEADS):
        e[SM_F + h, h * half:(h + 1) * half] = 1.0
        e[SM_F + h, full0 + h * LANES:full0 + (h + 1) * LANES] = 1.0
        e[SM_I + h, diff0 + h * LANES:diff0 + (h + 1) * LANES] = 1.0
        e[SM_F + h, diff0 + h * LANES:diff0 + (h + 1) * LANES] = -1.0
        sel[h // 2, SM_I + h] = 1.0
        sel[h // 2, SM_F + h] = -1.0
    return (jnp.asarray(np.concatenate([e, e], axis=0), BF16),
            jnp.asarray(np.concatenate([sel, sel], axis=1), BF16))


def _mlstm_stages(nw_ref, tri_ref, exp_ref, sel_ref):
    lane = _lane_iota()
    is_f = (lane >= SM_F) & (lane < SM_F + MLSTM_HEADS)

    half = LANES // 2
    ri = lax.broadcasted_iota(jnp.int32, (CHUNK, LANES), 0)
    ci = lax.broadcasted_iota(jnp.int32, (CHUNK, LANES), 1)
    causal2 = (ci & (half - 1)) <= ri
    lo_half = lane < half
    even = (lane & 1) == 0
    v0, o0, z0 = 2 * D_BRANCH, 3 * D_BRANCH, 4 * D_BRANCH
    heads = range(MLSTM_HEADS)
    pairs = range(MLSTM_HEADS // 2)
    full0 = half * MLSTM_HEADS
    diff0 = full0 + LANES * MLSTM_HEADS

    def qk_products(d, rows):
        qk_ref = d["p"]
        qbs = [qk_ref[rows, h * LANES:(h + 1) * LANES].astype(BF16) for h in heads]
        d["ks"] = [qk_ref[rows, D_BRANCH + h * LANES:D_BRANCH + (h + 1) * LANES] for h in heads]
        kbs = [k.astype(BF16) for k in d["ks"]]
        zero = jnp.zeros((CHUNK, LANES), BF16)
        ones = jnp.ones((CHUNK, LANES), BF16)
        d["qk_raw"] = [_dot_nt(jnp.concatenate([qbs[2 * p], qbs[2 * p + 1]], axis=1),
                               _side_by_side(kbs[2 * p], kbs[2 * p + 1], zero)) for p in pairs]
        d["qc"] = [_dot(qbs[h], d["c"][h].astype(BF16)) for h in heads]
        d["vaug"] = [jnp.concatenate([d["p"][rows, v0 + h * LANES:v0 + (h + 1) * LANES].astype(BF16),
                                      ones], axis=1) for h in heads]

    def gate_sums(d, rows):
        gc = d["g"][rows, :]
        bcol = _sel_dot2(tri_ref[...], gc)
        ib = jnp.where(is_f, bcol, gc)
        d["ib"] = ib
        by_parity = jnp.concatenate([jnp.where(even, ib, 0.0), jnp.where(even, 0.0, ib)], axis=0)
        d["drows"] = _sel_dot2_nt(sel_ref[...], by_parity)

    def spread_gates(ds):
        ex = _dot_sel2(jnp.concatenate([d["ib"] for d in ds], axis=0), exp_ref[...])
        for i, d in enumerate(ds):
            d["ex"] = ex[i * CHUNK:(i + 1) * CHUNK]

    def weights(d, rows):
        ex = d["ex"]
        mx = d["m"][0:1, :]
        ss, mrs = [], []
        for p in pairs:
            bx = ex[:, p * LANES:(p + 1) * LANES]
            lw = jnp.where(causal2, bx + d["drows"][p:p + 1, :], NEG_BIG)
            mr0 = jnp.max(jnp.where(lo_half, lw, NEG_BIG), axis=-1, keepdims=True)
            mr1 = jnp.max(jnp.where(lo_half, NEG_BIG, lw), axis=-1, keepdims=True)
            m64 = jnp.where(lo_half, mx[:, 2 * p * LANES:(2 * p + 1) * LANES],
                            mx[:, (2 * p + 1) * LANES:(2 * p + 2) * LANES])
            m_row = jnp.maximum(jnp.where(lo_half, mr0, mr1), bx + m64)
            ss.append((d["qk_raw"][p] * jnp.exp(lw - m_row)).astype(BF16))
            mrs += [mr0, mr1]
        d["ss"], d["mrs"] = ss, mrs
        b_last = ex[CHUNK - 1:CHUNK, full0:diff0]
        lwe = ex[:, diff0:] + b_last
        m_new = jnp.maximum(b_last + mx, jnp.max(lwe, axis=0, keepdims=True))
        d["cd"] = jnp.exp(b_last + mx - m_new)
        d["m_new"] = m_new
        kw = jnp.exp(lwe - m_new)
        d["kws"] = [(d["ks"][h] * kw[:, h * LANES:(h + 1) * LANES]).astype(BF16) for h in heads]

    def numerators(d, rows):
        zero = jnp.zeros((CHUNK, 2 * LANES), BF16)
        d["nums"] = [_dot(d["ss"][p], _side_by_side(d["vaug"][2 * p], d["vaug"][2 * p + 1], zero))
                     for p in pairs]
        d["cups"] = [_dot_tn(d["kws"][h], d["vaug"][h]) for h in heads]

    def finish(d, rows):
        ex = d["ex"]
        mx = d["m"][0:1, :]
        for h in heads:
            p, hd = divmod(h, 2)
            hs = slice(h * LANES, (h + 1) * LANES)
            m_inter = ex[:, full0 + h * LANES:full0 + (h + 1) * LANES] + mx[:, hs]
            m_row = jnp.maximum(d["mrs"][h], m_inter)
            inter = jnp.exp(m_inter - m_row)
            sv = d["nums"][p][:, hd * 2 * LANES:(hd + 1) * 2 * LANES]
            num = sv[:, 0:LANES] + inter * d["qc"][h][:, 0:LANES]
            den = sv[:, LANES:] + inter * d["qc"][h][:, LANES:]
            hh = num / jnp.maximum(jnp.abs(den), jnp.exp(-m_row))
            cd = d["cd"][:, hs]
            d["c"][h] = jnp.concatenate([cd, cd], axis=1) * d["c"][h] + d["cups"][h]
            og = d["p"][rows, o0 + h * LANES:o0 + (h + 1) * LANES]
            _norm_gate_store(d["y"], rows, h * LANES, [og * hh], nw_ref,
                             [d["p"][rows, z0 + h * LANES:z0 + (h + 1) * LANES]])
        d["m"][0:1, :] = d["m_new"]

    return qk_products, gate_sums, (spread_gates, weights), numerators, finish


def _ssd_constants():
    e = np.zeros((LANES, SSD_HEADS * SSD_HEAD_DIM), np.float32)
    sel = np.zeros((16, LANES), np.float32)
    for h in range(SSD_HEADS):
        e[SM_DT + h, h * SSD_HEAD_DIM:(h + 1) * SSD_HEAD_DIM] = 1.0
        sel[h // 2, SM_DT + h] = 1.0
    return (jnp.asarray(np.concatenate([e, e], axis=0), BF16),
            jnp.asarray(np.concatenate([sel, sel], axis=1), BF16))


def _ssd_stages(alog_ref, dx_ref, nw_ref, tri_ref, sel_ref, exp_ref):
    lane = _lane_iota()
    is_dt = (lane >= SM_DT) & (lane < SM_DT + SSD_HEADS)

    half = LANES // 2
    ri = lax.broadcasted_iota(jnp.int32, (CHUNK, LANES), 0)
    ci = lax.broadcasted_iota(jnp.int32, (CHUNK, LANES), 1)
    causal2 = (ci & (half - 1)) <= ri
    lo_half = lane < half
    even = (lane & 1) == 0
    a_lane = jnp.where(is_dt, -jnp.exp(alog_ref[...]), 0.0)
    b0, c0, z0 = D_BRANCH, D_BRANCH + SSD_BC, D_BRANCH + 2 * SSD_BC
    groups = range(SSD_GROUPS)
    group_w = D_BRANCH // SSD_GROUPS
    pairs = range(D_BRANCH // LANES)
    pairs_per_group = group_w // LANES

    def products(d, rows):
        xbc = d["p"]
        d["bgs"] = [xbc[rows, b0 + g * SSD_STATE:b0 + (g + 1) * SSD_STATE].astype(BF16) for g in groups]
        cgbs = [xbc[rows, c0 + g * SSD_STATE:c0 + (g + 1) * SSD_STATE].astype(BF16) for g in groups]
        d["cb2"] = [_dot_nt(cgbs[g], jnp.concatenate([d["bgs"][g], d["bgs"][g]], axis=0))
                    for g in groups]
        d["cst"] = [_dot(cgbs[g], d["st"][g].astype(BF16)) for g in groups]

    def decay_sums(d, rows):
        dt = d["dt"][rows, :]
        acs = _sel_dot2(tri_ref[...], dt * a_lane)
        d["dt_acs"] = jnp.concatenate([dt, acs], axis=0)
        by_parity = jnp.concatenate([jnp.where(even, acs, 0.0), jnp.where(even, 0.0, acs)], axis=0)
        d["a_rows"] = _sel_dot2_nt(sel_ref[...], by_parity)

    def spread_decays(ds):
        ex = _dot_sel2(jnp.concatenate([d["dt_acs"] for d in ds], axis=0), exp_ref[...])
        for i, d in enumerate(ds):
            d["dtx"] = ex[2 * i * CHUNK:(2 * i + 1) * CHUNK]
            d["acs_x"] = ex[(2 * i + 1) * CHUNK:(2 * i + 2) * CHUNK]

    def decays(d, rows):
        d["xss"], d["ms"], d["xblk"], xdecs = [], [], [], []
        for p in pairs:
            ls = slice(p * LANES, (p + 1) * LANES)
            ax = d["acs_x"][:, ls]
            lmat = jnp.exp(jnp.where(causal2, ax - d["a_rows"][p:p + 1, :], NEG_BIG))
            d["ms"].append((d["cb2"][p // pairs_per_group] * lmat).astype(BF16))
            xs = d["p"][rows, ls]
            xdt = xs * d["dtx"][:, ls]
            d["xss"].append(xs)
            d["xblk"].append(jnp.concatenate([jnp.where(lo_half, xdt, 0.0), jnp.where(lo_half, 0.0, xdt)],
                                             axis=0).astype(BF16))
            xdecs.append((xdt * jnp.exp(ax[CHUNK - 1:CHUNK, :] - ax)).astype(BF16))
        d["xdec"] = [jnp.concatenate(xdecs[g * pairs_per_group:(g + 1) * pairs_per_group], axis=1)
                     for g in groups]

    def chunk_products(d, rows):
        d["yds"] = [_dot(d["ms"][p], d["xblk"][p]) for p in pairs]
        d["ups"] = [_dot_tn(d["bgs"][g], d["xdec"][g]) for g in groups]

    def finish(d, rows):
        for g in groups:
            gs = slice(g * group_w, (g + 1) * group_w)
            eax = jnp.exp(d["acs_x"][:, gs])
            d["st"][g] = d["st"][g] * eax[CHUNK - 1:CHUNK, :] + d["ups"][g]
            ys = []
            for pp in range(pairs_per_group):
                p = g * pairs_per_group + pp
                ls = slice(p * LANES, (p + 1) * LANES)
                y = (d["cst"][g][:, pp * LANES:(pp + 1) * LANES] * eax[:, pp * LANES:(pp + 1) * LANES]
                     + dx_ref[:, ls] * d["xss"][p] + d["yds"][p])
                ys.append(y * d["p"][rows, z0 + p * LANES:z0 + (p + 1) * LANES])
            _norm_gate_store(d["y"], rows, g * group_w, ys, nw_ref, [None] * len(ys))

    return products, decay_sums, (spread_decays, decays), chunk_products, finish


def _mixers_kernel(pa_ref, pb_ref, pc_ref, pd_ref, sm_ref, nwa_ref, nwb_ref, nwc_ref, nwd_ref,
                   ms_ref, mk_ref, tri_ref, mexp_ref, msel_ref, alog_ref, dx_ref, dsel_ref, dexp_ref,
                   ya_ref, yb_ref, yc_ref, yd_ref, sta_s, stc_s, cb_s, mb_s, std_s, *, tile):
    @pl.when(pl.program_id(0) == 0)
    def _():
        for r in (sta_s, stc_s, cb_s, mb_s, std_s):
            r[...] = jnp.zeros_like(r)

    gla = _gla_stages(ms_ref, mk_ref)
    mlstm = _mlstm_stages(nwb_ref, tri_ref, mexp_ref, msel_ref)
    ssd = _ssd_stages(alog_ref, dx_ref, nwd_ref, tri_ref, dsel_ref, dexp_ref)
    work = []
    for s in range(pa_ref.shape[0]):
        work.append((_gla_stream(pa_ref, s, GLA_QK, GLA_DK, ya_ref, sta_s, nwa_ref), gla))
        work.append((dict(p=pb_ref.at[s], g=sm_ref.at[s], c=cb_s.at[s], m=mb_s.at[s], y=yb_ref.at[s]),
                     mlstm))
        work.append((_gla_stream(pc_ref, s, HGRN_QF, HGRN_DK, yc_ref, stc_s, nwc_ref), gla))
        work.append((dict(p=pd_ref.at[s], dt=sm_ref.at[s], st=std_s.at[s], y=yd_ref.at[s]), ssd))
    _run_chunks(work, tile)


def _mixers(tok_inputs, const_inputs, n_batch, seq, tile):
    tok = lambda t: (0, t, 0)
    in_specs = [pl.BlockSpec((n_batch, tile, a.shape[2]), tok) for a in tok_inputs]
    for a in const_inputs:
        in_specs.append(pl.BlockSpec(a.shape, lambda t, nd=a.ndim: (0,) * nd))
    scratch = [pltpu.VMEM((n_batch, GLA_QK // LANES, GLA_DV, LANES), F32),
               pltpu.VMEM((n_batch, HGRN_QF // LANES, HGRN_DV, LANES), F32),
               pltpu.VMEM((n_batch, MLSTM_HEADS, MLSTM_DH, 2 * MLSTM_DH), F32),
               pltpu.VMEM((n_batch, 8, MLSTM_HEADS * LANES), F32),
               pltpu.VMEM((n_batch, SSD_GROUPS, SSD_STATE, D_BRANCH // SSD_GROUPS), F32)]
    return pl.pallas_call(
        functools.partial(_mixers_kernel, tile=tile), grid=(seq // tile,), in_specs=in_specs,
        out_specs=[pl.BlockSpec((n_batch, tile, D_BRANCH), tok)] * 4,
        out_shape=[jax.ShapeDtypeStruct((n_batch, seq, D_BRANCH), BF16)] * 4,
        scratch_shapes=scratch,
        compiler_params=pltpu.CompilerParams(dimension_semantics=("arbitrary",),
                                             vmem_limit_bytes=VMEM_LIMIT),
        name="mixers",
    )(*tok_inputs, *const_inputs)


def _pad_lanes(parts, total=LANES):
    width = sum(p.shape[-1] for p in parts)
    lead = parts[0].shape[:-1]
    return jnp.concatenate(list(parts) + [jnp.zeros(lead + (total - width,), parts[0].dtype)], axis=-1)


def _small_vector(i_part, f_part, dt_part):
    z = jnp.zeros((GLA_GATE_RANK,), F32)
    return _pad_lanes([z, i_part.astype(F32), f_part.astype(F32), dt_part.astype(F32)])[None, :]


def kernel(x, norm_w, w_in, gla_gate_w, gla_gate_b, gla_norm_w, ml_conv_w, ml_conv_b, ml_i_b, ml_f_b,
           ml_norm_w, hg_lb_logits, hg_norm_w, ssd_conv_w, ssd_conv_b, ssd_dt_bias, ssd_A_log, ssd_D,
           ssd_norm_w, w_out, final_norm_w):
    n_batch, seq, _ = x.shape
    depth = w_in.shape[0]
    tile = min(MIX_TILE, seq)
    n_tok = n_batch * seq

    mstack, masks = _gla_constants()
    tri_c = _tri2(CHUNK)
    ml_exp, ml_sel = _mlstm_constants()
    ssd_exp, ssd_sel = _ssd_constants()
    zero4 = jnp.zeros((MLSTM_HEADS,), F32)
    in_tile = min(IN_TILE, seq)

    p = jax.nn.softmax(hg_lb_logits.astype(F32), axis=0)
    lower_bounds = jnp.cumsum(p, axis=0) - p[0:1]

    w_wide, w_small = _realign(jnp.swapaxes(w_in, 1, 2), REALIGN_ROWS)

    h = x.reshape(n_tok, D_MODEL)
    row2 = lambda v: v.astype(F32).reshape(1, -1)
    for l in range(depth):
        gate_w = jnp.concatenate(
            [gla_gate_w[l].astype(F32), jnp.zeros((LANES - GLA_GATE_RANK, GLA_QK), F32)], axis=0)
        consts = [gate_w, row2(gla_gate_b[l]), row2(lower_bounds[l]),
                  ml_conv_w[l].astype(F32), row2(ml_conv_b[l]),
                  ssd_conv_w[l].astype(F32), row2(ssd_conv_b[l]),
                  _small_vector(ml_i_b[l], ml_f_b[l], ssd_dt_bias[l])]
        pa, pb, pc, pd, sm = [p.reshape(n_batch, seq, -1)
                              for p in _inproj(h, row2(norm_w[l]), w_wide, w_small, l, consts, in_tile,
                                               seq // in_tile)]

        ys = _mixers(
            [pa, pb, pc, pd, sm],
            [row2(gla_norm_w[l]), row2(ml_norm_w[l]), row2(hg_norm_w[l]), row2(ssd_norm_w[l]),
             mstack, masks, tri_c, ml_exp, ml_sel, _small_vector(zero4, zero4, ssd_A_log[l]),
             row2(jnp.repeat(ssd_D[l].astype(F32), SSD_HEAD_DIM)), ssd_sel, ssd_exp],
            n_batch, seq, tile)
        ys = [y.reshape(n_tok, D_BRANCH) for y in ys]
        h = _outproj(h, ys, w_out[l].astype(BF16), row2(final_norm_w), l == depth - 1,
                     min(OUT_TILE, n_tok))
    return h.reshape(n_batch, seq, D_MODEL)
```

```python
import functools

import numpy as np
import jax
import jax.numpy as jnp
from jax import lax
from jax.experimental import pallas as pl
from jax.experimental.pallas import tpu as pltpu

F32 = jnp.float32
BF16 = jnp.bfloat16

D_MODEL = 1024
D_BRANCH = 512
EPS = 1e-6
NEG_BIG = -1e30

GLA_HEADS, GLA_DK, GLA_DV = 4, 64, 128
GLA_GATE_RANK, GLA_GATE_NORM = 16, 16.0
MLSTM_HEADS, MLSTM_DH, MLSTM_CONV = 4, 128, 4
HGRN_HEADS, HGRN_DK, HGRN_DV = 4, 128, 128
SSD_HEAD_DIM, SSD_HEADS, SSD_GROUPS, SSD_STATE, SSD_CONV = 64, 8, 2, 128, 4
GLA_QK = GLA_HEADS * GLA_DK
HGRN_QF = HGRN_HEADS * HGRN_DK
SSD_BC = SSD_GROUPS * SSD_STATE
PROJ_SIZES = (
    GLA_QK, GLA_QK, D_BRANCH, GLA_GATE_RANK, D_BRANCH,
    D_BRANCH, D_BRANCH, D_BRANCH, MLSTM_HEADS, MLSTM_HEADS, D_BRANCH, D_BRANCH,
    HGRN_QF, HGRN_QF, D_BRANCH, D_BRANCH,
    D_BRANCH, SSD_BC, SSD_BC, SSD_HEADS, D_BRANCH,
)

LANES = 128
HIST = 8
VMEM_LIMIT = 56 * 1024 * 1024

REALIGN_ROWS = 256
IN_TILE = 256
MIX_TILE = 256
OUT_TILE = 512

CHUNK = 64
N_LEVELS = 6
MIN_VPU_LEVEL = 4

SM_GR, SM_I, SM_F, SM_DT = 0, 16, 20, 24

W_A = 2 * GLA_QK + 2 * D_BRANCH
W_B = 5 * D_BRANCH
W_C = 2 * HGRN_QF + 2 * D_BRANCH
W_D = 2 * D_BRANCH + 2 * SSD_BC
COL_A = 0
COL_B = COL_A + W_A
COL_C = COL_B + W_B
COL_D = COL_C + W_C
COL_S = COL_D + W_D


def _gla_constants():
    c = CHUNK
    t = np.arange(c)[:, None]
    d = np.arange(c)[None, :]
    blocks = [(d <= t), (d > t)]
    masks = [np.eye(c, dtype=bool)]
    for l in range(N_LEVELS):
        s = c >> (l + 1)
        mid_t = (t // (2 * s)) * (2 * s) + s
        upper = t >= mid_t
        if 1 < s < MIN_VPU_LEVEL:
            blocks.append(np.where(upper, (d >= mid_t) & (d <= t), (d > t) & (d <= mid_t - 1)))
        same = (t // (2 * s)) == (d // (2 * s))
        masks.append(same & upper & (d < mid_t))
    mstack = np.concatenate(blocks, axis=0).astype(np.float32)
    mstack = np.concatenate([mstack, mstack], axis=1)
    masks = np.stack(masks).astype(np.float32)
    return jnp.asarray(mstack, BF16), jnp.asarray(np.concatenate([masks, masks], axis=-1))


def _tri2(n):
    tri = np.tril(np.ones((n, n), np.float32))
    return jnp.asarray(np.concatenate([tri, tri], axis=1), BF16)


def _dot(a, b):
    return jnp.dot(a, b, preferred_element_type=F32)


def _dot_nt(a, b):
    return lax.dot_general(a, b, (((1,), (1,)), ((), ())), preferred_element_type=F32)


def _dot_tn(a, b):
    return lax.dot_general(a, b, (((0,), (0,)), ((), ())), preferred_element_type=F32)


def _sel_dot2(sel2, x):
    hi = x.astype(BF16)
    mid = (x - hi.astype(F32)).astype(BF16)
    return _dot(sel2, jnp.concatenate([hi, mid], axis=0))


def _sel_dot2_nt(sel2, x):
    hi = x.astype(BF16)
    mid = (x - hi.astype(F32)).astype(BF16)
    return _dot_nt(sel2, jnp.concatenate([hi, mid], axis=1))


def _level_exponent(gcs, s):
    pieces = []
    for b in range(0, CHUNK, 2 * s):
        ref_row = gcs[b + s - 1:b + s, :]
        if s >= 8:
            pieces += [ref_row - gcs[b:b + s], gcs[b + s:b + 2 * s] - ref_row]
        else:
            diff = gcs[b:b + 2 * s] - ref_row
            pieces.append(jnp.minimum(diff, -diff))
    return jnp.concatenate(pieces, axis=0)


def _dot_sel2(x, sel2):
    hi = x.astype(BF16)
    mid = (x - hi.astype(F32)).astype(BF16)
    return _dot(jnp.concatenate([hi, mid], axis=1), sel2)


def _dot_f32(a, b):
    ah = a.astype(BF16)
    al = (a - ah.astype(F32)).astype(BF16)
    bh = b.astype(BF16)
    bl = (b - bh.astype(F32)).astype(BF16)
    return _dot(ah, bh) + _dot(al, bh) + _dot(ah, bl)


def _softplus(x):
    return jnp.maximum(x, 0.0) + jnp.log1p(jnp.exp(-jnp.abs(x)))


def _log_sigmoid(x):
    return -_softplus(-x)


def _silu(x):
    return x * jax.nn.sigmoid(x)


def _lane_iota():
    return lax.broadcasted_iota(jnp.int32, (1, LANES), 1)


def _side_by_side(a, b, zero):
    return jnp.concatenate([jnp.concatenate([a, zero], axis=1),
                            jnp.concatenate([zero, b], axis=1)], axis=0)


def _wide_segments():
    segs, src, dst = [], 0, 0
    run_start = None
    for size in PROJ_SIZES + (0,):
        wide = size >= LANES
        if wide and run_start is None:
            run_start = src
        if not wide and run_start is not None:
            segs.append((run_start, dst, src - run_start))
            dst += src - run_start
            run_start = None
        src += size
    return segs


def _realign_kernel(wt_ref, o_ref, os_ref):
    for src, dst, n in _wide_segments():
        for r in range(0, n, LANES):
            o_ref[:, dst + r:dst + r + LANES] = wt_ref[src + r:src + r + LANES, :].T.astype(BF16)
    narrow, src = [], 0
    for size in PROJ_SIZES:
        if size < LANES:
            narrow.append(wt_ref[src:src + size, :])
        src += size
    used = sum(p.shape[0] for p in narrow)
    narrow.append(jnp.zeros((LANES - used, wt_ref.shape[1]), F32))
    os_ref[...] = jnp.concatenate(narrow, axis=0).T


def _realign(wt, rows):
    depth, n_proj, d_model = wt.shape
    return pl.pallas_call(
        _realign_kernel, grid=(depth, d_model // rows),
        in_specs=[pl.BlockSpec((None, n_proj, rows), lambda l, j: (l, 0, j))],
        out_specs=[pl.BlockSpec((None, rows, COL_S), lambda l, j: (l, j, 0)),
                   pl.BlockSpec((None, rows, LANES), lambda l, j: (l, j, 0))],
        out_shape=[jax.ShapeDtypeStruct((depth, d_model, COL_S), BF16),
                   jax.ShapeDtypeStruct((depth, d_model, LANES), F32)],
        compiler_params=pltpu.CompilerParams(dimension_semantics=("arbitrary", "arbitrary"),
                                             vmem_limit_bytes=VMEM_LIMIT),
        name="realign",
    )(wt)


CONV_ROWS, CONV_LANES = 256, 128


def _causal_conv_silu(raw, xe_ref, hist_ref, w_ref, b_ref, taps, out_ref, scale=None):
    rows, n = raw.shape
    xe_ref[0:HIST, :] = hist_ref[...]
    xe_ref[HIST:HIST + rows, :] = raw
    hist_ref[...] = raw[rows - HIST:rows, :]
    for r in range(0, rows, CONV_ROWS):
        for l in range(0, n, CONV_LANES):
            cs = slice(l, l + CONV_LANES)
            acc = b_ref[:, cs] + w_ref[taps - 1:taps, cs] * xe_ref[HIST + r:HIST + r + CONV_ROWS, cs]
            for k in range(taps - 1):
                off = HIST - (taps - 1) + k + r
                acc = acc + w_ref[k:k + 1, cs] * xe_ref[off:off + CONV_ROWS, cs]
            out = _silu(acc)
            out_ref[r:r + CONV_ROWS, cs] = out if scale is None else out * scale[:, cs]


def _inproj_kernel(x_ref, xn_ref, nw_ref, w_ref, ws_ref, gw_ref, gb_ref, lb_ref, cwb_ref, cbb_ref, cwd_ref,
                   cbd_ref, gbias_ref, oa_ref, ob_ref, oc_ref, od_ref, os_ref, xe_s, hist_b, hist_d, u_s,
                   *, tiles_per_seq):
    step = pl.program_id(0)

    @pl.when(step % tiles_per_seq == 0)
    def _():
        hist_b[...] = jnp.zeros_like(hist_b)
        hist_d[...] = jnp.zeros_like(hist_d)

    def normed(x):
        ms = jnp.mean(x * x, axis=-1, keepdims=True)
        return (x * lax.rsqrt(ms + EPS) * nw_ref[...]).astype(BF16)

    @pl.when(step == 0)
    def _():
        u_s[0] = normed(x_ref[...])

    u = u_s[step % 2]
    lane = _lane_iota()

    small = _dot(u, ws_ref[...].astype(BF16))
    biased = small + gbias_ref[...]
    is_f = (lane >= SM_F) & (lane < SM_F + MLSTM_HEADS)
    is_dt = (lane >= SM_DT) & (lane < SM_DT + SSD_HEADS)
    os_ref[...] = jnp.where(is_f, _log_sigmoid(biased), jnp.where(is_dt, _softplus(biased), biased))

    raw = _dot(u, w_ref[:, COL_B:COL_B + W_B])
    conv_w = 2 * D_BRANCH
    k_scale = jnp.where(lax.broadcasted_iota(jnp.int32, (1, conv_w), 1) < D_BRANCH, 1.0, MLSTM_DH ** -0.5)
    _causal_conv_silu(raw[:, 0:conv_w], xe_s, hist_b, cwb_ref, cbb_ref, MLSTM_CONV, ob_ref, k_scale)
    o0, z0 = 3 * D_BRANCH, 4 * D_BRANCH
    ob_ref[:, conv_w:o0] = raw[:, conv_w:o0]
    ob_ref[:, o0:z0] = jax.nn.sigmoid(raw[:, o0:z0])
    ob_ref[:, z0:W_B] = _silu(raw[:, z0:W_B])

    raw = _dot(u, w_ref[:, COL_D:COL_D + W_D])
    conv_w = D_BRANCH + 2 * SSD_BC
    _causal_conv_silu(raw[:, 0:conv_w], xe_s, hist_d, cwd_ref, cbd_ref, SSD_CONV, od_ref)
    od_ref[:, conv_w:W_D] = _silu(raw[:, conv_w:W_D])
    u_s[(step + 1) % 2] = normed(xn_ref[...])

    raw = _dot(u, w_ref[:, COL_C:COL_C + W_C])
    lb = lb_ref[...]
    fr = raw[:, HGRN_QF:2 * HGRN_QF]
    oc_ref[:, 0:HGRN_QF] = raw[:, 0:HGRN_QF] * (HGRN_DK ** -0.5)
    oc_ref[:, HGRN_QF:2 * HGRN_QF] = (1.0 - lb) * jax.nn.sigmoid(-fr)
    z0 = 2 * HGRN_QF + D_BRANCH
    oc_ref[:, 2 * HGRN_QF:z0] = raw[:, 2 * HGRN_QF:z0]
    oc_ref[:, z0:W_C] = _silu(raw[:, z0:W_C])
    oc_ref[:, W_C:W_C + HGRN_QF] = jnp.log(jnp.maximum(lb + (1.0 - lb) * jax.nn.sigmoid(fr), 1e-30))

    gate = _dot_f32(small, gw_ref[...]) + gb_ref[...]
    oa_ref[:, W_A:W_A + GLA_QK] = _log_sigmoid(gate) * (1.0 / GLA_GATE_NORM)
    raw = _dot(u, w_ref[:, COL_A:COL_A + W_A])
    oa_ref[:, 0:GLA_QK] = raw[:, 0:GLA_QK] * (GLA_DK ** -0.5)
    z0 = 2 * GLA_QK + D_BRANCH
    oa_ref[:, GLA_QK:z0] = raw[:, GLA_QK:z0]
    oa_ref[:, z0:W_A] = _silu(raw[:, z0:W_A])


def _inproj(h, norm_w, w_wide, w_small, layer, consts, tile, tiles_per_seq):
    n_tok = h.shape[0]
    const = lambda i: (0, 0)
    row = lambda i: (i, 0)
    n_tiles = n_tok // tile
    in_specs = [pl.BlockSpec((tile, D_MODEL), row),
                pl.BlockSpec((tile, D_MODEL), lambda i: (jnp.minimum(i + 1, n_tiles - 1), 0)),
                pl.BlockSpec((1, D_MODEL), const),
                pl.BlockSpec((None, D_MODEL, COL_S), lambda i: (layer, 0, 0), pipeline_mode=pl.Buffered(1)),
                pl.BlockSpec((None, D_MODEL, LANES), lambda i: (layer, 0, 0))]
    in_specs += [pl.BlockSpec(c.shape, const) for c in consts]
    widths = (W_A + GLA_QK, W_B, W_C + HGRN_QF, W_D, LANES)
    conv_w = 2 * D_BRANCH
    return pl.pallas_call(
        functools.partial(_inproj_kernel, tiles_per_seq=tiles_per_seq),
        grid=(n_tiles,), in_specs=in_specs,
        out_specs=[pl.BlockSpec((tile, w), row) for w in widths],
        out_shape=[jax.ShapeDtypeStruct((n_tok, w), F32) for w in widths],
        scratch_shapes=[pltpu.VMEM((tile + HIST, conv_w), F32), pltpu.VMEM((HIST, conv_w), F32),
                        pltpu.VMEM((HIST, conv_w), F32), pltpu.VMEM((2, tile, D_MODEL), BF16)],
        compiler_params=pltpu.CompilerParams(dimension_semantics=("arbitrary",),
                                             vmem_limit_bytes=VMEM_LIMIT),
        name="inproj",
    )(h, h, norm_w, w_wide, w_small, *consts)


def _outproj_kernel(h_ref, ya_ref, yb_ref, yc_ref, yd_ref, w_ref, fw_ref, o_ref, *, final):
    acc = h_ref[...]
    for i, y_ref in enumerate((ya_ref, yb_ref, yc_ref, yd_ref)):
        acc = acc + _dot(y_ref[...], w_ref[i * D_BRANCH:(i + 1) * D_BRANCH, :])
    if final:
        ms = jnp.mean(acc * acc, axis=-1, keepdims=True)
        acc = acc * lax.rsqrt(ms + EPS) * fw_ref[...]
    o_ref[...] = acc


def _outproj(h, ys, w_out, final_w, final, tile):
    n_tok = h.shape[0]
    const = lambda i: (0, 0)
    row = lambda i: (i, 0)
    in_specs = [pl.BlockSpec((tile, D_MODEL), row)]
    in_specs += [pl.BlockSpec((tile, D_BRANCH), row) for _ in ys]
    in_specs += [pl.BlockSpec(w_out.shape, const), pl.BlockSpec((1, D_MODEL), const)]
    return pl.pallas_call(
        functools.partial(_outproj_kernel, final=final),
        grid=(n_tok // tile,), in_specs=in_specs,
        out_specs=pl.BlockSpec((tile, D_MODEL), row),
        out_shape=jax.ShapeDtypeStruct((n_tok, D_MODEL), F32),
        compiler_params=pltpu.CompilerParams(dimension_semantics=("arbitrary",),
                                             vmem_limit_bytes=VMEM_LIMIT),
        name="outproj",
    )(h, *ys, w_out, final_w)


def _norm_gate_store(y_ref, rows, col0, parts, nw_ref, z_parts):
    width = sum(p.shape[-1] for p in parts)
    ss = sum(jnp.sum(p * p, axis=-1, keepdims=True) for p in parts)
    scale = lax.rsqrt(ss * (1.0 / width) + EPS)
    c = col0
    for p, z in zip(parts, z_parts):
        w = p.shape[-1]
        out = p * scale * nw_ref[:, c:c + w]
        if z is not None:
            out = out * z
        y_ref[rows, c:c + w] = out.astype(y_ref.dtype)
        c += w


N_STAGES = 5
CHUNK_UNROLL = 1


def _run_chunks(work, tile):
    def body(c, carry):
        rows = pl.ds(pl.multiple_of(c * CHUNK, CHUNK), CHUNK)
        live = [(dict(d), stages) for d, stages in work]
        for k in range(N_STAGES):
            joined = []
            for d, stages in live:
                joint, fn = stages[k] if isinstance(stages[k], tuple) else (None, stages[k])
                if joint is not None and not any(stages is s for s in joined):
                    joint([dd for dd, st in live if st is stages], rows)
                    joined.append(stages)
                fn(d, rows)
        return carry

    lax.fori_loop(0, tile // CHUNK, body, 0, unroll=CHUNK_UNROLL)


def _gla_stages(ms_ref, mk_ref):
    lane = _lane_iota()

    def lane_mask(d, j):
        dk = d["dk"]
        return None if dk == LANES else (lane >= j * dk) & (lane < (j + 1) * dk)

    def pick(a, lm):
        return a if lm is None else jnp.where(lm, a, jnp.zeros_like(a))

    def n_groups(d):
        return d["q"].shape[1] // LANES

    def cumsums(d, rows):
        d["es"] = [_sel_dot2(ms_ref[...], d["lg"][rows, p * 2 * LANES:(p + 1) * 2 * LANES])
                   for p in range(n_groups(d) // 2)]

    def operands(d, rows):
        ops = []
        for g in range(n_groups(d)):
            ls = slice(g * LANES, (g + 1) * LANES)
            e = d["es"][g // 2][:, (g % 2) * LANES:(g % 2 + 1) * LANES]
            q = d["q"][rows, ls]
            k = d["k"][rows, ls]
            gcs = e[0:CHUNK]
            qg = (q * jnp.exp(gcs)).astype(BF16)
            kd = (k * jnp.exp(e[CHUNK:2 * CHUNK])).astype(BF16)
            dec = jnp.exp(gcs[CHUNK - 1:CHUNK, :])
            qb = q.astype(BF16)
            kb = k.astype(BF16)
            ql, kl = [q * k], [None]
            mxu_block = 2
            for l in range(N_LEVELS):
                s = CHUNK >> (l + 1)
                if s == 1:
                    ql.append(q * jnp.exp(d["lg"][rows, ls]) * pltpu.roll(k, 1, axis=0))
                    kl.append(None)
                    continue
                if s >= MIN_VPU_LEVEL:
                    el = _level_exponent(gcs, s)
                else:
                    el = e[mxu_block * CHUNK:(mxu_block + 1) * CHUNK]
                    mxu_block += 1
                w = jnp.exp(el).astype(BF16)
                ql.append(qb * w)
                kl.append(kb * w)
            ops.append((qg, kd, dec, ql, kl))
        d["ops"] = ops

    def stacked(d, a):
        hp = LANES // d["dk"]
        return a if hp == 1 else jnp.concatenate([pick(a, lane_mask(d, j)) for j in range(hp)], axis=0)

    def scores(d, rows):
        hp = LANES // d["dk"]
        mask_rows = lambda l: jnp.concatenate([mk_ref[l][:, 0:CHUNK]] * hp, axis=0)
        groups = []
        for g in range(n_groups(d)):
            ql, kl = d["ops"][g][3], d["ops"][g][4]
            a = None
            for l in range(N_LEVELS + 1):
                if kl[l] is None:
                    s = jnp.sum(stacked(d, ql[l]), axis=-1, keepdims=True)
                else:
                    s = _dot_nt(stacked(d, ql[l]), kl[l])
                s = s * mask_rows(l)
                a = s if a is None else a + s
            groups.append(a.astype(BF16))
        d["scores"] = groups

    def outputs(d, rows):
        hp = LANES // d["dk"]
        outs, upds = [], [None] * n_groups(d)
        sts = [d["st"][g] for g in range(n_groups(d))]
        for g in range(n_groups(d)):
            qg, kd = d["ops"][g][0], d["ops"][g][1]
            inter = _dot_nt(stacked(d, qg), sts[g].astype(BF16))
            for j in range(hp):
                h = g * hp + j
                hr = slice(j * CHUNK, (j + 1) * CHUNK)
                vh = d["v"][rows, h * LANES:(h + 1) * LANES].astype(BF16)
                outs.append(_dot(d["scores"][g][hr], vh) + inter[hr])
                u = _dot_tn(vh, kd)
                upds[g] = u if upds[g] is None else jnp.where(lane_mask(d, j), u, upds[g])
        d["outs"], d["upds"], d["sts"] = outs, upds, sts

    def finish(d, rows):
        for g in range(n_groups(d)):
            d["st"][g] = d["sts"][g] * d["ops"][g][2] + d["upds"][g]
        for h, o in enumerate(d["outs"]):
            _norm_gate_store(d["y"], rows, h * LANES, [o], d["nw"],
                             [d["z"][rows, h * LANES:(h + 1) * LANES]])

    return cumsums, operands, scores, outputs, finish


def _gla_stream(p_ref, s, qk, dk, y_ref, st_s, nw_ref):
    v0, z0, lg0 = 2 * qk, 2 * qk + D_BRANCH, 2 * qk + 2 * D_BRANCH
    return dict(q=p_ref.at[s, :, 0:qk], k=p_ref.at[s, :, qk:v0], v=p_ref.at[s, :, v0:z0],
                z=p_ref.at[s, :, z0:lg0], lg=p_ref.at[s, :, lg0:lg0 + qk],
                y=y_ref.at[s], st=st_s.at[s], nw=nw_ref, dk=dk)


def _mlstm_constants():
    half = LANES // 2
    full0 = half * MLSTM_HEADS
    e = np.zeros((LANES, full0 + 2 * LANES * MLSTM_HEADS), np.float32)
    diff0 = full0 + LANES * MLSTM_HEADS
    sel = np.zeros((16, LANES), np.float32)
    for h in range(MLSTM_HEADS):
        e[SM_F + h, h * half:(h + 1) * half] = 1.0
        e[SM_F + h, full0 + h * LANES:full0 + (h + 1) * LANES] = 1.0
        e[SM_I + h, diff0 + h * LANES:diff0 + (h + 1) * LANES] = 1.0
        e[SM_F + h, diff0 + h * LANES:diff0 + (h + 1) * LANES] = -1.0
        sel[h // 2, SM_I + h] = 1.0
        sel[h // 2, SM_F + h] = -1.0
    return (jnp.asarray(np.concatenate([e, e], axis=0), BF16),
            jnp.asarray(np.concatenate([sel, sel], axis=1), BF16))


def _mlstm_stages(nw_ref, tri_ref, exp_ref, sel_ref):
    lane = _lane_iota()
    is_f = (lane >= SM_F) & (lane < SM_F + MLSTM_HEADS)

    half = LANES // 2
    ri = lax.broadcasted_iota(jnp.int32, (CHUNK, LANES), 0)
    ci = lax.broadcasted_iota(jnp.int32, (CHUNK, LANES), 1)
    causal2 = (ci & (half - 1)) <= ri
    lo_half = lane < half
    even = (lane & 1) == 0
    v0, o0, z0 = 2 * D_BRANCH, 3 * D_BRANCH, 4 * D_BRANCH
    heads = range(MLSTM_HEADS)
    pairs = range(MLSTM_HEADS // 2)
    full0 = half * MLSTM_HEADS
    diff0 = full0 + LANES * MLSTM_HEADS

    def qk_products(d, rows):
        qk_ref = d["p"]
        qbs = [qk_ref[rows, h * LANES:(h + 1) * LANES].astype(BF16) for h in heads]
        d["ks"] = [qk_ref[rows, D_BRANCH + h * LANES:D_BRANCH + (h + 1) * LANES] for h in heads]
        kbs = [k.astype(BF16) for k in d["ks"]]
        zero = jnp.zeros((CHUNK, LANES), BF16)
        ones = jnp.ones((CHUNK, LANES), BF16)
        d["qk_raw"] = [_dot_nt(jnp.concatenate([qbs[2 * p], qbs[2 * p + 1]], axis=1),
                               _side_by_side(kbs[2 * p], kbs[2 * p + 1], zero)) for p in pairs]
        d["qc"] = [_dot(qbs[h], d["c"][h].astype(BF16)) for h in heads]
        d["vaug"] = [jnp.concatenate([d["p"][rows, v0 + h * LANES:v0 + (h + 1) * LANES].astype(BF16),
                                      ones], axis=1) for h in heads]

    def gate_cumsums(ds, rows):
        gcs = [d["g"][rows, :] for d in ds]
        bcols = _sel_dot2(tri_ref[...], jnp.concatenate(gcs, axis=1))
        for i, d in enumerate(ds):
            d["ib"] = jnp.where(is_f, bcols[:, i * LANES:(i + 1) * LANES], gcs[i])

    def gate_sums(d, rows):
        ib = d["ib"]
        d["by_parity"] = jnp.concatenate([jnp.where(even, ib, 0.0), jnp.where(even, 0.0, ib)], axis=0)

    def spread_gates(ds, rows):
        ex = _dot_sel2(jnp.concatenate([d["ib"] for d in ds], axis=0), exp_ref[...])
        drows = _sel_dot2_nt(sel_ref[...], jnp.concatenate([d["by_parity"] for d in ds], axis=0))
        for i, d in enumerate(ds):
            d["ex"] = ex[i * CHUNK:(i + 1) * CHUNK]
            d["drows"] = drows[:, i * LANES:(i + 1) * LANES]

    def weights(d, rows):
        ex = d["ex"]
        mx = d["m"][0:1, :]
        ss, mrs = [], []
        for p in pairs:
            bx = ex[:, p * LANES:(p + 1) * LANES]
            lw = jnp.where(causal2, bx + d["drows"][p:p + 1, :], NEG_BIG)
            mr0 = jnp.max(jnp.where(lo_half, lw, NEG_BIG), axis=-1, keepdims=True)
            mr1 = jnp.max(jnp.where(lo_half, NEG_BIG, lw), axis=-1, keepdims=True)
            m64 = jnp.where(lo_half, mx[:, 2 * p * LANES:(2 * p + 1) * LANES],
                            mx[:, (2 * p + 1) * LANES:(2 * p + 2) * LANES])
            m_row = jnp.maximum(jnp.where(lo_half, mr0, mr1), bx + m64)
            ss.append((d["qk_raw"][p] * jnp.exp(lw - m_row)).astype(BF16))
            mrs += [mr0, mr1]
        d["ss"], d["mrs"] = ss, mrs
        b_last = ex[CHUNK - 1:CHUNK, full0:diff0]
        lwe = ex[:, diff0:] + b_last
        m_new = jnp.maximum(b_last + mx, jnp.max(lwe, axis=0, keepdims=True))
        d["cd"] = jnp.exp(b_last + mx - m_new)
        d["m_new"] = m_new
        kw = jnp.exp(lwe - m_new)
        d["kws"] = [(d["ks"][h] * kw[:, h * LANES:(h + 1) * LANES]).astype(BF16) for h in heads]

    def numerators(d, rows):
        zero = jnp.zeros((CHUNK, 2 * LANES), BF16)
        d["nums"] = [_dot(d["ss"][p], _side_by_side(d["vaug"][2 * p], d["vaug"][2 * p + 1], zero))
                     for p in pairs]
        d["cups"] = [_dot_tn(d["kws"][h], d["vaug"][h]) for h in heads]

    def finish(d, rows):
        ex = d["ex"]
        mx = d["m"][0:1, :]
        for h in heads:
            p, hd = divmod(h, 2)
            hs = slice(h * LANES, (h + 1) * LANES)
            m_inter = ex[:, full0 + h * LANES:full0 + (h + 1) * LANES] + mx[:, hs]
            m_row = jnp.maximum(d["mrs"][h], m_inter)
            inter = jnp.exp(m_inter - m_row)
            sv = d["nums"][p][:, hd * 2 * LANES:(hd + 1) * 2 * LANES]
            num = sv[:, 0:LANES] + inter * d["qc"][h][:, 0:LANES]
            den = sv[:, LANES:] + inter * d["qc"][h][:, LANES:]
            hh = num / jnp.maximum(jnp.abs(den), jnp.exp(-m_row))
            cd = d["cd"][:, hs]
            d["c"][h] = jnp.concatenate([cd, cd], axis=1) * d["c"][h] + d["cups"][h]
            og = d["p"][rows, o0 + h * LANES:o0 + (h + 1) * LANES]
            _norm_gate_store(d["y"], rows, h * LANES, [og * hh], nw_ref,
                             [d["p"][rows, z0 + h * LANES:z0 + (h + 1) * LANES]])
        d["m"][0:1, :] = d["m_new"]

    return qk_products, (gate_cumsums, gate_sums), (spread_gates, weights), numerators, finish


def _ssd_constants():
    e = np.zeros((LANES, SSD_HEADS * SSD_HEAD_DIM), np.float32)
    sel = np.zeros((16, LANES), np.float32)
    for h in range(SSD_HEADS):
        e[SM_DT + h, h * SSD_HEAD_DIM:(h + 1) * SSD_HEAD_DIM] = 1.0
        sel[h // 2, SM_DT + h] = 1.0
    return (jnp.asarray(np.concatenate([e, e], axis=0), BF16),
            jnp.asarray(np.concatenate([sel, sel], axis=1), BF16))


def _ssd_stages(alog_ref, dx_ref, nw_ref, tri_ref, sel_ref, exp_ref):
    lane = _lane_iota()
    is_dt = (lane >= SM_DT) & (lane < SM_DT + SSD_HEADS)

    half = LANES // 2
    ri = lax.broadcasted_iota(jnp.int32, (CHUNK, LANES), 0)
    ci = lax.broadcasted_iota(jnp.int32, (CHUNK, LANES), 1)
    causal2 = (ci & (half - 1)) <= ri
    lo_half = lane < half
    even = (lane & 1) == 0
    a_lane = jnp.where(is_dt, -jnp.exp(alog_ref[...]), 0.0)
    b0, c0, z0 = D_BRANCH, D_BRANCH + SSD_BC, D_BRANCH + 2 * SSD_BC
    groups = range(SSD_GROUPS)
    group_w = D_BRANCH // SSD_GROUPS
    pairs = range(D_BRANCH // LANES)
    pairs_per_group = group_w // LANES

    def products(d, rows):
        xbc = d["p"]
        d["bgs"] = [xbc[rows, b0 + g * SSD_STATE:b0 + (g + 1) * SSD_STATE].astype(BF16) for g in groups]
        cgbs = [xbc[rows, c0 + g * SSD_STATE:c0 + (g + 1) * SSD_STATE].astype(BF16) for g in groups]
        d["cb2"] = [_dot_nt(cgbs[g], jnp.concatenate([d["bgs"][g], d["bgs"][g]], axis=0))
                    for g in groups]
        d["cst"] = [_dot(cgbs[g], d["st"][g].astype(BF16)) for g in groups]

    def decay_cumsums(ds, rows):
        dts = [d["dt"][rows, :] for d in ds]
        acs = _sel_dot2(tri_ref[...], jnp.concatenate([dt * a_lane for dt in dts], axis=1))
        for i, d in enumerate(ds):
            d["dt_c"], d["acs"] = dts[i], acs[:, i * LANES:(i + 1) * LANES]

    def decay_sums(d, rows):
        acs = d["acs"]
        d["dt_acs"] = jnp.concatenate([d["dt_c"], acs], axis=0)
        d["by_parity"] = jnp.concatenate([jnp.where(even, acs, 0.0), jnp.where(even, 0.0, acs)], axis=0)

    def spread_decays(ds, rows):
        ex = _dot_sel2(jnp.concatenate([d["dt_acs"] for d in ds], axis=0), exp_ref[...])
        a_rows = _sel_dot2_nt(sel_ref[...], jnp.concatenate([d["by_parity"] for d in ds], axis=0))
        for i, d in enumerate(ds):
            d["dtx"] = ex[2 * i * CHUNK:(2 * i + 1) * CHUNK]
            d["acs_x"] = ex[(2 * i + 1) * CHUNK:(2 * i + 2) * CHUNK]
            d["a_rows"] = a_rows[:, i * LANES:(i + 1) * LANES]

    def decays(d, rows):
        d["xss"], d["ms"], d["xblk"], xdecs = [], [], [], []
        for p in pairs:
            ls = slice(p * LANES, (p + 1) * LANES)
            ax = d["acs_x"][:, ls]
            lmat = jnp.exp(jnp.where(causal2, ax - d["a_rows"][p:p + 1, :], NEG_BIG))
            d["ms"].append((d["cb2"][p // pairs_per_group] * lmat).astype(BF16))
            xs = d["p"][rows, ls]
            xdt = xs * d["dtx"][:, ls]
            d["xss"].append(xs)
            d["xblk"].append(jnp.concatenate([jnp.where(lo_half, xdt, 0.0), jnp.where(lo_half, 0.0, xdt)],
                                             axis=0).astype(BF16))
            xdecs.append((xdt * jnp.exp(ax[CHUNK - 1:CHUNK, :] - ax)).astype(BF16))
        d["xdec"] = [jnp.concatenate(xdecs[g * pairs_per_group:(g + 1) * pairs_per_group], axis=1)
                     for g in groups]

    def chunk_products(d, rows):
        d["yds"] = [_dot(d["ms"][p], d["xblk"][p]) for p in pairs]
        d["ups"] = [_dot_tn(d["bgs"][g], d["xdec"][g]) for g in groups]

    def finish(d, rows):
        for g in groups:
            gs = slice(g * group_w, (g + 1) * group_w)
            eax = jnp.exp(d["acs_x"][:, gs])
            d["st"][g] = d["st"][g] * eax[CHUNK - 1:CHUNK, :] + d["ups"][g]
            ys = []
            for pp in range(pairs_per_group):
                p = g * pairs_per_group + pp
                ls = slice(p * LANES, (p + 1) * LANES)
                y = (d["cst"][g][:, pp * LANES:(pp + 1) * LANES] * eax[:, pp * LANES:(pp + 1) * LANES]
                     + dx_ref[:, ls] * d["xss"][p] + d["yds"][p])
                ys.append(y * d["p"][rows, z0 + p * LANES:z0 + (p + 1) * LANES])
            _norm_gate_store(d["y"], rows, g * group_w, ys, nw_ref, [None] * len(ys))

    return products, (decay_cumsums, decay_sums), (spread_decays, decays), chunk_products, finish


def _mixers_kernel(pa_ref, pb_ref, pc_ref, pd_ref, sm_ref, nwa_ref, nwb_ref, nwc_ref, nwd_ref,
                   ms_ref, mk_ref, tri_ref, mexp_ref, msel_ref, alog_ref, dx_ref, dsel_ref, dexp_ref,
                   ya_ref, yb_ref, yc_ref, yd_ref, sta_s, stc_s, cb_s, mb_s, std_s, *, tile):
    @pl.when(pl.program_id(0) == 0)
    def _():
        for r in (sta_s, stc_s, cb_s, mb_s, std_s):
            r[...] = jnp.zeros_like(r)

    gla = _gla_stages(ms_ref, mk_ref)
    mlstm = _mlstm_stages(nwb_ref, tri_ref, mexp_ref, msel_ref)
    ssd = _ssd_stages(alog_ref, dx_ref, nwd_ref, tri_ref, dsel_ref, dexp_ref)
    work = []
    for s in range(pa_ref.shape[0]):
        work.append((_gla_stream(pa_ref, s, GLA_QK, GLA_DK, ya_ref, sta_s, nwa_ref), gla))
        work.append((dict(p=pb_ref.at[s], g=sm_ref.at[s], c=cb_s.at[s], m=mb_s.at[s], y=yb_ref.at[s]),
                     mlstm))
        work.append((_gla_stream(pc_ref, s, HGRN_QF, HGRN_DK, yc_ref, stc_s, nwc_ref), gla))
        work.append((dict(p=pd_ref.at[s], dt=sm_ref.at[s], st=std_s.at[s], y=yd_ref.at[s]), ssd))
    _run_chunks(work, tile)


def _mixers(tok_inputs, const_inputs, n_batch, seq, tile):
    tok = lambda t: (0, t, 0)
    in_specs = [pl.BlockSpec((n_batch, tile, a.shape[2]), tok) for a in tok_inputs]
    for a in const_inputs:
        in_specs.append(pl.BlockSpec(a.shape, lambda t, nd=a.ndim: (0,) * nd))
    scratch = [pltpu.VMEM((n_batch, GLA_QK // LANES, GLA_DV, LANES), F32),
               pltpu.VMEM((n_batch, HGRN_QF // LANES, HGRN_DV, LANES), F32),
               pltpu.VMEM((n_batch, MLSTM_HEADS, MLSTM_DH, 2 * MLSTM_DH), F32),
               pltpu.VMEM((n_batch, 8, MLSTM_HEADS * LANES), F32),
               pltpu.VMEM((n_batch, SSD_GROUPS, SSD_STATE, D_BRANCH // SSD_GROUPS), F32)]
    return pl.pallas_call(
        functools.partial(_mixers_kernel, tile=tile), grid=(seq // tile,), in_specs=in_specs,
        out_specs=[pl.BlockSpec((n_batch, tile, D_BRANCH), tok)] * 4,
        out_shape=[jax.ShapeDtypeStruct((n_batch, seq, D_BRANCH), BF16)] * 4,
        scratch_shapes=scratch,
        compiler_params=pltpu.CompilerParams(dimension_semantics=("arbitrary",),
                                             vmem_limit_bytes=VMEM_LIMIT),
        name="mixers",
    )(*tok_inputs, *const_inputs)


def _pad_lanes(parts, total=LANES):
    width = sum(p.shape[-1] for p in parts)
    lead = parts[0].shape[:-1]
    return jnp.concatenate(list(parts) + [jnp.zeros(lead + (total - width,), parts[0].dtype)], axis=-1)


def _small_vector(i_part, f_part, dt_part):
    z = jnp.zeros((GLA_GATE_RANK,), F32)
    return _pad_lanes([z, i_part.astype(F32), f_part.astype(F32), dt_part.astype(F32)])[None, :]


def kernel(x, norm_w, w_in, gla_gate_w, gla_gate_b, gla_norm_w, ml_conv_w, ml_conv_b, ml_i_b, ml_f_b,
           ml_norm_w, hg_lb_logits, hg_norm_w, ssd_conv_w, ssd_conv_b, ssd_dt_bias, ssd_A_log, ssd_D,
           ssd_norm_w, w_out, final_norm_w):
    n_batch, seq, _ = x.shape
    depth = w_in.shape[0]
    tile = min(MIX_TILE, seq)
    n_tok = n_batch * seq

    mstack, masks = _gla_constants()
    tri_c = _tri2(CHUNK)
    ml_exp, ml_sel = _mlstm_constants()
    ssd_exp, ssd_sel = _ssd_constants()
    zero4 = jnp.zeros((MLSTM_HEADS,), F32)
    in_tile = min(IN_TILE, seq)

    p = jax.nn.softmax(hg_lb_logits.astype(F32), axis=0)
    lower_bounds = jnp.cumsum(p, axis=0) - p[0:1]

    w_wide, w_small = _realign(jnp.swapaxes(w_in, 1, 2), REALIGN_ROWS)

    h = x.reshape(n_tok, D_MODEL)
    row2 = lambda v: v.astype(F32).reshape(1, -1)
    for l in range(depth):
        gate_w = jnp.concatenate(
            [gla_gate_w[l].astype(F32), jnp.zeros((LANES - GLA_GATE_RANK, GLA_QK), F32)], axis=0)
        consts = [gate_w, row2(gla_gate_b[l]), row2(lower_bounds[l]),
                  ml_conv_w[l].astype(F32), row2(ml_conv_b[l]),
                  ssd_conv_w[l].astype(F32), row2(ssd_conv_b[l]),
                  _small_vector(ml_i_b[l], ml_f_b[l], ssd_dt_bias[l])]
        pa, pb, pc, pd, sm = [p.reshape(n_batch, seq, -1)
                              for p in _inproj(h, row2(norm_w[l]), w_wide, w_small, l, consts, in_tile,
                                               seq // in_tile)]

        ys = _mixers(
            [pa, pb, pc, pd, sm],
            [row2(gla_norm_w[l]), row2(ml_norm_w[l]), row2(hg_norm_w[l]), row2(ssd_norm_w[l]),
             mstack, masks, tri_c, ml_exp, ml_sel, _small_vector(zero4, zero4, ssd_A_log[l]),
             row2(jnp.repeat(ssd_D[l].astype(F32), SSD_HEAD_DIM)), ssd_sel, ssd_exp],
            n_batch, seq, tile)
        ys = [y.reshape(n_tok, D_BRANCH) for y in ys]
        h = _outproj(h, ys, w_out[l].astype(BF16), row2(final_norm_w), l == depth - 1,
                     min(OUT_TILE, n_tok))
    return h.reshape(n_batch, seq, D_MODEL)
```

```python
import functools

import numpy as np
import jax
import jax.numpy as jnp
from jax import lax
from jax.experimental import pallas as pl
from jax.experimental.pallas import tpu as pltpu

F32 = jnp.float32
BF16 = jnp.bfloat16

D_MODEL = 1024
D_BRANCH = 512
EPS = 1e-6
NEG_BIG = -1e30

GLA_HEADS, GLA_DK, GLA_DV = 4, 64, 128
GLA_GATE_RANK, GLA_GATE_NORM = 16, 16.0
MLSTM_HEADS, MLSTM_DH, MLSTM_CONV = 4, 128, 4
HGRN_HEADS, HGRN_DK, HGRN_DV = 4, 128, 128
SSD_HEAD_DIM, SSD_HEADS, SSD_GROUPS, SSD_STATE, SSD_CONV = 64, 8, 2, 128, 4
GLA_QK = GLA_HEADS * GLA_DK
HGRN_QF = HGRN_HEADS * HGRN_DK
SSD_BC = SSD_GROUPS * SSD_STATE
PROJ_SIZES = (
    GLA_QK, GLA_QK, D_BRANCH, GLA_GATE_RANK, D_BRANCH,
    D_BRANCH, D_BRANCH, D_BRANCH, MLSTM_HEADS, MLSTM_HEADS, D_BRANCH, D_BRANCH,
    HGRN_QF, HGRN_QF, D_BRANCH, D_BRANCH,
    D_BRANCH, SSD_BC, SSD_BC, SSD_HEADS, D_BRANCH,
)

LANES = 128
HIST = 8
VMEM_LIMIT = 56 * 1024 * 1024

REALIGN_ROWS = 256
IN_TILE = 256
MIX_TILE = 256
OUT_TILE = 512

CHUNK = 64
N_LEVELS = 6
MIN_VPU_LEVEL = 4

SM_GR, SM_I, SM_F, SM_DT = 0, 16, 20, 24

W_A = 2 * GLA_QK + 2 * D_BRANCH
W_B = 5 * D_BRANCH
W_C = 2 * HGRN_QF + 2 * D_BRANCH
W_D = 2 * D_BRANCH + 2 * SSD_BC
COL_A = 0
COL_B = COL_A + W_A
COL_C = COL_B + W_B
COL_D = COL_C + W_C
COL_S = COL_D + W_D


def _gla_constants():
    c = CHUNK
    t = np.arange(c)[:, None]
    d = np.arange(c)[None, :]
    blocks = [(d <= t), (d > t)]
    masks = [np.eye(c, dtype=bool)]
    for l in range(N_LEVELS):
        s = c >> (l + 1)
        mid_t = (t // (2 * s)) * (2 * s) + s
        upper = t >= mid_t
        if 1 < s < MIN_VPU_LEVEL:
            blocks.append(np.where(upper, (d >= mid_t) & (d <= t), (d > t) & (d <= mid_t - 1)))
        same = (t // (2 * s)) == (d // (2 * s))
        masks.append(same & upper & (d < mid_t))
    mstack = np.concatenate(blocks, axis=0).astype(np.float32)
    mstack = np.concatenate([mstack, mstack], axis=1)
    masks = np.stack(masks).astype(np.float32)
    return jnp.asarray(mstack, BF16), jnp.asarray(np.concatenate([masks, masks], axis=-1))


def _tri2(n):
    tri = np.tril(np.ones((n, n), np.float32))
    return jnp.asarray(np.concatenate([tri, tri], axis=1), BF16)


def _dot(a, b):
    return jnp.dot(a, b, preferred_element_type=F32)


def _dot_nt(a, b):
    return lax.dot_general(a, b, (((1,), (1,)), ((), ())), preferred_element_type=F32)


def _dot_tn(a, b):
    return lax.dot_general(a, b, (((0,), (0,)), ((), ())), preferred_element_type=F32)


def _sel_dot2(sel2, x):
    hi = x.astype(BF16)
    mid = (x - hi.astype(F32)).astype(BF16)
    return _dot(sel2, jnp.concatenate([hi, mid], axis=0))


def _sel_dot2_nt(sel2, x):
    hi = x.astype(BF16)
    mid = (x - hi.astype(F32)).astype(BF16)
    return _dot_nt(sel2, jnp.concatenate([hi, mid], axis=1))


def _level_exponent(gcs, s):
    pieces = []
    for b in range(0, CHUNK, 2 * s):
        ref_row = gcs[b + s - 1:b + s, :]
        if s >= 8:
            pieces += [ref_row - gcs[b:b + s], gcs[b + s:b + 2 * s] - ref_row]
        else:
            diff = gcs[b:b + 2 * s] - ref_row
            pieces.append(jnp.minimum(diff, -diff))
    return jnp.concatenate(pieces, axis=0)


def _dot_sel2(x, sel2):
    hi = x.astype(BF16)
    mid = (x - hi.astype(F32)).astype(BF16)
    return _dot(jnp.concatenate([hi, mid], axis=1), sel2)


def _dot_f32(a, b):
    ah = a.astype(BF16)
    al = (a - ah.astype(F32)).astype(BF16)
    bh = b.astype(BF16)
    bl = (b - bh.astype(F32)).astype(BF16)
    return _dot(ah, bh) + _dot(al, bh) + _dot(ah, bl)


def _softplus(x):
    return jnp.maximum(x, 0.0) + jnp.log1p(jnp.exp(-jnp.abs(x)))


def _log_sigmoid(x):
    return -_softplus(-x)


def _silu(x):
    return x * jax.nn.sigmoid(x)


def _lane_iota():
    return lax.broadcasted_iota(jnp.int32, (1, LANES), 1)


def _side_by_side(a, b, zero):
    return jnp.concatenate([jnp.concatenate([a, zero], axis=1),
                            jnp.concatenate([zero, b], axis=1)], axis=0)


def _wide_segments():
    segs, src, dst = [], 0, 0
    run_start = None
    for size in PROJ_SIZES + (0,):
        wide = size >= LANES
        if wide and run_start is None:
            run_start = src
        if not wide and run_start is not None:
            segs.append((run_start, dst, src - run_start))
            dst += src - run_start
            run_start = None
        src += size
    return segs


def _realign_kernel(wt_ref, o_ref, os_ref):
    for src, dst, n in _wide_segments():
        for r in range(0, n, LANES):
            o_ref[:, dst + r:dst + r + LANES] = wt_ref[src + r:src + r + LANES, :].T.astype(BF16)
    narrow, src = [], 0
    for size in PROJ_SIZES:
        if size < LANES:
            narrow.append(wt_ref[src:src + size, :])
        src += size
    used = sum(p.shape[0] for p in narrow)
    narrow.append(jnp.zeros((LANES - used, wt_ref.shape[1]), F32))
    os_ref[...] = jnp.concatenate(narrow, axis=0).T


def _realign(wt, rows):
    depth, n_proj, d_model = wt.shape
    return pl.pallas_call(
        _realign_kernel, grid=(depth, d_model // rows),
        in_specs=[pl.BlockSpec((None, n_proj, rows), lambda l, j: (l, 0, j))],
        out_specs=[pl.BlockSpec((None, rows, COL_S), lambda l, j: (l, j, 0)),
                   pl.BlockSpec((None, rows, LANES), lambda l, j: (l, j, 0))],
        out_shape=[jax.ShapeDtypeStruct((depth, d_model, COL_S), BF16),
                   jax.ShapeDtypeStruct((depth, d_model, LANES), F32)],
        compiler_params=pltpu.CompilerParams(dimension_semantics=("arbitrary", "arbitrary"),
                                             vmem_limit_bytes=VMEM_LIMIT),
        name="realign",
    )(wt)


CONV_ROWS, CONV_LANES = 256, 128


def _causal_conv_silu(raw, xe_ref, hist_ref, w_ref, b_ref, taps, out_ref, scale=None):
    rows, n = raw.shape
    xe_ref[0:HIST, :] = hist_ref[...]
    xe_ref[HIST:HIST + rows, :] = raw
    hist_ref[...] = raw[rows - HIST:rows, :]
    for r in range(0, rows, CONV_ROWS):
        for l in range(0, n, CONV_LANES):
            cs = slice(l, l + CONV_LANES)
            acc = b_ref[:, cs] + w_ref[taps - 1:taps, cs] * xe_ref[HIST + r:HIST + r + CONV_ROWS, cs]
            for k in range(taps - 1):
                off = HIST - (taps - 1) + k + r
                acc = acc + w_ref[k:k + 1, cs] * xe_ref[off:off + CONV_ROWS, cs]
            out = _silu(acc)
            out_ref[r:r + CONV_ROWS, cs] = out if scale is None else out * scale[:, cs]


def _inproj_kernel(x_ref, xn_ref, nw_ref, w_ref, ws_ref, gw_ref, gb_ref, lb_ref, cwb_ref, cbb_ref, cwd_ref,
                   cbd_ref, gbias_ref, oa_ref, ob_ref, oc_ref, od_ref, os_ref, xe_s, hist_b, hist_d, u_s,
                   *, tiles_per_seq):
    step = pl.program_id(0)

    @pl.when(step % tiles_per_seq == 0)
    def _():
        hist_b[...] = jnp.zeros_like(hist_b)
        hist_d[...] = jnp.zeros_like(hist_d)

    def normed(x):
        ms = jnp.mean(x * x, axis=-1, keepdims=True)
        return (x * lax.rsqrt(ms + EPS) * nw_ref[...]).astype(BF16)

    @pl.when(step == 0)
    def _():
        u_s[0] = normed(x_ref[...])

    u = u_s[step % 2]
    lane = _lane_iota()

    small = _dot(u, ws_ref[...].astype(BF16))
    biased = small + gbias_ref[...]
    is_f = (lane >= SM_F) & (lane < SM_F + MLSTM_HEADS)
    is_dt = (lane >= SM_DT) & (lane < SM_DT + SSD_HEADS)
    os_ref[...] = jnp.where(is_f, _log_sigmoid(biased), jnp.where(is_dt, _softplus(biased), biased))

    raw = _dot(u, w_ref[:, COL_B:COL_B + W_B])
    conv_w = 2 * D_BRANCH
    k_scale = jnp.where(lax.broadcasted_iota(jnp.int32, (1, conv_w), 1) < D_BRANCH, 1.0, MLSTM_DH ** -0.5)
    _causal_conv_silu(raw[:, 0:conv_w], xe_s, hist_b, cwb_ref, cbb_ref, MLSTM_CONV, ob_ref, k_scale)
    o0, z0 = 3 * D_BRANCH, 4 * D_BRANCH
    ob_ref[:, conv_w:o0] = raw[:, conv_w:o0]
    ob_ref[:, o0:z0] = jax.nn.sigmoid(raw[:, o0:z0])
    ob_ref[:, z0:W_B] = _silu(raw[:, z0:W_B])

    raw = _dot(u, w_ref[:, COL_D:COL_D + W_D])
    conv_w = D_BRANCH + 2 * SSD_BC
    _causal_conv_silu(raw[:, 0:conv_w], xe_s, hist_d, cwd_ref, cbd_ref, SSD_CONV, od_ref)
    od_ref[:, conv_w:W_D] = _silu(raw[:, conv_w:W_D])
    u_s[(step + 1) % 2] = normed(xn_ref[...])

    raw = _dot(u, w_ref[:, COL_C:COL_C + W_C])
    lb = lb_ref[...]
    fr = raw[:, HGRN_QF:2 * HGRN_QF]
    oc_ref[:, 0:HGRN_QF] = raw[:, 0:HGRN_QF] * (HGRN_DK ** -0.5)
    oc_ref[:, HGRN_QF:2 * HGRN_QF] = (1.0 - lb) * jax.nn.sigmoid(-fr)
    z0 = 2 * HGRN_QF + D_BRANCH
    oc_ref[:, 2 * HGRN_QF:z0] = raw[:, 2 * HGRN_QF:z0]
    oc_ref[:, z0:W_C] = _silu(raw[:, z0:W_C])
    oc_ref[:, W_C:W_C + HGRN_QF] = jnp.log(jnp.maximum(lb + (1.0 - lb) * jax.nn.sigmoid(fr), 1e-30))

    gate = _dot_f32(small, gw_ref[...]) + gb_ref[...]
    oa_ref[:, W_A:W_A + GLA_QK] = _log_sigmoid(gate) * (1.0 / GLA_GATE_NORM)
    raw = _dot(u, w_ref[:, COL_A:COL_A + W_A])
    oa_ref[:, 0:GLA_QK] = raw[:, 0:GLA_QK] * (GLA_DK ** -0.5)
    z0 = 2 * GLA_QK + D_BRANCH
    oa_ref[:, GLA_QK:z0] = raw[:, GLA_QK:z0]
    oa_ref[:, z0:W_A] = _silu(raw[:, z0:W_A])


def _inproj(h, norm_w, w_wide, w_small, layer, consts, tile, tiles_per_seq):
    n_tok = h.shape[0]
    const = lambda i: (0, 0)
    row = lambda i: (i, 0)
    n_tiles = n_tok // tile
    in_specs = [pl.BlockSpec((tile, D_MODEL), row),
                pl.BlockSpec((tile, D_MODEL), lambda i: (jnp.minimum(i + 1, n_tiles - 1), 0)),
                pl.BlockSpec((1, D_MODEL), const),
                pl.BlockSpec((None, D_MODEL, COL_S), lambda i: (layer, 0, 0), pipeline_mode=pl.Buffered(1)),
                pl.BlockSpec((None, D_MODEL, LANES), lambda i: (layer, 0, 0))]
    in_specs += [pl.BlockSpec(c.shape, const) for c in consts]
    widths = (W_A + GLA_QK, W_B, W_C + HGRN_QF, W_D, LANES)
    conv_w = 2 * D_BRANCH
    return pl.pallas_call(
        functools.partial(_inproj_kernel, tiles_per_seq=tiles_per_seq),
        grid=(n_tiles,), in_specs=in_specs,
        out_specs=[pl.BlockSpec((tile, w), row) for w in widths],
        out_shape=[jax.ShapeDtypeStruct((n_tok, w), F32) for w in widths],
        scratch_shapes=[pltpu.VMEM((tile + HIST, conv_w), F32), pltpu.VMEM((HIST, conv_w), F32),
                        pltpu.VMEM((HIST, conv_w), F32), pltpu.VMEM((2, tile, D_MODEL), BF16)],
        compiler_params=pltpu.CompilerParams(dimension_semantics=("arbitrary",),
                                             vmem_limit_bytes=VMEM_LIMIT),
        name="inproj",
    )(h, h, norm_w, w_wide, w_small, *consts)


def _outproj_kernel(h_ref, ya_ref, yb_ref, yc_ref, yd_ref, w_ref, fw_ref, o_ref, *, final):
    acc = h_ref[...]
    for i, y_ref in enumerate((ya_ref, yb_ref, yc_ref, yd_ref)):
        acc = acc + _dot(y_ref[...], w_ref[i * D_BRANCH:(i + 1) * D_BRANCH, :])
    if final:
        ms = jnp.mean(acc * acc, axis=-1, keepdims=True)
        acc = acc * lax.rsqrt(ms + EPS) * fw_ref[...]
    o_ref[...] = acc


def _outproj(h, ys, w_out, final_w, final, tile):
    n_tok = h.shape[0]
    const = lambda i: (0, 0)
    row = lambda i: (i, 0)
    in_specs = [pl.BlockSpec((tile, D_MODEL), row)]
    in_specs += [pl.BlockSpec((tile, D_BRANCH), row) for _ in ys]
    in_specs += [pl.BlockSpec(w_out.shape, const), pl.BlockSpec((1, D_MODEL), const)]
    return pl.pallas_call(
        functools.partial(_outproj_kernel, final=final),
        grid=(n_tok // tile,), in_specs=in_specs,
        out_specs=pl.BlockSpec((tile, D_MODEL), row),
        out_shape=jax.ShapeDtypeStruct((n_tok, D_MODEL), F32),
        compiler_params=pltpu.CompilerParams(dimension_semantics=("arbitrary",),
                                             vmem_limit_bytes=VMEM_LIMIT),
        name="outproj",
    )(h, *ys, w_out, final_w)


def _norm_gate_store(y_ref, rows, col0, parts, nw_ref, z_parts):
    width = sum(p.shape[-1] for p in parts)
    ss = sum(jnp.sum(p * p, axis=-1, keepdims=True) for p in parts)
    scale = lax.rsqrt(ss * (1.0 / width) + EPS)
    c = col0
    for p, z in zip(parts, z_parts):
        w = p.shape[-1]
        out = p * scale * nw_ref[:, c:c + w]
        if z is not None:
            out = out * z
        y_ref[rows, c:c + w] = out.astype(y_ref.dtype)
        c += w


N_STAGES = 5
CHUNK_UNROLL = 1


def _run_chunks(work, tile):
    def body(c, carry):
        rows = pl.ds(pl.multiple_of(c * CHUNK, CHUNK), CHUNK)
        live = [(dict(d), stages) for d, stages in work]
        for k in range(N_STAGES):
            joined = []
            for d, stages in live:
                joint, fn = stages[k] if isinstance(stages[k], tuple) else (None, stages[k])
                if joint is not None and not any(stages is s for s in joined):
                    joint([dd for dd, st in live if st is stages], rows)
                    joined.append(stages)
                fn(d, rows)
        return carry

    lax.fori_loop(0, tile // CHUNK, body, 0, unroll=CHUNK_UNROLL)


def _gla_stages(ms_ref, mk_ref):
    lane = _lane_iota()

    def lane_mask(d, j):
        dk = d["dk"]
        return None if dk == LANES else (lane >= j * dk) & (lane < (j + 1) * dk)

    def pick(a, lm):
        return a if lm is None else jnp.where(lm, a, jnp.zeros_like(a))

    def n_groups(d):
        return d["q"].shape[1] // LANES

    def cumsums(ds, rows):
        lgs = [d["lg"][rows, :] for d in ds]
        es = _sel_dot2(ms_ref[...], jnp.concatenate(lgs, axis=1))
        col = 0
        for d, lg in zip(ds, lgs):
            d["es"] = es[:, col:col + lg.shape[1]]
            col += lg.shape[1]

    def operands(d, rows):
        ops = []
        for g in range(n_groups(d)):
            ls = slice(g * LANES, (g + 1) * LANES)
            e = d["es"][:, ls]
            q = d["q"][rows, ls]
            k = d["k"][rows, ls]
            gcs = e[0:CHUNK]
            qg = (q * jnp.exp(gcs)).astype(BF16)
            kd = (k * jnp.exp(e[CHUNK:2 * CHUNK])).astype(BF16)
            dec = jnp.exp(gcs[CHUNK - 1:CHUNK, :])
            qb = q.astype(BF16)
            kb = k.astype(BF16)
            ql, kl = [q * k], [None]
            mxu_block = 2
            for l in range(N_LEVELS):
                s = CHUNK >> (l + 1)
                if s == 1:
                    ql.append(q * jnp.exp(d["lg"][rows, ls]) * pltpu.roll(k, 1, axis=0))
                    kl.append(None)
                    continue
                if s >= MIN_VPU_LEVEL:
                    el = _level_exponent(gcs, s)
                else:
                    el = e[mxu_block * CHUNK:(mxu_block + 1) * CHUNK]
                    mxu_block += 1
                w = jnp.exp(el).astype(BF16)
                ql.append(qb * w)
                kl.append(kb * w)
            ops.append((qg, kd, dec, ql, kl))
        d["ops"] = ops

    def stacked(d, a):
        hp = LANES // d["dk"]
        return a if hp == 1 else jnp.concatenate([pick(a, lane_mask(d, j)) for j in range(hp)], axis=0)

    def scores(d, rows):
        hp = LANES // d["dk"]
        mask_rows = lambda l: jnp.concatenate([mk_ref[l][:, 0:CHUNK]] * hp, axis=0)
        groups = []
        for g in range(n_groups(d)):
            ql, kl = d["ops"][g][3], d["ops"][g][4]
            a = None
            for l in range(N_LEVELS + 1):
                if kl[l] is None:
                    s = jnp.sum(stacked(d, ql[l]), axis=-1, keepdims=True)
                else:
                    s = _dot_nt(stacked(d, ql[l]), kl[l])
                s = s * mask_rows(l)
                a = s if a is None else a + s
            groups.append(a.astype(BF16))
        d["scores"] = groups

    def outputs(d, rows):
        hp = LANES // d["dk"]
        outs, upds = [], [None] * n_groups(d)
        sts = [d["st"][g] for g in range(n_groups(d))]
        for g in range(n_groups(d)):
            qg, kd = d["ops"][g][0], d["ops"][g][1]
            inter = _dot_nt(stacked(d, qg), sts[g].astype(BF16))
            vbs = [d["v"][rows, (g * hp + j) * LANES:(g * hp + j + 1) * LANES].astype(BF16)
                   for j in range(hp)]
            u = _dot_tn(jnp.concatenate(vbs, axis=1), kd)
            for j in range(hp):
                hr = slice(j * CHUNK, (j + 1) * CHUNK)
                outs.append(_dot(d["scores"][g][hr], vbs[j]) + inter[hr])
                uj = u[j * LANES:(j + 1) * LANES]
                upds[g] = uj if upds[g] is None else jnp.where(lane_mask(d, j), uj, upds[g])
        d["outs"], d["upds"], d["sts"] = outs, upds, sts

    def finish(d, rows):
        for g in range(n_groups(d)):
            d["st"][g] = d["sts"][g] * d["ops"][g][2] + d["upds"][g]
        for h, o in enumerate(d["outs"]):
            _norm_gate_store(d["y"], rows, h * LANES, [o], d["nw"],
                             [d["z"][rows, h * LANES:(h + 1) * LANES]])

    return (cumsums, lambda d, rows: None), operands, scores, outputs, finish


def _gla_stream(p_ref, s, qk, dk, y_ref, st_s, nw_ref):
    v0, z0, lg0 = 2 * qk, 2 * qk + D_BRANCH, 2 * qk + 2 * D_BRANCH
    return dict(q=p_ref.at[s, :, 0:qk], k=p_ref.at[s, :, qk:v0], v=p_ref.at[s, :, v0:z0],
                z=p_ref.at[s, :, z0:lg0], lg=p_ref.at[s, :, lg0:lg0 + qk],
                y=y_ref.at[s], st=st_s.at[s], nw=nw_ref, dk=dk)


def _mlstm_constants():
    half = LANES // 2
    full0 = half * MLSTM_HEADS
    e = np.zeros((LANES, full0 + 2 * LANES * MLSTM_HEADS), np.float32)
    diff0 = full0 + LANES * MLSTM_HEADS
    sel = np.zeros((16, LANES), np.float32)
    for h in range(MLSTM_HEADS):
        e[SM_F + h, h * half:(h + 1) * half] = 1.0
        e[SM_F + h, full0 + h * LANES:full0 + (h + 1) * LANES] = 1.0
        e[SM_I + h, diff0 + h * LANES:diff0 + (h + 1) * LANES] = 1.0
        e[SM_F + h, diff0 + h * LANES:diff0 + (h + 1) * LANES] = -1.0
        sel[h // 2, SM_I + h] = 1.0
        sel[h // 2, SM_F + h] = -1.0
    return (jnp.asarray(np.concatenate([e, e], axis=0), BF16),
            jnp.asarray(np.concatenate([sel, sel], axis=1), BF16))


def _mlstm_stages(nw_ref, tri_ref, exp_ref, sel_ref):
    lane = _lane_iota()
    is_f = (lane >= SM_F) & (lane < SM_F + MLSTM_HEADS)

    half = LANES // 2
    ri = lax.broadcasted_iota(jnp.int32, (CHUNK, LANES), 0)
    ci = lax.broadcasted_iota(jnp.int32, (CHUNK, LANES), 1)
    causal2 = (ci & (half - 1)) <= ri
    lo_half = lane < half
    even = (lane & 1) == 0
    v0, o0, z0 = 2 * D_BRANCH, 3 * D_BRANCH, 4 * D_BRANCH
    heads = range(MLSTM_HEADS)
    pairs = range(MLSTM_HEADS // 2)
    full0 = half * MLSTM_HEADS
    diff0 = full0 + LANES * MLSTM_HEADS

    def qk_products(d, rows):
        qk_ref = d["p"]
        qbs = [qk_ref[rows, h * LANES:(h + 1) * LANES].astype(BF16) for h in heads]
        d["ks"] = [qk_ref[rows, D_BRANCH + h * LANES:D_BRANCH + (h + 1) * LANES] for h in heads]
        kbs = [k.astype(BF16) for k in d["ks"]]
        zero = jnp.zeros((CHUNK, LANES), BF16)
        ones = jnp.ones((CHUNK, LANES), BF16)
        d["qk_raw"] = [_dot_nt(jnp.concatenate([qbs[2 * p], qbs[2 * p + 1]], axis=1),
                               _side_by_side(kbs[2 * p], kbs[2 * p + 1], zero)) for p in pairs]
        d["qc"] = [_dot(qbs[h], d["c"][h].astype(BF16)) for h in heads]
        d["vaug"] = [jnp.concatenate([d["p"][rows, v0 + h * LANES:v0 + (h + 1) * LANES].astype(BF16),
                                      ones], axis=1) for h in heads]

    def gate_cumsums(ds, rows):
        gcs = [d["g"][rows, :] for d in ds]
        bcols = _sel_dot2(tri_ref[...], jnp.concatenate(gcs, axis=1))
        for i, d in enumerate(ds):
            d["ib"] = jnp.where(is_f, bcols[:, i * LANES:(i + 1) * LANES], gcs[i])

    def gate_sums(d, rows):
        ib = d["ib"]
        d["by_parity"] = jnp.concatenate([jnp.where(even, ib, 0.0), jnp.where(even, 0.0, ib)], axis=0)

    def spread_gates(ds, rows):
        ex = _dot_sel2(jnp.concatenate([d["ib"] for d in ds], axis=0), exp_ref[...])
        drows = _sel_dot2_nt(sel_ref[...], jnp.concatenate([d["by_parity"] for d in ds], axis=0))
        for i, d in enumerate(ds):
            d["ex"] = ex[i * CHUNK:(i + 1) * CHUNK]
            d["drows"] = drows[:, i * LANES:(i + 1) * LANES]

    def weights(d, rows):
        ex = d["ex"]
        mx = d["m"][0:1, :]
        ss, mrs = [], []
        for p in pairs:
            bx = ex[:, p * LANES:(p + 1) * LANES]
            lw = jnp.where(causal2, bx + d["drows"][p:p + 1, :], NEG_BIG)
            mr0 = jnp.max(jnp.where(lo_half, lw, NEG_BIG), axis=-1, keepdims=True)
            mr1 = jnp.max(jnp.where(lo_half, NEG_BIG, lw), axis=-1, keepdims=True)
            m64 = jnp.where(lo_half, mx[:, 2 * p * LANES:(2 * p + 1) * LANES],
                            mx[:, (2 * p + 1) * LANES:(2 * p + 2) * LANES])
            m_row = jnp.maximum(jnp.where(lo_half, mr0, mr1), bx + m64)
            ss.append((d["qk_raw"][p] * jnp.exp(lw - m_row)).astype(BF16))
            mrs += [mr0, mr1]
        d["ss"], d["mrs"] = ss, mrs
        b_last = ex[CHUNK - 1:CHUNK, full0:diff0]
        lwe = ex[:, diff0:] + b_last
        m_new = jnp.maximum(b_last + mx, jnp.max(lwe, axis=0, keepdims=True))
        d["cd"] = jnp.exp(b_last + mx - m_new)
        d["m_new"] = m_new
        kw = jnp.exp(lwe - m_new)
        d["kws"] = [(d["ks"][h] * kw[:, h * LANES:(h + 1) * LANES]).astype(BF16) for h in heads]

    def numerators(d, rows):
        zero = jnp.zeros((CHUNK, 2 * LANES), BF16)
        d["nums"] = [_dot(d["ss"][p], _side_by_side(d["vaug"][2 * p], d["vaug"][2 * p + 1], zero))
                     for p in pairs]
        d["cups"] = [_dot_tn(d["kws"][h], d["vaug"][h]) for h in heads]

    def finish(d, rows):
        ex = d["ex"]
        mx = d["m"][0:1, :]
        for h in heads:
            p, hd = divmod(h, 2)
            hs = slice(h * LANES, (h + 1) * LANES)
            m_inter = ex[:, full0 + h * LANES:full0 + (h + 1) * LANES] + mx[:, hs]
            m_row = jnp.maximum(d["mrs"][h], m_inter)
            inter = jnp.exp(m_inter - m_row)
            sv = d["nums"][p][:, hd * 2 * LANES:(hd + 1) * 2 * LANES]
            num = sv[:, 0:LANES] + inter * d["qc"][h][:, 0:LANES]
            den = sv[:, LANES:] + inter * d["qc"][h][:, LANES:]
            hh = num / jnp.maximum(jnp.abs(den), jnp.exp(-m_row))
            cd = d["cd"][:, hs]
            d["c"][h] = jnp.concatenate([cd, cd], axis=1) * d["c"][h] + d["cups"][h]
            og = d["p"][rows, o0 + h * LANES:o0 + (h + 1) * LANES]
            _norm_gate_store(d["y"], rows, h * LANES, [og * hh], nw_ref,
                             [d["p"][rows, z0 + h * LANES:z0 + (h + 1) * LANES]])
        d["m"][0:1, :] = d["m_new"]

    return qk_products, (gate_cumsums, gate_sums), (spread_gates, weights), numerators, finish


def _ssd_constants():
    e = np.zeros((LANES, SSD_HEADS * SSD_HEAD_DIM), np.float32)
    sel = np.zeros((16, LANES), np.float32)
    for h in range(SSD_HEADS):
        e[SM_DT + h, h * SSD_HEAD_DIM:(h + 1) * SSD_HEAD_DIM] = 1.0
        sel[h // 2, SM_DT + h] = 1.0
    return (jnp.asarray(np.concatenate([e, e], axis=0), BF16),
            jnp.asarray(np.concatenate([sel, sel], axis=1), BF16))


def _ssd_stages(alog_ref, dx_ref, nw_ref, tri_ref, sel_ref, exp_ref):
    lane = _lane_iota()
    is_dt = (lane >= SM_DT) & (lane < SM_DT + SSD_HEADS)

    half = LANES // 2
    ri = lax.broadcasted_iota(jnp.int32, (CHUNK, LANES), 0)
    ci = lax.broadcasted_iota(jnp.int32, (CHUNK, LANES), 1)
    causal2 = (ci & (half - 1)) <= ri
    lo_half = lane < half
    even = (lane & 1) == 0
    a_lane = jnp.where(is_dt, -jnp.exp(alog_ref[...]), 0.0)
    b0, c0, z0 = D_BRANCH, D_BRANCH + SSD_BC, D_BRANCH + 2 * SSD_BC
    groups = range(SSD_GROUPS)
    group_w = D_BRANCH // SSD_GROUPS
    pairs = range(D_BRANCH // LANES)
    pairs_per_group = group_w // LANES

    def products(d, rows):
        xbc = d["p"]
        d["bgs"] = [xbc[rows, b0 + g * SSD_STATE:b0 + (g + 1) * SSD_STATE].astype(BF16) for g in groups]
        cgbs = [xbc[rows, c0 + g * SSD_STATE:c0 + (g + 1) * SSD_STATE].astype(BF16) for g in groups]
        d["cb2"] = [_dot_nt(cgbs[g], jnp.concatenate([d["bgs"][g], d["bgs"][g]], axis=0))
                    for g in groups]
        d["cst"] = [_dot(cgbs[g], d["st"][g].astype(BF16)) for g in groups]

    def decay_cumsums(ds, rows):
        dts = [d["dt"][rows, :] for d in ds]
        acs = _sel_dot2(tri_ref[...], jnp.concatenate([dt * a_lane for dt in dts], axis=1))
        for i, d in enumerate(ds):
            d["dt_c"], d["acs"] = dts[i], acs[:, i * LANES:(i + 1) * LANES]

    def decay_sums(d, rows):
        acs = d["acs"]
        d["dt_acs"] = jnp.concatenate([d["dt_c"], acs], axis=0)
        d["by_parity"] = jnp.concatenate([jnp.where(even, acs, 0.0), jnp.where(even, 0.0, acs)], axis=0)

    def spread_decays(ds, rows):
        ex = _dot_sel2(jnp.concatenate([d["dt_acs"] for d in ds], axis=0), exp_ref[...])
        a_rows = _sel_dot2_nt(sel_ref[...], jnp.concatenate([d["by_parity"] for d in ds], axis=0))
        for i, d in enumerate(ds):
            d["dtx"] = ex[2 * i * CHUNK:(2 * i + 1) * CHUNK]
            d["acs_x"] = ex[(2 * i + 1) * CHUNK:(2 * i + 2) * CHUNK]
            d["a_rows"] = a_rows[:, i * LANES:(i + 1) * LANES]

    def decays(d, rows):
        d["xss"], d["ms"], d["xblk"], xdecs = [], [], [], []
        for p in pairs:
            ls = slice(p * LANES, (p + 1) * LANES)
            ax = d["acs_x"][:, ls]
            lmat = jnp.exp(jnp.where(causal2, ax - d["a_rows"][p:p + 1, :], NEG_BIG))
            d["ms"].append((d["cb2"][p // pairs_per_group] * lmat).astype(BF16))
            xs = d["p"][rows, ls]
            xdt = xs * d["dtx"][:, ls]
            d["xss"].append(xs)
            d["xblk"].append(jnp.concatenate([jnp.where(lo_half, xdt, 0.0), jnp.where(lo_half, 0.0, xdt)],
                                             axis=0).astype(BF16))
            xdecs.append((xdt * jnp.exp(ax[CHUNK - 1:CHUNK, :] - ax)).astype(BF16))
        d["xdec"] = [jnp.concatenate(xdecs[g * pairs_per_group:(g + 1) * pairs_per_group], axis=1)
                     for g in groups]

    def chunk_products(d, rows):
        d["yds"] = [_dot(d["ms"][p], d["xblk"][p]) for p in pairs]
        d["ups"] = [_dot_tn(d["bgs"][g], d["xdec"][g]) for g in groups]

    def finish(d, rows):
        for g in groups:
            gs = slice(g * group_w, (g + 1) * group_w)
            eax = jnp.exp(d["acs_x"][:, gs])
            d["st"][g] = d["st"][g] * eax[CHUNK - 1:CHUNK, :] + d["ups"][g]
            ys = []
            for pp in range(pairs_per_group):
                p = g * pairs_per_group + pp
                ls = slice(p * LANES, (p + 1) * LANES)
                y = (d["cst"][g][:, pp * LANES:(pp + 1) * LANES] * eax[:, pp * LANES:(pp + 1) * LANES]
                     + dx_ref[:, ls] * d["xss"][p] + d["yds"][p])
                ys.append(y * d["p"][rows, z0 + p * LANES:z0 + (p + 1) * LANES])
            _norm_gate_store(d["y"], rows, g * group_w, ys, nw_ref, [None] * len(ys))

    return products, (decay_cumsums, decay_sums), (spread_decays, decays), chunk_products, finish


def _mixers_kernel(pa_ref, pb_ref, pc_ref, pd_ref, sm_ref, nwa_ref, nwb_ref, nwc_ref, nwd_ref,
                   ms_ref, mk_ref, tri_ref, mexp_ref, msel_ref, alog_ref, dx_ref, dsel_ref, dexp_ref,
                   ya_ref, yb_ref, yc_ref, yd_ref, sta_s, stc_s, cb_s, mb_s, std_s, *, tile):
    @pl.when(pl.program_id(0) == 0)
    def _():
        for r in (sta_s, stc_s, cb_s, mb_s, std_s):
            r[...] = jnp.zeros_like(r)

    gla = _gla_stages(ms_ref, mk_ref)
    mlstm = _mlstm_stages(nwb_ref, tri_ref, mexp_ref, msel_ref)
    ssd = _ssd_stages(alog_ref, dx_ref, nwd_ref, tri_ref, dsel_ref, dexp_ref)
    work = []
    for s in range(pa_ref.shape[0]):
        work.append((_gla_stream(pa_ref, s, GLA_QK, GLA_DK, ya_ref, sta_s, nwa_ref), gla))
        work.append((dict(p=pb_ref.at[s], g=sm_ref.at[s], c=cb_s.at[s], m=mb_s.at[s], y=yb_ref.at[s]),
                     mlstm))
        work.append((_gla_stream(pc_ref, s, HGRN_QF, HGRN_DK, yc_ref, stc_s, nwc_ref), gla))
        work.append((dict(p=pd_ref.at[s], dt=sm_ref.at[s], st=std_s.at[s], y=yd_ref.at[s]), ssd))
    _run_chunks(work, tile)


def _mixers(tok_inputs, const_inputs, n_batch, seq, tile):
    tok = lambda t: (0, t, 0)
    in_specs = [pl.BlockSpec((n_batch, tile, a.shape[2]), tok) for a in tok_inputs]
    for a in const_inputs:
        in_specs.append(pl.BlockSpec(a.shape, lambda t, nd=a.ndim: (0,) * nd))
    scratch = [pltpu.VMEM((n_batch, GLA_QK // LANES, GLA_DV, LANES), F32),
               pltpu.VMEM((n_batch, HGRN_QF // LANES, HGRN_DV, LANES), F32),
               pltpu.VMEM((n_batch, MLSTM_HEADS, MLSTM_DH, 2 * MLSTM_DH), F32),
               pltpu.VMEM((n_batch, 8, MLSTM_HEADS * LANES), F32),
               pltpu.VMEM((n_batch, SSD_GROUPS, SSD_STATE, D_BRANCH // SSD_GROUPS), F32)]
    return pl.pallas_call(
        functools.partial(_mixers_kernel, tile=tile), grid=(seq // tile,), in_specs=in_specs,
        out_specs=[pl.BlockSpec((n_batch, tile, D_BRANCH), tok)] * 4,
        out_shape=[jax.ShapeDtypeStruct((n_batch, seq, D_BRANCH), BF16)] * 4,
        scratch_shapes=scratch,
        compiler_params=pltpu.CompilerParams(dimension_semantics=("arbitrary",),
                                             vmem_limit_bytes=VMEM_LIMIT),
        name="mixers",
    )(*tok_inputs, *const_inputs)


def _pad_lanes(parts, total=LANES):
    width = sum(p.shape[-1] for p in parts)
    lead = parts[0].shape[:-1]
    return jnp.concatenate(list(parts) + [jnp.zeros(lead + (total - width,), parts[0].dtype)], axis=-1)


def _small_vector(i_part, f_part, dt_part):
    z = jnp.zeros((GLA_GATE_RANK,), F32)
    return _pad_lanes([z, i_part.astype(F32), f_part.astype(F32), dt_part.astype(F32)])[None, :]


def kernel(x, norm_w, w_in, gla_gate_w, gla_gate_b, gla_norm_w, ml_conv_w, ml_conv_b, ml_i_b, ml_f_b,
           ml_norm_w, hg_lb_logits, hg_norm_w, ssd_conv_w, ssd_conv_b, ssd_dt_bias, ssd_A_log, ssd_D,
           ssd_norm_w, w_out, final_norm_w):
    n_batch, seq, _ = x.shape
    depth = w_in.shape[0]
    tile = min(MIX_TILE, seq)
    n_tok = n_batch * seq

    mstack, masks = _gla_constants()
    tri_c = _tri2(CHUNK)
    ml_exp, ml_sel = _mlstm_constants()
    ssd_exp, ssd_sel = _ssd_constants()
    zero4 = jnp.zeros((MLSTM_HEADS,), F32)
    in_tile = min(IN_TILE, seq)

    p = jax.nn.softmax(hg_lb_logits.astype(F32), axis=0)
    lower_bounds = jnp.cumsum(p, axis=0) - p[0:1]

    w_wide, w_small = _realign(jnp.swapaxes(w_in, 1, 2), REALIGN_ROWS)

    h = x.reshape(n_tok, D_MODEL)
    row2 = lambda v: v.astype(F32).reshape(1, -1)
    for l in range(depth):
        gate_w = jnp.concatenate(
            [gla_gate_w[l].astype(F32), jnp.zeros((LANES - GLA_GATE_RANK, GLA_QK), F32)], axis=0)
        consts = [gate_w, row2(gla_gate_b[l]), row2(lower_bounds[l]),
                  ml_conv_w[l].astype(F32), row2(ml_conv_b[l]),
                  ssd_conv_w[l].astype(F32), row2(ssd_conv_b[l]),
                  _small_vector(ml_i_b[l], ml_f_b[l], ssd_dt_bias[l])]
        pa, pb, pc, pd, sm = [p.reshape(n_batch, seq, -1)
                              for p in _inproj(h, row2(norm_w[l]), w_wide, w_small, l, consts, in_tile,
                                               seq // in_tile)]

        ys = _mixers(
            [pa, pb, pc, pd, sm],
            [row2(gla_norm_w[l]), row2(ml_norm_w[l]), row2(hg_norm_w[l]), row2(ssd_norm_w[l]),
             mstack, masks, tri_c, ml_exp, ml_sel, _small_vector(zero4, zero4, ssd_A_log[l]),
             row2(jnp.repeat(ssd_D[l].astype(F32), SSD_HEAD_DIM)), ssd_sel, ssd_exp],
            n_batch, seq, tile)
        ys = [y.reshape(n_tok, D_BRANCH) for y in ys]
        h = _outproj(h, ys, w_out[l].astype(BF16), row2(final_norm_w), l == depth - 1,
                     min(OUT_TILE, n_tok))
    return h.reshape(n_batch, seq, D_MODEL)
```

```python
import functools

import numpy as np
import jax
import jax.numpy as jnp
from jax import lax
from jax.experimental import pallas as pl
from jax.experimental.pallas import tpu as pltpu

F32 = jnp.float32
BF16 = jnp.bfloat16

D_MODEL = 1024
D_BRANCH = 512
EPS = 1e-6
NEG_BIG = -1e30

GLA_HEADS, GLA_DK, GLA_DV = 4, 64, 128
GLA_GATE_RANK, GLA_GATE_NORM = 16, 16.0
MLSTM_HEADS, MLSTM_DH, MLSTM_CONV = 4, 128, 4
HGRN_HEADS, HGRN_DK, HGRN_DV = 4, 128, 128
SSD_HEAD_DIM, SSD_HEADS, SSD_GROUPS, SSD_STATE, SSD_CONV = 64, 8, 2, 128, 4
GLA_QK = GLA_HEADS * GLA_DK
HGRN_QF = HGRN_HEADS * HGRN_DK
SSD_BC = SSD_GROUPS * SSD_STATE
PROJ_SIZES = (
    GLA_QK, GLA_QK, D_BRANCH, GLA_GATE_RANK, D_BRANCH,
    D_BRANCH, D_BRANCH, D_BRANCH, MLSTM_HEADS, MLSTM_HEADS, D_BRANCH, D_BRANCH,
    HGRN_QF, HGRN_QF, D_BRANCH, D_BRANCH,
    D_BRANCH, SSD_BC, SSD_BC, SSD_HEADS, D_BRANCH,
)

LANES = 128
HIST = 8
VMEM_LIMIT = 56 * 1024 * 1024

REALIGN_ROWS = 256
IN_TILE = 256
MIX_TILE = 256
OUT_TILE = 512

CHUNK = 64
N_LEVELS = 6
MIN_VPU_LEVEL = 4

SM_GR, SM_I, SM_F, SM_DT = 0, 16, 20, 24

W_A = 2 * GLA_QK + 2 * D_BRANCH
W_B = 5 * D_BRANCH
W_C = 2 * HGRN_QF + 2 * D_BRANCH
W_D = 2 * D_BRANCH + 2 * SSD_BC
COL_A = 0
COL_B = COL_A + W_A
COL_C = COL_B + W_B
COL_D = COL_C + W_C
COL_S = COL_D + W_D


def _gla_constants():
    c = CHUNK
    t = np.arange(c)[:, None]
    d = np.arange(c)[None, :]
    blocks = [(d <= t), (d > t)]
    masks = [np.eye(c, dtype=bool)]
    for l in range(N_LEVELS):
        s = c >> (l + 1)
        mid_t = (t // (2 * s)) * (2 * s) + s
        upper = t >= mid_t
        if 1 < s < MIN_VPU_LEVEL:
            blocks.append(np.where(upper, (d >= mid_t) & (d <= t), (d > t) & (d <= mid_t - 1)))
        same = (t // (2 * s)) == (d // (2 * s))
        masks.append(same & upper & (d < mid_t))
    mstack = np.concatenate(blocks, axis=0).astype(np.float32)
    mstack = np.concatenate([mstack, mstack], axis=1)
    masks = np.stack(masks).astype(np.float32)
    return jnp.asarray(mstack, BF16), jnp.asarray(np.concatenate([masks, masks], axis=-1))


def _tri2(n):
    tri = np.tril(np.ones((n, n), np.float32))
    return jnp.asarray(np.concatenate([tri, tri], axis=1), BF16)


def _dot(a, b):
    return jnp.dot(a, b, preferred_element_type=F32)


def _dot_nt(a, b):
    return lax.dot_general(a, b, (((1,), (1,)), ((), ())), preferred_element_type=F32)


def _dot_tn(a, b):
    return lax.dot_general(a, b, (((0,), (0,)), ((), ())), preferred_element_type=F32)


def _sel_dot2(sel2, x):
    hi = x.astype(BF16)
    mid = (x - hi.astype(F32)).astype(BF16)
    return _dot(sel2, jnp.concatenate([hi, mid], axis=0))


def _sel_dot2_nt(sel2, x):
    hi = x.astype(BF16)
    mid = (x - hi.astype(F32)).astype(BF16)
    return _dot_nt(sel2, jnp.concatenate([hi, mid], axis=1))


def _level_exponent(gcs, s):
    pieces = []
    for b in range(0, CHUNK, 2 * s):
        ref_row = gcs[b + s - 1:b + s, :]
        if s >= 8:
            pieces += [ref_row - gcs[b:b + s], gcs[b + s:b + 2 * s] - ref_row]
        else:
            diff = gcs[b:b + 2 * s] - ref_row
            pieces.append(jnp.minimum(diff, -diff))
    return jnp.concatenate(pieces, axis=0)


def _dot_sel2(x, sel2):
    hi = x.astype(BF16)
    mid = (x - hi.astype(F32)).astype(BF16)
    return _dot(jnp.concatenate([hi, mid], axis=1), sel2)


def _dot_f32(a, b):
    ah = a.astype(BF16)
    al = (a - ah.astype(F32)).astype(BF16)
    bh = b.astype(BF16)
    bl = (b - bh.astype(F32)).astype(BF16)
    return _dot(ah, bh) + _dot(al, bh) + _dot(ah, bl)


def _softplus(x):
    return jnp.maximum(x, 0.0) + jnp.log1p(jnp.exp(-jnp.abs(x)))


def _log_sigmoid(x):
    return -_softplus(-x)


def _silu(x):
    return x * jax.nn.sigmoid(x)


def _lane_iota():
    return lax.broadcasted_iota(jnp.int32, (1, LANES), 1)


def _side_by_side(a, b, zero):
    return jnp.concatenate([jnp.concatenate([a, zero], axis=1),
                            jnp.concatenate([zero, b], axis=1)], axis=0)


def _wide_segments():
    segs, src, dst = [], 0, 0
    run_start = None
    for size in PROJ_SIZES + (0,):
        wide = size >= LANES
        if wide and run_start is None:
            run_start = src
        if not wide and run_start is not None:
            segs.append((run_start, dst, src - run_start))
            dst += src - run_start
            run_start = None
        src += size
    return segs


def _realign_kernel(wt_ref, o_ref, os_ref):
    for src, dst, n in _wide_segments():
        for r in range(0, n, LANES):
            o_ref[:, dst + r:dst + r + LANES] = wt_ref[src + r:src + r + LANES, :].T.astype(BF16)
    narrow, src = [], 0
    for size in PROJ_SIZES:
        if size < LANES:
            narrow.append(wt_ref[src:src + size, :])
        src += size
    used = sum(p.shape[0] for p in narrow)
    narrow.append(jnp.zeros((LANES - used, wt_ref.shape[1]), F32))
    os_ref[...] = jnp.concatenate(narrow, axis=0).T


def _realign(wt, rows):
    depth, n_proj, d_model = wt.shape
    return pl.pallas_call(
        _realign_kernel, grid=(depth, d_model // rows),
        in_specs=[pl.BlockSpec((None, n_proj, rows), lambda l, j: (l, 0, j))],
        out_specs=[pl.BlockSpec((None, rows, COL_S), lambda l, j: (l, j, 0)),
                   pl.BlockSpec((None, rows, LANES), lambda l, j: (l, j, 0))],
        out_shape=[jax.ShapeDtypeStruct((depth, d_model, COL_S), BF16),
                   jax.ShapeDtypeStruct((depth, d_model, LANES), F32)],
        compiler_params=pltpu.CompilerParams(dimension_semantics=("arbitrary", "arbitrary"),
                                             vmem_limit_bytes=VMEM_LIMIT),
        name="realign",
    )(wt)


CONV_ROWS, CONV_LANES = 256, 128


def _causal_conv_silu(raw, xe_ref, hist_ref, w_ref, b_ref, taps, out_ref, scale=None):
    rows, n = raw.shape
    xe_ref[0:HIST, :] = hist_ref[...]
    xe_ref[HIST:HIST + rows, :] = raw
    hist_ref[...] = raw[rows - HIST:rows, :]
    for r in range(0, rows, CONV_ROWS):
        for l in range(0, n, CONV_LANES):
            cs = slice(l, l + CONV_LANES)
            acc = b_ref[:, cs] + w_ref[taps - 1:taps, cs] * xe_ref[HIST + r:HIST + r + CONV_ROWS, cs]
            for k in range(taps - 1):
                off = HIST - (taps - 1) + k + r
                acc = acc + w_ref[k:k + 1, cs] * xe_ref[off:off + CONV_ROWS, cs]
            out = _silu(acc)
            out_ref[r:r + CONV_ROWS, cs] = out if scale is None else out * scale[:, cs]


def _inproj_kernel(x_ref, xn_ref, nw_ref, w_ref, ws_ref, gw_ref, gb_ref, lb_ref, cwb_ref, cbb_ref, cwd_ref,
                   cbd_ref, gbias_ref, oa_ref, ob_ref, oc_ref, od_ref, os_ref, xe_s, hist_b, hist_d, u_s,
                   *, tiles_per_seq):
    step = pl.program_id(0)

    @pl.when(step % tiles_per_seq == 0)
    def _():
        hist_b[...] = jnp.zeros_like(hist_b)
        hist_d[...] = jnp.zeros_like(hist_d)

    def normed(x):
        ms = jnp.mean(x * x, axis=-1, keepdims=True)
        return (x * lax.rsqrt(ms + EPS) * nw_ref[...]).astype(BF16)

    @pl.when(step == 0)
    def _():
        u_s[0] = normed(x_ref[...])

    u = u_s[step % 2]
    lane = _lane_iota()

    small = _dot(u, ws_ref[...].astype(BF16))
    biased = small + gbias_ref[...]
    is_f = (lane >= SM_F) & (lane < SM_F + MLSTM_HEADS)
    is_dt = (lane >= SM_DT) & (lane < SM_DT + SSD_HEADS)
    os_ref[...] = jnp.where(is_f, _log_sigmoid(biased), jnp.where(is_dt, _softplus(biased), biased))

    raw = _dot(u, w_ref[:, COL_B:COL_B + W_B])
    conv_w = 2 * D_BRANCH
    k_scale = jnp.where(lax.broadcasted_iota(jnp.int32, (1, conv_w), 1) < D_BRANCH, 1.0, MLSTM_DH ** -0.5)
    _causal_conv_silu(raw[:, 0:conv_w], xe_s, hist_b, cwb_ref, cbb_ref, MLSTM_CONV, ob_ref, k_scale)
    o0, z0 = 3 * D_BRANCH, 4 * D_BRANCH
    ob_ref[:, conv_w:o0] = raw[:, conv_w:o0]
    ob_ref[:, o0:z0] = jax.nn.sigmoid(raw[:, o0:z0])
    ob_ref[:, z0:W_B] = _silu(raw[:, z0:W_B])

    raw = _dot(u, w_ref[:, COL_D:COL_D + W_D])
    conv_w = D_BRANCH + 2 * SSD_BC
    _causal_conv_silu(raw[:, 0:conv_w], xe_s, hist_d, cwd_ref, cbd_ref, SSD_CONV, od_ref)
    od_ref[:, conv_w:W_D] = _silu(raw[:, conv_w:W_D])
    u_s[(step + 1) % 2] = normed(xn_ref[...])

    raw = _dot(u, w_ref[:, COL_C:COL_C + W_C])
    lb = lb_ref[...]
    fr = raw[:, HGRN_QF:2 * HGRN_QF]
    oc_ref[:, 0:HGRN_QF] = raw[:, 0:HGRN_QF] * (HGRN_DK ** -0.5)
    oc_ref[:, HGRN_QF:2 * HGRN_QF] = (1.0 - lb) * jax.nn.sigmoid(-fr)
    z0 = 2 * HGRN_QF + D_BRANCH
    oc_ref[:, 2 * HGRN_QF:z0] = raw[:, 2 * HGRN_QF:z0]
    oc_ref[:, z0:W_C] = _silu(raw[:, z0:W_C])
    oc_ref[:, W_C:W_C + HGRN_QF] = jnp.log(jnp.maximum(lb + (1.0 - lb) * jax.nn.sigmoid(fr), 1e-30))

    gate = _dot_f32(small, gw_ref[...]) + gb_ref[...]
    oa_ref[:, W_A:W_A + GLA_QK] = _log_sigmoid(gate) * (1.0 / GLA_GATE_NORM)
    raw = _dot(u, w_ref[:, COL_A:COL_A + W_A])
    oa_ref[:, 0:GLA_QK] = raw[:, 0:GLA_QK] * (GLA_DK ** -0.5)
    z0 = 2 * GLA_QK + D_BRANCH
    oa_ref[:, GLA_QK:z0] = raw[:, GLA_QK:z0]
    oa_ref[:, z0:W_A] = _silu(raw[:, z0:W_A])


def _inproj(h, norm_w, w_wide, w_small, layer, consts, tile, tiles_per_seq):
    n_tok = h.shape[0]
    const = lambda i: (0, 0)
    row = lambda i: (i, 0)
    n_tiles = n_tok // tile
    in_specs = [pl.BlockSpec((tile, D_MODEL), row),
                pl.BlockSpec((tile, D_MODEL), lambda i: (jnp.minimum(i + 1, n_tiles - 1), 0)),
                pl.BlockSpec((1, D_MODEL), const),
                pl.BlockSpec((None, D_MODEL, COL_S), lambda i: (layer, 0, 0), pipeline_mode=pl.Buffered(1)),
                pl.BlockSpec((None, D_MODEL, LANES), lambda i: (layer, 0, 0))]
    in_specs += [pl.BlockSpec(c.shape, const) for c in consts]
    widths = (W_A + GLA_QK, W_B, W_C + HGRN_QF, W_D, LANES)
    conv_w = 2 * D_BRANCH
    return pl.pallas_call(
        functools.partial(_inproj_kernel, tiles_per_seq=tiles_per_seq),
        grid=(n_tiles,), in_specs=in_specs,
        out_specs=[pl.BlockSpec((tile, w), row) for w in widths],
        out_shape=[jax.ShapeDtypeStruct((n_tok, w), F32) for w in widths],
        scratch_shapes=[pltpu.VMEM((tile + HIST, conv_w), F32), pltpu.VMEM((HIST, conv_w), F32),
                        pltpu.VMEM((HIST, conv_w), F32), pltpu.VMEM((2, tile, D_MODEL), BF16)],
        compiler_params=pltpu.CompilerParams(dimension_semantics=("arbitrary",),
                                             vmem_limit_bytes=VMEM_LIMIT),
        name="inproj",
    )(h, h, norm_w, w_wide, w_small, *consts)


def _outproj_kernel(h_ref, ya_ref, yb_ref, yc_ref, yd_ref, w_ref, fw_ref, o_ref, *, final):
    acc = h_ref[...]
    for i, y_ref in enumerate((ya_ref, yb_ref, yc_ref, yd_ref)):
        acc = acc + _dot(y_ref[...], w_ref[i * D_BRANCH:(i + 1) * D_BRANCH, :])
    if final:
        ms = jnp.mean(acc * acc, axis=-1, keepdims=True)
        acc = acc * lax.rsqrt(ms + EPS) * fw_ref[...]
    o_ref[...] = acc


def _outproj(h, ys, w_out, final_w, final, tile):
    n_tok = h.shape[0]
    const = lambda i: (0, 0)
    row = lambda i: (i, 0)
    in_specs = [pl.BlockSpec((tile, D_MODEL), row)]
    in_specs += [pl.BlockSpec((tile, D_BRANCH), row) for _ in ys]
    in_specs += [pl.BlockSpec(w_out.shape, const), pl.BlockSpec((1, D_MODEL), const)]
    return pl.pallas_call(
        functools.partial(_outproj_kernel, final=final),
        grid=(n_tok // tile,), in_specs=in_specs,
        out_specs=pl.BlockSpec((tile, D_MODEL), row),
        out_shape=jax.ShapeDtypeStruct((n_tok, D_MODEL), F32),
        compiler_params=pltpu.CompilerParams(dimension_semantics=("arbitrary",),
                                             vmem_limit_bytes=VMEM_LIMIT),
        name="outproj",
    )(h, *ys, w_out, final_w)


def _norm_gate_store(y_ref, rows, col0, parts, nw_ref, z_parts):
    width = sum(p.shape[-1] for p in parts)
    ss = sum(jnp.sum(p * p, axis=-1, keepdims=True) for p in parts)
    scale = lax.rsqrt(ss * (1.0 / width) + EPS)
    c = col0
    for p, z in zip(parts, z_parts):
        w = p.shape[-1]
        out = p * scale * nw_ref[:, c:c + w]
        if z is not None:
            out = out * z
        y_ref[rows, c:c + w] = out.astype(y_ref.dtype)
        c += w


N_STAGES = 5
CHUNK_UNROLL = 1


def _run_chunks(work, tile):
    def body(c, carry):
        rows = pl.ds(pl.multiple_of(c * CHUNK, CHUNK), CHUNK)
        live = [(dict(d), stages) for d, stages in work]
        for k in range(N_STAGES):
            joined = []
            for d, stages in live:
                joint, fn = stages[k] if isinstance(stages[k], tuple) else (None, stages[k])
                if joint is not None and not any(stages is s for s in joined):
                    joint([dd for dd, st in live if st is stages], rows)
                    joined.append(stages)
                fn(d, rows)
        return carry

    lax.fori_loop(0, tile // CHUNK, body, 0, unroll=CHUNK_UNROLL)


def _gla_stages(ms_ref, mk_ref):
    lane = _lane_iota()

    def lane_mask(d, j):
        dk = d["dk"]
        return None if dk == LANES else (lane >= j * dk) & (lane < (j + 1) * dk)

    def pick(a, lm):
        return a if lm is None else jnp.where(lm, a, jnp.zeros_like(a))

    def n_groups(d):
        return d["q"].shape[1] // LANES

    def cumsums(d, rows):
        d["es"] = [_sel_dot2(ms_ref[...], d["lg"][rows, p * 2 * LANES:(p + 1) * 2 * LANES])
                   for p in range(n_groups(d) // 2)]

    def operands(d, rows):
        ops = []
        for g in range(n_groups(d)):
            ls = slice(g * LANES, (g + 1) * LANES)
            e = d["es"][g // 2][:, (g % 2) * LANES:(g % 2 + 1) * LANES]
            q = d["q"][rows, ls]
            k = d["k"][rows, ls]
            gcs = e[0:CHUNK]
            qg = (q * jnp.exp(gcs)).astype(BF16)
            kd = (k * jnp.exp(e[CHUNK:2 * CHUNK])).astype(BF16)
            dec = jnp.exp(gcs[CHUNK - 1:CHUNK, :])
            qb = q.astype(BF16)
            kb = k.astype(BF16)
            ql, kl = [q * k], [None]
            mxu_block = 2
            for l in range(N_LEVELS):
                s = CHUNK >> (l + 1)
                if s == 1:
                    ql.append(q * jnp.exp(d["lg"][rows, ls]) * pltpu.roll(k, 1, axis=0))
                    kl.append(None)
                    continue
                if s >= MIN_VPU_LEVEL:
                    el = _level_exponent(gcs, s)
                else:
                    el = e[mxu_block * CHUNK:(mxu_block + 1) * CHUNK]
                    mxu_block += 1
                w = jnp.exp(el).astype(BF16)
                ql.append(qb * w)
                kl.append(kb * w)
            ops.append((qg, kd, dec, ql, kl))
        d["ops"] = ops

    def stacked(d, a):
        hp = LANES // d["dk"]
        return a if hp == 1 else jnp.concatenate([pick(a, lane_mask(d, j)) for j in range(hp)], axis=0)

    def scores(d, rows):
        hp = LANES // d["dk"]
        mask_rows = lambda l: jnp.concatenate([mk_ref[l][:, 0:CHUNK]] * hp, axis=0)
        groups = []
        for g in range(n_groups(d)):
            ql, kl = d["ops"][g][3], d["ops"][g][4]
            a = None
            for l in range(N_LEVELS + 1):
                if kl[l] is None:
                    s = jnp.sum(stacked(d, ql[l]), axis=-1, keepdims=True)
                else:
                    s = _dot_nt(stacked(d, ql[l]), kl[l])
                s = s * mask_rows(l)
                a = s if a is None else a + s
            groups.append(a.astype(BF16))
        d["scores"] = groups

    def outputs(d, rows):
        hp = LANES // d["dk"]
        outs, upds = [], [None] * n_groups(d)
        sts = [d["st"][g] for g in range(n_groups(d))]
        for g in range(n_groups(d)):
            qg, kd = d["ops"][g][0], d["ops"][g][1]
            inter = _dot_nt(stacked(d, qg), sts[g].astype(BF16))
            vbs = [d["v"][rows, (g * hp + j) * LANES:(g * hp + j + 1) * LANES].astype(BF16)
                   for j in range(hp)]
            u = _dot_tn(jnp.concatenate(vbs, axis=1), kd)
            for j in range(hp):
                hr = slice(j * CHUNK, (j + 1) * CHUNK)
                outs.append(_dot(d["scores"][g][hr], vbs[j]) + inter[hr])
                uj = u[j * LANES:(j + 1) * LANES]
                upds[g] = uj if upds[g] is None else jnp.where(lane_mask(d, j), uj, upds[g])
        d["outs"], d["upds"], d["sts"] = outs, upds, sts

    def finish(d, rows):
        for g in range(n_groups(d)):
            d["st"][g] = d["sts"][g] * d["ops"][g][2] + d["upds"][g]
        for h, o in enumerate(d["outs"]):
            _norm_gate_store(d["y"], rows, h * LANES, [o], d["nw"],
                             [d["z"][rows, h * LANES:(h + 1) * LANES]])

    return cumsums, operands, scores, outputs, finish


def _gla_stream(p_ref, s, qk, dk, y_ref, st_s, nw_ref):
    v0, z0, lg0 = 2 * qk, 2 * qk + D_BRANCH, 2 * qk + 2 * D_BRANCH
    return dict(q=p_ref.at[s, :, 0:qk], k=p_ref.at[s, :, qk:v0], v=p_ref.at[s, :, v0:z0],
                z=p_ref.at[s, :, z0:lg0], lg=p_ref.at[s, :, lg0:lg0 + qk],
                y=y_ref.at[s], st=st_s.at[s], nw=nw_ref, dk=dk)


def _mlstm_constants():
    half = LANES // 2
    full0 = half * MLSTM_HEADS
    e = np.zeros((LANES, full0 + 2 * LANES * MLSTM_HEADS), np.float32)
    diff0 = full0 + LANES * MLSTM_HEADS
    sel = np.zeros((16, LANES), np.float32)
    for h in range(MLSTM_HEADS):
        e[SM_F + h, h * half:(h + 1) * half] = 1.0
        e[SM_F + h, full0 + h * LANES:full0 + (h + 1) * LANES] = 1.0
        e[SM_I + h, diff0 + h * LANES:diff0 + (h + 1) * LANES] = 1.0
        e[SM_F + h, diff0 + h * LANES:diff0 + (h + 1) * LANES] = -1.0
        sel[h // 2, SM_I + h] = 1.0
        sel[h // 2, SM_F + h] = -1.0
    return (jnp.asarray(np.concatenate([e, e], axis=0), BF16),
            jnp.asarray(np.concatenate([sel, sel], axis=1), BF16))


def _mlstm_stages(nw_ref, tri_ref, exp_ref, sel_ref):
    lane = _lane_iota()
    is_f = (lane >= SM_F) & (lane < SM_F + MLSTM_HEADS)

    half = LANES // 2
    ri = lax.broadcasted_iota(jnp.int32, (CHUNK, LANES), 0)
    ci = lax.broadcasted_iota(jnp.int32, (CHUNK, LANES), 1)
    causal2 = (ci & (half - 1)) <= ri
    lo_half = lane < half
    even = (lane & 1) == 0
    v0, o0, z0 = 2 * D_BRANCH, 3 * D_BRANCH, 4 * D_BRANCH
    heads = range(MLSTM_HEADS)
    pairs = range(MLSTM_HEADS // 2)
    full0 = half * MLSTM_HEADS
    diff0 = full0 + LANES * MLSTM_HEADS

    def qk_products(d, rows):
        qk_ref = d["p"]
        qbs = [qk_ref[rows, h * LANES:(h + 1) * LANES].astype(BF16) for h in heads]
        d["ks"] = [qk_ref[rows, D_BRANCH + h * LANES:D_BRANCH + (h + 1) * LANES] for h in heads]
        kbs = [k.astype(BF16) for k in d["ks"]]
        zero = jnp.zeros((CHUNK, LANES), BF16)
        ones = jnp.ones((CHUNK, LANES), BF16)
        d["qk_raw"] = [_dot_nt(jnp.concatenate([qbs[2 * p], qbs[2 * p + 1]], axis=1),
                               _side_by_side(kbs[2 * p], kbs[2 * p + 1], zero)) for p in pairs]
        d["qc"] = [_dot(qbs[h], d["c"][h].astype(BF16)) for h in heads]
        d["vaug"] = [jnp.concatenate([d["p"][rows, v0 + h * LANES:v0 + (h + 1) * LANES].astype(BF16),
                                      ones], axis=1) for h in heads]

    def gate_cumsums(ds, rows):
        gcs = [d["g"][rows, :] for d in ds]
        bcols = _sel_dot2(tri_ref[...], jnp.concatenate(gcs, axis=1))
        for i, d in enumerate(ds):
            d["ib"] = jnp.where(is_f, bcols[:, i * LANES:(i + 1) * LANES], gcs[i])

    def gate_sums(d, rows):
        ib = d["ib"]
        d["by_parity"] = jnp.concatenate([jnp.where(even, ib, 0.0), jnp.where(even, 0.0, ib)], axis=0)

    def spread_gates(ds, rows):
        ex = _dot_sel2(jnp.concatenate([d["ib"] for d in ds], axis=0), exp_ref[...])
        drows = _sel_dot2_nt(sel_ref[...], jnp.concatenate([d["by_parity"] for d in ds], axis=0))
        for i, d in enumerate(ds):
            d["ex"] = ex[i * CHUNK:(i + 1) * CHUNK]
            d["drows"] = drows[:, i * LANES:(i + 1) * LANES]

    def weights(d, rows):
        ex = d["ex"]
        mx = d["m"][0:1, :]
        ss, mrs = [], []
        for p in pairs:
            bx = ex[:, p * LANES:(p + 1) * LANES]
            lw = jnp.where(causal2, bx + d["drows"][p:p + 1, :], NEG_BIG)
            mr0 = jnp.max(jnp.where(lo_half, lw, NEG_BIG), axis=-1, keepdims=True)
            mr1 = jnp.max(jnp.where(lo_half, NEG_BIG, lw), axis=-1, keepdims=True)
            m64 = jnp.where(lo_half, mx[:, 2 * p * LANES:(2 * p + 1) * LANES],
                            mx[:, (2 * p + 1) * LANES:(2 * p + 2) * LANES])
            m_row = jnp.maximum(jnp.where(lo_half, mr0, mr1), bx + m64)
            ss.append((d["qk_raw"][p] * jnp.exp(lw - m_row)).astype(BF16))
            mrs += [mr0, mr1]
        d["ss"], d["mrs"] = ss, mrs
        b_last = ex[CHUNK - 1:CHUNK, full0:diff0]
        lwe = ex[:, diff0:] + b_last
        m_new = jnp.maximum(b_last + mx, jnp.max(lwe, axis=0, keepdims=True))
        d["cd"] = jnp.exp(b_last + mx - m_new)
        d["m_new"] = m_new
        kw = jnp.exp(lwe - m_new)
        d["kws"] = [(d["ks"][h] * kw[:, h * LANES:(h + 1) * LANES]).astype(BF16) for h in heads]

    def numerators(d, rows):
        zero = jnp.zeros((CHUNK, 2 * LANES), BF16)
        d["nums"] = [_dot(d["ss"][p], _side_by_side(d["vaug"][2 * p], d["vaug"][2 * p + 1], zero))
                     for p in pairs]
        d["cups"] = [_dot_tn(d["kws"][h], d["vaug"][h]) for h in heads]

    def finish(d, rows):
        ex = d["ex"]
        mx = d["m"][0:1, :]
        for h in heads:
            p, hd = divmod(h, 2)
            hs = slice(h * LANES, (h + 1) * LANES)
            m_inter = ex[:, full0 + h * LANES:full0 + (h + 1) * LANES] + mx[:, hs]
            m_row = jnp.maximum(d["mrs"][h], m_inter)
            inter = jnp.exp(m_inter - m_row)
            sv = d["nums"][p][:, hd * 2 * LANES:(hd + 1) * 2 * LANES]
            num = sv[:, 0:LANES] + inter * d["qc"][h][:, 0:LANES]
            den = sv[:, LANES:] + inter * d["qc"][h][:, LANES:]
            hh = num / jnp.maximum(jnp.abs(den), jnp.exp(-m_row))
            cd = d["cd"][:, hs]
            d["c"][h] = jnp.concatenate([cd, cd], axis=1) * d["c"][h] + d["cups"][h]
            og = d["p"][rows, o0 + h * LANES:o0 + (h + 1) * LANES]
            _norm_gate_store(d["y"], rows, h * LANES, [og * hh], nw_ref,
                             [d["p"][rows, z0 + h * LANES:z0 + (h + 1) * LANES]])
        d["m"][0:1, :] = d["m_new"]

    return qk_products, (gate_cumsums, gate_sums), (spread_gates, weights), numerators, finish


def _ssd_constants():
    e = np.zeros((LANES, SSD_HEADS * SSD_HEAD_DIM), np.float32)
    sel = np.zeros((16, LANES), np.float32)
    for h in range(SSD_HEADS):
        e[SM_DT + h, h * SSD_HEAD_DIM:(h + 1) * SSD_HEAD_DIM] = 1.0
        sel[h // 2, SM_DT + h] = 1.0
    return (jnp.asarray(np.concatenate([e, e], axis=0), BF16),
            jnp.asarray(np.concatenate([sel, sel], axis=1), BF16))


def _ssd_stages(alog_ref, dx_ref, nw_ref, tri_ref, sel_ref, exp_ref):
    lane = _lane_iota()
    is_dt = (lane >= SM_DT) & (lane < SM_DT + SSD_HEADS)

    half = LANES // 2
    ri = lax.broadcasted_iota(jnp.int32, (CHUNK, LANES), 0)
    ci = lax.broadcasted_iota(jnp.int32, (CHUNK, LANES), 1)
    causal2 = (ci & (half - 1)) <= ri
    lo_half = lane < half
    even = (lane & 1) == 0
    a_lane = jnp.where(is_dt, -jnp.exp(alog_ref[...]), 0.0)
    b0, c0, z0 = D_BRANCH, D_BRANCH + SSD_BC, D_BRANCH + 2 * SSD_BC
    groups = range(SSD_GROUPS)
    group_w = D_BRANCH // SSD_GROUPS
    pairs = range(D_BRANCH // LANES)
    pairs_per_group = group_w // LANES

    def products(d, rows):
        xbc = d["p"]
        d["bgs"] = [xbc[rows, b0 + g * SSD_STATE:b0 + (g + 1) * SSD_STATE].astype(BF16) for g in groups]
        cgbs = [xbc[rows, c0 + g * SSD_STATE:c0 + (g + 1) * SSD_STATE].astype(BF16) for g in groups]
        d["cb2"] = [_dot_nt(cgbs[g], jnp.concatenate([d["bgs"][g], d["bgs"][g]], axis=0))
                    for g in groups]
        d["cst"] = [_dot(cgbs[g], d["st"][g].astype(BF16)) for g in groups]

    def decay_cumsums(ds, rows):
        dts = [d["dt"][rows, :] for d in ds]
        acs = _sel_dot2(tri_ref[...], jnp.concatenate([dt * a_lane for dt in dts], axis=1))
        for i, d in enumerate(ds):
            d["dt_c"], d["acs"] = dts[i], acs[:, i * LANES:(i + 1) * LANES]

    def decay_sums(d, rows):
        acs = d["acs"]
        d["dt_acs"] = jnp.concatenate([d["dt_c"], acs], axis=0)
        d["by_parity"] = jnp.concatenate([jnp.where(even, acs, 0.0), jnp.where(even, 0.0, acs)], axis=0)

    def spread_decays(ds, rows):
        ex = _dot_sel2(jnp.concatenate([d["dt_acs"] for d in ds], axis=0), exp_ref[...])
        a_rows = _sel_dot2_nt(sel_ref[...], jnp.concatenate([d["by_parity"] for d in ds], axis=0))
        for i, d in enumerate(ds):
            d["dtx"] = ex[2 * i * CHUNK:(2 * i + 1) * CHUNK]
            d["acs_x"] = ex[(2 * i + 1) * CHUNK:(2 * i + 2) * CHUNK]
            d["a_rows"] = a_rows[:, i * LANES:(i + 1) * LANES]

    def decays(d, rows):
        d["xss"], d["ms"], d["xblk"], xdecs = [], [], [], []
        for p in pairs:
            ls = slice(p * LANES, (p + 1) * LANES)
            ax = d["acs_x"][:, ls]
            lmat = jnp.exp(jnp.where(causal2, ax - d["a_rows"][p:p + 1, :], NEG_BIG))
            d["ms"].append((d["cb2"][p // pairs_per_group] * lmat).astype(BF16))
            xs = d["p"][rows, ls]
            xdt = xs * d["dtx"][:, ls]
            d["xss"].append(xs)
            d["xblk"].append(jnp.concatenate([jnp.where(lo_half, xdt, 0.0), jnp.where(lo_half, 0.0, xdt)],
                                             axis=0).astype(BF16))
            xdecs.append((xdt * jnp.exp(ax[CHUNK - 1:CHUNK, :] - ax)).astype(BF16))
        d["xdec"] = [jnp.concatenate(xdecs[g * pairs_per_group:(g + 1) * pairs_per_group], axis=1)
                     for g in groups]

    def chunk_products(d, rows):
        d["yds"] = [_dot(d["ms"][p], d["xblk"][p]) for p in pairs]
        d["ups"] = [_dot_tn(d["bgs"][g], d["xdec"][g]) for g in groups]

    def finish(d, rows):
        for g in groups:
            gs = slice(g * group_w, (g + 1) * group_w)
            eax = jnp.exp(d["acs_x"][:, gs])
            d["st"][g] = d["st"][g] * eax[CHUNK - 1:CHUNK, :] + d["ups"][g]
            ys = []
            for pp in range(pairs_per_group):
                p = g * pairs_per_group + pp
                ls = slice(p * LANES, (p + 1) * LANES)
                y = (d["cst"][g][:, pp * LANES:(pp + 1) * LANES] * eax[:, pp * LANES:(pp + 1) * LANES]
                     + dx_ref[:, ls] * d["xss"][p] + d["yds"][p])
                ys.append(y * d["p"][rows, z0 + p * LANES:z0 + (p + 1) * LANES])
            _norm_gate_store(d["y"], rows, g * group_w, ys, nw_ref, [None] * len(ys))

    return products, (decay_cumsums, decay_sums), (spread_decays, decays), chunk_products, finish


def _mixers_kernel(pa_ref, pb_ref, pc_ref, pd_ref, sm_ref, nwa_ref, nwb_ref, nwc_ref, nwd_ref,
                   ms_ref, mk_ref, tri_ref, mexp_ref, msel_ref, alog_ref, dx_ref, dsel_ref, dexp_ref,
                   ya_ref, yb_ref, yc_ref, yd_ref, sta_s, stc_s, cb_s, mb_s, std_s, *, tile):
    @pl.when(pl.program_id(0) == 0)
    def _():
        for r in (sta_s, stc_s, cb_s, mb_s, std_s):
            r[...] = jnp.zeros_like(r)

    gla = _gla_stages(ms_ref, mk_ref)
    mlstm = _mlstm_stages(nwb_ref, tri_ref, mexp_ref, msel_ref)
    ssd = _ssd_stages(alog_ref, dx_ref, nwd_ref, tri_ref, dsel_ref, dexp_ref)
    work = []
    for s in range(pa_ref.shape[0]):
        work.append((_gla_stream(pa_ref, s, GLA_QK, GLA_DK, ya_ref, sta_s, nwa_ref), gla))
        work.append((dict(p=pb_ref.at[s], g=sm_ref.at[s], c=cb_s.at[s], m=mb_s.at[s], y=yb_ref.at[s]),
                     mlstm))
        work.append((_gla_stream(pc_ref, s, HGRN_QF, HGRN_DK, yc_ref, stc_s, nwc_ref), gla))
        work.append((dict(p=pd_ref.at[s], dt=sm_ref.at[s], st=std_s.at[s], y=yd_ref.at[s]), ssd))
    _run_chunks(work, tile)


def _mixers(tok_inputs, const_inputs, n_batch, seq, tile):
    tok = lambda t: (0, t, 0)
    in_specs = [pl.BlockSpec((n_batch, tile, a.shape[2]), tok) for a in tok_inputs]
    for a in const_inputs:
        in_specs.append(pl.BlockSpec(a.shape, lambda t, nd=a.ndim: (0,) * nd))
    scratch = [pltpu.VMEM((n_batch, GLA_QK // LANES, GLA_DV, LANES), F32),
               pltpu.VMEM((n_batch, HGRN_QF // LANES, HGRN_DV, LANES), F32),
               pltpu.VMEM((n_batch, MLSTM_HEADS, MLSTM_DH, 2 * MLSTM_DH), F32),
               pltpu.VMEM((n_batch, 8, MLSTM_HEADS * LANES), F32),
               pltpu.VMEM((n_batch, SSD_GROUPS, SSD_STATE, D_BRANCH // SSD_GROUPS), F32)]
    return pl.pallas_call(
        functools.partial(_mixers_kernel, tile=tile), grid=(seq // tile,), in_specs=in_specs,
        out_specs=[pl.BlockSpec((n_batch, tile, D_BRANCH), tok)] * 4,
        out_shape=[jax.ShapeDtypeStruct((n_batch, seq, D_BRANCH), BF16)] * 4,
        scratch_shapes=scratch,
        compiler_params=pltpu.CompilerParams(dimension_semantics=("arbitrary",),
                                             vmem_limit_bytes=VMEM_LIMIT),
        name="mixers",
    )(*tok_inputs, *const_inputs)


def _pad_lanes(parts, total=LANES):
    width = sum(p.shape[-1] for p in parts)
    lead = parts[0].shape[:-1]
    return jnp.concatenate(list(parts) + [jnp.zeros(lead + (total - width,), parts[0].dtype)], axis=-1)


def _small_vector(i_part, f_part, dt_part):
    z = jnp.zeros((GLA_GATE_RANK,), F32)
    return _pad_lanes([z, i_part.astype(F32), f_part.astype(F32), dt_part.astype(F32)])[None, :]


def kernel(x, norm_w, w_in, gla_gate_w, gla_gate_b, gla_norm_w, ml_conv_w, ml_conv_b, ml_i_b, ml_f_b,
           ml_norm_w, hg_lb_logits, hg_norm_w, ssd_conv_w, ssd_conv_b, ssd_dt_bias, ssd_A_log, ssd_D,
           ssd_norm_w, w_out, final_norm_w):
    n_batch, seq, _ = x.shape
    depth = w_in.shape[0]
    tile = min(MIX_TILE, seq)
    n_tok = n_batch * seq

    mstack, masks = _gla_constants()
    tri_c = _tri2(CHUNK)
    ml_exp, ml_sel = _mlstm_constants()
    ssd_exp, ssd_sel = _ssd_constants()
    zero4 = jnp.zeros((MLSTM_HEADS,), F32)
    in_tile = min(IN_TILE, seq)

    p = jax.nn.softmax(hg_lb_logits.astype(F32), axis=0)
    lower_bounds = jnp.cumsum(p, axis=0) - p[0:1]

    w_wide, w_small = _realign(jnp.swapaxes(w_in, 1, 2), REALIGN_ROWS)

    h = x.reshape(n_tok, D_MODEL)
    row2 = lambda v: v.astype(F32).reshape(1, -1)
    for l in range(depth):
        gate_w = jnp.concatenate(
            [gla_gate_w[l].astype(F32), jnp.zeros((LANES - GLA_GATE_RANK, GLA_QK), F32)], axis=0)
        consts = [gate_w, row2(gla_gate_b[l]), row2(lower_bounds[l]),
                  ml_conv_w[l].astype(F32), row2(ml_conv_b[l]),
                  ssd_conv_w[l].astype(F32), row2(ssd_conv_b[l]),
                  _small_vector(ml_i_b[l], ml_f_b[l], ssd_dt_bias[l])]
        pa, pb, pc, pd, sm = [p.reshape(n_batch, seq, -1)
                              for p in _inproj(h, row2(norm_w[l]), w_wide, w_small, l, consts, in_tile,
                                               seq // in_tile)]

        ys = _mixers(
            [pa, pb, pc, pd, sm],
            [row2(gla_norm_w[l]), row2(ml_norm_w[l]), row2(hg_norm_w[l]), row2(ssd_norm_w[l]),
             mstack, masks, tri_c, ml_exp, ml_sel, _small_vector(zero4, zero4, ssd_A_log[l]),
             row2(jnp.repeat(ssd_D[l].astype(F32), SSD_HEAD_DIM)), ssd_sel, ssd_exp],
            n_batch, seq, tile)
        ys = [y.reshape(n_tok, D_BRANCH) for y in ys]
        h = _outproj(h, ys, w_out[l].astype(BF16), row2(final_norm_w), l == depth - 1,
                     min(OUT_TILE, n_tok))
    return h.reshape(n_batch, seq, D_MODEL)
```

```python
import functools

import numpy as np
import jax
import jax.numpy as jnp
from jax import lax
from jax.experimental import pallas as pl
from jax.experimental.pallas import tpu as pltpu

F32 = jnp.float32
BF16 = jnp.bfloat16

D_MODEL = 1024
D_BRANCH = 512
EPS = 1e-6
NEG_BIG = -1e30

GLA_HEADS, GLA_DK, GLA_DV = 4, 64, 128
GLA_GATE_RANK, GLA_GATE_NORM = 16, 16.0
MLSTM_HEADS, MLSTM_DH, MLSTM_CONV = 4, 128, 4
HGRN_HEADS, HGRN_DK, HGRN_DV = 4, 128, 128
SSD_HEAD_DIM, SSD_HEADS, SSD_GROUPS, SSD_STATE, SSD_CONV = 64, 8, 2, 128, 4
GLA_QK = GLA_HEADS * GLA_DK
HGRN_QF = HGRN_HEADS * HGRN_DK
SSD_BC = SSD_GROUPS * SSD_STATE
PROJ_SIZES = (
    GLA_QK, GLA_QK, D_BRANCH, GLA_GATE_RANK, D_BRANCH,
    D_BRANCH, D_BRANCH, D_BRANCH, MLSTM_HEADS, MLSTM_HEADS, D_BRANCH, D_BRANCH,
    HGRN_QF, HGRN_QF, D_BRANCH, D_BRANCH,
    D_BRANCH, SSD_BC, SSD_BC, SSD_HEADS, D_BRANCH,
)

LANES = 128
HIST = 8
VMEM_LIMIT = 56 * 1024 * 1024

REALIGN_ROWS = 256
IN_TILE = 256
MIX_TILE = 256
OUT_TILE = 512

CHUNK = 64
N_LEVELS = 6
MIN_VPU_LEVEL = 4

SM_GR, SM_I, SM_F, SM_DT = 0, 16, 20, 24

W_A = 2 * GLA_QK + 2 * D_BRANCH
W_B = 5 * D_BRANCH
W_C = 2 * HGRN_QF + 2 * D_BRANCH
W_D = 2 * D_BRANCH + 2 * SSD_BC
COL_A = 0
COL_B = COL_A + W_A
COL_C = COL_B + W_B
COL_D = COL_C + W_C
COL_S = COL_D + W_D


def _gla_constants():
    c = CHUNK
    t = np.arange(c)[:, None]
    d = np.arange(c)[None, :]
    blocks = [(d <= t), (d > t)]
    masks = [np.eye(c, dtype=bool)]
    for l in range(N_LEVELS):
        s = c >> (l + 1)
        mid_t = (t // (2 * s)) * (2 * s) + s
        upper = t >= mid_t
        if 1 < s < MIN_VPU_LEVEL:
            blocks.append(np.where(upper, (d >= mid_t) & (d <= t), (d > t) & (d <= mid_t - 1)))
        same = (t // (2 * s)) == (d // (2 * s))
        masks.append(same & upper & (d < mid_t))
    mstack = np.concatenate(blocks, axis=0).astype(np.float32)
    mstack = np.concatenate([mstack, mstack], axis=1)
    masks = np.stack(masks).astype(np.float32)
    return jnp.asarray(mstack, BF16), jnp.asarray(np.concatenate([masks, masks], axis=-1))


def _tri2(n):
    tri = np.tril(np.ones((n, n), np.float32))
    return jnp.asarray(np.concatenate([tri, tri], axis=1), BF16)


def _dot(a, b):
    return jnp.dot(a, b, preferred_element_type=F32)


def _dot_nt(a, b):
    return lax.dot_general(a, b, (((1,), (1,)), ((), ())), preferred_element_type=F32)


def _dot_tn(a, b):
    return lax.dot_general(a, b, (((0,), (0,)), ((), ())), preferred_element_type=F32)


def _sel_dot2(sel2, x):
    hi = x.astype(BF16)
    mid = (x - hi.astype(F32)).astype(BF16)
    return _dot(sel2, jnp.concatenate([hi, mid], axis=0))


def _sel_dot2_nt(sel2, x):
    hi = x.astype(BF16)
    mid = (x - hi.astype(F32)).astype(BF16)
    return _dot_nt(sel2, jnp.concatenate([hi, mid], axis=1))


def _level_exponent(gcs, s):
    pieces = []
    for b in range(0, CHUNK, 2 * s):
        ref_row = gcs[b + s - 1:b + s, :]
        if s >= 8:
            pieces += [ref_row - gcs[b:b + s], gcs[b + s:b + 2 * s] - ref_row]
        else:
            diff = gcs[b:b + 2 * s] - ref_row
            pieces.append(jnp.minimum(diff, -diff))
    return jnp.concatenate(pieces, axis=0)


def _dot_sel2(x, sel2):
    hi = x.astype(BF16)
    mid = (x - hi.astype(F32)).astype(BF16)
    return _dot(jnp.concatenate([hi, mid], axis=1), sel2)


def _dot_f32(a, b):
    ah = a.astype(BF16)
    al = (a - ah.astype(F32)).astype(BF16)
    bh = b.astype(BF16)
    bl = (b - bh.astype(F32)).astype(BF16)
    return _dot(ah, bh) + _dot(al, bh) + _dot(ah, bl)


def _softplus(x):
    return jnp.maximum(x, 0.0) + jnp.log1p(jnp.exp(-jnp.abs(x)))


def _log_sigmoid(x):
    return -_softplus(-x)


def _silu(x):
    return x * jax.nn.sigmoid(x)


def _lane_iota():
    return lax.broadcasted_iota(jnp.int32, (1, LANES), 1)


def _side_by_side(a, b, zero):
    return jnp.concatenate([jnp.concatenate([a, zero], axis=1),
                            jnp.concatenate([zero, b], axis=1)], axis=0)


def _wide_segments():
    segs, src, dst = [], 0, 0
    run_start = None
    for size in PROJ_SIZES + (0,):
        wide = size >= LANES
        if wide and run_start is None:
            run_start = src
        if not wide and run_start is not None:
            segs.append((run_start, dst, src - run_start))
            dst += src - run_start
            run_start = None
        src += size
    return segs


def _realign_kernel(wt_ref, o_ref, os_ref):
    for src, dst, n in _wide_segments():
        for r in range(0, n, LANES):
            o_ref[:, dst + r:dst + r + LANES] = wt_ref[src + r:src + r + LANES, :].T.astype(BF16)
    narrow, src = [], 0
    for size in PROJ_SIZES:
        if size < LANES:
            narrow.append(wt_ref[src:src + size, :])
        src += size
    used = sum(p.shape[0] for p in narrow)
    narrow.append(jnp.zeros((LANES - used, wt_ref.shape[1]), F32))
    os_ref[...] = jnp.concatenate(narrow, axis=0).T


def _realign(wt, rows):
    depth, n_proj, d_model = wt.shape
    return pl.pallas_call(
        _realign_kernel, grid=(depth, d_model // rows),
        in_specs=[pl.BlockSpec((None, n_proj, rows), lambda l, j: (l, 0, j))],
        out_specs=[pl.BlockSpec((None, rows, COL_S), lambda l, j: (l, j, 0)),
                   pl.BlockSpec((None, rows, LANES), lambda l, j: (l, j, 0))],
        out_shape=[jax.ShapeDtypeStruct((depth, d_model, COL_S), BF16),
                   jax.ShapeDtypeStruct((depth, d_model, LANES), F32)],
        compiler_params=pltpu.CompilerParams(dimension_semantics=("arbitrary", "arbitrary"),
                                             vmem_limit_bytes=VMEM_LIMIT),
        name="realign",
    )(wt)


CONV_ROWS, CONV_LANES = 256, 128


def _causal_conv_silu(raw, xe_ref, hist_ref, w_ref, b_ref, taps, out_ref, scale=None):
    rows, n = raw.shape
    xe_ref[0:HIST, :] = hist_ref[...]
    xe_ref[HIST:HIST + rows, :] = raw
    hist_ref[...] = raw[rows - HIST:rows, :]
    for r in range(0, rows, CONV_ROWS):
        for l in range(0, n, CONV_LANES):
            cs = slice(l, l + CONV_LANES)
            acc = b_ref[:, cs] + w_ref[taps - 1:taps, cs] * xe_ref[HIST + r:HIST + r + CONV_ROWS, cs]
            for k in range(taps - 1):
                off = HIST - (taps - 1) + k + r
                acc = acc + w_ref[k:k + 1, cs] * xe_ref[off:off + CONV_ROWS, cs]
            out = _silu(acc)
            out_ref[r:r + CONV_ROWS, cs] = out if scale is None else out * scale[:, cs]


def _inproj_kernel(x_ref, xn_ref, nw_ref, w_ref, ws_ref, gw_ref, gb_ref, lb_ref, cwb_ref, cbb_ref, cwd_ref,
                   cbd_ref, gbias_ref, oa_ref, ob_ref, oc_ref, od_ref, os_ref, xe_s, hist_b, hist_d, u_s,
                   *, tiles_per_seq):
    step = pl.program_id(0)

    @pl.when(step % tiles_per_seq == 0)
    def _():
        hist_b[...] = jnp.zeros_like(hist_b)
        hist_d[...] = jnp.zeros_like(hist_d)

    def normed(x):
        ms = jnp.mean(x * x, axis=-1, keepdims=True)
        return (x * lax.rsqrt(ms + EPS) * nw_ref[...]).astype(BF16)

    @pl.when(step == 0)
    def _():
        u_s[0] = normed(x_ref[...])

    u = u_s[step % 2]
    lane = _lane_iota()

    small = _dot(u, ws_ref[...].astype(BF16))
    biased = small + gbias_ref[...]
    is_f = (lane >= SM_F) & (lane < SM_F + MLSTM_HEADS)
    is_dt = (lane >= SM_DT) & (lane < SM_DT + SSD_HEADS)
    os_ref[...] = jnp.where(is_f, _log_sigmoid(biased), jnp.where(is_dt, _softplus(biased), biased))

    raw = _dot(u, w_ref[:, COL_B:COL_B + W_B])
    conv_w = 2 * D_BRANCH
    k_scale = jnp.where(lax.broadcasted_iota(jnp.int32, (1, conv_w), 1) < D_BRANCH, 1.0, MLSTM_DH ** -0.5)
    _causal_conv_silu(raw[:, 0:conv_w], xe_s, hist_b, cwb_ref, cbb_ref, MLSTM_CONV, ob_ref, k_scale)
    o0, z0 = 3 * D_BRANCH, 4 * D_BRANCH
    ob_ref[:, conv_w:o0] = raw[:, conv_w:o0]
    ob_ref[:, o0:z0] = jax.nn.sigmoid(raw[:, o0:z0])
    ob_ref[:, z0:W_B] = _silu(raw[:, z0:W_B])

    raw = _dot(u, w_ref[:, COL_D:COL_D + W_D])
    conv_w = D_BRANCH + 2 * SSD_BC
    _causal_conv_silu(raw[:, 0:conv_w], xe_s, hist_d, cwd_ref, cbd_ref, SSD_CONV, od_ref)
    od_ref[:, conv_w:W_D] = _silu(raw[:, conv_w:W_D])
    u_s[(step + 1) % 2] = normed(xn_ref[...])

    raw = _dot(u, w_ref[:, COL_C:COL_C + W_C])
    lb = lb_ref[...]
    fr = raw[:, HGRN_QF:2 * HGRN_QF]
    oc_ref[:, 0:HGRN_QF] = raw[:, 0:HGRN_QF] * (HGRN_DK ** -0.5)
    oc_ref[:, HGRN_QF:2 * HGRN_QF] = (1.0 - lb) * jax.nn.sigmoid(-fr)
    z0 = 2 * HGRN_QF + D_BRANCH
    oc_ref[:, 2 * HGRN_QF:z0] = raw[:, 2 * HGRN_QF:z0]
    oc_ref[:, z0:W_C] = _silu(raw[:, z0:W_C])
    oc_ref[:, W_C:W_C + HGRN_QF] = jnp.log(jnp.maximum(lb + (1.0 - lb) * jax.nn.sigmoid(fr), 1e-30))

    gate = _dot_f32(small, gw_ref[...]) + gb_ref[...]
    oa_ref[:, W_A:W_A + GLA_QK] = _log_sigmoid(gate) * (1.0 / GLA_GATE_NORM)
    raw = _dot(u, w_ref[:, COL_A:COL_A + W_A])
    oa_ref[:, 0:GLA_QK] = raw[:, 0:GLA_QK] * (GLA_DK ** -0.5)
    z0 = 2 * GLA_QK + D_BRANCH
    oa_ref[:, GLA_QK:z0] = raw[:, GLA_QK:z0]
    oa_ref[:, z0:W_A] = _silu(raw[:, z0:W_A])


def _inproj(h, norm_w, w_wide, w_small, layer, consts, tile, tiles_per_seq):
    n_tok = h.shape[0]
    const = lambda i: (0, 0)
    row = lambda i: (i, 0)
    n_tiles = n_tok // tile
    in_specs = [pl.BlockSpec((tile, D_MODEL), row),
                pl.BlockSpec((tile, D_MODEL), lambda i: (jnp.minimum(i + 1, n_tiles - 1), 0)),
                pl.BlockSpec((1, D_MODEL), const),
                pl.BlockSpec((None, D_MODEL, COL_S), lambda i: (layer, 0, 0), pipeline_mode=pl.Buffered(1)),
                pl.BlockSpec((None, D_MODEL, LANES), lambda i: (layer, 0, 0))]
    in_specs += [pl.BlockSpec(c.shape, const) for c in consts]
    widths = (W_A + GLA_QK, W_B, W_C + HGRN_QF, W_D, LANES)
    conv_w = 2 * D_BRANCH
    return pl.pallas_call(
        functools.partial(_inproj_kernel, tiles_per_seq=tiles_per_seq),
        grid=(n_tiles,), in_specs=in_specs,
        out_specs=[pl.BlockSpec((tile, w), row) for w in widths],
        out_shape=[jax.ShapeDtypeStruct((n_tok, w), F32) for w in widths],
        scratch_shapes=[pltpu.VMEM((tile + HIST, conv_w), F32), pltpu.VMEM((HIST, conv_w), F32),
                        pltpu.VMEM((HIST, conv_w), F32), pltpu.VMEM((2, tile, D_MODEL), BF16)],
        compiler_params=pltpu.CompilerParams(dimension_semantics=("arbitrary",),
                                             vmem_limit_bytes=VMEM_LIMIT),
        name="inproj",
    )(h, h, norm_w, w_wide, w_small, *consts)


def _outproj_kernel(h_ref, ya_ref, yb_ref, yc_ref, yd_ref, w_ref, fw_ref, o_ref, *, final):
    acc = h_ref[...]
    for i, y_ref in enumerate((ya_ref, yb_ref, yc_ref, yd_ref)):
        acc = acc + _dot(y_ref[...], w_ref[i * D_BRANCH:(i + 1) * D_BRANCH, :])
    if final:
        ms = jnp.mean(acc * acc, axis=-1, keepdims=True)
        acc = acc * lax.rsqrt(ms + EPS) * fw_ref[...]
    o_ref[...] = acc


def _outproj(h, ys, w_out, final_w, final, tile):
    n_tok = h.shape[0]
    const = lambda i: (0, 0)
    row = lambda i: (i, 0)
    in_specs = [pl.BlockSpec((tile, D_MODEL), row)]
    in_specs += [pl.BlockSpec((tile, D_BRANCH), row) for _ in ys]
    in_specs += [pl.BlockSpec(w_out.shape, const), pl.BlockSpec((1, D_MODEL), const)]
    return pl.pallas_call(
        functools.partial(_outproj_kernel, final=final),
        grid=(n_tok // tile,), in_specs=in_specs,
        out_specs=pl.BlockSpec((tile, D_MODEL), row),
        out_shape=jax.ShapeDtypeStruct((n_tok, D_MODEL), F32),
        compiler_params=pltpu.CompilerParams(dimension_semantics=("arbitrary",),
                                             vmem_limit_bytes=VMEM_LIMIT),
        name="outproj",
    )(h, *ys, w_out, final_w)


def _norm_gate_store(y_ref, rows, col0, parts, nw_ref, z_parts):
    width = sum(p.shape[-1] for p in parts)
    ss = sum(jnp.sum(p * p, axis=-1, keepdims=True) for p in parts)
    scale = lax.rsqrt(ss * (1.0 / width) + EPS)
    c = col0
    for p, z in zip(parts, z_parts):
        w = p.shape[-1]
        out = p * scale * nw_ref[:, c:c + w]
        if z is not None:
            out = out * z
        y_ref[rows, c:c + w] = out.astype(y_ref.dtype)
        c += w


N_STAGES = 5
CHUNK_UNROLL = 1


def _run_chunks(work, tile):
    def body(c, carry):
        rows = pl.ds(pl.multiple_of(c * CHUNK, CHUNK), CHUNK)
        live = [(dict(d), stages) for d, stages in work]
        for k in range(N_STAGES):
            joined = []
            for d, stages in live:
                joint, fn = stages[k] if isinstance(stages[k], tuple) else (None, stages[k])
                if joint is not None and not any(stages is s for s in joined):
                    joint([dd for dd, st in live if st is stages], rows)
                    joined.append(stages)
                fn(d, rows)
        return carry

    lax.fori_loop(0, tile // CHUNK, body, 0, unroll=CHUNK_UNROLL)


def _gla_stages(ms_ref, mk_ref):
    lane = _lane_iota()

    def lane_mask(d, j):
        dk = d["dk"]
        return None if dk == LANES else (lane >= j * dk) & (lane < (j + 1) * dk)

    def pick(a, lm):
        return a if lm is None else jnp.where(lm, a, jnp.zeros_like(a))

    def n_groups(d):
        return d["q"].shape[1] // LANES

    def cumsums(ds, rows):
        lgs = [d["lg"][rows, :] for d in ds]
        es = _sel_dot2(ms_ref[...], jnp.concatenate(lgs, axis=1))
        col = 0
        for d, lg in zip(ds, lgs):
            d["es"] = es[:, col:col + lg.shape[1]]
            col += lg.shape[1]

    def operands(d, rows):
        ops = []
        for g in range(n_groups(d)):
            ls = slice(g * LANES, (g + 1) * LANES)
            e = d["es"][:, ls]
            q = d["q"][rows, ls]
            k = d["k"][rows, ls]
            gcs = e[0:CHUNK]
            qg = (q * jnp.exp(gcs)).astype(BF16)
            kd = (k * jnp.exp(e[CHUNK:2 * CHUNK])).astype(BF16)
            dec = jnp.exp(gcs[CHUNK - 1:CHUNK, :])
            qb = q.astype(BF16)
            kb = k.astype(BF16)
            ql, kl = [q * k], [None]
            mxu_block = 2
            for l in range(N_LEVELS):
                s = CHUNK >> (l + 1)
                if s == 1:
                    ql.append(q * jnp.exp(d["lg"][rows, ls]) * pltpu.roll(k, 1, axis=0))
                    kl.append(None)
                    continue
                if s >= MIN_VPU_LEVEL:
                    el = _level_exponent(gcs, s)
                else:
                    el = e[mxu_block * CHUNK:(mxu_block + 1) * CHUNK]
                    mxu_block += 1
                w = jnp.exp(el).astype(BF16)
                ql.append(qb * w)
                kl.append(kb * w)
            ops.append((qg, kd, dec, ql, kl))
        d["ops"] = ops

    def stacked(d, a):
        hp = LANES // d["dk"]
        return a if hp == 1 else jnp.concatenate([pick(a, lane_mask(d, j)) for j in range(hp)], axis=0)

    def scores(d, rows):
        hp = LANES // d["dk"]
        mask_rows = lambda l: jnp.concatenate([mk_ref[l][:, 0:CHUNK]] * hp, axis=0)
        groups = []
        for g in range(n_groups(d)):
            ql, kl = d["ops"][g][3], d["ops"][g][4]
            a = None
            for l in range(N_LEVELS + 1):
                if kl[l] is None:
                    s = jnp.sum(stacked(d, ql[l]), axis=-1, keepdims=True)
                else:
                    s = _dot_nt(stacked(d, ql[l]), kl[l])
                s = s * mask_rows(l)
                a = s if a is None else a + s
            groups.append(a.astype(BF16))
        d["scores"] = groups

    def outputs(d, rows):
        hp = LANES // d["dk"]
        outs, upds = [], [None] * n_groups(d)
        sts = [d["st"][g] for g in range(n_groups(d))]
        for g in range(n_groups(d)):
            qg, kd = d["ops"][g][0], d["ops"][g][1]
            inter = _dot_nt(stacked(d, qg), sts[g].astype(BF16))
            for j in range(hp):
                h = g * hp + j
                hr = slice(j * CHUNK, (j + 1) * CHUNK)
                vh = d["v"][rows, h * LANES:(h + 1) * LANES].astype(BF16)
                outs.append(_dot(d["scores"][g][hr], vh) + inter[hr])
                u = _dot_tn(vh, kd)
                upds[g] = u if upds[g] is None else jnp.where(lane_mask(d, j), u, upds[g])
        d["outs"], d["upds"], d["sts"] = outs, upds, sts

    def finish(d, rows):
        for g in range(n_groups(d)):
            d["st"][g] = d["sts"][g] * d["ops"][g][2] + d["upds"][g]
        for h, o in enumerate(d["outs"]):
            _norm_gate_store(d["y"], rows, h * LANES, [o], d["nw"],
                             [d["z"][rows, h * LANES:(h + 1) * LANES]])

    return (cumsums, lambda d, rows: None), operands, scores, outputs, finish


def _gla_stream(p_ref, s, qk, dk, y_ref, st_s, nw_ref):
    v0, z0, lg0 = 2 * qk, 2 * qk + D_BRANCH, 2 * qk + 2 * D_BRANCH
    return dict(q=p_ref.at[s, :, 0:qk], k=p_ref.at[s, :, qk:v0], v=p_ref.at[s, :, v0:z0],
                z=p_ref.at[s, :, z0:lg0], lg=p_ref.at[s, :, lg0:lg0 + qk],
                y=y_ref.at[s], st=st_s.at[s], nw=nw_ref, dk=dk)


def _mlstm_constants():
    half = LANES // 2
    full0 = half * MLSTM_HEADS
    e = np.zeros((LANES, full0 + 2 * LANES * MLSTM_HEADS), np.float32)
    diff0 = full0 + LANES * MLSTM_HEADS
    sel = np.zeros((16, LANES), np.float32)
    for h in range(MLSTM_HEADS):
        e[SM_F + h, h * half:(h + 1) * half] = 1.0
        e[SM_F + h, full0 + h * LANES:full0 + (h + 1) * LANES] = 1.0
        e[SM_I + h, diff0 + h * LANES:diff0 + (h + 1) * LANES] = 1.0
        e[SM_F + h, diff0 + h * LANES:diff0 + (h + 1) * LANES] = -1.0
        sel[h // 2, SM_I + h] = 1.0
        sel[h // 2, SM_F + h] = -1.0
    return (jnp.asarray(np.concatenate([e, e], axis=0), BF16),
            jnp.asarray(np.concatenate([sel, sel], axis=1), BF16))


def _mlstm_stages(nw_ref, tri_ref, exp_ref, sel_ref):
    lane = _lane_iota()
    is_f = (lane >= SM_F) & (lane < SM_F + MLSTM_HEADS)

    half = LANES // 2
    ri = lax.broadcasted_iota(jnp.int32, (CHUNK, LANES), 0)
    ci = lax.broadcasted_iota(jnp.int32, (CHUNK, LANES), 1)
    causal2 = (ci & (half - 1)) <= ri
    lo_half = lane < half
    even = (lane & 1) == 0
    v0, o0, z0 = 2 * D_BRANCH, 3 * D_BRANCH, 4 * D_BRANCH
    heads = range(MLSTM_HEADS)
    pairs = range(MLSTM_HEADS // 2)
    full0 = half * MLSTM_HEADS
    diff0 = full0 + LANES * MLSTM_HEADS

    def qk_products(d, rows):
        qk_ref = d["p"]
        qbs = [qk_ref[rows, h * LANES:(h + 1) * LANES].astype(BF16) for h in heads]
        d["ks"] = [qk_ref[rows, D_BRANCH + h * LANES:D_BRANCH + (h + 1) * LANES] for h in heads]
        kbs = [k.astype(BF16) for k in d["ks"]]
        zero = jnp.zeros((CHUNK, LANES), BF16)
        ones = jnp.ones((CHUNK, LANES), BF16)
        d["qk_raw"] = [_dot_nt(jnp.concatenate([qbs[2 * p], qbs[2 * p + 1]], axis=1),
                               _side_by_side(kbs[2 * p], kbs[2 * p + 1], zero)) for p in pairs]
        d["qc"] = [_dot(qbs[h], d["c"][h].astype(BF16)) for h in heads]
        d["vaug"] = [jnp.concatenate([d["p"][rows, v0 + h * LANES:v0 + (h + 1) * LANES].astype(BF16),
                                      ones], axis=1) for h in heads]

    def gate_cumsums(ds, rows):
        gcs = [d["g"][rows, :] for d in ds]
        bcols = _sel_dot2(tri_ref[...], jnp.concatenate(gcs, axis=1))
        for i, d in enumerate(ds):
            d["ib"] = jnp.where(is_f, bcols[:, i * LANES:(i + 1) * LANES], gcs[i])

    def gate_sums(d, rows):
        ib = d["ib"]
        d["by_parity"] = jnp.concatenate([jnp.where(even, ib, 0.0), jnp.where(even, 0.0, ib)], axis=0)

    def spread_gates(ds, rows):
        ex = _dot_sel2(jnp.concatenate([d["ib"] for d in ds], axis=0), exp_ref[...])
        drows = _sel_dot2_nt(sel_ref[...], jnp.concatenate([d["by_parity"] for d in ds], axis=0))
        for i, d in enumerate(ds):
            d["ex"] = ex[i * CHUNK:(i + 1) * CHUNK]
            d["drows"] = drows[:, i * LANES:(i + 1) * LANES]

    def weights(d, rows):
        ex = d["ex"]
        mx = d["m"][0:1, :]
        ss, mrs = [], []
        for p in pairs:
            bx = ex[:, p * LANES:(p + 1) * LANES]
            lw = jnp.where(causal2, bx + d["drows"][p:p + 1, :], NEG_BIG)
            mr0 = jnp.max(jnp.where(lo_half, lw, NEG_BIG), axis=-1, keepdims=True)
            mr1 = jnp.max(jnp.where(lo_half, NEG_BIG, lw), axis=-1, keepdims=True)
            m64 = jnp.where(lo_half, mx[:, 2 * p * LANES:(2 * p + 1) * LANES],
                            mx[:, (2 * p + 1) * LANES:(2 * p + 2) * LANES])
            m_row = jnp.maximum(jnp.where(lo_half, mr0, mr1), bx + m64)
            ss.append((d["qk_raw"][p] * jnp.exp(lw - m_row)).astype(BF16))
            mrs += [mr0, mr1]
        d["ss"], d["mrs"] = ss, mrs
        b_last = ex[CHUNK - 1:CHUNK, full0:diff0]
        lwe = ex[:, diff0:] + b_last
        m_new = jnp.maximum(b_last + mx, jnp.max(lwe, axis=0, keepdims=True))
        d["cd"] = jnp.exp(b_last + mx - m_new)
        d["m_new"] = m_new
        kw = jnp.exp(lwe - m_new)
        d["kws"] = [(d["ks"][h] * kw[:, h * LANES:(h + 1) * LANES]).astype(BF16) for h in heads]

    def numerators(d, rows):
        zero = jnp.zeros((CHUNK, 2 * LANES), BF16)
        d["nums"] = [_dot(d["ss"][p], _side_by_side(d["vaug"][2 * p], d["vaug"][2 * p + 1], zero))
                     for p in pairs]
        d["cups"] = [_dot_tn(d["kws"][h], d["vaug"][h]) for h in heads]

    def finish(d, rows):
        ex = d["ex"]
        mx = d["m"][0:1, :]
        for h in heads:
            p, hd = divmod(h, 2)
            hs = slice(h * LANES, (h + 1) * LANES)
            m_inter = ex[:, full0 + h * LANES:full0 + (h + 1) * LANES] + mx[:, hs]
            m_row = jnp.maximum(d["mrs"][h], m_inter)
            inter = jnp.exp(m_inter - m_row)
            sv = d["nums"][p][:, hd * 2 * LANES:(hd + 1) * 2 * LANES]
            num = sv[:, 0:LANES] + inter * d["qc"][h][:, 0:LANES]
            den = sv[:, LANES:] + inter * d["qc"][h][:, LANES:]
            hh = num / jnp.maximum(jnp.abs(den), jnp.exp(-m_row))
            cd = d["cd"][:, hs]
            d["c"][h] = jnp.concatenate([cd, cd], axis=1) * d["c"][h] + d["cups"][h]
            og = d["p"][rows, o0 + h * LANES:o0 + (h + 1) * LANES]
            _norm_gate_store(d["y"], rows, h * LANES, [og * hh], nw_ref,
                             [d["p"][rows, z0 + h * LANES:z0 + (h + 1) * LANES]])
        d["m"][0:1, :] = d["m_new"]

    return qk_products, (gate_cumsums, gate_sums), (spread_gates, weights), numerators, finish


def _ssd_constants():
    e = np.zeros((LANES, SSD_HEADS * SSD_HEAD_DIM), np.float32)
    sel = np.zeros((16, LANES), np.float32)
    for h in range(SSD_HEADS):
        e[SM_DT + h, h * SSD_HEAD_DIM:(h + 1) * SSD_HEAD_DIM] = 1.0
        sel[h // 2, SM_DT + h] = 1.0
    return (jnp.asarray(np.concatenate([e, e], axis=0), BF16),
            jnp.asarray(np.concatenate([sel, sel], axis=1), BF16))


def _ssd_stages(alog_ref, dx_ref, nw_ref, tri_ref, sel_ref, exp_ref):
    lane = _lane_iota()
    is_dt = (lane >= SM_DT) & (lane < SM_DT + SSD_HEADS)

    half = LANES // 2
    ri = lax.broadcasted_iota(jnp.int32, (CHUNK, LANES), 0)
    ci = lax.broadcasted_iota(jnp.int32, (CHUNK, LANES), 1)
    causal2 = (ci & (half - 1)) <= ri
    lo_half = lane < half
    even = (lane & 1) == 0
    a_lane = jnp.where(is_dt, -jnp.exp(alog_ref[...]), 0.0)
    b0, c0, z0 = D_BRANCH, D_BRANCH + SSD_BC, D_BRANCH + 2 * SSD_BC
    groups = range(SSD_GROUPS)
    group_w = D_BRANCH // SSD_GROUPS
    pairs = range(D_BRANCH // LANES)
    pairs_per_group = group_w // LANES

    def products(d, rows):
        xbc = d["p"]
        d["bgs"] = [xbc[rows, b0 + g * SSD_STATE:b0 + (g + 1) * SSD_STATE].astype(BF16) for g in groups]
        cgbs = [xbc[rows, c0 + g * SSD_STATE:c0 + (g + 1) * SSD_STATE].astype(BF16) for g in groups]
        d["cb2"] = [_dot_nt(cgbs[g], jnp.concatenate([d["bgs"][g], d["bgs"][g]], axis=0))
                    for g in groups]
        d["cst"] = [_dot(cgbs[g], d["st"][g].astype(BF16)) for g in groups]

    def decay_cumsums(ds, rows):
        dts = [d["dt"][rows, :] for d in ds]
        acs = _sel_dot2(tri_ref[...], jnp.concatenate([dt * a_lane for dt in dts], axis=1))
        for i, d in enumerate(ds):
            d["dt_c"], d["acs"] = dts[i], acs[:, i * LANES:(i + 1) * LANES]

    def decay_sums(d, rows):
        acs = d["acs"]
        d["dt_acs"] = jnp.concatenate([d["dt_c"], acs], axis=0)
        d["by_parity"] = jnp.concatenate([jnp.where(even, acs, 0.0), jnp.where(even, 0.0, acs)], axis=0)

    def spread_decays(ds, rows):
        ex = _dot_sel2(jnp.concatenate([d["dt_acs"] for d in ds], axis=0), exp_ref[...])
        a_rows = _sel_dot2_nt(sel_ref[...], jnp.concatenate([d["by_parity"] for d in ds], axis=0))
        for i, d in enumerate(ds):
            d["dtx"] = ex[2 * i * CHUNK:(2 * i + 1) * CHUNK]
            d["acs_x"] = ex[(2 * i + 1) * CHUNK:(2 * i + 2) * CHUNK]
            d["a_rows"] = a_rows[:, i * LANES:(i + 1) * LANES]

    def decays(d, rows):
        d["xss"], d["ms"], d["xblk"], xdecs = [], [], [], []
        for p in pairs:
            ls = slice(p * LANES, (p + 1) * LANES)
            ax = d["acs_x"][:, ls]
            lmat = jnp.exp(jnp.where(causal2, ax - d["a_rows"][p:p + 1, :], NEG_BIG))
            d["ms"].append((d["cb2"][p // pairs_per_group] * lmat).astype(BF16))
            xs = d["p"][rows, ls]
            xdt = xs * d["dtx"][:, ls]
            d["xss"].append(xs)
            d["xblk"].append(jnp.concatenate([jnp.where(lo_half, xdt, 0.0), jnp.where(lo_half, 0.0, xdt)],
                                             axis=0).astype(BF16))
            xdecs.append((xdt * jnp.exp(ax[CHUNK - 1:CHUNK, :] - ax)).astype(BF16))
        d["xdec"] = [jnp.concatenate(xdecs[g * pairs_per_group:(g + 1) * pairs_per_group], axis=1)
                     for g in groups]

    def chunk_products(d, rows):
        d["yds"] = [_dot(d["ms"][p], d["xblk"][p]) for p in pairs]
        d["ups"] = [_dot_tn(d["bgs"][g], d["xdec"][g]) for g in groups]

    def finish(d, rows):
        for g in groups:
            gs = slice(g * group_w, (g + 1) * group_w)
            eax = jnp.exp(d["acs_x"][:, gs])
            d["st"][g] = d["st"][g] * eax[CHUNK - 1:CHUNK, :] + d["ups"][g]
            ys = []
            for pp in range(pairs_per_group):
                p = g * pairs_per_group + pp
                ls = slice(p * LANES, (p + 1) * LANES)
                y = (d["cst"][g][:, pp * LANES:(pp + 1) * LANES] * eax[:, pp * LANES:(pp + 1) * LANES]
                     + dx_ref[:, ls] * d["xss"][p] + d["yds"][p])
                ys.append(y * d["p"][rows, z0 + p * LANES:z0 + (p + 1) * LANES])
            _norm_gate_store(d["y"], rows, g * group_w, ys, nw_ref, [None] * len(ys))

    return products, (decay_cumsums, decay_sums), (spread_decays, decays), chunk_products, finish


def _mixers_kernel(pa_ref, pb_ref, pc_ref, pd_ref, sm_ref, nwa_ref, nwb_ref, nwc_ref, nwd_ref,
                   ms_ref, mk_ref, tri_ref, mexp_ref, msel_ref, alog_ref, dx_ref, dsel_ref, dexp_ref,
                   ya_ref, yb_ref, yc_ref, yd_ref, sta_s, stc_s, cb_s, mb_s, std_s, *, tile):
    @pl.when(pl.program_id(0) == 0)
    def _():
        for r in (sta_s, stc_s, cb_s, mb_s, std_s):
            r[...] = jnp.zeros_like(r)

    gla = _gla_stages(ms_ref, mk_ref)
    mlstm = _mlstm_stages(nwb_ref, tri_ref, mexp_ref, msel_ref)
    ssd = _ssd_stages(alog_ref, dx_ref, nwd_ref, tri_ref, dsel_ref, dexp_ref)
    work = []
    for s in range(pa_ref.shape[0]):
        work.append((_gla_stream(pa_ref, s, GLA_QK, GLA_DK, ya_ref, sta_s, nwa_ref), gla))
        work.append((dict(p=pb_ref.at[s], g=sm_ref.at[s], c=cb_s.at[s], m=mb_s.at[s], y=yb_ref.at[s]),
                     mlstm))
        work.append((_gla_stream(pc_ref, s, HGRN_QF, HGRN_DK, yc_ref, stc_s, nwc_ref), gla))
        work.append((dict(p=pd_ref.at[s], dt=sm_ref.at[s], st=std_s.at[s], y=yd_ref.at[s]), ssd))
    _run_chunks(work, tile)


def _mixers(tok_inputs, const_inputs, n_batch, seq, tile):
    tok = lambda t: (0, t, 0)
    in_specs = [pl.BlockSpec((n_batch, tile, a.shape[2]), tok) for a in tok_inputs]
    for a in const_inputs:
        in_specs.append(pl.BlockSpec(a.shape, lambda t, nd=a.ndim: (0,) * nd))
    scratch = [pltpu.VMEM((n_batch, GLA_QK // LANES, GLA_DV, LANES), F32),
               pltpu.VMEM((n_batch, HGRN_QF // LANES, HGRN_DV, LANES), F32),
               pltpu.VMEM((n_batch, MLSTM_HEADS, MLSTM_DH, 2 * MLSTM_DH), F32),
               pltpu.VMEM((n_batch, 8, MLSTM_HEADS * LANES), F32),
               pltpu.VMEM((n_batch, SSD_GROUPS, SSD_STATE, D_BRANCH // SSD_GROUPS), F32)]
    return pl.pallas_call(
        functools.partial(_mixers_kernel, tile=tile), grid=(seq // tile,), in_specs=in_specs,
        out_specs=[pl.BlockSpec((n_batch, tile, D_BRANCH), tok)] * 4,
        out_shape=[jax.ShapeDtypeStruct((n_batch, seq, D_BRANCH), BF16)] * 4,
        scratch_shapes=scratch,
        compiler_params=pltpu.CompilerParams(dimension_semantics=("arbitrary",),
                                             vmem_limit_bytes=VMEM_LIMIT),
        name="mixers",
    )(*tok_inputs, *const_inputs)


def _pad_lanes(parts, total=LANES):
    width = sum(p.shape[-1] for p in parts)
    lead = parts[0].shape[:-1]
    return jnp.concatenate(list(parts) + [jnp.zeros(lead + (total - width,), parts[0].dtype)], axis=-1)


def _small_vector(i_part, f_part, dt_part):
    z = jnp.zeros((GLA_GATE_RANK,), F32)
    return _pad_lanes([z, i_part.astype(F32), f_part.astype(F32), dt_part.astype(F32)])[None, :]


def kernel(x, norm_w, w_in, gla_gate_w, gla_gate_b, gla_norm_w, ml_conv_w, ml_conv_b, ml_i_b, ml_f_b,
           ml_norm_w, hg_lb_logits, hg_norm_w, ssd_conv_w, ssd_conv_b, ssd_dt_bias, ssd_A_log, ssd_D,
           ssd_norm_w, w_out, final_norm_w):
    n_batch, seq, _ = x.shape
    depth = w_in.shape[0]
    tile = min(MIX_TILE, seq)
    n_tok = n_batch * seq

    mstack, masks = _gla_constants()
    tri_c = _tri2(CHUNK)
    ml_exp, ml_sel = _mlstm_constants()
    ssd_exp, ssd_sel = _ssd_constants()
    zero4 = jnp.zeros((MLSTM_HEADS,), F32)
    in_tile = min(IN_TILE, seq)

    p = jax.nn.softmax(hg_lb_logits.astype(F32), axis=0)
    lower_bounds = jnp.cumsum(p, axis=0) - p[0:1]

    w_wide, w_small = _realign(jnp.swapaxes(w_in, 1, 2), REALIGN_ROWS)

    h = x.reshape(n_tok, D_MODEL)
    row2 = lambda v: v.astype(F32).reshape(1, -1)
    for l in range(depth):
        gate_w = jnp.concatenate(
            [gla_gate_w[l].astype(F32), jnp.zeros((LANES - GLA_GATE_RANK, GLA_QK), F32)], axis=0)
        consts = [gate_w, row2(gla_gate_b[l]), row2(lower_bounds[l]),
                  ml_conv_w[l].astype(F32), row2(ml_conv_b[l]),
                  ssd_conv_w[l].astype(F32), row2(ssd_conv_b[l]),
                  _small_vector(ml_i_b[l], ml_f_b[l], ssd_dt_bias[l])]
        pa, pb, pc, pd, sm = [p.reshape(n_batch, seq, -1)
                              for p in _inproj(h, row2(norm_w[l]), w_wide, w_small, l, consts, in_tile,
                                               seq // in_tile)]

        ys = _mixers(
            [pa, pb, pc, pd, sm],
            [row2(gla_norm_w[l]), row2(ml_norm_w[l]), row2(hg_norm_w[l]), row2(ssd_norm_w[l]),
             mstack, masks, tri_c, ml_exp, ml_sel, _small_vector(zero4, zero4, ssd_A_log[l]),
             row2(jnp.repeat(ssd_D[l].astype(F32), SSD_HEAD_DIM)), ssd_sel, ssd_exp],
            n_batch, seq, tile)
        ys = [y.reshape(n_tok, D_BRANCH) for y in ys]
        h = _outproj(h, ys, w_out[l].astype(BF16), row2(final_norm_w), l == depth - 1,
                     min(OUT_TILE, n_tok))
    return h.reshape(n_batch, seq, D_MODEL)
```

```python
import functools

import numpy as np
import jax
import jax.numpy as jnp
from jax import lax
from jax.experimental import pallas as pl
from jax.experimental.pallas import tpu as pltpu

F32 = jnp.float32
BF16 = jnp.bfloat16

D_MODEL = 1024
D_BRANCH = 512
EPS = 1e-6
NEG_BIG = -1e30

GLA_HEADS, GLA_DK, GLA_DV = 4, 64, 128
GLA_GATE_RANK, GLA_GATE_NORM = 16, 16.0
MLSTM_HEADS, MLSTM_DH, MLSTM_CONV = 4, 128, 4
HGRN_HEADS, HGRN_DK, HGRN_DV = 4, 128, 128
SSD_HEAD_DIM, SSD_HEADS, SSD_GROUPS, SSD_STATE, SSD_CONV = 64, 8, 2, 128, 4
GLA_QK = GLA_HEADS * GLA_DK
HGRN_QF = HGRN_HEADS * HGRN_DK
SSD_BC = SSD_GROUPS * SSD_STATE
PROJ_SIZES = (
    GLA_QK, GLA_QK, D_BRANCH, GLA_GATE_RANK, D_BRANCH,
    D_BRANCH, D_BRANCH, D_BRANCH, MLSTM_HEADS, MLSTM_HEADS, D_BRANCH, D_BRANCH,
    HGRN_QF, HGRN_QF, D_BRANCH, D_BRANCH,
    D_BRANCH, SSD_BC, SSD_BC, SSD_HEADS, D_BRANCH,
)

LANES = 128
SUBLANES = 8
BF16_ROWS = 16
HIST = SUBLANES
VMEM_LIMIT = 56 * 1024 * 1024

REALIGN_ROWS = 256
IN_TILE = 256
MIX_TILE = 256
OUT_TILE = 512

CHUNK = 64
N_LEVELS = 6
MIN_VPU_LEVEL = 4

SM_GR, SM_I, SM_F, SM_DT = 0, 16, 20, 24

W_A = 2 * GLA_QK + 2 * D_BRANCH
W_B = 5 * D_BRANCH
W_C = 2 * HGRN_QF + 2 * D_BRANCH
W_D = 2 * D_BRANCH + 2 * SSD_BC
COL_A = 0
COL_B = COL_A + W_A
COL_C = COL_B + W_B
COL_D = COL_C + W_C
COL_S = COL_D + W_D


def _gla_constants():
    c = CHUNK
    t = np.arange(c)[:, None]
    d = np.arange(c)[None, :]
    blocks = [(d <= t), (d > t)]
    masks = [np.eye(c, dtype=bool)]
    for l in range(N_LEVELS):
        s = c >> (l + 1)
        mid_t = (t // (2 * s)) * (2 * s) + s
        upper = t >= mid_t
        if 1 < s < MIN_VPU_LEVEL:
            blocks.append(np.where(upper, (d >= mid_t) & (d <= t), (d > t) & (d <= mid_t - 1)))
        same = (t // (2 * s)) == (d // (2 * s))
        masks.append(same & upper & (d < mid_t))
    mstack = np.concatenate(blocks, axis=0).astype(np.float32)
    mstack = np.concatenate([mstack, mstack], axis=1)
    masks = np.stack(masks).astype(np.float32)
    return jnp.asarray(mstack, BF16), jnp.asarray(np.concatenate([masks, masks], axis=-1))


def _tri2(n):
    tri = np.tril(np.ones((n, n), np.float32))
    return jnp.asarray(np.concatenate([tri, tri], axis=1), BF16)


def _dot(a, b):
    return jnp.dot(a, b, preferred_element_type=F32)


def _dot_nt(a, b):
    return lax.dot_general(a, b, (((1,), (1,)), ((), ())), preferred_element_type=F32)


def _dot_tn(a, b):
    return lax.dot_general(a, b, (((0,), (0,)), ((), ())), preferred_element_type=F32)


def _sel_dot2(sel2, x):
    hi = x.astype(BF16)
    mid = (x - hi.astype(F32)).astype(BF16)
    return _dot(sel2, jnp.concatenate([hi, mid], axis=0))


def _sel_dot2_nt(sel2, x):
    hi = x.astype(BF16)
    mid = (x - hi.astype(F32)).astype(BF16)
    return _dot_nt(sel2, jnp.concatenate([hi, mid], axis=1))


def _level_exponent(gcs, s):
    pieces = []
    for b in range(0, CHUNK, 2 * s):
        ref_row = gcs[b + s - 1:b + s, :]
        if s >= 8:
            pieces += [ref_row - gcs[b:b + s], gcs[b + s:b + 2 * s] - ref_row]
        else:
            diff = gcs[b:b + 2 * s] - ref_row
            pieces.append(jnp.minimum(diff, -diff))
    return jnp.concatenate(pieces, axis=0)


def _dot_sel2(x, sel2):
    hi = x.astype(BF16)
    mid = (x - hi.astype(F32)).astype(BF16)
    return _dot(jnp.concatenate([hi, mid], axis=1), sel2)


def _dot_f32(a, b):
    ah = a.astype(BF16)
    al = (a - ah.astype(F32)).astype(BF16)
    bh = b.astype(BF16)
    bl = (b - bh.astype(F32)).astype(BF16)
    return _dot(ah, bh) + _dot(al, bh) + _dot(ah, bl)


def _softplus(x):
    return jnp.maximum(x, 0.0) + jnp.log1p(jnp.exp(-jnp.abs(x)))


def _log_sigmoid(x):
    return -_softplus(-x)


def _silu(x):
    return x * jax.nn.sigmoid(x)


def _lane_iota():
    return lax.broadcasted_iota(jnp.int32, (1, LANES), 1)


def _side_by_side(a, b, zero):
    return jnp.concatenate([jnp.concatenate([a, zero], axis=1),
                            jnp.concatenate([zero, b], axis=1)], axis=0)


def _wide_segments():
    segs, src, dst = [], 0, 0
    run_start = None
    for size in PROJ_SIZES + (0,):
        wide = size >= LANES
        if wide and run_start is None:
            run_start = src
        if not wide and run_start is not None:
            segs.append((run_start, dst, src - run_start))
            dst += src - run_start
            run_start = None
        src += size
    return segs


def _realign_kernel(wt_ref, o_ref, os_ref):
    for src, dst, n in _wide_segments():
        for r in range(0, n, LANES):
            o_ref[:, dst + r:dst + r + LANES] = wt_ref[src + r:src + r + LANES, :].T.astype(BF16)
    narrow, src = [], 0
    for size in PROJ_SIZES:
        if size < LANES:
            narrow.append(wt_ref[src:src + size, :])
        src += size
    used = sum(p.shape[0] for p in narrow)
    narrow.append(jnp.zeros((LANES - used, wt_ref.shape[1]), F32))
    os_ref[...] = jnp.concatenate(narrow, axis=0).T


def _realign(wt, rows):
    depth, n_proj, d_model = wt.shape
    return pl.pallas_call(
        _realign_kernel, grid=(depth, d_model // rows),
        in_specs=[pl.BlockSpec((None, n_proj, rows), lambda l, j: (l, 0, j))],
        out_specs=[pl.BlockSpec((None, rows, COL_S), lambda l, j: (l, j, 0)),
                   pl.BlockSpec((None, rows, LANES), lambda l, j: (l, j, 0))],
        out_shape=[jax.ShapeDtypeStruct((depth, d_model, COL_S), BF16),
                   jax.ShapeDtypeStruct((depth, d_model, LANES), F32)],
        compiler_params=pltpu.CompilerParams(dimension_semantics=("arbitrary", "arbitrary"),
                                             vmem_limit_bytes=VMEM_LIMIT),
        name="realign",
    )(wt)


CONV_ROWS, CONV_LANES = 256, 128


def _causal_conv_silu(raw, xe_ref, hist_ref, w_ref, b_ref, taps, out_ref, scale=None):
    rows, n = raw.shape
    xe_ref[0:HIST, :] = hist_ref[...]
    xe_ref[HIST:HIST + rows, :] = raw
    hist_ref[...] = raw[rows - HIST:rows, :]
    for r in range(0, rows, CONV_ROWS):
        for l in range(0, n, CONV_LANES):
            cs = slice(l, l + CONV_LANES)
            acc = b_ref[:, cs] + w_ref[taps - 1:taps, cs] * xe_ref[HIST + r:HIST + r + CONV_ROWS, cs]
            for k in range(taps - 1):
                off = HIST - (taps - 1) + k + r
                acc = acc + w_ref[k:k + 1, cs] * xe_ref[off:off + CONV_ROWS, cs]
            out = _silu(acc)
            out_ref[r:r + CONV_ROWS, cs] = out if scale is None else out * scale[:, cs]


def _inproj_kernel(x_ref, xn_ref, nw_ref, w_ref, ws_ref, gw_ref, gb_ref, lb_ref, cwb_ref, cbb_ref, cwd_ref,
                   cbd_ref, gbias_ref, oa_ref, ob_ref, oc_ref, od_ref, os_ref, xe_s, hist_b, hist_d, u_s,
                   *, tiles_per_seq):
    step = pl.program_id(0)

    @pl.when(step % tiles_per_seq == 0)
    def _():
        hist_b[...] = jnp.zeros_like(hist_b)
        hist_d[...] = jnp.zeros_like(hist_d)

    def normed(x):
        ms = jnp.mean(x * x, axis=-1, keepdims=True)
        return (x * lax.rsqrt(ms + EPS) * nw_ref[...]).astype(BF16)

    @pl.when(step == 0)
    def _():
        u_s[0] = normed(x_ref[...])

    u = u_s[step % 2]
    lane = _lane_iota()

    small = _dot(u, ws_ref[...].astype(BF16))
    biased = small + gbias_ref[...]
    is_f = (lane >= SM_F) & (lane < SM_F + MLSTM_HEADS)
    is_dt = (lane >= SM_DT) & (lane < SM_DT + SSD_HEADS)
    os_ref[...] = jnp.where(is_f, _log_sigmoid(biased), jnp.where(is_dt, _softplus(biased), biased))

    raw = _dot(u, w_ref[:, COL_B:COL_B + W_B])
    conv_w = 2 * D_BRANCH
    k_scale = jnp.where(lax.broadcasted_iota(jnp.int32, (1, conv_w), 1) < D_BRANCH, 1.0, MLSTM_DH ** -0.5)
    _causal_conv_silu(raw[:, 0:conv_w], xe_s, hist_b, cwb_ref, cbb_ref, MLSTM_CONV, ob_ref, k_scale)
    o0, z0 = 3 * D_BRANCH, 4 * D_BRANCH
    ob_ref[:, conv_w:o0] = raw[:, conv_w:o0]
    ob_ref[:, o0:z0] = jax.nn.sigmoid(raw[:, o0:z0])
    ob_ref[:, z0:W_B] = _silu(raw[:, z0:W_B])

    raw = _dot(u, w_ref[:, COL_D:COL_D + W_D])
    conv_w = D_BRANCH + 2 * SSD_BC
    _causal_conv_silu(raw[:, 0:conv_w], xe_s, hist_d, cwd_ref, cbd_ref, SSD_CONV, od_ref)
    od_ref[:, conv_w:W_D] = _silu(raw[:, conv_w:W_D])
    u_s[(step + 1) % 2] = normed(xn_ref[...])

    raw = _dot(u, w_ref[:, COL_C:COL_C + W_C])
    lb = lb_ref[...]
    fr = raw[:, HGRN_QF:2 * HGRN_QF]
    oc_ref[:, 0:HGRN_QF] = raw[:, 0:HGRN_QF] * (HGRN_DK ** -0.5)
    oc_ref[:, HGRN_QF:2 * HGRN_QF] = (1.0 - lb) * jax.nn.sigmoid(-fr)
    z0 = 2 * HGRN_QF + D_BRANCH
    oc_ref[:, 2 * HGRN_QF:z0] = raw[:, 2 * HGRN_QF:z0]
    oc_ref[:, z0:W_C] = _silu(raw[:, z0:W_C])
    oc_ref[:, W_C:W_C + HGRN_QF] = jnp.log(jnp.maximum(lb + (1.0 - lb) * jax.nn.sigmoid(fr), 1e-30))

    gate = _dot_f32(small, gw_ref[...]) + gb_ref[...]
    oa_ref[:, W_A:W_A + GLA_QK] = _log_sigmoid(gate) * (1.0 / GLA_GATE_NORM)
    raw = _dot(u, w_ref[:, COL_A:COL_A + W_A])
    oa_ref[:, 0:GLA_QK] = raw[:, 0:GLA_QK] * (GLA_DK ** -0.5)
    z0 = 2 * GLA_QK + D_BRANCH
    oa_ref[:, GLA_QK:z0] = raw[:, GLA_QK:z0]
    oa_ref[:, z0:W_A] = _silu(raw[:, z0:W_A])


def _inproj(h, norm_w, w_wide, w_small, layer, consts, tile, tiles_per_seq):
    n_tok = h.shape[0]
    const = lambda i: (0, 0)
    row = lambda i: (i, 0)
    n_tiles = n_tok // tile
    in_specs = [pl.BlockSpec((tile, D_MODEL), row),
                pl.BlockSpec((tile, D_MODEL), lambda i: (jnp.minimum(i + 1, n_tiles - 1), 0)),
                pl.BlockSpec((1, D_MODEL), const),
                pl.BlockSpec((None, D_MODEL, COL_S), lambda i: (layer, 0, 0), pipeline_mode=pl.Buffered(1)),
                pl.BlockSpec((None, D_MODEL, LANES), lambda i: (layer, 0, 0))]
    in_specs += [pl.BlockSpec(c.shape, const) for c in consts]
    widths = (W_A + GLA_QK, W_B, W_C + HGRN_QF, W_D, LANES)
    conv_w = 2 * D_BRANCH
    return pl.pallas_call(
        functools.partial(_inproj_kernel, tiles_per_seq=tiles_per_seq),
        grid=(n_tiles,), in_specs=in_specs,
        out_specs=[pl.BlockSpec((tile, w), row) for w in widths],
        out_shape=[jax.ShapeDtypeStruct((n_tok, w), F32) for w in widths],
        scratch_shapes=[pltpu.VMEM((tile + HIST, conv_w), F32), pltpu.VMEM((HIST, conv_w), F32),
                        pltpu.VMEM((HIST, conv_w), F32), pltpu.VMEM((2, tile, D_MODEL), BF16)],
        compiler_params=pltpu.CompilerParams(dimension_semantics=("arbitrary",),
                                             vmem_limit_bytes=VMEM_LIMIT),
        name="inproj",
    )(h, h, norm_w, w_wide, w_small, *consts)


def _outproj_kernel(h_ref, ya_ref, yb_ref, yc_ref, yd_ref, w_ref, fw_ref, o_ref, *, final):
    acc = h_ref[...]
    for i, y_ref in enumerate((ya_ref, yb_ref, yc_ref, yd_ref)):
        acc = acc + _dot(y_ref[...], w_ref[i * D_BRANCH:(i + 1) * D_BRANCH, :])
    if final:
        ms = jnp.mean(acc * acc, axis=-1, keepdims=True)
        acc = acc * lax.rsqrt(ms + EPS) * fw_ref[...]
    o_ref[...] = acc


def _outproj(h, ys, w_out, final_w, final, tile):
    n_tok = h.shape[0]
    const = lambda i: (0, 0)
    row = lambda i: (i, 0)
    in_specs = [pl.BlockSpec((tile, D_MODEL), row)]
    in_specs += [pl.BlockSpec((tile, D_BRANCH), row) for _ in ys]
    in_specs += [pl.BlockSpec(w_out.shape, const), pl.BlockSpec((1, D_MODEL), const)]
    return pl.pallas_call(
        functools.partial(_outproj_kernel, final=final),
        grid=(n_tok // tile,), in_specs=in_specs,
        out_specs=pl.BlockSpec((tile, D_MODEL), row),
        out_shape=jax.ShapeDtypeStruct((n_tok, D_MODEL), F32),
        compiler_params=pltpu.CompilerParams(dimension_semantics=("arbitrary",),
                                             vmem_limit_bytes=VMEM_LIMIT),
        name="outproj",
    )(h, *ys, w_out, final_w)


def _norm_gate_store(y_ref, rows, col0, parts, nw_ref, z_parts):
    width = sum(p.shape[-1] for p in parts)
    ss = sum(jnp.sum(p * p, axis=-1, keepdims=True) for p in parts)
    scale = lax.rsqrt(ss * (1.0 / width) + EPS)
    c = col0
    for p, z in zip(parts, z_parts):
        w = p.shape[-1]
        out = p * scale * nw_ref[:, c:c + w]
        if z is not None:
            out = out * z
        y_ref[rows, c:c + w] = out.astype(y_ref.dtype)
        c += w


N_STAGES = 5
CHUNK_UNROLL = 1


def _run_chunks(work, tile):
    def body(c, carry):
        rows = pl.ds(pl.multiple_of(c * CHUNK, CHUNK), CHUNK)
        live = [(dict(d), stages) for d, stages in work]
        for k in range(N_STAGES):
            joined = []
            for d, stages in live:
                joint, fn = stages[k] if isinstance(stages[k], tuple) else (None, stages[k])
                if joint is not None and not any(stages is s for s in joined):
                    joint([dd for dd, st in live if st is stages], rows)
                    joined.append(stages)
                fn(d, rows)
        return carry

    lax.fori_loop(0, tile // CHUNK, body, 0, unroll=CHUNK_UNROLL)


def _gla_stages(ms_ref, mk_ref):
    lane = _lane_iota()

    def lane_mask(d, j):
        dk = d["dk"]
        return None if dk == LANES else (lane >= j * dk) & (lane < (j + 1) * dk)

    def pick(a, lm):
        return a if lm is None else jnp.where(lm, a, jnp.zeros_like(a))

    def n_groups(d):
        return d["q"].shape[1] // LANES

    def cumsums(d, rows):
        d["es"] = [_sel_dot2(ms_ref[...], d["lg"][rows, p * 2 * LANES:(p + 1) * 2 * LANES])
                   for p in range(n_groups(d) // 2)]

    def operands(d, rows):
        ops = []
        for g in range(n_groups(d)):
            ls = slice(g * LANES, (g + 1) * LANES)
            e = d["es"][g // 2][:, (g % 2) * LANES:(g % 2 + 1) * LANES]
            q = d["q"][rows, ls]
            k = d["k"][rows, ls]
            gcs = e[0:CHUNK]
            qg = (q * jnp.exp(gcs)).astype(BF16)
            kd = (k * jnp.exp(e[CHUNK:2 * CHUNK])).astype(BF16)
            dec = jnp.exp(gcs[CHUNK - 1:CHUNK, :])
            qb = q.astype(BF16)
            kb = k.astype(BF16)
            ql, kl = [q * k], [None]
            mxu_block = 2
            for l in range(N_LEVELS):
                s = CHUNK >> (l + 1)
                if s == 1:
                    ql.append(q * jnp.exp(d["lg"][rows, ls]) * pltpu.roll(k, 1, axis=0))
                    kl.append(None)
                    continue
                if s >= MIN_VPU_LEVEL:
                    el = _level_exponent(gcs, s)
                else:
                    el = e[mxu_block * CHUNK:(mxu_block + 1) * CHUNK]
                    mxu_block += 1
                w = jnp.exp(el).astype(BF16)
                ql.append(qb * w)
                kl.append(kb * w)
            ops.append((qg, kd, dec, ql, kl))
        d["ops"] = ops

    def stacked(d, a):
        hp = LANES // d["dk"]
        return a if hp == 1 else jnp.concatenate([pick(a, lane_mask(d, j)) for j in range(hp)], axis=0)

    def scores(d, rows):
        hp = LANES // d["dk"]
        mask_rows = lambda l: jnp.concatenate([mk_ref[l][:, 0:CHUNK]] * hp, axis=0)
        groups = []
        for g in range(n_groups(d)):
            ql, kl = d["ops"][g][3], d["ops"][g][4]
            a = None
            for l in range(N_LEVELS + 1):
                if kl[l] is None:
                    s = jnp.sum(stacked(d, ql[l]), axis=-1, keepdims=True)
                else:
                    s = _dot_nt(stacked(d, ql[l]), kl[l])
                s = s * mask_rows(l)
                a = s if a is None else a + s
            groups.append(a.astype(BF16))
        d["scores"] = groups

    def outputs(d, rows):
        hp = LANES // d["dk"]
        outs, upds = [], [None] * n_groups(d)
        sts = [d["st"][g] for g in range(n_groups(d))]
        for g in range(n_groups(d)):
            qg, kd = d["ops"][g][0], d["ops"][g][1]
            inter = _dot_nt(stacked(d, qg), sts[g].astype(BF16))
            for j in range(hp):
                h = g * hp + j
                hr = slice(j * CHUNK, (j + 1) * CHUNK)
                vh = d["v"][rows, h * LANES:(h + 1) * LANES].astype(BF16)
                outs.append(_dot(d["scores"][g][hr], vh) + inter[hr])
                u = _dot_tn(vh, kd)
                upds[g] = u if upds[g] is None else jnp.where(lane_mask(d, j), u, upds[g])
        d["outs"], d["upds"], d["sts"] = outs, upds, sts

    def finish(d, rows):
        for g in range(n_groups(d)):
            d["st"][g] = d["sts"][g] * d["ops"][g][2] + d["upds"][g]
        for h, o in enumerate(d["outs"]):
            _norm_gate_store(d["y"], rows, h * LANES, [o], d["nw"],
                             [d["z"][rows, h * LANES:(h + 1) * LANES]])

    return cumsums, operands, scores, outputs, finish


def _gla_stream(p_ref, s, qk, dk, y_ref, st_s, nw_ref):
    v0, z0, lg0 = 2 * qk, 2 * qk + D_BRANCH, 2 * qk + 2 * D_BRANCH
    return dict(q=p_ref.at[s, :, 0:qk], k=p_ref.at[s, :, qk:v0], v=p_ref.at[s, :, v0:z0],
                z=p_ref.at[s, :, z0:lg0], lg=p_ref.at[s, :, lg0:lg0 + qk],
                y=y_ref.at[s], st=st_s.at[s], nw=nw_ref, dk=dk)


def _mlstm_constants():
    half = LANES // 2
    full0 = half * MLSTM_HEADS
    e = np.zeros((LANES, full0 + 2 * LANES * MLSTM_HEADS), np.float32)
    diff0 = full0 + LANES * MLSTM_HEADS
    sel = np.zeros((BF16_ROWS, LANES), np.float32)
    for h in range(MLSTM_HEADS):
        e[SM_F + h, h * half:(h + 1) * half] = 1.0
        e[SM_F + h, full0 + h * LANES:full0 + (h + 1) * LANES] = 1.0
        e[SM_I + h, diff0 + h * LANES:diff0 + (h + 1) * LANES] = 1.0
        e[SM_F + h, diff0 + h * LANES:diff0 + (h + 1) * LANES] = -1.0
        sel[h // 2, SM_I + h] = 1.0
        sel[h // 2, SM_F + h] = -1.0
    return (jnp.asarray(np.concatenate([e, e], axis=0), BF16),
            jnp.asarray(np.concatenate([sel, sel], axis=1), BF16))


def _mlstm_stages(nw_ref, tri_ref, exp_ref, sel_ref):
    lane = _lane_iota()
    is_f = (lane >= SM_F) & (lane < SM_F + MLSTM_HEADS)

    half = LANES // 2
    ri = lax.broadcasted_iota(jnp.int32, (CHUNK, LANES), 0)
    ci = lax.broadcasted_iota(jnp.int32, (CHUNK, LANES), 1)
    causal2 = (ci & (half - 1)) <= ri
    lo_half = lane < half
    even = (lane & 1) == 0
    v0, o0, z0 = 2 * D_BRANCH, 3 * D_BRANCH, 4 * D_BRANCH
    heads = range(MLSTM_HEADS)
    pairs = range(MLSTM_HEADS // 2)
    full0 = half * MLSTM_HEADS
    diff0 = full0 + LANES * MLSTM_HEADS

    def qk_products(d, rows):
        qk_ref = d["p"]
        qbs = [qk_ref[rows, h * LANES:(h + 1) * LANES].astype(BF16) for h in heads]
        d["ks"] = [qk_ref[rows, D_BRANCH + h * LANES:D_BRANCH + (h + 1) * LANES] for h in heads]
        kbs = [k.astype(BF16) for k in d["ks"]]
        zero = jnp.zeros((CHUNK, LANES), BF16)
        ones = jnp.ones((CHUNK, LANES), BF16)
        d["qk_raw"] = [_dot_nt(jnp.concatenate([qbs[2 * p], qbs[2 * p + 1]], axis=1),
                               _side_by_side(kbs[2 * p], kbs[2 * p + 1], zero)) for p in pairs]
        d["qc"] = [_dot(qbs[h], d["c"][h].astype(BF16)) for h in heads]
        d["vaug"] = [jnp.concatenate([d["p"][rows, v0 + h * LANES:v0 + (h + 1) * LANES].astype(BF16),
                                      ones], axis=1) for h in heads]

    def gate_cumsums(ds, rows):
        gcs = [d["g"][rows, :] for d in ds]
        bcols = _sel_dot2(tri_ref[...], jnp.concatenate(gcs, axis=1))
        for i, d in enumerate(ds):
            d["ib"] = jnp.where(is_f, bcols[:, i * LANES:(i + 1) * LANES], gcs[i])

    def gate_sums(d, rows):
        ib = d["ib"]
        d["by_parity"] = jnp.concatenate([jnp.where(even, ib, 0.0), jnp.where(even, 0.0, ib)], axis=0)

    def spread_gates(ds, rows):
        ex = _dot_sel2(jnp.concatenate([d["ib"] for d in ds], axis=0), exp_ref[...])
        drows = _sel_dot2_nt(sel_ref[...], jnp.concatenate([d["by_parity"] for d in ds], axis=0))
        for i, d in enumerate(ds):
            d["ex"] = ex[i * CHUNK:(i + 1) * CHUNK]
            d["drows"] = drows[:, i * LANES:(i + 1) * LANES]

    def weights(d, rows):
        ex = d["ex"]
        mx = d["m"][0:1, :]
        ss, mrs = [], []
        for p in pairs:
            bx = ex[:, p * LANES:(p + 1) * LANES]
            lw = jnp.where(causal2, bx + d["drows"][p:p + 1, :], NEG_BIG)
            mr0 = jnp.max(jnp.where(lo_half, lw, NEG_BIG), axis=-1, keepdims=True)
            mr1 = jnp.max(jnp.where(lo_half, NEG_BIG, lw), axis=-1, keepdims=True)
            m64 = jnp.where(lo_half, mx[:, 2 * p * LANES:(2 * p + 1) * LANES],
                            mx[:, (2 * p + 1) * LANES:(2 * p + 2) * LANES])
            m_row = jnp.maximum(jnp.where(lo_half, mr0, mr1), bx + m64)
            ss.append((d["qk_raw"][p] * jnp.exp(lw - m_row)).astype(BF16))
            mrs += [mr0, mr1]
        d["ss"], d["mrs"] = ss, mrs
        b_last = ex[CHUNK - 1:CHUNK, full0:diff0]
        lwe = ex[:, diff0:] + b_last
        m_new = jnp.maximum(b_last + mx, jnp.max(lwe, axis=0, keepdims=True))
        d["cd"] = jnp.exp(b_last + mx - m_new)
        d["m_new"] = m_new
        kw = jnp.exp(lwe - m_new)
        d["kws"] = [(d["ks"][h] * kw[:, h * LANES:(h + 1) * LANES]).astype(BF16) for h in heads]

    def numerators(d, rows):
        zero = jnp.zeros((CHUNK, 2 * LANES), BF16)
        d["nums"] = [_dot(d["ss"][p], _side_by_side(d["vaug"][2 * p], d["vaug"][2 * p + 1], zero))
                     for p in pairs]
        d["cups"] = [_dot_tn(d["kws"][h], d["vaug"][h]) for h in heads]

    def finish(d, rows):
        ex = d["ex"]
        mx = d["m"][0:1, :]
        for h in heads:
            p, hd = divmod(h, 2)
            hs = slice(h * LANES, (h + 1) * LANES)
            m_inter = ex[:, full0 + h * LANES:full0 + (h + 1) * LANES] + mx[:, hs]
            m_row = jnp.maximum(d["mrs"][h], m_inter)
            inter = jnp.exp(m_inter - m_row)
            sv = d["nums"][p][:, hd * 2 * LANES:(hd + 1) * 2 * LANES]
            num = sv[:, 0:LANES] + inter * d["qc"][h][:, 0:LANES]
            den = sv[:, LANES:] + inter * d["qc"][h][:, LANES:]
            hh = num / jnp.maximum(jnp.abs(den), jnp.exp(-m_row))
            cd = d["cd"][:, hs]
            d["c"][h] = jnp.concatenate([cd, cd], axis=1) * d["c"][h] + d["cups"][h]
            og = d["p"][rows, o0 + h * LANES:o0 + (h + 1) * LANES]
            _norm_gate_store(d["y"], rows, h * LANES, [og * hh], nw_ref,
                             [d["p"][rows, z0 + h * LANES:z0 + (h + 1) * LANES]])
        d["m"][0:1, :] = d["m_new"]

    return qk_products, (gate_cumsums, gate_sums), (spread_gates, weights), numerators, finish


def _ssd_constants():
    e = np.zeros((LANES, SSD_HEADS * SSD_HEAD_DIM), np.float32)
    sel = np.zeros((BF16_ROWS, LANES), np.float32)
    for h in range(SSD_HEADS):
        e[SM_DT + h, h * SSD_HEAD_DIM:(h + 1) * SSD_HEAD_DIM] = 1.0
        sel[h // 2, SM_DT + h] = 1.0
    return (jnp.asarray(np.concatenate([e, e], axis=0), BF16),
            jnp.asarray(np.concatenate([sel, sel], axis=1), BF16))


def _ssd_stages(alog_ref, dx_ref, nw_ref, tri_ref, sel_ref, exp_ref):
    lane = _lane_iota()
    is_dt = (lane >= SM_DT) & (lane < SM_DT + SSD_HEADS)

    half = LANES // 2
    ri = lax.broadcasted_iota(jnp.int32, (CHUNK, LANES), 0)
    ci = lax.broadcasted_iota(jnp.int32, (CHUNK, LANES), 1)
    causal2 = (ci & (half - 1)) <= ri
    lo_half = lane < half
    even = (lane & 1) == 0
    a_lane = jnp.where(is_dt, -jnp.exp(alog_ref[...]), 0.0)
    b0, c0, z0 = D_BRANCH, D_BRANCH + SSD_BC, D_BRANCH + 2 * SSD_BC
    groups = range(SSD_GROUPS)
    group_w = D_BRANCH // SSD_GROUPS
    pairs = range(D_BRANCH // LANES)
    pairs_per_group = group_w // LANES

    def products(d, rows):
        xbc = d["p"]
        d["bgs"] = [xbc[rows, b0 + g * SSD_STATE:b0 + (g + 1) * SSD_STATE].astype(BF16) for g in groups]
        cgbs = [xbc[rows, c0 + g * SSD_STATE:c0 + (g + 1) * SSD_STATE].astype(BF16) for g in groups]
        d["cb2"] = [_dot_nt(cgbs[g], jnp.concatenate([d["bgs"][g], d["bgs"][g]], axis=0))
                    for g in groups]
        d["cst"] = [_dot(cgbs[g], d["st"][g].astype(BF16)) for g in groups]

    def decay_cumsums(ds, rows):
        dts = [d["dt"][rows, :] for d in ds]
        acs = _sel_dot2(tri_ref[...], jnp.concatenate([dt * a_lane for dt in dts], axis=1))
        for i, d in enumerate(ds):
            d["dt_c"], d["acs"] = dts[i], acs[:, i * LANES:(i + 1) * LANES]

    def decay_sums(d, rows):
        acs = d["acs"]
        d["dt_acs"] = jnp.concatenate([d["dt_c"], acs], axis=0)
        d["by_parity"] = jnp.concatenate([jnp.where(even, acs, 0.0), jnp.where(even, 0.0, acs)], axis=0)

    def spread_decays(ds, rows):
        ex = _dot_sel2(jnp.concatenate([d["dt_acs"] for d in ds], axis=0), exp_ref[...])
        a_rows = _sel_dot2_nt(sel_ref[...], jnp.concatenate([d["by_parity"] for d in ds], axis=0))
        for i, d in enumerate(ds):
            d["dtx"] = ex[2 * i * CHUNK:(2 * i + 1) * CHUNK]
            d["acs_x"] = ex[(2 * i + 1) * CHUNK:(2 * i + 2) * CHUNK]
            d["a_rows"] = a_rows[:, i * LANES:(i + 1) * LANES]

    def decays(d, rows):
        d["xss"], d["ms"], d["xblk"], xdecs = [], [], [], []
        for p in pairs:
            ls = slice(p * LANES, (p + 1) * LANES)
            ax = d["acs_x"][:, ls]
            lmat = jnp.exp(jnp.where(causal2, ax - d["a_rows"][p:p + 1, :], NEG_BIG))
            d["ms"].append((d["cb2"][p // pairs_per_group] * lmat).astype(BF16))
            xs = d["p"][rows, ls]
            xdt = xs * d["dtx"][:, ls]
            d["xss"].append(xs)
            d["xblk"].append(jnp.concatenate([jnp.where(lo_half, xdt, 0.0), jnp.where(lo_half, 0.0, xdt)],
                                             axis=0).astype(BF16))
            xdecs.append((xdt * jnp.exp(ax[CHUNK - 1:CHUNK, :] - ax)).astype(BF16))
        d["xdec"] = [jnp.concatenate(xdecs[g * pairs_per_group:(g + 1) * pairs_per_group], axis=1)
                     for g in groups]

    def chunk_products(d, rows):
        d["yds"] = [_dot(d["ms"][p], d["xblk"][p]) for p in pairs]
        d["ups"] = [_dot_tn(d["bgs"][g], d["xdec"][g]) for g in groups]

    def finish(d, rows):
        for g in groups:
            gs = slice(g * group_w, (g + 1) * group_w)
            eax = jnp.exp(d["acs_x"][:, gs])
            d["st"][g] = d["st"][g] * eax[CHUNK - 1:CHUNK, :] + d["ups"][g]
            ys = []
            for pp in range(pairs_per_group):
                p = g * pairs_per_group + pp
                ls = slice(p * LANES, (p + 1) * LANES)
                y = (d["cst"][g][:, pp * LANES:(pp + 1) * LANES] * eax[:, pp * LANES:(pp + 1) * LANES]
                     + dx_ref[:, ls] * d["xss"][p] + d["yds"][p])
                ys.append(y * d["p"][rows, z0 + p * LANES:z0 + (p + 1) * LANES])
            _norm_gate_store(d["y"], rows, g * group_w, ys, nw_ref, [None] * len(ys))

    return products, (decay_cumsums, decay_sums), (spread_decays, decays), chunk_products, finish


def _mixers_kernel(pa_ref, pb_ref, pc_ref, pd_ref, sm_ref, nwa_ref, nwb_ref, nwc_ref, nwd_ref,
                   ms_ref, mk_ref, tri_ref, mexp_ref, msel_ref, alog_ref, dx_ref, dsel_ref, dexp_ref,
                   ya_ref, yb_ref, yc_ref, yd_ref, sta_s, stc_s, cb_s, mb_s, std_s, *, tile):
    @pl.when(pl.program_id(0) == 0)
    def _():
        for r in (sta_s, stc_s, cb_s, mb_s, std_s):
            r[...] = jnp.zeros_like(r)

    gla = _gla_stages(ms_ref, mk_ref)
    mlstm = _mlstm_stages(nwb_ref, tri_ref, mexp_ref, msel_ref)
    ssd = _ssd_stages(alog_ref, dx_ref, nwd_ref, tri_ref, dsel_ref, dexp_ref)
    work = []
    for s in range(pa_ref.shape[0]):
        work.append((_gla_stream(pa_ref, s, GLA_QK, GLA_DK, ya_ref, sta_s, nwa_ref), gla))
        work.append((dict(p=pb_ref.at[s], g=sm_ref.at[s], c=cb_s.at[s], m=mb_s.at[s], y=yb_ref.at[s]),
                     mlstm))
        work.append((_gla_stream(pc_ref, s, HGRN_QF, HGRN_DK, yc_ref, stc_s, nwc_ref), gla))
        work.append((dict(p=pd_ref.at[s], dt=sm_ref.at[s], st=std_s.at[s], y=yd_ref.at[s]), ssd))
    _run_chunks(work, tile)


def _mixers(tok_inputs, const_inputs, n_batch, seq, tile):
    tok = lambda t: (0, t, 0)
    in_specs = [pl.BlockSpec((n_batch, tile, a.shape[2]), tok) for a in tok_inputs]
    for a in const_inputs:
        in_specs.append(pl.BlockSpec(a.shape, lambda t, nd=a.ndim: (0,) * nd))
    scratch = [pltpu.VMEM((n_batch, GLA_QK // LANES, GLA_DV, LANES), F32),
               pltpu.VMEM((n_batch, HGRN_QF // LANES, HGRN_DV, LANES), F32),
               pltpu.VMEM((n_batch, MLSTM_HEADS, MLSTM_DH, 2 * MLSTM_DH), F32),
               pltpu.VMEM((n_batch, SUBLANES, MLSTM_HEADS * LANES), F32),
               pltpu.VMEM((n_batch, SSD_GROUPS, SSD_STATE, D_BRANCH // SSD_GROUPS), F32)]
    return pl.pallas_call(
        functools.partial(_mixers_kernel, tile=tile), grid=(seq // tile,), in_specs=in_specs,
        out_specs=[pl.BlockSpec((n_batch, tile, D_BRANCH), tok)] * 4,
        out_shape=[jax.ShapeDtypeStruct((n_batch, seq, D_BRANCH), BF16)] * 4,
        scratch_shapes=scratch,
        compiler_params=pltpu.CompilerParams(dimension_semantics=("arbitrary",),
                                             vmem_limit_bytes=VMEM_LIMIT),
        name="mixers",
    )(*tok_inputs, *const_inputs)


def _pad_lanes(parts, total=LANES):
    width = sum(p.shape[-1] for p in parts)
    lead = parts[0].shape[:-1]
    return jnp.concatenate(list(parts) + [jnp.zeros(lead + (total - width,), parts[0].dtype)], axis=-1)


def _small_vector(i_part, f_part, dt_part):
    z = jnp.zeros((GLA_GATE_RANK,), F32)
    return _pad_lanes([z, i_part.astype(F32), f_part.astype(F32), dt_part.astype(F32)])[None, :]


def kernel(x, norm_w, w_in, gla_gate_w, gla_gate_b, gla_norm_w, ml_conv_w, ml_conv_b, ml_i_b, ml_f_b,
           ml_norm_w, hg_lb_logits, hg_norm_w, ssd_conv_w, ssd_conv_b, ssd_dt_bias, ssd_A_log, ssd_D,
           ssd_norm_w, w_out, final_norm_w):
    n_batch, seq, _ = x.shape
    depth = w_in.shape[0]
    tile = min(MIX_TILE, seq)
    n_tok = n_batch * seq

    mstack, masks = _gla_constants()
    tri_c = _tri2(CHUNK)
    ml_exp, ml_sel = _mlstm_constants()
    ssd_exp, ssd_sel = _ssd_constants()
    zero4 = jnp.zeros((MLSTM_HEADS,), F32)
    in_tile = min(IN_TILE, seq)

    p = jax.nn.softmax(hg_lb_logits.astype(F32), axis=0)
    lower_bounds = jnp.cumsum(p, axis=0) - p[0:1]

    w_wide, w_small = _realign(jnp.swapaxes(w_in, 1, 2), REALIGN_ROWS)

    h = x.reshape(n_tok, D_MODEL)
    row2 = lambda v: v.astype(F32).reshape(1, -1)
    for l in range(depth):
        gate_w = jnp.concatenate(
            [gla_gate_w[l].astype(F32), jnp.zeros((LANES - GLA_GATE_RANK, GLA_QK), F32)], axis=0)
        consts = [gate_w, row2(gla_gate_b[l]), row2(lower_bounds[l]),
                  ml_conv_w[l].astype(F32), row2(ml_conv_b[l]),
                  ssd_conv_w[l].astype(F32), row2(ssd_conv_b[l]),
                  _small_vector(ml_i_b[l], ml_f_b[l], ssd_dt_bias[l])]
        pa, pb, pc, pd, sm = [p.reshape(n_batch, seq, -1)
                              for p in _inproj(h, row2(norm_w[l]), w_wide, w_small, l, consts, in_tile,
                                               seq // in_tile)]

        ys = _mixers(
            [pa, pb, pc, pd, sm],
            [row2(gla_norm_w[l]), row2(ml_norm_w[l]), row2(hg_norm_w[l]), row2(ssd_norm_w[l]),
             mstack, masks, tri_c, ml_exp, ml_sel, _small_vector(zero4, zero4, ssd_A_log[l]),
             row2(jnp.repeat(ssd_D[l].astype(F32), SSD_HEAD_DIM)), ssd_sel, ssd_exp],
            n_batch, seq, tile)
        ys = [y.reshape(n_tok, D_BRANCH) for y in ys]
        h = _outproj(h, ys, w_out[l].astype(BF16), row2(final_norm_w), l == depth - 1,
                     min(OUT_TILE, n_tok))
    return h.reshape(n_batch, seq, D_MODEL)
```
